```python
import math
import jax, jax.numpy as jnp
from jax import lax
import numpy as np

D_MODEL = 1024
BATCH = 8
SEQ = 2048
DEPTH = 4

CHUNK = 64
N_META = 16
N_A_LAYERS = DEPTH // 2
N_B_LAYERS = DEPTH - N_A_LAYERS
D_RNN = 3 * D_MODEL // 2
N_LRU_BLOCKS = 16
LRU_BLOCK = D_RNN // N_LRU_BLOCKS
LRU_C = 8.0
CONV_A_WIDTH = 4
N_FOX_HEADS = 16
FOX_HEAD_DIM = D_MODEL // N_FOX_HEADS
Q_BLOCK = 128
D_FF = ((8 * D_MODEL // 3 + 255) // 256) * 256
CONV_F_WIDTH = 3
DN_ALPHA = (2 * DEPTH) ** 0.25
DN_BETA = (8 * DEPTH) ** -0.25
LN_EPS = 1e-5

kernel_name = "yoco_rglru_fox_convffn_deepnorm"


def layer_norm(x, g, b):
    xf = x.astype(jnp.float32)
    mu = jnp.mean(xf, axis=-1, keepdims=True)
    var = jnp.mean(jnp.square(xf - mu), axis=-1, keepdims=True)
    y = (xf - mu) * lax.rsqrt(var + LN_EPS)
    return (y * g.astype(jnp.float32) + b.astype(jnp.float32)).astype(x.dtype)


def causal_dwconv(x, w, b):
    width = w.shape[0]
    length = x.shape[1]
    xp = jnp.pad(x, ((0, 0), (width - 1, 0), (0, 0)))
    y = b
    for k in range(width):
        y = y + xp[:, k:k + length] * w[k]
    return y


def rg_lru(x, w_r, b_r, w_i, b_i, lam):
    bsz, length, width = x.shape
    xb = x.reshape(bsz, length, N_LRU_BLOCKS, LRU_BLOCK)
    r = jax.nn.sigmoid(jnp.einsum('blnc,ncd->blnd', xb, w_r).reshape(bsz, length, width) + b_r)
    i = jax.nn.sigmoid(jnp.einsum('blnc,ncd->blnd', xb, w_i).reshape(bsz, length, width) + b_i)
    log_a = -LRU_C * r.astype(jnp.float32) * jax.nn.softplus(-lam.astype(jnp.float32))
    a = jnp.exp(log_a)
    u = jnp.sqrt(-jnp.expm1(2.0 * log_a)) * (i * x).astype(jnp.float32)

    def combine(left, right):
        a1, b1 = left
        a2, b2 = right
        return a1 * a2, a2 * b1 + b2

    _, h = lax.associative_scan(combine, (a, u), axis=1)
    return h.astype(x.dtype)


def recurrent_mixer(x, w_in, conv_w, conv_b, w_r, b_r, w_i, b_i, lam, w_out):
    gr = x @ w_in
    gate, rec = gr[..., :D_RNN], gr[..., D_RNN:]
    rec = causal_dwconv(rec, conv_w, conv_b)
    h = rg_lru(rec, w_r, b_r, w_i, b_i, lam)
    return (jax.nn.gelu(gate) * h) @ w_out


def conv_ffn(x, w_in, conv_w, conv_b, w_out):
    h = causal_dwconv(x @ w_in, conv_w, conv_b)
    gate, val = h[..., :D_FF], h[..., D_FF:]
    return (jax.nn.gelu(gate) * val) @ w_out


def to_heads_padded(t, lp):
    bsz, length, _ = t.shape
    t = t.reshape(bsz, length, N_FOX_HEADS, FOX_HEAD_DIM).transpose(0, 2, 1, 3)
    return jnp.pad(t, ((0, 0), (0, 0), (0, lp - length), (0, 0)))


def shared_kv(x, kv_w, f_b):
    length = x.shape[1]
    lp = -(-length // Q_BLOCK) * Q_BLOCK
    z = x @ kv_w
    k = to_heads_padded(z[..., :D_MODEL], lp)
    v = to_heads_padded(z[..., D_MODEL:2 * D_MODEL], lp)
    log_f = jax.nn.log_sigmoid(z[..., 2 * D_MODEL:].astype(jnp.float32) + f_b.astype(jnp.float32))
    c = jnp.cumsum(log_f, axis=1).transpose(0, 2, 1)
    c = jnp.pad(c, ((0, 0), (0, 0), (0, lp - length)), mode='edge')
    return k, v, c


def forgetting_attention(q, k, v, c):
    lp = q.shape[2]
    scale = q.shape[-1] ** -0.5
    outs = []
    for q0 in range(0, lp, Q_BLOCK):
        end = q0 + Q_BLOCK
        s = jnp.einsum('bhqd,bhkd->bhqk', q[:, :, q0:end], k[:, :, :end]).astype(jnp.float32) * scale
        s = s + c[:, :, q0:end, None] - c[:, :, None, :end]
        mask = jnp.arange(end)[None, :] <= jnp.arange(q0, end)[:, None]
        s = jnp.where(mask, s, -jnp.inf)
        p = jax.nn.softmax(s, axis=-1).astype(v.dtype)
        outs.append(jnp.einsum('bhqk,bhkd->bhqd', p, v[:, :, :end]))
    return jnp.concatenate(outs, axis=2)


def fox_mixer(x, w_in, w_out, k, v, c):
    bsz, length, _ = x.shape
    qg = x @ w_in
    q = to_heads_padded(qg[..., :D_MODEL], k.shape[2])
    o = forgetting_attention(q, k, v, c)[:, :, :length]
    o = o.transpose(0, 2, 1, 3).reshape(bsz, length, D_MODEL)
    return (o * jax.nn.sigmoid(qg[..., D_MODEL:])) @ w_out


def _fwd_setup_inputs(seed: int = 0) -> dict:
    key = jax.random.key(seed)
    ks = jax.random.split(key, 24)
    f32 = jnp.float32
    d = D_MODEL

    def nrm(k, shape, scale):
        return jax.random.normal(k, shape, f32) * scale

    u = jax.random.uniform(ks[9], (N_A_LAYERS, D_RNN), f32, 0.9, 0.999)
    a0 = u ** (1.0 / LRU_C)
    lam = jnp.log(a0) - jnp.log1p(-a0)

    kv_w = jnp.concatenate([
        nrm(ks[11], (d, d), d ** -0.5),
        nrm(ks[12], (d, d), d ** -0.5 * DN_BETA),
        nrm(ks[13], (d, N_FOX_HEADS), d ** -0.5),
    ], axis=1)

    return {
        "x": nrm(ks[0], (BATCH, SEQ, d), 1.0),
        "meta": nrm(ks[1], (N_META, d), 1.0),
        "a_w_in": nrm(ks[2], (N_A_LAYERS, d, 2 * D_RNN), d ** -0.5),
        "a_conv_w": nrm(ks[3], (N_A_LAYERS, CONV_A_WIDTH, D_RNN), CONV_A_WIDTH ** -0.5),
        "a_conv_b": nrm(ks[4], (N_A_LAYERS, D_RNN), 0.02),
        "a_w_r": nrm(ks[5], (N_A_LAYERS, N_LRU_BLOCKS, LRU_BLOCK, LRU_BLOCK), LRU_BLOCK ** -0.5),
        "a_b_r": nrm(ks[6], (N_A_LAYERS, D_RNN), 0.02),
        "a_w_i": nrm(ks[7], (N_A_LAYERS, N_LRU_BLOCKS, LRU_BLOCK, LRU_BLOCK), LRU_BLOCK ** -0.5),
        "a_b_i": nrm(ks[8], (N_A_LAYERS, D_RNN), 0.02),
        "a_lambda": lam,
        "a_w_out": nrm(ks[10], (N_A_LAYERS, D_RNN, d), D_RNN ** -0.5 * DN_BETA),
        "kv_w": kv_w,
        "kv_f_b": jax.random.uniform(ks[14], (N_FOX_HEADS,), f32, 1.0, 4.0),
        "b_w_in": nrm(ks[15], (N_B_LAYERS, d, 2 * d), d ** -0.5),
        "b_w_out": nrm(ks[16], (N_B_LAYERS, d, d), d ** -0.5 * DN_BETA),
        "f_w_in": nrm(ks[17], (DEPTH, d, 2 * D_FF), d ** -0.5),
        "f_conv_w": nrm(ks[18], (DEPTH, CONV_F_WIDTH, 2 * D_FF), CONV_F_WIDTH ** -0.5),
        "f_conv_b": nrm(ks[19], (DEPTH, 2 * D_FF), 0.02),
        "f_w_out": nrm(ks[20], (DEPTH, D_FF, d), D_FF ** -0.5 * DN_BETA),
        "ln1_g": 1.0 + nrm(ks[21], (DEPTH, d), 0.02),
        "ln1_b": nrm(ks[22], (DEPTH, d), 0.02),
        "ln2_g": 1.0 + nrm(ks[23], (DEPTH, d), 0.02),
        "ln2_b": nrm(jax.random.fold_in(key, 99), (DEPTH, d), 0.02),
    }


def _fwd_reference(x, meta, a_w_in, a_conv_w, a_conv_b, a_w_r, a_b_r, a_w_i, a_b_i, a_lambda, a_w_out,
              kv_w, kv_f_b, b_w_in, b_w_out, f_w_in, f_conv_w, f_conv_b, f_w_out,
              ln1_g, ln1_b, ln2_g, ln2_b):
    bsz = x.shape[0]
    h = jnp.concatenate([jnp.broadcast_to(meta.astype(x.dtype), (bsz, N_META, D_MODEL)), x], axis=1)
    k = v = c = None
    for layer in range(DEPTH):
        if layer < N_A_LAYERS:
            mix = recurrent_mixer(h, a_w_in[layer], a_conv_w[layer], a_conv_b[layer],
                                  a_w_r[layer], a_b_r[layer], a_w_i[layer], a_b_i[layer],
                                  a_lambda[layer], a_w_out[layer])
        else:
            if layer == N_A_LAYERS:
                k, v, c = shared_kv(h, kv_w, kv_f_b)
            j = layer - N_A_LAYERS
            mix = fox_mixer(h, b_w_in[j], b_w_out[j], k, v, c)
        h = layer_norm(DN_ALPHA * h + mix, ln1_g[layer], ln1_b[layer])
        ffn = conv_ffn(h, f_w_in[layer], f_conv_w[layer], f_conv_b[layer], f_w_out[layer])
        h = layer_norm(DN_ALPHA * h + ffn, ln2_g[layer], ln2_b[layer])
    return h[:, N_META:]


import jax as _jax
import jax.numpy as _jnp

TWIN_FORMAT = 'train_step'
FWD_PARAMS = ['x', 'meta', 'a_w_in', 'a_conv_w', 'a_conv_b', 'a_w_r', 'a_b_r', 'a_w_i', 'a_b_i', 'a_lambda', 'a_w_out', 'kv_w', 'kv_f_b', 'b_w_in', 'b_w_out', 'f_w_in', 'f_conv_w', 'f_conv_b', 'f_w_out', 'ln1_g', 'ln1_b', 'ln2_g', 'ln2_b']
TWIN_WEIGHTS = ['meta', 'a_w_in', 'a_conv_w', 'a_conv_b', 'a_w_r', 'a_b_r', 'a_w_i', 'a_b_i', 'a_lambda', 'a_w_out', 'kv_w', 'kv_f_b', 'b_w_in', 'b_w_out', 'f_w_in', 'f_conv_w', 'f_conv_b', 'f_w_out', 'ln1_g', 'ln1_b', 'ln2_g', 'ln2_b']
TWIN_DIFF_INPUT = 'x'
TWIN_INPUTS = ['x', 'meta', 'a_w_in', 'a_conv_w', 'a_conv_b', 'a_w_r', 'a_b_r', 'a_w_i', 'a_b_i', 'a_lambda', 'a_w_out', 'kv_w', 'kv_f_b', 'b_w_in', 'b_w_out', 'f_w_in', 'f_conv_w', 'f_conv_b', 'f_w_out', 'ln1_g', 'ln1_b', 'ln2_g', 'ln2_b', 'loss_target', 'm_meta', 'm_a_w_in', 'm_a_conv_w', 'm_a_conv_b', 'm_a_w_r', 'm_a_b_r', 'm_a_w_i', 'm_a_b_i', 'm_a_lambda', 'm_a_w_out', 'm_kv_w', 'm_kv_f_b', 'm_b_w_in', 'm_b_w_out', 'm_f_w_in', 'm_f_conv_w', 'm_f_conv_b', 'm_f_w_out', 'm_ln1_g', 'm_ln1_b', 'm_ln2_g', 'm_ln2_b', 'v_meta', 'v_a_w_in', 'v_a_conv_w', 'v_a_conv_b', 'v_a_w_r', 'v_a_b_r', 'v_a_w_i', 'v_a_b_i', 'v_a_lambda', 'v_a_w_out', 'v_kv_w', 'v_kv_f_b', 'v_b_w_in', 'v_b_w_out', 'v_f_w_in', 'v_f_conv_w', 'v_f_conv_b', 'v_f_w_out', 'v_ln1_g', 'v_ln1_b', 'v_ln2_g', 'v_ln2_b']
TWIN_OUTPUTS = ['loss', 'grad_x', 'grad_meta', 'grad_a_w_in', 'grad_a_conv_w', 'grad_a_conv_b', 'grad_a_w_r', 'grad_a_b_r', 'grad_a_w_i', 'grad_a_b_i', 'grad_a_lambda', 'grad_a_w_out', 'grad_kv_w', 'grad_kv_f_b', 'grad_b_w_in', 'grad_b_w_out', 'grad_f_w_in', 'grad_f_conv_w', 'grad_f_conv_b', 'grad_f_w_out', 'grad_ln1_g', 'grad_ln1_b', 'grad_ln2_g', 'grad_ln2_b', 'delta_meta', 'delta_a_w_in', 'delta_a_conv_w', 'delta_a_conv_b', 'delta_a_w_r', 'delta_a_b_r', 'delta_a_w_i', 'delta_a_b_i', 'delta_a_lambda', 'delta_a_w_out', 'delta_kv_w', 'delta_kv_f_b', 'delta_b_w_in', 'delta_b_w_out', 'delta_f_w_in', 'delta_f_conv_w', 'delta_f_conv_b', 'delta_f_w_out', 'delta_ln1_g', 'delta_ln1_b', 'delta_ln2_g', 'delta_ln2_b', 'new_m_meta', 'new_m_a_w_in', 'new_m_a_conv_w', 'new_m_a_conv_b', 'new_m_a_w_r', 'new_m_a_b_r', 'new_m_a_w_i', 'new_m_a_b_i', 'new_m_a_lambda', 'new_m_a_w_out', 'new_m_kv_w', 'new_m_kv_f_b', 'new_m_b_w_in', 'new_m_b_w_out', 'new_m_f_w_in', 'new_m_f_conv_w', 'new_m_f_conv_b', 'new_m_f_w_out', 'new_m_ln1_g', 'new_m_ln1_b', 'new_m_ln2_g', 'new_m_ln2_b', 'new_v_meta', 'new_v_a_w_in', 'new_v_a_conv_w', 'new_v_a_conv_b', 'new_v_a_w_r', 'new_v_a_b_r', 'new_v_a_w_i', 'new_v_a_b_i', 'new_v_a_lambda', 'new_v_a_w_out', 'new_v_kv_w', 'new_v_kv_f_b', 'new_v_b_w_in', 'new_v_b_w_out', 'new_v_f_w_in', 'new_v_f_conv_w', 'new_v_f_conv_b', 'new_v_f_w_out', 'new_v_ln1_g', 'new_v_ln1_b', 'new_v_ln2_g', 'new_v_ln2_b']
TWIN_LEAF_KINDS = {'loss': 'loss', 'grad_x': 'grad_x', 'grad_meta': 'grad_w', 'grad_a_w_in': 'grad_w', 'grad_a_conv_w': 'grad_w', 'grad_a_conv_b': 'grad_w', 'grad_a_w_r': 'grad_w', 'grad_a_b_r': 'grad_w', 'grad_a_w_i': 'grad_w', 'grad_a_b_i': 'grad_w', 'grad_a_lambda': 'grad_w', 'grad_a_w_out': 'grad_w', 'grad_kv_w': 'grad_w', 'grad_kv_f_b': 'grad_w', 'grad_b_w_in': 'grad_w', 'grad_b_w_out': 'grad_w', 'grad_f_w_in': 'grad_w', 'grad_f_conv_w': 'grad_w', 'grad_f_conv_b': 'grad_w', 'grad_f_w_out': 'grad_w', 'grad_ln1_g': 'grad_w', 'grad_ln1_b': 'grad_w', 'grad_ln2_g': 'grad_w', 'grad_ln2_b': 'grad_w', 'delta_meta': 'delta_w', 'delta_a_w_in': 'delta_w', 'delta_a_conv_w': 'delta_w', 'delta_a_conv_b': 'delta_w', 'delta_a_w_r': 'delta_w', 'delta_a_b_r': 'delta_w', 'delta_a_w_i': 'delta_w', 'delta_a_b_i': 'delta_w', 'delta_a_lambda': 'delta_w', 'delta_a_w_out': 'delta_w', 'delta_kv_w': 'delta_w', 'delta_kv_f_b': 'delta_w', 'delta_b_w_in': 'delta_w', 'delta_b_w_out': 'delta_w', 'delta_f_w_in': 'delta_w', 'delta_f_conv_w': 'delta_w', 'delta_f_conv_b': 'delta_w', 'delta_f_w_out': 'delta_w', 'delta_ln1_g': 'delta_w', 'delta_ln1_b': 'delta_w', 'delta_ln2_g': 'delta_w', 'delta_ln2_b': 'delta_w', 'new_m_meta': 'new_m', 'new_m_a_w_in': 'new_m', 'new_m_a_conv_w': 'new_m', 'new_m_a_conv_b': 'new_m', 'new_m_a_w_r': 'new_m', 'new_m_a_b_r': 'new_m', 'new_m_a_w_i': 'new_m', 'new_m_a_b_i': 'new_m', 'new_m_a_lambda': 'new_m', 'new_m_a_w_out': 'new_m', 'new_m_kv_w': 'new_m', 'new_m_kv_f_b': 'new_m', 'new_m_b_w_in': 'new_m', 'new_m_b_w_out': 'new_m', 'new_m_f_w_in': 'new_m', 'new_m_f_conv_w': 'new_m', 'new_m_f_conv_b': 'new_m', 'new_m_f_w_out': 'new_m', 'new_m_ln1_g': 'new_m', 'new_m_ln1_b': 'new_m', 'new_m_ln2_g': 'new_m', 'new_m_ln2_b': 'new_m', 'new_v_meta': 'new_v', 'new_v_a_w_in': 'new_v', 'new_v_a_conv_w': 'new_v', 'new_v_a_conv_b': 'new_v', 'new_v_a_w_r': 'new_v', 'new_v_a_b_r': 'new_v', 'new_v_a_w_i': 'new_v', 'new_v_a_b_i': 'new_v', 'new_v_a_lambda': 'new_v', 'new_v_a_w_out': 'new_v', 'new_v_kv_w': 'new_v', 'new_v_kv_f_b': 'new_v', 'new_v_b_w_in': 'new_v', 'new_v_b_w_out': 'new_v', 'new_v_f_w_in': 'new_v', 'new_v_f_conv_w': 'new_v', 'new_v_f_conv_b': 'new_v', 'new_v_f_w_out': 'new_v', 'new_v_ln1_g': 'new_v', 'new_v_ln1_b': 'new_v', 'new_v_ln2_g': 'new_v', 'new_v_ln2_b': 'new_v'}


def _forward(args):
    return _fwd_reference(*[args[k] for k in FWD_PARAMS])


def _output_shape():
    out = _jax.eval_shape(lambda: _forward(_fwd_setup_inputs(0)))
    return out.shape, out.dtype

N_MICROBATCH = 1
ADAM_LR = 0.001
ADAM_B1 = 0.9
ADAM_B2 = 0.999
ADAM_EPS = 1e-08
ADAM_WD = 0.01
ADAM_STEP = 10
PER_EXAMPLE_BATCH_AXIS = {'x': 0, 'loss_target': 0}
SHARED_INPUTS = []
_WEIGHT_DTYPES = {'meta': _jnp.float32, 'a_w_in': _jnp.float32, 'a_conv_w': _jnp.float32, 'a_conv_b': _jnp.float32, 'a_w_r': _jnp.float32, 'a_b_r': _jnp.float32, 'a_w_i': _jnp.float32, 'a_b_i': _jnp.float32, 'a_lambda': _jnp.float32, 'a_w_out': _jnp.float32, 'kv_w': _jnp.float32, 'kv_f_b': _jnp.float32, 'b_w_in': _jnp.float32, 'b_w_out': _jnp.float32, 'f_w_in': _jnp.float32, 'f_conv_w': _jnp.float32, 'f_conv_b': _jnp.float32, 'f_w_out': _jnp.float32, 'ln1_g': _jnp.float32, 'ln1_b': _jnp.float32, 'ln2_g': _jnp.float32, 'ln2_b': _jnp.float32}
MOMENT_SCALE = {'meta': 1.412515e-03, 'a_w_in': 1.219920e-02, 'a_conv_w': 1.286375e-02, 'a_conv_b': 1.466318e-01, 'a_w_r': 4.403261e-03, 'a_b_r': 3.378074e-03, 'a_w_i': 7.968856e-03, 'a_b_i': 4.300197e-03, 'a_lambda': 6.750673e-03, 'a_w_out': 3.949393e-02, 'kv_w': 7.231288e-03, 'kv_f_b': 2.823346e-02, 'b_w_in': 1.707286e-03, 'b_w_out': 6.947329e-03, 'f_w_in': 1.276464e-02, 'f_conv_w': 1.277196e-02, 'f_conv_b': 1.415621e-02, 'f_w_out': 4.964418e-02, 'ln1_g': 5.507307e-01, 'ln1_b': 2.679869e-01, 'ln2_g': 8.048218e+00, 'ln2_b': 5.472639e-01}


def _to_microbatches(a, axis):
    t = _jnp.moveaxis(a, axis, 0)
    t = t.reshape((N_MICROBATCH, t.shape[0] // N_MICROBATCH) + t.shape[1:])
    return _jnp.moveaxis(t, 1, axis + 1)


def setup_inputs(seed: int = 0) -> dict:
    inp = _fwd_setup_inputs(seed)
    key = _jax.random.fold_in(_jax.random.key(seed), 7919)
    shape, _ = _output_shape()
    out = dict(inp)
    out["loss_target"] = _jax.random.normal(_jax.random.fold_in(key, 0), shape, _jnp.float32)
    for i, name in enumerate(TWIN_WEIGHTS):
        w = inp[name].astype(_jnp.float32)
        if MOMENT_SCALE is None:
            s = _jnp.sqrt(_jnp.mean(_jnp.square(w)) + 1e-30)
        else:
            s = MOMENT_SCALE[name]
        km, kv = _jax.random.split(_jax.random.fold_in(key, i + 1))
        out[name] = w
        out["m_" + name] = s * _jax.random.normal(km, w.shape, _jnp.float32)
        out["v_" + name] = (s * s) * _jax.random.uniform(kv, w.shape, _jnp.float32, 0.5, 1.5)
    if N_MICROBATCH > 1:
        for name, axis in PER_EXAMPLE_BATCH_AXIS.items():
            out[name] = _to_microbatches(out[name], axis)
    return {'x': out['x'], 'meta': out['meta'], 'a_w_in': out['a_w_in'], 'a_conv_w': out['a_conv_w'], 'a_conv_b': out['a_conv_b'], 'a_w_r': out['a_w_r'], 'a_b_r': out['a_b_r'], 'a_w_i': out['a_w_i'], 'a_b_i': out['a_b_i'], 'a_lambda': out['a_lambda'], 'a_w_out': out['a_w_out'], 'kv_w': out['kv_w'], 'kv_f_b': out['kv_f_b'], 'b_w_in': out['b_w_in'], 'b_w_out': out['b_w_out'], 'f_w_in': out['f_w_in'], 'f_conv_w': out['f_conv_w'], 'f_conv_b': out['f_conv_b'], 'f_w_out': out['f_w_out'], 'ln1_g': out['ln1_g'], 'ln1_b': out['ln1_b'], 'ln2_g': out['ln2_g'], 'ln2_b': out['ln2_b'], 'loss_target': out['loss_target'], 'm_meta': out['m_meta'], 'm_a_w_in': out['m_a_w_in'], 'm_a_conv_w': out['m_a_conv_w'], 'm_a_conv_b': out['m_a_conv_b'], 'm_a_w_r': out['m_a_w_r'], 'm_a_b_r': out['m_a_b_r'], 'm_a_w_i': out['m_a_w_i'], 'm_a_b_i': out['m_a_b_i'], 'm_a_lambda': out['m_a_lambda'], 'm_a_w_out': out['m_a_w_out'], 'm_kv_w': out['m_kv_w'], 'm_kv_f_b': out['m_kv_f_b'], 'm_b_w_in': out['m_b_w_in'], 'm_b_w_out': out['m_b_w_out'], 'm_f_w_in': out['m_f_w_in'], 'm_f_conv_w': out['m_f_conv_w'], 'm_f_conv_b': out['m_f_conv_b'], 'm_f_w_out': out['m_f_w_out'], 'm_ln1_g': out['m_ln1_g'], 'm_ln1_b': out['m_ln1_b'], 'm_ln2_g': out['m_ln2_g'], 'm_ln2_b': out['m_ln2_b'], 'v_meta': out['v_meta'], 'v_a_w_in': out['v_a_w_in'], 'v_a_conv_w': out['v_a_conv_w'], 'v_a_conv_b': out['v_a_conv_b'], 'v_a_w_r': out['v_a_w_r'], 'v_a_b_r': out['v_a_b_r'], 'v_a_w_i': out['v_a_w_i'], 'v_a_b_i': out['v_a_b_i'], 'v_a_lambda': out['v_a_lambda'], 'v_a_w_out': out['v_a_w_out'], 'v_kv_w': out['v_kv_w'], 'v_kv_f_b': out['v_kv_f_b'], 'v_b_w_in': out['v_b_w_in'], 'v_b_w_out': out['v_b_w_out'], 'v_f_w_in': out['v_f_w_in'], 'v_f_conv_w': out['v_f_conv_w'], 'v_f_conv_b': out['v_f_conv_b'], 'v_f_w_out': out['v_f_w_out'], 'v_ln1_g': out['v_ln1_g'], 'v_ln1_b': out['v_ln1_b'], 'v_ln2_g': out['v_ln2_g'], 'v_ln2_b': out['v_ln2_b']}


def _loss(weights, diff, rest, loss_target):
    with _jax.named_scope("forward"):
        args = {**rest, TWIN_DIFF_INPUT: diff, **{k: w.astype(_WEIGHT_DTYPES[k]) for k, w in weights.items()}}
        y = _forward(args)
    with _jax.named_scope("loss_head"):
        err = _jnp.square(y.astype(_jnp.float32) - loss_target)
        return 0.5 * _jnp.sum(_jnp.mean(err, axis=-1)) if err.ndim else 0.5 * err


def _adamw(w, g, m, v):
    m = ADAM_B1 * m + (1.0 - ADAM_B1) * g
    v = ADAM_B2 * v + (1.0 - ADAM_B2) * _jnp.square(g)
    m_hat = m / (1.0 - ADAM_B1 ** ADAM_STEP)
    v_hat = v / (1.0 - ADAM_B2 ** ADAM_STEP)
    delta = -ADAM_LR * (m_hat / (_jnp.sqrt(v_hat) + ADAM_EPS) + ADAM_WD * w)
    return delta, m, v


def reference(x, meta, a_w_in, a_conv_w, a_conv_b, a_w_r, a_b_r, a_w_i, a_b_i, a_lambda, a_w_out, kv_w, kv_f_b, b_w_in, b_w_out, f_w_in, f_conv_w, f_conv_b, f_w_out, ln1_g, ln1_b, ln2_g, ln2_b, loss_target, m_meta, m_a_w_in, m_a_conv_w, m_a_conv_b, m_a_w_r, m_a_b_r, m_a_w_i, m_a_b_i, m_a_lambda, m_a_w_out, m_kv_w, m_kv_f_b, m_b_w_in, m_b_w_out, m_f_w_in, m_f_conv_w, m_f_conv_b, m_f_w_out, m_ln1_g, m_ln1_b, m_ln2_g, m_ln2_b, v_meta, v_a_w_in, v_a_conv_w, v_a_conv_b, v_a_w_r, v_a_b_r, v_a_w_i, v_a_b_i, v_a_lambda, v_a_w_out, v_kv_w, v_kv_f_b, v_b_w_in, v_b_w_out, v_f_w_in, v_f_conv_w, v_f_conv_b, v_f_w_out, v_ln1_g, v_ln1_b, v_ln2_g, v_ln2_b):
    given = dict(x=x, meta=meta, a_w_in=a_w_in, a_conv_w=a_conv_w, a_conv_b=a_conv_b, a_w_r=a_w_r, a_b_r=a_b_r, a_w_i=a_w_i, a_b_i=a_b_i, a_lambda=a_lambda, a_w_out=a_w_out, kv_w=kv_w, kv_f_b=kv_f_b, b_w_in=b_w_in, b_w_out=b_w_out, f_w_in=f_w_in, f_conv_w=f_conv_w, f_conv_b=f_conv_b, f_w_out=f_w_out, ln1_g=ln1_g, ln1_b=ln1_b, ln2_g=ln2_g, ln2_b=ln2_b, loss_target=loss_target, m_meta=m_meta, m_a_w_in=m_a_w_in, m_a_conv_w=m_a_conv_w, m_a_conv_b=m_a_conv_b, m_a_w_r=m_a_w_r, m_a_b_r=m_a_b_r, m_a_w_i=m_a_w_i, m_a_b_i=m_a_b_i, m_a_lambda=m_a_lambda, m_a_w_out=m_a_w_out, m_kv_w=m_kv_w, m_kv_f_b=m_kv_f_b, m_b_w_in=m_b_w_in, m_b_w_out=m_b_w_out, m_f_w_in=m_f_w_in, m_f_conv_w=m_f_conv_w, m_f_conv_b=m_f_conv_b, m_f_w_out=m_f_w_out, m_ln1_g=m_ln1_g, m_ln1_b=m_ln1_b, m_ln2_g=m_ln2_g, m_ln2_b=m_ln2_b, v_meta=v_meta, v_a_w_in=v_a_w_in, v_a_conv_w=v_a_conv_w, v_a_conv_b=v_a_conv_b, v_a_w_r=v_a_w_r, v_a_b_r=v_a_b_r, v_a_w_i=v_a_w_i, v_a_b_i=v_a_b_i, v_a_lambda=v_a_lambda, v_a_w_out=v_a_w_out, v_kv_w=v_kv_w, v_kv_f_b=v_kv_f_b, v_b_w_in=v_b_w_in, v_b_w_out=v_b_w_out, v_f_w_in=v_f_w_in, v_f_conv_w=v_f_conv_w, v_f_conv_b=v_f_conv_b, v_f_w_out=v_f_w_out, v_ln1_g=v_ln1_g, v_ln1_b=v_ln1_b, v_ln2_g=v_ln2_g, v_ln2_b=v_ln2_b)
    weights = {n: given[n] for n in TWIN_WEIGHTS}
    shared = {n: given[n] for n in SHARED_INPUTS}
    per_example = {n: given[n] for n in ['x']}
    grad_fn = _jax.value_and_grad(_loss, argnums=(0, 1))

    def one_microbatch(ex, loss_target):
        ex = dict(ex)
        diff = ex.pop(TWIN_DIFF_INPUT)
        return grad_fn(weights, diff, {**shared, **ex}, loss_target)

    if N_MICROBATCH == 1:
        loss, (grad_w, grad_x) = one_microbatch(per_example, given["loss_target"])
    else:
        def body(carry, xs):
            loss_sum, grad_sum = carry
            l_k, (gw_k, gx_k) = one_microbatch(xs[0], xs[1])
            with _jax.named_scope("update"):
                return (loss_sum + l_k, _jax.tree.map(_jnp.add, grad_sum, gw_k)), gx_k

        init = (_jnp.zeros((), _jnp.float32), _jax.tree.map(_jnp.zeros_like, weights))
        (loss, grad_w), grad_x = _jax.lax.scan(body, init, (per_example, given["loss_target"]))
    with _jax.named_scope("update"):
        delta_w, new_m, new_v = {}, {}, {}
        for n in TWIN_WEIGHTS:
            delta_w[n], new_m[n], new_v[n] = _adamw(weights[n], grad_w[n], given["m_" + n], given["v_" + n])
    return (loss, grad_x, *[grad_w[n] for n in TWIN_WEIGHTS], *[delta_w[n] for n in TWIN_WEIGHTS],
            *[new_m[n] for n in TWIN_WEIGHTS], *[new_v[n] for n in TWIN_WEIGHTS])
```

```python
import functools
import math

import jax
import jax.numpy as jnp
from jax import lax
from jax.experimental import pallas as pl
from jax.experimental.pallas import tpu as pltpu

F32 = jnp.float32
BF16 = jnp.bfloat16

N_DEV = 8
MESH_AXES = ("x", "y", "c")
N_LAYERS = 4
N_A_LAYERS = 2
N_LRU_BLOCKS = 16
N_HEADS = 16
LRU_C = 8.0
DN_ALPHA = (2 * N_LAYERS) ** 0.25
LN_EPS = 1e-5
ADAM_LR, ADAM_B1, ADAM_B2, ADAM_EPS, ADAM_WD, ADAM_STEP = 0.001, 0.9, 0.999, 1e-08, 0.01, 10

LANES = 128
ROW_ALIGN = 128
VMEM_LIMIT_BYTES = 56 * 1024 * 1024
GELU_K = math.sqrt(2.0 / math.pi)
GELU_C = 0.044715


def _params(*sem):
    return pltpu.CompilerParams(dimension_semantics=sem, vmem_limit_bytes=VMEM_LIMIT_BYTES)


def _gelu(x):
    th = jnp.tanh(GELU_K * (x + GELU_C * x * x * x))
    return 0.5 * x * (1.0 + th)


def _gelu_and_grad(x):
    x2 = x * x
    th = jnp.tanh(GELU_K * (x + GELU_C * x2 * x))
    g = 0.5 * x * (1.0 + th)
    dg = 0.5 * (1.0 + th) + 0.5 * x * (1.0 - th * th) * (GELU_K * (1.0 + 3.0 * GELU_C * x2))
    return g, dg


def _sigmoid(x):
    return 1.0 / (1.0 + jnp.exp(-x))


def _expm1(x):
    small = x * (1.0 + 0.5 * x * (1.0 + (1.0 / 3.0) * x * (1.0 + 0.25 * x)))
    return jnp.where(jnp.abs(x) < 1e-2, small, jnp.exp(x) - 1.0)


def _softplus(x):
    e = jnp.exp(-jnp.abs(x))
    small = e * (1.0 - 0.5 * e * (1.0 - (2.0 / 3.0) * e))
    return jnp.maximum(x, 0.0) + jnp.where(e < 1e-2, small, jnp.log(1.0 + e))


def _shift_down(x, s):
    if s == 0:
        return x
    rows = lax.broadcasted_iota(jnp.int32, x.shape, 0)
    return jnp.where(rows >= s, pltpu.roll(x, s, 0), 0.0)


def _shift_up(x, s):
    if s == 0:
        return x
    n = x.shape[0]
    rows = lax.broadcasted_iota(jnp.int32, x.shape, 0)
    return jnp.where(rows < n - s, pltpu.roll(x, n - s, 0), 0.0)


def _dot_nn(a, b):
    return lax.dot_general(a, b, (((1,), (0,)), ((), ())), preferred_element_type=F32)


def _dot_nt(a, b):
    return lax.dot_general(a, b, (((1,), (1,)), ((), ())), preferred_element_type=F32)


def _dot_tn(a, b):
    return lax.dot_general(a, b, (((0,), (0,)), ((), ())), preferred_element_type=F32)


def _rows8(vals, width):
    rows = lax.broadcasted_iota(jnp.int32, (8, width), 0)
    out = jnp.zeros((8, width), F32)
    for k, v in enumerate(vals):
        out = jnp.where(rows == k, jnp.broadcast_to(v, (8, width)), out)
    return out


def _mm_nn(a, b, *, tn, out_dtype, name):
    m, k = a.shape
    n = b.shape[1]

    def body(a_ref, b_ref, o_ref):
        o_ref[...] = _dot_nn(a_ref[...], b_ref[...]).astype(o_ref.dtype)

    return pl.pallas_call(
        body, name=name, grid=(n // tn,),
        in_specs=[pl.BlockSpec((m, k), lambda j: (0, 0)), pl.BlockSpec((k, tn), lambda j: (0, j))],
        out_specs=pl.BlockSpec((m, tn), lambda j: (0, j)),
        out_shape=jax.ShapeDtypeStruct((m, n), out_dtype),
        compiler_params=_params("parallel"),
    )(a, b)


def _mm_nt_n(a, b, *, tn, name):
    m, k = a.shape
    n = b.shape[0]

    def body(a_ref, b_ref, o_ref):
        o_ref[...] = _dot_nt(a_ref[...], b_ref[...])

    return pl.pallas_call(
        body, name=name, grid=(n // tn,),
        in_specs=[pl.BlockSpec((m, k), lambda j: (0, 0)), pl.BlockSpec((tn, k), lambda j: (j, 0))],
        out_specs=pl.BlockSpec((m, tn), lambda j: (0, j)),
        out_shape=jax.ShapeDtypeStruct((m, n), F32),
        compiler_params=_params("parallel"),
    )(a, b)


def _mm_nt_k(a3, b, add, *, tm, name, alpha=DN_ALPHA):
    nh, m, kh = a3.shape
    n = b.shape[0]

    def body(a_ref, b_ref, add_ref, o_ref, acc_ref):
        h = pl.program_id(1)

        @pl.when(h == 0)
        def _():
            acc_ref[...] = alpha * add_ref[...]

        acc_ref[...] += _dot_nt(a_ref[...], b_ref[...])

        @pl.when(h == nh - 1)
        def _():
            o_ref[...] = acc_ref[...]

    return pl.pallas_call(
        body, name=name, grid=(m // tm, nh),
        in_specs=[pl.BlockSpec((None, tm, kh), lambda i, h: (h, i, 0)),
                  pl.BlockSpec((n, kh), lambda i, h: (0, h)),
                  pl.BlockSpec((tm, n), lambda i, h: (i, 0))],
        out_specs=pl.BlockSpec((tm, n), lambda i, h: (i, 0)),
        out_shape=jax.ShapeDtypeStruct((m, n), F32),
        scratch_shapes=[pltpu.VMEM((tm, n), F32)],
        compiler_params=_params("parallel", "arbitrary"),
    )(a3, b, add)


def _mm_tn_in(a, b3, *, tn, name):
    t, m = a.shape
    nh, _, nn = b3.shape
    per = nn // tn

    def body(a_ref, b_ref, o_ref):
        o_ref[...] = _dot_tn(a_ref[...], b_ref[...])

    return pl.pallas_call(
        body, name=name, grid=(nh * per,),
        in_specs=[pl.BlockSpec((t, m), lambda j: (0, 0)),
                  pl.BlockSpec((None, t, tn), lambda j: (j // per, 0, j % per))],
        out_specs=pl.BlockSpec((m, tn), lambda j: (0, j)),
        out_shape=jax.ShapeDtypeStruct((m, nh * nn), F32),
        compiler_params=_params("parallel"),
    )(a, b3)


def _mm_tn_out(a, b, *, tm, name):
    t, m = a.shape
    n = b.shape[1]

    def body(a_ref, b_ref, o_ref):
        o_ref[...] = _dot_tn(a_ref[...], b_ref[...])

    return pl.pallas_call(
        body, name=name, grid=(m // tm,),
        in_specs=[pl.BlockSpec((t, tm), lambda i: (0, i)), pl.BlockSpec((t, n), lambda i: (0, 0))],
        out_specs=pl.BlockSpec((tm, n), lambda i: (i, 0)),
        out_shape=jax.ShapeDtypeStruct((m, n), F32),
        compiler_params=_params("parallel"),
    )(a, b)


def _mm_ln(y, w, hin, g, b, *, tm, name):
    t, k = y.shape
    d = w.shape[1]

    def body(y_ref, w_ref, hin_ref, g_ref, b_ref, s_ref, h_ref, hb_ref):
        s = DN_ALPHA * hin_ref[...] + _dot_nn(y_ref[...], w_ref[...])
        mu = jnp.mean(s, axis=-1, keepdims=True)
        xc = s - mu
        var = jnp.mean(xc * xc, axis=-1, keepdims=True)
        h = xc * lax.rsqrt(var + LN_EPS) * g_ref[...] + b_ref[...]
        s_ref[...] = s
        h_ref[...] = h
        hb_ref[...] = h.astype(BF16)

    row = pl.BlockSpec((tm, d), lambda i: (i, 0))
    vec = pl.BlockSpec((1, d), lambda i: (0, 0))
    return pl.pallas_call(
        body, name=name, grid=(t // tm,),
        in_specs=[pl.BlockSpec((tm, k), lambda i: (i, 0)), pl.BlockSpec((k, d), lambda i: (0, 0)), row, vec, vec],
        out_specs=[row, row, row],
        out_shape=[jax.ShapeDtypeStruct((t, d), F32), jax.ShapeDtypeStruct((t, d), F32),
                   jax.ShapeDtypeStruct((t, d), BF16)],
        compiler_params=_params("parallel"),
    )(y, w, hin, g, b)


def _ln_bwd(dout, s, g, *, tm, name):
    t, d = s.shape

    def body(do_ref, s_ref, g_ref, ds_ref, dsb_ref, gb_ref):
        i = pl.program_id(0)
        sv = s_ref[...]
        do = do_ref[...]
        mu = jnp.mean(sv, axis=-1, keepdims=True)
        xc = sv - mu
        var = jnp.mean(xc * xc, axis=-1, keepdims=True)
        rstd = lax.rsqrt(var + LN_EPS)
        xhat = xc * rstd
        dxhat = do * g_ref[...]
        m1 = jnp.mean(dxhat, axis=-1, keepdims=True)
        m2 = jnp.mean(dxhat * xhat, axis=-1, keepdims=True)
        ds = rstd * (dxhat - m1 - xhat * m2)
        ds_ref[...] = ds
        dsb_ref[...] = ds.astype(BF16)
        upd = _rows8([jnp.sum(do * xhat, axis=0, keepdims=True), jnp.sum(do, axis=0, keepdims=True)], d)

        @pl.when(i == 0)
        def _():
            gb_ref[...] = upd

        @pl.when(i > 0)
        def _():
            gb_ref[...] += upd

    row = pl.BlockSpec((tm, d), lambda i: (i, 0))
    return pl.pallas_call(
        body, name=name, grid=(t // tm,),
        in_specs=[row, row, pl.BlockSpec((1, d), lambda i: (0, 0))],
        out_specs=[row, row, pl.BlockSpec((8, d), lambda i: (0, 0))],
        out_shape=[jax.ShapeDtypeStruct((t, d), F32), jax.ShapeDtypeStruct((t, d), BF16),
                   jax.ShapeDtypeStruct((8, d), F32)],
        compiler_params=_params("arbitrary"),
    )(dout, s, g)


def _conv_taps(x, wb, width):
    y = jnp.broadcast_to(wb[width:width + 1, :], x.shape)
    for k in range(width):
        y = y + _shift_down(x, width - 1 - k) * wb[k:k + 1, :]
    return y


def _conv_taps_bwd(dy, x, wb, width):
    dx = jnp.zeros_like(dy)
    rows = []
    for k in range(width):
        s = width - 1 - k
        dx = dx + _shift_up(dy, s) * wb[k:k + 1, :]
        rows.append(jnp.sum(dy * _shift_down(x, s), axis=0, keepdims=True))
    rows.append(jnp.sum(dy, axis=0, keepdims=True))
    return dx, _rows8(rows, dy.shape[1])


def _convglu_fwd(z, fwb, *, tc, name):
    t, f2 = z.shape
    f = f2 // 2
    nt = f // tc

    def body(zg_ref, zv_ref, wg_ref, wv_ref, y_ref):
        gate = _conv_taps(zg_ref[...], wg_ref[...], 3)
        val = _conv_taps(zv_ref[...], wv_ref[...], 3)
        y_ref[...] = (_gelu(gate) * val).astype(BF16)

    return pl.pallas_call(
        body, name=name, grid=(nt,),
        in_specs=[pl.BlockSpec((t, tc), lambda j: (0, j)), pl.BlockSpec((t, tc), lambda j: (0, j + nt)),
                  pl.BlockSpec((8, tc), lambda j: (0, j)), pl.BlockSpec((8, tc), lambda j: (0, j + nt))],
        out_specs=pl.BlockSpec((t, tc), lambda j: (0, j)),
        out_shape=jax.ShapeDtypeStruct((t, f), BF16),
        compiler_params=_params("parallel"),
    )(z, z, fwb, fwb)


def _ffn_bwd_mid(ds_bf, w_out, z, fwb, *, tc, name):
    t, d = ds_bf.shape
    f = w_out.shape[0]
    nt = f // tc

    def body(ds_ref, w_ref, zg_ref, zv_ref, wg_ref, wv_ref, dz_ref, dwb_ref):
        dyf = _dot_nt(ds_ref[...], w_ref[...])
        zg, zv = zg_ref[...], zv_ref[...]
        wg, wv = wg_ref[...], wv_ref[...]
        gate = _conv_taps(zg, wg, 3)
        val = _conv_taps(zv, wv, 3)
        gl, dgl = _gelu_and_grad(gate)
        dzg, dwg = _conv_taps_bwd(dyf * val * dgl, zg, wg, 3)
        dzv, dwv = _conv_taps_bwd(dyf * gl, zv, wv, 3)
        dz_ref[0] = dzg.astype(BF16)
        dz_ref[1] = dzv.astype(BF16)
        dwb_ref[0] = dwg
        dwb_ref[1] = dwv

    return pl.pallas_call(
        body, name=name, grid=(nt,),
        in_specs=[pl.BlockSpec((t, d), lambda j: (0, 0)), pl.BlockSpec((tc, d), lambda j: (j, 0)),
                  pl.BlockSpec((t, tc), lambda j: (0, j)), pl.BlockSpec((t, tc), lambda j: (0, j + nt)),
                  pl.BlockSpec((8, tc), lambda j: (0, j)), pl.BlockSpec((8, tc), lambda j: (0, j + nt))],
        out_specs=[pl.BlockSpec((2, t, tc), lambda j: (0, 0, j)), pl.BlockSpec((2, 8, tc), lambda j: (0, 0, j))],
        out_shape=[jax.ShapeDtypeStruct((2, t, f), BF16), jax.ShapeDtypeStruct((2, 8, f), F32)],
        compiler_params=_params("parallel"),
    )(ds_bf, w_out, z, z, fwb, fwb)


def _conv_a_fwd(gr, cwb, *, cb, name):
    t, r2 = gr.shape
    r = r2 // 2
    nb = r // cb

    def body(x_ref, w_ref, o_ref):
        o_ref[...] = _conv_taps(x_ref[...], w_ref[...], 4)

    return pl.pallas_call(
        body, name=name, grid=(nb,),
        in_specs=[pl.BlockSpec((t, cb), lambda j: (0, j + nb)), pl.BlockSpec((8, cb), lambda j: (0, j))],
        out_specs=pl.BlockSpec((t, cb), lambda j: (0, j)),
        out_shape=jax.ShapeDtypeStruct((t, r), F32),
        compiler_params=_params("parallel"),
    )(gr, cwb)


def _gates_fwd(rec, bd_r, bd_i, vecs, *, tm, name):
    t, r_dim = rec.shape
    nb, cb, _ = bd_r.shape

    def body(x_ref, wr_ref, wi_ref, v_ref, a_ref, u_ref, r_ref, i_ref):
        x = x_ref[...]
        xb = x.astype(BF16)
        v = v_ref[...]
        r = _sigmoid(_dot_nn(xb, wr_ref[...]) + v[0:1, :])
        i = _sigmoid(_dot_nn(xb, wi_ref[...]) + v[1:2, :])
        log_a = (-LRU_C) * r * _softplus(-v[2:3, :])
        a_ref[...] = jnp.exp(log_a)
        u_ref[...] = jnp.sqrt(-_expm1(2.0 * log_a)) * (i * x)
        r_ref[...] = r
        i_ref[...] = i

    blk = pl.BlockSpec((tm, cb), lambda j, i: (i, j))
    wspec = pl.BlockSpec((None, cb, cb), lambda j, i: (j, 0, 0))
    out = jax.ShapeDtypeStruct((t, r_dim), F32)
    return pl.pallas_call(
        body, name=name, grid=(nb, t // tm),
        in_specs=[blk, wspec, wspec, pl.BlockSpec((8, cb), lambda j, i: (0, j))],
        out_specs=[blk, blk, blk, blk],
        out_shape=[out, out, out, out],
        compiler_params=_params("parallel", "parallel"),
    )(rec, bd_r, bd_i, vecs)


def _scan_fwd(a, u, gr, *, cb, name):
    t, r = a.shape
    nb = r // cb

    def body(a_ref, u_ref, g_ref, h_ref, y_ref):
        def step(k, h):
            h = a_ref[pl.ds(k, 1), :] * h + u_ref[pl.ds(k, 1), :]
            h_ref[pl.ds(k, 1), :] = h
            return h

        lax.fori_loop(0, t, step, jnp.zeros((1, cb), F32), unroll=8)
        y_ref[...] = (_gelu(g_ref[...]) * h_ref[...]).astype(BF16)

    blk = pl.BlockSpec((t, cb), lambda j: (0, j))
    return pl.pallas_call(
        body, name=name, grid=(nb,),
        in_specs=[blk, blk, blk],
        out_specs=[blk, blk],
        out_shape=[jax.ShapeDtypeStruct((t, r), F32), jax.ShapeDtypeStruct((t, r), BF16)],
        compiler_params=_params("parallel"),
    )(a, u, gr)


def _scan_bwd(dy, gr, hr, a, *, cb, name):
    t, r = a.shape
    nb = r // cb

    def body(dy_ref, g_ref, h_ref, a_ref, dh_ref, da_ref, dg_ref, dhr_ref):
        gl, dgl = _gelu_and_grad(g_ref[...])
        dyv = dy_ref[...]
        dhr_ref[...] = dyv * gl
        dg_ref[...] = (dyv * h_ref[...] * dgl).astype(BF16)

        def step(k, carry):
            row = t - 1 - k
            dh = dhr_ref[pl.ds(row, 1), :] + carry
            dh_ref[pl.ds(row, 1), :] = dh
            prev = jnp.maximum(row - 1, 0)
            hprev = h_ref[pl.ds(prev, 1), :] * jnp.where(row > 0, 1.0, 0.0)
            da_ref[pl.ds(row, 1), :] = dh * hprev
            return a_ref[pl.ds(row, 1), :] * dh

        lax.fori_loop(0, t, step, jnp.zeros((1, cb), F32), unroll=8)

    blk = pl.BlockSpec((t, cb), lambda j: (0, j))
    return pl.pallas_call(
        body, name=name, grid=(nb,),
        in_specs=[blk, blk, blk, blk],
        out_specs=[blk, blk, blk],
        out_shape=[jax.ShapeDtypeStruct((t, r), F32), jax.ShapeDtypeStruct((t, r), F32),
                   jax.ShapeDtypeStruct((t, r), BF16)],
        scratch_shapes=[pltpu.VMEM((t, cb), F32)],
        compiler_params=_params("parallel"),
    )(dy, gr, hr, a)


def _gates_bwd(rec, r, i, a, dh, da, bd_r, bd_i, vecs, *, tm, name):
    t, r_dim = rec.shape
    nb, cb, _ = bd_r.shape

    def body(x_ref, r_ref, i_ref, a_ref, dh_ref, da_ref, wr_ref, wi_ref, v_ref, dx_ref, dpr_ref, dpi_ref, dv_ref):
        step = pl.program_id(1)
        x, r, i, a, dh, da = x_ref[...], r_ref[...], i_ref[...], a_ref[...], dh_ref[...], da_ref[...]
        lam = v_ref[...][2:3, :]
        sp = _softplus(-lam)
        a2 = a * a
        mult = jnp.sqrt(-_expm1(2.0 * (-LRU_C) * r * sp))
        d_i = dh * mult * x
        d_log_a = da * a - (dh * i * x) * a2 / mult
        d_r = d_log_a * ((-LRU_C) * sp)
        d_sp = jnp.sum(d_log_a * ((-LRU_C) * r), axis=0, keepdims=True)
        d_pre_r = d_r * r * (1.0 - r)
        d_pre_i = d_i * i * (1.0 - i)
        dprb = d_pre_r.astype(BF16)
        dpib = d_pre_i.astype(BF16)
        dx_ref[...] = dh * mult * i + _dot_nt(dprb, wr_ref[...]) + _dot_nt(dpib, wi_ref[...])
        dpr_ref[...] = dprb
        dpi_ref[...] = dpib
        upd = _rows8([jnp.sum(d_pre_r, axis=0, keepdims=True), jnp.sum(d_pre_i, axis=0, keepdims=True),
                      -d_sp * _sigmoid(-lam)], cb)

        @pl.when(step == 0)
        def _():
            dv_ref[...] = upd

        @pl.when(step > 0)
        def _():
            dv_ref[...] += upd

    blk = pl.BlockSpec((tm, cb), lambda j, i: (i, j))
    wspec = pl.BlockSpec((None, cb, cb), lambda j, i: (j, 0, 0))
    vspec = pl.BlockSpec((8, cb), lambda j, i: (0, j))
    return pl.pallas_call(
        body, name=name, grid=(nb, t // tm),
        in_specs=[blk] * 6 + [wspec, wspec, vspec],
        out_specs=[blk, blk, blk, vspec],
        out_shape=[jax.ShapeDtypeStruct((t, r_dim), F32), jax.ShapeDtypeStruct((t, r_dim), BF16),
                   jax.ShapeDtypeStruct((t, r_dim), BF16), jax.ShapeDtypeStruct((8, r_dim), F32)],
        compiler_params=_params("parallel", "arbitrary"),
    )(rec, r, i, a, dh, da, bd_r, bd_i, vecs)


def _bd_grad(rec, dpr, dpi, *, cb, name):
    t, r = rec.shape
    nb = r // cb

    def body(x_ref, dr_ref, di_ref, gr_ref, gi_ref):
        xb = x_ref[...].astype(BF16)
        gr_ref[...] = _dot_tn(xb, dr_ref[...])
        gi_ref[...] = _dot_tn(xb, di_ref[...])

    blk = pl.BlockSpec((t, cb), lambda j: (0, j))
    wspec = pl.BlockSpec((None, cb, cb), lambda j: (j, 0, 0))
    out = jax.ShapeDtypeStruct((nb, cb, cb), F32)
    return pl.pallas_call(
        body, name=name, grid=(nb,),
        in_specs=[blk, blk, blk], out_specs=[wspec, wspec], out_shape=[out, out],
        compiler_params=_params("parallel"),
    )(rec, dpr, dpi)


def _conv_a_bwd(d_rec, gr, cwb, *, cb, name):
    t, r = d_rec.shape
    nb = r // cb

    def body(dy_ref, x_ref, w_ref, dx_ref, dw_ref):
        dx, dw = _conv_taps_bwd(dy_ref[...], x_ref[...], w_ref[...], 4)
        dx_ref[...] = dx.astype(BF16)
        dw_ref[...] = dw

    blk = pl.BlockSpec((t, cb), lambda j: (0, j))
    vspec = pl.BlockSpec((8, cb), lambda j: (0, j))
    return pl.pallas_call(
        body, name=name, grid=(nb,),
        in_specs=[blk, pl.BlockSpec((t, cb), lambda j: (0, j + nb)), vspec],
        out_specs=[blk, vspec],
        out_shape=[jax.ShapeDtypeStruct((t, r), BF16), jax.ShapeDtypeStruct((8, r), F32)],
        compiler_params=_params("parallel"),
    )(d_rec, gr, cwb)


def _split3(x):
    p0 = x.astype(BF16)
    r1 = x - p0.astype(F32)
    p1 = r1.astype(BF16)
    p2 = (r1 - p1.astype(F32)).astype(BF16)
    return p0, p1, p2


def _fgate_fwd(fp, fb, *, tq, name):
    t = fp.shape[0]

    def body(f_ref, b_ref, c_ref, ct_ref):
        logf = -_softplus(-(f_ref[...] + b_ref[...]))
        rows = pl.program_id(0) * tq + lax.broadcasted_iota(jnp.int32, (tq, t), 0)
        cols = lax.broadcasted_iota(jnp.int32, (tq, t), 1)
        tri = (cols <= rows).astype(BF16)
        p0, p1, p2 = _split3(logf)
        c = _dot_nn(tri, p0) + _dot_nn(tri, p1) + _dot_nn(tri, p2)
        c_ref[...] = c
        ct_ref[...] = c.T

    return pl.pallas_call(
        body, name=name, grid=(t // tq,),
        in_specs=[pl.BlockSpec((t, LANES), lambda i: (0, 0)), pl.BlockSpec((1, LANES), lambda i: (0, 0))],
        out_specs=[pl.BlockSpec((tq, LANES), lambda i: (i, 0)), pl.BlockSpec((LANES, tq), lambda i: (0, i))],
        out_shape=[jax.ShapeDtypeStruct((t, LANES), F32), jax.ShapeDtypeStruct((LANES, t), F32)],
        compiler_params=_params("parallel"),
    )(fp, fb)


def _fgate_bwd(dct, fp, fb, *, tq, name):
    t = fp.shape[0]

    def body(d_ref, f_ref, b_ref, o_ref, db_ref):
        i = pl.program_id(0)
        rows = lax.broadcasted_iota(jnp.int32, (t, tq), 0)
        cols = i * tq + lax.broadcasted_iota(jnp.int32, (t, tq), 1)
        tri = (rows >= cols).astype(BF16)
        p0, p1, p2 = _split3(d_ref[...])
        dlogf = (_dot_nn(p0, tri) + _dot_nn(p1, tri) + _dot_nn(p2, tri)).T
        df = dlogf * _sigmoid(-(f_ref[...] + b_ref[...]))
        o_ref[...] = df.astype(BF16)
        upd = _rows8([jnp.sum(df, axis=0, keepdims=True)], LANES)

        @pl.when(i == 0)
        def _():
            db_ref[...] = upd

        @pl.when(i > 0)
        def _():
            db_ref[...] += upd

    return pl.pallas_call(
        body, name=name, grid=(t // tq,),
        in_specs=[pl.BlockSpec((LANES, t), lambda i: (0, 0)), pl.BlockSpec((tq, LANES), lambda i: (i, 0)),
                  pl.BlockSpec((1, LANES), lambda i: (0, 0))],
        out_specs=[pl.BlockSpec((tq, LANES), lambda i: (i, 0)), pl.BlockSpec((8, LANES), lambda i: (0, 0))],
        out_shape=[jax.ShapeDtypeStruct((t, LANES), BF16), jax.ShapeDtypeStruct((8, LANES), F32)],
        compiler_params=_params("arbitrary"),
    )(dct, fp, fb)


def _pair_sum(a, b, *, tm, name):
    t, d = a.shape

    def body(a_ref, b_ref, o_ref):
        o_ref[...] = (a_ref[...] + b_ref[...]).astype(BF16)

    row = pl.BlockSpec((tm, d), lambda i: (i, 0))
    return pl.pallas_call(
        body, name=name, grid=(t // tm,), in_specs=[row, row], out_specs=row,
        out_shape=jax.ShapeDtypeStruct((t, d), BF16), compiler_params=_params("parallel"),
    )(a, b)


def _head_masks(dh):
    lane = lax.broadcasted_iota(jnp.int32, (1, LANES), 1)
    return [((lane >= e * dh) & (lane < (e + 1) * dh)) for e in range(LANES // dh)]


def _head_c(c_blk, ct_blk, head):
    lane = lax.broadcasted_iota(jnp.int32, c_blk.shape, 1)
    c_col = jnp.sum(jnp.where(lane == head, c_blk, 0.0), axis=1, keepdims=True)
    sub = lax.broadcasted_iota(jnp.int32, ct_blk.shape, 0)
    c_row = jnp.sum(jnp.where(sub == head, ct_blk, 0.0), axis=0, keepdims=True)
    return c_col, c_row


def _attn_probs(qm, k, c_col, c_row, q0, scale):
    tq, t = qm.shape[0], k.shape[0]
    s = _dot_nt(qm, k) * scale + c_col - c_row
    qi = q0 + lax.broadcasted_iota(jnp.int32, (tq, t), 0)
    ki = lax.broadcasted_iota(jnp.int32, (tq, t), 1)
    s = jnp.where(ki <= qi, s, -jnp.inf)
    m = jnp.max(s, axis=-1, keepdims=True)
    p = jnp.exp(s - m)
    return p / jnp.sum(p, axis=-1, keepdims=True)


def _attn_fwd(qg, kv, c, ct, *, tq, name):
    t, d2 = qg.shape
    d = d2 // 2
    dh = d // N_HEADS
    hpb = LANES // dh
    nhb = d // LANES
    scale = dh ** -0.5

    def body(q_ref, og_ref, k_ref, v_ref, c_ref, ct_ref, o_ref, y_ref):
        hb = pl.program_id(0)
        q0 = pl.program_id(1) * tq
        q = q_ref[...]
        k = k_ref[...]
        v = v_ref[...]
        o = jnp.zeros((tq, LANES), F32)
        for e, msk in enumerate(_head_masks(dh)):
            c_col, c_row = _head_c(c_ref[...], ct_ref[...], hb * hpb + e)
            p = _attn_probs(jnp.where(msk, q, 0.0).astype(BF16), k, c_col, c_row, q0, scale)
            o = o + _dot_nn(p.astype(BF16), jnp.where(msk, v, jnp.zeros_like(v)))
        o_ref[...] = o
        y_ref[...] = (o * _sigmoid(og_ref[...])).astype(BF16)

    qblk = pl.BlockSpec((tq, LANES), lambda h, i: (i, h))
    return pl.pallas_call(
        body, name=name, grid=(nhb, t // tq),
        in_specs=[qblk, pl.BlockSpec((tq, LANES), lambda h, i: (i, h + nhb)),
                  pl.BlockSpec((t, LANES), lambda h, i: (0, h)), pl.BlockSpec((t, LANES), lambda h, i: (0, h + nhb)),
                  pl.BlockSpec((tq, LANES), lambda h, i: (i, 0)), pl.BlockSpec((N_HEADS, t), lambda h, i: (0, 0))],
        out_specs=[qblk, qblk],
        out_shape=[jax.ShapeDtypeStruct((t, d), F32), jax.ShapeDtypeStruct((t, d), BF16)],
        compiler_params=_params("parallel", "parallel"),
    )(qg, qg, kv, kv, c, ct)


def _attn_bwd(dy, qg, o, kv, c, ct, *, tq, name):
    t, d2 = qg.shape
    d = d2 // 2
    dh = d // N_HEADS
    hpb = LANES // dh
    nhb = d // LANES
    scale = dh ** -0.5

    def body(dy_ref, q_ref, og_ref, o_ref, k_ref, v_ref, c_ref, ct_ref, dqg_ref, dk_ref, dv_ref, dc_ref):
        hb = pl.program_id(0)
        step = pl.program_id(1)
        q0 = step * tq
        q = q_ref[...]
        k = k_ref[...]
        v = v_ref[...]
        sg = _sigmoid(og_ref[...])
        dyv = dy_ref[...]
        do = dyv * sg
        dqg_ref[1] = (dyv * o_ref[...] * sg * (1.0 - sg)).astype(BF16)
        dq = jnp.zeros((tq, LANES), F32)
        dk = jnp.zeros((t, LANES), F32)
        dv = jnp.zeros((t, LANES), F32)
        dc_rows = []
        for e, msk in enumerate(_head_masks(dh)):
            c_col, c_row = _head_c(c_ref[...], ct_ref[...], hb * hpb + e)
            qm = jnp.where(msk, q, 0.0).astype(BF16)
            dom = jnp.where(msk, do, 0.0).astype(BF16)
            p = _attn_probs(qm, k, c_col, c_row, q0, scale)
            dp = _dot_nt(dom, v)
            dsc = p * (dp - jnp.sum(p * dp, axis=-1, keepdims=True))
            dsb = (dsc * scale).astype(BF16)
            dq = dq + _dot_nn(dsb, jnp.where(msk, k, jnp.zeros_like(k)))
            dk = dk + _dot_tn(dsb, qm)
            dv = dv + _dot_tn(p.astype(BF16), dom)
            dc_rows.append(-jnp.sum(dsc, axis=0, keepdims=True))
        dqg_ref[0] = dq.astype(BF16)
        dcu = _rows8(dc_rows, t)

        @pl.when(step == 0)
        def _():
            dk_ref[...] = dk
            dv_ref[...] = dv
            dc_ref[...] = dcu

        @pl.when(step > 0)
        def _():
            dk_ref[...] += dk
            dv_ref[...] += dv
            dc_ref[...] += dcu

    qblk = pl.BlockSpec((tq, LANES), lambda h, i: (i, h))
    kblk = pl.BlockSpec((t, LANES), lambda h, i: (0, h))
    return pl.pallas_call(
        body, name=name, grid=(nhb, t // tq),
        in_specs=[qblk, qblk, pl.BlockSpec((tq, LANES), lambda h, i: (i, h + nhb)), qblk,
                  kblk, pl.BlockSpec((t, LANES), lambda h, i: (0, h + nhb)),
                  pl.BlockSpec((tq, LANES), lambda h, i: (i, 0)), pl.BlockSpec((N_HEADS, t), lambda h, i: (0, 0))],
        out_specs=[pl.BlockSpec((2, tq, LANES), lambda h, i: (0, i, h)), kblk, kblk,
                   pl.BlockSpec((None, 8, t), lambda h, i: (h, 0, 0))],
        out_shape=[jax.ShapeDtypeStruct((2, t, d), BF16), jax.ShapeDtypeStruct((t, d), F32),
                   jax.ShapeDtypeStruct((t, d), F32), jax.ShapeDtypeStruct((nhb, 8, t), F32)],
        compiler_params=_params("parallel", "arbitrary"),
    )(dy, qg, qg, o, kv, kv, c, ct)


def _loss_bwd(h, tgt, *, lo, hi, tm, name):
    t, d = h.shape

    def body(h_ref, t_ref, l_ref, dy_ref):
        i = pl.program_id(0)
        rows = i * tm + lax.broadcasted_iota(jnp.int32, (tm, d), 0)
        err = jnp.where((rows >= lo) & (rows < hi), h_ref[...] - t_ref[...], 0.0)
        dy_ref[...] = err * (1.0 / d)
        part = jnp.sum(jnp.sum(err * err, axis=0, keepdims=True), axis=1, keepdims=True) * (0.5 / d)
        upd = jnp.broadcast_to(part, (8, LANES))

        @pl.when(i == 0)
        def _():
            l_ref[...] = upd

        @pl.when(i > 0)
        def _():
            l_ref[...] += upd

    row = pl.BlockSpec((tm, d), lambda i: (i, 0))
    return pl.pallas_call(
        body, name=name, grid=(t // tm,),
        in_specs=[row, row],
        out_specs=[pl.BlockSpec((8, LANES), lambda i: (0, 0)), row],
        out_shape=[jax.ShapeDtypeStruct((8, LANES), F32), jax.ShapeDtypeStruct((t, d), F32)],
        compiler_params=_params("arbitrary"),
    )(h, tgt)


def _adamw(w, g, m, v, *, name):
    r, c = w.shape
    tr = r
    for cand in (512, 256, 128, 64, 32, 16, 8):
        if r % cand == 0 and r > cand:
            tr = cand
            break
    bc1 = 1.0 / (1.0 - ADAM_B1 ** ADAM_STEP)
    bc2 = 1.0 / (1.0 - ADAM_B2 ** ADAM_STEP)

    def body(w_ref, g_ref, m_ref, v_ref, d_ref, nm_ref, nv_ref):
        gv = g_ref[...]
        nm = ADAM_B1 * m_ref[...] + (1.0 - ADAM_B1) * gv
        nv = ADAM_B2 * v_ref[...] + (1.0 - ADAM_B2) * (gv * gv)
        d_ref[...] = (-ADAM_LR) * ((nm * bc1) / (jnp.sqrt(nv * bc2) + ADAM_EPS) + ADAM_WD * w_ref[...])
        nm_ref[...] = nm
        nv_ref[...] = nv

    blk = pl.BlockSpec((tr, c), lambda i: (i, 0))
    out = jax.ShapeDtypeStruct((r, c), F32)
    return pl.pallas_call(
        body, name=name, grid=(r // tr,),
        in_specs=[blk] * 4, out_specs=[blk] * 3, out_shape=[out] * 3,
        compiler_params=_params("parallel"),
    )(w, g, m, v)


def _sum8(parts, *, name):
    _, r, c = parts.shape
    tr = r
    for cand in (512, 256, 128, 64, 32, 16):
        if r % cand == 0 and r > cand:
            tr = cand
            break

    def body(p_ref, o_ref):
        acc = p_ref[0].astype(F32)
        for k in range(1, N_DEV):
            acc = acc + p_ref[k].astype(F32)
        o_ref[...] = acc

    return pl.pallas_call(
        body, name=name, grid=(r // tr,),
        in_specs=[pl.BlockSpec((N_DEV, tr, c), lambda i: (0, i, 0))],
        out_specs=pl.BlockSpec((tr, c), lambda i: (i, 0)),
        out_shape=jax.ShapeDtypeStruct((r, c), F32),
        compiler_params=_params("parallel"),
    )(parts)


def _my_index():
    return 4 * lax.axis_index("x") + 2 * lax.axis_index("y") + lax.axis_index("c")


def _peer(k):
    x, y, c = lax.axis_index("x"), lax.axis_index("y"), lax.axis_index("c")
    px = x ^ ((k >> 2) & 1)
    py = y ^ ((k >> 1) & 1)
    pc = c ^ (k & 1)
    return (px, py, pc), 4 * px + 2 * py + pc


def _all_gather(shards, *, name):
    n_arr = len(shards)

    def body(*refs):
        ins, outs = refs[:n_arr], refs[n_arr:2 * n_arr]
        send_sems, recv_sems, local_sems = refs[2 * n_arr:]
        me = _my_index()
        local = [pltpu.make_async_copy(ins[n], outs[n].at[me], local_sems.at[n]) for n in range(n_arr)]
        for cp in local:
            cp.start()
        sends = []
        for k in range(1, N_DEV):
            peer, _ = _peer(k)
            for n in range(n_arr):
                cp = pltpu.make_async_remote_copy(
                    src_ref=ins[n], dst_ref=outs[n].at[me], send_sem=send_sems.at[n, k - 1],
                    recv_sem=recv_sems.at[n, k - 1], device_id=peer, device_id_type=pl.DeviceIdType.MESH)
                cp.start()
                sends.append(cp)
        for k in range(1, N_DEV):
            peer, pidx = _peer(k)
            for n in range(n_arr):
                pltpu.make_async_remote_copy(
                    src_ref=ins[n], dst_ref=outs[n].at[pidx], send_sem=send_sems.at[n, k - 1],
                    recv_sem=recv_sems.at[n, k - 1], device_id=peer, device_id_type=pl.DeviceIdType.MESH).wait_recv()
        for cp in sends:
            cp.wait_send()
        for cp in local:
            cp.wait()

    hbm = pl.BlockSpec(memory_space=pl.ANY)
    return pl.pallas_call(
        body, name=name,
        in_specs=[hbm] * n_arr, out_specs=[hbm] * n_arr,
        out_shape=[jax.ShapeDtypeStruct((N_DEV,) + s.shape, s.dtype) for s in shards],
        scratch_shapes=[pltpu.SemaphoreType.DMA((n_arr, N_DEV - 1)), pltpu.SemaphoreType.DMA((n_arr, N_DEV - 1)),
                        pltpu.SemaphoreType.DMA((n_arr,))],
        compiler_params=pltpu.CompilerParams(has_side_effects=True),
    )(*shards)


def _exchange(parts, *, name):
    n_arr = len(parts)

    def body(*refs):
        ins, outs = refs[:n_arr], refs[n_arr:2 * n_arr]
        send_sems, recv_sems, local_sems = refs[2 * n_arr:]
        me = _my_index()
        local = [pltpu.make_async_copy(ins[n].at[me], outs[n].at[me], local_sems.at[n]) for n in range(n_arr)]
        for cp in local:
            cp.start()
        sends = []
        for k in range(1, N_DEV):
            peer, pidx = _peer(k)
            for n in range(n_arr):
                cp = pltpu.make_async_remote_copy(
                    src_ref=ins[n].at[pidx], dst_ref=outs[n].at[me], send_sem=send_sems.at[n, k - 1],
                    recv_sem=recv_sems.at[n, k - 1], device_id=peer, device_id_type=pl.DeviceIdType.MESH)
                cp.start()
                sends.append(cp)
        for k in range(1, N_DEV):
            peer, pidx = _peer(k)
            for n in range(n_arr):
                pltpu.make_async_remote_copy(
                    src_ref=ins[n].at[me], dst_ref=outs[n].at[pidx], send_sem=send_sems.at[n, k - 1],
                    recv_sem=recv_sems.at[n, k - 1], device_id=peer, device_id_type=pl.DeviceIdType.MESH).wait_recv()
        for cp in sends:
            cp.wait_send()
        for cp in local:
            cp.wait()

    hbm = pl.BlockSpec(memory_space=pl.ANY)
    return pl.pallas_call(
        body, name=name,
        in_specs=[hbm] * n_arr, out_specs=[hbm] * n_arr,
        out_shape=[jax.ShapeDtypeStruct(p.shape, p.dtype) for p in parts],
        scratch_shapes=[pltpu.SemaphoreType.DMA((n_arr, N_DEV - 1)), pltpu.SemaphoreType.DMA((n_arr, N_DEV - 1)),
                        pltpu.SemaphoreType.DMA((n_arr,))],
        compiler_params=pltpu.CompilerParams(has_side_effects=True),
    )(*parts)


PACK_COLS = 1024


def _pack(arrs, dtype):
    flat = jnp.concatenate([a.astype(dtype).reshape(-1) for a in arrs])
    quantum = 16 * PACK_COLS
    pad = (-flat.shape[0]) % quantum
    if pad:
        flat = jnp.concatenate([flat, jnp.zeros((pad,), dtype)])
    return flat.reshape(-1, PACK_COLS)


def _unpack(slab, shapes, lead):
    lead_shape = slab.shape[:lead]
    flat = slab.reshape(lead_shape + (-1,))
    outs, off = [], 0
    for shp in shapes:
        size = math.prod(shp)
        outs.append(flat[..., off:off + size].reshape(lead_shape + tuple(shp)))
        off += size
    return outs


def _cols_full(g):
    g = jnp.moveaxis(g, 0, -2)
    return g.reshape(g.shape[:-2] + (g.shape[-2] * g.shape[-1],))


def _cols_split(full):
    n = full.shape[-1] // N_DEV
    return jnp.moveaxis(full.reshape(full.shape[:-1] + (N_DEV, n)), -2, 0)


def _rows_full(g):
    g = jnp.moveaxis(g, 0, 1)
    return g.reshape(g.shape[0], g.shape[1] * g.shape[2], g.shape[3])


def _rows_split(full):
    l, r8, d = full.shape
    return jnp.moveaxis(full.reshape(l, N_DEV, r8 // N_DEV, d), 1, 0)


def _block_diag(w, per):
    n, b, _ = w.shape
    w4 = w.reshape(n // per, per, b, b)
    eye = jnp.eye(per, dtype=w.dtype)
    return jnp.einsum('gpab,pq->gpaqb', w4, eye).reshape(n // per, per * b, per * b)


def _block_diag_extract(g, per):
    gn, cb, _ = g.shape
    b = cb // per
    g5 = g.reshape(gn, per, b, per, b)
    return jnp.stack([g5[:, p, :, p, :] for p in range(per)], axis=1).reshape(gn * per, b, b)


def _pad_rows8(rows, width):
    out = jnp.zeros((8, width), F32)
    for k, r in enumerate(rows):
        out = out.at[k].set(r)
    return out


def _local_step(h0, tgt, n_meta, n_tok, wts):
    tp, d = h0.shape
    tm = tp // 8 if (tp // 8) % 16 == 0 else tp
    tq = 128
    r_dim = wts["a_w_out"].shape[1]
    cb = wts["a_bd_r"].shape[-1]
    f_dim = wts["f_w_out"].shape[1]
    tcf = 128
    sb = 256 if r_dim % 256 == 0 else LANES
    n_b = N_LAYERS - N_A_LAYERS

    def tile(n, prefer):
        for c in prefer:
            if n % c == 0:
                return c
        return n

    h, h_bf = h0, h0.astype(BF16)
    saved = []
    kvs = None
    for layer in range(N_LAYERS):
        sv = {"h": h, "h_bf": h_bf}
        if layer < N_A_LAYERS:
            w_in = wts["a_w_in"][layer]
            sv["gr"] = _mm_nn(h_bf, w_in, tn=tile(w_in.shape[1], (512, 256, 128)), out_dtype=F32, name="a_in_proj")
            sv["rec"] = _conv_a_fwd(sv["gr"], wts["a_cwb"][layer], cb=cb, name="a_conv_fwd")
            a, u, sv["r"], sv["i"] = _gates_fwd(sv["rec"], wts["a_bd_r"][layer], wts["a_bd_i"][layer],
                                                wts["a_vecs"][layer], tm=tm, name="a_gates_fwd")
            sv["a"] = a
            sv["hr"], y_bf = _scan_fwd(a, u, sv["gr"], cb=sb, name="a_scan_fwd")
            w_out = wts["a_w_out"][layer]
        else:
            j = layer - N_A_LAYERS
            if j == 0:
                kvs = {"h_bf": h_bf}
                kvs["kv"] = _mm_nn(h_bf, wts["kv_w"][:, :2 * d], tn=tile(2 * d, (512, 256, 128)), out_dtype=BF16,
                                   name="kv_proj")
                kvs["fp"] = _mm_nn(h_bf, wts["kv_w"][:, 2 * d:], tn=LANES, out_dtype=F32, name="f_proj")
                kvs["c"], ct = _fgate_fwd(kvs["fp"], wts["kv_fb"], tq=tq, name="fgate_fwd")
                kvs["ct"] = ct[:N_HEADS]
            w_in = wts["b_w_in"][j]
            sv["qg"] = _mm_nn(h_bf, w_in, tn=tile(2 * d, (512, 256, 128)), out_dtype=F32, name="b_in_proj")
            sv["o"], y_bf = _attn_fwd(sv["qg"], kvs["kv"], kvs["c"], kvs["ct"], tq=tq, name="attn_fwd")
            w_out = wts["b_w_out"][j]
        sv["y_bf"] = y_bf
        sv["s1"], h, h_bf = _mm_ln(y_bf, w_out, h, wts["ln1_g"][layer], wts["ln1_b"][layer], tm=tm, name="mix_out_ln")
        sv["h1"], sv["h1_bf"] = h, h_bf
        sv["z"] = _mm_nn(h_bf, wts["f_w_in"][layer], tn=tile(2 * f_dim, (512, 256, 128)), out_dtype=F32,
                         name="f_in_proj")
        sv["yf_bf"] = _convglu_fwd(sv["z"], wts["f_cwb"][layer], tc=tile(f_dim, (tcf, 128)), name="f_convglu_fwd")
        sv["s2"], h, h_bf = _mm_ln(sv["yf_bf"], wts["f_w_out"][layer], h, wts["ln2_g"][layer], wts["ln2_b"][layer],
                                   tm=tm, name="ffn_out_ln")
        saved.append(sv)

    loss_tile, dh = _loss_bwd(h, tgt, lo=n_meta, hi=n_tok, tm=tm, name="loss")

    grads = {k: [None] * N_LAYERS for k in ("f_w_in", "f_cwb", "f_w_out", "ln1_gb", "ln2_gb")}
    grads.update({k: [None] * N_A_LAYERS for k in ("a_w_in", "a_cwb", "a_bd_r", "a_bd_i", "a_vecs", "a_w_out")})
    grads.update({k: [None] * n_b for k in ("b_w_in", "b_w_out")})
    dkv = []
    for layer in reversed(range(N_LAYERS)):
        sv = saved[layer]
        ds, ds_bf, grads["ln2_gb"][layer] = _ln_bwd(dh, sv["s2"], wts["ln2_g"][layer], tm=tm, name="ln_bwd")
        dz, grads["f_cwb"][layer] = _ffn_bwd_mid(ds_bf, wts["f_w_out"][layer], sv["z"], wts["f_cwb"][layer],
                                                 tc=tile(f_dim, (tcf, 128)), name="f_bwd_mid")
        grads["f_w_out"][layer] = _mm_tn_out(sv["yf_bf"], ds_bf, tm=tile(f_dim, (1408, 768, 512, 256, 128)),
                                             name="f_w_out_grad")
        dh = _mm_nt_k(dz, wts["f_w_in"][layer], ds, tm=tile(tp, (544, 512, 256, 128)), name="f_in_bwd")
        grads["f_w_in"][layer] = _mm_tn_in(sv["h1_bf"], dz, tn=tile(f_dim, (1408, 768, 512, 256, 128)),
                                           name="f_w_in_grad")
        ds, ds_bf, grads["ln1_gb"][layer] = _ln_bwd(dh, sv["s1"], wts["ln1_g"][layer], tm=tm, name="ln_bwd")
        if layer < N_A_LAYERS:
            w_out = wts["a_w_out"][layer]
            dy = _mm_nt_n(ds_bf, w_out, tn=tile(r_dim, (512, 384, 256, 128)), name="a_out_bwd")
            grads["a_w_out"][layer] = _mm_tn_out(sv["y_bf"], ds_bf, tm=tile(r_dim, (768, 512, 384, 256, 128)),
                                                 name="a_w_out_grad")
            d_h, d_a, dgate = _scan_bwd(dy, sv["gr"], sv["hr"], sv["a"], cb=sb, name="a_scan_bwd")
            d_rec, dpr, dpi, grads["a_vecs"][layer] = _gates_bwd(
                sv["rec"], sv["r"], sv["i"], sv["a"], d_h, d_a, wts["a_bd_r"][layer], wts["a_bd_i"][layer],
                wts["a_vecs"][layer], tm=tm, name="a_gates_bwd")
            grads["a_bd_r"][layer], grads["a_bd_i"][layer] = _bd_grad(sv["rec"], dpr, dpi, cb=cb, name="a_bd_grad")
            d_rec0, grads["a_cwb"][layer] = _conv_a_bwd(d_rec, sv["gr"], wts["a_cwb"][layer], cb=cb,
                                                        name="a_conv_bwd")
            dact = jnp.stack([dgate, d_rec0])
            dh = _mm_nt_k(dact, wts["a_w_in"][layer], ds, tm=tile(tp, (544, 512, 256, 128)), name="a_in_bwd")
            grads["a_w_in"][layer] = _mm_tn_in(sv["h_bf"], dact, tn=tile(r_dim, (1536, 768, 512, 384, 256, 128)),
                                               name="a_w_in_grad")
        else:
            j = layer - N_A_LAYERS
            dy = _mm_nt_n(ds_bf, wts["b_w_out"][j], tn=tile(d, (512, 256, 128)), name="b_out_bwd")
            grads["b_w_out"][j] = _mm_tn_out(sv["y_bf"], ds_bf, tm=tile(d, (512, 256, 128)), name="b_w_out_grad")
            dqg, dk, dv, dc = _attn_bwd(dy, sv["qg"], sv["o"], kvs["kv"], kvs["c"], kvs["ct"], tq=tq,
                                        name="attn_bwd")
            dkv.append((dk, dv, dc))
            dh = _mm_nt_k(dqg, wts["b_w_in"][j], ds, tm=tile(tp, (544, 512, 256, 128)), name="b_in_bwd")
            grads["b_w_in"][j] = _mm_tn_in(sv["h_bf"], dqg, tn=tile(d, (1024, 512, 256, 128)), name="b_w_in_grad")
            if j == 0:
                hpb = LANES // (d // N_HEADS)
                dct = (dkv[0][2] + dkv[1][2])[:, :hpb, :].reshape(N_HEADS, tp)
                dct = jnp.concatenate([dct, jnp.zeros((LANES - N_HEADS, tp), F32)])
                df_bf, grads["kv_fb"] = _fgate_bwd(dct, kvs["fp"], wts["kv_fb"], tq=tq, name="fgate_bwd")
                dkvz = jnp.concatenate([_pair_sum(dkv[0][0], dkv[1][0], tm=tm, name="kv_pair_sum"),
                                        _pair_sum(dkv[0][1], dkv[1][1], tm=tm, name="kv_pair_sum"), df_bf], axis=1)
                dh = _mm_nt_k(dkvz[None], wts["kv_w"], dh, tm=tile(tp, (544, 512, 256, 128)), name="kv_in_bwd",
                              alpha=1.0)
                grads["kv_w"] = _mm_tn_in(kvs["h_bf"], dkvz[None], tn=LANES, name="kv_w_grad")
    return loss_tile, dh, grads


WEIGHT_NAMES = ("meta", "a_w_in", "a_conv_w", "a_conv_b", "a_w_r", "a_b_r", "a_w_i", "a_b_i", "a_lambda", "a_w_out",
                "kv_w", "kv_f_b", "b_w_in", "b_w_out", "f_w_in", "f_conv_w", "f_conv_b", "f_w_out",
                "ln1_g", "ln1_b", "ln2_g", "ln2_b")
COL_BF16 = ("a_w_in", "kv_w", "b_w_in", "f_w_in")
ROW_BF16 = ("a_w_out", "b_w_out", "f_w_out")
COL_F32 = ("meta", "a_conv_w", "a_conv_b", "a_b_r", "a_b_i", "a_lambda", "f_conv_w")
REPLICATED = ("a_w_r", "a_w_i", "kv_f_b", "f_conv_b", "ln1_g", "ln1_b", "ln2_g", "ln2_b")


def _lru_block_cols(r_dim):
    lru = r_dim // N_LRU_BLOCKS
    return lru * LANES // math.gcd(lru, LANES)


def _prepare_weights(full):
    d = full["meta"].shape[1]
    r_dim = full["a_w_out"].shape[1]
    f2 = full["f_w_in"].shape[2]
    cb = _lru_block_cols(r_dim)
    per = cb // (r_dim // N_LRU_BLOCKS)
    n_a = full["a_w_in"].shape[0]
    zeros_r = jnp.zeros((n_a, 3, r_dim), F32)
    zeros_f = jnp.zeros((N_LAYERS, 4, f2), F32)
    kv_cols = full["kv_w"].shape[1]
    kv_pad = 2 * d + LANES - kv_cols
    return {
        "a_w_in": full["a_w_in"], "a_w_out": full["a_w_out"], "b_w_in": full["b_w_in"], "b_w_out": full["b_w_out"],
        "f_w_in": full["f_w_in"], "f_w_out": full["f_w_out"],
        "kv_w": jnp.concatenate([full["kv_w"], jnp.zeros((d, kv_pad), BF16)], axis=1),
        "kv_fb": jnp.concatenate([full["kv_f_b"], jnp.zeros((LANES - N_HEADS,), F32)])[None],
        "a_cwb": jnp.concatenate([full["a_conv_w"], full["a_conv_b"][:, None], zeros_r], axis=1),
        "a_vecs": jnp.concatenate([jnp.stack([full["a_b_r"], full["a_b_i"], full["a_lambda"]], axis=1),
                                   jnp.zeros((n_a, 5, r_dim), F32)], axis=1),
        "a_bd_r": jnp.stack([_block_diag(full["a_w_r"][l], per) for l in range(n_a)]).astype(BF16),
        "a_bd_i": jnp.stack([_block_diag(full["a_w_i"][l], per) for l in range(n_a)]).astype(BF16),
        "f_cwb": jnp.concatenate([full["f_conv_w"], full["f_conv_b"][:, None], zeros_f], axis=1),
        "ln1_g": full["ln1_g"][:, None], "ln1_b": full["ln1_b"][:, None],
        "ln2_g": full["ln2_g"][:, None], "ln2_b": full["ln2_b"][:, None],
    }


def _finish_grads(grads, d_h0, n_meta, full_shapes):
    r_dim = full_shapes["a_w_out"][1]
    per = _lru_block_cols(r_dim) // (r_dim // N_LRU_BLOCKS)
    a_cwb = jnp.stack(grads["a_cwb"])
    a_vecs = jnp.stack(grads["a_vecs"])
    f_cwb = jnp.stack([jnp.concatenate([g[0], g[1]], axis=1) for g in grads["f_cwb"]])
    ln1 = jnp.stack(grads["ln1_gb"])
    ln2 = jnp.stack(grads["ln2_gb"])
    return {
        "meta": d_h0[:n_meta],
        "a_w_in": jnp.stack(grads["a_w_in"]), "a_conv_w": a_cwb[:, :4], "a_conv_b": a_cwb[:, 4],
        "a_w_r": jnp.stack([_block_diag_extract(g, per) for g in grads["a_bd_r"]]),
        "a_b_r": a_vecs[:, 0],
        "a_w_i": jnp.stack([_block_diag_extract(g, per) for g in grads["a_bd_i"]]),
        "a_b_i": a_vecs[:, 1], "a_lambda": a_vecs[:, 2],
        "a_w_out": jnp.stack(grads["a_w_out"]),
        "kv_w": grads["kv_w"][:, :full_shapes["kv_w"][1]], "kv_f_b": grads["kv_fb"][0, :N_HEADS],
        "b_w_in": jnp.stack(grads["b_w_in"]), "b_w_out": jnp.stack(grads["b_w_out"]),
        "f_w_in": jnp.stack(grads["f_w_in"]), "f_conv_w": f_cwb[:, :3], "f_conv_b": f_cwb[:, 3],
        "f_w_out": jnp.stack(grads["f_w_out"]),
        "ln1_g": ln1[:, 0], "ln1_b": ln1[:, 1], "ln2_g": ln2[:, 0], "ln2_b": ln2[:, 1],
    }


def _pack8(arrs, dtype):
    flat = jnp.concatenate([a.astype(dtype).reshape(N_DEV, -1) for a in arrs], axis=1)
    pad = (-flat.shape[1]) % (16 * PACK_COLS)
    if pad:
        flat = jnp.concatenate([flat, jnp.zeros((N_DEV, pad), dtype)], axis=1)
    return flat.reshape(N_DEV, -1, PACK_COLS)


def _slab2d(a):
    if a.size % PACK_COLS == 0:
        return a.reshape(-1, PACK_COLS)
    return a.reshape(-1, a.shape[-1])


def kernel(x, meta, a_w_in, a_conv_w, a_conv_b, a_w_r, a_b_r, a_w_i, a_b_i, a_lambda, a_w_out, kv_w, kv_f_b, b_w_in, b_w_out, f_w_in, f_conv_w, f_conv_b, f_w_out, ln1_g, ln1_b, ln2_g, ln2_b, loss_target, m_meta, m_a_w_in, m_a_conv_w, m_a_conv_b, m_a_w_r, m_a_b_r, m_a_w_i, m_a_b_i, m_a_lambda, m_a_w_out, m_kv_w, m_kv_f_b, m_b_w_in, m_b_w_out, m_f_w_in, m_f_conv_w, m_f_conv_b, m_f_w_out, m_ln1_g, m_ln1_b, m_ln2_g, m_ln2_b, v_meta, v_a_w_in, v_a_conv_w, v_a_conv_b, v_a_w_r, v_a_b_r, v_a_w_i, v_a_b_i, v_a_lambda, v_a_w_out, v_kv_w, v_kv_f_b, v_b_w_in, v_b_w_out, v_f_w_in, v_f_conv_w, v_f_conv_b, v_f_w_out, v_ln1_g, v_ln1_b, v_ln2_g, v_ln2_b):
    w = dict(meta=meta, a_w_in=a_w_in, a_conv_w=a_conv_w, a_conv_b=a_conv_b, a_w_r=a_w_r, a_b_r=a_b_r, a_w_i=a_w_i,
             a_b_i=a_b_i, a_lambda=a_lambda, a_w_out=a_w_out, kv_w=kv_w, kv_f_b=kv_f_b, b_w_in=b_w_in,
             b_w_out=b_w_out, f_w_in=f_w_in, f_conv_w=f_conv_w, f_conv_b=f_conv_b, f_w_out=f_w_out, ln1_g=ln1_g,
             ln1_b=ln1_b, ln2_g=ln2_g, ln2_b=ln2_b)
    m = dict(meta=m_meta, a_w_in=m_a_w_in, a_conv_w=m_a_conv_w, a_conv_b=m_a_conv_b, a_w_r=m_a_w_r, a_b_r=m_a_b_r,
             a_w_i=m_a_w_i, a_b_i=m_a_b_i, a_lambda=m_a_lambda, a_w_out=m_a_w_out, kv_w=m_kv_w, kv_f_b=m_kv_f_b,
             b_w_in=m_b_w_in, b_w_out=m_b_w_out, f_w_in=m_f_w_in, f_conv_w=m_f_conv_w, f_conv_b=m_f_conv_b,
             f_w_out=m_f_w_out, ln1_g=m_ln1_g, ln1_b=m_ln1_b, ln2_g=m_ln2_g, ln2_b=m_ln2_b)
    v = dict(meta=v_meta, a_w_in=v_a_w_in, a_conv_w=v_a_conv_w, a_conv_b=v_a_conv_b, a_w_r=v_a_w_r, a_b_r=v_a_b_r,
             a_w_i=v_a_w_i, a_b_i=v_a_b_i, a_lambda=v_a_lambda, a_w_out=v_a_w_out, kv_w=v_kv_w, kv_f_b=v_kv_f_b,
             b_w_in=v_b_w_in, b_w_out=v_b_w_out, f_w_in=v_f_w_in, f_conv_w=v_f_conv_w, f_conv_b=v_f_conv_b,
             f_w_out=v_f_w_out, ln1_g=v_ln1_g, ln1_b=v_ln1_b, ln2_g=v_ln2_g, ln2_b=v_ln2_b)

    big = COL_BF16 + ROW_BF16
    slab_b = _pack([w[n] for n in big], BF16)
    slab_s = _pack([w[n] for n in COL_F32], F32)
    got_b, got_s = _all_gather([slab_b, slab_s], name="gather_weights")
    parts = dict(zip(big, _unpack(got_b, [w[n].shape for n in big], 1)))
    parts.update(zip(COL_F32, _unpack(got_s, [w[n].shape for n in COL_F32], 1)))
    full = {n: w[n] for n in REPLICATED}
    for n in COL_BF16 + COL_F32:
        full[n] = _cols_full(parts[n])
    for n in ROW_BF16:
        full[n] = _rows_full(parts[n])
    full_shapes = {n: full[n].shape for n in WEIGHT_NAMES}

    n_meta, d = full["meta"].shape
    n_tok = n_meta + x.shape[1]
    tp = -(-n_tok // ROW_ALIGN) * ROW_ALIGN
    pad = jnp.zeros((tp - n_tok, d), F32)
    h0 = jnp.concatenate([full["meta"], x[0], pad])
    tgt = jnp.concatenate([jnp.zeros((n_meta, d), F32), loss_target[0], pad])
    loss_tile, d_h0, grads = _local_step(h0, tgt, n_meta, n_tok, _prepare_weights(full))
    g_full = _finish_grads(grads, d_h0, n_meta, full_shapes)
    loss = lax.psum(loss_tile[0, 0], MESH_AXES)
    grad_x = d_h0[n_meta:n_tok][None]

    out_b = [_cols_split(g_full[n]) for n in COL_BF16] + [_rows_split(g_full[n]) for n in ROW_BF16]
    out_s = [_cols_split(g_full[n]) for n in COL_F32]
    recv_b, recv_s = _exchange([_pack8(out_b, BF16), _pack8(out_s, F32)], name="scatter_grads")
    sum_b = _sum8(recv_b, name="sum_grads_bf16")
    sum_s = _sum8(recv_s, name="sum_grads_f32")
    g = dict(zip(big, _unpack(sum_b, [w[n].shape for n in big], 0)))
    g.update(zip(COL_F32, _unpack(sum_s, [w[n].shape for n in COL_F32], 0)))
    (got_r,) = _all_gather([_pack([g_full[n] for n in REPLICATED], F32)], name="gather_replicated_grads")
    sum_r = _sum8(got_r, name="sum_grads_replicated")
    g.update(zip(REPLICATED, _unpack(sum_r, [w[n].shape for n in REPLICATED], 0)))

    delta, new_m, new_v = {}, {}, {}
    for n in WEIGHT_NAMES:
        shp = w[n].shape
        dl, nm, nv = _adamw(_slab2d(w[n]), _slab2d(g[n]), _slab2d(m[n]), _slab2d(v[n]), name="adamw")
        delta[n], new_m[n], new_v[n] = dl.reshape(shp), nm.reshape(shp), nv.reshape(shp)
    return (loss, grad_x, *[g[n] for n in WEIGHT_NAMES], *[delta[n] for n in WEIGHT_NAMES],
            *[new_m[n] for n in WEIGHT_NAMES], *[new_v[n] for n in WEIGHT_NAMES])
```

```python
import math

import jax
import jax.numpy as jnp
from jax import lax
from jax.experimental import pallas as pl
from jax.experimental.pallas import tpu as pltpu

F32 = jnp.float32
BF16 = jnp.bfloat16

N_DEV = 8
MESH_AXES = ("x", "y", "c")
N_LAYERS = 4
N_A_LAYERS = 2
N_LRU_BLOCKS = 16
N_HEADS = 16
LRU_C = 8.0
DN_ALPHA = (2 * N_LAYERS) ** 0.25
LN_EPS = 1e-5
ADAM_LR, ADAM_B1, ADAM_B2, ADAM_EPS, ADAM_WD, ADAM_STEP = 0.001, 0.9, 0.999, 1e-08, 0.01, 10

LANES = 128
SUBLANES = 8
ROW_ALIGN = 128
VMEM_LIMIT_BYTES = 56 * 1024 * 1024
GELU_K = math.sqrt(2.0 / math.pi)
GELU_C = 0.044715
PACK_COLS = 1024


def _params(*sem):
    return pltpu.CompilerParams(dimension_semantics=sem, vmem_limit_bytes=VMEM_LIMIT_BYTES)


def _gelu(x):
    th = jnp.tanh(GELU_K * (x + GELU_C * x * x * x))
    return 0.5 * x * (1.0 + th)


def _gelu_and_grad(x):
    x2 = x * x
    th = jnp.tanh(GELU_K * (x + GELU_C * x2 * x))
    g = 0.5 * x * (1.0 + th)
    dg = 0.5 * (1.0 + th) + 0.5 * x * (1.0 - th * th) * (GELU_K * (1.0 + 3.0 * GELU_C * x2))
    return g, dg


def _sigmoid(x):
    return 1.0 / (1.0 + jnp.exp(-x))


def _expm1(x):
    small = x * (1.0 + 0.5 * x * (1.0 + (1.0 / 3.0) * x * (1.0 + 0.25 * x)))
    return jnp.where(jnp.abs(x) < 1e-2, small, jnp.exp(x) - 1.0)


def _softplus(x):
    e = jnp.exp(-jnp.abs(x))
    small = e * (1.0 - 0.5 * e * (1.0 - (2.0 / 3.0) * e))
    return jnp.maximum(x, 0.0) + jnp.where(e < 1e-2, small, jnp.log(1.0 + e))


def _shift_down(x, s):
    if s == 0:
        return x
    rows = lax.broadcasted_iota(jnp.int32, x.shape, 0)
    return jnp.where(rows >= s, pltpu.roll(x, s, 0), 0.0)


def _shift_up(x, s):
    if s == 0:
        return x
    n = x.shape[0]
    rows = lax.broadcasted_iota(jnp.int32, x.shape, 0)
    return jnp.where(rows < n - s, pltpu.roll(x, n - s, 0), 0.0)


def _dot_nn(a, b):
    return lax.dot_general(a, b, (((1,), (0,)), ((), ())), preferred_element_type=F32)


def _dot_nt(a, b):
    return lax.dot_general(a, b, (((1,), (1,)), ((), ())), preferred_element_type=F32)


def _dot_tn(a, b):
    return lax.dot_general(a, b, (((0,), (0,)), ((), ())), preferred_element_type=F32)


def _rows8(vals, width):
    rows = lax.broadcasted_iota(jnp.int32, (8, width), 0)
    out = jnp.zeros((8, width), F32)
    for k, v in enumerate(vals):
        out = jnp.where(rows == k, jnp.broadcast_to(v, (8, width)), out)
    return out


def _tile(n, prefer):
    for c in prefer:
        if n % c == 0:
            return c
    return n


def _mm_nn(a, b, *, tn, out_dtype, name):
    m, k = a.shape
    n = b.shape[1]

    def body(a_ref, b_ref, o_ref):
        o_ref[...] = _dot_nn(a_ref[...], b_ref[...]).astype(o_ref.dtype)

    return pl.pallas_call(
        body, name=name, grid=(n // tn,),
        in_specs=[pl.BlockSpec((m, k), lambda j: (0, 0)), pl.BlockSpec((k, tn), lambda j: (0, j))],
        out_specs=pl.BlockSpec((m, tn), lambda j: (0, j)),
        out_shape=jax.ShapeDtypeStruct((m, n), out_dtype),
        compiler_params=_params("parallel"),
    )(a, b)


def _proj_in(h_bf, g_in, layer, *, shard_major, name):
    t, k = h_bf.shape
    n = g_in.shape[3]

    def body(a_ref, b_ref, o_ref):
        o_ref[...] = _dot_nn(a_ref[...], b_ref[...])

    if shard_major:
        out_spec = pl.BlockSpec((None, t, n), lambda j: (j, 0, 0))
        out_shape = jax.ShapeDtypeStruct((N_DEV, t, n), F32)
    else:
        out_spec = pl.BlockSpec((t, n), lambda j: (0, j))
        out_shape = jax.ShapeDtypeStruct((t, N_DEV * n), F32)
    return pl.pallas_call(
        body, name=name, grid=(N_DEV,),
        in_specs=[pl.BlockSpec((t, k), lambda j: (0, 0)),
                  pl.BlockSpec((None, None, k, n), lambda j: (j, layer, 0, 0))],
        out_specs=out_spec, out_shape=out_shape,
        compiler_params=_params("parallel"),
    )(h_bf, g_in)


def _out_ln(y3, g_out, layer, hin, g, b, *, tm, name):
    nj, t, kj = y3.shape
    _, _, r, d = g_out.shape

    def body(y_ref, w_ref, hin_ref, g_ref, b_ref, s_ref, h_ref, hb_ref):
        w = w_ref[...].reshape(N_DEV * r, d)
        s = DN_ALPHA * hin_ref[...]
        for jj in range(nj):
            s = s + _dot_nn(y_ref[jj], w[jj * kj:(jj + 1) * kj])
        mu = jnp.mean(s, axis=-1, keepdims=True)
        xc = s - mu
        var = jnp.mean(xc * xc, axis=-1, keepdims=True)
        h = xc * lax.rsqrt(var + LN_EPS) * g_ref[...] + b_ref[...]
        s_ref[...] = s
        h_ref[...] = h
        hb_ref[...] = h.astype(BF16)

    row = pl.BlockSpec((tm, d), lambda i: (i, 0))
    vec = pl.BlockSpec((1, d), lambda i: (0, 0))
    return pl.pallas_call(
        body, name=name, grid=(t // tm,),
        in_specs=[pl.BlockSpec((nj, tm, kj), lambda i: (0, i, 0)),
                  pl.BlockSpec((N_DEV, None, r, d), lambda i: (0, layer, 0, 0)), row, vec, vec],
        out_specs=[row, row, row],
        out_shape=[jax.ShapeDtypeStruct((t, d), F32), jax.ShapeDtypeStruct((t, d), F32),
                   jax.ShapeDtypeStruct((t, d), BF16)],
        compiler_params=_params("parallel"),
    )(y3, g_out, hin, g, b)


def _out_bwd(ds_bf, g_out, layer, *, tm, name):
    t, d = ds_bf.shape
    r = g_out.shape[2]

    def body(a_ref, w_ref, o_ref):
        o_ref[...] = _dot_nt(a_ref[...], w_ref[...].reshape(N_DEV * r, d))

    return pl.pallas_call(
        body, name=name, grid=(t // tm,),
        in_specs=[pl.BlockSpec((tm, d), lambda i: (i, 0)),
                  pl.BlockSpec((N_DEV, None, r, d), lambda i: (0, layer, 0, 0))],
        out_specs=pl.BlockSpec((tm, N_DEV * r), lambda i: (i, 0)),
        out_shape=jax.ShapeDtypeStruct((t, N_DEV * r), F32),
        compiler_params=_params("parallel"),
    )(ds_bf, g_out)


def _in_bwd(dact, g_in, layer, add, *, tm, name, alpha=DN_ALPHA):
    t = dact.shape[1]
    _, _, k, n = g_in.shape
    halves = dact.shape[0] == 2
    per = N_DEV // 2

    def body(a_ref, b_ref, add_ref, o_ref, acc_ref):
        j = pl.program_id(1)

        @pl.when(j == 0)
        def _():
            acc_ref[...] = alpha * add_ref[...]

        acc_ref[...] += _dot_nt(a_ref[...], b_ref[...])

        @pl.when(j == N_DEV - 1)
        def _():
            o_ref[...] = acc_ref[...]

    if halves:
        a_spec = pl.BlockSpec((None, tm, n), lambda i, j: (j // per, i, j % per))
    else:
        a_spec = pl.BlockSpec((None, tm, n), lambda i, j: (j, i, 0))
    return pl.pallas_call(
        body, name=name, grid=(t // tm, N_DEV),
        in_specs=[a_spec, pl.BlockSpec((None, None, k, n), lambda i, j: (j, layer, 0, 0)),
                  pl.BlockSpec((tm, k), lambda i, j: (i, 0))],
        out_specs=pl.BlockSpec((tm, k), lambda i, j: (i, 0)),
        out_shape=jax.ShapeDtypeStruct((t, k), F32),
        scratch_shapes=[pltpu.VMEM((tm, k), F32)],
        compiler_params=_params("parallel", "arbitrary"),
    )(dact, g_in, add)


def _mm_nt_full(a, b, add, *, tm, name):
    t, n = a.shape
    k = b.shape[0]

    def body(a_ref, b_ref, add_ref, o_ref):
        o_ref[...] = add_ref[...] + _dot_nt(a_ref[...], b_ref[...])

    return pl.pallas_call(
        body, name=name, grid=(t // tm,),
        in_specs=[pl.BlockSpec((tm, n), lambda i: (i, 0)), pl.BlockSpec((k, n), lambda i: (0, 0)),
                  pl.BlockSpec((tm, k), lambda i: (i, 0))],
        out_specs=pl.BlockSpec((tm, k), lambda i: (i, 0)),
        out_shape=jax.ShapeDtypeStruct((t, k), F32),
        compiler_params=_params("parallel"),
    )(a, b, add)


def _w_in_grad(h_bf, dact, *, name):
    t, k = h_bf.shape
    halves = dact.shape[0] == 2
    per = N_DEV // 2
    n = dact.shape[2] // per if halves else dact.shape[2]

    def body(a_ref, b_ref, o_ref):
        o_ref[...] = _dot_tn(a_ref[...], b_ref[...]).astype(BF16)

    if halves:
        b_spec = pl.BlockSpec((None, t, n), lambda j: (j // per, 0, j % per))
    else:
        b_spec = pl.BlockSpec((None, t, n), lambda j: (j, 0, 0))
    return pl.pallas_call(
        body, name=name, grid=(N_DEV,),
        in_specs=[pl.BlockSpec((t, k), lambda j: (0, 0)), b_spec],
        out_specs=pl.BlockSpec((None, k, n), lambda j: (j, 0, 0)),
        out_shape=jax.ShapeDtypeStruct((N_DEV, k, n), BF16),
        compiler_params=_params("parallel"),
    )(h_bf, dact)


def _w_out_grad(y3, ds_bf, r, *, name):
    nj, t, kj = y3.shape
    d = ds_bf.shape[1]
    unit = r * LANES // math.gcd(r, LANES)
    ks = max([c for c in range(unit, min(kj, 768) + 1, unit) if kj % c == 0], default=kj)
    gsz = ks // r
    per = kj // ks

    def body(a_ref, b_ref, o_ref):
        o_ref[...] = _dot_tn(a_ref[...], b_ref[...]).reshape(gsz, r, d).astype(BF16)

    return pl.pallas_call(
        body, name=name, grid=(nj * per,),
        in_specs=[pl.BlockSpec((None, t, ks), lambda j: (j // per, 0, j % per)),
                  pl.BlockSpec((t, d), lambda j: (0, 0))],
        out_specs=pl.BlockSpec((gsz, r, d), lambda j: (j, 0, 0)),
        out_shape=jax.ShapeDtypeStruct((N_DEV, r, d), BF16),
        compiler_params=_params("parallel"),
    )(y3, ds_bf)


def _mm_tn_cols(a, b, *, tn, name):
    t, m = a.shape
    n = b.shape[1]

    def body(a_ref, b_ref, o_ref):
        o_ref[...] = _dot_tn(a_ref[...], b_ref[...])

    return pl.pallas_call(
        body, name=name, grid=(n // tn,),
        in_specs=[pl.BlockSpec((t, m), lambda j: (0, 0)), pl.BlockSpec((t, tn), lambda j: (0, j))],
        out_specs=pl.BlockSpec((m, tn), lambda j: (0, j)),
        out_shape=jax.ShapeDtypeStruct((m, n), F32),
        compiler_params=_params("parallel"),
    )(a, b)


def _ln_bwd(dout, s, g, *, tm, name):
    t, d = s.shape

    def body(do_ref, s_ref, g_ref, ds_ref, dsb_ref, gb_ref):
        i = pl.program_id(0)
        sv = s_ref[...]
        do = do_ref[...]
        mu = jnp.mean(sv, axis=-1, keepdims=True)
        xc = sv - mu
        var = jnp.mean(xc * xc, axis=-1, keepdims=True)
        rstd = lax.rsqrt(var + LN_EPS)
        xhat = xc * rstd
        dxhat = do * g_ref[...]
        m1 = jnp.mean(dxhat, axis=-1, keepdims=True)
        m2 = jnp.mean(dxhat * xhat, axis=-1, keepdims=True)
        ds = rstd * (dxhat - m1 - xhat * m2)
        ds_ref[...] = ds
        dsb_ref[...] = ds.astype(BF16)
        upd = _rows8([jnp.sum(do * xhat, axis=0, keepdims=True), jnp.sum(do, axis=0, keepdims=True)], d)

        @pl.when(i == 0)
        def _():
            gb_ref[...] = upd

        @pl.when(i > 0)
        def _():
            gb_ref[...] += upd

    row = pl.BlockSpec((tm, d), lambda i: (i, 0))
    return pl.pallas_call(
        body, name=name, grid=(t // tm,),
        in_specs=[row, row, pl.BlockSpec((1, d), lambda i: (0, 0))],
        out_specs=[row, row, pl.BlockSpec((8, d), lambda i: (0, 0))],
        out_shape=[jax.ShapeDtypeStruct((t, d), F32), jax.ShapeDtypeStruct((t, d), BF16),
                   jax.ShapeDtypeStruct((8, d), F32)],
        compiler_params=_params("arbitrary"),
    )(dout, s, g)


def _conv_taps(x, wb, width):
    y = jnp.broadcast_to(wb[width:width + 1, :], x.shape)
    for k in range(width):
        y = y + _shift_down(x, width - 1 - k) * wb[k:k + 1, :]
    return y


def _conv_taps_bwd(dy, x, wb, width):
    dx = jnp.zeros_like(dy)
    rows = []
    for k in range(width):
        s = width - 1 - k
        dx = dx + _shift_up(dy, s) * wb[k:k + 1, :]
        rows.append(jnp.sum(dy * _shift_down(x, s), axis=0, keepdims=True))
    rows.append(jnp.sum(dy, axis=0, keepdims=True))
    return dx, _rows8(rows, dy.shape[1])


def _convglu_fwd(z3, fwb3, *, name):
    _, t, n = z3.shape
    half = N_DEV // 2
    nc = pl.cdiv(n, LANES)

    def body(zg_ref, zv_ref, wg_ref, wv_ref, y_ref):
        gate = _conv_taps(zg_ref[...], wg_ref[...], 3)
        val = _conv_taps(zv_ref[...], wv_ref[...], 3)
        y_ref[...] = (_gelu(gate) * val).astype(BF16)

    zblk = lambda off: pl.BlockSpec((None, t, LANES), lambda j, c: (j + off, 0, c))
    wblk = lambda off: pl.BlockSpec((None, 8, LANES), lambda j, c: (j + off, 0, c))
    return pl.pallas_call(
        body, name=name, grid=(half, nc),
        in_specs=[zblk(0), zblk(half), wblk(0), wblk(half)],
        out_specs=zblk(0),
        out_shape=jax.ShapeDtypeStruct((half, t, n), BF16),
        compiler_params=_params("parallel", "parallel"),
    )(z3, z3, fwb3, fwb3)


def _ffn_bwd_mid(ds_bf, g_out, layer, z3, fwb3, *, name):
    t, d = ds_bf.shape
    r = g_out.shape[2]
    n = z3.shape[2]
    half = N_DEV // 2
    nc = pl.cdiv(n, LANES)
    assert n == 2 * r

    def body(ds_ref, w_ref, zg_ref, zv_ref, wg_ref, wv_ref, dz_ref, dwb_ref, wsc_ref):
        c = pl.program_id(1)

        @pl.when(c == 0)
        def _():
            wsc_ref[0:r, :] = w_ref[0]
            wsc_ref[r:2 * r, :] = w_ref[1]
            if nc * LANES > n:
                wsc_ref[n:nc * LANES, :] = jnp.zeros((nc * LANES - n, d), BF16)

        w = wsc_ref[pl.ds(pl.multiple_of(c * LANES, LANES), LANES), :]
        dyf = _dot_nt(ds_ref[...], w)
        zg, zv = zg_ref[...], zv_ref[...]
        wg, wv = wg_ref[...], wv_ref[...]
        gate = _conv_taps(zg, wg, 3)
        val = _conv_taps(zv, wv, 3)
        gl, dgl = _gelu_and_grad(gate)
        dzg, dwg = _conv_taps_bwd(dyf * val * dgl, zg, wg, 3)
        dzv, dwv = _conv_taps_bwd(dyf * gl, zv, wv, 3)
        dz_ref[0] = dzg.astype(BF16)
        dz_ref[1] = dzv.astype(BF16)
        dwb_ref[0] = dwg
        dwb_ref[1] = dwv

    zblk = lambda off: pl.BlockSpec((None, t, LANES), lambda j, c: (j + off, 0, c))
    wblk = lambda off: pl.BlockSpec((None, 8, LANES), lambda j, c: (j + off, 0, c))
    return pl.pallas_call(
        body, name=name, grid=(half, nc),
        in_specs=[pl.BlockSpec((t, d), lambda j, c: (0, 0)),
                  pl.BlockSpec((2, None, r, d), lambda j, c: (j, layer, 0, 0)),
                  zblk(0), zblk(half), wblk(0), wblk(half)],
        out_specs=[pl.BlockSpec((2, None, t, LANES), lambda j, c: (0, j, 0, c)),
                   pl.BlockSpec((2, None, 8, LANES), lambda j, c: (0, j, 0, c))],
        out_shape=[jax.ShapeDtypeStruct((2, half, t, n), BF16), jax.ShapeDtypeStruct((2, half, 8, n), F32)],
        scratch_shapes=[pltpu.VMEM((nc * LANES, d), BF16)],
        compiler_params=_params("parallel", "arbitrary"),
    )(ds_bf, g_out, z3, z3, fwb3, fwb3)


def _conv_a_fwd(gr, cwb, *, cb, name):
    t, r2 = gr.shape
    r = r2 // 2
    nb = r // cb

    def body(x_ref, w_ref, o_ref):
        o_ref[...] = _conv_taps(x_ref[...], w_ref[...], 4)

    return pl.pallas_call(
        body, name=name, grid=(nb,),
        in_specs=[pl.BlockSpec((t, cb), lambda j: (0, j + nb)), pl.BlockSpec((8, cb), lambda j: (0, j))],
        out_specs=pl.BlockSpec((t, cb), lambda j: (0, j)),
        out_shape=jax.ShapeDtypeStruct((t, r), F32),
        compiler_params=_params("parallel"),
    )(gr, cwb)


def _gates_fwd(rec, bd_r, bd_i, vecs, *, tm, name):
    t, r_dim = rec.shape
    nb, cb, _ = bd_r.shape

    def body(x_ref, wr_ref, wi_ref, v_ref, a_ref, u_ref, r_ref, i_ref):
        x = x_ref[...]
        xb = x.astype(BF16)
        v = v_ref[...]
        r = _sigmoid(_dot_nn(xb, wr_ref[...]) + v[0:1, :])
        i = _sigmoid(_dot_nn(xb, wi_ref[...]) + v[1:2, :])
        log_a = (-LRU_C) * r * _softplus(-v[2:3, :])
        a_ref[...] = jnp.exp(log_a)
        u_ref[...] = jnp.sqrt(-_expm1(2.0 * log_a)) * (i * x)
        r_ref[...] = r
        i_ref[...] = i

    blk = pl.BlockSpec((tm, cb), lambda j, i: (i, j))
    wspec = pl.BlockSpec((None, cb, cb), lambda j, i: (j, 0, 0))
    out = jax.ShapeDtypeStruct((t, r_dim), F32)
    return pl.pallas_call(
        body, name=name, grid=(nb, t // tm),
        in_specs=[blk, wspec, wspec, pl.BlockSpec((8, cb), lambda j, i: (0, j))],
        out_specs=[blk, blk, blk, blk],
        out_shape=[out, out, out, out],
        compiler_params=_params("parallel", "parallel"),
    )(rec, bd_r, bd_i, vecs)


def _scan_fwd(a, u, gr, *, cb, name):
    t, r = a.shape
    nb = r // cb
    seg = t // SUBLANES

    def body(a_ref, u_ref, g_ref, h_ref, y_ref, p_ref):
        def step(k, carry):
            h, p = carry
            rows = pl.ds(k, SUBLANES, stride=seg)
            av = a_ref[rows, :]
            h = av * h + u_ref[rows, :]
            p = av * p
            h_ref[rows, :] = h
            p_ref[rows, :] = p
            return h, p

        h_fin, p_fin = lax.fori_loop(0, seg, step, (jnp.zeros((SUBLANES, cb), F32), jnp.ones((SUBLANES, cb), F32)),
                                     unroll=4)
        carry = h_fin[0:1, :]
        for s in range(1, SUBLANES):
            rows = slice(s * seg, (s + 1) * seg)
            h_ref[rows, :] = h_ref[rows, :] + p_ref[rows, :] * carry
            carry = h_fin[s:s + 1, :] + p_fin[s:s + 1, :] * carry
        y_ref[...] = (_gelu(g_ref[...]) * h_ref[...]).astype(BF16)

    blk = pl.BlockSpec((t, cb), lambda j: (0, j))
    return pl.pallas_call(
        body, name=name, grid=(nb,),
        in_specs=[blk, blk, blk],
        out_specs=[blk, pl.BlockSpec((None, t, cb), lambda j: (0, 0, j))],
        out_shape=[jax.ShapeDtypeStruct((t, r), F32), jax.ShapeDtypeStruct((1, t, r), BF16)],
        scratch_shapes=[pltpu.VMEM((t, cb), F32)],
        compiler_params=_params("parallel"),
    )(a, u, gr)


def _scan_bwd(dy, gr, hr, a, *, cb, name):
    t, r = a.shape
    nb = r // cb
    seg = t // SUBLANES

    def body(dy_ref, g_ref, h_ref, a_ref, dh_ref, da_ref, dg_ref, q_ref):
        gl, dgl = _gelu_and_grad(g_ref[...])
        dyv = dy_ref[...]
        dh_ref[...] = dyv * gl
        dg_ref[...] = (dyv * h_ref[...] * dgl).astype(BF16)

        def step(k, carry):
            cin, q = carry
            rows = pl.ds(seg - 1 - k, SUBLANES, stride=seg)
            dh = dh_ref[rows, :] + cin
            dh_ref[rows, :] = dh
            q_ref[rows, :] = q
            av = a_ref[rows, :]
            return av * dh, av * q

        c_fin, q_fin = lax.fori_loop(0, seg, step, (jnp.zeros((SUBLANES, cb), F32), jnp.ones((SUBLANES, cb), F32)),
                                     unroll=4)
        carry = c_fin[SUBLANES - 1:SUBLANES, :]
        for s in range(SUBLANES - 2, -1, -1):
            rows = slice(s * seg, (s + 1) * seg)
            dh_ref[rows, :] = dh_ref[rows, :] + q_ref[rows, :] * carry
            carry = c_fin[s:s + 1, :] + q_fin[s:s + 1, :] * carry
        da_ref[...] = dh_ref[...] * _shift_down(h_ref[...], 1)

    blk = pl.BlockSpec((t, cb), lambda j: (0, j))
    return pl.pallas_call(
        body, name=name, grid=(nb,),
        in_specs=[blk, blk, blk, blk],
        out_specs=[blk, blk, blk],
        out_shape=[jax.ShapeDtypeStruct((t, r), F32), jax.ShapeDtypeStruct((t, r), F32),
                   jax.ShapeDtypeStruct((t, r), BF16)],
        scratch_shapes=[pltpu.VMEM((t, cb), F32)],
        compiler_params=_params("parallel"),
    )(dy, gr, hr, a)


def _gates_bwd(rec, r, i, a, dh, da, bd_r, bd_i, vecs, *, tm, name):
    t, r_dim = rec.shape
    nb, cb, _ = bd_r.shape

    def body(x_ref, r_ref, i_ref, a_ref, dh_ref, da_ref, wr_ref, wi_ref, v_ref, dx_ref, dpr_ref, dpi_ref, dv_ref):
        step = pl.program_id(1)
        x, r, i, a, dh, da = x_ref[...], r_ref[...], i_ref[...], a_ref[...], dh_ref[...], da_ref[...]
        lam = v_ref[...][2:3, :]
        sp = _softplus(-lam)
        a2 = a * a
        mult = jnp.sqrt(-_expm1(2.0 * (-LRU_C) * r * sp))
        d_i = dh * mult * x
        d_log_a = da * a - (dh * i * x) * a2 / mult
        d_r = d_log_a * ((-LRU_C) * sp)
        d_sp = jnp.sum(d_log_a * ((-LRU_C) * r), axis=0, keepdims=True)
        d_pre_r = d_r * r * (1.0 - r)
        d_pre_i = d_i * i * (1.0 - i)
        dprb = d_pre_r.astype(BF16)
        dpib = d_pre_i.astype(BF16)
        dx_ref[...] = dh * mult * i + _dot_nt(dprb, wr_ref[...]) + _dot_nt(dpib, wi_ref[...])
        dpr_ref[...] = dprb
        dpi_ref[...] = dpib
        upd = _rows8([jnp.sum(d_pre_r, axis=0, keepdims=True), jnp.sum(d_pre_i, axis=0, keepdims=True),
                      -d_sp * _sigmoid(-lam)], cb)

        @pl.when(step == 0)
        def _():
            dv_ref[...] = upd

        @pl.when(step > 0)
        def _():
            dv_ref[...] += upd

    blk = pl.BlockSpec((tm, cb), lambda j, i: (i, j))
    wspec = pl.BlockSpec((None, cb, cb), lambda j, i: (j, 0, 0))
    vspec = pl.BlockSpec((8, cb), lambda j, i: (0, j))
    return pl.pallas_call(
        body, name=name, grid=(nb, t // tm),
        in_specs=[blk] * 6 + [wspec, wspec, vspec],
        out_specs=[blk, blk, blk, vspec],
        out_shape=[jax.ShapeDtypeStruct((t, r_dim), F32), jax.ShapeDtypeStruct((t, r_dim), BF16),
                   jax.ShapeDtypeStruct((t, r_dim), BF16), jax.ShapeDtypeStruct((8, r_dim), F32)],
        compiler_params=_params("parallel", "arbitrary"),
    )(rec, r, i, a, dh, da, bd_r, bd_i, vecs)


def _bd_grad(rec, dpr, dpi, *, cb, name):
    t, r = rec.shape
    nb = r // cb

    def body(x_ref, dr_ref, di_ref, gr_ref, gi_ref):
        xb = x_ref[...].astype(BF16)
        gr_ref[...] = _dot_tn(xb, dr_ref[...])
        gi_ref[...] = _dot_tn(xb, di_ref[...])

    blk = pl.BlockSpec((t, cb), lambda j: (0, j))
    wspec = pl.BlockSpec((None, cb, cb), lambda j: (j, 0, 0))
    out = jax.ShapeDtypeStruct((nb, cb, cb), F32)
    return pl.pallas_call(
        body, name=name, grid=(nb,),
        in_specs=[blk, blk, blk], out_specs=[wspec, wspec], out_shape=[out, out],
        compiler_params=_params("parallel"),
    )(rec, dpr, dpi)


def _conv_a_bwd(d_rec, gr, dgate, cwb, *, cb, name):
    t, r = d_rec.shape
    nb = r // cb

    def body(dy_ref, x_ref, dg_ref, w_ref, dact_ref, dw_ref):
        dx, dw = _conv_taps_bwd(dy_ref[...], x_ref[...], w_ref[...], 4)
        dact_ref[0] = dg_ref[...]
        dact_ref[1] = dx.astype(BF16)
        dw_ref[...] = dw

    blk = pl.BlockSpec((t, cb), lambda j: (0, j))
    vspec = pl.BlockSpec((8, cb), lambda j: (0, j))
    return pl.pallas_call(
        body, name=name, grid=(nb,),
        in_specs=[blk, pl.BlockSpec((t, cb), lambda j: (0, j + nb)), blk, vspec],
        out_specs=[pl.BlockSpec((2, t, cb), lambda j: (0, 0, j)), vspec],
        out_shape=[jax.ShapeDtypeStruct((2, t, r), BF16), jax.ShapeDtypeStruct((8, r), F32)],
        compiler_params=_params("parallel"),
    )(d_rec, gr, dgate, cwb)


def _split3(x):
    p0 = x.astype(BF16)
    r1 = x - p0.astype(F32)
    p1 = r1.astype(BF16)
    p2 = (r1 - p1.astype(F32)).astype(BF16)
    return p0, p1, p2


def _fgate_fwd(fp, fb, *, tq, name):
    t = fp.shape[0]

    def body(f_ref, b_ref, c_ref, ct_ref):
        logf = -_softplus(-(f_ref[...] + b_ref[...]))
        rows = pl.program_id(0) * tq + lax.broadcasted_iota(jnp.int32, (tq, t), 0)
        cols = lax.broadcasted_iota(jnp.int32, (tq, t), 1)
        tri = (cols <= rows).astype(BF16)
        p0, p1, p2 = _split3(logf)
        c = _dot_nn(tri, p0) + _dot_nn(tri, p1) + _dot_nn(tri, p2)
        c_ref[...] = c
        ct_ref[...] = c.T

    return pl.pallas_call(
        body, name=name, grid=(t // tq,),
        in_specs=[pl.BlockSpec((t, LANES), lambda i: (0, 0)), pl.BlockSpec((1, LANES), lambda i: (0, 0))],
        out_specs=[pl.BlockSpec((tq, LANES), lambda i: (i, 0)), pl.BlockSpec((LANES, tq), lambda i: (0, i))],
        out_shape=[jax.ShapeDtypeStruct((t, LANES), F32), jax.ShapeDtypeStruct((LANES, t), F32)],
        compiler_params=_params("parallel"),
    )(fp, fb)


def _fgate_bwd(dct, fp, fb, *, tq, name):
    t = fp.shape[0]

    def body(d_ref, f_ref, b_ref, o_ref, db_ref):
        i = pl.program_id(0)
        rows = lax.broadcasted_iota(jnp.int32, (t, tq), 0)
        cols = i * tq + lax.broadcasted_iota(jnp.int32, (t, tq), 1)
        tri = (rows >= cols).astype(BF16)
        p0, p1, p2 = _split3(d_ref[...])
        dlogf = (_dot_nn(p0, tri) + _dot_nn(p1, tri) + _dot_nn(p2, tri)).T
        df = dlogf * _sigmoid(-(f_ref[...] + b_ref[...]))
        o_ref[...] = df.astype(BF16)
        upd = _rows8([jnp.sum(df, axis=0, keepdims=True)], LANES)

        @pl.when(i == 0)
        def _():
            db_ref[...] = upd

        @pl.when(i > 0)
        def _():
            db_ref[...] += upd

    return pl.pallas_call(
        body, name=name, grid=(t // tq,),
        in_specs=[pl.BlockSpec((LANES, t), lambda i: (0, 0)), pl.BlockSpec((tq, LANES), lambda i: (i, 0)),
                  pl.BlockSpec((1, LANES), lambda i: (0, 0))],
        out_specs=[pl.BlockSpec((tq, LANES), lambda i: (i, 0)), pl.BlockSpec((8, LANES), lambda i: (0, 0))],
        out_shape=[jax.ShapeDtypeStruct((t, LANES), BF16), jax.ShapeDtypeStruct((8, LANES), F32)],
        compiler_params=_params("arbitrary"),
    )(dct, fp, fb)


def _pair_sum(a, b, *, tm, name):
    t, d = a.shape

    def body(a_ref, b_ref, o_ref):
        o_ref[...] = (a_ref[...] + b_ref[...]).astype(BF16)

    row = pl.BlockSpec((tm, d), lambda i: (i, 0))
    return pl.pallas_call(
        body, name=name, grid=(t // tm,), in_specs=[row, row], out_specs=row,
        out_shape=jax.ShapeDtypeStruct((t, d), BF16), compiler_params=_params("parallel"),
    )(a, b)


def _head_masks(dh):
    lane = lax.broadcasted_iota(jnp.int32, (1, LANES), 1)
    return [((lane >= e * dh) & (lane < (e + 1) * dh)) for e in range(LANES // dh)]


def _head_c(c_blk, ct_blk, head):
    lane = lax.broadcasted_iota(jnp.int32, c_blk.shape, 1)
    c_col = jnp.sum(jnp.where(lane == head, c_blk, 0.0), axis=1, keepdims=True)
    sub = lax.broadcasted_iota(jnp.int32, ct_blk.shape, 0)
    c_row = jnp.sum(jnp.where(sub == head, ct_blk, 0.0), axis=0, keepdims=True)
    return c_col, c_row


def _attn_probs(qm, k, c_col, c_row, q0, scale):
    tq, t = qm.shape[0], k.shape[0]
    s = _dot_nt(qm, k) * scale + c_col - c_row
    qi = q0 + lax.broadcasted_iota(jnp.int32, (tq, t), 0)
    ki = lax.broadcasted_iota(jnp.int32, (tq, t), 1)
    s = jnp.where(ki <= qi, s, -jnp.inf)
    m = jnp.max(s, axis=-1, keepdims=True)
    p = jnp.exp(s - m)
    return p / jnp.sum(p, axis=-1, keepdims=True)


def _key_buckets(t, tq):
    step = 3 * tq
    return tuple(range(step, t, step)) + (t,)


def _for_prefix(needed, buckets, fn):
    prev = 0
    for length in buckets:
        pl.when((needed > prev) & (needed <= length))(lambda length=length: fn(length))
        prev = length


def _attn_fwd(qg, kv, c, ct, *, tq, name):
    t, d2 = qg.shape
    d = d2 // 2
    dh = d // N_HEADS
    hpb = LANES // dh
    nhb = d // LANES
    scale = dh ** -0.5
    buckets = _key_buckets(t, tq)

    def body(q_ref, og_ref, k_ref, v_ref, c_ref, ct_ref, o_ref, y_ref):
        hb = pl.program_id(0)
        q0 = pl.program_id(1) * tq

        def run(length):
            q = q_ref[...]
            k = k_ref[0:length, :]
            v = v_ref[0:length, :]
            o = jnp.zeros((tq, LANES), F32)
            for e, msk in enumerate(_head_masks(dh)):
                c_col, c_row = _head_c(c_ref[...], ct_ref[:, 0:length], hb * hpb + e)
                p = _attn_probs(jnp.where(msk, q, 0.0).astype(BF16), k, c_col, c_row, q0, scale)
                o = o + _dot_nn(p.astype(BF16), jnp.where(msk, v, jnp.zeros_like(v)))
            o_ref[...] = o
            y_ref[...] = (o * _sigmoid(og_ref[...])).astype(BF16)

        _for_prefix(q0 + tq, buckets, run)

    qblk = pl.BlockSpec((tq, LANES), lambda h, i: (i, h))
    return pl.pallas_call(
        body, name=name, grid=(nhb, t // tq),
        in_specs=[qblk, pl.BlockSpec((tq, LANES), lambda h, i: (i, h + nhb)),
                  pl.BlockSpec((t, LANES), lambda h, i: (0, h)), pl.BlockSpec((t, LANES), lambda h, i: (0, h + nhb)),
                  pl.BlockSpec((tq, LANES), lambda h, i: (i, 0)), pl.BlockSpec((N_HEADS, t), lambda h, i: (0, 0))],
        out_specs=[qblk, pl.BlockSpec((None, tq, LANES), lambda h, i: (0, i, h))],
        out_shape=[jax.ShapeDtypeStruct((t, d), F32), jax.ShapeDtypeStruct((1, t, d), BF16)],
        compiler_params=_params("parallel", "parallel"),
    )(qg, qg, kv, kv, c, ct)


def _attn_bwd(dy, qg, o, kv, c, ct, *, tq, name):
    t, d2 = qg.shape
    d = d2 // 2
    dh = d // N_HEADS
    hpb = LANES // dh
    nhb = d // LANES
    scale = dh ** -0.5
    buckets = _key_buckets(t, tq)

    def body(dy_ref, q_ref, og_ref, o_ref, k_ref, v_ref, c_ref, ct_ref, dqg_ref, dk_ref, dv_ref, dc_ref):
        hb = pl.program_id(0)
        step = pl.program_id(1)
        q0 = step * tq

        @pl.when(step == 0)
        def _():
            dk_ref[...] = jnp.zeros((t, LANES), F32)
            dv_ref[...] = jnp.zeros((t, LANES), F32)
            dc_ref[...] = jnp.zeros((8, t), F32)

        def run(length):
            q = q_ref[...]
            k = k_ref[0:length, :]
            v = v_ref[0:length, :]
            sg = _sigmoid(og_ref[...])
            dyv = dy_ref[...]
            do = dyv * sg
            dqg_ref[1] = (dyv * o_ref[...] * sg * (1.0 - sg)).astype(BF16)
            dq = jnp.zeros((tq, LANES), F32)
            dk = jnp.zeros((length, LANES), F32)
            dv = jnp.zeros((length, LANES), F32)
            dc_rows = []
            for e, msk in enumerate(_head_masks(dh)):
                c_col, c_row = _head_c(c_ref[...], ct_ref[:, 0:length], hb * hpb + e)
                qm = jnp.where(msk, q, 0.0).astype(BF16)
                dom = jnp.where(msk, do, 0.0).astype(BF16)
                p = _attn_probs(qm, k, c_col, c_row, q0, scale)
                dp = _dot_nt(dom, v)
                dsc = p * (dp - jnp.sum(p * dp, axis=-1, keepdims=True))
                dsb = (dsc * scale).astype(BF16)
                dq = dq + _dot_nn(dsb, jnp.where(msk, k, jnp.zeros_like(k)))
                dk = dk + _dot_tn(dsb, qm)
                dv = dv + _dot_tn(p.astype(BF16), dom)
                dc_rows.append(-jnp.sum(dsc, axis=0, keepdims=True))
            dqg_ref[0] = dq.astype(BF16)
            dk_ref[0:length, :] += dk
            dv_ref[0:length, :] += dv
            dc_ref[:, 0:length] += _rows8(dc_rows, length)

        _for_prefix(q0 + tq, buckets, run)

    qblk = pl.BlockSpec((tq, LANES), lambda h, i: (i, h))
    kblk = pl.BlockSpec((t, LANES), lambda h, i: (0, h))
    return pl.pallas_call(
        body, name=name, grid=(nhb, t // tq),
        in_specs=[qblk, qblk, pl.BlockSpec((tq, LANES), lambda h, i: (i, h + nhb)), qblk,
                  kblk, pl.BlockSpec((t, LANES), lambda h, i: (0, h + nhb)),
                  pl.BlockSpec((tq, LANES), lambda h, i: (i, 0)), pl.BlockSpec((N_HEADS, t), lambda h, i: (0, 0))],
        out_specs=[pl.BlockSpec((2, tq, LANES), lambda h, i: (0, i, h)), kblk, kblk,
                   pl.BlockSpec((None, 8, t), lambda h, i: (h, 0, 0))],
        out_shape=[jax.ShapeDtypeStruct((2, t, d), BF16), jax.ShapeDtypeStruct((t, d), F32),
                   jax.ShapeDtypeStruct((t, d), F32), jax.ShapeDtypeStruct((nhb, 8, t), F32)],
        compiler_params=_params("parallel", "arbitrary"),
    )(dy, qg, qg, o, kv, kv, c, ct)


def _loss_bwd(h, tgt, *, lo, hi, tm, name):
    t, d = h.shape

    def body(h_ref, t_ref, l_ref, dy_ref):
        i = pl.program_id(0)
        rows = i * tm + lax.broadcasted_iota(jnp.int32, (tm, d), 0)
        err = jnp.where((rows >= lo) & (rows < hi), h_ref[...] - t_ref[...], 0.0)
        dy_ref[...] = err * (1.0 / d)
        part = jnp.sum(jnp.sum(err * err, axis=0, keepdims=True), axis=1, keepdims=True) * (0.5 / d)
        upd = jnp.broadcast_to(part, (8, LANES))

        @pl.when(i == 0)
        def _():
            l_ref[...] = upd

        @pl.when(i > 0)
        def _():
            l_ref[...] += upd

    row = pl.BlockSpec((tm, d), lambda i: (i, 0))
    return pl.pallas_call(
        body, name=name, grid=(t // tm,),
        in_specs=[row, row],
        out_specs=[pl.BlockSpec((8, LANES), lambda i: (0, 0)), row],
        out_shape=[jax.ShapeDtypeStruct((8, LANES), F32), jax.ShapeDtypeStruct((t, d), F32)],
        compiler_params=_params("arbitrary"),
    )(h, tgt)


def _adamw_math(w, gv, m, v):
    bc1 = 1.0 / (1.0 - ADAM_B1 ** ADAM_STEP)
    bc2 = 1.0 / (1.0 - ADAM_B2 ** ADAM_STEP)
    nm = ADAM_B1 * m + (1.0 - ADAM_B1) * gv
    nv = ADAM_B2 * v + (1.0 - ADAM_B2) * (gv * gv)
    delta = (-ADAM_LR) * ((nm * bc1) / (jnp.sqrt(nv * bc2) + ADAM_EPS) + ADAM_WD * w)
    return delta, nm, nv


def _adamw(w, g, m, v, *, name):
    r, c = w.shape
    tr = r
    for cand in (512, 256, 128, 64, 32, 16, 8):
        if r % cand == 0 and r > cand:
            tr = cand
            break

    def body(w_ref, g_ref, m_ref, v_ref, d_ref, nm_ref, nv_ref):
        d_ref[...], nm_ref[...], nv_ref[...] = _adamw_math(w_ref[...], g_ref[...], m_ref[...], v_ref[...])

    blk = pl.BlockSpec((tr, c), lambda i: (i, 0))
    out = jax.ShapeDtypeStruct((r, c), F32)
    return pl.pallas_call(
        body, name=name, grid=(r // tr,),
        in_specs=[blk] * 4, out_specs=[blk] * 3, out_shape=[out] * 3,
        compiler_params=_params("parallel"),
    )(w, g, m, v)


def _sum_adamw(recvs, w, m, v, *, name):
    n_l = len(recvs)
    _, r, c = recvs[0].shape
    tr = r
    for cand in (256, 192, 176, 128, 96, 64, 48, 32, 16):
        if r % cand == 0:
            tr = cand
            break

    def body(*refs):
        p_refs = refs[:n_l]
        w_ref, m_ref, v_ref, g_ref, d_ref, nm_ref, nv_ref = refs[n_l:]
        layer = pl.program_id(0)
        for k in range(n_l):
            @pl.when(layer == k)
            def _(k=k):
                acc = p_refs[k][0].astype(F32)
                for dev in range(1, N_DEV):
                    acc = acc + p_refs[k][dev].astype(F32)
                g_ref[...] = acc
                d_ref[...], nm_ref[...], nv_ref[...] = _adamw_math(w_ref[...], acc, m_ref[...], v_ref[...])

    p_specs = [pl.BlockSpec((N_DEV, tr, c), lambda l, i, k=k: (0, jnp.where(l == k, i, 0), 0)) for k in range(n_l)]
    blk = pl.BlockSpec((None, tr, c), lambda l, i: (l, i, 0))
    out = jax.ShapeDtypeStruct((n_l, r, c), F32)
    return pl.pallas_call(
        body, name=name, grid=(n_l, r // tr),
        in_specs=p_specs + [blk] * 3, out_specs=[blk] * 4, out_shape=[out] * 4,
        compiler_params=_params("arbitrary", "arbitrary"),
    )(*recvs, w, m, v)


def _sum8(parts, *, name):
    _, r, c = parts.shape
    tr = r
    for cand in (512, 256, 128, 64, 32, 16):
        if r % cand == 0 and r > cand:
            tr = cand
            break

    def body(p_ref, o_ref):
        acc = p_ref[0].astype(F32)
        for k in range(1, N_DEV):
            acc = acc + p_ref[k].astype(F32)
        o_ref[...] = acc

    return pl.pallas_call(
        body, name=name, grid=(r // tr,),
        in_specs=[pl.BlockSpec((N_DEV, tr, c), lambda i: (0, i, 0))],
        out_specs=pl.BlockSpec((tr, c), lambda i: (i, 0)),
        out_shape=jax.ShapeDtypeStruct((r, c), F32),
        compiler_params=_params("parallel"),
    )(parts)


def _my_index():
    return 4 * lax.axis_index("x") + 2 * lax.axis_index("y") + lax.axis_index("c")


def _peer(k):
    x, y, c = lax.axis_index("x"), lax.axis_index("y"), lax.axis_index("c")
    px = x ^ ((k >> 2) & 1)
    py = y ^ ((k >> 1) & 1)
    pc = c ^ (k & 1)
    return (px, py, pc), 4 * px + 2 * py + pc


def _all_gather(shards, *, name):
    n_arr = len(shards)

    def body(*refs):
        ins, outs = refs[:n_arr], refs[n_arr:2 * n_arr]
        send_sems, recv_sems, local_sems = refs[2 * n_arr:]
        me = _my_index()
        local = [pltpu.make_async_copy(ins[n], outs[n].at[me], local_sems.at[n]) for n in range(n_arr)]
        for cp in local:
            cp.start()
        sends = []
        for k in range(1, N_DEV):
            peer, _ = _peer(k)
            for n in range(n_arr):
                cp = pltpu.make_async_remote_copy(
                    src_ref=ins[n], dst_ref=outs[n].at[me], send_sem=send_sems.at[n, k - 1],
                    recv_sem=recv_sems.at[n, k - 1], device_id=peer, device_id_type=pl.DeviceIdType.MESH)
                cp.start()
                sends.append(cp)
        for k in range(1, N_DEV):
            peer, pidx = _peer(k)
            for n in range(n_arr):
                pltpu.make_async_remote_copy(
                    src_ref=ins[n], dst_ref=outs[n].at[pidx], send_sem=send_sems.at[n, k - 1],
                    recv_sem=recv_sems.at[n, k - 1], device_id=peer, device_id_type=pl.DeviceIdType.MESH).wait_recv()
        for cp in sends:
            cp.wait_send()
        for cp in local:
            cp.wait()

    hbm = pl.BlockSpec(memory_space=pl.ANY)
    return pl.pallas_call(
        body, name=name,
        in_specs=[hbm] * n_arr, out_specs=[hbm] * n_arr,
        out_shape=[jax.ShapeDtypeStruct((N_DEV,) + s.shape, s.dtype) for s in shards],
        scratch_shapes=[pltpu.SemaphoreType.DMA((n_arr, N_DEV - 1)), pltpu.SemaphoreType.DMA((n_arr, N_DEV - 1)),
                        pltpu.SemaphoreType.DMA((n_arr,))],
        compiler_params=pltpu.CompilerParams(has_side_effects=True),
    )(*shards)


def _exchange(parts, *, name):
    n_arr = len(parts)

    def body(*refs):
        ins, outs = refs[:n_arr], refs[n_arr:2 * n_arr]
        send_sems, recv_sems, local_sems = refs[2 * n_arr:]
        me = _my_index()
        local = [pltpu.make_async_copy(ins[n].at[me], outs[n].at[me], local_sems.at[n]) for n in range(n_arr)]
        for cp in local:
            cp.start()
        sends = []
        for k in range(1, N_DEV):
            peer, pidx = _peer(k)
            for n in range(n_arr):
                cp = pltpu.make_async_remote_copy(
                    src_ref=ins[n].at[pidx], dst_ref=outs[n].at[me], send_sem=send_sems.at[n, k - 1],
                    recv_sem=recv_sems.at[n, k - 1], device_id=peer, device_id_type=pl.DeviceIdType.MESH)
                cp.start()
                sends.append(cp)
        for k in range(1, N_DEV):
            peer, pidx = _peer(k)
            for n in range(n_arr):
                pltpu.make_async_remote_copy(
                    src_ref=ins[n].at[me], dst_ref=outs[n].at[pidx], send_sem=send_sems.at[n, k - 1],
                    recv_sem=recv_sems.at[n, k - 1], device_id=peer, device_id_type=pl.DeviceIdType.MESH).wait_recv()
        for cp in sends:
            cp.wait_send()
        for cp in local:
            cp.wait()

    hbm = pl.BlockSpec(memory_space=pl.ANY)
    return pl.pallas_call(
        body, name=name,
        in_specs=[hbm] * n_arr, out_specs=[hbm] * n_arr,
        out_shape=[jax.ShapeDtypeStruct(p.shape, p.dtype) for p in parts],
        scratch_shapes=[pltpu.SemaphoreType.DMA((n_arr, N_DEV - 1)), pltpu.SemaphoreType.DMA((n_arr, N_DEV - 1)),
                        pltpu.SemaphoreType.DMA((n_arr,))],
        compiler_params=pltpu.CompilerParams(has_side_effects=True),
    )(*parts)


def _pack(arrs, dtype):
    flat = jnp.concatenate([a.astype(dtype).reshape(-1) for a in arrs])
    pad = (-flat.shape[0]) % (16 * PACK_COLS)
    if pad:
        flat = jnp.concatenate([flat, jnp.zeros((pad,), dtype)])
    return flat.reshape(-1, PACK_COLS)


def _pack8(arrs, dtype):
    flat = jnp.concatenate([a.astype(dtype).reshape(N_DEV, -1) for a in arrs], axis=1)
    pad = (-flat.shape[1]) % (16 * PACK_COLS)
    if pad:
        flat = jnp.concatenate([flat, jnp.zeros((N_DEV, pad), dtype)], axis=1)
    return flat.reshape(N_DEV, -1, PACK_COLS)


def _unpack(slab, shapes, lead):
    lead_shape = slab.shape[:lead]
    flat = slab.reshape(lead_shape + (-1,))
    outs, off = [], 0
    for shp in shapes:
        size = math.prod(shp)
        outs.append(flat[..., off:off + size].reshape(lead_shape + tuple(shp)))
        off += size
    return outs


def _cols_full(g):
    g = jnp.moveaxis(g, 0, -2)
    return g.reshape(g.shape[:-2] + (g.shape[-2] * g.shape[-1],))


def _cols_split(full):
    n = full.shape[-1] // N_DEV
    return jnp.moveaxis(full.reshape(full.shape[:-1] + (N_DEV, n)), -2, 0)


def _block_diag(w, per):
    n, b, _ = w.shape
    w4 = w.reshape(n // per, per, b, b)
    eye = jnp.eye(per, dtype=w.dtype)
    return jnp.einsum('gpab,pq->gpaqb', w4, eye).reshape(n // per, per * b, per * b)


def _block_diag_extract(g, per):
    gn, cb, _ = g.shape
    b = cb // per
    g5 = g.reshape(gn, per, b, per, b)
    return jnp.stack([g5[:, p, :, p, :] for p in range(per)], axis=1).reshape(gn * per, b, b)


def _slab2d(a):
    if a.size % PACK_COLS == 0:
        return a.reshape(-1, PACK_COLS)
    return a.reshape(-1, a.shape[-1])


def _lru_block_cols(r_dim):
    lru = r_dim // N_LRU_BLOCKS
    return lru * LANES // math.gcd(lru, LANES)


BIG = ("a_w_in", "a_w_out", "b_w_in", "b_w_out", "f_w_in", "f_w_out")
COL_F32 = ("meta", "a_conv_w", "a_conv_b", "a_b_r", "a_b_i", "a_lambda", "f_conv_w")
REPLICATED = ("a_w_r", "a_w_i", "kv_f_b", "f_conv_b", "ln1_g", "ln1_b", "ln2_g", "ln2_b")
WEIGHT_NAMES = ("meta", "a_w_in", "a_conv_w", "a_conv_b", "a_w_r", "a_b_r", "a_w_i", "a_b_i", "a_lambda", "a_w_out",
                "kv_w", "kv_f_b", "b_w_in", "b_w_out", "f_w_in", "f_conv_w", "f_conv_b", "f_w_out",
                "ln1_g", "ln1_b", "ln2_g", "ln2_b")


def _prepare_weights(gathered, small):
    d = small["meta"].shape[1]
    r_dim = small["a_lambda"].shape[1]
    n_f = gathered["f_w_in"].shape[3]
    cb = _lru_block_cols(r_dim)
    per = cb // (r_dim // N_LRU_BLOCKS)
    n_a = small["a_lambda"].shape[0]
    kv_full = _cols_full(gathered["kv_w"])
    kv_pad = 2 * d + LANES - kv_full.shape[1]
    f_conv_w3 = small["f_conv_w"].reshape(N_LAYERS, 3, N_DEV, n_f).transpose(0, 2, 1, 3)
    f_conv_b3 = small["f_conv_b"].reshape(N_LAYERS, N_DEV, 1, n_f)
    return {
        "a_in": gathered["a_w_in"], "a_out": gathered["a_w_out"], "b_in": gathered["b_w_in"],
        "b_out": gathered["b_w_out"], "f_in": gathered["f_w_in"], "f_out": gathered["f_w_out"],
        "kv_w": jnp.concatenate([kv_full, jnp.zeros((d, kv_pad), kv_full.dtype)], axis=1),
        "kv_fb": jnp.concatenate([small["kv_f_b"], jnp.zeros((LANES - N_HEADS,), F32)])[None],
        "a_cwb": jnp.concatenate([small["a_conv_w"], small["a_conv_b"][:, None],
                                  jnp.zeros((n_a, 3, r_dim), F32)], axis=1),
        "a_vecs": jnp.concatenate([jnp.stack([small["a_b_r"], small["a_b_i"], small["a_lambda"]], axis=1),
                                   jnp.zeros((n_a, 5, r_dim), F32)], axis=1),
        "a_bd_r": jnp.stack([_block_diag(small["a_w_r"][l], per) for l in range(n_a)]).astype(BF16),
        "a_bd_i": jnp.stack([_block_diag(small["a_w_i"][l], per) for l in range(n_a)]).astype(BF16),
        "f_cwb3": jnp.concatenate([f_conv_w3, f_conv_b3, jnp.zeros((N_LAYERS, N_DEV, 4, n_f), F32)], axis=2),
        "ln1_g": small["ln1_g"][:, None], "ln1_b": small["ln1_b"][:, None],
        "ln2_g": small["ln2_g"][:, None], "ln2_b": small["ln2_b"][:, None],
    }


def _local_step(h0, tgt, n_meta, n_tok, wts):
    tp, d = h0.shape
    tm = tp // 8 if (tp // 8) % 16 == 0 else tp
    tmb = _tile(tp, (1088, 512, 320, 256, 128))
    tq = 128
    r_dim = wts["a_vecs"].shape[2]
    cb = wts["a_bd_r"].shape[-1]
    sb = LANES
    n_b = N_LAYERS - N_A_LAYERS

    h, h_bf = h0, h0.astype(BF16)
    saved = []
    kvs = None
    for layer in range(N_LAYERS):
        sv = {"h_bf": h_bf}
        if layer < N_A_LAYERS:
            sv["gr"] = _proj_in(h_bf, wts["a_in"], layer, shard_major=False, name="a_in_proj")
            sv["rec"] = _conv_a_fwd(sv["gr"], wts["a_cwb"][layer], cb=cb, name="a_conv_fwd")
            a, u, sv["r"], sv["i"] = _gates_fwd(sv["rec"], wts["a_bd_r"][layer], wts["a_bd_i"][layer],
                                                wts["a_vecs"][layer], tm=tm, name="a_gates_fwd")
            sv["a"] = a
            sv["hr"], y3 = _scan_fwd(a, u, sv["gr"], cb=sb, name="a_scan_fwd")
            g_out, lidx = wts["a_out"], layer
        else:
            j = layer - N_A_LAYERS
            if j == 0:
                kvs = {"h_bf": h_bf}
                kvs["kv"] = _mm_nn(h_bf, wts["kv_w"][:, :2 * d], tn=_tile(2 * d, (512, 256, 128)), out_dtype=BF16,
                                   name="kv_proj")
                kvs["fp"] = _mm_nn(h_bf, wts["kv_w"][:, 2 * d:], tn=LANES, out_dtype=F32, name="f_proj")
                kvs["c"], ct = _fgate_fwd(kvs["fp"], wts["kv_fb"], tq=tq, name="fgate_fwd")
                kvs["ct"] = ct[:N_HEADS]
            sv["qg"] = _proj_in(h_bf, wts["b_in"], j, shard_major=False, name="b_in_proj")
            sv["o"], y3 = _attn_fwd(sv["qg"], kvs["kv"], kvs["c"], kvs["ct"], tq=tq, name="attn_fwd")
            g_out, lidx = wts["b_out"], j
        sv["y3"] = y3
        sv["s1"], h, h_bf = _out_ln(y3, g_out, lidx, h, wts["ln1_g"][layer], wts["ln1_b"][layer], tm=tm,
                                    name="mix_out_ln")
        sv["h1_bf"] = h_bf
        sv["z3"] = _proj_in(h_bf, wts["f_in"], layer, shard_major=True, name="f_in_proj")
        sv["yf3"] = _convglu_fwd(sv["z3"], wts["f_cwb3"][layer], name="f_convglu_fwd")
        sv["s2"], h, h_bf = _out_ln(sv["yf3"], wts["f_out"], layer, h, wts["ln2_g"][layer], wts["ln2_b"][layer],
                                    tm=tm, name="ffn_out_ln")
        saved.append(sv)

    loss_tile, dh = _loss_bwd(h, tgt, lo=n_meta, hi=n_tok, tm=tm, name="loss")

    grads = {k: [None] * N_LAYERS for k in ("f_w_in", "f_cwb3", "f_w_out", "ln1_gb", "ln2_gb")}
    grads.update({k: [None] * N_A_LAYERS for k in ("a_w_in", "a_cwb", "a_bd_r", "a_bd_i", "a_vecs", "a_w_out")})
    grads.update({k: [None] * n_b for k in ("b_w_in", "b_w_out")})
    dkv = []
    for layer in reversed(range(N_LAYERS)):
        sv = saved[layer]
        ds, ds_bf, grads["ln2_gb"][layer] = _ln_bwd(dh, sv["s2"], wts["ln2_g"][layer], tm=tm, name="ln_bwd")
        dz, dcw = _ffn_bwd_mid(ds_bf, wts["f_out"], layer, sv["z3"], wts["f_cwb3"][layer], name="f_bwd_mid")
        grads["f_cwb3"][layer] = dcw.reshape((N_DEV,) + dcw.shape[2:])
        dz3 = dz.reshape((N_DEV,) + dz.shape[2:])
        grads["f_w_out"][layer] = _w_out_grad(sv["yf3"], ds_bf, wts["f_out"].shape[2], name="f_w_out_grad")
        dh = _in_bwd(dz3, wts["f_in"], layer, ds, tm=tmb, name="f_in_bwd")
        grads["f_w_in"][layer] = _w_in_grad(sv["h1_bf"], dz3, name="f_w_in_grad")
        ds, ds_bf, grads["ln1_gb"][layer] = _ln_bwd(dh, sv["s1"], wts["ln1_g"][layer], tm=tm, name="ln_bwd")
        if layer < N_A_LAYERS:
            dy = _out_bwd(ds_bf, wts["a_out"], layer, tm=tmb // 2, name="a_out_bwd")
            grads["a_w_out"][layer] = _w_out_grad(sv["y3"], ds_bf, wts["a_out"].shape[2], name="a_w_out_grad")
            d_h, d_a, dgate = _scan_bwd(dy, sv["gr"], sv["hr"], sv["a"], cb=sb, name="a_scan_bwd")
            d_rec, dpr, dpi, grads["a_vecs"][layer] = _gates_bwd(
                sv["rec"], sv["r"], sv["i"], sv["a"], d_h, d_a, wts["a_bd_r"][layer], wts["a_bd_i"][layer],
                wts["a_vecs"][layer], tm=tm, name="a_gates_bwd")
            grads["a_bd_r"][layer], grads["a_bd_i"][layer] = _bd_grad(sv["rec"], dpr, dpi, cb=cb, name="a_bd_grad")
            dact, grads["a_cwb"][layer] = _conv_a_bwd(d_rec, sv["gr"], dgate, wts["a_cwb"][layer], cb=cb,
                                                      name="a_conv_bwd")
            dh = _in_bwd(dact, wts["a_in"], layer, ds, tm=tmb, name="a_in_bwd")
            grads["a_w_in"][layer] = _w_in_grad(sv["h_bf"], dact, name="a_w_in_grad")
        else:
            j = layer - N_A_LAYERS
            dy = _out_bwd(ds_bf, wts["b_out"], j, tm=tmb // 2, name="b_out_bwd")
            grads["b_w_out"][j] = _w_out_grad(sv["y3"], ds_bf, wts["b_out"].shape[2], name="b_w_out_grad")
            dqg, dk, dv, dc = _attn_bwd(dy, sv["qg"], sv["o"], kvs["kv"], kvs["c"], kvs["ct"], tq=tq,
                                        name="attn_bwd")
            dkv.append((dk, dv, dc))
            dh = _in_bwd(dqg, wts["b_in"], j, ds, tm=tmb, name="b_in_bwd")
            grads["b_w_in"][j] = _w_in_grad(sv["h_bf"], dqg, name="b_w_in_grad")
            if j == 0:
                hpb = LANES // (d // N_HEADS)
                dct = (dkv[0][2] + dkv[1][2])[:, :hpb, :].reshape(N_HEADS, tp)
                dct = jnp.concatenate([dct, jnp.zeros((LANES - N_HEADS, tp), F32)])
                df_bf, grads["kv_fb"] = _fgate_bwd(dct, kvs["fp"], wts["kv_fb"], tq=tq, name="fgate_bwd")
                dkvz = jnp.concatenate([_pair_sum(dkv[0][0], dkv[1][0], tm=tm, name="kv_pair_sum"),
                                        _pair_sum(dkv[0][1], dkv[1][1], tm=tm, name="kv_pair_sum"), df_bf], axis=1)
                dh = _mm_nt_full(dkvz, wts["kv_w"], dh, tm=tmb // 2, name="kv_in_bwd")
                grads["kv_w"] = _mm_tn_cols(kvs["h_bf"], dkvz, tn=LANES, name="kv_w_grad")
    return loss_tile, dh, grads


def _finish_small_grads(grads, d_h0, n_meta, shapes):
    r_dim = grads["a_vecs"][0].shape[1]
    per = _lru_block_cols(r_dim) // (r_dim // N_LRU_BLOCKS)
    a_cwb = jnp.stack(grads["a_cwb"])
    a_vecs = jnp.stack(grads["a_vecs"])
    f_cwb3 = jnp.stack(grads["f_cwb3"])
    ln1 = jnp.stack(grads["ln1_gb"])
    ln2 = jnp.stack(grads["ln2_gb"])
    f_rows = f_cwb3.transpose(0, 2, 1, 3).reshape(N_LAYERS, 8, -1)
    return {
        "meta": d_h0[:n_meta],
        "a_conv_w": a_cwb[:, :4], "a_conv_b": a_cwb[:, 4],
        "a_w_r": jnp.stack([_block_diag_extract(g, per) for g in grads["a_bd_r"]]),
        "a_b_r": a_vecs[:, 0],
        "a_w_i": jnp.stack([_block_diag_extract(g, per) for g in grads["a_bd_i"]]),
        "a_b_i": a_vecs[:, 1], "a_lambda": a_vecs[:, 2],
        "kv_w": grads["kv_w"][:, :shapes["kv_w"][1] * N_DEV], "kv_f_b": grads["kv_fb"][0, :N_HEADS],
        "f_conv_w": f_rows[:, :3], "f_conv_b": f_rows[:, 3],
        "ln1_g": ln1[:, 0], "ln1_b": ln1[:, 1], "ln2_g": ln2[:, 0], "ln2_b": ln2[:, 1],
    }


def kernel(x, meta, a_w_in, a_conv_w, a_conv_b, a_w_r, a_b_r, a_w_i, a_b_i, a_lambda, a_w_out, kv_w, kv_f_b, b_w_in, b_w_out, f_w_in, f_conv_w, f_conv_b, f_w_out, ln1_g, ln1_b, ln2_g, ln2_b, loss_target, m_meta, m_a_w_in, m_a_conv_w, m_a_conv_b, m_a_w_r, m_a_b_r, m_a_w_i, m_a_b_i, m_a_lambda, m_a_w_out, m_kv_w, m_kv_f_b, m_b_w_in, m_b_w_out, m_f_w_in, m_f_conv_w, m_f_conv_b, m_f_w_out, m_ln1_g, m_ln1_b, m_ln2_g, m_ln2_b, v_meta, v_a_w_in, v_a_conv_w, v_a_conv_b, v_a_w_r, v_a_b_r, v_a_w_i, v_a_b_i, v_a_lambda, v_a_w_out, v_kv_w, v_kv_f_b, v_b_w_in, v_b_w_out, v_f_w_in, v_f_conv_w, v_f_conv_b, v_f_w_out, v_ln1_g, v_ln1_b, v_ln2_g, v_ln2_b):
    w = dict(meta=meta, a_w_in=a_w_in, a_conv_w=a_conv_w, a_conv_b=a_conv_b, a_w_r=a_w_r, a_b_r=a_b_r, a_w_i=a_w_i,
             a_b_i=a_b_i, a_lambda=a_lambda, a_w_out=a_w_out, kv_w=kv_w, kv_f_b=kv_f_b, b_w_in=b_w_in,
             b_w_out=b_w_out, f_w_in=f_w_in, f_conv_w=f_conv_w, f_conv_b=f_conv_b, f_w_out=f_w_out, ln1_g=ln1_g,
             ln1_b=ln1_b, ln2_g=ln2_g, ln2_b=ln2_b)
    m = dict(meta=m_meta, a_w_in=m_a_w_in, a_conv_w=m_a_conv_w, a_conv_b=m_a_conv_b, a_w_r=m_a_w_r, a_b_r=m_a_b_r,
             a_w_i=m_a_w_i, a_b_i=m_a_b_i, a_lambda=m_a_lambda, a_w_out=m_a_w_out, kv_w=m_kv_w, kv_f_b=m_kv_f_b,
             b_w_in=m_b_w_in, b_w_out=m_b_w_out, f_w_in=m_f_w_in, f_conv_w=m_f_conv_w, f_conv_b=m_f_conv_b,
             f_w_out=m_f_w_out, ln1_g=m_ln1_g, ln1_b=m_ln1_b, ln2_g=m_ln2_g, ln2_b=m_ln2_b)
    v = dict(meta=v_meta, a_w_in=v_a_w_in, a_conv_w=v_a_conv_w, a_conv_b=v_a_conv_b, a_w_r=v_a_w_r, a_b_r=v_a_b_r,
             a_w_i=v_a_w_i, a_b_i=v_a_b_i, a_lambda=v_a_lambda, a_w_out=v_a_w_out, kv_w=v_kv_w, kv_f_b=v_kv_f_b,
             b_w_in=v_b_w_in, b_w_out=v_b_w_out, f_w_in=v_f_w_in, f_conv_w=v_f_conv_w, f_conv_b=v_f_conv_b,
             f_w_out=v_f_w_out, ln1_g=v_ln1_g, ln1_b=v_ln1_b, ln2_g=v_ln2_g, ln2_b=v_ln2_b)
    shapes = {n: w[n].shape for n in WEIGHT_NAMES}

    big_names = BIG + ("kv_w",)
    slab_s = _pack([w[n] for n in COL_F32], F32)
    got = _all_gather([w[n].astype(BF16) for n in big_names] + [slab_s], name="gather_weights")
    gathered = dict(zip(big_names, got[:-1]))
    small = {n: w[n] for n in REPLICATED}
    for n, part in zip(COL_F32, _unpack(got[-1], [w[n].shape for n in COL_F32], 1)):
        small[n] = _cols_full(part)

    n_meta, d = small["meta"].shape
    n_tok = n_meta + x.shape[1]
    tp = -(-n_tok // ROW_ALIGN) * ROW_ALIGN
    pad = jnp.zeros((tp - n_tok, d), F32)
    h0 = jnp.concatenate([small["meta"], x[0], pad])
    tgt = jnp.concatenate([jnp.zeros((n_meta, d), F32), loss_target[0], pad])
    loss_tile, d_h0, grads = _local_step(h0, tgt, n_meta, n_tok, _prepare_weights(gathered, small))
    g_small = _finish_small_grads(grads, d_h0, n_meta, shapes)
    loss = lax.psum(loss_tile[0, 0], MESH_AXES)
    grad_x = d_h0[n_meta:n_tok][None]

    send = []
    for n in BIG:
        send += grads[n]
    send.append(_cols_split(g_small["kv_w"]).astype(BF16))
    send.append(_pack8([_cols_split(g_small[n]) for n in COL_F32], F32))
    recv = _exchange(send, name="scatter_grads")
    g, delta, new_m, new_v = {}, {}, {}, {}
    pos = 0
    for n in BIG + ("kv_w",):
        n_l = shapes[n][0] if n != "kv_w" else 1
        lift = (lambda a: a) if n != "kv_w" else (lambda a: a[None])
        outs = _sum_adamw(recv[pos:pos + n_l], lift(w[n]), lift(m[n]), lift(v[n]), name="sum_adamw_" + n)
        g[n], delta[n], new_m[n], new_v[n] = [o.reshape(shapes[n]) for o in outs]
        pos += n_l
    sum_s = _sum8(recv[pos], name="sum_grads_f32")
    g.update(zip(COL_F32, _unpack(sum_s, [shapes[n] for n in COL_F32], 0)))
    (got_r,) = _all_gather([_pack([g_small[n] for n in REPLICATED], F32)], name="gather_replicated_grads")
    sum_r = _sum8(got_r, name="sum_grads_replicated")
    g.update(zip(REPLICATED, _unpack(sum_r, [shapes[n] for n in REPLICATED], 0)))

    for n in COL_F32 + REPLICATED:
        shp = shapes[n]
        dl, nm, nv = _adamw(_slab2d(w[n]), _slab2d(g[n]), _slab2d(m[n]), _slab2d(v[n]), name="adamw")
        delta[n], new_m[n], new_v[n] = dl.reshape(shp), nm.reshape(shp), nv.reshape(shp)
    return (loss, grad_x, *[g[n] for n in WEIGHT_NAMES], *[delta[n] for n in WEIGHT_NAMES],
            *[new_m[n] for n in WEIGHT_NAMES], *[new_v[n] for n in WEIGHT_NAMES])
```

```python
import math

import jax
import jax.numpy as jnp
from jax import lax
from jax.experimental import pallas as pl
from jax.experimental.pallas import tpu as pltpu

F32 = jnp.float32
BF16 = jnp.bfloat16

N_DEV = 8
MESH_AXES = ("x", "y", "c")
N_LAYERS = 4
N_A_LAYERS = 2
N_LRU_BLOCKS = 16
N_HEADS = 16
LRU_C = 8.0
DN_ALPHA = (2 * N_LAYERS) ** 0.25
LN_EPS = 1e-5
ADAM_LR, ADAM_B1, ADAM_B2, ADAM_EPS, ADAM_WD, ADAM_STEP = 0.001, 0.9, 0.999, 1e-08, 0.01, 10

LANES = 128
SUBLANES = 8
ROW_ALIGN = 128
VMEM_LIMIT_BYTES = 56 * 1024 * 1024
GELU_K = math.sqrt(2.0 / math.pi)
GELU_C = 0.044715
PACK_COLS = 1024


def _params(*sem):
    return pltpu.CompilerParams(dimension_semantics=sem, vmem_limit_bytes=VMEM_LIMIT_BYTES)


def _gelu(x):
    th = jnp.tanh(GELU_K * (x + GELU_C * x * x * x))
    return 0.5 * x * (1.0 + th)


def _gelu_and_grad(x):
    x2 = x * x
    th = jnp.tanh(GELU_K * (x + GELU_C * x2 * x))
    g = 0.5 * x * (1.0 + th)
    dg = 0.5 * (1.0 + th) + 0.5 * x * (1.0 - th * th) * (GELU_K * (1.0 + 3.0 * GELU_C * x2))
    return g, dg


def _sigmoid(x):
    return 1.0 / (1.0 + jnp.exp(-x))


def _expm1(x):
    small = x * (1.0 + 0.5 * x * (1.0 + (1.0 / 3.0) * x * (1.0 + 0.25 * x)))
    return jnp.where(jnp.abs(x) < 1e-2, small, jnp.exp(x) - 1.0)


def _softplus(x):
    e = jnp.exp(-jnp.abs(x))
    small = e * (1.0 - 0.5 * e * (1.0 - (2.0 / 3.0) * e))
    return jnp.maximum(x, 0.0) + jnp.where(e < 1e-2, small, jnp.log(1.0 + e))


def _shift_down(x, s):
    if s == 0:
        return x
    rows = lax.broadcasted_iota(jnp.int32, x.shape, 0)
    return jnp.where(rows >= s, pltpu.roll(x, s, 0), 0.0)


def _shift_up(x, s):
    if s == 0:
        return x
    n = x.shape[0]
    rows = lax.broadcasted_iota(jnp.int32, x.shape, 0)
    return jnp.where(rows < n - s, pltpu.roll(x, n - s, 0), 0.0)


def _dot_nn(a, b):
    return lax.dot_general(a, b, (((1,), (0,)), ((), ())), preferred_element_type=F32)


def _dot_nt(a, b):
    return lax.dot_general(a, b, (((1,), (1,)), ((), ())), preferred_element_type=F32)


def _dot_tn(a, b):
    return lax.dot_general(a, b, (((0,), (0,)), ((), ())), preferred_element_type=F32)


def _rows8(vals, width):
    rows = lax.broadcasted_iota(jnp.int32, (8, width), 0)
    out = jnp.zeros((8, width), F32)
    for k, v in enumerate(vals):
        out = jnp.where(rows == k, jnp.broadcast_to(v, (8, width)), out)
    return out


def _tile(n, prefer):
    for c in prefer:
        if n % c == 0:
            return c
    return n


def _mm_nn(a, b, *, tn, out_dtype, name):
    m, k = a.shape
    n = b.shape[1]

    def body(a_ref, b_ref, o_ref):
        o_ref[...] = _dot_nn(a_ref[...], b_ref[...]).astype(o_ref.dtype)

    return pl.pallas_call(
        body, name=name, grid=(n // tn,),
        in_specs=[pl.BlockSpec((m, k), lambda j: (0, 0)), pl.BlockSpec((k, tn), lambda j: (0, j))],
        out_specs=pl.BlockSpec((m, tn), lambda j: (0, j)),
        out_shape=jax.ShapeDtypeStruct((m, n), out_dtype),
        compiler_params=_params("parallel"),
    )(a, b)


def _proj_in(h_bf, g_in, *, shard_major, name):
    t, k = h_bf.shape
    n = g_in.shape[2]

    def body(a_ref, b_ref, o_ref):
        o_ref[...] = _dot_nn(a_ref[...], b_ref[...])

    if shard_major:
        out_spec = pl.BlockSpec((None, t, n), lambda j: (j, 0, 0))
        out_shape = jax.ShapeDtypeStruct((N_DEV, t, n), F32)
    else:
        out_spec = pl.BlockSpec((t, n), lambda j: (0, j))
        out_shape = jax.ShapeDtypeStruct((t, N_DEV * n), F32)
    return pl.pallas_call(
        body, name=name, grid=(N_DEV,),
        in_specs=[pl.BlockSpec((t, k), lambda j: (0, 0)),
                  pl.BlockSpec((None, k, n), lambda j: (j, 0, 0))],
        out_specs=out_spec, out_shape=out_shape,
        compiler_params=_params("parallel"),
    )(h_bf, g_in)


def _out_ln(y3, g_out, hin, g, b, *, tm, name):
    nj, t, kj = y3.shape
    _, r, d = g_out.shape

    def body(y_ref, w_ref, hin_ref, g_ref, b_ref, s_ref, h_ref, hb_ref):
        w = w_ref[...].reshape(N_DEV * r, d)
        s = DN_ALPHA * hin_ref[...]
        for jj in range(nj):
            s = s + _dot_nn(y_ref[jj], w[jj * kj:(jj + 1) * kj])
        mu = jnp.mean(s, axis=-1, keepdims=True)
        xc = s - mu
        var = jnp.mean(xc * xc, axis=-1, keepdims=True)
        h = xc * lax.rsqrt(var + LN_EPS) * g_ref[...] + b_ref[...]
        s_ref[...] = s
        h_ref[...] = h
        hb_ref[...] = h.astype(BF16)

    row = pl.BlockSpec((tm, d), lambda i: (i, 0))
    vec = pl.BlockSpec((1, d), lambda i: (0, 0))
    return pl.pallas_call(
        body, name=name, grid=(t // tm,),
        in_specs=[pl.BlockSpec((nj, tm, kj), lambda i: (0, i, 0)),
                  pl.BlockSpec((N_DEV, r, d), lambda i: (0, 0, 0)), row, vec, vec],
        out_specs=[row, row, row],
        out_shape=[jax.ShapeDtypeStruct((t, d), F32), jax.ShapeDtypeStruct((t, d), F32),
                   jax.ShapeDtypeStruct((t, d), BF16)],
        compiler_params=_params("parallel"),
    )(y3, g_out, hin, g, b)


def _out_bwd(ds_bf, g_out, *, tm, name):
    t, d = ds_bf.shape
    r = g_out.shape[1]

    def body(a_ref, w_ref, o_ref):
        o_ref[...] = _dot_nt(a_ref[...], w_ref[...].reshape(N_DEV * r, d))

    return pl.pallas_call(
        body, name=name, grid=(t // tm,),
        in_specs=[pl.BlockSpec((tm, d), lambda i: (i, 0)),
                  pl.BlockSpec((N_DEV, r, d), lambda i: (0, 0, 0))],
        out_specs=pl.BlockSpec((tm, N_DEV * r), lambda i: (i, 0)),
        out_shape=jax.ShapeDtypeStruct((t, N_DEV * r), F32),
        compiler_params=_params("parallel"),
    )(ds_bf, g_out)


def _in_bwd(dact, g_in, add, *, tm, name, alpha=DN_ALPHA):
    t = dact.shape[1]
    _, k, n = g_in.shape
    halves = dact.shape[0] == 2
    per = N_DEV // 2

    def body(a_ref, b_ref, add_ref, o_ref, acc_ref):
        j = pl.program_id(1)

        @pl.when(j == 0)
        def _():
            acc_ref[...] = alpha * add_ref[...]

        acc_ref[...] += _dot_nt(a_ref[...], b_ref[...])

        @pl.when(j == N_DEV - 1)
        def _():
            o_ref[...] = acc_ref[...]

    if halves:
        a_spec = pl.BlockSpec((None, tm, n), lambda i, j: (j // per, i, j % per))
    else:
        a_spec = pl.BlockSpec((None, tm, n), lambda i, j: (j, i, 0))
    return pl.pallas_call(
        body, name=name, grid=(t // tm, N_DEV),
        in_specs=[a_spec, pl.BlockSpec((None, k, n), lambda i, j: (j, 0, 0)),
                  pl.BlockSpec((tm, k), lambda i, j: (i, 0))],
        out_specs=pl.BlockSpec((tm, k), lambda i, j: (i, 0)),
        out_shape=jax.ShapeDtypeStruct((t, k), F32),
        scratch_shapes=[pltpu.VMEM((tm, k), F32)],
        compiler_params=_params("parallel", "arbitrary"),
    )(dact, g_in, add)


def _mm_nt_full(a, b, add, *, tm, name):
    t, n = a.shape
    k = b.shape[0]

    def body(a_ref, b_ref, add_ref, o_ref):
        o_ref[...] = add_ref[...] + _dot_nt(a_ref[...], b_ref[...])

    return pl.pallas_call(
        body, name=name, grid=(t // tm,),
        in_specs=[pl.BlockSpec((tm, n), lambda i: (i, 0)), pl.BlockSpec((k, n), lambda i: (0, 0)),
                  pl.BlockSpec((tm, k), lambda i: (i, 0))],
        out_specs=pl.BlockSpec((tm, k), lambda i: (i, 0)),
        out_shape=jax.ShapeDtypeStruct((t, k), F32),
        compiler_params=_params("parallel"),
    )(a, b, add)


def _w_in_grad(h_bf, dact, *, name):
    t, k = h_bf.shape
    halves = dact.shape[0] == 2
    per = N_DEV // 2
    n = dact.shape[2] // per if halves else dact.shape[2]

    def body(a_ref, b_ref, o_ref):
        o_ref[...] = _dot_tn(a_ref[...], b_ref[...]).astype(BF16)

    if halves:
        b_spec = pl.BlockSpec((None, t, n), lambda j: (j // per, 0, j % per))
    else:
        b_spec = pl.BlockSpec((None, t, n), lambda j: (j, 0, 0))
    return pl.pallas_call(
        body, name=name, grid=(N_DEV,),
        in_specs=[pl.BlockSpec((t, k), lambda j: (0, 0)), b_spec],
        out_specs=pl.BlockSpec((None, k, n), lambda j: (j, 0, 0)),
        out_shape=jax.ShapeDtypeStruct((N_DEV, k, n), BF16),
        compiler_params=_params("parallel"),
    )(h_bf, dact)


def _w_out_grad(y3, ds_bf, r, *, name):
    nj, t, kj = y3.shape
    d = ds_bf.shape[1]
    unit = r * LANES // math.gcd(r, LANES)
    ks = max([c for c in range(unit, min(kj, 768) + 1, unit) if kj % c == 0], default=kj)
    gsz = ks // r
    per = kj // ks

    def body(a_ref, b_ref, o_ref):
        o_ref[...] = _dot_tn(a_ref[...], b_ref[...]).reshape(gsz, r, d).astype(BF16)

    return pl.pallas_call(
        body, name=name, grid=(nj * per,),
        in_specs=[pl.BlockSpec((None, t, ks), lambda j: (j // per, 0, j % per)),
                  pl.BlockSpec((t, d), lambda j: (0, 0))],
        out_specs=pl.BlockSpec((gsz, r, d), lambda j: (j, 0, 0)),
        out_shape=jax.ShapeDtypeStruct((N_DEV, r, d), BF16),
        compiler_params=_params("parallel"),
    )(y3, ds_bf)


def _mm_tn_cols(a, b, *, tn, name):
    t, m = a.shape
    n = b.shape[1]

    def body(a_ref, b_ref, o_ref):
        o_ref[...] = _dot_tn(a_ref[...], b_ref[...])

    return pl.pallas_call(
        body, name=name, grid=(n // tn,),
        in_specs=[pl.BlockSpec((t, m), lambda j: (0, 0)), pl.BlockSpec((t, tn), lambda j: (0, j))],
        out_specs=pl.BlockSpec((m, tn), lambda j: (0, j)),
        out_shape=jax.ShapeDtypeStruct((m, n), F32),
        compiler_params=_params("parallel"),
    )(a, b)


def _ln_bwd(dout, s, g, *, tm, name):
    t, d = s.shape

    def body(do_ref, s_ref, g_ref, ds_ref, dsb_ref, gb_ref):
        i = pl.program_id(0)
        sv = s_ref[...]
        do = do_ref[...]
        mu = jnp.mean(sv, axis=-1, keepdims=True)
        xc = sv - mu
        var = jnp.mean(xc * xc, axis=-1, keepdims=True)
        rstd = lax.rsqrt(var + LN_EPS)
        xhat = xc * rstd
        dxhat = do * g_ref[...]
        m1 = jnp.mean(dxhat, axis=-1, keepdims=True)
        m2 = jnp.mean(dxhat * xhat, axis=-1, keepdims=True)
        ds = rstd * (dxhat - m1 - xhat * m2)
        ds_ref[...] = ds
        dsb_ref[...] = ds.astype(BF16)
        upd = _rows8([jnp.sum(do * xhat, axis=0, keepdims=True), jnp.sum(do, axis=0, keepdims=True)], d)

        @pl.when(i == 0)
        def _():
            gb_ref[...] = upd

        @pl.when(i > 0)
        def _():
            gb_ref[...] += upd

    row = pl.BlockSpec((tm, d), lambda i: (i, 0))
    return pl.pallas_call(
        body, name=name, grid=(t // tm,),
        in_specs=[row, row, pl.BlockSpec((1, d), lambda i: (0, 0))],
        out_specs=[row, row, pl.BlockSpec((8, d), lambda i: (0, 0))],
        out_shape=[jax.ShapeDtypeStruct((t, d), F32), jax.ShapeDtypeStruct((t, d), BF16),
                   jax.ShapeDtypeStruct((8, d), F32)],
        compiler_params=_params("arbitrary"),
    )(dout, s, g)


def _conv_taps(x, wb, width):
    y = jnp.broadcast_to(wb[width:width + 1, :], x.shape)
    for k in range(width):
        y = y + _shift_down(x, width - 1 - k) * wb[k:k + 1, :]
    return y


def _conv_taps_bwd(dy, x, wb, width):
    dx = jnp.zeros_like(dy)
    rows = []
    for k in range(width):
        s = width - 1 - k
        dx = dx + _shift_up(dy, s) * wb[k:k + 1, :]
        rows.append(jnp.sum(dy * _shift_down(x, s), axis=0, keepdims=True))
    rows.append(jnp.sum(dy, axis=0, keepdims=True))
    return dx, _rows8(rows, dy.shape[1])


def _convglu_fwd(z3, fwb3, *, name):
    _, t, n = z3.shape
    half = N_DEV // 2
    nc = pl.cdiv(n, LANES)

    def body(zg_ref, zv_ref, wg_ref, wv_ref, y_ref):
        gate = _conv_taps(zg_ref[...], wg_ref[...], 3)
        val = _conv_taps(zv_ref[...], wv_ref[...], 3)
        y_ref[...] = (_gelu(gate) * val).astype(BF16)

    zblk = lambda off: pl.BlockSpec((None, t, LANES), lambda j, c: (j + off, 0, c))
    wblk = lambda off: pl.BlockSpec((None, 8, LANES), lambda j, c: (j + off, 0, c))
    return pl.pallas_call(
        body, name=name, grid=(half, nc),
        in_specs=[zblk(0), zblk(half), wblk(0), wblk(half)],
        out_specs=zblk(0),
        out_shape=jax.ShapeDtypeStruct((half, t, n), BF16),
        compiler_params=_params("parallel", "parallel"),
    )(z3, z3, fwb3, fwb3)


def _ffn_bwd_mid(ds_bf, g_out, z3, fwb3, *, name):
    t, d = ds_bf.shape
    r = g_out.shape[1]
    n = z3.shape[2]
    half = N_DEV // 2
    nc = pl.cdiv(n, LANES)
    assert n == 2 * r

    def body(ds_ref, w_ref, zg_ref, zv_ref, wg_ref, wv_ref, dz_ref, dwb_ref, wsc_ref):
        c = pl.program_id(1)

        @pl.when(c == 0)
        def _():
            wsc_ref[0:r, :] = w_ref[0]
            wsc_ref[r:2 * r, :] = w_ref[1]
            if nc * LANES > n:
                wsc_ref[n:nc * LANES, :] = jnp.zeros((nc * LANES - n, d), BF16)

        w = wsc_ref[pl.ds(pl.multiple_of(c * LANES, LANES), LANES), :]
        dyf = _dot_nt(ds_ref[...], w)
        zg, zv = zg_ref[...], zv_ref[...]
        wg, wv = wg_ref[...], wv_ref[...]
        gate = _conv_taps(zg, wg, 3)
        val = _conv_taps(zv, wv, 3)
        gl, dgl = _gelu_and_grad(gate)
        dzg, dwg = _conv_taps_bwd(dyf * val * dgl, zg, wg, 3)
        dzv, dwv = _conv_taps_bwd(dyf * gl, zv, wv, 3)
        dz_ref[0] = dzg.astype(BF16)
        dz_ref[1] = dzv.astype(BF16)
        dwb_ref[0] = dwg
        dwb_ref[1] = dwv

    zblk = lambda off: pl.BlockSpec((None, t, LANES), lambda j, c: (j + off, 0, c))
    wblk = lambda off: pl.BlockSpec((None, 8, LANES), lambda j, c: (j + off, 0, c))
    return pl.pallas_call(
        body, name=name, grid=(half, nc),
        in_specs=[pl.BlockSpec((t, d), lambda j, c: (0, 0)),
                  pl.BlockSpec((2, r, d), lambda j, c: (j, 0, 0)),
                  zblk(0), zblk(half), wblk(0), wblk(half)],
        out_specs=[pl.BlockSpec((2, None, t, LANES), lambda j, c: (0, j, 0, c)),
                   pl.BlockSpec((2, None, 8, LANES), lambda j, c: (0, j, 0, c))],
        out_shape=[jax.ShapeDtypeStruct((2, half, t, n), BF16), jax.ShapeDtypeStruct((2, half, 8, n), F32)],
        scratch_shapes=[pltpu.VMEM((nc * LANES, d), BF16)],
        compiler_params=_params("parallel", "arbitrary"),
    )(ds_bf, g_out, z3, z3, fwb3, fwb3)


def _conv_a_fwd(gr, cwb, *, cb, name):
    t, r2 = gr.shape
    r = r2 // 2
    nb = r // cb

    def body(x_ref, w_ref, o_ref):
        o_ref[...] = _conv_taps(x_ref[...], w_ref[...], 4)

    return pl.pallas_call(
        body, name=name, grid=(nb,),
        in_specs=[pl.BlockSpec((t, cb), lambda j: (0, j + nb)), pl.BlockSpec((8, cb), lambda j: (0, j))],
        out_specs=pl.BlockSpec((t, cb), lambda j: (0, j)),
        out_shape=jax.ShapeDtypeStruct((t, r), F32),
        compiler_params=_params("parallel"),
    )(gr, cwb)


def _gates_fwd(rec, bd_r, bd_i, vecs, *, tm, name):
    t, r_dim = rec.shape
    nb, cb, _ = bd_r.shape

    def body(x_ref, wr_ref, wi_ref, v_ref, a_ref, u_ref, r_ref, i_ref):
        x = x_ref[...]
        xb = x.astype(BF16)
        v = v_ref[...]
        r = _sigmoid(_dot_nn(xb, wr_ref[...]) + v[0:1, :])
        i = _sigmoid(_dot_nn(xb, wi_ref[...]) + v[1:2, :])
        log_a = (-LRU_C) * r * _softplus(-v[2:3, :])
        a_ref[...] = jnp.exp(log_a)
        u_ref[...] = jnp.sqrt(-_expm1(2.0 * log_a)) * (i * x)
        r_ref[...] = r
        i_ref[...] = i

    blk = pl.BlockSpec((tm, cb), lambda j, i: (i, j))
    wspec = pl.BlockSpec((None, cb, cb), lambda j, i: (j, 0, 0))
    out = jax.ShapeDtypeStruct((t, r_dim), F32)
    return pl.pallas_call(
        body, name=name, grid=(nb, t // tm),
        in_specs=[blk, wspec, wspec, pl.BlockSpec((8, cb), lambda j, i: (0, j))],
        out_specs=[blk, blk, blk, blk],
        out_shape=[out, out, out, out],
        compiler_params=_params("parallel", "parallel"),
    )(rec, bd_r, bd_i, vecs)


def _scan_fwd(a, u, gr, *, cb, name):
    t, r = a.shape
    nb = r // cb
    seg = t // SUBLANES

    def body(a_ref, u_ref, g_ref, h_ref, y_ref, p_ref):
        def step(k, carry):
            h, p = carry
            rows = pl.ds(k, SUBLANES, stride=seg)
            av = a_ref[rows, :]
            h = av * h + u_ref[rows, :]
            p = av * p
            h_ref[rows, :] = h
            p_ref[rows, :] = p
            return h, p

        h_fin, p_fin = lax.fori_loop(0, seg, step, (jnp.zeros((SUBLANES, cb), F32), jnp.ones((SUBLANES, cb), F32)),
                                     unroll=4)
        carry = h_fin[0:1, :]
        for s in range(1, SUBLANES):
            rows = slice(s * seg, (s + 1) * seg)
            h_ref[rows, :] = h_ref[rows, :] + p_ref[rows, :] * carry
            carry = h_fin[s:s + 1, :] + p_fin[s:s + 1, :] * carry
        y_ref[...] = (_gelu(g_ref[...]) * h_ref[...]).astype(BF16)

    blk = pl.BlockSpec((t, cb), lambda j: (0, j))
    return pl.pallas_call(
        body, name=name, grid=(nb,),
        in_specs=[blk, blk, blk],
        out_specs=[blk, pl.BlockSpec((None, t, cb), lambda j: (0, 0, j))],
        out_shape=[jax.ShapeDtypeStruct((t, r), F32), jax.ShapeDtypeStruct((1, t, r), BF16)],
        scratch_shapes=[pltpu.VMEM((t, cb), F32)],
        compiler_params=_params("parallel"),
    )(a, u, gr)


def _scan_bwd(dy, gr, hr, a, *, cb, name):
    t, r = a.shape
    nb = r // cb
    seg = t // SUBLANES

    def body(dy_ref, g_ref, h_ref, a_ref, dh_ref, da_ref, dg_ref, q_ref):
        gl, dgl = _gelu_and_grad(g_ref[...])
        dyv = dy_ref[...]
        dh_ref[...] = dyv * gl
        dg_ref[...] = (dyv * h_ref[...] * dgl).astype(BF16)

        def step(k, carry):
            cin, q = carry
            rows = pl.ds(seg - 1 - k, SUBLANES, stride=seg)
            dh = dh_ref[rows, :] + cin
            dh_ref[rows, :] = dh
            q_ref[rows, :] = q
            av = a_ref[rows, :]
            return av * dh, av * q

        c_fin, q_fin = lax.fori_loop(0, seg, step, (jnp.zeros((SUBLANES, cb), F32), jnp.ones((SUBLANES, cb), F32)),
                                     unroll=4)
        carry = c_fin[SUBLANES - 1:SUBLANES, :]
        for s in range(SUBLANES - 2, -1, -1):
            rows = slice(s * seg, (s + 1) * seg)
            dh_ref[rows, :] = dh_ref[rows, :] + q_ref[rows, :] * carry
            carry = c_fin[s:s + 1, :] + q_fin[s:s + 1, :] * carry
        da_ref[...] = dh_ref[...] * _shift_down(h_ref[...], 1)

    blk = pl.BlockSpec((t, cb), lambda j: (0, j))
    return pl.pallas_call(
        body, name=name, grid=(nb,),
        in_specs=[blk, blk, blk, blk],
        out_specs=[blk, blk, blk],
        out_shape=[jax.ShapeDtypeStruct((t, r), F32), jax.ShapeDtypeStruct((t, r), F32),
                   jax.ShapeDtypeStruct((t, r), BF16)],
        scratch_shapes=[pltpu.VMEM((t, cb), F32)],
        compiler_params=_params("parallel"),
    )(dy, gr, hr, a)


def _gates_bwd(rec, r, i, a, dh, da, bd_r, bd_i, vecs, *, tm, name):
    t, r_dim = rec.shape
    nb, cb, _ = bd_r.shape

    def body(x_ref, r_ref, i_ref, a_ref, dh_ref, da_ref, wr_ref, wi_ref, v_ref, dx_ref, dpr_ref, dpi_ref, dv_ref):
        step = pl.program_id(1)
        x, r, i, a, dh, da = x_ref[...], r_ref[...], i_ref[...], a_ref[...], dh_ref[...], da_ref[...]
        lam = v_ref[...][2:3, :]
        sp = _softplus(-lam)
        a2 = a * a
        mult = jnp.sqrt(-_expm1(2.0 * (-LRU_C) * r * sp))
        d_i = dh * mult * x
        d_log_a = da * a - (dh * i * x) * a2 / mult
        d_r = d_log_a * ((-LRU_C) * sp)
        d_sp = jnp.sum(d_log_a * ((-LRU_C) * r), axis=0, keepdims=True)
        d_pre_r = d_r * r * (1.0 - r)
        d_pre_i = d_i * i * (1.0 - i)
        dprb = d_pre_r.astype(BF16)
        dpib = d_pre_i.astype(BF16)
        dx_ref[...] = dh * mult * i + _dot_nt(dprb, wr_ref[...]) + _dot_nt(dpib, wi_ref[...])
        dpr_ref[...] = dprb
        dpi_ref[...] = dpib
        upd = _rows8([jnp.sum(d_pre_r, axis=0, keepdims=True), jnp.sum(d_pre_i, axis=0, keepdims=True),
                      -d_sp * _sigmoid(-lam)], cb)

        @pl.when(step == 0)
        def _():
            dv_ref[...] = upd

        @pl.when(step > 0)
        def _():
            dv_ref[...] += upd

    blk = pl.BlockSpec((tm, cb), lambda j, i: (i, j))
    wspec = pl.BlockSpec((None, cb, cb), lambda j, i: (j, 0, 0))
    vspec = pl.BlockSpec((8, cb), lambda j, i: (0, j))
    return pl.pallas_call(
        body, name=name, grid=(nb, t // tm),
        in_specs=[blk] * 6 + [wspec, wspec, vspec],
        out_specs=[blk, blk, blk, vspec],
        out_shape=[jax.ShapeDtypeStruct((t, r_dim), F32), jax.ShapeDtypeStruct((t, r_dim), BF16),
                   jax.ShapeDtypeStruct((t, r_dim), BF16), jax.ShapeDtypeStruct((8, r_dim), F32)],
        compiler_params=_params("parallel", "arbitrary"),
    )(rec, r, i, a, dh, da, bd_r, bd_i, vecs)


def _bd_grad(rec, dpr, dpi, *, cb, name):
    t, r = rec.shape
    nb = r // cb

    def body(x_ref, dr_ref, di_ref, gr_ref, gi_ref):
        xb = x_ref[...].astype(BF16)
        gr_ref[...] = _dot_tn(xb, dr_ref[...])
        gi_ref[...] = _dot_tn(xb, di_ref[...])

    blk = pl.BlockSpec((t, cb), lambda j: (0, j))
    wspec = pl.BlockSpec((None, cb, cb), lambda j: (j, 0, 0))
    out = jax.ShapeDtypeStruct((nb, cb, cb), F32)
    return pl.pallas_call(
        body, name=name, grid=(nb,),
        in_specs=[blk, blk, blk], out_specs=[wspec, wspec], out_shape=[out, out],
        compiler_params=_params("parallel"),
    )(rec, dpr, dpi)


def _conv_a_bwd(d_rec, gr, dgate, cwb, *, cb, name):
    t, r = d_rec.shape
    nb = r // cb

    def body(dy_ref, x_ref, dg_ref, w_ref, dact_ref, dw_ref):
        dx, dw = _conv_taps_bwd(dy_ref[...], x_ref[...], w_ref[...], 4)
        dact_ref[0] = dg_ref[...]
        dact_ref[1] = dx.astype(BF16)
        dw_ref[...] = dw

    blk = pl.BlockSpec((t, cb), lambda j: (0, j))
    vspec = pl.BlockSpec((8, cb), lambda j: (0, j))
    return pl.pallas_call(
        body, name=name, grid=(nb,),
        in_specs=[blk, pl.BlockSpec((t, cb), lambda j: (0, j + nb)), blk, vspec],
        out_specs=[pl.BlockSpec((2, t, cb), lambda j: (0, 0, j)), vspec],
        out_shape=[jax.ShapeDtypeStruct((2, t, r), BF16), jax.ShapeDtypeStruct((8, r), F32)],
        compiler_params=_params("parallel"),
    )(d_rec, gr, dgate, cwb)


def _split3(x):
    p0 = x.astype(BF16)
    r1 = x - p0.astype(F32)
    p1 = r1.astype(BF16)
    p2 = (r1 - p1.astype(F32)).astype(BF16)
    return p0, p1, p2


def _fgate_fwd(fp, fb, *, tq, name):
    t = fp.shape[0]

    def body(f_ref, b_ref, c_ref, ct_ref):
        logf = -_softplus(-(f_ref[...] + b_ref[...]))
        rows = pl.program_id(0) * tq + lax.broadcasted_iota(jnp.int32, (tq, t), 0)
        cols = lax.broadcasted_iota(jnp.int32, (tq, t), 1)
        tri = (cols <= rows).astype(BF16)
        p0, p1, p2 = _split3(logf)
        c = _dot_nn(tri, p0) + _dot_nn(tri, p1) + _dot_nn(tri, p2)
        c_ref[...] = c
        ct_ref[...] = c.T

    return pl.pallas_call(
        body, name=name, grid=(t // tq,),
        in_specs=[pl.BlockSpec((t, LANES), lambda i: (0, 0)), pl.BlockSpec((1, LANES), lambda i: (0, 0))],
        out_specs=[pl.BlockSpec((tq, LANES), lambda i: (i, 0)), pl.BlockSpec((LANES, tq), lambda i: (0, i))],
        out_shape=[jax.ShapeDtypeStruct((t, LANES), F32), jax.ShapeDtypeStruct((LANES, t), F32)],
        compiler_params=_params("parallel"),
    )(fp, fb)


def _fgate_bwd(dct, fp, fb, *, tq, name):
    t = fp.shape[0]

    def body(d_ref, f_ref, b_ref, o_ref, db_ref):
        i = pl.program_id(0)
        rows = lax.broadcasted_iota(jnp.int32, (t, tq), 0)
        cols = i * tq + lax.broadcasted_iota(jnp.int32, (t, tq), 1)
        tri = (rows >= cols).astype(BF16)
        p0, p1, p2 = _split3(d_ref[...])
        dlogf = (_dot_nn(p0, tri) + _dot_nn(p1, tri) + _dot_nn(p2, tri)).T
        df = dlogf * _sigmoid(-(f_ref[...] + b_ref[...]))
        o_ref[...] = df.astype(BF16)
        upd = _rows8([jnp.sum(df, axis=0, keepdims=True)], LANES)

        @pl.when(i == 0)
        def _():
            db_ref[...] = upd

        @pl.when(i > 0)
        def _():
            db_ref[...] += upd

    return pl.pallas_call(
        body, name=name, grid=(t // tq,),
        in_specs=[pl.BlockSpec((LANES, t), lambda i: (0, 0)), pl.BlockSpec((tq, LANES), lambda i: (i, 0)),
                  pl.BlockSpec((1, LANES), lambda i: (0, 0))],
        out_specs=[pl.BlockSpec((tq, LANES), lambda i: (i, 0)), pl.BlockSpec((8, LANES), lambda i: (0, 0))],
        out_shape=[jax.ShapeDtypeStruct((t, LANES), BF16), jax.ShapeDtypeStruct((8, LANES), F32)],
        compiler_params=_params("arbitrary"),
    )(dct, fp, fb)


def _pair_sum(a, b, *, tm, name):
    t, d = a.shape

    def body(a_ref, b_ref, o_ref):
        o_ref[...] = (a_ref[...] + b_ref[...]).astype(BF16)

    row = pl.BlockSpec((tm, d), lambda i: (i, 0))
    return pl.pallas_call(
        body, name=name, grid=(t // tm,), in_specs=[row, row], out_specs=row,
        out_shape=jax.ShapeDtypeStruct((t, d), BF16), compiler_params=_params("parallel"),
    )(a, b)


def _head_masks(dh):
    lane = lax.broadcasted_iota(jnp.int32, (1, LANES), 1)
    return [((lane >= e * dh) & (lane < (e + 1) * dh)) for e in range(LANES // dh)]


def _head_c(c_blk, ct_blk, head):
    lane = lax.broadcasted_iota(jnp.int32, c_blk.shape, 1)
    c_col = jnp.sum(jnp.where(lane == head, c_blk, 0.0), axis=1, keepdims=True)
    sub = lax.broadcasted_iota(jnp.int32, ct_blk.shape, 0)
    c_row = jnp.sum(jnp.where(sub == head, ct_blk, 0.0), axis=0, keepdims=True)
    return c_col, c_row


def _attn_probs(qm, k, c_col, c_row, q0, scale):
    tq, t = qm.shape[0], k.shape[0]
    s = _dot_nt(qm, k) * scale + c_col - c_row
    qi = q0 + lax.broadcasted_iota(jnp.int32, (tq, t), 0)
    ki = lax.broadcasted_iota(jnp.int32, (tq, t), 1)
    s = jnp.where(ki <= qi, s, -jnp.inf)
    m = jnp.max(s, axis=-1, keepdims=True)
    p = jnp.exp(s - m)
    return p / jnp.sum(p, axis=-1, keepdims=True)


def _key_buckets(t, tq):
    step = 3 * tq
    return tuple(range(step, t, step)) + (t,)


def _for_prefix(needed, buckets, fn):
    prev = 0
    for length in buckets:
        pl.when((needed > prev) & (needed <= length))(lambda length=length: fn(length))
        prev = length


def _attn_fwd(qg, kv, c, ct, *, tq, name):
    t, d2 = qg.shape
    d = d2 // 2
    dh = d // N_HEADS
    hpb = LANES // dh
    nhb = d // LANES
    scale = dh ** -0.5
    buckets = _key_buckets(t, tq)

    def body(q_ref, og_ref, k_ref, v_ref, c_ref, ct_ref, o_ref, y_ref):
        hb = pl.program_id(0)
        q0 = pl.program_id(1) * tq

        def run(length):
            q = q_ref[...]
            k = k_ref[0:length, :]
            v = v_ref[0:length, :]
            o = jnp.zeros((tq, LANES), F32)
            for e, msk in enumerate(_head_masks(dh)):
                c_col, c_row = _head_c(c_ref[...], ct_ref[:, 0:length], hb * hpb + e)
                p = _attn_probs(jnp.where(msk, q, 0.0).astype(BF16), k, c_col, c_row, q0, scale)
                o = o + _dot_nn(p.astype(BF16), jnp.where(msk, v, jnp.zeros_like(v)))
            o_ref[...] = o
            y_ref[...] = (o * _sigmoid(og_ref[...])).astype(BF16)

        _for_prefix(q0 + tq, buckets, run)

    qblk = pl.BlockSpec((tq, LANES), lambda h, i: (i, h))
    return pl.pallas_call(
        body, name=name, grid=(nhb, t // tq),
        in_specs=[qblk, pl.BlockSpec((tq, LANES), lambda h, i: (i, h + nhb)),
                  pl.BlockSpec((t, LANES), lambda h, i: (0, h)), pl.BlockSpec((t, LANES), lambda h, i: (0, h + nhb)),
                  pl.BlockSpec((tq, LANES), lambda h, i: (i, 0)), pl.BlockSpec((N_HEADS, t), lambda h, i: (0, 0))],
        out_specs=[qblk, pl.BlockSpec((None, tq, LANES), lambda h, i: (0, i, h))],
        out_shape=[jax.ShapeDtypeStruct((t, d), F32), jax.ShapeDtypeStruct((1, t, d), BF16)],
        compiler_params=_params("parallel", "parallel"),
    )(qg, qg, kv, kv, c, ct)


def _attn_bwd(dy, qg, o, kv, c, ct, *, tq, name):
    t, d2 = qg.shape
    d = d2 // 2
    dh = d // N_HEADS
    hpb = LANES // dh
    nhb = d // LANES
    scale = dh ** -0.5
    buckets = _key_buckets(t, tq)

    def body(dy_ref, q_ref, og_ref, o_ref, k_ref, v_ref, c_ref, ct_ref, dqg_ref, dk_ref, dv_ref, dc_ref):
        hb = pl.program_id(0)
        step = pl.program_id(1)
        q0 = step * tq

        @pl.when(step == 0)
        def _():
            dk_ref[...] = jnp.zeros((t, LANES), F32)
            dv_ref[...] = jnp.zeros((t, LANES), F32)
            dc_ref[...] = jnp.zeros((8, t), F32)

        def run(length):
            q = q_ref[...]
            k = k_ref[0:length, :]
            v = v_ref[0:length, :]
            sg = _sigmoid(og_ref[...])
            dyv = dy_ref[...]
            do = dyv * sg
            dqg_ref[1] = (dyv * o_ref[...] * sg * (1.0 - sg)).astype(BF16)
            dq = jnp.zeros((tq, LANES), F32)
            dk = jnp.zeros((length, LANES), F32)
            dv = jnp.zeros((length, LANES), F32)
            dc_rows = []
            for e, msk in enumerate(_head_masks(dh)):
                c_col, c_row = _head_c(c_ref[...], ct_ref[:, 0:length], hb * hpb + e)
                qm = jnp.where(msk, q, 0.0).astype(BF16)
                dom = jnp.where(msk, do, 0.0).astype(BF16)
                p = _attn_probs(qm, k, c_col, c_row, q0, scale)
                dp = _dot_nt(dom, v)
                dsc = p * (dp - jnp.sum(p * dp, axis=-1, keepdims=True))
                dsb = (dsc * scale).astype(BF16)
                dq = dq + _dot_nn(dsb, jnp.where(msk, k, jnp.zeros_like(k)))
                dk = dk + _dot_tn(dsb, qm)
                dv = dv + _dot_tn(p.astype(BF16), dom)
                dc_rows.append(-jnp.sum(dsc, axis=0, keepdims=True))
            dqg_ref[0] = dq.astype(BF16)
            dk_ref[0:length, :] += dk
            dv_ref[0:length, :] += dv
            dc_ref[:, 0:length] += _rows8(dc_rows, length)

        _for_prefix(q0 + tq, buckets, run)

    qblk = pl.BlockSpec((tq, LANES), lambda h, i: (i, h))
    kblk = pl.BlockSpec((t, LANES), lambda h, i: (0, h))
    return pl.pallas_call(
        body, name=name, grid=(nhb, t // tq),
        in_specs=[qblk, qblk, pl.BlockSpec((tq, LANES), lambda h, i: (i, h + nhb)), qblk,
                  kblk, pl.BlockSpec((t, LANES), lambda h, i: (0, h + nhb)),
                  pl.BlockSpec((tq, LANES), lambda h, i: (i, 0)), pl.BlockSpec((N_HEADS, t), lambda h, i: (0, 0))],
        out_specs=[pl.BlockSpec((2, tq, LANES), lambda h, i: (0, i, h)), kblk, kblk,
                   pl.BlockSpec((None, 8, t), lambda h, i: (h, 0, 0))],
        out_shape=[jax.ShapeDtypeStruct((2, t, d), BF16), jax.ShapeDtypeStruct((t, d), F32),
                   jax.ShapeDtypeStruct((t, d), F32), jax.ShapeDtypeStruct((nhb, 8, t), F32)],
        compiler_params=_params("parallel", "arbitrary"),
    )(dy, qg, qg, o, kv, kv, c, ct)


def _loss_bwd(h, tgt, *, lo, hi, tm, name):
    t, d = h.shape

    def body(h_ref, t_ref, l_ref, dy_ref):
        i = pl.program_id(0)
        rows = i * tm + lax.broadcasted_iota(jnp.int32, (tm, d), 0)
        err = jnp.where((rows >= lo) & (rows < hi), h_ref[...] - t_ref[...], 0.0)
        dy_ref[...] = err * (1.0 / d)
        part = jnp.sum(jnp.sum(err * err, axis=0, keepdims=True), axis=1, keepdims=True) * (0.5 / d)
        upd = jnp.broadcast_to(part, (8, LANES))

        @pl.when(i == 0)
        def _():
            l_ref[...] = upd

        @pl.when(i > 0)
        def _():
            l_ref[...] += upd

    row = pl.BlockSpec((tm, d), lambda i: (i, 0))
    return pl.pallas_call(
        body, name=name, grid=(t // tm,),
        in_specs=[row, row],
        out_specs=[pl.BlockSpec((8, LANES), lambda i: (0, 0)), row],
        out_shape=[jax.ShapeDtypeStruct((8, LANES), F32), jax.ShapeDtypeStruct((t, d), F32)],
        compiler_params=_params("arbitrary"),
    )(h, tgt)


def _adamw_math(w, gv, m, v):
    bc1 = 1.0 / (1.0 - ADAM_B1 ** ADAM_STEP)
    bc2 = 1.0 / (1.0 - ADAM_B2 ** ADAM_STEP)
    nm = ADAM_B1 * m + (1.0 - ADAM_B1) * gv
    nv = ADAM_B2 * v + (1.0 - ADAM_B2) * (gv * gv)
    delta = (-ADAM_LR) * ((nm * bc1) / (jnp.sqrt(nv * bc2) + ADAM_EPS) + ADAM_WD * w)
    return delta, nm, nv


def _adamw(w, g, m, v, *, name):
    r, c = w.shape
    tr = r
    for cand in (512, 256, 128, 64, 32, 16, 8):
        if r % cand == 0 and r > cand:
            tr = cand
            break

    def body(w_ref, g_ref, m_ref, v_ref, d_ref, nm_ref, nv_ref):
        d_ref[...], nm_ref[...], nv_ref[...] = _adamw_math(w_ref[...], g_ref[...], m_ref[...], v_ref[...])

    blk = pl.BlockSpec((tr, c), lambda i: (i, 0))
    out = jax.ShapeDtypeStruct((r, c), F32)
    return pl.pallas_call(
        body, name=name, grid=(r // tr,),
        in_specs=[blk] * 4, out_specs=[blk] * 3, out_shape=[out] * 3,
        compiler_params=_params("parallel"),
    )(w, g, m, v)


def _sum_adamw(recvs, w, m, v, *, name):
    n_l = len(recvs)
    _, r, c = recvs[0].shape
    tr = r
    for cand in (256, 192, 176, 128, 96, 64, 48, 32, 16):
        if r % cand == 0:
            tr = cand
            break

    def body(*refs):
        p_refs = refs[:n_l]
        w_ref, m_ref, v_ref, g_ref, d_ref, nm_ref, nv_ref = refs[n_l:]
        layer = pl.program_id(0)
        for k in range(n_l):
            @pl.when(layer == k)
            def _(k=k):
                acc = p_refs[k][0].astype(F32)
                for dev in range(1, N_DEV):
                    acc = acc + p_refs[k][dev].astype(F32)
                g_ref[...] = acc
                d_ref[...], nm_ref[...], nv_ref[...] = _adamw_math(w_ref[...], acc, m_ref[...], v_ref[...])

    p_specs = [pl.BlockSpec((N_DEV, tr, c), lambda l, i, k=k: (0, jnp.where(l == k, i, 0), 0)) for k in range(n_l)]
    blk = pl.BlockSpec((None, tr, c), lambda l, i: (l, i, 0))
    out = jax.ShapeDtypeStruct((n_l, r, c), F32)
    return pl.pallas_call(
        body, name=name, grid=(n_l, r // tr),
        in_specs=p_specs + [blk] * 3, out_specs=[blk] * 4, out_shape=[out] * 4,
        compiler_params=_params("arbitrary", "arbitrary"),
    )(*recvs, w, m, v)


def _sum8(parts, *, name):
    _, r, c = parts.shape
    tr = r
    for cand in (512, 256, 128, 64, 32, 16):
        if r % cand == 0 and r > cand:
            tr = cand
            break

    def body(p_ref, o_ref):
        acc = p_ref[0].astype(F32)
        for k in range(1, N_DEV):
            acc = acc + p_ref[k].astype(F32)
        o_ref[...] = acc

    return pl.pallas_call(
        body, name=name, grid=(r // tr,),
        in_specs=[pl.BlockSpec((N_DEV, tr, c), lambda i: (0, i, 0))],
        out_specs=pl.BlockSpec((tr, c), lambda i: (i, 0)),
        out_shape=jax.ShapeDtypeStruct((r, c), F32),
        compiler_params=_params("parallel"),
    )(parts)


def _my_index():
    return 4 * lax.axis_index("x") + 2 * lax.axis_index("y") + lax.axis_index("c")


def _peer(k):
    x, y, c = lax.axis_index("x"), lax.axis_index("y"), lax.axis_index("c")
    px = x ^ ((k >> 2) & 1)
    py = y ^ ((k >> 1) & 1)
    pc = c ^ (k & 1)
    return (px, py, pc), 4 * px + 2 * py + pc


def _all_gather(shards, *, name):
    n_arr = len(shards)

    def body(*refs):
        ins, outs = refs[:n_arr], refs[n_arr:2 * n_arr]
        send_sems, recv_sems, local_sems = refs[2 * n_arr:]
        me = _my_index()
        local = [pltpu.make_async_copy(ins[n], outs[n].at[me], local_sems.at[n]) for n in range(n_arr)]
        for cp in local:
            cp.start()
        sends = []
        for k in range(1, N_DEV):
            peer, _ = _peer(k)
            for n in range(n_arr):
                cp = pltpu.make_async_remote_copy(
                    src_ref=ins[n], dst_ref=outs[n].at[me], send_sem=send_sems.at[n, k - 1],
                    recv_sem=recv_sems.at[n, k - 1], device_id=peer, device_id_type=pl.DeviceIdType.MESH)
                cp.start()
                sends.append(cp)
        for k in range(1, N_DEV):
            peer, pidx = _peer(k)
            for n in range(n_arr):
                pltpu.make_async_remote_copy(
                    src_ref=ins[n], dst_ref=outs[n].at[pidx], send_sem=send_sems.at[n, k - 1],
                    recv_sem=recv_sems.at[n, k - 1], device_id=peer, device_id_type=pl.DeviceIdType.MESH).wait_recv()
        for cp in sends:
            cp.wait_send()
        for cp in local:
            cp.wait()

    hbm = pl.BlockSpec(memory_space=pl.ANY)
    return pl.pallas_call(
        body, name=name,
        in_specs=[hbm] * n_arr, out_specs=[hbm] * n_arr,
        out_shape=[jax.ShapeDtypeStruct((N_DEV,) + s.shape, s.dtype) for s in shards],
        scratch_shapes=[pltpu.SemaphoreType.DMA((n_arr, N_DEV - 1)), pltpu.SemaphoreType.DMA((n_arr, N_DEV - 1)),
                        pltpu.SemaphoreType.DMA((n_arr,))],
        compiler_params=pltpu.CompilerParams(has_side_effects=True),
    )(*shards)


_HBM = pl.BlockSpec(memory_space=pltpu.HBM)
_SEM = pl.BlockSpec(memory_space=pltpu.SEMAPHORE)
_EFFECT = pltpu.SideEffectType.DATAFLOW_SIDE_EFFECTING


def _remote(src, dst, send_sem, recv_sem, peer):
    return pltpu.make_async_remote_copy(src_ref=src, dst_ref=dst, send_sem=send_sem, recv_sem=recv_sem,
                                        device_id=peer, device_id_type=pl.DeviceIdType.MESH)


def _own_blocks(srcs, *, scatter, name):
    n = len(srcs)

    def body(*refs):
        ins, outs, sems = refs[:n], refs[n:2 * n], refs[2 * n]
        me = _my_index()
        cps = [pltpu.make_async_copy(ins[t].at[me] if scatter else ins[t], outs[t].at[me], sems.at[t])
               for t in range(n)]
        for cp in cps:
            cp.start()
        for cp in cps:
            cp.wait()

    return pl.pallas_call(
        body, name=name, in_specs=[_HBM] * n, out_specs=[_HBM] * n,
        out_shape=[jax.ShapeDtypeStruct(s.shape if scatter else (N_DEV,) + s.shape, s.dtype) for s in srcs],
        scratch_shapes=[pltpu.SemaphoreType.DMA((n,))],
    )(*srcs)


def _split_start(groups, *, scatter, name):
    sizes = [len(srcs) for srcs, _ in groups]
    flat_src = [s for srcs, _ in groups for s in srcs]
    flat_land = [l for _, lands in groups for l in lands]
    n, n_g = len(flat_src), len(groups)

    def body(*refs):
        ins, lands = refs[:n], refs[n:2 * n]
        sems = refs[2 * n:2 * n + 2 * n_g]
        token = refs[-1]
        me = _my_index()
        t = 0
        for g in range(n_g):
            for q in range(sizes[g]):
                for k in range(1, N_DEV):
                    peer, pidx = _peer(k)
                    src = ins[t].at[pidx] if scatter else ins[t]
                    slot = q * (N_DEV - 1) + k - 1
                    _remote(src, lands[t].at[me], sems[2 * g].at[slot], sems[2 * g + 1].at[slot], peer).start()
                t += 1
        token[...] = jnp.zeros_like(token)

    sem_shapes = []
    for sz in sizes:
        sem_shapes += [pltpu.SemaphoreType.DMA((sz * (N_DEV - 1),)), pltpu.SemaphoreType.DMA((sz * (N_DEV - 1),))]
    outs = pl.pallas_call(
        body, name=name,
        in_specs=[_HBM] * (2 * n),
        out_specs=[_SEM] * (2 * n_g) + [_HBM] * (2 * n) + [pl.BlockSpec(memory_space=pltpu.VMEM)],
        out_shape=sem_shapes + [pltpu.HBM(a.shape, a.dtype) for a in flat_src + flat_land]
        + [jax.ShapeDtypeStruct((8, LANES), F32)],
        input_output_aliases={i: 2 * n_g + i for i in range(2 * n)},
        compiler_params=pltpu.CompilerParams(has_side_effects=_EFFECT),
    )(*[pltpu.with_memory_space_constraint(a, pltpu.HBM) for a in flat_src + flat_land])
    sems, thru, token = outs[:2 * n_g], outs[2 * n_g:2 * n_g + 2 * n], outs[-1]
    handles, pos = [], 0
    for g, sz in enumerate(sizes):
        handles.append((sems[2 * g], sems[2 * g + 1], thru[pos:pos + sz], thru[n + pos:n + pos + sz]))
        pos += sz
    return handles, token


def _split_wait(handle, after, *, scatter, name):
    send_sems, recv_sems, srcs, lands = handle
    n = len(srcs)

    def body(*refs):
        ins, lnd = refs[:n], refs[n:2 * n]
        ssem, rsem = refs[2 * n], refs[2 * n + 1]
        me = _my_index()
        for t in range(n):
            for k in range(1, N_DEV):
                peer, pidx = _peer(k)
                block = ins[t].at[me] if scatter else ins[t]
                slot = t * (N_DEV - 1) + k - 1
                _remote(block, lnd[t].at[me], ssem.at[slot], rsem.at[slot], peer).wait_send()
                _remote(block, lnd[t].at[pidx], ssem.at[slot], rsem.at[slot], peer).wait_recv()

    return pl.pallas_call(
        body, name=name,
        in_specs=[_HBM] * (2 * n) + [_SEM, _SEM, pl.BlockSpec(memory_space=pl.ANY)],
        out_specs=[_HBM] * n,
        out_shape=[pltpu.HBM(l.shape, l.dtype) for l in lands],
        input_output_aliases={n + t: t for t in range(n)},
        compiler_params=pltpu.CompilerParams(has_side_effects=_EFFECT),
    )(*srcs, *lands, send_sems, recv_sems, after)


def _pack(arrs, dtype, row_quantum=16):
    flat = jnp.concatenate([a.astype(dtype).reshape(-1) for a in arrs])
    pad = (-flat.shape[0]) % (row_quantum * PACK_COLS)
    if pad:
        flat = jnp.concatenate([flat, jnp.zeros((pad,), dtype)])
    return flat.reshape(-1, PACK_COLS)


def _pack8(arrs, dtype):
    flat = jnp.concatenate([a.astype(dtype).reshape(N_DEV, -1) for a in arrs], axis=1)
    pad = (-flat.shape[1]) % (16 * PACK_COLS)
    if pad:
        flat = jnp.concatenate([flat, jnp.zeros((N_DEV, pad), dtype)], axis=1)
    return flat.reshape(N_DEV, -1, PACK_COLS)


def _unpack(slab, shapes, lead):
    lead_shape = slab.shape[:lead]
    flat = slab.reshape(lead_shape + (-1,))
    outs, off = [], 0
    for shp in shapes:
        size = math.prod(shp)
        outs.append(flat[..., off:off + size].reshape(lead_shape + tuple(shp)))
        off += size
    return outs


def _cols_full(g):
    g = jnp.moveaxis(g, 0, -2)
    return g.reshape(g.shape[:-2] + (g.shape[-2] * g.shape[-1],))


def _cols_split(full):
    n = full.shape[-1] // N_DEV
    return jnp.moveaxis(full.reshape(full.shape[:-1] + (N_DEV, n)), -2, 0)


def _block_diag(w, per):
    n, b, _ = w.shape
    w4 = w.reshape(n // per, per, b, b)
    eye = jnp.eye(per, dtype=w.dtype)
    return jnp.einsum('gpab,pq->gpaqb', w4, eye).reshape(n // per, per * b, per * b)


def _block_diag_extract(g, per):
    gn, cb, _ = g.shape
    b = cb // per
    g5 = g.reshape(gn, per, b, per, b)
    return jnp.stack([g5[:, p, :, p, :] for p in range(per)], axis=1).reshape(gn * per, b, b)


def _slab2d(a):
    if a.size % PACK_COLS == 0:
        return a.reshape(-1, PACK_COLS)
    return a.reshape(-1, a.shape[-1])


def _lru_block_cols(r_dim):
    lru = r_dim // N_LRU_BLOCKS
    return lru * LANES // math.gcd(lru, LANES)


BIG = ("a_w_in", "a_w_out", "b_w_in", "b_w_out", "f_w_in", "f_w_out")
COL_F32 = ("meta", "a_conv_w", "a_conv_b", "a_b_r", "a_b_i", "a_lambda", "f_conv_w")
REPLICATED = ("a_w_r", "a_w_i", "kv_f_b", "f_conv_b", "ln1_g", "ln1_b", "ln2_g", "ln2_b")
WEIGHT_NAMES = ("meta", "a_w_in", "a_conv_w", "a_conv_b", "a_w_r", "a_b_r", "a_w_i", "a_b_i", "a_lambda", "a_w_out",
                "kv_w", "kv_f_b", "b_w_in", "b_w_out", "f_w_in", "f_conv_w", "f_conv_b", "f_w_out",
                "ln1_g", "ln1_b", "ln2_g", "ln2_b")


def _kv_layout(kv_gathered, d):
    kv_full = _cols_full(kv_gathered)
    kv_pad = 2 * d + LANES - kv_full.shape[1]
    return jnp.concatenate([kv_full, jnp.zeros((d, kv_pad), kv_full.dtype)], axis=1)


def _small_layouts(small):
    r_dim = small["a_lambda"].shape[1]
    n_f = small["f_conv_b"].shape[1] // N_DEV
    cb = _lru_block_cols(r_dim)
    per = cb // (r_dim // N_LRU_BLOCKS)
    n_a = small["a_lambda"].shape[0]
    f_conv_w3 = small["f_conv_w"].reshape(N_LAYERS, 3, N_DEV, n_f).transpose(0, 2, 1, 3)
    f_conv_b3 = small["f_conv_b"].reshape(N_LAYERS, N_DEV, 1, n_f)
    return {
        "kv_fb": jnp.concatenate([small["kv_f_b"], jnp.zeros((LANES - N_HEADS,), F32)])[None],
        "a_cwb": jnp.concatenate([small["a_conv_w"], small["a_conv_b"][:, None],
                                  jnp.zeros((n_a, 3, r_dim), F32)], axis=1),
        "a_vecs": jnp.concatenate([jnp.stack([small["a_b_r"], small["a_b_i"], small["a_lambda"]], axis=1),
                                   jnp.zeros((n_a, 5, r_dim), F32)], axis=1),
        "a_bd_r": jnp.stack([_block_diag(small["a_w_r"][l], per) for l in range(n_a)]).astype(BF16),
        "a_bd_i": jnp.stack([_block_diag(small["a_w_i"][l], per) for l in range(n_a)]).astype(BF16),
        "f_cwb3": jnp.concatenate([f_conv_w3, f_conv_b3, jnp.zeros((N_LAYERS, N_DEV, 4, n_f), F32)], axis=2),
        "ln1_g": small["ln1_g"][:, None], "ln1_b": small["ln1_b"][:, None],
        "ln2_g": small["ln2_g"][:, None], "ln2_b": small["ln2_b"][:, None],
    }


def _local_step(h0, tgt, n_meta, n_tok, wts, hooks):
    tp, d = h0.shape
    tm = tp // 8 if (tp // 8) % 16 == 0 else tp
    tmb = _tile(tp, (1088, 512, 320, 256, 128))
    tq = 128
    r_dim = wts["a_vecs"].shape[2]
    cb = wts["a_bd_r"].shape[-1]
    sb = LANES
    n_b = N_LAYERS - N_A_LAYERS

    h, h_bf = h0, h0.astype(BF16)
    saved = []
    kvs = None
    for layer in range(N_LAYERS):
        lw = hooks.weights(layer, h)
        sv = {"h_bf": h_bf, "w": lw}
        if layer < N_A_LAYERS:
            sv["gr"] = _proj_in(h_bf, lw["in"], shard_major=False, name="a_in_proj")
            sv["rec"] = _conv_a_fwd(sv["gr"], wts["a_cwb"][layer], cb=cb, name="a_conv_fwd")
            a, u, sv["r"], sv["i"] = _gates_fwd(sv["rec"], wts["a_bd_r"][layer], wts["a_bd_i"][layer],
                                                wts["a_vecs"][layer], tm=tm, name="a_gates_fwd")
            sv["a"] = a
            sv["hr"], y3 = _scan_fwd(a, u, sv["gr"], cb=sb, name="a_scan_fwd")
        else:
            j = layer - N_A_LAYERS
            if j == 0:
                kvs = {"h_bf": h_bf, "w": lw["kv_w"]}
                kvs["kv"] = _mm_nn(h_bf, lw["kv_w"][:, :2 * d], tn=_tile(2 * d, (512, 256, 128)), out_dtype=BF16,
                                   name="kv_proj")
                kvs["fp"] = _mm_nn(h_bf, lw["kv_w"][:, 2 * d:], tn=LANES, out_dtype=F32, name="f_proj")
                kvs["c"], ct = _fgate_fwd(kvs["fp"], wts["kv_fb"], tq=tq, name="fgate_fwd")
                kvs["ct"] = ct[:N_HEADS]
            sv["qg"] = _proj_in(h_bf, lw["in"], shard_major=False, name="b_in_proj")
            sv["o"], y3 = _attn_fwd(sv["qg"], kvs["kv"], kvs["c"], kvs["ct"], tq=tq, name="attn_fwd")
        sv["y3"] = y3
        sv["s1"], h, h_bf = _out_ln(y3, lw["out"], h, wts["ln1_g"][layer], wts["ln1_b"][layer], tm=tm,
                                    name="mix_out_ln")
        sv["h1_bf"] = h_bf
        sv["z3"] = _proj_in(h_bf, lw["f_in"], shard_major=True, name="f_in_proj")
        sv["yf3"] = _convglu_fwd(sv["z3"], wts["f_cwb3"][layer], name="f_convglu_fwd")
        sv["s2"], h, h_bf = _out_ln(sv["yf3"], lw["f_out"], h, wts["ln2_g"][layer], wts["ln2_b"][layer],
                                    tm=tm, name="ffn_out_ln")
        saved.append(sv)

    loss_tile, dh = _loss_bwd(h, tgt, lo=n_meta, hi=n_tok, tm=tm, name="loss")

    grads = {k: [None] * N_LAYERS for k in ("f_cwb3", "ln1_gb", "ln2_gb")}
    grads.update({k: [None] * N_A_LAYERS for k in ("a_cwb", "a_bd_r", "a_bd_i", "a_vecs")})
    dkv = []
    token = jnp.zeros((), F32)
    for layer in reversed(range(N_LAYERS)):
        sv = saved[layer]
        lw = sv["w"]
        big = {}
        ds, ds_bf, grads["ln2_gb"][layer] = _ln_bwd(dh, sv["s2"], wts["ln2_g"][layer] + token, tm=tm, name="ln_bwd")
        dz, dcw = _ffn_bwd_mid(ds_bf, lw["f_out"], sv["z3"], wts["f_cwb3"][layer], name="f_bwd_mid")
        grads["f_cwb3"][layer] = dcw.reshape((N_DEV,) + dcw.shape[2:])
        dz3 = dz.reshape((N_DEV,) + dz.shape[2:])
        big["f_out"] = _w_out_grad(sv["yf3"], ds_bf, lw["f_out"].shape[1], name="f_w_out_grad")
        dh = _in_bwd(dz3, lw["f_in"], ds, tm=tmb, name="f_in_bwd")
        big["f_in"] = _w_in_grad(sv["h1_bf"], dz3, name="f_w_in_grad")
        ds, ds_bf, grads["ln1_gb"][layer] = _ln_bwd(dh, sv["s1"], wts["ln1_g"][layer], tm=tm, name="ln_bwd")
        if layer < N_A_LAYERS:
            dy = _out_bwd(ds_bf, lw["out"], tm=tmb // 2, name="a_out_bwd")
            big["out"] = _w_out_grad(sv["y3"], ds_bf, lw["out"].shape[1], name="a_w_out_grad")
            d_h, d_a, dgate = _scan_bwd(dy, sv["gr"], sv["hr"], sv["a"], cb=sb, name="a_scan_bwd")
            d_rec, dpr, dpi, grads["a_vecs"][layer] = _gates_bwd(
                sv["rec"], sv["r"], sv["i"], sv["a"], d_h, d_a, wts["a_bd_r"][layer], wts["a_bd_i"][layer],
                wts["a_vecs"][layer], tm=tm, name="a_gates_bwd")
            grads["a_bd_r"][layer], grads["a_bd_i"][layer] = _bd_grad(sv["rec"], dpr, dpi, cb=cb, name="a_bd_grad")
            dact, grads["a_cwb"][layer] = _conv_a_bwd(d_rec, sv["gr"], dgate, wts["a_cwb"][layer], cb=cb,
                                                      name="a_conv_bwd")
            dh = _in_bwd(dact, lw["in"], ds, tm=tmb, name="a_in_bwd")
            big["in"] = _w_in_grad(sv["h_bf"], dact, name="a_w_in_grad")
        else:
            j = layer - N_A_LAYERS
            dy = _out_bwd(ds_bf, lw["out"], tm=tmb // 2, name="b_out_bwd")
            big["out"] = _w_out_grad(sv["y3"], ds_bf, lw["out"].shape[1], name="b_w_out_grad")
            dqg, dk, dv, dc = _attn_bwd(dy, sv["qg"], sv["o"], kvs["kv"], kvs["c"], kvs["ct"], tq=tq,
                                        name="attn_bwd")
            dkv.append((dk, dv, dc))
            dh = _in_bwd(dqg, lw["in"], ds, tm=tmb, name="b_in_bwd")
            big["in"] = _w_in_grad(sv["h_bf"], dqg, name="b_w_in_grad")
            if j == 0:
                hpb = LANES // (d // N_HEADS)
                dct = (dkv[0][2] + dkv[1][2])[:, :hpb, :].reshape(N_HEADS, tp)
                dct = jnp.concatenate([dct, jnp.zeros((LANES - N_HEADS, tp), F32)])
                df_bf, grads["kv_fb"] = _fgate_bwd(dct, kvs["fp"], wts["kv_fb"], tq=tq, name="fgate_bwd")
                dkvz = jnp.concatenate([_pair_sum(dkv[0][0], dkv[1][0], tm=tm, name="kv_pair_sum"),
                                        _pair_sum(dkv[0][1], dkv[1][1], tm=tm, name="kv_pair_sum"), df_bf], axis=1)
                dh = _mm_nt_full(dkvz, kvs["w"], dh, tm=tmb // 2, name="kv_in_bwd")
                big["kv_w"] = _mm_tn_cols(kvs["h_bf"], dkvz, tn=LANES, name="kv_w_grad")
        hooks.layer_bwd_done(layer, dh)
        token = hooks.grads_ready(layer, big)
    return loss_tile, dh, grads


def _finish_small_grads(grads, d_h0, n_meta):
    r_dim = grads["a_vecs"][0].shape[1]
    per = _lru_block_cols(r_dim) // (r_dim // N_LRU_BLOCKS)
    a_cwb = jnp.stack(grads["a_cwb"])
    a_vecs = jnp.stack(grads["a_vecs"])
    f_cwb3 = jnp.stack(grads["f_cwb3"])
    ln1 = jnp.stack(grads["ln1_gb"])
    ln2 = jnp.stack(grads["ln2_gb"])
    f_rows = f_cwb3.transpose(0, 2, 1, 3).reshape(N_LAYERS, 8, -1)
    return {
        "meta": d_h0[:n_meta],
        "a_conv_w": a_cwb[:, :4], "a_conv_b": a_cwb[:, 4],
        "a_w_r": jnp.stack([_block_diag_extract(g, per) for g in grads["a_bd_r"]]),
        "a_b_r": a_vecs[:, 0],
        "a_w_i": jnp.stack([_block_diag_extract(g, per) for g in grads["a_bd_i"]]),
        "a_b_i": a_vecs[:, 1], "a_lambda": a_vecs[:, 2],
        "kv_f_b": grads["kv_fb"][0, :N_HEADS],
        "f_conv_w": f_rows[:, :3], "f_conv_b": f_rows[:, 3],
        "ln1_g": ln1[:, 0], "ln1_b": ln1[:, 1], "ln2_g": ln2[:, 0], "ln2_b": ln2[:, 1],
    }


def kernel(x, meta, a_w_in, a_conv_w, a_conv_b, a_w_r, a_b_r, a_w_i, a_b_i, a_lambda, a_w_out, kv_w, kv_f_b, b_w_in, b_w_out, f_w_in, f_conv_w, f_conv_b, f_w_out, ln1_g, ln1_b, ln2_g, ln2_b, loss_target, m_meta, m_a_w_in, m_a_conv_w, m_a_conv_b, m_a_w_r, m_a_b_r, m_a_w_i, m_a_b_i, m_a_lambda, m_a_w_out, m_kv_w, m_kv_f_b, m_b_w_in, m_b_w_out, m_f_w_in, m_f_conv_w, m_f_conv_b, m_f_w_out, m_ln1_g, m_ln1_b, m_ln2_g, m_ln2_b, v_meta, v_a_w_in, v_a_conv_w, v_a_conv_b, v_a_w_r, v_a_b_r, v_a_w_i, v_a_b_i, v_a_lambda, v_a_w_out, v_kv_w, v_kv_f_b, v_b_w_in, v_b_w_out, v_f_w_in, v_f_conv_w, v_f_conv_b, v_f_w_out, v_ln1_g, v_ln1_b, v_ln2_g, v_ln2_b):
    w = dict(meta=meta, a_w_in=a_w_in, a_conv_w=a_conv_w, a_conv_b=a_conv_b, a_w_r=a_w_r, a_b_r=a_b_r, a_w_i=a_w_i,
             a_b_i=a_b_i, a_lambda=a_lambda, a_w_out=a_w_out, kv_w=kv_w, kv_f_b=kv_f_b, b_w_in=b_w_in,
             b_w_out=b_w_out, f_w_in=f_w_in, f_conv_w=f_conv_w, f_conv_b=f_conv_b, f_w_out=f_w_out, ln1_g=ln1_g,
             ln1_b=ln1_b, ln2_g=ln2_g, ln2_b=ln2_b)
    m = dict(meta=m_meta, a_w_in=m_a_w_in, a_conv_w=m_a_conv_w, a_conv_b=m_a_conv_b, a_w_r=m_a_w_r, a_b_r=m_a_b_r,
             a_w_i=m_a_w_i, a_b_i=m_a_b_i, a_lambda=m_a_lambda, a_w_out=m_a_w_out, kv_w=m_kv_w, kv_f_b=m_kv_f_b,
             b_w_in=m_b_w_in, b_w_out=m_b_w_out, f_w_in=m_f_w_in, f_conv_w=m_f_conv_w, f_conv_b=m_f_conv_b,
             f_w_out=m_f_w_out, ln1_g=m_ln1_g, ln1_b=m_ln1_b, ln2_g=m_ln2_g, ln2_b=m_ln2_b)
    v = dict(meta=v_meta, a_w_in=v_a_w_in, a_conv_w=v_a_conv_w, a_conv_b=v_a_conv_b, a_w_r=v_a_w_r, a_b_r=v_a_b_r,
             a_w_i=v_a_w_i, a_b_i=v_a_b_i, a_lambda=v_a_lambda, a_w_out=v_a_w_out, kv_w=v_kv_w, kv_f_b=v_kv_f_b,
             b_w_in=v_b_w_in, b_w_out=v_b_w_out, f_w_in=v_f_w_in, f_conv_w=v_f_conv_w, f_conv_b=v_f_conv_b,
             f_w_out=v_f_w_out, ln1_g=v_ln1_g, ln1_b=v_ln1_b, ln2_g=v_ln2_g, ln2_b=v_ln2_b)
    shapes = {n: w[n].shape for n in WEIGHT_NAMES}

    wb = {n: w[n].astype(BF16) for n in BIG + ("kv_w",)}
    prefix = {layer: ("a_w" if layer < N_A_LAYERS else "b_w") for layer in range(N_LAYERS)}
    keys, srcs = [("small",)], [[_pack([w[n] for n in COL_F32], F32)]]
    for layer in range(N_LAYERS):
        j = layer if layer < N_A_LAYERS else layer - N_A_LAYERS
        k_l = ["in", "out", "f_in", "f_out"]
        s_l = [wb[prefix[layer] + "_in"][j], wb[prefix[layer] + "_out"][j], wb["f_w_in"][layer], wb["f_w_out"][layer]]
        if layer == N_A_LAYERS:
            k_l.append("kv_w")
            s_l.append(wb["kv_w"])
        keys.append(tuple(k_l))
        srcs.append(s_l)
    own = _own_blocks([s for grp in srcs for s in grp], scatter=False, name="gather_own")
    groups, pos = [], 0
    for grp in srcs:
        groups.append((grp, own[pos:pos + len(grp)]))
        pos += len(grp)
    gather_handles, gather_token = _split_start(groups, scatter=False, name="gather_start")
    (got_s,) = _split_wait(gather_handles[0], gather_token, scatter=False, name="gather_wait_small")
    small = {n: w[n] for n in REPLICATED}
    for n, part in zip(COL_F32, _unpack(got_s, [w[n].shape for n in COL_F32], 1)):
        small[n] = _cols_full(part)
    n_meta, d = small["meta"].shape

    class Hooks:
        pending = {}
        received = {}

        @staticmethod
        def weights(layer, after):
            got = _split_wait(gather_handles[layer + 1], after, scatter=False, name=f"gather_wait_{layer}")
            lw = dict(zip(keys[layer + 1], got))
            if "kv_w" in lw:
                lw["kv_w"] = _kv_layout(lw["kv_w"], d)
            return lw

        @staticmethod
        def grads_ready(layer, big):
            send = [big["in"], big["out"], big["f_in"], big["f_out"]]
            if "kv_w" in big:
                send.append(_cols_split(big["kv_w"][:, :shapes["kv_w"][1] * N_DEV]).astype(BF16))
            lands = _own_blocks(send, scatter=True, name=f"scatter_own_{layer}")
            handles, token = _split_start([(send, lands)], scatter=True, name=f"scatter_start_{layer}")
            Hooks.pending[layer] = handles[0]
            return token[0, 0]

        @staticmethod
        def layer_bwd_done(layer, after):
            if layer + 1 in Hooks.pending:
                Hooks.received[layer + 1] = _split_wait(Hooks.pending.pop(layer + 1), after, scatter=True,
                                                        name=f"scatter_wait_{layer + 1}")

    Hooks.pending, Hooks.received = {}, {}

    n_tok = n_meta + x.shape[1]
    tp = -(-n_tok // ROW_ALIGN) * ROW_ALIGN
    pad = jnp.zeros((tp - n_tok, d), F32)
    h0 = jnp.concatenate([small["meta"], x[0], pad])
    tgt = jnp.concatenate([jnp.zeros((n_meta, d), F32), loss_target[0], pad])
    loss_tile, d_h0, grads = _local_step(h0, tgt, n_meta, n_tok, _small_layouts(small), Hooks)
    g_small = _finish_small_grads(grads, d_h0, n_meta)
    loss = lax.psum(loss_tile[0, 0], MESH_AXES)
    grad_x = d_h0[n_meta:n_tok][None]

    rep = _pack([g_small[n] for n in REPLICATED], F32, row_quantum=16 * N_DEV)
    send = [_pack8([_cols_split(g_small[n]) for n in COL_F32], F32), rep.reshape(N_DEV, -1, PACK_COLS)]
    lands = _own_blocks(send, scatter=True, name="scatter_own_small")
    handles, token = _split_start([(send, lands)], scatter=True, name="scatter_start_small")
    Hooks.received[0] = _split_wait(Hooks.pending.pop(0), token, scatter=True, name="scatter_wait_0")
    recv_s, recv_r = _split_wait(handles[0], token, scatter=True, name="scatter_wait_small")

    g, delta, new_m, new_v = {}, {}, {}, {}
    slot = {"in": 0, "out": 1, "f_in": 2, "f_out": 3, "kv_w": 4}
    per_param = {
        "a_w_in": [Hooks.received[l][slot["in"]] for l in range(N_A_LAYERS)],
        "a_w_out": [Hooks.received[l][slot["out"]] for l in range(N_A_LAYERS)],
        "b_w_in": [Hooks.received[l][slot["in"]] for l in range(N_A_LAYERS, N_LAYERS)],
        "b_w_out": [Hooks.received[l][slot["out"]] for l in range(N_A_LAYERS, N_LAYERS)],
        "f_w_in": [Hooks.received[l][slot["f_in"]] for l in range(N_LAYERS)],
        "f_w_out": [Hooks.received[l][slot["f_out"]] for l in range(N_LAYERS)],
        "kv_w": [Hooks.received[N_A_LAYERS][slot["kv_w"]]],
    }
    for n in BIG + ("kv_w",):
        lift = (lambda a: a) if n != "kv_w" else (lambda a: a[None])
        outs = _sum_adamw(per_param[n], lift(w[n]), lift(m[n]), lift(v[n]), name="sum_adamw_" + n)
        g[n], delta[n], new_m[n], new_v[n] = [o.reshape(shapes[n]) for o in outs]
    sum_s = _sum8(recv_s, name="sum_grads_f32")
    g.update(zip(COL_F32, _unpack(sum_s, [shapes[n] for n in COL_F32], 0)))
    (got_r,) = _all_gather([_sum8(recv_r, name="sum_grads_replicated")], name="gather_replicated_sums")
    g.update(zip(REPLICATED, _unpack(got_r.reshape(-1, PACK_COLS), [shapes[n] for n in REPLICATED], 0)))

    for n in COL_F32 + REPLICATED:
        shp = shapes[n]
        dl, nm, nv = _adamw(_slab2d(w[n]), _slab2d(g[n]), _slab2d(m[n]), _slab2d(v[n]), name="adamw")
        delta[n], new_m[n], new_v[n] = dl.reshape(shp), nm.reshape(shp), nv.reshape(shp)
    return (loss, grad_x, *[g[n] for n in WEIGHT_NAMES], *[delta[n] for n in WEIGHT_NAMES],
            *[new_m[n] for n in WEIGHT_NAMES], *[new_v[n] for n in WEIGHT_NAMES])
```

```python
import math

import jax
import jax.numpy as jnp
from jax import lax
from jax.experimental import pallas as pl
from jax.experimental.pallas import tpu as pltpu

F32 = jnp.float32
BF16 = jnp.bfloat16

N_DEV = 8
MESH_AXES = ("x", "y", "c")
N_LAYERS = 4
N_A_LAYERS = 2
N_LRU_BLOCKS = 16
N_HEADS = 16
LRU_C = 8.0
DN_ALPHA = (2 * N_LAYERS) ** 0.25
LN_EPS = 1e-5
ADAM_LR, ADAM_B1, ADAM_B2, ADAM_EPS, ADAM_WD, ADAM_STEP = 0.001, 0.9, 0.999, 1e-08, 0.01, 10

LANES = 128
SUBLANES = 8
ROW_ALIGN = 128
VMEM_LIMIT_BYTES = 56 * 1024 * 1024
GELU_K = math.sqrt(2.0 / math.pi)
GELU_C = 0.044715
PACK_COLS = 1024


def _params(*sem):
    return pltpu.CompilerParams(dimension_semantics=sem, vmem_limit_bytes=VMEM_LIMIT_BYTES)


def _gelu(x):
    th = jnp.tanh(GELU_K * (x + GELU_C * x * x * x))
    return 0.5 * x * (1.0 + th)


def _gelu_and_grad(x):
    x2 = x * x
    th = jnp.tanh(GELU_K * (x + GELU_C * x2 * x))
    g = 0.5 * x * (1.0 + th)
    dg = 0.5 * (1.0 + th) + 0.5 * x * (1.0 - th * th) * (GELU_K * (1.0 + 3.0 * GELU_C * x2))
    return g, dg


def _sigmoid(x):
    return 1.0 / (1.0 + jnp.exp(-x))


def _expm1(x):
    small = x * (1.0 + 0.5 * x * (1.0 + (1.0 / 3.0) * x * (1.0 + 0.25 * x)))
    return jnp.where(jnp.abs(x) < 1e-2, small, jnp.exp(x) - 1.0)


def _softplus(x):
    e = jnp.exp(-jnp.abs(x))
    small = e * (1.0 - 0.5 * e * (1.0 - (2.0 / 3.0) * e))
    return jnp.maximum(x, 0.0) + jnp.where(e < 1e-2, small, jnp.log(1.0 + e))


def _shift_down(x, s):
    if s == 0:
        return x
    rows = lax.broadcasted_iota(jnp.int32, x.shape, 0)
    return jnp.where(rows >= s, pltpu.roll(x, s, 0), 0.0)


def _shift_up(x, s):
    if s == 0:
        return x
    n = x.shape[0]
    rows = lax.broadcasted_iota(jnp.int32, x.shape, 0)
    return jnp.where(rows < n - s, pltpu.roll(x, n - s, 0), 0.0)


def _dot_nn(a, b):
    return lax.dot_general(a, b, (((1,), (0,)), ((), ())), preferred_element_type=F32)


def _dot_nt(a, b):
    return lax.dot_general(a, b, (((1,), (1,)), ((), ())), preferred_element_type=F32)


def _dot_tn(a, b):
    return lax.dot_general(a, b, (((0,), (0,)), ((), ())), preferred_element_type=F32)


def _rows8(vals, width):
    rows = lax.broadcasted_iota(jnp.int32, (8, width), 0)
    out = jnp.zeros((8, width), F32)
    for k, v in enumerate(vals):
        out = jnp.where(rows == k, jnp.broadcast_to(v, (8, width)), out)
    return out


def _tile(n, prefer):
    for c in prefer:
        if n % c == 0:
            return c
    return n


def _mm_nn(a, b, *, tn, out_dtype, name):
    m, k = a.shape
    n = b.shape[1]

    def body(a_ref, b_ref, o_ref):
        o_ref[...] = _dot_nn(a_ref[...], b_ref[...]).astype(o_ref.dtype)

    return pl.pallas_call(
        body, name=name, grid=(n // tn,),
        in_specs=[pl.BlockSpec((m, k), lambda j: (0, 0)), pl.BlockSpec((k, tn), lambda j: (0, j))],
        out_specs=pl.BlockSpec((m, tn), lambda j: (0, j)),
        out_shape=jax.ShapeDtypeStruct((m, n), out_dtype),
        compiler_params=_params("parallel"),
    )(a, b)


def _proj_in(h_bf, g_in, *, shard_major, name):
    t, k = h_bf.shape
    n = g_in.shape[2]

    def body(a_ref, b_ref, o_ref):
        o_ref[...] = _dot_nn(a_ref[...], b_ref[...])

    if shard_major:
        out_spec = pl.BlockSpec((None, t, n), lambda j: (j, 0, 0))
        out_shape = jax.ShapeDtypeStruct((N_DEV, t, n), F32)
    else:
        out_spec = pl.BlockSpec((t, n), lambda j: (0, j))
        out_shape = jax.ShapeDtypeStruct((t, N_DEV * n), F32)
    return pl.pallas_call(
        body, name=name, grid=(N_DEV,),
        in_specs=[pl.BlockSpec((t, k), lambda j: (0, 0)),
                  pl.BlockSpec((None, k, n), lambda j: (j, 0, 0))],
        out_specs=out_spec, out_shape=out_shape,
        compiler_params=_params("parallel"),
    )(h_bf, g_in)


def _out_ln(y3, g_out, hin, g, b, *, tm, name):
    nj, t, kj = y3.shape
    _, r, d = g_out.shape

    def body(y_ref, w_ref, hin_ref, g_ref, b_ref, s_ref, h_ref, hb_ref):
        w = w_ref[...].reshape(N_DEV * r, d)
        s = DN_ALPHA * hin_ref[...]
        for jj in range(nj):
            s = s + _dot_nn(y_ref[jj], w[jj * kj:(jj + 1) * kj])
        mu = jnp.mean(s, axis=-1, keepdims=True)
        xc = s - mu
        var = jnp.mean(xc * xc, axis=-1, keepdims=True)
        h = xc * lax.rsqrt(var + LN_EPS) * g_ref[...] + b_ref[...]
        s_ref[...] = s
        h_ref[...] = h
        hb_ref[...] = h.astype(BF16)

    row = pl.BlockSpec((tm, d), lambda i: (i, 0))
    vec = pl.BlockSpec((1, d), lambda i: (0, 0))
    return pl.pallas_call(
        body, name=name, grid=(t // tm,),
        in_specs=[pl.BlockSpec((nj, tm, kj), lambda i: (0, i, 0)),
                  pl.BlockSpec((N_DEV, r, d), lambda i: (0, 0, 0)), row, vec, vec],
        out_specs=[row, row, row],
        out_shape=[jax.ShapeDtypeStruct((t, d), F32), jax.ShapeDtypeStruct((t, d), F32),
                   jax.ShapeDtypeStruct((t, d), BF16)],
        compiler_params=_params("parallel"),
    )(y3, g_out, hin, g, b)


def _out_bwd(ds_bf, g_out, *, tm, name):
    t, d = ds_bf.shape
    r = g_out.shape[1]

    def body(a_ref, w_ref, o_ref):
        o_ref[...] = _dot_nt(a_ref[...], w_ref[...].reshape(N_DEV * r, d))

    return pl.pallas_call(
        body, name=name, grid=(t // tm,),
        in_specs=[pl.BlockSpec((tm, d), lambda i: (i, 0)),
                  pl.BlockSpec((N_DEV, r, d), lambda i: (0, 0, 0))],
        out_specs=pl.BlockSpec((tm, N_DEV * r), lambda i: (i, 0)),
        out_shape=jax.ShapeDtypeStruct((t, N_DEV * r), F32),
        compiler_params=_params("parallel"),
    )(ds_bf, g_out)


def _in_bwd(dact, g_in, add, *, tm, name, alpha=DN_ALPHA):
    t = dact.shape[1]
    _, k, n = g_in.shape
    halves = dact.shape[0] == 2
    per = N_DEV // 2

    def body(a_ref, b_ref, add_ref, o_ref, acc_ref):
        j = pl.program_id(1)

        @pl.when(j == 0)
        def _():
            acc_ref[...] = alpha * add_ref[...]

        acc_ref[...] += _dot_nt(a_ref[...], b_ref[...])

        @pl.when(j == N_DEV - 1)
        def _():
            o_ref[...] = acc_ref[...]

    if halves:
        a_spec = pl.BlockSpec((None, tm, n), lambda i, j: (j // per, i, j % per))
    else:
        a_spec = pl.BlockSpec((None, tm, n), lambda i, j: (j, i, 0))
    return pl.pallas_call(
        body, name=name, grid=(t // tm, N_DEV),
        in_specs=[a_spec, pl.BlockSpec((None, k, n), lambda i, j: (j, 0, 0)),
                  pl.BlockSpec((tm, k), lambda i, j: (i, 0))],
        out_specs=pl.BlockSpec((tm, k), lambda i, j: (i, 0)),
        out_shape=jax.ShapeDtypeStruct((t, k), F32),
        scratch_shapes=[pltpu.VMEM((tm, k), F32)],
        compiler_params=_params("parallel", "arbitrary"),
    )(dact, g_in, add)


def _mm_nt_full(a, b, add, *, tm, name):
    t, n = a.shape
    k = b.shape[0]

    def body(a_ref, b_ref, add_ref, o_ref):
        o_ref[...] = add_ref[...] + _dot_nt(a_ref[...], b_ref[...])

    return pl.pallas_call(
        body, name=name, grid=(t // tm,),
        in_specs=[pl.BlockSpec((tm, n), lambda i: (i, 0)), pl.BlockSpec((k, n), lambda i: (0, 0)),
                  pl.BlockSpec((tm, k), lambda i: (i, 0))],
        out_specs=pl.BlockSpec((tm, k), lambda i: (i, 0)),
        out_shape=jax.ShapeDtypeStruct((t, k), F32),
        compiler_params=_params("parallel"),
    )(a, b, add)


def _w_in_grad(h_bf, dact, *, name):
    t, k = h_bf.shape
    halves = dact.shape[0] == 2
    per = N_DEV // 2
    n = dact.shape[2] // per if halves else dact.shape[2]

    def body(a_ref, b_ref, o_ref):
        o_ref[...] = _dot_tn(a_ref[...], b_ref[...]).astype(BF16)

    if halves:
        b_spec = pl.BlockSpec((None, t, n), lambda j: (j // per, 0, j % per))
    else:
        b_spec = pl.BlockSpec((None, t, n), lambda j: (j, 0, 0))
    return pl.pallas_call(
        body, name=name, grid=(N_DEV,),
        in_specs=[pl.BlockSpec((t, k), lambda j: (0, 0)), b_spec],
        out_specs=pl.BlockSpec((None, k, n), lambda j: (j, 0, 0)),
        out_shape=jax.ShapeDtypeStruct((N_DEV, k, n), BF16),
        compiler_params=_params("parallel"),
    )(h_bf, dact)


def _w_out_grad(y3, ds_bf, r, *, name):
    nj, t, kj = y3.shape
    d = ds_bf.shape[1]
    unit = r * LANES // math.gcd(r, LANES)
    ks = max([c for c in range(unit, min(kj, 768) + 1, unit) if kj % c == 0], default=kj)
    gsz = ks // r
    per = kj // ks

    def body(a_ref, b_ref, o_ref):
        o_ref[...] = _dot_tn(a_ref[...], b_ref[...]).reshape(gsz, r, d).astype(BF16)

    return pl.pallas_call(
        body, name=name, grid=(nj * per,),
        in_specs=[pl.BlockSpec((None, t, ks), lambda j: (j // per, 0, j % per)),
                  pl.BlockSpec((t, d), lambda j: (0, 0))],
        out_specs=pl.BlockSpec((gsz, r, d), lambda j: (j, 0, 0)),
        out_shape=jax.ShapeDtypeStruct((N_DEV, r, d), BF16),
        compiler_params=_params("parallel"),
    )(y3, ds_bf)


def _mm_tn_cols(a, b, *, tn, name):
    t, m = a.shape
    n = b.shape[1]

    def body(a_ref, b_ref, o_ref):
        o_ref[...] = _dot_tn(a_ref[...], b_ref[...])

    return pl.pallas_call(
        body, name=name, grid=(n // tn,),
        in_specs=[pl.BlockSpec((t, m), lambda j: (0, 0)), pl.BlockSpec((t, tn), lambda j: (0, j))],
        out_specs=pl.BlockSpec((m, tn), lambda j: (0, j)),
        out_shape=jax.ShapeDtypeStruct((m, n), F32),
        compiler_params=_params("parallel"),
    )(a, b)


def _ln_bwd(dout, s, g, *, tm, name):
    t, d = s.shape

    def body(do_ref, s_ref, g_ref, ds_ref, dsb_ref, gb_ref):
        i = pl.program_id(0)
        sv = s_ref[...]
        do = do_ref[...]
        mu = jnp.mean(sv, axis=-1, keepdims=True)
        xc = sv - mu
        var = jnp.mean(xc * xc, axis=-1, keepdims=True)
        rstd = lax.rsqrt(var + LN_EPS)
        xhat = xc * rstd
        dxhat = do * g_ref[...]
        m1 = jnp.mean(dxhat, axis=-1, keepdims=True)
        m2 = jnp.mean(dxhat * xhat, axis=-1, keepdims=True)
        ds = rstd * (dxhat - m1 - xhat * m2)
        ds_ref[...] = ds
        dsb_ref[...] = ds.astype(BF16)
        upd = _rows8([jnp.sum(do * xhat, axis=0, keepdims=True), jnp.sum(do, axis=0, keepdims=True)], d)

        @pl.when(i == 0)
        def _():
            gb_ref[...] = upd

        @pl.when(i > 0)
        def _():
            gb_ref[...] += upd

    row = pl.BlockSpec((tm, d), lambda i: (i, 0))
    return pl.pallas_call(
        body, name=name, grid=(t // tm,),
        in_specs=[row, row, pl.BlockSpec((1, d), lambda i: (0, 0))],
        out_specs=[row, row, pl.BlockSpec((8, d), lambda i: (0, 0))],
        out_shape=[jax.ShapeDtypeStruct((t, d), F32), jax.ShapeDtypeStruct((t, d), BF16),
                   jax.ShapeDtypeStruct((8, d), F32)],
        compiler_params=_params("arbitrary"),
    )(dout, s, g)


def _conv_taps(x, wb, width):
    y = jnp.broadcast_to(wb[width:width + 1, :], x.shape)
    for k in range(width):
        y = y + _shift_down(x, width - 1 - k) * wb[k:k + 1, :]
    return y


def _conv_taps_bwd(dy, x, wb, width):
    dx = jnp.zeros_like(dy)
    rows = []
    for k in range(width):
        s = width - 1 - k
        dx = dx + _shift_up(dy, s) * wb[k:k + 1, :]
        rows.append(jnp.sum(dy * _shift_down(x, s), axis=0, keepdims=True))
    rows.append(jnp.sum(dy, axis=0, keepdims=True))
    return dx, _rows8(rows, dy.shape[1])


def _convglu_fwd(z3, fwb3, *, name):
    _, t, n = z3.shape
    half = N_DEV // 2
    nc = pl.cdiv(n, LANES)

    def body(zg_ref, zv_ref, wg_ref, wv_ref, y_ref):
        gate = _conv_taps(zg_ref[...], wg_ref[...], 3)
        val = _conv_taps(zv_ref[...], wv_ref[...], 3)
        y_ref[...] = (_gelu(gate) * val).astype(BF16)

    zblk = lambda off: pl.BlockSpec((None, t, LANES), lambda j, c: (j + off, 0, c))
    wblk = lambda off: pl.BlockSpec((None, 8, LANES), lambda j, c: (j + off, 0, c))
    return pl.pallas_call(
        body, name=name, grid=(half, nc),
        in_specs=[zblk(0), zblk(half), wblk(0), wblk(half)],
        out_specs=zblk(0),
        out_shape=jax.ShapeDtypeStruct((half, t, n), BF16),
        compiler_params=_params("parallel", "parallel"),
    )(z3, z3, fwb3, fwb3)


def _ffn_bwd_mid(ds_bf, g_out, z3, fwb3, *, name):
    t, d = ds_bf.shape
    r = g_out.shape[1]
    n = z3.shape[2]
    half = N_DEV // 2
    nc = pl.cdiv(n, LANES)
    assert n == 2 * r

    def body(ds_ref, w_ref, zg_ref, zv_ref, wg_ref, wv_ref, dz_ref, dwb_ref, wsc_ref):
        c = pl.program_id(1)

        @pl.when(c == 0)
        def _():
            wsc_ref[0:r, :] = w_ref[0]
            wsc_ref[r:2 * r, :] = w_ref[1]
            if nc * LANES > n:
                wsc_ref[n:nc * LANES, :] = jnp.zeros((nc * LANES - n, d), BF16)

        w = wsc_ref[pl.ds(pl.multiple_of(c * LANES, LANES), LANES), :]
        dyf = _dot_nt(ds_ref[...], w)
        zg, zv = zg_ref[...], zv_ref[...]
        wg, wv = wg_ref[...], wv_ref[...]
        gate = _conv_taps(zg, wg, 3)
        val = _conv_taps(zv, wv, 3)
        gl, dgl = _gelu_and_grad(gate)
        dzg, dwg = _conv_taps_bwd(dyf * val * dgl, zg, wg, 3)
        dzv, dwv = _conv_taps_bwd(dyf * gl, zv, wv, 3)
        dz_ref[0] = dzg.astype(BF16)
        dz_ref[1] = dzv.astype(BF16)
        dwb_ref[0] = dwg
        dwb_ref[1] = dwv

    zblk = lambda off: pl.BlockSpec((None, t, LANES), lambda j, c: (j + off, 0, c))
    wblk = lambda off: pl.BlockSpec((None, 8, LANES), lambda j, c: (j + off, 0, c))
    return pl.pallas_call(
        body, name=name, grid=(half, nc),
        in_specs=[pl.BlockSpec((t, d), lambda j, c: (0, 0)),
                  pl.BlockSpec((2, r, d), lambda j, c: (j, 0, 0)),
                  zblk(0), zblk(half), wblk(0), wblk(half)],
        out_specs=[pl.BlockSpec((2, None, t, LANES), lambda j, c: (0, j, 0, c)),
                   pl.BlockSpec((2, None, 8, LANES), lambda j, c: (0, j, 0, c))],
        out_shape=[jax.ShapeDtypeStruct((2, half, t, n), BF16), jax.ShapeDtypeStruct((2, half, 8, n), F32)],
        scratch_shapes=[pltpu.VMEM((nc * LANES, d), BF16)],
        compiler_params=_params("parallel", "arbitrary"),
    )(ds_bf, g_out, z3, z3, fwb3, fwb3)


def _conv_a_fwd(gr, cwb, *, cb, name):
    t, r2 = gr.shape
    r = r2 // 2
    nb = r // cb

    def body(x_ref, w_ref, o_ref):
        o_ref[...] = _conv_taps(x_ref[...], w_ref[...], 4)

    return pl.pallas_call(
        body, name=name, grid=(nb,),
        in_specs=[pl.BlockSpec((t, cb), lambda j: (0, j + nb)), pl.BlockSpec((8, cb), lambda j: (0, j))],
        out_specs=pl.BlockSpec((t, cb), lambda j: (0, j)),
        out_shape=jax.ShapeDtypeStruct((t, r), F32),
        compiler_params=_params("parallel"),
    )(gr, cwb)


def _gates_fwd(rec, bd_r, bd_i, vecs, *, tm, name):
    t, r_dim = rec.shape
    nb, cb, _ = bd_r.shape

    def body(x_ref, wr_ref, wi_ref, v_ref, a_ref, u_ref, r_ref, i_ref):
        x = x_ref[...]
        xb = x.astype(BF16)
        v = v_ref[...]
        r = _sigmoid(_dot_nn(xb, wr_ref[...]) + v[0:1, :])
        i = _sigmoid(_dot_nn(xb, wi_ref[...]) + v[1:2, :])
        log_a = (-LRU_C) * r * _softplus(-v[2:3, :])
        a_ref[...] = jnp.exp(log_a)
        u_ref[...] = jnp.sqrt(-_expm1(2.0 * log_a)) * (i * x)
        r_ref[...] = r
        i_ref[...] = i

    blk = pl.BlockSpec((tm, cb), lambda j, i: (i, j))
    wspec = pl.BlockSpec((None, cb, cb), lambda j, i: (j, 0, 0))
    out = jax.ShapeDtypeStruct((t, r_dim), F32)
    return pl.pallas_call(
        body, name=name, grid=(nb, t // tm),
        in_specs=[blk, wspec, wspec, pl.BlockSpec((8, cb), lambda j, i: (0, j))],
        out_specs=[blk, blk, blk, blk],
        out_shape=[out, out, out, out],
        compiler_params=_params("parallel", "parallel"),
    )(rec, bd_r, bd_i, vecs)


def _scan_fwd(a, u, gr, *, cb, name):
    t, r = a.shape
    nb = r // cb
    seg = t // SUBLANES

    def body(a_ref, u_ref, g_ref, h_ref, y_ref, p_ref):
        def step(k, carry):
            h, p = carry
            rows = pl.ds(k, SUBLANES, stride=seg)
            av = a_ref[rows, :]
            h = av * h + u_ref[rows, :]
            p = av * p
            h_ref[rows, :] = h
            p_ref[rows, :] = p
            return h, p

        h_fin, p_fin = lax.fori_loop(0, seg, step, (jnp.zeros((SUBLANES, cb), F32), jnp.ones((SUBLANES, cb), F32)),
                                     unroll=4)
        carry = h_fin[0:1, :]
        for s in range(1, SUBLANES):
            rows = slice(s * seg, (s + 1) * seg)
            h_ref[rows, :] = h_ref[rows, :] + p_ref[rows, :] * carry
            carry = h_fin[s:s + 1, :] + p_fin[s:s + 1, :] * carry
        y_ref[...] = (_gelu(g_ref[...]) * h_ref[...]).astype(BF16)

    blk = pl.BlockSpec((t, cb), lambda j: (0, j))
    return pl.pallas_call(
        body, name=name, grid=(nb,),
        in_specs=[blk, blk, blk],
        out_specs=[blk, pl.BlockSpec((None, t, cb), lambda j: (0, 0, j))],
        out_shape=[jax.ShapeDtypeStruct((t, r), F32), jax.ShapeDtypeStruct((1, t, r), BF16)],
        scratch_shapes=[pltpu.VMEM((t, cb), F32)],
        compiler_params=_params("parallel"),
    )(a, u, gr)


def _scan_bwd(dy, gr, hr, a, *, cb, name):
    t, r = a.shape
    nb = r // cb
    seg = t // SUBLANES

    def body(dy_ref, g_ref, h_ref, a_ref, dh_ref, da_ref, dg_ref, q_ref):
        gl, dgl = _gelu_and_grad(g_ref[...])
        dyv = dy_ref[...]
        dh_ref[...] = dyv * gl
        dg_ref[...] = (dyv * h_ref[...] * dgl).astype(BF16)

        def step(k, carry):
            cin, q = carry
            rows = pl.ds(seg - 1 - k, SUBLANES, stride=seg)
            dh = dh_ref[rows, :] + cin
            dh_ref[rows, :] = dh
            q_ref[rows, :] = q
            av = a_ref[rows, :]
            return av * dh, av * q

        c_fin, q_fin = lax.fori_loop(0, seg, step, (jnp.zeros((SUBLANES, cb), F32), jnp.ones((SUBLANES, cb), F32)),
                                     unroll=4)
        carry = c_fin[SUBLANES - 1:SUBLANES, :]
        for s in range(SUBLANES - 2, -1, -1):
            rows = slice(s * seg, (s + 1) * seg)
            dh_ref[rows, :] = dh_ref[rows, :] + q_ref[rows, :] * carry
            carry = c_fin[s:s + 1, :] + q_fin[s:s + 1, :] * carry
        da_ref[...] = dh_ref[...] * _shift_down(h_ref[...], 1)

    blk = pl.BlockSpec((t, cb), lambda j: (0, j))
    return pl.pallas_call(
        body, name=name, grid=(nb,),
        in_specs=[blk, blk, blk, blk],
        out_specs=[blk, blk, blk],
        out_shape=[jax.ShapeDtypeStruct((t, r), F32), jax.ShapeDtypeStruct((t, r), F32),
                   jax.ShapeDtypeStruct((t, r), BF16)],
        scratch_shapes=[pltpu.VMEM((t, cb), F32)],
        compiler_params=_params("parallel"),
    )(dy, gr, hr, a)


def _gates_bwd(rec, r, i, a, dh, da, bd_r, bd_i, vecs, *, tm, name):
    t, r_dim = rec.shape
    nb, cb, _ = bd_r.shape

    def body(x_ref, r_ref, i_ref, a_ref, dh_ref, da_ref, wr_ref, wi_ref, v_ref, dx_ref, dpr_ref, dpi_ref, dv_ref):
        step = pl.program_id(1)
        x, r, i, a, dh, da = x_ref[...], r_ref[...], i_ref[...], a_ref[...], dh_ref[...], da_ref[...]
        lam = v_ref[...][2:3, :]
        sp = _softplus(-lam)
        a2 = a * a
        mult = jnp.sqrt(-_expm1(2.0 * (-LRU_C) * r * sp))
        d_i = dh * mult * x
        d_log_a = da * a - (dh * i * x) * a2 / mult
        d_r = d_log_a * ((-LRU_C) * sp)
        d_sp = jnp.sum(d_log_a * ((-LRU_C) * r), axis=0, keepdims=True)
        d_pre_r = d_r * r * (1.0 - r)
        d_pre_i = d_i * i * (1.0 - i)
        dprb = d_pre_r.astype(BF16)
        dpib = d_pre_i.astype(BF16)
        dx_ref[...] = dh * mult * i + _dot_nt(dprb, wr_ref[...]) + _dot_nt(dpib, wi_ref[...])
        dpr_ref[...] = dprb
        dpi_ref[...] = dpib
        upd = _rows8([jnp.sum(d_pre_r, axis=0, keepdims=True), jnp.sum(d_pre_i, axis=0, keepdims=True),
                      -d_sp * _sigmoid(-lam)], cb)

        @pl.when(step == 0)
        def _():
            dv_ref[...] = upd

        @pl.when(step > 0)
        def _():
            dv_ref[...] += upd

    blk = pl.BlockSpec((tm, cb), lambda j, i: (i, j))
    wspec = pl.BlockSpec((None, cb, cb), lambda j, i: (j, 0, 0))
    vspec = pl.BlockSpec((8, cb), lambda j, i: (0, j))
    return pl.pallas_call(
        body, name=name, grid=(nb, t // tm),
        in_specs=[blk] * 6 + [wspec, wspec, vspec],
        out_specs=[blk, blk, blk, vspec],
        out_shape=[jax.ShapeDtypeStruct((t, r_dim), F32), jax.ShapeDtypeStruct((t, r_dim), BF16),
                   jax.ShapeDtypeStruct((t, r_dim), BF16), jax.ShapeDtypeStruct((8, r_dim), F32)],
        compiler_params=_params("parallel", "arbitrary"),
    )(rec, r, i, a, dh, da, bd_r, bd_i, vecs)


def _bd_grad(rec, dpr, dpi, *, cb, name):
    t, r = rec.shape
    nb = r // cb

    def body(x_ref, dr_ref, di_ref, gr_ref, gi_ref):
        xb = x_ref[...].astype(BF16)
        gr_ref[...] = _dot_tn(xb, dr_ref[...])
        gi_ref[...] = _dot_tn(xb, di_ref[...])

    blk = pl.BlockSpec((t, cb), lambda j: (0, j))
    wspec = pl.BlockSpec((None, cb, cb), lambda j: (j, 0, 0))
    out = jax.ShapeDtypeStruct((nb, cb, cb), F32)
    return pl.pallas_call(
        body, name=name, grid=(nb,),
        in_specs=[blk, blk, blk], out_specs=[wspec, wspec], out_shape=[out, out],
        compiler_params=_params("parallel"),
    )(rec, dpr, dpi)


def _conv_a_bwd(d_rec, gr, dgate, cwb, *, cb, name):
    t, r = d_rec.shape
    nb = r // cb

    def body(dy_ref, x_ref, dg_ref, w_ref, dact_ref, dw_ref):
        dx, dw = _conv_taps_bwd(dy_ref[...], x_ref[...], w_ref[...], 4)
        dact_ref[0] = dg_ref[...]
        dact_ref[1] = dx.astype(BF16)
        dw_ref[...] = dw

    blk = pl.BlockSpec((t, cb), lambda j: (0, j))
    vspec = pl.BlockSpec((8, cb), lambda j: (0, j))
    return pl.pallas_call(
        body, name=name, grid=(nb,),
        in_specs=[blk, pl.BlockSpec((t, cb), lambda j: (0, j + nb)), blk, vspec],
        out_specs=[pl.BlockSpec((2, t, cb), lambda j: (0, 0, j)), vspec],
        out_shape=[jax.ShapeDtypeStruct((2, t, r), BF16), jax.ShapeDtypeStruct((8, r), F32)],
        compiler_params=_params("parallel"),
    )(d_rec, gr, dgate, cwb)


def _split3(x):
    p0 = x.astype(BF16)
    r1 = x - p0.astype(F32)
    p1 = r1.astype(BF16)
    p2 = (r1 - p1.astype(F32)).astype(BF16)
    return p0, p1, p2


def _fgate_fwd(fp, fb, *, tq, name):
    t = fp.shape[0]

    def body(f_ref, b_ref, c_ref, ct_ref):
        logf = -_softplus(-(f_ref[...] + b_ref[...]))
        rows = pl.program_id(0) * tq + lax.broadcasted_iota(jnp.int32, (tq, t), 0)
        cols = lax.broadcasted_iota(jnp.int32, (tq, t), 1)
        tri = (cols <= rows).astype(BF16)
        p0, p1, p2 = _split3(logf)
        c = _dot_nn(tri, p0) + _dot_nn(tri, p1) + _dot_nn(tri, p2)
        c_ref[...] = c
        ct_ref[...] = c.T

    return pl.pallas_call(
        body, name=name, grid=(t // tq,),
        in_specs=[pl.BlockSpec((t, LANES), lambda i: (0, 0)), pl.BlockSpec((1, LANES), lambda i: (0, 0))],
        out_specs=[pl.BlockSpec((tq, LANES), lambda i: (i, 0)), pl.BlockSpec((LANES, tq), lambda i: (0, i))],
        out_shape=[jax.ShapeDtypeStruct((t, LANES), F32), jax.ShapeDtypeStruct((LANES, t), F32)],
        compiler_params=_params("parallel"),
    )(fp, fb)


def _fgate_bwd(dct, fp, fb, *, tq, name):
    t = fp.shape[0]

    def body(d_ref, f_ref, b_ref, o_ref, db_ref):
        i = pl.program_id(0)
        rows = lax.broadcasted_iota(jnp.int32, (t, tq), 0)
        cols = i * tq + lax.broadcasted_iota(jnp.int32, (t, tq), 1)
        tri = (rows >= cols).astype(BF16)
        p0, p1, p2 = _split3(d_ref[...])
        dlogf = (_dot_nn(p0, tri) + _dot_nn(p1, tri) + _dot_nn(p2, tri)).T
        df = dlogf * _sigmoid(-(f_ref[...] + b_ref[...]))
        o_ref[...] = df.astype(BF16)
        upd = _rows8([jnp.sum(df, axis=0, keepdims=True)], LANES)

        @pl.when(i == 0)
        def _():
            db_ref[...] = upd

        @pl.when(i > 0)
        def _():
            db_ref[...] += upd

    return pl.pallas_call(
        body, name=name, grid=(t // tq,),
        in_specs=[pl.BlockSpec((LANES, t), lambda i: (0, 0)), pl.BlockSpec((tq, LANES), lambda i: (i, 0)),
                  pl.BlockSpec((1, LANES), lambda i: (0, 0))],
        out_specs=[pl.BlockSpec((tq, LANES), lambda i: (i, 0)), pl.BlockSpec((8, LANES), lambda i: (0, 0))],
        out_shape=[jax.ShapeDtypeStruct((t, LANES), BF16), jax.ShapeDtypeStruct((8, LANES), F32)],
        compiler_params=_params("arbitrary"),
    )(dct, fp, fb)


def _pair_sum(a, b, *, tm, name):
    t, d = a.shape

    def body(a_ref, b_ref, o_ref):
        o_ref[...] = (a_ref[...] + b_ref[...]).astype(BF16)

    row = pl.BlockSpec((tm, d), lambda i: (i, 0))
    return pl.pallas_call(
        body, name=name, grid=(t // tm,), in_specs=[row, row], out_specs=row,
        out_shape=jax.ShapeDtypeStruct((t, d), BF16), compiler_params=_params("parallel"),
    )(a, b)


def _head_masks(dh):
    lane = lax.broadcasted_iota(jnp.int32, (1, LANES), 1)
    return [((lane >= e * dh) & (lane < (e + 1) * dh)) for e in range(LANES // dh)]


def _head_c(c_blk, ct_blk, head):
    lane = lax.broadcasted_iota(jnp.int32, c_blk.shape, 1)
    c_col = jnp.sum(jnp.where(lane == head, c_blk, 0.0), axis=1, keepdims=True)
    sub = lax.broadcasted_iota(jnp.int32, ct_blk.shape, 0)
    c_row = jnp.sum(jnp.where(sub == head, ct_blk, 0.0), axis=0, keepdims=True)
    return c_col, c_row


def _attn_probs(qm, k, c_col, c_row, q0, scale):
    tq, t = qm.shape[0], k.shape[0]
    s = _dot_nt(qm, k) * scale + c_col - c_row
    qi = q0 + lax.broadcasted_iota(jnp.int32, (tq, t), 0)
    ki = lax.broadcasted_iota(jnp.int32, (tq, t), 1)
    s = jnp.where(ki <= qi, s, -jnp.inf)
    m = jnp.max(s, axis=-1, keepdims=True)
    p = jnp.exp(s - m)
    return p / jnp.sum(p, axis=-1, keepdims=True)


def _key_buckets(t, tq):
    step = 3 * tq
    return tuple(range(step, t, step)) + (t,)


def _for_prefix(needed, buckets, fn):
    prev = 0
    for length in buckets:
        pl.when((needed > prev) & (needed <= length))(lambda length=length: fn(length))
        prev = length


def _attn_fwd(qg, kv, c, ct, *, tq, name):
    t, d2 = qg.shape
    d = d2 // 2
    dh = d // N_HEADS
    hpb = LANES // dh
    nhb = d // LANES
    scale = dh ** -0.5
    buckets = _key_buckets(t, tq)

    def body(q_ref, og_ref, k_ref, v_ref, c_ref, ct_ref, o_ref, y_ref):
        hb = pl.program_id(0)
        q0 = pl.program_id(1) * tq

        def run(length):
            q = q_ref[...]
            k = k_ref[0:length, :]
            v = v_ref[0:length, :]
            o = jnp.zeros((tq, LANES), F32)
            for e, msk in enumerate(_head_masks(dh)):
                c_col, c_row = _head_c(c_ref[...], ct_ref[:, 0:length], hb * hpb + e)
                p = _attn_probs(jnp.where(msk, q, 0.0).astype(BF16), k, c_col, c_row, q0, scale)
                o = o + _dot_nn(p.astype(BF16), jnp.where(msk, v, jnp.zeros_like(v)))
            o_ref[...] = o
            y_ref[...] = (o * _sigmoid(og_ref[...])).astype(BF16)

        _for_prefix(q0 + tq, buckets, run)

    qblk = pl.BlockSpec((tq, LANES), lambda h, i: (i, h))
    return pl.pallas_call(
        body, name=name, grid=(nhb, t // tq),
        in_specs=[qblk, pl.BlockSpec((tq, LANES), lambda h, i: (i, h + nhb)),
                  pl.BlockSpec((t, LANES), lambda h, i: (0, h)), pl.BlockSpec((t, LANES), lambda h, i: (0, h + nhb)),
                  pl.BlockSpec((tq, LANES), lambda h, i: (i, 0)), pl.BlockSpec((N_HEADS, t), lambda h, i: (0, 0))],
        out_specs=[qblk, pl.BlockSpec((None, tq, LANES), lambda h, i: (0, i, h))],
        out_shape=[jax.ShapeDtypeStruct((t, d), F32), jax.ShapeDtypeStruct((1, t, d), BF16)],
        compiler_params=_params("parallel", "parallel"),
    )(qg, qg, kv, kv, c, ct)


def _attn_bwd(dy, qg, o, kv, c, ct, *, tq, name):
    t, d2 = qg.shape
    d = d2 // 2
    dh = d // N_HEADS
    hpb = LANES // dh
    nhb = d // LANES
    scale = dh ** -0.5
    buckets = _key_buckets(t, tq)

    def body(dy_ref, q_ref, og_ref, o_ref, k_ref, v_ref, c_ref, ct_ref, dqg_ref, dk_ref, dv_ref, dc_ref):
        hb = pl.program_id(0)
        step = pl.program_id(1)
        q0 = step * tq

        @pl.when(step == 0)
        def _():
            dk_ref[...] = jnp.zeros((t, LANES), F32)
            dv_ref[...] = jnp.zeros((t, LANES), F32)
            dc_ref[...] = jnp.zeros((8, t), F32)

        def run(length):
            q = q_ref[...]
            k = k_ref[0:length, :]
            v = v_ref[0:length, :]
            sg = _sigmoid(og_ref[...])
            dyv = dy_ref[...]
            do = dyv * sg
            dqg_ref[1] = (dyv * o_ref[...] * sg * (1.0 - sg)).astype(BF16)
            dq = jnp.zeros((tq, LANES), F32)
            dk = jnp.zeros((length, LANES), F32)
            dv = jnp.zeros((length, LANES), F32)
            dc_rows = []
            for e, msk in enumerate(_head_masks(dh)):
                c_col, c_row = _head_c(c_ref[...], ct_ref[:, 0:length], hb * hpb + e)
                qm = jnp.where(msk, q, 0.0).astype(BF16)
                dom = jnp.where(msk, do, 0.0).astype(BF16)
                p = _attn_probs(qm, k, c_col, c_row, q0, scale)
                dp = _dot_nt(dom, v)
                dsc = p * (dp - jnp.sum(p * dp, axis=-1, keepdims=True))
                dsb = (dsc * scale).astype(BF16)
                dq = dq + _dot_nn(dsb, jnp.where(msk, k, jnp.zeros_like(k)))
                dk = dk + _dot_tn(dsb, qm)
                dv = dv + _dot_tn(p.astype(BF16), dom)
                dc_rows.append(-jnp.sum(dsc, axis=0, keepdims=True))
            dqg_ref[0] = dq.astype(BF16)
            dk_ref[0:length, :] += dk
            dv_ref[0:length, :] += dv
            dc_ref[:, 0:length] += _rows8(dc_rows, length)

        _for_prefix(q0 + tq, buckets, run)

    qblk = pl.BlockSpec((tq, LANES), lambda h, i: (i, h))
    kblk = pl.BlockSpec((t, LANES), lambda h, i: (0, h))
    return pl.pallas_call(
        body, name=name, grid=(nhb, t // tq),
        in_specs=[qblk, qblk, pl.BlockSpec((tq, LANES), lambda h, i: (i, h + nhb)), qblk,
                  kblk, pl.BlockSpec((t, LANES), lambda h, i: (0, h + nhb)),
                  pl.BlockSpec((tq, LANES), lambda h, i: (i, 0)), pl.BlockSpec((N_HEADS, t), lambda h, i: (0, 0))],
        out_specs=[pl.BlockSpec((2, tq, LANES), lambda h, i: (0, i, h)), kblk, kblk,
                   pl.BlockSpec((None, 8, t), lambda h, i: (h, 0, 0))],
        out_shape=[jax.ShapeDtypeStruct((2, t, d), BF16), jax.ShapeDtypeStruct((t, d), F32),
                   jax.ShapeDtypeStruct((t, d), F32), jax.ShapeDtypeStruct((nhb, 8, t), F32)],
        compiler_params=_params("parallel", "arbitrary"),
    )(dy, qg, qg, o, kv, kv, c, ct)


def _loss_bwd(h, tgt, *, lo, hi, tm, name):
    t, d = h.shape

    def body(h_ref, t_ref, l_ref, dy_ref):
        i = pl.program_id(0)
        rows = i * tm + lax.broadcasted_iota(jnp.int32, (tm, d), 0)
        err = jnp.where((rows >= lo) & (rows < hi), h_ref[...] - t_ref[...], 0.0)
        dy_ref[...] = err * (1.0 / d)
        part = jnp.sum(jnp.sum(err * err, axis=0, keepdims=True), axis=1, keepdims=True) * (0.5 / d)
        upd = jnp.broadcast_to(part, (8, LANES))

        @pl.when(i == 0)
        def _():
            l_ref[...] = upd

        @pl.when(i > 0)
        def _():
            l_ref[...] += upd

    row = pl.BlockSpec((tm, d), lambda i: (i, 0))
    return pl.pallas_call(
        body, name=name, grid=(t // tm,),
        in_specs=[row, row],
        out_specs=[pl.BlockSpec((8, LANES), lambda i: (0, 0)), row],
        out_shape=[jax.ShapeDtypeStruct((8, LANES), F32), jax.ShapeDtypeStruct((t, d), F32)],
        compiler_params=_params("arbitrary"),
    )(h, tgt)


def _adamw_math(w, gv, m, v):
    bc1 = 1.0 / (1.0 - ADAM_B1 ** ADAM_STEP)
    bc2 = 1.0 / (1.0 - ADAM_B2 ** ADAM_STEP)
    nm = ADAM_B1 * m + (1.0 - ADAM_B1) * gv
    nv = ADAM_B2 * v + (1.0 - ADAM_B2) * (gv * gv)
    delta = (-ADAM_LR) * ((nm * bc1) / (jnp.sqrt(nv * bc2) + ADAM_EPS) + ADAM_WD * w)
    return delta, nm, nv


def _adamw(w, g, m, v, *, name):
    r, c = w.shape
    tr = r
    for cand in (512, 256, 128, 64, 32, 16, 8):
        if r % cand == 0 and r > cand:
            tr = cand
            break

    def body(w_ref, g_ref, m_ref, v_ref, d_ref, nm_ref, nv_ref):
        d_ref[...], nm_ref[...], nv_ref[...] = _adamw_math(w_ref[...], g_ref[...], m_ref[...], v_ref[...])

    blk = pl.BlockSpec((tr, c), lambda i: (i, 0))
    out = jax.ShapeDtypeStruct((r, c), F32)
    return pl.pallas_call(
        body, name=name, grid=(r // tr,),
        in_specs=[blk] * 4, out_specs=[blk] * 3, out_shape=[out] * 3,
        compiler_params=_params("parallel"),
    )(w, g, m, v)


def _sum_adamw(recvs, sends, me, w, m, v, *, name):
    n_l = len(recvs)
    _, r, c = recvs[0].shape
    tr = _tile(r, (256, 192, 176, 128, 96, 64, 48, 32, 16))

    def body(me_ref, *refs):
        p_refs, own_refs = refs[:n_l], refs[n_l:2 * n_l]
        w_ref, m_ref, v_ref, g_ref, d_ref, nm_ref, nv_ref, acc_ref = refs[2 * n_l:]
        layer = pl.program_id(0)
        mine = me_ref[0]
        for k in range(n_l):
            @pl.when(layer == k)
            def _(k=k):
                acc_ref[...] = jnp.zeros((tr, c), F32)
                for dev in range(N_DEV):
                    @pl.when(mine == dev)
                    def _():
                        acc_ref[...] += own_refs[k][...].astype(F32)

                    @pl.when(mine != dev)
                    def _(dev=dev):
                        acc_ref[...] += p_refs[k][dev].astype(F32)
                acc = acc_ref[...]
                g_ref[...] = acc
                d_ref[...], nm_ref[...], nv_ref[...] = _adamw_math(w_ref[...], acc, m_ref[...], v_ref[...])

    p_specs = [pl.BlockSpec((N_DEV, tr, c), lambda l, i, me_ref, k=k: (0, jnp.where(l == k, i, 0), 0))
               for k in range(n_l)]
    own_specs = [pl.BlockSpec((None, tr, c), lambda l, i, me_ref, k=k: (me_ref[0], jnp.where(l == k, i, 0), 0))
                 for k in range(n_l)]
    blk = pl.BlockSpec((None, tr, c), lambda l, i, me_ref: (l, i, 0))
    out = jax.ShapeDtypeStruct((n_l, r, c), F32)
    return pl.pallas_call(
        body, name=name,
        grid_spec=pltpu.PrefetchScalarGridSpec(
            num_scalar_prefetch=1, grid=(n_l, r // tr),
            in_specs=p_specs + own_specs + [blk] * 3, out_specs=[blk] * 4,
            scratch_shapes=[pltpu.VMEM((tr, c), F32)]),
        out_shape=[out] * 4,
        compiler_params=_params("arbitrary", "arbitrary"),
    )(me, *recvs, *sends, w, m, v)


def _sum8(parts, *, name):
    _, r, c = parts.shape
    tr = r
    for cand in (512, 256, 128, 64, 32, 16):
        if r % cand == 0 and r > cand:
            tr = cand
            break

    def body(p_ref, o_ref):
        acc = p_ref[0].astype(F32)
        for k in range(1, N_DEV):
            acc = acc + p_ref[k].astype(F32)
        o_ref[...] = acc

    return pl.pallas_call(
        body, name=name, grid=(r // tr,),
        in_specs=[pl.BlockSpec((N_DEV, tr, c), lambda i: (0, i, 0))],
        out_specs=pl.BlockSpec((tr, c), lambda i: (i, 0)),
        out_shape=jax.ShapeDtypeStruct((r, c), F32),
        compiler_params=_params("parallel"),
    )(parts)


def _my_index():
    return 4 * lax.axis_index("x") + 2 * lax.axis_index("y") + lax.axis_index("c")


def _peer(k):
    x, y, c = lax.axis_index("x"), lax.axis_index("y"), lax.axis_index("c")
    px = x ^ ((k >> 2) & 1)
    py = y ^ ((k >> 1) & 1)
    pc = c ^ (k & 1)
    return (px, py, pc), 4 * px + 2 * py + pc


def _all_gather(shards, *, name):
    n_arr = len(shards)

    def body(*refs):
        ins, outs = refs[:n_arr], refs[n_arr:2 * n_arr]
        send_sems, recv_sems, local_sems = refs[2 * n_arr:]
        me = _my_index()
        local = [pltpu.make_async_copy(ins[n], outs[n].at[me], local_sems.at[n]) for n in range(n_arr)]
        for cp in local:
            cp.start()
        sends = []
        for k in range(1, N_DEV):
            peer, _ = _peer(k)
            for n in range(n_arr):
                cp = pltpu.make_async_remote_copy(
                    src_ref=ins[n], dst_ref=outs[n].at[me], send_sem=send_sems.at[n, k - 1],
                    recv_sem=recv_sems.at[n, k - 1], device_id=peer, device_id_type=pl.DeviceIdType.MESH)
                cp.start()
                sends.append(cp)
        for k in range(1, N_DEV):
            peer, pidx = _peer(k)
            for n in range(n_arr):
                pltpu.make_async_remote_copy(
                    src_ref=ins[n], dst_ref=outs[n].at[pidx], send_sem=send_sems.at[n, k - 1],
                    recv_sem=recv_sems.at[n, k - 1], device_id=peer, device_id_type=pl.DeviceIdType.MESH).wait_recv()
        for cp in sends:
            cp.wait_send()
        for cp in local:
            cp.wait()

    hbm = pl.BlockSpec(memory_space=pl.ANY)
    return pl.pallas_call(
        body, name=name,
        in_specs=[hbm] * n_arr, out_specs=[hbm] * n_arr,
        out_shape=[jax.ShapeDtypeStruct((N_DEV,) + s.shape, s.dtype) for s in shards],
        scratch_shapes=[pltpu.SemaphoreType.DMA((n_arr, N_DEV - 1)), pltpu.SemaphoreType.DMA((n_arr, N_DEV - 1)),
                        pltpu.SemaphoreType.DMA((n_arr,))],
        compiler_params=pltpu.CompilerParams(has_side_effects=True),
    )(*shards)


_HBM = pl.BlockSpec(memory_space=pltpu.HBM)
_SEM = pl.BlockSpec(memory_space=pltpu.SEMAPHORE)
_EFFECT = pltpu.SideEffectType.DATAFLOW_SIDE_EFFECTING


def _remote(src, dst, send_sem, recv_sem, peer):
    return pltpu.make_async_remote_copy(src_ref=src, dst_ref=dst, send_sem=send_sem, recv_sem=recv_sem,
                                        device_id=peer, device_id_type=pl.DeviceIdType.MESH)


def _place_own(src, layer, me, *, out_dtype, name):
    _, r, c = src.shape
    tr = _tile(r, (256, 192, 176, 128, 96, 64, 48, 32, 16))

    def body(me_ref, s_ref, o_ref):
        o_ref[...] = s_ref[...].astype(out_dtype)

    return pl.pallas_call(
        body, name=name,
        grid_spec=pltpu.PrefetchScalarGridSpec(
            num_scalar_prefetch=1, grid=(r // tr,),
            in_specs=[pl.BlockSpec((None, tr, c), lambda i, me_ref: (layer, i, 0))],
            out_specs=pl.BlockSpec((None, tr, c), lambda i, me_ref: (me_ref[0], i, 0))),
        out_shape=jax.ShapeDtypeStruct((N_DEV, r, c), out_dtype),
        compiler_params=_params("parallel"),
    )(me, src)


def _own_blocks(srcs, *, name):
    n = len(srcs)

    def body(*refs):
        ins, outs, sems = refs[:n], refs[n:2 * n], refs[2 * n]
        me = _my_index()
        cps = [pltpu.make_async_copy(ins[t].at[me], outs[t].at[me], sems.at[t]) for t in range(n)]
        for cp in cps:
            cp.start()
        for cp in cps:
            cp.wait()

    return pl.pallas_call(
        body, name=name, in_specs=[_HBM] * n, out_specs=[_HBM] * n,
        out_shape=[jax.ShapeDtypeStruct(s.shape, s.dtype) for s in srcs],
        scratch_shapes=[pltpu.SemaphoreType.DMA((n,))],
    )(*srcs)


def _split_start(groups, *, scatter, name):
    sizes = [len(srcs) for srcs, _ in groups]
    flat_src = [s for srcs, _ in groups for s in srcs]
    flat_land = [l for _, lands in groups for l in lands]
    n, n_g = len(flat_land), len(groups)
    if not scatter:
        flat_src = []
    n_in = len(flat_src) + n

    def body(*refs):
        lands = refs[n_in - n:n_in]
        ins = refs[:n] if scatter else lands
        sems = refs[n_in:n_in + 2 * n_g]
        token = refs[-1]
        me = _my_index()
        t = 0
        for g in range(n_g):
            for q in range(sizes[g]):
                for k in range(1, N_DEV):
                    peer, pidx = _peer(k)
                    src = ins[t].at[pidx] if scatter else ins[t].at[me]
                    slot = q * (N_DEV - 1) + k - 1
                    _remote(src, lands[t].at[me], sems[2 * g].at[slot], sems[2 * g + 1].at[slot], peer).start()
                t += 1
        token[...] = jnp.zeros_like(token)

    sem_shapes = []
    for sz in sizes:
        sem_shapes += [pltpu.SemaphoreType.DMA((sz * (N_DEV - 1),)), pltpu.SemaphoreType.DMA((sz * (N_DEV - 1),))]
    outs = pl.pallas_call(
        body, name=name,
        in_specs=[_HBM] * n_in,
        out_specs=[_SEM] * (2 * n_g) + [_HBM] * n_in + [pl.BlockSpec(memory_space=pltpu.VMEM)],
        out_shape=sem_shapes + [pltpu.HBM(a.shape, a.dtype) for a in flat_src + flat_land]
        + [jax.ShapeDtypeStruct((8, LANES), F32)],
        input_output_aliases={i: 2 * n_g + i for i in range(n_in)},
        compiler_params=pltpu.CompilerParams(has_side_effects=_EFFECT),
    )(*[pltpu.with_memory_space_constraint(a, pltpu.HBM) for a in flat_src + flat_land])
    sems, thru, token = outs[:2 * n_g], outs[2 * n_g:2 * n_g + n_in], outs[-1]
    handles, pos = [], 0
    for g, sz in enumerate(sizes):
        lands_g = thru[n_in - n + pos:n_in - n + pos + sz]
        handles.append((sems[2 * g], sems[2 * g + 1], thru[pos:pos + sz] if scatter else [], lands_g))
        pos += sz
    return handles, token


def _split_wait(handle, after, *, scatter, name):
    send_sems, recv_sems, srcs, lands = handle
    n, n_src = len(lands), len(srcs)

    def body(*refs):
        lnd = refs[n_src:n_src + n]
        ins = refs[:n_src] if scatter else lnd
        ssem, rsem = refs[n_src + n], refs[n_src + n + 1]
        me = _my_index()
        for t in range(n):
            for k in range(1, N_DEV):
                peer, pidx = _peer(k)
                block = ins[t].at[me]
                slot = t * (N_DEV - 1) + k - 1
                _remote(block, lnd[t].at[me], ssem.at[slot], rsem.at[slot], peer).wait_send()
                _remote(block, lnd[t].at[pidx], ssem.at[slot], rsem.at[slot], peer).wait_recv()

    return pl.pallas_call(
        body, name=name,
        in_specs=[_HBM] * (n_src + n) + [_SEM, _SEM, pl.BlockSpec(memory_space=pl.ANY)],
        out_specs=[_HBM] * n,
        out_shape=[pltpu.HBM(l.shape, l.dtype) for l in lands],
        input_output_aliases={n_src + t: t for t in range(n)},
        compiler_params=pltpu.CompilerParams(has_side_effects=_EFFECT),
    )(*srcs, *lands, send_sems, recv_sems, after)


def _pack(arrs, dtype, row_quantum=16):
    flat = jnp.concatenate([a.astype(dtype).reshape(-1) for a in arrs])
    pad = (-flat.shape[0]) % (row_quantum * PACK_COLS)
    if pad:
        flat = jnp.concatenate([flat, jnp.zeros((pad,), dtype)])
    return flat.reshape(-1, PACK_COLS)


def _pack8(arrs, dtype):
    flat = jnp.concatenate([a.astype(dtype).reshape(N_DEV, -1) for a in arrs], axis=1)
    pad = (-flat.shape[1]) % (16 * PACK_COLS)
    if pad:
        flat = jnp.concatenate([flat, jnp.zeros((N_DEV, pad), dtype)], axis=1)
    return flat.reshape(N_DEV, -1, PACK_COLS)


def _unpack(slab, shapes, lead):
    lead_shape = slab.shape[:lead]
    flat = slab.reshape(lead_shape + (-1,))
    outs, off = [], 0
    for shp in shapes:
        size = math.prod(shp)
        outs.append(flat[..., off:off + size].reshape(lead_shape + tuple(shp)))
        off += size
    return outs


def _cols_full(g):
    g = jnp.moveaxis(g, 0, -2)
    return g.reshape(g.shape[:-2] + (g.shape[-2] * g.shape[-1],))


def _cols_split(full):
    n = full.shape[-1] // N_DEV
    return jnp.moveaxis(full.reshape(full.shape[:-1] + (N_DEV, n)), -2, 0)


def _block_diag(w, per):
    n, b, _ = w.shape
    w4 = w.reshape(n // per, per, b, b)
    eye = jnp.eye(per, dtype=w.dtype)
    return jnp.einsum('gpab,pq->gpaqb', w4, eye).reshape(n // per, per * b, per * b)


def _block_diag_extract(g, per):
    gn, cb, _ = g.shape
    b = cb // per
    g5 = g.reshape(gn, per, b, per, b)
    return jnp.stack([g5[:, p, :, p, :] for p in range(per)], axis=1).reshape(gn * per, b, b)


def _slab2d(a):
    if a.size % PACK_COLS == 0:
        return a.reshape(-1, PACK_COLS)
    return a.reshape(-1, a.shape[-1])


def _lru_block_cols(r_dim):
    lru = r_dim // N_LRU_BLOCKS
    return lru * LANES // math.gcd(lru, LANES)


BIG = ("a_w_in", "a_w_out", "b_w_in", "b_w_out", "f_w_in", "f_w_out")
COL_F32 = ("meta", "a_conv_w", "a_conv_b", "a_b_r", "a_b_i", "a_lambda", "f_conv_w")
REPLICATED = ("a_w_r", "a_w_i", "kv_f_b", "f_conv_b", "ln1_g", "ln1_b", "ln2_g", "ln2_b")
WEIGHT_NAMES = ("meta", "a_w_in", "a_conv_w", "a_conv_b", "a_w_r", "a_b_r", "a_w_i", "a_b_i", "a_lambda", "a_w_out",
                "kv_w", "kv_f_b", "b_w_in", "b_w_out", "f_w_in", "f_conv_w", "f_conv_b", "f_w_out",
                "ln1_g", "ln1_b", "ln2_g", "ln2_b")


def _kv_layout(kv_gathered, d):
    kv_full = _cols_full(kv_gathered)
    kv_pad = 2 * d + LANES - kv_full.shape[1]
    return jnp.concatenate([kv_full, jnp.zeros((d, kv_pad), kv_full.dtype)], axis=1)


def _small_layouts(small):
    r_dim = small["a_lambda"].shape[1]
    n_f = small["f_conv_b"].shape[1] // N_DEV
    cb = _lru_block_cols(r_dim)
    per = cb // (r_dim // N_LRU_BLOCKS)
    n_a = small["a_lambda"].shape[0]
    f_conv_w3 = small["f_conv_w"].reshape(N_LAYERS, 3, N_DEV, n_f).transpose(0, 2, 1, 3)
    f_conv_b3 = small["f_conv_b"].reshape(N_LAYERS, N_DEV, 1, n_f)
    return {
        "kv_fb": jnp.concatenate([small["kv_f_b"], jnp.zeros((LANES - N_HEADS,), F32)])[None],
        "a_cwb": jnp.concatenate([small["a_conv_w"], small["a_conv_b"][:, None],
                                  jnp.zeros((n_a, 3, r_dim), F32)], axis=1),
        "a_vecs": jnp.concatenate([jnp.stack([small["a_b_r"], small["a_b_i"], small["a_lambda"]], axis=1),
                                   jnp.zeros((n_a, 5, r_dim), F32)], axis=1),
        "a_bd_r": jnp.stack([_block_diag(small["a_w_r"][l], per) for l in range(n_a)]).astype(BF16),
        "a_bd_i": jnp.stack([_block_diag(small["a_w_i"][l], per) for l in range(n_a)]).astype(BF16),
        "f_cwb3": jnp.concatenate([f_conv_w3, f_conv_b3, jnp.zeros((N_LAYERS, N_DEV, 4, n_f), F32)], axis=2),
        "ln1_g": small["ln1_g"][:, None], "ln1_b": small["ln1_b"][:, None],
        "ln2_g": small["ln2_g"][:, None], "ln2_b": small["ln2_b"][:, None],
    }


def _local_step(h0, tgt, n_meta, n_tok, wts, hooks):
    tp, d = h0.shape
    tm = tp // 8 if (tp // 8) % 16 == 0 else tp
    tmb = _tile(tp, (1088, 512, 320, 256, 128))
    tq = 128
    r_dim = wts["a_vecs"].shape[2]
    cb = wts["a_bd_r"].shape[-1]
    sb = LANES
    n_b = N_LAYERS - N_A_LAYERS

    h, h_bf = h0, h0.astype(BF16)
    saved = []
    kvs = None
    for layer in range(N_LAYERS):
        lw = {}
        sv = {"h_bf": h_bf, "w": lw}
        if layer < N_A_LAYERS:
            lw["in"] = hooks.weight(layer, "in", h)
            sv["gr"] = _proj_in(h_bf, lw["in"], shard_major=False, name="a_in_proj")
            sv["rec"] = _conv_a_fwd(sv["gr"], wts["a_cwb"][layer], cb=cb, name="a_conv_fwd")
            a, u, sv["r"], sv["i"] = _gates_fwd(sv["rec"], wts["a_bd_r"][layer], wts["a_bd_i"][layer],
                                                wts["a_vecs"][layer], tm=tm, name="a_gates_fwd")
            sv["a"] = a
            sv["hr"], y3 = _scan_fwd(a, u, sv["gr"], cb=sb, name="a_scan_fwd")
        else:
            j = layer - N_A_LAYERS
            if j == 0:
                kv_w = _kv_layout(hooks.weight(layer, "kv_w", h), d)
                kvs = {"h_bf": h_bf, "w": kv_w}
                kvs["kv"] = _mm_nn(h_bf, kv_w[:, :2 * d], tn=_tile(2 * d, (512, 256, 128)), out_dtype=BF16,
                                   name="kv_proj")
                kvs["fp"] = _mm_nn(h_bf, kv_w[:, 2 * d:], tn=LANES, out_dtype=F32, name="f_proj")
                kvs["c"], ct = _fgate_fwd(kvs["fp"], wts["kv_fb"], tq=tq, name="fgate_fwd")
                kvs["ct"] = ct[:N_HEADS]
            lw["in"] = hooks.weight(layer, "in", kvs["c"] if j == 0 else h)
            sv["qg"] = _proj_in(h_bf, lw["in"], shard_major=False, name="b_in_proj")
            sv["o"], y3 = _attn_fwd(sv["qg"], kvs["kv"], kvs["c"], kvs["ct"], tq=tq, name="attn_fwd")
        sv["y3"] = y3
        lw["out"] = hooks.weight(layer, "out", y3)
        sv["s1"], h, h_bf = _out_ln(y3, lw["out"], h, wts["ln1_g"][layer], wts["ln1_b"][layer], tm=tm,
                                    name="mix_out_ln")
        sv["h1_bf"] = h_bf
        lw["f_in"] = hooks.weight(layer, "f_in", h)
        sv["z3"] = _proj_in(h_bf, lw["f_in"], shard_major=True, name="f_in_proj")
        sv["yf3"] = _convglu_fwd(sv["z3"], wts["f_cwb3"][layer], name="f_convglu_fwd")
        lw["f_out"] = hooks.weight(layer, "f_out", sv["yf3"])
        sv["s2"], h, h_bf = _out_ln(sv["yf3"], lw["f_out"], h, wts["ln2_g"][layer], wts["ln2_b"][layer],
                                    tm=tm, name="ffn_out_ln")
        saved.append(sv)

    loss_tile, dh = _loss_bwd(h, tgt, lo=n_meta, hi=n_tok, tm=tm, name="loss")

    grads = {k: [None] * N_LAYERS for k in ("f_cwb3", "ln1_gb", "ln2_gb")}
    grads.update({k: [None] * N_A_LAYERS for k in ("a_cwb", "a_bd_r", "a_bd_i", "a_vecs")})
    dkv = []
    token = jnp.zeros((), F32)
    for layer in reversed(range(N_LAYERS)):
        sv = saved[layer]
        lw = sv["w"]
        big = {}
        ds, ds_bf, grads["ln2_gb"][layer] = _ln_bwd(dh, sv["s2"], wts["ln2_g"][layer] + token, tm=tm, name="ln_bwd")
        dz, dcw = _ffn_bwd_mid(ds_bf, lw["f_out"], sv["z3"], wts["f_cwb3"][layer], name="f_bwd_mid")
        grads["f_cwb3"][layer] = dcw.reshape((N_DEV,) + dcw.shape[2:])
        dz3 = dz.reshape((N_DEV,) + dz.shape[2:])
        big["f_out"] = _w_out_grad(sv["yf3"], ds_bf, lw["f_out"].shape[1], name="f_w_out_grad")
        dh = _in_bwd(dz3, lw["f_in"], ds, tm=tmb, name="f_in_bwd")
        big["f_in"] = _w_in_grad(sv["h1_bf"], dz3, name="f_w_in_grad")
        token = hooks.grads_ready(layer, "ffn", big)
        big = {}
        ds, ds_bf, grads["ln1_gb"][layer] = _ln_bwd(dh, sv["s1"], wts["ln1_g"][layer] + token, tm=tm, name="ln_bwd")
        if layer < N_A_LAYERS:
            dy = _out_bwd(ds_bf, lw["out"], tm=tmb // 2, name="a_out_bwd")
            big["out"] = _w_out_grad(sv["y3"], ds_bf, lw["out"].shape[1], name="a_w_out_grad")
            d_h, d_a, dgate = _scan_bwd(dy, sv["gr"], sv["hr"], sv["a"], cb=sb, name="a_scan_bwd")
            d_rec, dpr, dpi, grads["a_vecs"][layer] = _gates_bwd(
                sv["rec"], sv["r"], sv["i"], sv["a"], d_h, d_a, wts["a_bd_r"][layer], wts["a_bd_i"][layer],
                wts["a_vecs"][layer], tm=tm, name="a_gates_bwd")
            grads["a_bd_r"][layer], grads["a_bd_i"][layer] = _bd_grad(sv["rec"], dpr, dpi, cb=cb, name="a_bd_grad")
            dact, grads["a_cwb"][layer] = _conv_a_bwd(d_rec, sv["gr"], dgate, wts["a_cwb"][layer], cb=cb,
                                                      name="a_conv_bwd")
            dh = _in_bwd(dact, lw["in"], ds, tm=tmb, name="a_in_bwd")
            big["in"] = _w_in_grad(sv["h_bf"], dact, name="a_w_in_grad")
        else:
            j = layer - N_A_LAYERS
            dy = _out_bwd(ds_bf, lw["out"], tm=tmb // 2, name="b_out_bwd")
            big["out"] = _w_out_grad(sv["y3"], ds_bf, lw["out"].shape[1], name="b_w_out_grad")
            dqg, dk, dv, dc = _attn_bwd(dy, sv["qg"], sv["o"], kvs["kv"], kvs["c"], kvs["ct"], tq=tq,
                                        name="attn_bwd")
            dkv.append((dk, dv, dc))
            dh = _in_bwd(dqg, lw["in"], ds, tm=tmb, name="b_in_bwd")
            big["in"] = _w_in_grad(sv["h_bf"], dqg, name="b_w_in_grad")
            if j == 0:
                hpb = LANES // (d // N_HEADS)
                dct = (dkv[0][2] + dkv[1][2])[:, :hpb, :].reshape(N_HEADS, tp)
                dct = jnp.concatenate([dct, jnp.zeros((LANES - N_HEADS, tp), F32)])
                df_bf, grads["kv_fb"] = _fgate_bwd(dct, kvs["fp"], wts["kv_fb"], tq=tq, name="fgate_bwd")
                dkvz = jnp.concatenate([_pair_sum(dkv[0][0], dkv[1][0], tm=tm, name="kv_pair_sum"),
                                        _pair_sum(dkv[0][1], dkv[1][1], tm=tm, name="kv_pair_sum"), df_bf], axis=1)
                dh = _mm_nt_full(dkvz, kvs["w"], dh, tm=tmb // 2, name="kv_in_bwd")
                big["kv_w"] = _mm_tn_cols(kvs["h_bf"], dkvz, tn=LANES, name="kv_w_grad")
        token = hooks.grads_ready(layer, "mix", big)
    return loss_tile, dh, grads


def _finish_small_grads(grads, d_h0, n_meta):
    r_dim = grads["a_vecs"][0].shape[1]
    per = _lru_block_cols(r_dim) // (r_dim // N_LRU_BLOCKS)
    a_cwb = jnp.stack(grads["a_cwb"])
    a_vecs = jnp.stack(grads["a_vecs"])
    f_cwb3 = jnp.stack(grads["f_cwb3"])
    ln1 = jnp.stack(grads["ln1_gb"])
    ln2 = jnp.stack(grads["ln2_gb"])
    f_rows = f_cwb3.transpose(0, 2, 1, 3).reshape(N_LAYERS, 8, -1)
    return {
        "meta": d_h0[:n_meta],
        "a_conv_w": a_cwb[:, :4], "a_conv_b": a_cwb[:, 4],
        "a_w_r": jnp.stack([_block_diag_extract(g, per) for g in grads["a_bd_r"]]),
        "a_b_r": a_vecs[:, 0],
        "a_w_i": jnp.stack([_block_diag_extract(g, per) for g in grads["a_bd_i"]]),
        "a_b_i": a_vecs[:, 1], "a_lambda": a_vecs[:, 2],
        "kv_f_b": grads["kv_fb"][0, :N_HEADS],
        "f_conv_w": f_rows[:, :3], "f_conv_b": f_rows[:, 3],
        "ln1_g": ln1[:, 0], "ln1_b": ln1[:, 1], "ln2_g": ln2[:, 0], "ln2_b": ln2[:, 1],
    }


def kernel(x, meta, a_w_in, a_conv_w, a_conv_b, a_w_r, a_b_r, a_w_i, a_b_i, a_lambda, a_w_out, kv_w, kv_f_b, b_w_in, b_w_out, f_w_in, f_conv_w, f_conv_b, f_w_out, ln1_g, ln1_b, ln2_g, ln2_b, loss_target, m_meta, m_a_w_in, m_a_conv_w, m_a_conv_b, m_a_w_r, m_a_b_r, m_a_w_i, m_a_b_i, m_a_lambda, m_a_w_out, m_kv_w, m_kv_f_b, m_b_w_in, m_b_w_out, m_f_w_in, m_f_conv_w, m_f_conv_b, m_f_w_out, m_ln1_g, m_ln1_b, m_ln2_g, m_ln2_b, v_meta, v_a_w_in, v_a_conv_w, v_a_conv_b, v_a_w_r, v_a_b_r, v_a_w_i, v_a_b_i, v_a_lambda, v_a_w_out, v_kv_w, v_kv_f_b, v_b_w_in, v_b_w_out, v_f_w_in, v_f_conv_w, v_f_conv_b, v_f_w_out, v_ln1_g, v_ln1_b, v_ln2_g, v_ln2_b):
    w = dict(meta=meta, a_w_in=a_w_in, a_conv_w=a_conv_w, a_conv_b=a_conv_b, a_w_r=a_w_r, a_b_r=a_b_r, a_w_i=a_w_i,
             a_b_i=a_b_i, a_lambda=a_lambda, a_w_out=a_w_out, kv_w=kv_w, kv_f_b=kv_f_b, b_w_in=b_w_in,
             b_w_out=b_w_out, f_w_in=f_w_in, f_conv_w=f_conv_w, f_conv_b=f_conv_b, f_w_out=f_w_out, ln1_g=ln1_g,
             ln1_b=ln1_b, ln2_g=ln2_g, ln2_b=ln2_b)
    m = dict(meta=m_meta, a_w_in=m_a_w_in, a_conv_w=m_a_conv_w, a_conv_b=m_a_conv_b, a_w_r=m_a_w_r, a_b_r=m_a_b_r,
             a_w_i=m_a_w_i, a_b_i=m_a_b_i, a_lambda=m_a_lambda, a_w_out=m_a_w_out, kv_w=m_kv_w, kv_f_b=m_kv_f_b,
             b_w_in=m_b_w_in, b_w_out=m_b_w_out, f_w_in=m_f_w_in, f_conv_w=m_f_conv_w, f_conv_b=m_f_conv_b,
             f_w_out=m_f_w_out, ln1_g=m_ln1_g, ln1_b=m_ln1_b, ln2_g=m_ln2_g, ln2_b=m_ln2_b)
    v = dict(meta=v_meta, a_w_in=v_a_w_in, a_conv_w=v_a_conv_w, a_conv_b=v_a_conv_b, a_w_r=v_a_w_r, a_b_r=v_a_b_r,
             a_w_i=v_a_w_i, a_b_i=v_a_b_i, a_lambda=v_a_lambda, a_w_out=v_a_w_out, kv_w=v_kv_w, kv_f_b=v_kv_f_b,
             b_w_in=v_b_w_in, b_w_out=v_b_w_out, f_w_in=v_f_w_in, f_conv_w=v_f_conv_w, f_conv_b=v_f_conv_b,
             f_w_out=v_f_w_out, ln1_g=v_ln1_g, ln1_b=v_ln1_b, ln2_g=v_ln2_g, ln2_b=v_ln2_b)
    shapes = {n: w[n].shape for n in WEIGHT_NAMES}

    me = jnp.reshape(_my_index(), (1,)).astype(jnp.int32)
    param_of = {"in": ("a_w_in", "b_w_in"), "out": ("a_w_out", "b_w_out"), "f_in": ("f_w_in",) * 2,
                "f_out": ("f_w_out",) * 2}
    order = [("small", None, None)]
    for layer in range(N_LAYERS):
        if layer == N_A_LAYERS:
            order.append(("kv_w", layer, 0))
        for key in ("in", "out", "f_in", "f_out"):
            order.append((key, layer, layer if key[0] == "f" or layer < N_A_LAYERS else layer - N_A_LAYERS))
    lands = []
    for key, layer, idx in order:
        if key == "small":
            lands.append(_place_own(_pack([w[n] for n in COL_F32], F32)[None], 0, me, out_dtype=F32,
                                    name="place_small"))
        elif key == "kv_w":
            lands.append(_place_own(w["kv_w"][None], 0, me, out_dtype=BF16, name="place_kv_w"))
        else:
            name = param_of[key][0 if layer < N_A_LAYERS else 1]
            lands.append(_place_own(w[name], idx, me, out_dtype=BF16, name=f"place_{name}_{idx}"))
    gather_handles, gather_token = _split_start([([l], [l]) for l in lands], scatter=False, name="gather_start")
    group_of = {(key, layer): g for g, (key, layer, _) in enumerate(order)}
    (got_s,) = _split_wait(gather_handles[0], gather_token, scatter=False, name="gather_wait_small")
    small = {n: w[n] for n in REPLICATED}
    for n, part in zip(COL_F32, _unpack(got_s, [w[n].shape for n in COL_F32], 1)):
        small[n] = _cols_full(part)
    n_meta, d = small["meta"].shape

    class Hooks:
        pending = None
        received = {}
        sent = {}

        @staticmethod
        def weight(layer, key, after):
            (got,) = _split_wait(gather_handles[group_of[(key, layer)]], after, scatter=False,
                                 name=f"gather_wait_{key}_{layer}")
            return got

        @staticmethod
        def collect(after):
            if Hooks.pending is not None:
                tag, names, handle = Hooks.pending
                got = _split_wait(handle, after, scatter=True, name=f"scatter_wait_{tag}")
                Hooks.received.update(zip(names, got))
                Hooks.pending = None

        @staticmethod
        def grads_ready(layer, part, big):
            if "kv_w" in big:
                big["kv_w"] = _cols_split(big["kv_w"][:, :shapes["kv_w"][1] * N_DEV]).astype(BF16)
            names = [(key, layer) for key in big]
            send = [big[key] for key in big]
            Hooks.collect(send[0])
            empty = [lax.empty(s.shape, s.dtype) for s in send]
            handles, token = _split_start([(send, empty)], scatter=True, name=f"scatter_start_{part}_{layer}")
            Hooks.pending = (f"{part}_{layer}", names, handles[0])
            Hooks.sent.update(zip(names, handles[0][2]))
            return token[0, 0]

    Hooks.pending, Hooks.received, Hooks.sent = None, {}, {}

    n_tok = n_meta + x.shape[1]
    tp = -(-n_tok // ROW_ALIGN) * ROW_ALIGN
    pad = jnp.zeros((tp - n_tok, d), F32)
    h0 = jnp.concatenate([small["meta"], x[0], pad])
    tgt = jnp.concatenate([jnp.zeros((n_meta, d), F32), loss_target[0], pad])
    loss_tile, d_h0, grads = _local_step(h0, tgt, n_meta, n_tok, _small_layouts(small), Hooks)
    g_small = _finish_small_grads(grads, d_h0, n_meta)
    loss = lax.psum(loss_tile[0, 0], MESH_AXES)
    grad_x = d_h0[n_meta:n_tok][None]

    rep = _pack([g_small[n] for n in REPLICATED], F32, row_quantum=16 * N_DEV)
    send = [_pack8([_cols_split(g_small[n]) for n in COL_F32], F32), rep.reshape(N_DEV, -1, PACK_COLS)]
    lands = _own_blocks(send, name="scatter_own_small")
    handles, token = _split_start([(send, lands)], scatter=True, name="scatter_start_small")
    Hooks.collect(token)
    recv_s, recv_r = _split_wait(handles[0], token, scatter=True, name="scatter_wait_small")

    g, delta, new_m, new_v = {}, {}, {}, {}
    layers_of = {
        "a_w_in": [("in", l) for l in range(N_A_LAYERS)], "a_w_out": [("out", l) for l in range(N_A_LAYERS)],
        "b_w_in": [("in", l) for l in range(N_A_LAYERS, N_LAYERS)],
        "b_w_out": [("out", l) for l in range(N_A_LAYERS, N_LAYERS)],
        "f_w_in": [("f_in", l) for l in range(N_LAYERS)], "f_w_out": [("f_out", l) for l in range(N_LAYERS)],
        "kv_w": [("kv_w", N_A_LAYERS)],
    }
    for n in BIG + ("kv_w",):
        lift = (lambda a: a) if n != "kv_w" else (lambda a: a[None])
        outs = _sum_adamw([Hooks.received[t] for t in layers_of[n]], [Hooks.sent[t] for t in layers_of[n]], me,
                          lift(w[n]), lift(m[n]), lift(v[n]), name="sum_adamw_" + n)
        g[n], delta[n], new_m[n], new_v[n] = [o.reshape(shapes[n]) for o in outs]
    sum_s = _sum8(recv_s, name="sum_grads_f32")
    g.update(zip(COL_F32, _unpack(sum_s, [shapes[n] for n in COL_F32], 0)))
    (got_r,) = _all_gather([_sum8(recv_r, name="sum_grads_replicated")], name="gather_replicated_sums")
    g.update(zip(REPLICATED, _unpack(got_r.reshape(-1, PACK_COLS), [shapes[n] for n in REPLICATED], 0)))

    for n in COL_F32 + REPLICATED:
        shp = shapes[n]
        dl, nm, nv = _adamw(_slab2d(w[n]), _slab2d(g[n]), _slab2d(m[n]), _slab2d(v[n]), name="adamw")
        delta[n], new_m[n], new_v[n] = dl.reshape(shp), nm.reshape(shp), nv.reshape(shp)
    return (loss, grad_x, *[g[n] for n in WEIGHT_NAMES], *[delta[n] for n in WEIGHT_NAMES],
            *[new_m[n] for n in WEIGHT_NAMES], *[new_v[n] for n in WEIGHT_NAMES])
```

```python
import math

import jax
import jax.numpy as jnp
from jax import lax
from jax.experimental import pallas as pl
from jax.experimental.pallas import tpu as pltpu

F32 = jnp.float32
BF16 = jnp.bfloat16

N_DEV = 8
MESH_AXES = ("x", "y", "c")
N_LAYERS = 4
N_A_LAYERS = 2
N_LRU_BLOCKS = 16
N_HEADS = 16
LRU_C = 8.0
DN_ALPHA = (2 * N_LAYERS) ** 0.25
LN_EPS = 1e-5
ADAM_LR, ADAM_B1, ADAM_B2, ADAM_EPS, ADAM_WD, ADAM_STEP = 0.001, 0.9, 0.999, 1e-08, 0.01, 10

LANES = 128
SUBLANES = 8
ROW_ALIGN = 128
VMEM_LIMIT_BYTES = 56 * 1024 * 1024
GELU_K = math.sqrt(2.0 / math.pi)
GELU_C = 0.044715
PACK_COLS = 1024


def _params(*sem):
    return pltpu.CompilerParams(dimension_semantics=sem, vmem_limit_bytes=VMEM_LIMIT_BYTES)


def _gelu(x):
    th = jnp.tanh(GELU_K * (x + GELU_C * x * x * x))
    return 0.5 * x * (1.0 + th)


def _gelu_and_grad(x):
    x2 = x * x
    th = jnp.tanh(GELU_K * (x + GELU_C * x2 * x))
    g = 0.5 * x * (1.0 + th)
    dg = 0.5 * (1.0 + th) + 0.5 * x * (1.0 - th * th) * (GELU_K * (1.0 + 3.0 * GELU_C * x2))
    return g, dg


def _sigmoid(x):
    return 1.0 / (1.0 + jnp.exp(-x))


def _expm1(x):
    small = x * (1.0 + 0.5 * x * (1.0 + (1.0 / 3.0) * x * (1.0 + 0.25 * x)))
    return jnp.where(jnp.abs(x) < 1e-2, small, jnp.exp(x) - 1.0)


def _softplus(x):
    e = jnp.exp(-jnp.abs(x))
    small = e * (1.0 - 0.5 * e * (1.0 - (2.0 / 3.0) * e))
    return jnp.maximum(x, 0.0) + jnp.where(e < 1e-2, small, jnp.log(1.0 + e))


def _shift_down(x, s):
    if s == 0:
        return x
    rows = lax.broadcasted_iota(jnp.int32, x.shape, 0)
    return jnp.where(rows >= s, pltpu.roll(x, s, 0), 0.0)


def _shift_up(x, s):
    if s == 0:
        return x
    n = x.shape[0]
    rows = lax.broadcasted_iota(jnp.int32, x.shape, 0)
    return jnp.where(rows < n - s, pltpu.roll(x, n - s, 0), 0.0)


def _dot_nn(a, b):
    return lax.dot_general(a, b, (((1,), (0,)), ((), ())), preferred_element_type=F32)


def _dot_nt(a, b):
    return lax.dot_general(a, b, (((1,), (1,)), ((), ())), preferred_element_type=F32)


def _dot_tn(a, b):
    return lax.dot_general(a, b, (((0,), (0,)), ((), ())), preferred_element_type=F32)


def _rows8(vals, width):
    rows = lax.broadcasted_iota(jnp.int32, (8, width), 0)
    out = jnp.zeros((8, width), F32)
    for k, v in enumerate(vals):
        out = jnp.where(rows == k, jnp.broadcast_to(v, (8, width)), out)
    return out


def _tile(n, prefer):
    for c in prefer:
        if n % c == 0:
            return c
    return n


def _mm_nn(a, b, *, tn, out_dtype, name):
    m, k = a.shape
    n = b.shape[1]

    def body(a_ref, b_ref, o_ref):
        o_ref[...] = _dot_nn(a_ref[...], b_ref[...]).astype(o_ref.dtype)

    return pl.pallas_call(
        body, name=name, grid=(n // tn,),
        in_specs=[pl.BlockSpec((m, k), lambda j: (0, 0)), pl.BlockSpec((k, tn), lambda j: (0, j))],
        out_specs=pl.BlockSpec((m, tn), lambda j: (0, j)),
        out_shape=jax.ShapeDtypeStruct((m, n), out_dtype),
        compiler_params=_params("parallel"),
    )(a, b)


def _proj_in(h_bf, g_in, *, shard_major, name, transposed=False):
    t, k = h_bf.shape
    n = g_in.shape[1] if transposed else g_in.shape[2]

    def body(a_ref, b_ref, o_ref):
        o_ref[...] = _dot_nt(a_ref[...], b_ref[...]) if transposed else _dot_nn(a_ref[...], b_ref[...])

    if shard_major:
        out_spec = pl.BlockSpec((None, t, n), lambda j: (j, 0, 0))
        out_shape = jax.ShapeDtypeStruct((N_DEV, t, n), F32)
    else:
        out_spec = pl.BlockSpec((t, n), lambda j: (0, j))
        out_shape = jax.ShapeDtypeStruct((t, N_DEV * n), F32)
    return pl.pallas_call(
        body, name=name, grid=(N_DEV,),
        in_specs=[pl.BlockSpec((t, k), lambda j: (0, 0)),
                  pl.BlockSpec((None,) + g_in.shape[1:], lambda j: (j, 0, 0))],
        out_specs=out_spec, out_shape=out_shape,
        compiler_params=_params("parallel"),
    )(h_bf, g_in)


def _out_ln(y3, g_out, hin, g, b, *, n_valid, tm, name):
    nj, t, kj = y3.shape
    _, r, d = g_out.shape

    def body(y_ref, w_ref, hin_ref, g_ref, b_ref, s_ref, h_ref, hb_ref):
        w = w_ref[...].reshape(N_DEV * r, d)
        s = DN_ALPHA * hin_ref[...]
        for jj in range(nj):
            s = s + _dot_nn(y_ref[jj], w[jj * kj:(jj + 1) * kj])
        mu = jnp.mean(s, axis=-1, keepdims=True)
        xc = s - mu
        var = jnp.mean(xc * xc, axis=-1, keepdims=True)
        h = xc * lax.rsqrt(var + LN_EPS) * g_ref[...] + b_ref[...]
        s_ref[...] = s
        h_ref[...] = h
        rows = pl.program_id(0) * tm + lax.broadcasted_iota(jnp.int32, (tm, d), 0)
        hb_ref[...] = jnp.where(rows < n_valid, h, 0.0).astype(BF16)

    row = pl.BlockSpec((tm, d), lambda i: (i, 0))
    vec = pl.BlockSpec((1, d), lambda i: (0, 0))
    return pl.pallas_call(
        body, name=name, grid=(t // tm,),
        in_specs=[pl.BlockSpec((nj, tm, kj), lambda i: (0, i, 0)),
                  pl.BlockSpec((N_DEV, r, d), lambda i: (0, 0, 0)), row, vec, vec],
        out_specs=[row, row, row],
        out_shape=[jax.ShapeDtypeStruct((t, d), F32), jax.ShapeDtypeStruct((t, d), F32),
                   jax.ShapeDtypeStruct((t, d), BF16)],
        compiler_params=_params("parallel"),
    )(y3, g_out, hin, g, b)


def _out_bwd(ds_bf, g_out, *, tm, name):
    t, d = ds_bf.shape
    r = g_out.shape[1]

    def body(a_ref, w_ref, o_ref):
        o_ref[...] = _dot_nt(a_ref[...], w_ref[...].reshape(N_DEV * r, d))

    return pl.pallas_call(
        body, name=name, grid=(t // tm,),
        in_specs=[pl.BlockSpec((tm, d), lambda i: (i, 0)),
                  pl.BlockSpec((N_DEV, r, d), lambda i: (0, 0, 0))],
        out_specs=pl.BlockSpec((tm, N_DEV * r), lambda i: (i, 0)),
        out_shape=jax.ShapeDtypeStruct((t, N_DEV * r), F32),
        compiler_params=_params("parallel"),
    )(ds_bf, g_out)


def _in_bwd(dact, g_in, add, *, tm, name, alpha=DN_ALPHA, transposed=False):
    t = dact.shape[1]
    _, k, n = g_in.shape
    if transposed:
        k, n = n, k
    halves = dact.shape[0] == 2
    per = N_DEV // 2

    def body(a_ref, b_ref, add_ref, o_ref, acc_ref):
        j = pl.program_id(1)

        @pl.when(j == 0)
        def _():
            acc_ref[...] = alpha * add_ref[...]

        acc_ref[...] += _dot_nn(a_ref[...], b_ref[...]) if transposed else _dot_nt(a_ref[...], b_ref[...])

        @pl.when(j == N_DEV - 1)
        def _():
            o_ref[...] = acc_ref[...]

    if halves:
        a_spec = pl.BlockSpec((None, tm, n), lambda i, j: (j // per, i, j % per))
    else:
        a_spec = pl.BlockSpec((None, tm, n), lambda i, j: (j, i, 0))
    return pl.pallas_call(
        body, name=name, grid=(t // tm, N_DEV),
        in_specs=[a_spec, pl.BlockSpec((None,) + g_in.shape[1:], lambda i, j: (j, 0, 0)),
                  pl.BlockSpec((tm, k), lambda i, j: (i, 0))],
        out_specs=pl.BlockSpec((tm, k), lambda i, j: (i, 0)),
        out_shape=jax.ShapeDtypeStruct((t, k), F32),
        scratch_shapes=[pltpu.VMEM((tm, k), F32)],
        compiler_params=_params("parallel", "arbitrary"),
    )(dact, g_in, add)


def _mm_nt_full(a, b, add, *, tm, name):
    t, n = a.shape
    k = b.shape[0]

    def body(a_ref, b_ref, add_ref, o_ref):
        o_ref[...] = add_ref[...] + _dot_nt(a_ref[...], b_ref[...])

    return pl.pallas_call(
        body, name=name, grid=(t // tm,),
        in_specs=[pl.BlockSpec((tm, n), lambda i: (i, 0)), pl.BlockSpec((k, n), lambda i: (0, 0)),
                  pl.BlockSpec((tm, k), lambda i: (i, 0))],
        out_specs=pl.BlockSpec((tm, k), lambda i: (i, 0)),
        out_shape=jax.ShapeDtypeStruct((t, k), F32),
        compiler_params=_params("parallel"),
    )(a, b, add)


def _w_in_grad(h_bf, dact, *, name, transposed=False):
    t, k = h_bf.shape
    halves = dact.shape[0] == 2
    per = N_DEV // 2
    n = dact.shape[2] // per if halves else dact.shape[2]

    def body(a_ref, b_ref, o_ref):
        if transposed:
            o_ref[...] = _dot_tn(b_ref[...], a_ref[...]).astype(BF16)
        else:
            o_ref[...] = _dot_tn(a_ref[...], b_ref[...]).astype(BF16)

    if halves:
        b_spec = pl.BlockSpec((None, t, n), lambda j: (j // per, 0, j % per))
    else:
        b_spec = pl.BlockSpec((None, t, n), lambda j: (j, 0, 0))
    return pl.pallas_call(
        body, name=name, grid=(N_DEV,),
        in_specs=[pl.BlockSpec((t, k), lambda j: (0, 0)), b_spec],
        out_specs=pl.BlockSpec((None, n, k) if transposed else (None, k, n), lambda j: (j, 0, 0)),
        out_shape=jax.ShapeDtypeStruct((N_DEV, n, k) if transposed else (N_DEV, k, n), BF16),
        compiler_params=_params("parallel"),
    )(h_bf, dact)


def _w_out_grad(y3, ds_bf, r, *, name):
    nj, t, kj = y3.shape
    d = ds_bf.shape[1]
    unit = r * LANES // math.gcd(r, LANES)
    ks = max([c for c in range(unit, min(kj, 768) + 1, unit) if kj % c == 0], default=kj)
    gsz = ks // r
    per = kj // ks

    def body(a_ref, b_ref, o_ref):
        o_ref[...] = _dot_tn(a_ref[...], b_ref[...]).reshape(gsz, r, d).astype(BF16)

    return pl.pallas_call(
        body, name=name, grid=(nj * per,),
        in_specs=[pl.BlockSpec((None, t, ks), lambda j: (j // per, 0, j % per)),
                  pl.BlockSpec((t, d), lambda j: (0, 0))],
        out_specs=pl.BlockSpec((gsz, r, d), lambda j: (j, 0, 0)),
        out_shape=jax.ShapeDtypeStruct((N_DEV, r, d), BF16),
        compiler_params=_params("parallel"),
    )(y3, ds_bf)


def _mm_tn_cols(a, b, *, tn, name):
    t, m = a.shape
    n = b.shape[1]

    def body(a_ref, b_ref, o_ref):
        o_ref[...] = _dot_tn(a_ref[...], b_ref[...])

    return pl.pallas_call(
        body, name=name, grid=(n // tn,),
        in_specs=[pl.BlockSpec((t, m), lambda j: (0, 0)), pl.BlockSpec((t, tn), lambda j: (0, j))],
        out_specs=pl.BlockSpec((m, tn), lambda j: (0, j)),
        out_shape=jax.ShapeDtypeStruct((m, n), F32),
        compiler_params=_params("parallel"),
    )(a, b)


def _ln_bwd(dout, s, g, *, tm, name):
    t, d = s.shape

    def body(do_ref, s_ref, g_ref, ds_ref, dsb_ref, gb_ref):
        i = pl.program_id(0)
        sv = s_ref[...]
        do = do_ref[...]
        mu = jnp.mean(sv, axis=-1, keepdims=True)
        xc = sv - mu
        var = jnp.mean(xc * xc, axis=-1, keepdims=True)
        rstd = lax.rsqrt(var + LN_EPS)
        xhat = xc * rstd
        dxhat = do * g_ref[...]
        m1 = jnp.mean(dxhat, axis=-1, keepdims=True)
        m2 = jnp.mean(dxhat * xhat, axis=-1, keepdims=True)
        ds = rstd * (dxhat - m1 - xhat * m2)
        ds_ref[...] = ds
        dsb_ref[...] = ds.astype(BF16)
        upd = _rows8([jnp.sum(do * xhat, axis=0, keepdims=True), jnp.sum(do, axis=0, keepdims=True)], d)

        @pl.when(i == 0)
        def _():
            gb_ref[...] = upd

        @pl.when(i > 0)
        def _():
            gb_ref[...] += upd

    row = pl.BlockSpec((tm, d), lambda i: (i, 0))
    return pl.pallas_call(
        body, name=name, grid=(t // tm,),
        in_specs=[row, row, pl.BlockSpec((1, d), lambda i: (0, 0))],
        out_specs=[row, row, pl.BlockSpec((8, d), lambda i: (0, 0))],
        out_shape=[jax.ShapeDtypeStruct((t, d), F32), jax.ShapeDtypeStruct((t, d), BF16),
                   jax.ShapeDtypeStruct((8, d), F32)],
        compiler_params=_params("arbitrary"),
    )(dout, s, g)


def _roll_down(x, s):
    return x if s == 0 else pltpu.roll(x, s, 0)


def _conv_taps(x, wb, width):
    y = jnp.broadcast_to(wb[width:width + 1, :], x.shape)
    for k in range(width):
        y = y + _roll_down(x, width - 1 - k) * wb[k:k + 1, :]
    return y


def _conv_taps_bwd(dy, x, wb, width):
    n = dy.shape[0]
    dx = jnp.zeros_like(dy)
    rows = []
    for k in range(width):
        s = width - 1 - k
        dx = dx + (dy if s == 0 else pltpu.roll(dy, n - s, 0)) * wb[k:k + 1, :]
        rows.append(jnp.sum(dy * _roll_down(x, s), axis=0, keepdims=True))
    rows.append(jnp.sum(dy, axis=0, keepdims=True))
    t_idx = lax.broadcasted_iota(jnp.int32, dy.shape, 0)
    return jnp.where(t_idx < n - (width - 1), dx, 0.0), _rows8(rows, dy.shape[1])


def _convglu_fwd(z3, fwb3, *, name):
    _, t, n = z3.shape
    half = N_DEV // 2
    nc = pl.cdiv(n, LANES)

    def body(zg_ref, zv_ref, wg_ref, wv_ref, y_ref):
        gate = _conv_taps(zg_ref[...], wg_ref[...], 3)
        val = _conv_taps(zv_ref[...], wv_ref[...], 3)
        y_ref[...] = (_gelu(gate) * val).astype(BF16)

    zblk = lambda off: pl.BlockSpec((None, t, LANES), lambda j, c: (j + off, 0, c))
    wblk = lambda off: pl.BlockSpec((None, 8, LANES), lambda j, c: (j + off, 0, c))
    return pl.pallas_call(
        body, name=name, grid=(half, nc),
        in_specs=[zblk(0), zblk(half), wblk(0), wblk(half)],
        out_specs=zblk(0),
        out_shape=jax.ShapeDtypeStruct((half, t, n), BF16),
        compiler_params=_params("parallel", "parallel"),
    )(z3, z3, fwb3, fwb3)


def _ffn_bwd_mid(ds_bf, g_out, z3, fwb3, *, name):
    t, d = ds_bf.shape
    r = g_out.shape[1]
    n = z3.shape[2]
    half = N_DEV // 2
    nc = pl.cdiv(n, LANES)
    assert n == 2 * r

    def body(ds_ref, w_ref, zg_ref, zv_ref, wg_ref, wv_ref, dz_ref, dwb_ref, wsc_ref):
        c = pl.program_id(1)

        @pl.when(c == 0)
        def _():
            wsc_ref[0:r, :] = w_ref[0]
            wsc_ref[r:2 * r, :] = w_ref[1]
            if nc * LANES > n:
                wsc_ref[n:nc * LANES, :] = jnp.zeros((nc * LANES - n, d), BF16)

        w = wsc_ref[pl.ds(pl.multiple_of(c * LANES, LANES), LANES), :]
        dyf = _dot_nt(ds_ref[...], w)
        zg, zv = zg_ref[...], zv_ref[...]
        wg, wv = wg_ref[...], wv_ref[...]
        gate = _conv_taps(zg, wg, 3)
        val = _conv_taps(zv, wv, 3)
        gl, dgl = _gelu_and_grad(gate)
        dzg, dwg = _conv_taps_bwd(dyf * val * dgl, zg, wg, 3)
        dzv, dwv = _conv_taps_bwd(dyf * gl, zv, wv, 3)
        dz_ref[0] = dzg.astype(BF16)
        dz_ref[1] = dzv.astype(BF16)
        dwb_ref[0] = dwg
        dwb_ref[1] = dwv

    zblk = lambda off: pl.BlockSpec((None, t, LANES), lambda j, c: (j + off, 0, c))
    wblk = lambda off: pl.BlockSpec((None, 8, LANES), lambda j, c: (j + off, 0, c))
    return pl.pallas_call(
        body, name=name, grid=(half, nc),
        in_specs=[pl.BlockSpec((t, d), lambda j, c: (0, 0)),
                  pl.BlockSpec((2, r, d), lambda j, c: (j, 0, 0)),
                  zblk(0), zblk(half), wblk(0), wblk(half)],
        out_specs=[pl.BlockSpec((2, None, t, LANES), lambda j, c: (0, j, 0, c)),
                   pl.BlockSpec((2, None, 8, LANES), lambda j, c: (0, j, 0, c))],
        out_shape=[jax.ShapeDtypeStruct((2, half, t, n), BF16), jax.ShapeDtypeStruct((2, half, 8, n), F32)],
        scratch_shapes=[pltpu.VMEM((nc * LANES, d), BF16)],
        compiler_params=_params("parallel", "arbitrary"),
    )(ds_bf, g_out, z3, z3, fwb3, fwb3)


def _conv_a_fwd(gr, cwb, *, cb, name):
    t, r2 = gr.shape
    r = r2 // 2
    nb = r // cb

    def body(x_ref, w_ref, o_ref):
        o_ref[...] = _conv_taps(x_ref[...], w_ref[...], 4)

    return pl.pallas_call(
        body, name=name, grid=(nb,),
        in_specs=[pl.BlockSpec((t, cb), lambda j: (0, j + nb)), pl.BlockSpec((8, cb), lambda j: (0, j))],
        out_specs=pl.BlockSpec((t, cb), lambda j: (0, j)),
        out_shape=jax.ShapeDtypeStruct((t, r), F32),
        compiler_params=_params("parallel"),
    )(gr, cwb)


def _gates_fwd(rec, bd_r, bd_i, vecs, *, tm, name):
    t, r_dim = rec.shape
    nb, cb, _ = bd_r.shape

    def body(x_ref, wr_ref, wi_ref, v_ref, a_ref, u_ref, r_ref, i_ref):
        x = x_ref[...]
        xb = x.astype(BF16)
        v = v_ref[...]
        r = _sigmoid(_dot_nn(xb, wr_ref[...]) + v[0:1, :])
        i = _sigmoid(_dot_nn(xb, wi_ref[...]) + v[1:2, :])
        log_a = (-LRU_C) * r * _softplus(-v[2:3, :])
        a_ref[...] = jnp.exp(log_a)
        u_ref[...] = jnp.sqrt(-_expm1(2.0 * log_a)) * (i * x)
        r_ref[...] = r
        i_ref[...] = i

    blk = pl.BlockSpec((tm, cb), lambda j, i: (i, j))
    wspec = pl.BlockSpec((None, cb, cb), lambda j, i: (j, 0, 0))
    out = jax.ShapeDtypeStruct((t, r_dim), F32)
    return pl.pallas_call(
        body, name=name, grid=(nb, t // tm),
        in_specs=[blk, wspec, wspec, pl.BlockSpec((8, cb), lambda j, i: (0, j))],
        out_specs=[blk, blk, blk, blk],
        out_shape=[out, out, out, out],
        compiler_params=_params("parallel", "parallel"),
    )(rec, bd_r, bd_i, vecs)


def _scan_fwd(a, u, gr, *, cb, name):
    t, r = a.shape
    nb = r // cb
    seg = t // SUBLANES

    def body(a_ref, u_ref, g_ref, h_ref, y_ref, p_ref):
        def step(k, carry):
            h, p = carry
            rows = pl.ds(k, SUBLANES, stride=seg)
            av = a_ref[rows, :]
            h = av * h + u_ref[rows, :]
            p = av * p
            h_ref[rows, :] = h
            p_ref[rows, :] = p
            return h, p

        h_fin, p_fin = lax.fori_loop(0, seg, step, (jnp.zeros((SUBLANES, cb), F32), jnp.ones((SUBLANES, cb), F32)),
                                     unroll=4)
        carry = h_fin[0:1, :]
        for s in range(1, SUBLANES):
            rows = slice(s * seg, (s + 1) * seg)
            h_ref[rows, :] = h_ref[rows, :] + p_ref[rows, :] * carry
            carry = h_fin[s:s + 1, :] + p_fin[s:s + 1, :] * carry
        y_ref[...] = (_gelu(g_ref[...]) * h_ref[...]).astype(BF16)

    blk = pl.BlockSpec((t, cb), lambda j: (0, j))
    return pl.pallas_call(
        body, name=name, grid=(nb,),
        in_specs=[blk, blk, blk],
        out_specs=[blk, pl.BlockSpec((None, t, cb), lambda j: (0, 0, j))],
        out_shape=[jax.ShapeDtypeStruct((t, r), F32), jax.ShapeDtypeStruct((1, t, r), BF16)],
        scratch_shapes=[pltpu.VMEM((t, cb), F32)],
        compiler_params=_params("parallel"),
    )(a, u, gr)


def _scan_bwd(dy, gr, hr, a, *, cb, name):
    t, r = a.shape
    nb = r // cb
    seg = t // SUBLANES

    def body(dy_ref, g_ref, h_ref, a_ref, dh_ref, da_ref, dg_ref, q_ref):
        gl, dgl = _gelu_and_grad(g_ref[...])
        dyv = dy_ref[...]
        dh_ref[...] = dyv * gl
        dg_ref[...] = (dyv * h_ref[...] * dgl).astype(BF16)

        def step(k, carry):
            cin, q = carry
            rows = pl.ds(seg - 1 - k, SUBLANES, stride=seg)
            dh = dh_ref[rows, :] + cin
            dh_ref[rows, :] = dh
            q_ref[rows, :] = q
            av = a_ref[rows, :]
            return av * dh, av * q

        c_fin, q_fin = lax.fori_loop(0, seg, step, (jnp.zeros((SUBLANES, cb), F32), jnp.ones((SUBLANES, cb), F32)),
                                     unroll=4)
        carry = c_fin[SUBLANES - 1:SUBLANES, :]
        for s in range(SUBLANES - 2, -1, -1):
            rows = slice(s * seg, (s + 1) * seg)
            dh_ref[rows, :] = dh_ref[rows, :] + q_ref[rows, :] * carry
            carry = c_fin[s:s + 1, :] + q_fin[s:s + 1, :] * carry
        da_ref[...] = dh_ref[...] * _shift_down(h_ref[...], 1)

    blk = pl.BlockSpec((t, cb), lambda j: (0, j))
    return pl.pallas_call(
        body, name=name, grid=(nb,),
        in_specs=[blk, blk, blk, blk],
        out_specs=[blk, blk, blk],
        out_shape=[jax.ShapeDtypeStruct((t, r), F32), jax.ShapeDtypeStruct((t, r), F32),
                   jax.ShapeDtypeStruct((t, r), BF16)],
        scratch_shapes=[pltpu.VMEM((t, cb), F32)],
        compiler_params=_params("parallel"),
    )(dy, gr, hr, a)


def _gates_bwd(rec, r, i, a, dh, da, bd_r, bd_i, vecs, *, tm, name):
    t, r_dim = rec.shape
    nb, cb, _ = bd_r.shape

    def body(x_ref, r_ref, i_ref, a_ref, dh_ref, da_ref, wr_ref, wi_ref, v_ref, dx_ref, dpr_ref, dpi_ref, dv_ref):
        step = pl.program_id(1)
        x, r, i, a, dh, da = x_ref[...], r_ref[...], i_ref[...], a_ref[...], dh_ref[...], da_ref[...]
        lam = v_ref[...][2:3, :]
        sp = _softplus(-lam)
        a2 = a * a
        mult = jnp.sqrt(-_expm1(2.0 * (-LRU_C) * r * sp))
        d_i = dh * mult * x
        d_log_a = da * a - (dh * i * x) * a2 / mult
        d_r = d_log_a * ((-LRU_C) * sp)
        d_sp = jnp.sum(d_log_a * ((-LRU_C) * r), axis=0, keepdims=True)
        d_pre_r = d_r * r * (1.0 - r)
        d_pre_i = d_i * i * (1.0 - i)
        dprb = d_pre_r.astype(BF16)
        dpib = d_pre_i.astype(BF16)
        dx_ref[...] = dh * mult * i + _dot_nt(dprb, wr_ref[...]) + _dot_nt(dpib, wi_ref[...])
        dpr_ref[...] = dprb
        dpi_ref[...] = dpib
        upd = _rows8([jnp.sum(d_pre_r, axis=0, keepdims=True), jnp.sum(d_pre_i, axis=0, keepdims=True),
                      -d_sp * _sigmoid(-lam)], cb)

        @pl.when(step == 0)
        def _():
            dv_ref[...] = upd

        @pl.when(step > 0)
        def _():
            dv_ref[...] += upd

    blk = pl.BlockSpec((tm, cb), lambda j, i: (i, j))
    wspec = pl.BlockSpec((None, cb, cb), lambda j, i: (j, 0, 0))
    vspec = pl.BlockSpec((8, cb), lambda j, i: (0, j))
    return pl.pallas_call(
        body, name=name, grid=(nb, t // tm),
        in_specs=[blk] * 6 + [wspec, wspec, vspec],
        out_specs=[blk, blk, blk, vspec],
        out_shape=[jax.ShapeDtypeStruct((t, r_dim), F32), jax.ShapeDtypeStruct((t, r_dim), BF16),
                   jax.ShapeDtypeStruct((t, r_dim), BF16), jax.ShapeDtypeStruct((8, r_dim), F32)],
        compiler_params=_params("parallel", "arbitrary"),
    )(rec, r, i, a, dh, da, bd_r, bd_i, vecs)


def _bd_grad(rec, dpr, dpi, *, cb, name):
    t, r = rec.shape
    nb = r // cb

    def body(x_ref, dr_ref, di_ref, gr_ref, gi_ref):
        xb = x_ref[...].astype(BF16)
        gr_ref[...] = _dot_tn(xb, dr_ref[...])
        gi_ref[...] = _dot_tn(xb, di_ref[...])

    blk = pl.BlockSpec((t, cb), lambda j: (0, j))
    wspec = pl.BlockSpec((None, cb, cb), lambda j: (j, 0, 0))
    out = jax.ShapeDtypeStruct((nb, cb, cb), F32)
    return pl.pallas_call(
        body, name=name, grid=(nb,),
        in_specs=[blk, blk, blk], out_specs=[wspec, wspec], out_shape=[out, out],
        compiler_params=_params("parallel"),
    )(rec, dpr, dpi)


def _conv_a_bwd(d_rec, gr, dgate, cwb, *, cb, name):
    t, r = d_rec.shape
    nb = r // cb

    def body(dy_ref, x_ref, dg_ref, w_ref, dact_ref, dw_ref):
        dx, dw = _conv_taps_bwd(dy_ref[...], x_ref[...], w_ref[...], 4)
        dact_ref[0] = dg_ref[...]
        dact_ref[1] = dx.astype(BF16)
        dw_ref[...] = dw

    blk = pl.BlockSpec((t, cb), lambda j: (0, j))
    vspec = pl.BlockSpec((8, cb), lambda j: (0, j))
    return pl.pallas_call(
        body, name=name, grid=(nb,),
        in_specs=[blk, pl.BlockSpec((t, cb), lambda j: (0, j + nb)), blk, vspec],
        out_specs=[pl.BlockSpec((2, t, cb), lambda j: (0, 0, j)), vspec],
        out_shape=[jax.ShapeDtypeStruct((2, t, r), BF16), jax.ShapeDtypeStruct((8, r), F32)],
        compiler_params=_params("parallel"),
    )(d_rec, gr, dgate, cwb)


def _split3(x):
    p0 = x.astype(BF16)
    r1 = x - p0.astype(F32)
    p1 = r1.astype(BF16)
    p2 = (r1 - p1.astype(F32)).astype(BF16)
    return p0, p1, p2


def _fgate_fwd(fp, fb, *, tq, name):
    t = fp.shape[0]

    def body(f_ref, b_ref, c_ref, ct_ref):
        logf = -_softplus(-(f_ref[...] + b_ref[...]))
        rows = pl.program_id(0) * tq + lax.broadcasted_iota(jnp.int32, (tq, t), 0)
        cols = lax.broadcasted_iota(jnp.int32, (tq, t), 1)
        tri = (cols <= rows).astype(BF16)
        p0, p1, p2 = _split3(logf)
        c = _dot_nn(tri, p0) + _dot_nn(tri, p1) + _dot_nn(tri, p2)
        c_ref[...] = c
        ct_ref[...] = c.T

    return pl.pallas_call(
        body, name=name, grid=(t // tq,),
        in_specs=[pl.BlockSpec((t, LANES), lambda i: (0, 0)), pl.BlockSpec((1, LANES), lambda i: (0, 0))],
        out_specs=[pl.BlockSpec((tq, LANES), lambda i: (i, 0)), pl.BlockSpec((LANES, tq), lambda i: (0, i))],
        out_shape=[jax.ShapeDtypeStruct((t, LANES), F32), jax.ShapeDtypeStruct((LANES, t), F32)],
        compiler_params=_params("parallel"),
    )(fp, fb)


def _fgate_bwd(dct, fp, fb, *, tq, name):
    t = fp.shape[0]

    def body(d_ref, f_ref, b_ref, o_ref, db_ref):
        i = pl.program_id(0)
        rows = lax.broadcasted_iota(jnp.int32, (t, tq), 0)
        cols = i * tq + lax.broadcasted_iota(jnp.int32, (t, tq), 1)
        tri = (rows >= cols).astype(BF16)
        p0, p1, p2 = _split3(d_ref[...])
        dlogf = (_dot_nn(p0, tri) + _dot_nn(p1, tri) + _dot_nn(p2, tri)).T
        df = dlogf * _sigmoid(-(f_ref[...] + b_ref[...]))
        o_ref[...] = df.astype(BF16)
        upd = _rows8([jnp.sum(df, axis=0, keepdims=True)], LANES)

        @pl.when(i == 0)
        def _():
            db_ref[...] = upd

        @pl.when(i > 0)
        def _():
            db_ref[...] += upd

    return pl.pallas_call(
        body, name=name, grid=(t // tq,),
        in_specs=[pl.BlockSpec((LANES, t), lambda i: (0, 0)), pl.BlockSpec((tq, LANES), lambda i: (i, 0)),
                  pl.BlockSpec((1, LANES), lambda i: (0, 0))],
        out_specs=[pl.BlockSpec((tq, LANES), lambda i: (i, 0)), pl.BlockSpec((8, LANES), lambda i: (0, 0))],
        out_shape=[jax.ShapeDtypeStruct((t, LANES), BF16), jax.ShapeDtypeStruct((8, LANES), F32)],
        compiler_params=_params("arbitrary"),
    )(dct, fp, fb)


def _pair_sum(a, b, *, tm, name):
    t, d = a.shape

    def body(a_ref, b_ref, o_ref):
        o_ref[...] = (a_ref[...] + b_ref[...]).astype(BF16)

    row = pl.BlockSpec((tm, d), lambda i: (i, 0))
    return pl.pallas_call(
        body, name=name, grid=(t // tm,), in_specs=[row, row], out_specs=row,
        out_shape=jax.ShapeDtypeStruct((t, d), BF16), compiler_params=_params("parallel"),
    )(a, b)


def _head_masks(dh):
    lane = lax.broadcasted_iota(jnp.int32, (1, LANES), 1)
    return [((lane >= e * dh) & (lane < (e + 1) * dh)) for e in range(LANES // dh)]


def _head_c(c_blk, ct_blk, head):
    lane = lax.broadcasted_iota(jnp.int32, c_blk.shape, 1)
    c_col = jnp.sum(jnp.where(lane == head, c_blk, 0.0), axis=1, keepdims=True)
    sub = lax.broadcasted_iota(jnp.int32, ct_blk.shape, 0)
    c_row = jnp.sum(jnp.where(sub == head, ct_blk, 0.0), axis=0, keepdims=True)
    return c_col, c_row


def _attn_probs(qm, k, c_col, c_row, q0, scale):
    tq, t = qm.shape[0], k.shape[0]
    s = _dot_nt(qm, k) * scale + c_col - c_row
    qi = q0 + lax.broadcasted_iota(jnp.int32, (tq, t), 0)
    ki = lax.broadcasted_iota(jnp.int32, (tq, t), 1)
    s = jnp.where(ki <= qi, s, -jnp.inf)
    m = jnp.max(s, axis=-1, keepdims=True)
    p = jnp.exp(s - m)
    return p / jnp.sum(p, axis=-1, keepdims=True)


def _key_buckets(t, tq):
    step = 3 * tq
    return tuple(range(step, t, step)) + (t,)


def _for_prefix(needed, buckets, fn):
    prev = 0
    for length in buckets:
        pl.when((needed > prev) & (needed <= length))(lambda length=length: fn(length))
        prev = length


def _attn_fwd(qg, kv, c, ct, *, tq, name):
    t, d2 = qg.shape
    d = d2 // 2
    dh = d // N_HEADS
    hpb = LANES // dh
    nhb = d // LANES
    scale = dh ** -0.5
    buckets = _key_buckets(t, tq)

    def body(q_ref, og_ref, k_ref, v_ref, c_ref, ct_ref, o_ref, y_ref):
        hb = pl.program_id(0)
        q0 = pl.program_id(1) * tq

        def run(length):
            q = q_ref[...]
            k = k_ref[0:length, :]
            v = v_ref[0:length, :]
            o = jnp.zeros((tq, LANES), F32)
            for e, msk in enumerate(_head_masks(dh)):
                c_col, c_row = _head_c(c_ref[...], ct_ref[:, 0:length], hb * hpb + e)
                p = _attn_probs(jnp.where(msk, q, 0.0).astype(BF16), k, c_col, c_row, q0, scale)
                o = o + _dot_nn(p.astype(BF16), jnp.where(msk, v, jnp.zeros_like(v)))
            o_ref[...] = o
            y_ref[...] = (o * _sigmoid(og_ref[...])).astype(BF16)

        _for_prefix(q0 + tq, buckets, run)

    qblk = pl.BlockSpec((tq, LANES), lambda h, i: (i, h))
    return pl.pallas_call(
        body, name=name, grid=(nhb, t // tq),
        in_specs=[qblk, pl.BlockSpec((tq, LANES), lambda h, i: (i, h + nhb)),
                  pl.BlockSpec((t, LANES), lambda h, i: (0, h)), pl.BlockSpec((t, LANES), lambda h, i: (0, h + nhb)),
                  pl.BlockSpec((tq, LANES), lambda h, i: (i, 0)), pl.BlockSpec((N_HEADS, t), lambda h, i: (0, 0))],
        out_specs=[qblk, pl.BlockSpec((None, tq, LANES), lambda h, i: (0, i, h))],
        out_shape=[jax.ShapeDtypeStruct((t, d), F32), jax.ShapeDtypeStruct((1, t, d), BF16)],
        compiler_params=_params("parallel", "parallel"),
    )(qg, qg, kv, kv, c, ct)


def _attn_bwd(dy, qg, o, kv, c, ct, *, tq, name):
    t, d2 = qg.shape
    d = d2 // 2
    dh = d // N_HEADS
    hpb = LANES // dh
    nhb = d // LANES
    scale = dh ** -0.5
    buckets = _key_buckets(t, tq)

    def body(dy_ref, q_ref, og_ref, o_ref, k_ref, v_ref, c_ref, ct_ref, dqg_ref, dk_ref, dv_ref, dc_ref):
        hb = pl.program_id(0)
        step = pl.program_id(1)
        q0 = step * tq

        @pl.when(step == 0)
        def _():
            dk_ref[...] = jnp.zeros((t, LANES), F32)
            dv_ref[...] = jnp.zeros((t, LANES), F32)
            dc_ref[...] = jnp.zeros((8, t), F32)

        def run(length):
            q = q_ref[...]
            k = k_ref[0:length, :]
            v = v_ref[0:length, :]
            sg = _sigmoid(og_ref[...])
            dyv = dy_ref[...]
            do = dyv * sg
            dqg_ref[1] = (dyv * o_ref[...] * sg * (1.0 - sg)).astype(BF16)
            dq = jnp.zeros((tq, LANES), F32)
            dk = jnp.zeros((length, LANES), F32)
            dv = jnp.zeros((length, LANES), F32)
            dc_rows = []
            for e, msk in enumerate(_head_masks(dh)):
                c_col, c_row = _head_c(c_ref[...], ct_ref[:, 0:length], hb * hpb + e)
                qm = jnp.where(msk, q, 0.0).astype(BF16)
                dom = jnp.where(msk, do, 0.0).astype(BF16)
                p = _attn_probs(qm, k, c_col, c_row, q0, scale)
                dp = _dot_nt(dom, v)
                dsc = p * (dp - jnp.sum(p * dp, axis=-1, keepdims=True))
                dsb = (dsc * scale).astype(BF16)
                dq = dq + _dot_nn(dsb, jnp.where(msk, k, jnp.zeros_like(k)))
                dk = dk + _dot_tn(dsb, qm)
                dv = dv + _dot_tn(p.astype(BF16), dom)
                dc_rows.append(-jnp.sum(dsc, axis=0, keepdims=True))
            dqg_ref[0] = dq.astype(BF16)
            dk_ref[0:length, :] += dk
            dv_ref[0:length, :] += dv
            dc_ref[:, 0:length] += _rows8(dc_rows, length)

        _for_prefix(q0 + tq, buckets, run)

    qblk = pl.BlockSpec((tq, LANES), lambda h, i: (i, h))
    kblk = pl.BlockSpec((t, LANES), lambda h, i: (0, h))
    return pl.pallas_call(
        body, name=name, grid=(nhb, t // tq),
        in_specs=[qblk, qblk, pl.BlockSpec((tq, LANES), lambda h, i: (i, h + nhb)), qblk,
                  kblk, pl.BlockSpec((t, LANES), lambda h, i: (0, h + nhb)),
                  pl.BlockSpec((tq, LANES), lambda h, i: (i, 0)), pl.BlockSpec((N_HEADS, t), lambda h, i: (0, 0))],
        out_specs=[pl.BlockSpec((2, tq, LANES), lambda h, i: (0, i, h)), kblk, kblk,
                   pl.BlockSpec((None, 8, t), lambda h, i: (h, 0, 0))],
        out_shape=[jax.ShapeDtypeStruct((2, t, d), BF16), jax.ShapeDtypeStruct((t, d), F32),
                   jax.ShapeDtypeStruct((t, d), F32), jax.ShapeDtypeStruct((nhb, 8, t), F32)],
        compiler_params=_params("parallel", "arbitrary"),
    )(dy, qg, qg, o, kv, kv, c, ct)


def _loss_bwd(h, tgt, *, lo, hi, tm, name):
    t, d = h.shape

    def body(h_ref, t_ref, l_ref, dy_ref):
        i = pl.program_id(0)
        rows = i * tm + lax.broadcasted_iota(jnp.int32, (tm, d), 0)
        err = jnp.where((rows >= lo) & (rows < hi), h_ref[...] - t_ref[...], 0.0)
        dy_ref[...] = err * (1.0 / d)
        part = jnp.sum(jnp.sum(err * err, axis=0, keepdims=True), axis=1, keepdims=True) * (0.5 / d)
        upd = jnp.broadcast_to(part, (8, LANES))

        @pl.when(i == 0)
        def _():
            l_ref[...] = upd

        @pl.when(i > 0)
        def _():
            l_ref[...] += upd

    row = pl.BlockSpec((tm, d), lambda i: (i, 0))
    return pl.pallas_call(
        body, name=name, grid=(t // tm,),
        in_specs=[row, row],
        out_specs=[pl.BlockSpec((8, LANES), lambda i: (0, 0)), row],
        out_shape=[jax.ShapeDtypeStruct((8, LANES), F32), jax.ShapeDtypeStruct((t, d), F32)],
        compiler_params=_params("arbitrary"),
    )(h, tgt)


def _adamw_math(w, gv, m, v):
    bc1 = 1.0 / (1.0 - ADAM_B1 ** ADAM_STEP)
    bc2 = 1.0 / (1.0 - ADAM_B2 ** ADAM_STEP)
    nm = ADAM_B1 * m + (1.0 - ADAM_B1) * gv
    nv = ADAM_B2 * v + (1.0 - ADAM_B2) * (gv * gv)
    delta = (-ADAM_LR) * ((nm * bc1) / (jnp.sqrt(nv * bc2) + ADAM_EPS) + ADAM_WD * w)
    return delta, nm, nv


def _adamw(w, g, m, v, *, name):
    r, c = w.shape
    tr = r
    for cand in (512, 256, 128, 64, 32, 16, 8):
        if r % cand == 0 and r > cand:
            tr = cand
            break

    def body(w_ref, g_ref, m_ref, v_ref, d_ref, nm_ref, nv_ref):
        d_ref[...], nm_ref[...], nv_ref[...] = _adamw_math(w_ref[...], g_ref[...], m_ref[...], v_ref[...])

    blk = pl.BlockSpec((tr, c), lambda i: (i, 0))
    out = jax.ShapeDtypeStruct((r, c), F32)
    return pl.pallas_call(
        body, name=name, grid=(r // tr,),
        in_specs=[blk] * 4, out_specs=[blk] * 3, out_shape=[out] * 3,
        compiler_params=_params("parallel"),
    )(w, g, m, v)


def _sum_adamw(recvs, sends, me, w, m, v, *, name):
    n_l = len(recvs)
    _, r, c = recvs[0].shape
    tr = _tile(r, (256, 192, 176, 128, 96, 64, 48, 32, 16))

    def body(me_ref, *refs):
        p_refs, own_refs = refs[:n_l], refs[n_l:2 * n_l]
        w_ref, m_ref, v_ref, g_ref, d_ref, nm_ref, nv_ref, acc_ref = refs[2 * n_l:]
        layer = pl.program_id(0)
        mine = me_ref[0]
        for k in range(n_l):
            @pl.when(layer == k)
            def _(k=k):
                acc_ref[...] = jnp.zeros((tr, c), F32)
                for dev in range(N_DEV):
                    @pl.when(mine == dev)
                    def _():
                        acc_ref[...] += own_refs[k][...].astype(F32)

                    @pl.when(mine != dev)
                    def _(dev=dev):
                        acc_ref[...] += p_refs[k][dev].astype(F32)
                acc = acc_ref[...]
                g_ref[...] = acc
                d_ref[...], nm_ref[...], nv_ref[...] = _adamw_math(w_ref[...], acc, m_ref[...], v_ref[...])

    p_specs = [pl.BlockSpec((N_DEV, tr, c), lambda l, i, me_ref, k=k: (0, jnp.where(l == k, i, 0), 0))
               for k in range(n_l)]
    own_specs = [pl.BlockSpec((None, tr, c), lambda l, i, me_ref, k=k: (me_ref[0], jnp.where(l == k, i, 0), 0))
                 for k in range(n_l)]
    blk = pl.BlockSpec((None, tr, c), lambda l, i, me_ref: (l, i, 0))
    out = jax.ShapeDtypeStruct((n_l, r, c), F32)
    return pl.pallas_call(
        body, name=name,
        grid_spec=pltpu.PrefetchScalarGridSpec(
            num_scalar_prefetch=1, grid=(n_l, r // tr),
            in_specs=p_specs + own_specs + [blk] * 3, out_specs=[blk] * 4,
            scratch_shapes=[pltpu.VMEM((tr, c), F32)]),
        out_shape=[out] * 4,
        compiler_params=_params("arbitrary", "arbitrary"),
    )(me, *recvs, *sends, w, m, v)


def _sum8(parts, *, name):
    _, r, c = parts.shape
    tr = r
    for cand in (512, 256, 128, 64, 32, 16):
        if r % cand == 0 and r > cand:
            tr = cand
            break

    def body(p_ref, o_ref):
        acc = p_ref[0].astype(F32)
        for k in range(1, N_DEV):
            acc = acc + p_ref[k].astype(F32)
        o_ref[...] = acc

    return pl.pallas_call(
        body, name=name, grid=(r // tr,),
        in_specs=[pl.BlockSpec((N_DEV, tr, c), lambda i: (0, i, 0))],
        out_specs=pl.BlockSpec((tr, c), lambda i: (i, 0)),
        out_shape=jax.ShapeDtypeStruct((r, c), F32),
        compiler_params=_params("parallel"),
    )(parts)


def _my_index():
    return 4 * lax.axis_index("x") + 2 * lax.axis_index("y") + lax.axis_index("c")


def _peer(k):
    x, y, c = lax.axis_index("x"), lax.axis_index("y"), lax.axis_index("c")
    px = x ^ ((k >> 2) & 1)
    py = y ^ ((k >> 1) & 1)
    pc = c ^ (k & 1)
    return (px, py, pc), 4 * px + 2 * py + pc


def _all_gather(shards, *, name):
    n_arr = len(shards)

    def body(*refs):
        ins, outs = refs[:n_arr], refs[n_arr:2 * n_arr]
        send_sems, recv_sems, local_sems = refs[2 * n_arr:]
        me = _my_index()
        local = [pltpu.make_async_copy(ins[n], outs[n].at[me], local_sems.at[n]) for n in range(n_arr)]
        for cp in local:
            cp.start()
        sends = []
        for k in range(1, N_DEV):
            peer, _ = _peer(k)
            for n in range(n_arr):
                cp = pltpu.make_async_remote_copy(
                    src_ref=ins[n], dst_ref=outs[n].at[me], send_sem=send_sems.at[n, k - 1],
                    recv_sem=recv_sems.at[n, k - 1], device_id=peer, device_id_type=pl.DeviceIdType.MESH)
                cp.start()
                sends.append(cp)
        for k in range(1, N_DEV):
            peer, pidx = _peer(k)
            for n in range(n_arr):
                pltpu.make_async_remote_copy(
                    src_ref=ins[n], dst_ref=outs[n].at[pidx], send_sem=send_sems.at[n, k - 1],
                    recv_sem=recv_sems.at[n, k - 1], device_id=peer, device_id_type=pl.DeviceIdType.MESH).wait_recv()
        for cp in sends:
            cp.wait_send()
        for cp in local:
            cp.wait()

    hbm = pl.BlockSpec(memory_space=pl.ANY)
    return pl.pallas_call(
        body, name=name,
        in_specs=[hbm] * n_arr, out_specs=[hbm] * n_arr,
        out_shape=[jax.ShapeDtypeStruct((N_DEV,) + s.shape, s.dtype) for s in shards],
        scratch_shapes=[pltpu.SemaphoreType.DMA((n_arr, N_DEV - 1)), pltpu.SemaphoreType.DMA((n_arr, N_DEV - 1)),
                        pltpu.SemaphoreType.DMA((n_arr,))],
        compiler_params=pltpu.CompilerParams(has_side_effects=True),
    )(*shards)


_HBM = pl.BlockSpec(memory_space=pltpu.HBM)
_SEM = pl.BlockSpec(memory_space=pltpu.SEMAPHORE)
_EFFECT = pltpu.SideEffectType.DATAFLOW_SIDE_EFFECTING


def _remote(src, dst, send_sem, recv_sem, peer):
    return pltpu.make_async_remote_copy(src_ref=src, dst_ref=dst, send_sem=send_sem, recv_sem=recv_sem,
                                        device_id=peer, device_id_type=pl.DeviceIdType.MESH)


def _place_own(src, layer, me, *, out_dtype, name):
    _, r, c = src.shape
    tr = _tile(r, (256, 192, 176, 128, 96, 64, 48, 32, 16))

    def body(me_ref, s_ref, o_ref):
        o_ref[...] = s_ref[...].astype(out_dtype)

    return pl.pallas_call(
        body, name=name,
        grid_spec=pltpu.PrefetchScalarGridSpec(
            num_scalar_prefetch=1, grid=(r // tr,),
            in_specs=[pl.BlockSpec((None, tr, c), lambda i, me_ref: (layer, i, 0))],
            out_specs=pl.BlockSpec((None, tr, c), lambda i, me_ref: (me_ref[0], i, 0))),
        out_shape=jax.ShapeDtypeStruct((N_DEV, r, c), out_dtype),
        compiler_params=_params("parallel"),
    )(me, src)


def _own_blocks(srcs, *, name):
    n = len(srcs)

    def body(*refs):
        ins, outs, sems = refs[:n], refs[n:2 * n], refs[2 * n]
        me = _my_index()
        cps = [pltpu.make_async_copy(ins[t].at[me], outs[t].at[me], sems.at[t]) for t in range(n)]
        for cp in cps:
            cp.start()
        for cp in cps:
            cp.wait()

    return pl.pallas_call(
        body, name=name, in_specs=[_HBM] * n, out_specs=[_HBM] * n,
        out_shape=[jax.ShapeDtypeStruct(s.shape, s.dtype) for s in srcs],
        scratch_shapes=[pltpu.SemaphoreType.DMA((n,))],
    )(*srcs)


def _split_start(groups, *, scatter, name):
    sizes = [len(srcs) for srcs, _ in groups]
    flat_src = [s for srcs, _ in groups for s in srcs]
    flat_land = [l for _, lands in groups for l in lands]
    n, n_g = len(flat_land), len(groups)
    if not scatter:
        flat_src = []
    n_in = len(flat_src) + n

    def body(*refs):
        lands = refs[n_in - n:n_in]
        ins = refs[:n] if scatter else lands
        sems = refs[n_in:n_in + 2 * n_g]
        token = refs[-1]
        me = _my_index()
        t = 0
        for g in range(n_g):
            for q in range(sizes[g]):
                for k in range(1, N_DEV):
                    peer, pidx = _peer(k)
                    src = ins[t].at[pidx] if scatter else ins[t].at[me]
                    slot = q * (N_DEV - 1) + k - 1
                    _remote(src, lands[t].at[me], sems[2 * g].at[slot], sems[2 * g + 1].at[slot], peer).start()
                t += 1
        token[...] = jnp.zeros_like(token)

    sem_shapes = []
    for sz in sizes:
        sem_shapes += [pltpu.SemaphoreType.DMA((sz * (N_DEV - 1),)), pltpu.SemaphoreType.DMA((sz * (N_DEV - 1),))]
    outs = pl.pallas_call(
        body, name=name,
        in_specs=[_HBM] * n_in,
        out_specs=[_SEM] * (2 * n_g) + [_HBM] * n_in + [pl.BlockSpec(memory_space=pltpu.VMEM)],
        out_shape=sem_shapes + [pltpu.HBM(a.shape, a.dtype) for a in flat_src + flat_land]
        + [jax.ShapeDtypeStruct((8, LANES), F32)],
        input_output_aliases={i: 2 * n_g + i for i in range(n_in)},
        compiler_params=pltpu.CompilerParams(has_side_effects=_EFFECT),
    )(*[pltpu.with_memory_space_constraint(a, pltpu.HBM) for a in flat_src + flat_land])
    sems, thru, token = outs[:2 * n_g], outs[2 * n_g:2 * n_g + n_in], outs[-1]
    handles, pos = [], 0
    for g, sz in enumerate(sizes):
        lands_g = thru[n_in - n + pos:n_in - n + pos + sz]
        handles.append((sems[2 * g], sems[2 * g + 1], thru[pos:pos + sz] if scatter else [], lands_g))
        pos += sz
    return handles, token


def _split_wait(handle, after, *, scatter, name):
    send_sems, recv_sems, srcs, lands = handle
    n, n_src = len(lands), len(srcs)

    def body(*refs):
        lnd = refs[n_src:n_src + n]
        ins = refs[:n_src] if scatter else lnd
        ssem, rsem = refs[n_src + n], refs[n_src + n + 1]
        me = _my_index()
        for t in range(n):
            for k in range(1, N_DEV):
                peer, pidx = _peer(k)
                block = ins[t].at[me]
                slot = t * (N_DEV - 1) + k - 1
                _remote(block, lnd[t].at[me], ssem.at[slot], rsem.at[slot], peer).wait_send()
                _remote(block, lnd[t].at[pidx], ssem.at[slot], rsem.at[slot], peer).wait_recv()

    return pl.pallas_call(
        body, name=name,
        in_specs=[_HBM] * (n_src + n) + [_SEM, _SEM, pl.BlockSpec(memory_space=pl.ANY)],
        out_specs=[_HBM] * n,
        out_shape=[pltpu.HBM(l.shape, l.dtype) for l in lands],
        input_output_aliases={n_src + t: t for t in range(n)},
        compiler_params=pltpu.CompilerParams(has_side_effects=_EFFECT),
    )(*srcs, *lands, send_sems, recv_sems, after)


def _pack(arrs, dtype, row_quantum=16):
    flat = jnp.concatenate([a.astype(dtype).reshape(-1) for a in arrs])
    pad = (-flat.shape[0]) % (row_quantum * PACK_COLS)
    if pad:
        flat = jnp.concatenate([flat, jnp.zeros((pad,), dtype)])
    return flat.reshape(-1, PACK_COLS)


def _pack8(arrs, dtype):
    flat = jnp.concatenate([a.astype(dtype).reshape(N_DEV, -1) for a in arrs], axis=1)
    pad = (-flat.shape[1]) % (16 * PACK_COLS)
    if pad:
        flat = jnp.concatenate([flat, jnp.zeros((N_DEV, pad), dtype)], axis=1)
    return flat.reshape(N_DEV, -1, PACK_COLS)


def _unpack(slab, shapes, lead):
    lead_shape = slab.shape[:lead]
    flat = slab.reshape(lead_shape + (-1,))
    outs, off = [], 0
    for shp in shapes:
        size = math.prod(shp)
        outs.append(flat[..., off:off + size].reshape(lead_shape + tuple(shp)))
        off += size
    return outs


def _cols_full(g):
    g = jnp.moveaxis(g, 0, -2)
    return g.reshape(g.shape[:-2] + (g.shape[-2] * g.shape[-1],))


def _cols_split(full):
    n = full.shape[-1] // N_DEV
    return jnp.moveaxis(full.reshape(full.shape[:-1] + (N_DEV, n)), -2, 0)


def _block_diag(w, per):
    n, b, _ = w.shape
    w4 = w.reshape(n // per, per, b, b)
    eye = jnp.eye(per, dtype=w.dtype)
    return jnp.einsum('gpab,pq->gpaqb', w4, eye).reshape(n // per, per * b, per * b)


def _block_diag_extract(g, per):
    gn, cb, _ = g.shape
    b = cb // per
    g5 = g.reshape(gn, per, b, per, b)
    return jnp.stack([g5[:, p, :, p, :] for p in range(per)], axis=1).reshape(gn * per, b, b)


def _slab2d(a):
    if a.size % PACK_COLS == 0:
        return a.reshape(-1, PACK_COLS)
    return a.reshape(-1, a.shape[-1])


def _lru_block_cols(r_dim):
    lru = r_dim // N_LRU_BLOCKS
    return lru * LANES // math.gcd(lru, LANES)


BIG = ("a_w_in", "a_w_out", "b_w_in", "b_w_out", "f_w_in", "f_w_out")
COL_F32 = ("meta", "a_conv_w", "a_conv_b", "a_b_r", "a_b_i", "a_lambda", "f_conv_w")
REPLICATED = ("a_w_r", "a_w_i", "kv_f_b", "f_conv_b", "ln1_g", "ln1_b", "ln2_g", "ln2_b")
WEIGHT_NAMES = ("meta", "a_w_in", "a_conv_w", "a_conv_b", "a_w_r", "a_b_r", "a_w_i", "a_b_i", "a_lambda", "a_w_out",
                "kv_w", "kv_f_b", "b_w_in", "b_w_out", "f_w_in", "f_conv_w", "f_conv_b", "f_w_out",
                "ln1_g", "ln1_b", "ln2_g", "ln2_b")


def _kv_layout(kv_gathered, d):
    kv_full = _cols_full(kv_gathered)
    kv_pad = 2 * d + LANES - kv_full.shape[1]
    return jnp.concatenate([kv_full, jnp.zeros((d, kv_pad), kv_full.dtype)], axis=1)


def _small_layouts(small):
    r_dim = small["a_lambda"].shape[1]
    n_f = small["f_conv_b"].shape[1] // N_DEV
    cb = _lru_block_cols(r_dim)
    per = cb // (r_dim // N_LRU_BLOCKS)
    n_a = small["a_lambda"].shape[0]
    f_conv_w3 = small["f_conv_w"].reshape(N_LAYERS, 3, N_DEV, n_f).transpose(0, 2, 1, 3)
    f_conv_b3 = small["f_conv_b"].reshape(N_LAYERS, N_DEV, 1, n_f)
    return {
        "kv_fb": jnp.concatenate([small["kv_f_b"], jnp.zeros((LANES - N_HEADS,), F32)])[None],
        "a_cwb": jnp.concatenate([small["a_conv_w"], small["a_conv_b"][:, None],
                                  jnp.zeros((n_a, 3, r_dim), F32)], axis=1),
        "a_vecs": jnp.concatenate([jnp.stack([small["a_b_r"], small["a_b_i"], small["a_lambda"]], axis=1),
                                   jnp.zeros((n_a, 5, r_dim), F32)], axis=1),
        "a_bd_r": jnp.stack([_block_diag(small["a_w_r"][l], per) for l in range(n_a)]).astype(BF16),
        "a_bd_i": jnp.stack([_block_diag(small["a_w_i"][l], per) for l in range(n_a)]).astype(BF16),
        "f_cwb3": jnp.concatenate([f_conv_w3, f_conv_b3, jnp.zeros((N_LAYERS, N_DEV, 4, n_f), F32)], axis=2),
        "ln1_g": small["ln1_g"][:, None], "ln1_b": small["ln1_b"][:, None],
        "ln2_g": small["ln2_g"][:, None], "ln2_b": small["ln2_b"][:, None],
    }


def _local_step(h0, tgt, n_meta, n_tok, wts, hooks):
    tp, d = h0.shape
    tm = tp // 8 if (tp // 8) % 16 == 0 else tp
    tmb = _tile(tp, (1088, 512, 320, 256, 128))
    tq = 128
    r_dim = wts["a_vecs"].shape[2]
    cb = wts["a_bd_r"].shape[-1]
    sb = LANES
    n_b = N_LAYERS - N_A_LAYERS

    h, h_bf = h0, h0.astype(BF16)
    saved = []
    kvs = None
    for layer in range(N_LAYERS):
        lw = {}
        sv = {"h_bf": h_bf, "w": lw}
        if layer < N_A_LAYERS:
            lw["in"] = hooks.weight(layer, "in", h)
            sv["gr"] = _proj_in(h_bf, lw["in"], shard_major=False, name="a_in_proj")
            sv["rec"] = _conv_a_fwd(sv["gr"], wts["a_cwb"][layer], cb=cb, name="a_conv_fwd")
            a, u, sv["r"], sv["i"] = _gates_fwd(sv["rec"], wts["a_bd_r"][layer], wts["a_bd_i"][layer],
                                                wts["a_vecs"][layer], tm=tm, name="a_gates_fwd")
            sv["a"] = a
            sv["hr"], y3 = _scan_fwd(a, u, sv["gr"], cb=sb, name="a_scan_fwd")
        else:
            j = layer - N_A_LAYERS
            if j == 0:
                kv_w = _kv_layout(hooks.weight(layer, "kv_w", h), d)
                kvs = {"h_bf": h_bf, "w": kv_w}
                kvs["kv"] = _mm_nn(h_bf, kv_w[:, :2 * d], tn=_tile(2 * d, (512, 256, 128)), out_dtype=BF16,
                                   name="kv_proj")
                kvs["fp"] = _mm_nn(h_bf, kv_w[:, 2 * d:], tn=LANES, out_dtype=F32, name="f_proj")
                kvs["c"], ct = _fgate_fwd(kvs["fp"], wts["kv_fb"], tq=tq, name="fgate_fwd")
                kvs["ct"] = ct[:N_HEADS]
            lw["in"] = hooks.weight(layer, "in", kvs["c"] if j == 0 else h)
            sv["qg"] = _proj_in(h_bf, lw["in"], shard_major=False, name="b_in_proj")
            sv["o"], y3 = _attn_fwd(sv["qg"], kvs["kv"], kvs["c"], kvs["ct"], tq=tq, name="attn_fwd")
        sv["y3"] = y3
        lw["out"] = hooks.weight(layer, "out", y3)
        sv["s1"], h, h_bf = _out_ln(y3, lw["out"], h, wts["ln1_g"][layer], wts["ln1_b"][layer], n_valid=n_tok,
                                    tm=tm, name="mix_out_ln")
        sv["h1_bf"] = h_bf
        lw["f_in"] = hooks.weight(layer, "f_in", h)
        sv["z3"] = _proj_in(h_bf, lw["f_in"], shard_major=True, transposed=True, name="f_in_proj")
        sv["yf3"] = _convglu_fwd(sv["z3"], wts["f_cwb3"][layer], name="f_convglu_fwd")
        lw["f_out"] = hooks.weight(layer, "f_out", sv["yf3"])
        sv["s2"], h, h_bf = _out_ln(sv["yf3"], lw["f_out"], h, wts["ln2_g"][layer], wts["ln2_b"][layer],
                                    n_valid=n_tok, tm=tm, name="ffn_out_ln")
        saved.append(sv)

    loss_tile, dh = _loss_bwd(h, tgt, lo=n_meta, hi=n_tok, tm=tm, name="loss")

    grads = {k: [None] * N_LAYERS for k in ("f_cwb3", "ln1_gb", "ln2_gb")}
    grads.update({k: [None] * N_A_LAYERS for k in ("a_cwb", "a_bd_r", "a_bd_i", "a_vecs")})
    dkv = []
    token = jnp.zeros((), F32)
    for layer in reversed(range(N_LAYERS)):
        sv = saved[layer]
        lw = sv["w"]
        big = {}
        ds, ds_bf, grads["ln2_gb"][layer] = _ln_bwd(dh, sv["s2"], wts["ln2_g"][layer] + token, tm=tm, name="ln_bwd")
        dz, dcw = _ffn_bwd_mid(ds_bf, lw["f_out"], sv["z3"], wts["f_cwb3"][layer], name="f_bwd_mid")
        grads["f_cwb3"][layer] = dcw.reshape((N_DEV,) + dcw.shape[2:])
        dz3 = dz.reshape((N_DEV,) + dz.shape[2:])
        big["f_out"] = _w_out_grad(sv["yf3"], ds_bf, lw["f_out"].shape[1], name="f_w_out_grad")
        dh = _in_bwd(dz3, lw["f_in"], ds, tm=tmb, transposed=True, name="f_in_bwd")
        big["f_in"] = _w_in_grad(sv["h1_bf"], dz3, transposed=True, name="f_w_in_grad")
        token = hooks.grads_ready(layer, "ffn", big)
        big = {}
        ds, ds_bf, grads["ln1_gb"][layer] = _ln_bwd(dh, sv["s1"], wts["ln1_g"][layer] + token, tm=tm, name="ln_bwd")
        if layer < N_A_LAYERS:
            dy = _out_bwd(ds_bf, lw["out"], tm=tmb // 2, name="a_out_bwd")
            big["out"] = _w_out_grad(sv["y3"], ds_bf, lw["out"].shape[1], name="a_w_out_grad")
            d_h, d_a, dgate = _scan_bwd(dy, sv["gr"], sv["hr"], sv["a"], cb=sb, name="a_scan_bwd")
            d_rec, dpr, dpi, grads["a_vecs"][layer] = _gates_bwd(
                sv["rec"], sv["r"], sv["i"], sv["a"], d_h, d_a, wts["a_bd_r"][layer], wts["a_bd_i"][layer],
                wts["a_vecs"][layer], tm=tm, name="a_gates_bwd")
            grads["a_bd_r"][layer], grads["a_bd_i"][layer] = _bd_grad(sv["rec"], dpr, dpi, cb=cb, name="a_bd_grad")
            dact, grads["a_cwb"][layer] = _conv_a_bwd(d_rec, sv["gr"], dgate, wts["a_cwb"][layer], cb=cb,
                                                      name="a_conv_bwd")
            dh = _in_bwd(dact, lw["in"], ds, tm=tmb, name="a_in_bwd")
            big["in"] = _w_in_grad(sv["h_bf"], dact, name="a_w_in_grad")
        else:
            j = layer - N_A_LAYERS
            dy = _out_bwd(ds_bf, lw["out"], tm=tmb // 2, name="b_out_bwd")
            big["out"] = _w_out_grad(sv["y3"], ds_bf, lw["out"].shape[1], name="b_w_out_grad")
            dqg, dk, dv, dc = _attn_bwd(dy, sv["qg"], sv["o"], kvs["kv"], kvs["c"], kvs["ct"], tq=tq,
                                        name="attn_bwd")
            dkv.append((dk, dv, dc))
            dh = _in_bwd(dqg, lw["in"], ds, tm=tmb, name="b_in_bwd")
            big["in"] = _w_in_grad(sv["h_bf"], dqg, name="b_w_in_grad")
            if j == 0:
                hpb = LANES // (d // N_HEADS)
                dct = (dkv[0][2] + dkv[1][2])[:, :hpb, :].reshape(N_HEADS, tp)
                dct = jnp.concatenate([dct, jnp.zeros((LANES - N_HEADS, tp), F32)])
                df_bf, grads["kv_fb"] = _fgate_bwd(dct, kvs["fp"], wts["kv_fb"], tq=tq, name="fgate_bwd")
                dkvz = jnp.concatenate([_pair_sum(dkv[0][0], dkv[1][0], tm=tm, name="kv_pair_sum"),
                                        _pair_sum(dkv[0][1], dkv[1][1], tm=tm, name="kv_pair_sum"), df_bf], axis=1)
                dh = _mm_nt_full(dkvz, kvs["w"], dh, tm=tmb // 2, name="kv_in_bwd")
                big["kv_w"] = _mm_tn_cols(kvs["h_bf"], dkvz, tn=LANES, name="kv_w_grad")
        token = hooks.grads_ready(layer, "mix", big)
    return loss_tile, dh, grads


def _finish_small_grads(grads, d_h0, n_meta):
    r_dim = grads["a_vecs"][0].shape[1]
    per = _lru_block_cols(r_dim) // (r_dim // N_LRU_BLOCKS)
    a_cwb = jnp.stack(grads["a_cwb"])
    a_vecs = jnp.stack(grads["a_vecs"])
    f_cwb3 = jnp.stack(grads["f_cwb3"])
    ln1 = jnp.stack(grads["ln1_gb"])
    ln2 = jnp.stack(grads["ln2_gb"])
    f_rows = f_cwb3.transpose(0, 2, 1, 3).reshape(N_LAYERS, 8, -1)
    return {
        "meta": d_h0[:n_meta],
        "a_conv_w": a_cwb[:, :4], "a_conv_b": a_cwb[:, 4],
        "a_w_r": jnp.stack([_block_diag_extract(g, per) for g in grads["a_bd_r"]]),
        "a_b_r": a_vecs[:, 0],
        "a_w_i": jnp.stack([_block_diag_extract(g, per) for g in grads["a_bd_i"]]),
        "a_b_i": a_vecs[:, 1], "a_lambda": a_vecs[:, 2],
        "kv_f_b": grads["kv_fb"][0, :N_HEADS],
        "f_conv_w": f_rows[:, :3], "f_conv_b": f_rows[:, 3],
        "ln1_g": ln1[:, 0], "ln1_b": ln1[:, 1], "ln2_g": ln2[:, 0], "ln2_b": ln2[:, 1],
    }


def kernel(x, meta, a_w_in, a_conv_w, a_conv_b, a_w_r, a_b_r, a_w_i, a_b_i, a_lambda, a_w_out, kv_w, kv_f_b, b_w_in, b_w_out, f_w_in, f_conv_w, f_conv_b, f_w_out, ln1_g, ln1_b, ln2_g, ln2_b, loss_target, m_meta, m_a_w_in, m_a_conv_w, m_a_conv_b, m_a_w_r, m_a_b_r, m_a_w_i, m_a_b_i, m_a_lambda, m_a_w_out, m_kv_w, m_kv_f_b, m_b_w_in, m_b_w_out, m_f_w_in, m_f_conv_w, m_f_conv_b, m_f_w_out, m_ln1_g, m_ln1_b, m_ln2_g, m_ln2_b, v_meta, v_a_w_in, v_a_conv_w, v_a_conv_b, v_a_w_r, v_a_b_r, v_a_w_i, v_a_b_i, v_a_lambda, v_a_w_out, v_kv_w, v_kv_f_b, v_b_w_in, v_b_w_out, v_f_w_in, v_f_conv_w, v_f_conv_b, v_f_w_out, v_ln1_g, v_ln1_b, v_ln2_g, v_ln2_b):
    w = dict(meta=meta, a_w_in=a_w_in, a_conv_w=a_conv_w, a_conv_b=a_conv_b, a_w_r=a_w_r, a_b_r=a_b_r, a_w_i=a_w_i,
             a_b_i=a_b_i, a_lambda=a_lambda, a_w_out=a_w_out, kv_w=kv_w, kv_f_b=kv_f_b, b_w_in=b_w_in,
             b_w_out=b_w_out, f_w_in=f_w_in, f_conv_w=f_conv_w, f_conv_b=f_conv_b, f_w_out=f_w_out, ln1_g=ln1_g,
             ln1_b=ln1_b, ln2_g=ln2_g, ln2_b=ln2_b)
    m = dict(meta=m_meta, a_w_in=m_a_w_in, a_conv_w=m_a_conv_w, a_conv_b=m_a_conv_b, a_w_r=m_a_w_r, a_b_r=m_a_b_r,
             a_w_i=m_a_w_i, a_b_i=m_a_b_i, a_lambda=m_a_lambda, a_w_out=m_a_w_out, kv_w=m_kv_w, kv_f_b=m_kv_f_b,
             b_w_in=m_b_w_in, b_w_out=m_b_w_out, f_w_in=m_f_w_in, f_conv_w=m_f_conv_w, f_conv_b=m_f_conv_b,
             f_w_out=m_f_w_out, ln1_g=m_ln1_g, ln1_b=m_ln1_b, ln2_g=m_ln2_g, ln2_b=m_ln2_b)
    v = dict(meta=v_meta, a_w_in=v_a_w_in, a_conv_w=v_a_conv_w, a_conv_b=v_a_conv_b, a_w_r=v_a_w_r, a_b_r=v_a_b_r,
             a_w_i=v_a_w_i, a_b_i=v_a_b_i, a_lambda=v_a_lambda, a_w_out=v_a_w_out, kv_w=v_kv_w, kv_f_b=v_kv_f_b,
             b_w_in=v_b_w_in, b_w_out=v_b_w_out, f_w_in=v_f_w_in, f_conv_w=v_f_conv_w, f_conv_b=v_f_conv_b,
             f_w_out=v_f_w_out, ln1_g=v_ln1_g, ln1_b=v_ln1_b, ln2_g=v_ln2_g, ln2_b=v_ln2_b)
    shapes = {n: w[n].shape for n in WEIGHT_NAMES}

    me = jnp.reshape(_my_index(), (1,)).astype(jnp.int32)

    def as_stored(name, a):
        return jnp.swapaxes(a, 1, 2) if name == "f_w_in" else a

    param_of = {"in": ("a_w_in", "b_w_in"), "out": ("a_w_out", "b_w_out"), "f_in": ("f_w_in",) * 2,
                "f_out": ("f_w_out",) * 2}
    order = [("small", None, None)]
    for layer in range(N_LAYERS):
        if layer == N_A_LAYERS:
            order.append(("kv_w", layer, 0))
        for key in ("in", "out", "f_in", "f_out"):
            order.append((key, layer, layer if key[0] == "f" or layer < N_A_LAYERS else layer - N_A_LAYERS))
    lands = []
    for key, layer, idx in order:
        if key == "small":
            lands.append(_place_own(_pack([w[n] for n in COL_F32], F32)[None], 0, me, out_dtype=F32,
                                    name="place_small"))
        elif key == "kv_w":
            lands.append(_place_own(w["kv_w"][None], 0, me, out_dtype=BF16, name="place_kv_w"))
        else:
            name = param_of[key][0 if layer < N_A_LAYERS else 1]
            lands.append(_place_own(as_stored(name, w[name]), idx, me, out_dtype=BF16, name=f"place_{name}_{idx}"))
    gather_handles, gather_token = _split_start([([l], [l]) for l in lands], scatter=False, name="gather_start")
    group_of = {(key, layer): g for g, (key, layer, _) in enumerate(order)}
    (got_s,) = _split_wait(gather_handles[0], gather_token, scatter=False, name="gather_wait_small")
    small = {n: w[n] for n in REPLICATED}
    for n, part in zip(COL_F32, _unpack(got_s, [w[n].shape for n in COL_F32], 1)):
        small[n] = _cols_full(part)
    n_meta, d = small["meta"].shape

    class Hooks:
        pending = None
        received = {}
        sent = {}

        @staticmethod
        def weight(layer, key, after):
            (got,) = _split_wait(gather_handles[group_of[(key, layer)]], after, scatter=False,
                                 name=f"gather_wait_{key}_{layer}")
            return got

        @staticmethod
        def collect(after):
            if Hooks.pending is not None:
                tag, names, handle = Hooks.pending
                got = _split_wait(handle, after, scatter=True, name=f"scatter_wait_{tag}")
                Hooks.received.update(zip(names, got))
                Hooks.pending = None

        @staticmethod
        def grads_ready(layer, part, big):
            if "kv_w" in big:
                big["kv_w"] = _cols_split(big["kv_w"][:, :shapes["kv_w"][1] * N_DEV]).astype(BF16)
            names = [(key, layer) for key in big]
            send = [big[key] for key in big]
            Hooks.collect(send[0])
            empty = [lax.empty(s.shape, s.dtype) for s in send]
            handles, token = _split_start([(send, empty)], scatter=True, name=f"scatter_start_{part}_{layer}")
            Hooks.pending = (f"{part}_{layer}", names, handles[0])
            Hooks.sent.update(zip(names, handles[0][2]))
            return token[0, 0]

    Hooks.pending, Hooks.received, Hooks.sent = None, {}, {}

    n_tok = n_meta + x.shape[1]
    tp = -(-n_tok // ROW_ALIGN) * ROW_ALIGN
    pad = jnp.zeros((tp - n_tok, d), F32)
    h0 = jnp.concatenate([small["meta"], x[0], pad])
    tgt = jnp.concatenate([jnp.zeros((n_meta, d), F32), loss_target[0], pad])
    loss_tile, d_h0, grads = _local_step(h0, tgt, n_meta, n_tok, _small_layouts(small), Hooks)
    g_small = _finish_small_grads(grads, d_h0, n_meta)
    loss = lax.psum(loss_tile[0, 0], MESH_AXES)
    grad_x = d_h0[n_meta:n_tok][None]

    rep = _pack([g_small[n] for n in REPLICATED], F32, row_quantum=16 * N_DEV)
    send = [_pack8([_cols_split(g_small[n]) for n in COL_F32], F32), rep.reshape(N_DEV, -1, PACK_COLS)]
    lands = _own_blocks(send, name="scatter_own_small")
    handles, token = _split_start([(send, lands)], scatter=True, name="scatter_start_small")

    g, delta, new_m, new_v = {}, {}, {}, {}
    layers_of = {
        "a_w_in": [("in", l) for l in range(N_A_LAYERS)], "a_w_out": [("out", l) for l in range(N_A_LAYERS)],
        "b_w_in": [("in", l) for l in range(N_A_LAYERS, N_LAYERS)],
        "b_w_out": [("out", l) for l in range(N_A_LAYERS, N_LAYERS)],
        "f_w_in": [("f_in", l) for l in range(N_LAYERS)], "f_w_out": [("f_out", l) for l in range(N_LAYERS)],
        "kv_w": [("kv_w", N_A_LAYERS)],
    }
    ready = [n for n in BIG + ("kv_w",) if all(t in Hooks.received for t in layers_of[n])]
    for n in ready + [n for n in BIG + ("kv_w",) if n not in ready]:
        if n not in ready and Hooks.pending is not None:
            Hooks.collect(g[ready[-1]])
        lift = (lambda a: a[None]) if n == "kv_w" else (lambda a, n=n: as_stored(n, a))
        outs = _sum_adamw([Hooks.received[t] for t in layers_of[n]], [Hooks.sent[t] for t in layers_of[n]], me,
                          lift(w[n]), lift(m[n]), lift(v[n]), name="sum_adamw_" + n)
        g[n], delta[n], new_m[n], new_v[n] = [as_stored(n, o).reshape(shapes[n]) for o in outs]
    recv_s, recv_r = _split_wait(handles[0], g[n], scatter=True, name="scatter_wait_small")
    sum_s = _sum8(recv_s, name="sum_grads_f32")
    g.update(zip(COL_F32, _unpack(sum_s, [shapes[n] for n in COL_F32], 0)))
    (got_r,) = _all_gather([_sum8(recv_r, name="sum_grads_replicated")], name="gather_replicated_sums")
    g.update(zip(REPLICATED, _unpack(got_r.reshape(-1, PACK_COLS), [shapes[n] for n in REPLICATED], 0)))

    for n in COL_F32 + REPLICATED:
        shp = shapes[n]
        dl, nm, nv = _adamw(_slab2d(w[n]), _slab2d(g[n]), _slab2d(m[n]), _slab2d(v[n]), name="adamw")
        delta[n], new_m[n], new_v[n] = dl.reshape(shp), nm.reshape(shp), nv.reshape(shp)
    return (loss, grad_x, *[g[n] for n in WEIGHT_NAMES], *[delta[n] for n in WEIGHT_NAMES],
            *[new_m[n] for n in WEIGHT_NAMES], *[new_v[n] for n in WEIGHT_NAMES])
```

```python
import math

import jax
import jax.numpy as jnp
from jax import lax
from jax.experimental import pallas as pl
from jax.experimental.pallas import tpu as pltpu

F32 = jnp.float32
BF16 = jnp.bfloat16

N_DEV = 8
MESH_AXES = ("x", "y", "c")
N_LAYERS = 4
N_A_LAYERS = 2
N_LRU_BLOCKS = 16
N_HEADS = 16
LRU_C = 8.0
DN_ALPHA = (2 * N_LAYERS) ** 0.25
LN_EPS = 1e-5
ADAM_LR, ADAM_B1, ADAM_B2, ADAM_EPS, ADAM_WD, ADAM_STEP = 0.001, 0.9, 0.999, 1e-08, 0.01, 10

LANES = 128
SUBLANES = 8
ROW_ALIGN = 128
VMEM_LIMIT_BYTES = 56 * 1024 * 1024
GELU_K = math.sqrt(2.0 / math.pi)
GELU_C = 0.044715
PACK_COLS = 1024


def _params(*sem):
    return pltpu.CompilerParams(dimension_semantics=sem, vmem_limit_bytes=VMEM_LIMIT_BYTES)


def _gelu(x):
    th = jnp.tanh(GELU_K * (x + GELU_C * x * x * x))
    return 0.5 * x * (1.0 + th)


def _gelu_and_grad(x):
    x2 = x * x
    th = jnp.tanh(GELU_K * (x + GELU_C * x2 * x))
    g = 0.5 * x * (1.0 + th)
    dg = 0.5 * (1.0 + th) + 0.5 * x * (1.0 - th * th) * (GELU_K * (1.0 + 3.0 * GELU_C * x2))
    return g, dg


def _sigmoid(x):
    return 1.0 / (1.0 + jnp.exp(-x))


def _expm1(x):
    small = x * (1.0 + 0.5 * x * (1.0 + (1.0 / 3.0) * x * (1.0 + 0.25 * x)))
    return jnp.where(jnp.abs(x) < 1e-2, small, jnp.exp(x) - 1.0)


def _softplus(x):
    e = jnp.exp(-jnp.abs(x))
    small = e * (1.0 - 0.5 * e * (1.0 - (2.0 / 3.0) * e))
    return jnp.maximum(x, 0.0) + jnp.where(e < 1e-2, small, jnp.log(1.0 + e))


def _shift_down(x, s):
    if s == 0:
        return x
    rows = lax.broadcasted_iota(jnp.int32, x.shape, 0)
    return jnp.where(rows >= s, pltpu.roll(x, s, 0), 0.0)


def _shift_up(x, s):
    if s == 0:
        return x
    n = x.shape[0]
    rows = lax.broadcasted_iota(jnp.int32, x.shape, 0)
    return jnp.where(rows < n - s, pltpu.roll(x, n - s, 0), 0.0)


def _dot_nn(a, b):
    return lax.dot_general(a, b, (((1,), (0,)), ((), ())), preferred_element_type=F32)


def _dot_nt(a, b):
    return lax.dot_general(a, b, (((1,), (1,)), ((), ())), preferred_element_type=F32)


def _dot_tn(a, b):
    return lax.dot_general(a, b, (((0,), (0,)), ((), ())), preferred_element_type=F32)


def _rows8(vals, width):
    rows = lax.broadcasted_iota(jnp.int32, (8, width), 0)
    out = jnp.zeros((8, width), F32)
    for k, v in enumerate(vals):
        out = jnp.where(rows == k, jnp.broadcast_to(v, (8, width)), out)
    return out


def _tile(n, prefer):
    for c in prefer:
        if n % c == 0:
            return c
    return n


def _mm_nn(a, b, *, tn, out_dtype, name):
    m, k = a.shape
    n = b.shape[1]

    def body(a_ref, b_ref, o_ref):
        o_ref[...] = _dot_nn(a_ref[...], b_ref[...]).astype(o_ref.dtype)

    return pl.pallas_call(
        body, name=name, grid=(n // tn,),
        in_specs=[pl.BlockSpec((m, k), lambda j: (0, 0)), pl.BlockSpec((k, tn), lambda j: (0, j))],
        out_specs=pl.BlockSpec((m, tn), lambda j: (0, j)),
        out_shape=jax.ShapeDtypeStruct((m, n), out_dtype),
        compiler_params=_params("parallel"),
    )(a, b)


def _proj_in(h_bf, g_in, *, shard_major, name, transposed=False):
    t, k = h_bf.shape
    n = g_in.shape[1] if transposed else g_in.shape[2]

    def body(a_ref, b_ref, o_ref):
        o_ref[...] = _dot_nt(a_ref[...], b_ref[...]) if transposed else _dot_nn(a_ref[...], b_ref[...])

    if shard_major:
        out_spec = pl.BlockSpec((None, t, n), lambda j: (j, 0, 0))
        out_shape = jax.ShapeDtypeStruct((N_DEV, t, n), F32)
    else:
        out_spec = pl.BlockSpec((t, n), lambda j: (0, j))
        out_shape = jax.ShapeDtypeStruct((t, N_DEV * n), F32)
    return pl.pallas_call(
        body, name=name, grid=(N_DEV,),
        in_specs=[pl.BlockSpec((t, k), lambda j: (0, 0)),
                  pl.BlockSpec((None,) + g_in.shape[1:], lambda j: (j, 0, 0))],
        out_specs=out_spec, out_shape=out_shape,
        compiler_params=_params("parallel"),
    )(h_bf, g_in)


def _out_ln(y3, g_out, hin, g, b, *, n_valid, tm, name):
    nj, t, kj = y3.shape
    _, r, d = g_out.shape

    def body(y_ref, w_ref, hin_ref, g_ref, b_ref, s_ref, h_ref, hb_ref):
        w = w_ref[...].reshape(N_DEV * r, d)
        s = DN_ALPHA * hin_ref[...]
        for jj in range(nj):
            s = s + _dot_nn(y_ref[jj], w[jj * kj:(jj + 1) * kj])
        mu = jnp.mean(s, axis=-1, keepdims=True)
        xc = s - mu
        var = jnp.mean(xc * xc, axis=-1, keepdims=True)
        h = xc * lax.rsqrt(var + LN_EPS) * g_ref[...] + b_ref[...]
        s_ref[...] = s
        h_ref[...] = h
        rows = pl.program_id(0) * tm + lax.broadcasted_iota(jnp.int32, (tm, d), 0)
        hb_ref[...] = jnp.where(rows < n_valid, h, 0.0).astype(BF16)

    row = pl.BlockSpec((tm, d), lambda i: (i, 0))
    vec = pl.BlockSpec((1, d), lambda i: (0, 0))
    return pl.pallas_call(
        body, name=name, grid=(t // tm,),
        in_specs=[pl.BlockSpec((nj, tm, kj), lambda i: (0, i, 0)),
                  pl.BlockSpec((N_DEV, r, d), lambda i: (0, 0, 0)), row, vec, vec],
        out_specs=[row, row, row],
        out_shape=[jax.ShapeDtypeStruct((t, d), F32), jax.ShapeDtypeStruct((t, d), F32),
                   jax.ShapeDtypeStruct((t, d), BF16)],
        compiler_params=_params("parallel"),
    )(y3, g_out, hin, g, b)


def _out_bwd(ds_bf, g_out, *, tm, name):
    t, d = ds_bf.shape
    r = g_out.shape[1]

    def body(a_ref, w_ref, o_ref):
        o_ref[...] = _dot_nt(a_ref[...], w_ref[...].reshape(N_DEV * r, d))

    return pl.pallas_call(
        body, name=name, grid=(t // tm,),
        in_specs=[pl.BlockSpec((tm, d), lambda i: (i, 0)),
                  pl.BlockSpec((N_DEV, r, d), lambda i: (0, 0, 0))],
        out_specs=pl.BlockSpec((tm, N_DEV * r), lambda i: (i, 0)),
        out_shape=jax.ShapeDtypeStruct((t, N_DEV * r), F32),
        compiler_params=_params("parallel"),
    )(ds_bf, g_out)


def _in_bwd(dact, g_in, add, *, tm, name, alpha=DN_ALPHA, transposed=False):
    t = dact.shape[-2]
    _, k, n = g_in.shape
    if transposed:
        k, n = n, k
    halves = dact.shape[0] == 2 and dact.ndim == 3
    per = N_DEV // 2

    def body(a_ref, b_ref, add_ref, o_ref, acc_ref):
        j = pl.program_id(1)

        @pl.when(j == 0)
        def _():
            acc_ref[...] = alpha * add_ref[...]

        acc_ref[...] += _dot_nn(a_ref[...], b_ref[...]) if transposed else _dot_nt(a_ref[...], b_ref[...])

        @pl.when(j == N_DEV - 1)
        def _():
            o_ref[...] = acc_ref[...]

    if halves:
        a_spec = pl.BlockSpec((None, tm, n), lambda i, j: (j // per, i, j % per))
    elif dact.ndim == 4:
        a_spec = pl.BlockSpec((None, None, tm, n), lambda i, j: (j // per, j % per, i, 0))
    else:
        a_spec = pl.BlockSpec((None, tm, n), lambda i, j: (j, i, 0))
    return pl.pallas_call(
        body, name=name, grid=(t // tm, N_DEV),
        in_specs=[a_spec, pl.BlockSpec((None,) + g_in.shape[1:], lambda i, j: (j, 0, 0)),
                  pl.BlockSpec((tm, k), lambda i, j: (i, 0))],
        out_specs=pl.BlockSpec((tm, k), lambda i, j: (i, 0)),
        out_shape=jax.ShapeDtypeStruct((t, k), F32),
        scratch_shapes=[pltpu.VMEM((tm, k), F32)],
        compiler_params=_params("parallel", "arbitrary"),
    )(dact, g_in, add)


def _mm_nt_full(a, b, add, *, tm, name):
    t, n = a.shape
    k = b.shape[0]

    def body(a_ref, b_ref, add_ref, o_ref):
        o_ref[...] = add_ref[...] + _dot_nt(a_ref[...], b_ref[...])

    return pl.pallas_call(
        body, name=name, grid=(t // tm,),
        in_specs=[pl.BlockSpec((tm, n), lambda i: (i, 0)), pl.BlockSpec((k, n), lambda i: (0, 0)),
                  pl.BlockSpec((tm, k), lambda i: (i, 0))],
        out_specs=pl.BlockSpec((tm, k), lambda i: (i, 0)),
        out_shape=jax.ShapeDtypeStruct((t, k), F32),
        compiler_params=_params("parallel"),
    )(a, b, add)


def _w_in_grad(h_bf, dact, *, name, transposed=False):
    t, k = h_bf.shape
    halves = dact.shape[0] == 2 and dact.ndim == 3
    per = N_DEV // 2
    n = dact.shape[-1] // per if halves else dact.shape[-1]

    def body(a_ref, b_ref, o_ref):
        if transposed:
            o_ref[...] = _dot_tn(b_ref[...], a_ref[...]).astype(BF16)
        else:
            o_ref[...] = _dot_tn(a_ref[...], b_ref[...]).astype(BF16)

    if halves:
        b_spec = pl.BlockSpec((None, t, n), lambda j: (j // per, 0, j % per))
    elif dact.ndim == 4:
        b_spec = pl.BlockSpec((None, None, t, n), lambda j: (j // per, j % per, 0, 0))
    else:
        b_spec = pl.BlockSpec((None, t, n), lambda j: (j, 0, 0))
    return pl.pallas_call(
        body, name=name, grid=(N_DEV,),
        in_specs=[pl.BlockSpec((t, k), lambda j: (0, 0)), b_spec],
        out_specs=pl.BlockSpec((None, n, k) if transposed else (None, k, n), lambda j: (j, 0, 0)),
        out_shape=jax.ShapeDtypeStruct((N_DEV, n, k) if transposed else (N_DEV, k, n), BF16),
        compiler_params=_params("parallel"),
    )(h_bf, dact)


def _w_out_grad(y3, ds_bf, r, *, name):
    nj, t, kj = y3.shape
    d = ds_bf.shape[1]
    unit = r * LANES // math.gcd(r, LANES)
    ks = max([c for c in range(unit, min(kj, 768) + 1, unit) if kj % c == 0], default=kj)
    gsz = ks // r
    per = kj // ks

    def body(a_ref, b_ref, o_ref):
        o_ref[...] = _dot_tn(a_ref[...], b_ref[...]).reshape(gsz, r, d).astype(BF16)

    return pl.pallas_call(
        body, name=name, grid=(nj * per,),
        in_specs=[pl.BlockSpec((None, t, ks), lambda j: (j // per, 0, j % per)),
                  pl.BlockSpec((t, d), lambda j: (0, 0))],
        out_specs=pl.BlockSpec((gsz, r, d), lambda j: (j, 0, 0)),
        out_shape=jax.ShapeDtypeStruct((N_DEV, r, d), BF16),
        compiler_params=_params("parallel"),
    )(y3, ds_bf)


def _mm_tn_cols(a, b, *, tn, name):
    t, m = a.shape
    n = b.shape[1]

    def body(a_ref, b_ref, o_ref):
        o_ref[...] = _dot_tn(a_ref[...], b_ref[...])

    return pl.pallas_call(
        body, name=name, grid=(n // tn,),
        in_specs=[pl.BlockSpec((t, m), lambda j: (0, 0)), pl.BlockSpec((t, tn), lambda j: (0, j))],
        out_specs=pl.BlockSpec((m, tn), lambda j: (0, j)),
        out_shape=jax.ShapeDtypeStruct((m, n), F32),
        compiler_params=_params("parallel"),
    )(a, b)


def _ln_bwd(dout, s, g, *, tm, name):
    t, d = s.shape

    def body(do_ref, s_ref, g_ref, ds_ref, dsb_ref, gb_ref):
        i = pl.program_id(0)
        sv = s_ref[...]
        do = do_ref[...]
        mu = jnp.mean(sv, axis=-1, keepdims=True)
        xc = sv - mu
        var = jnp.mean(xc * xc, axis=-1, keepdims=True)
        rstd = lax.rsqrt(var + LN_EPS)
        xhat = xc * rstd
        dxhat = do * g_ref[...]
        m1 = jnp.mean(dxhat, axis=-1, keepdims=True)
        m2 = jnp.mean(dxhat * xhat, axis=-1, keepdims=True)
        ds = rstd * (dxhat - m1 - xhat * m2)
        ds_ref[...] = ds
        dsb_ref[...] = ds.astype(BF16)
        upd = _rows8([jnp.sum(do * xhat, axis=0, keepdims=True), jnp.sum(do, axis=0, keepdims=True)], d)

        @pl.when(i == 0)
        def _():
            gb_ref[...] = upd

        @pl.when(i > 0)
        def _():
            gb_ref[...] += upd

    row = pl.BlockSpec((tm, d), lambda i: (i, 0))
    return pl.pallas_call(
        body, name=name, grid=(t // tm,),
        in_specs=[row, row, pl.BlockSpec((1, d), lambda i: (0, 0))],
        out_specs=[row, row, pl.BlockSpec((8, d), lambda i: (0, 0))],
        out_shape=[jax.ShapeDtypeStruct((t, d), F32), jax.ShapeDtypeStruct((t, d), BF16),
                   jax.ShapeDtypeStruct((8, d), F32)],
        compiler_params=_params("arbitrary"),
    )(dout, s, g)


def _roll_down(x, s):
    return x if s == 0 else pltpu.roll(x, s, 0)


def _conv_taps(x, wb, width):
    y = jnp.broadcast_to(wb[width:width + 1, :], x.shape)
    for k in range(width):
        y = y + _roll_down(x, width - 1 - k) * wb[k:k + 1, :]
    return y


def _conv_taps_bwd(dy, x, wb, width):
    n = dy.shape[0]
    dx = jnp.zeros_like(dy)
    rows = []
    for k in range(width):
        s = width - 1 - k
        dy_up = dy if s == 0 else pltpu.roll(dy, n - s, 0)
        dx = dx + dy_up * wb[k:k + 1, :]
        rows.append(jnp.sum(dy_up * x, axis=0, keepdims=True))
    rows.append(jnp.sum(dy, axis=0, keepdims=True))
    t_idx = lax.broadcasted_iota(jnp.int32, dy.shape, 0)
    return jnp.where(t_idx < n - (width - 1), dx, 0.0), _rows8(rows, dy.shape[1])


def _convglu_fwd(z3, fwb3, *, name):
    _, t, n = z3.shape
    half = N_DEV // 2
    nc = pl.cdiv(n, LANES)

    def body(zg_ref, zv_ref, wg_ref, wv_ref, y_ref):
        gate = _conv_taps(zg_ref[...], wg_ref[...], 3)
        val = _conv_taps(zv_ref[...], wv_ref[...], 3)
        y_ref[...] = (_gelu(gate) * val).astype(BF16)

    zblk = lambda off: pl.BlockSpec((None, t, LANES), lambda j, c: (j + off, 0, c))
    wblk = lambda off: pl.BlockSpec((None, 8, LANES), lambda j, c: (j + off, 0, c))
    return pl.pallas_call(
        body, name=name, grid=(half, nc),
        in_specs=[zblk(0), zblk(half), wblk(0), wblk(half)],
        out_specs=zblk(0),
        out_shape=jax.ShapeDtypeStruct((half, t, n), BF16),
        compiler_params=_params("parallel", "parallel"),
    )(z3, z3, fwb3, fwb3)


def _ffn_bwd_mid(ds_bf, g_out, z3, fwb3, *, name):
    t, d = ds_bf.shape
    r = g_out.shape[1]
    n = z3.shape[2]
    half = N_DEV // 2
    nc = pl.cdiv(n, LANES)
    assert n == 2 * r

    def body(ds_ref, w_ref, zg_ref, zv_ref, wg_ref, wv_ref, dz_ref, dwb_ref, wsc_ref):
        c = pl.program_id(1)

        @pl.when(c == 0)
        def _():
            wsc_ref[0:r, :] = w_ref[0]
            wsc_ref[r:2 * r, :] = w_ref[1]
            if nc * LANES > n:
                wsc_ref[n:nc * LANES, :] = jnp.zeros((nc * LANES - n, d), BF16)

        w = wsc_ref[pl.ds(pl.multiple_of(c * LANES, LANES), LANES), :]
        dyf = _dot_nt(ds_ref[...], w)
        zg, zv = zg_ref[...], zv_ref[...]
        wg, wv = wg_ref[...], wv_ref[...]
        gate = _conv_taps(zg, wg, 3)
        val = _conv_taps(zv, wv, 3)
        gl, dgl = _gelu_and_grad(gate)
        dzg, dwg = _conv_taps_bwd(dyf * val * dgl, zg, wg, 3)
        dzv, dwv = _conv_taps_bwd(dyf * gl, zv, wv, 3)
        dz_ref[0] = dzg.astype(BF16)
        dz_ref[1] = dzv.astype(BF16)
        dwb_ref[0] = dwg
        dwb_ref[1] = dwv

    zblk = lambda off: pl.BlockSpec((None, t, LANES), lambda j, c: (j + off, 0, c))
    wblk = lambda off: pl.BlockSpec((None, 8, LANES), lambda j, c: (j + off, 0, c))
    return pl.pallas_call(
        body, name=name, grid=(half, nc),
        in_specs=[pl.BlockSpec((t, d), lambda j, c: (0, 0)),
                  pl.BlockSpec((2, r, d), lambda j, c: (j, 0, 0)),
                  zblk(0), zblk(half), wblk(0), wblk(half)],
        out_specs=[pl.BlockSpec((2, None, t, LANES), lambda j, c: (0, j, 0, c)),
                   pl.BlockSpec((2, None, 8, LANES), lambda j, c: (0, j, 0, c))],
        out_shape=[jax.ShapeDtypeStruct((2, half, t, n), BF16), jax.ShapeDtypeStruct((2, half, 8, n), F32)],
        scratch_shapes=[pltpu.VMEM((nc * LANES, d), BF16)],
        compiler_params=_params("parallel", "arbitrary"),
    )(ds_bf, g_out, z3, z3, fwb3, fwb3)


def _conv_a_fwd(gr, cwb, *, cb, name):
    t, r2 = gr.shape
    r = r2 // 2
    nb = r // cb

    def body(x_ref, w_ref, o_ref):
        o_ref[...] = _conv_taps(x_ref[...], w_ref[...], 4)

    return pl.pallas_call(
        body, name=name, grid=(nb,),
        in_specs=[pl.BlockSpec((t, cb), lambda j: (0, j + nb)), pl.BlockSpec((8, cb), lambda j: (0, j))],
        out_specs=pl.BlockSpec((t, cb), lambda j: (0, j)),
        out_shape=jax.ShapeDtypeStruct((t, r), F32),
        compiler_params=_params("parallel"),
    )(gr, cwb)


def _gates_fwd(rec, bd_r, bd_i, vecs, *, tm, name):
    t, r_dim = rec.shape
    nb, cb, _ = bd_r.shape

    def body(x_ref, wr_ref, wi_ref, v_ref, a_ref, u_ref, r_ref, i_ref):
        x = x_ref[...]
        xb = x.astype(BF16)
        v = v_ref[...]
        r = _sigmoid(_dot_nn(xb, wr_ref[...]) + v[0:1, :])
        i = _sigmoid(_dot_nn(xb, wi_ref[...]) + v[1:2, :])
        log_a = (-LRU_C) * r * _softplus(-v[2:3, :])
        a_ref[...] = jnp.exp(log_a)
        u_ref[...] = jnp.sqrt(-_expm1(2.0 * log_a)) * (i * x)
        r_ref[...] = r
        i_ref[...] = i

    blk = pl.BlockSpec((tm, cb), lambda j, i: (i, j))
    wspec = pl.BlockSpec((None, cb, cb), lambda j, i: (j, 0, 0))
    out = jax.ShapeDtypeStruct((t, r_dim), F32)
    return pl.pallas_call(
        body, name=name, grid=(nb, t // tm),
        in_specs=[blk, wspec, wspec, pl.BlockSpec((8, cb), lambda j, i: (0, j))],
        out_specs=[blk, blk, blk, blk],
        out_shape=[out, out, out, out],
        compiler_params=_params("parallel", "parallel"),
    )(rec, bd_r, bd_i, vecs)


def _scan_fwd(a, u, gr, *, cb, name):
    t, r = a.shape
    nb = r // cb
    seg = t // SUBLANES

    def body(a_ref, u_ref, g_ref, h_ref, y_ref, p_ref):
        def step(k, carry):
            h, p = carry
            rows = pl.ds(k, SUBLANES, stride=seg)
            av = a_ref[rows, :]
            h = av * h + u_ref[rows, :]
            p = av * p
            h_ref[rows, :] = h
            p_ref[rows, :] = p
            return h, p

        h_fin, p_fin = lax.fori_loop(0, seg, step, (jnp.zeros((SUBLANES, cb), F32), jnp.ones((SUBLANES, cb), F32)),
                                     unroll=4)
        carry = h_fin[0:1, :]
        for s in range(1, SUBLANES):
            rows = slice(s * seg, (s + 1) * seg)
            h_ref[rows, :] = h_ref[rows, :] + p_ref[rows, :] * carry
            carry = h_fin[s:s + 1, :] + p_fin[s:s + 1, :] * carry
        y_ref[...] = (_gelu(g_ref[...]) * h_ref[...]).astype(BF16)

    blk = pl.BlockSpec((t, cb), lambda j: (0, j))
    return pl.pallas_call(
        body, name=name, grid=(nb,),
        in_specs=[blk, blk, blk],
        out_specs=[blk, pl.BlockSpec((None, t, cb), lambda j: (0, 0, j))],
        out_shape=[jax.ShapeDtypeStruct((t, r), F32), jax.ShapeDtypeStruct((1, t, r), BF16)],
        scratch_shapes=[pltpu.VMEM((t, cb), F32)],
        compiler_params=_params("parallel"),
    )(a, u, gr)


def _scan_bwd(dy, gr, hr, a, *, cb, name):
    t, r = a.shape
    nb = r // cb
    seg = t // SUBLANES

    def body(dy_ref, g_ref, h_ref, a_ref, dh_ref, da_ref, dg_ref, q_ref):
        gl, dgl = _gelu_and_grad(g_ref[...])
        dyv = dy_ref[...]
        dh_ref[...] = dyv * gl
        dg_ref[...] = (dyv * h_ref[...] * dgl).astype(BF16)

        def step(k, carry):
            cin, q = carry
            rows = pl.ds(seg - 1 - k, SUBLANES, stride=seg)
            dh = dh_ref[rows, :] + cin
            dh_ref[rows, :] = dh
            q_ref[rows, :] = q
            av = a_ref[rows, :]
            return av * dh, av * q

        c_fin, q_fin = lax.fori_loop(0, seg, step, (jnp.zeros((SUBLANES, cb), F32), jnp.ones((SUBLANES, cb), F32)),
                                     unroll=4)
        carry = c_fin[SUBLANES - 1:SUBLANES, :]
        for s in range(SUBLANES - 2, -1, -1):
            rows = slice(s * seg, (s + 1) * seg)
            dh_ref[rows, :] = dh_ref[rows, :] + q_ref[rows, :] * carry
            carry = c_fin[s:s + 1, :] + q_fin[s:s + 1, :] * carry
        da_ref[...] = dh_ref[...] * _shift_down(h_ref[...], 1)

    blk = pl.BlockSpec((t, cb), lambda j: (0, j))
    return pl.pallas_call(
        body, name=name, grid=(nb,),
        in_specs=[blk, blk, blk, blk],
        out_specs=[blk, blk, blk],
        out_shape=[jax.ShapeDtypeStruct((t, r), F32), jax.ShapeDtypeStruct((t, r), F32),
                   jax.ShapeDtypeStruct((t, r), BF16)],
        scratch_shapes=[pltpu.VMEM((t, cb), F32)],
        compiler_params=_params("parallel"),
    )(dy, gr, hr, a)


def _gates_bwd(rec, r, i, a, dh, da, bd_r, bd_i, vecs, *, tm, name):
    t, r_dim = rec.shape
    nb, cb, _ = bd_r.shape

    def body(x_ref, r_ref, i_ref, a_ref, dh_ref, da_ref, wr_ref, wi_ref, v_ref, dx_ref, dpr_ref, dpi_ref, dv_ref):
        step = pl.program_id(1)
        x, r, i, a, dh, da = x_ref[...], r_ref[...], i_ref[...], a_ref[...], dh_ref[...], da_ref[...]
        lam = v_ref[...][2:3, :]
        sp = _softplus(-lam)
        a2 = a * a
        mult = jnp.sqrt(-_expm1(2.0 * (-LRU_C) * r * sp))
        d_i = dh * mult * x
        d_log_a = da * a - (dh * i * x) * a2 / mult
        d_r = d_log_a * ((-LRU_C) * sp)
        d_sp = jnp.sum(d_log_a * ((-LRU_C) * r), axis=0, keepdims=True)
        d_pre_r = d_r * r * (1.0 - r)
        d_pre_i = d_i * i * (1.0 - i)
        dprb = d_pre_r.astype(BF16)
        dpib = d_pre_i.astype(BF16)
        dx_ref[...] = dh * mult * i + _dot_nt(dprb, wr_ref[...]) + _dot_nt(dpib, wi_ref[...])
        dpr_ref[...] = dprb
        dpi_ref[...] = dpib
        upd = _rows8([jnp.sum(d_pre_r, axis=0, keepdims=True), jnp.sum(d_pre_i, axis=0, keepdims=True),
                      -d_sp * _sigmoid(-lam)], cb)

        @pl.when(step == 0)
        def _():
            dv_ref[...] = upd

        @pl.when(step > 0)
        def _():
            dv_ref[...] += upd

    blk = pl.BlockSpec((tm, cb), lambda j, i: (i, j))
    wspec = pl.BlockSpec((None, cb, cb), lambda j, i: (j, 0, 0))
    vspec = pl.BlockSpec((8, cb), lambda j, i: (0, j))
    return pl.pallas_call(
        body, name=name, grid=(nb, t // tm),
        in_specs=[blk] * 6 + [wspec, wspec, vspec],
        out_specs=[blk, blk, blk, vspec],
        out_shape=[jax.ShapeDtypeStruct((t, r_dim), F32), jax.ShapeDtypeStruct((t, r_dim), BF16),
                   jax.ShapeDtypeStruct((t, r_dim), BF16), jax.ShapeDtypeStruct((8, r_dim), F32)],
        compiler_params=_params("parallel", "arbitrary"),
    )(rec, r, i, a, dh, da, bd_r, bd_i, vecs)


def _bd_grad(rec, dpr, dpi, *, cb, name):
    t, r = rec.shape
    nb = r // cb

    def body(x_ref, dr_ref, di_ref, gr_ref, gi_ref):
        xb = x_ref[...].astype(BF16)
        gr_ref[...] = _dot_tn(xb, dr_ref[...])
        gi_ref[...] = _dot_tn(xb, di_ref[...])

    blk = pl.BlockSpec((t, cb), lambda j: (0, j))
    wspec = pl.BlockSpec((None, cb, cb), lambda j: (j, 0, 0))
    out = jax.ShapeDtypeStruct((nb, cb, cb), F32)
    return pl.pallas_call(
        body, name=name, grid=(nb,),
        in_specs=[blk, blk, blk], out_specs=[wspec, wspec], out_shape=[out, out],
        compiler_params=_params("parallel"),
    )(rec, dpr, dpi)


def _conv_a_bwd(d_rec, gr, dgate, cwb, *, cb, name):
    t, r = d_rec.shape
    nb = r // cb

    def body(dy_ref, x_ref, dg_ref, w_ref, dact_ref, dw_ref):
        dx, dw = _conv_taps_bwd(dy_ref[...], x_ref[...], w_ref[...], 4)
        dact_ref[0] = dg_ref[...]
        dact_ref[1] = dx.astype(BF16)
        dw_ref[...] = dw

    blk = pl.BlockSpec((t, cb), lambda j: (0, j))
    vspec = pl.BlockSpec((8, cb), lambda j: (0, j))
    return pl.pallas_call(
        body, name=name, grid=(nb,),
        in_specs=[blk, pl.BlockSpec((t, cb), lambda j: (0, j + nb)), blk, vspec],
        out_specs=[pl.BlockSpec((2, t, cb), lambda j: (0, 0, j)), vspec],
        out_shape=[jax.ShapeDtypeStruct((2, t, r), BF16), jax.ShapeDtypeStruct((8, r), F32)],
        compiler_params=_params("parallel"),
    )(d_rec, gr, dgate, cwb)


def _split3(x):
    p0 = x.astype(BF16)
    r1 = x - p0.astype(F32)
    p1 = r1.astype(BF16)
    p2 = (r1 - p1.astype(F32)).astype(BF16)
    return p0, p1, p2


def _fgate_fwd(fp, fb, *, tq, name):
    t = fp.shape[0]

    def body(f_ref, b_ref, c_ref, ct_ref):
        logf = -_softplus(-(f_ref[...] + b_ref[...]))
        rows = pl.program_id(0) * tq + lax.broadcasted_iota(jnp.int32, (tq, t), 0)
        cols = lax.broadcasted_iota(jnp.int32, (tq, t), 1)
        tri = (cols <= rows).astype(BF16)
        p0, p1, p2 = _split3(logf)
        c = _dot_nn(tri, p0) + _dot_nn(tri, p1) + _dot_nn(tri, p2)
        c_ref[...] = c
        ct_ref[...] = c.T

    return pl.pallas_call(
        body, name=name, grid=(t // tq,),
        in_specs=[pl.BlockSpec((t, LANES), lambda i: (0, 0)), pl.BlockSpec((1, LANES), lambda i: (0, 0))],
        out_specs=[pl.BlockSpec((tq, LANES), lambda i: (i, 0)), pl.BlockSpec((LANES, tq), lambda i: (0, i))],
        out_shape=[jax.ShapeDtypeStruct((t, LANES), F32), jax.ShapeDtypeStruct((LANES, t), F32)],
        compiler_params=_params("parallel"),
    )(fp, fb)


def _fgate_bwd(dct, fp, fb, *, tq, name):
    t = fp.shape[0]

    def body(d_ref, f_ref, b_ref, o_ref, db_ref):
        i = pl.program_id(0)
        rows = lax.broadcasted_iota(jnp.int32, (t, tq), 0)
        cols = i * tq + lax.broadcasted_iota(jnp.int32, (t, tq), 1)
        tri = (rows >= cols).astype(BF16)
        p0, p1, p2 = _split3(d_ref[...])
        dlogf = (_dot_nn(p0, tri) + _dot_nn(p1, tri) + _dot_nn(p2, tri)).T
        df = dlogf * _sigmoid(-(f_ref[...] + b_ref[...]))
        o_ref[...] = df.astype(BF16)
        upd = _rows8([jnp.sum(df, axis=0, keepdims=True)], LANES)

        @pl.when(i == 0)
        def _():
            db_ref[...] = upd

        @pl.when(i > 0)
        def _():
            db_ref[...] += upd

    return pl.pallas_call(
        body, name=name, grid=(t // tq,),
        in_specs=[pl.BlockSpec((LANES, t), lambda i: (0, 0)), pl.BlockSpec((tq, LANES), lambda i: (i, 0)),
                  pl.BlockSpec((1, LANES), lambda i: (0, 0))],
        out_specs=[pl.BlockSpec((tq, LANES), lambda i: (i, 0)), pl.BlockSpec((8, LANES), lambda i: (0, 0))],
        out_shape=[jax.ShapeDtypeStruct((t, LANES), BF16), jax.ShapeDtypeStruct((8, LANES), F32)],
        compiler_params=_params("arbitrary"),
    )(dct, fp, fb)


def _pair_sum(a, b, *, tm, name):
    t, d = a.shape

    def body(a_ref, b_ref, o_ref):
        o_ref[...] = (a_ref[...] + b_ref[...]).astype(BF16)

    row = pl.BlockSpec((tm, d), lambda i: (i, 0))
    return pl.pallas_call(
        body, name=name, grid=(t // tm,), in_specs=[row, row], out_specs=row,
        out_shape=jax.ShapeDtypeStruct((t, d), BF16), compiler_params=_params("parallel"),
    )(a, b)


def _head_masks(dh):
    lane = lax.broadcasted_iota(jnp.int32, (1, LANES), 1)
    return [((lane >= e * dh) & (lane < (e + 1) * dh)) for e in range(LANES // dh)]


def _head_c(c_blk, ct_blk, head):
    lane = lax.broadcasted_iota(jnp.int32, c_blk.shape, 1)
    c_col = jnp.sum(jnp.where(lane == head, c_blk, 0.0), axis=1, keepdims=True)
    sub = lax.broadcasted_iota(jnp.int32, ct_blk.shape, 0)
    c_row = jnp.sum(jnp.where(sub == head, ct_blk, 0.0), axis=0, keepdims=True)
    return c_col, c_row


def _attn_probs(qm, k, c_col, c_row, q0, scale):
    tq, t = qm.shape[0], k.shape[0]
    s = _dot_nt(qm, k) * scale + c_col - c_row
    qi = q0 + lax.broadcasted_iota(jnp.int32, (tq, t), 0)
    ki = lax.broadcasted_iota(jnp.int32, (tq, t), 1)
    s = jnp.where(ki <= qi, s, -jnp.inf)
    m = jnp.max(s, axis=-1, keepdims=True)
    p = jnp.exp(s - m)
    return p / jnp.sum(p, axis=-1, keepdims=True)


def _key_buckets(t, tq):
    step = 3 * tq
    return tuple(range(step, t, step)) + (t,)


def _for_prefix(needed, buckets, fn):
    prev = 0
    for length in buckets:
        pl.when((needed > prev) & (needed <= length))(lambda length=length: fn(length))
        prev = length


def _attn_fwd(qg, kv, c, ct, *, tq, name):
    t, d2 = qg.shape
    d = d2 // 2
    dh = d // N_HEADS
    hpb = LANES // dh
    nhb = d // LANES
    scale = dh ** -0.5
    buckets = _key_buckets(t, tq)

    def body(q_ref, og_ref, k_ref, v_ref, c_ref, ct_ref, o_ref, y_ref):
        hb = pl.program_id(0)
        q0 = pl.program_id(1) * tq

        def run(length):
            q = q_ref[...]
            k = k_ref[0:length, :]
            v = v_ref[0:length, :]
            o = jnp.zeros((tq, LANES), F32)
            for e, msk in enumerate(_head_masks(dh)):
                c_col, c_row = _head_c(c_ref[...], ct_ref[:, 0:length], hb * hpb + e)
                p = _attn_probs(jnp.where(msk, q, 0.0).astype(BF16), k, c_col, c_row, q0, scale)
                o = o + _dot_nn(p.astype(BF16), jnp.where(msk, v, jnp.zeros_like(v)))
            o_ref[...] = o
            y_ref[...] = (o * _sigmoid(og_ref[...])).astype(BF16)

        _for_prefix(q0 + tq, buckets, run)

    qblk = pl.BlockSpec((tq, LANES), lambda h, i: (i, h))
    return pl.pallas_call(
        body, name=name, grid=(nhb, t // tq),
        in_specs=[qblk, pl.BlockSpec((tq, LANES), lambda h, i: (i, h + nhb)),
                  pl.BlockSpec((t, LANES), lambda h, i: (0, h)), pl.BlockSpec((t, LANES), lambda h, i: (0, h + nhb)),
                  pl.BlockSpec((tq, LANES), lambda h, i: (i, 0)), pl.BlockSpec((N_HEADS, t), lambda h, i: (0, 0))],
        out_specs=[qblk, pl.BlockSpec((None, tq, LANES), lambda h, i: (0, i, h))],
        out_shape=[jax.ShapeDtypeStruct((t, d), F32), jax.ShapeDtypeStruct((1, t, d), BF16)],
        compiler_params=_params("parallel", "parallel"),
    )(qg, qg, kv, kv, c, ct)


def _attn_bwd(dy, qg, o, kv, c, ct, *, tq, name):
    t, d2 = qg.shape
    d = d2 // 2
    dh = d // N_HEADS
    hpb = LANES // dh
    nhb = d // LANES
    scale = dh ** -0.5
    buckets = _key_buckets(t, tq)

    def body(dy_ref, q_ref, og_ref, o_ref, k_ref, v_ref, c_ref, ct_ref, dqg_ref, dk_ref, dv_ref, dc_ref):
        hb = pl.program_id(0)
        step = pl.program_id(1)
        q0 = step * tq

        @pl.when(step == 0)
        def _():
            dk_ref[...] = jnp.zeros((t, LANES), F32)
            dv_ref[...] = jnp.zeros((t, LANES), F32)
            dc_ref[...] = jnp.zeros((8, t), F32)

        def run(length):
            q = q_ref[...]
            k = k_ref[0:length, :]
            v = v_ref[0:length, :]
            sg = _sigmoid(og_ref[...])
            dyv = dy_ref[...]
            do = dyv * sg
            dqg_ref[1] = (dyv * o_ref[...] * sg * (1.0 - sg)).astype(BF16)
            dq = jnp.zeros((tq, LANES), F32)
            dk = jnp.zeros((length, LANES), F32)
            dv = jnp.zeros((length, LANES), F32)
            dc_rows = []
            for e, msk in enumerate(_head_masks(dh)):
                c_col, c_row = _head_c(c_ref[...], ct_ref[:, 0:length], hb * hpb + e)
                qm = jnp.where(msk, q, 0.0).astype(BF16)
                dom = jnp.where(msk, do, 0.0).astype(BF16)
                p = _attn_probs(qm, k, c_col, c_row, q0, scale)
                dp = _dot_nt(dom, v)
                dsc = p * (dp - jnp.sum(p * dp, axis=-1, keepdims=True))
                dsb = (dsc * scale).astype(BF16)
                dq = dq + _dot_nn(dsb, jnp.where(msk, k, jnp.zeros_like(k)))
                dk = dk + _dot_tn(dsb, qm)
                dv = dv + _dot_tn(p.astype(BF16), dom)
                dc_rows.append(-jnp.sum(dsc, axis=0, keepdims=True))
            dqg_ref[0] = dq.astype(BF16)
            dk_ref[0:length, :] += dk
            dv_ref[0:length, :] += dv
            dc_ref[:, 0:length] += _rows8(dc_rows, length)

        _for_prefix(q0 + tq, buckets, run)

    qblk = pl.BlockSpec((tq, LANES), lambda h, i: (i, h))
    kblk = pl.BlockSpec((t, LANES), lambda h, i: (0, h))
    return pl.pallas_call(
        body, name=name, grid=(nhb, t // tq),
        in_specs=[qblk, qblk, pl.BlockSpec((tq, LANES), lambda h, i: (i, h + nhb)), qblk,
                  kblk, pl.BlockSpec((t, LANES), lambda h, i: (0, h + nhb)),
                  pl.BlockSpec((tq, LANES), lambda h, i: (i, 0)), pl.BlockSpec((N_HEADS, t), lambda h, i: (0, 0))],
        out_specs=[pl.BlockSpec((2, tq, LANES), lambda h, i: (0, i, h)), kblk, kblk,
                   pl.BlockSpec((None, 8, t), lambda h, i: (h, 0, 0))],
        out_shape=[jax.ShapeDtypeStruct((2, t, d), BF16), jax.ShapeDtypeStruct((t, d), F32),
                   jax.ShapeDtypeStruct((t, d), F32), jax.ShapeDtypeStruct((nhb, 8, t), F32)],
        compiler_params=_params("parallel", "arbitrary"),
    )(dy, qg, qg, o, kv, kv, c, ct)


def _loss_bwd(h, tgt, *, lo, hi, tm, name):
    t, d = h.shape

    def body(h_ref, t_ref, l_ref, dy_ref):
        i = pl.program_id(0)
        rows = i * tm + lax.broadcasted_iota(jnp.int32, (tm, d), 0)
        err = jnp.where((rows >= lo) & (rows < hi), h_ref[...] - t_ref[...], 0.0)
        dy_ref[...] = err * (1.0 / d)
        part = jnp.sum(jnp.sum(err * err, axis=0, keepdims=True), axis=1, keepdims=True) * (0.5 / d)
        upd = jnp.broadcast_to(part, (8, LANES))

        @pl.when(i == 0)
        def _():
            l_ref[...] = upd

        @pl.when(i > 0)
        def _():
            l_ref[...] += upd

    row = pl.BlockSpec((tm, d), lambda i: (i, 0))
    return pl.pallas_call(
        body, name=name, grid=(t // tm,),
        in_specs=[row, row],
        out_specs=[pl.BlockSpec((8, LANES), lambda i: (0, 0)), row],
        out_shape=[jax.ShapeDtypeStruct((8, LANES), F32), jax.ShapeDtypeStruct((t, d), F32)],
        compiler_params=_params("arbitrary"),
    )(h, tgt)


def _adamw_math(w, gv, m, v):
    bc1 = 1.0 / (1.0 - ADAM_B1 ** ADAM_STEP)
    bc2 = 1.0 / (1.0 - ADAM_B2 ** ADAM_STEP)
    nm = ADAM_B1 * m + (1.0 - ADAM_B1) * gv
    nv = ADAM_B2 * v + (1.0 - ADAM_B2) * (gv * gv)
    delta = (-ADAM_LR) * ((nm * bc1) / (jnp.sqrt(nv * bc2) + ADAM_EPS) + ADAM_WD * w)
    return delta, nm, nv


def _adamw(w, g, m, v, *, name):
    r, c = w.shape
    tr = r
    for cand in (512, 256, 128, 64, 32, 16, 8):
        if r % cand == 0 and r > cand:
            tr = cand
            break

    def body(w_ref, g_ref, m_ref, v_ref, d_ref, nm_ref, nv_ref):
        d_ref[...], nm_ref[...], nv_ref[...] = _adamw_math(w_ref[...], g_ref[...], m_ref[...], v_ref[...])

    blk = pl.BlockSpec((tr, c), lambda i: (i, 0))
    out = jax.ShapeDtypeStruct((r, c), F32)
    return pl.pallas_call(
        body, name=name, grid=(r // tr,),
        in_specs=[blk] * 4, out_specs=[blk] * 3, out_shape=[out] * 3,
        compiler_params=_params("parallel"),
    )(w, g, m, v)


def _sum_adamw(recvs, sends, me, w, m, v, *, name):
    n_l = len(recvs)
    _, r, c = recvs[0].shape
    tr = _tile(r, (256, 192, 176, 128, 96, 64, 48, 32, 16))

    def body(me_ref, *refs):
        p_refs, own_refs = refs[:n_l], refs[n_l:2 * n_l]
        w_ref, m_ref, v_ref, g_ref, d_ref, nm_ref, nv_ref, acc_ref = refs[2 * n_l:]
        layer = pl.program_id(0)
        mine = me_ref[0]
        for k in range(n_l):
            @pl.when(layer == k)
            def _(k=k):
                acc_ref[...] = jnp.zeros((tr, c), F32)
                for dev in range(N_DEV):
                    @pl.when(mine == dev)
                    def _():
                        acc_ref[...] += own_refs[k][...].astype(F32)

                    @pl.when(mine != dev)
                    def _(dev=dev):
                        acc_ref[...] += p_refs[k][dev].astype(F32)
                acc = acc_ref[...]
                g_ref[...] = acc
                d_ref[...], nm_ref[...], nv_ref[...] = _adamw_math(w_ref[...], acc, m_ref[...], v_ref[...])

    p_specs = [pl.BlockSpec((N_DEV, tr, c), lambda l, i, me_ref, k=k: (0, jnp.where(l == k, i, 0), 0))
               for k in range(n_l)]
    own_specs = [pl.BlockSpec((None, tr, c), lambda l, i, me_ref, k=k: (me_ref[0], jnp.where(l == k, i, 0), 0))
                 for k in range(n_l)]
    blk = pl.BlockSpec((None, tr, c), lambda l, i, me_ref: (l, i, 0))
    out = jax.ShapeDtypeStruct((n_l, r, c), F32)
    return pl.pallas_call(
        body, name=name,
        grid_spec=pltpu.PrefetchScalarGridSpec(
            num_scalar_prefetch=1, grid=(n_l, r // tr),
            in_specs=p_specs + own_specs + [blk] * 3, out_specs=[blk] * 4,
            scratch_shapes=[pltpu.VMEM((tr, c), F32)]),
        out_shape=[out] * 4,
        compiler_params=_params("arbitrary", "arbitrary"),
    )(me, *recvs, *sends, w, m, v)


def _sum8(parts, *, name):
    _, r, c = parts.shape
    tr = r
    for cand in (512, 256, 128, 64, 32, 16):
        if r % cand == 0 and r > cand:
            tr = cand
            break

    def body(p_ref, o_ref):
        acc = p_ref[0].astype(F32)
        for k in range(1, N_DEV):
            acc = acc + p_ref[k].astype(F32)
        o_ref[...] = acc

    return pl.pallas_call(
        body, name=name, grid=(r // tr,),
        in_specs=[pl.BlockSpec((N_DEV, tr, c), lambda i: (0, i, 0))],
        out_specs=pl.BlockSpec((tr, c), lambda i: (i, 0)),
        out_shape=jax.ShapeDtypeStruct((r, c), F32),
        compiler_params=_params("parallel"),
    )(parts)


def _my_index():
    return 4 * lax.axis_index("x") + 2 * lax.axis_index("y") + lax.axis_index("c")


def _peer(k):
    x, y, c = lax.axis_index("x"), lax.axis_index("y"), lax.axis_index("c")
    px = x ^ ((k >> 2) & 1)
    py = y ^ ((k >> 1) & 1)
    pc = c ^ (k & 1)
    return (px, py, pc), 4 * px + 2 * py + pc


def _all_gather(shards, *, name):
    n_arr = len(shards)

    def body(*refs):
        ins, outs = refs[:n_arr], refs[n_arr:2 * n_arr]
        send_sems, recv_sems, local_sems = refs[2 * n_arr:]
        me = _my_index()
        local = [pltpu.make_async_copy(ins[n], outs[n].at[me], local_sems.at[n]) for n in range(n_arr)]
        for cp in local:
            cp.start()
        sends = []
        for k in range(1, N_DEV):
            peer, _ = _peer(k)
            for n in range(n_arr):
                cp = pltpu.make_async_remote_copy(
                    src_ref=ins[n], dst_ref=outs[n].at[me], send_sem=send_sems.at[n, k - 1],
                    recv_sem=recv_sems.at[n, k - 1], device_id=peer, device_id_type=pl.DeviceIdType.MESH)
                cp.start()
                sends.append(cp)
        for k in range(1, N_DEV):
            peer, pidx = _peer(k)
            for n in range(n_arr):
                pltpu.make_async_remote_copy(
                    src_ref=ins[n], dst_ref=outs[n].at[pidx], send_sem=send_sems.at[n, k - 1],
                    recv_sem=recv_sems.at[n, k - 1], device_id=peer, device_id_type=pl.DeviceIdType.MESH).wait_recv()
        for cp in sends:
            cp.wait_send()
        for cp in local:
            cp.wait()

    hbm = pl.BlockSpec(memory_space=pl.ANY)
    return pl.pallas_call(
        body, name=name,
        in_specs=[hbm] * n_arr, out_specs=[hbm] * n_arr,
        out_shape=[jax.ShapeDtypeStruct((N_DEV,) + s.shape, s.dtype) for s in shards],
        scratch_shapes=[pltpu.SemaphoreType.DMA((n_arr, N_DEV - 1)), pltpu.SemaphoreType.DMA((n_arr, N_DEV - 1)),
                        pltpu.SemaphoreType.DMA((n_arr,))],
        compiler_params=pltpu.CompilerParams(has_side_effects=True),
    )(*shards)


_HBM = pl.BlockSpec(memory_space=pltpu.HBM)
_SEM = pl.BlockSpec(memory_space=pltpu.SEMAPHORE)
_EFFECT = pltpu.SideEffectType.DATAFLOW_SIDE_EFFECTING


def _remote(src, dst, send_sem, recv_sem, peer):
    return pltpu.make_async_remote_copy(src_ref=src, dst_ref=dst, send_sem=send_sem, recv_sem=recv_sem,
                                        device_id=peer, device_id_type=pl.DeviceIdType.MESH)


def _place_own(src, layer, me, *, out_dtype, name):
    _, r, c = src.shape
    tr = _tile(r, (256, 192, 176, 128, 96, 64, 48, 32, 16))

    def body(me_ref, s_ref, o_ref):
        o_ref[...] = s_ref[...].astype(out_dtype)

    return pl.pallas_call(
        body, name=name,
        grid_spec=pltpu.PrefetchScalarGridSpec(
            num_scalar_prefetch=1, grid=(r // tr,),
            in_specs=[pl.BlockSpec((None, tr, c), lambda i, me_ref: (layer, i, 0))],
            out_specs=pl.BlockSpec((None, tr, c), lambda i, me_ref: (me_ref[0], i, 0))),
        out_shape=jax.ShapeDtypeStruct((N_DEV, r, c), out_dtype),
        compiler_params=_params("parallel"),
    )(me, src)


def _own_blocks(srcs, *, name):
    n = len(srcs)

    def body(*refs):
        ins, outs, sems = refs[:n], refs[n:2 * n], refs[2 * n]
        me = _my_index()
        cps = [pltpu.make_async_copy(ins[t].at[me], outs[t].at[me], sems.at[t]) for t in range(n)]
        for cp in cps:
            cp.start()
        for cp in cps:
            cp.wait()

    return pl.pallas_call(
        body, name=name, in_specs=[_HBM] * n, out_specs=[_HBM] * n,
        out_shape=[jax.ShapeDtypeStruct(s.shape, s.dtype) for s in srcs],
        scratch_shapes=[pltpu.SemaphoreType.DMA((n,))],
    )(*srcs)


def _split_start(groups, *, scatter, name):
    sizes = [len(srcs) for srcs, _ in groups]
    flat_src = [s for srcs, _ in groups for s in srcs]
    flat_land = [l for _, lands in groups for l in lands]
    n, n_g = len(flat_land), len(groups)
    if not scatter:
        flat_src = []
    n_in = len(flat_src) + n

    def body(*refs):
        lands = refs[n_in - n:n_in]
        ins = refs[:n] if scatter else lands
        sems = refs[n_in:n_in + 2 * n_g]
        token = refs[-1]
        me = _my_index()
        t = 0
        for g in range(n_g):
            for q in range(sizes[g]):
                for k in range(1, N_DEV):
                    peer, pidx = _peer(k)
                    src = ins[t].at[pidx] if scatter else ins[t].at[me]
                    slot = q * (N_DEV - 1) + k - 1
                    _remote(src, lands[t].at[me], sems[2 * g].at[slot], sems[2 * g + 1].at[slot], peer).start()
                t += 1
        token[...] = jnp.zeros_like(token)

    sem_shapes = []
    for sz in sizes:
        sem_shapes += [pltpu.SemaphoreType.DMA((sz * (N_DEV - 1),)), pltpu.SemaphoreType.DMA((sz * (N_DEV - 1),))]
    outs = pl.pallas_call(
        body, name=name,
        in_specs=[_HBM] * n_in,
        out_specs=[_SEM] * (2 * n_g) + [_HBM] * n_in + [pl.BlockSpec(memory_space=pltpu.VMEM)],
        out_shape=sem_shapes + [pltpu.HBM(a.shape, a.dtype) for a in flat_src + flat_land]
        + [jax.ShapeDtypeStruct((8, LANES), F32)],
        input_output_aliases={i: 2 * n_g + i for i in range(n_in)},
        compiler_params=pltpu.CompilerParams(has_side_effects=_EFFECT),
    )(*[pltpu.with_memory_space_constraint(a, pltpu.HBM) for a in flat_src + flat_land])
    sems, thru, token = outs[:2 * n_g], outs[2 * n_g:2 * n_g + n_in], outs[-1]
    handles, pos = [], 0
    for g, sz in enumerate(sizes):
        lands_g = thru[n_in - n + pos:n_in - n + pos + sz]
        handles.append((sems[2 * g], sems[2 * g + 1], thru[pos:pos + sz] if scatter else [], lands_g))
        pos += sz
    return handles, token


def _split_wait(handle, after, *, scatter, name):
    send_sems, recv_sems, srcs, lands = handle
    n, n_src = len(lands), len(srcs)

    def body(*refs):
        lnd = refs[n_src:n_src + n]
        ins = refs[:n_src] if scatter else lnd
        ssem, rsem = refs[n_src + n], refs[n_src + n + 1]
        me = _my_index()
        for t in range(n):
            for k in range(1, N_DEV):
                peer, pidx = _peer(k)
                block = ins[t].at[me]
                slot = t * (N_DEV - 1) + k - 1
                _remote(block, lnd[t].at[me], ssem.at[slot], rsem.at[slot], peer).wait_send()
                _remote(block, lnd[t].at[pidx], ssem.at[slot], rsem.at[slot], peer).wait_recv()

    return pl.pallas_call(
        body, name=name,
        in_specs=[_HBM] * (n_src + n) + [_SEM, _SEM, pl.BlockSpec(memory_space=pl.ANY)],
        out_specs=[_HBM] * n,
        out_shape=[pltpu.HBM(l.shape, l.dtype) for l in lands],
        input_output_aliases={n_src + t: t for t in range(n)},
        compiler_params=pltpu.CompilerParams(has_side_effects=_EFFECT),
    )(*srcs, *lands, send_sems, recv_sems, after)


def _pack(arrs, dtype, row_quantum=16):
    flat = jnp.concatenate([a.astype(dtype).reshape(-1) for a in arrs])
    pad = (-flat.shape[0]) % (row_quantum * PACK_COLS)
    if pad:
        flat = jnp.concatenate([flat, jnp.zeros((pad,), dtype)])
    return flat.reshape(-1, PACK_COLS)


def _pack8(arrs, dtype):
    flat = jnp.concatenate([a.astype(dtype).reshape(N_DEV, -1) for a in arrs], axis=1)
    pad = (-flat.shape[1]) % (16 * PACK_COLS)
    if pad:
        flat = jnp.concatenate([flat, jnp.zeros((N_DEV, pad), dtype)], axis=1)
    return flat.reshape(N_DEV, -1, PACK_COLS)


def _unpack(slab, shapes, lead):
    lead_shape = slab.shape[:lead]
    flat = slab.reshape(lead_shape + (-1,))
    outs, off = [], 0
    for shp in shapes:
        size = math.prod(shp)
        outs.append(flat[..., off:off + size].reshape(lead_shape + tuple(shp)))
        off += size
    return outs


def _cols_full(g):
    g = jnp.moveaxis(g, 0, -2)
    return g.reshape(g.shape[:-2] + (g.shape[-2] * g.shape[-1],))


def _cols_split(full):
    n = full.shape[-1] // N_DEV
    return jnp.moveaxis(full.reshape(full.shape[:-1] + (N_DEV, n)), -2, 0)


def _block_diag(w, per):
    n, b, _ = w.shape
    w4 = w.reshape(n // per, per, b, b)
    eye = jnp.eye(per, dtype=w.dtype)
    return jnp.einsum('gpab,pq->gpaqb', w4, eye).reshape(n // per, per * b, per * b)


def _block_diag_extract(g, per):
    gn, cb, _ = g.shape
    b = cb // per
    g5 = g.reshape(gn, per, b, per, b)
    return jnp.stack([g5[:, p, :, p, :] for p in range(per)], axis=1).reshape(gn * per, b, b)


def _slab2d(a):
    if a.size % PACK_COLS == 0:
        return a.reshape(-1, PACK_COLS)
    return a.reshape(-1, a.shape[-1])


def _lru_block_cols(r_dim):
    lru = r_dim // N_LRU_BLOCKS
    return lru * LANES // math.gcd(lru, LANES)


BIG = ("a_w_in", "a_w_out", "b_w_in", "b_w_out", "f_w_in", "f_w_out")
COL_F32 = ("meta", "a_conv_w", "a_conv_b", "a_b_r", "a_b_i", "a_lambda", "f_conv_w")
REPLICATED = ("a_w_r", "a_w_i", "kv_f_b", "f_conv_b", "ln1_g", "ln1_b", "ln2_g", "ln2_b")
WEIGHT_NAMES = ("meta", "a_w_in", "a_conv_w", "a_conv_b", "a_w_r", "a_b_r", "a_w_i", "a_b_i", "a_lambda", "a_w_out",
                "kv_w", "kv_f_b", "b_w_in", "b_w_out", "f_w_in", "f_conv_w", "f_conv_b", "f_w_out",
                "ln1_g", "ln1_b", "ln2_g", "ln2_b")


def _kv_layout(kv_gathered, d):
    kv_full = _cols_full(kv_gathered)
    kv_pad = 2 * d + LANES - kv_full.shape[1]
    return jnp.concatenate([kv_full, jnp.zeros((d, kv_pad), kv_full.dtype)], axis=1)


def _small_layouts(small):
    r_dim = small["a_lambda"].shape[1]
    n_f = small["f_conv_b"].shape[1] // N_DEV
    cb = _lru_block_cols(r_dim)
    per = cb // (r_dim // N_LRU_BLOCKS)
    n_a = small["a_lambda"].shape[0]
    f_conv_w3 = small["f_conv_w"].reshape(N_LAYERS, 3, N_DEV, n_f).transpose(0, 2, 1, 3)
    f_conv_b3 = small["f_conv_b"].reshape(N_LAYERS, N_DEV, 1, n_f)
    return {
        "kv_fb": jnp.concatenate([small["kv_f_b"], jnp.zeros((LANES - N_HEADS,), F32)])[None],
        "a_cwb": jnp.concatenate([small["a_conv_w"], small["a_conv_b"][:, None],
                                  jnp.zeros((n_a, 3, r_dim), F32)], axis=1),
        "a_vecs": jnp.concatenate([jnp.stack([small["a_b_r"], small["a_b_i"], small["a_lambda"]], axis=1),
                                   jnp.zeros((n_a, 5, r_dim), F32)], axis=1),
        "a_bd_r": jnp.stack([_block_diag(small["a_w_r"][l], per) for l in range(n_a)]).astype(BF16),
        "a_bd_i": jnp.stack([_block_diag(small["a_w_i"][l], per) for l in range(n_a)]).astype(BF16),
        "f_cwb3": jnp.concatenate([f_conv_w3, f_conv_b3, jnp.zeros((N_LAYERS, N_DEV, 4, n_f), F32)], axis=2),
        "ln1_g": small["ln1_g"][:, None], "ln1_b": small["ln1_b"][:, None],
        "ln2_g": small["ln2_g"][:, None], "ln2_b": small["ln2_b"][:, None],
    }


def _local_step(h0, tgt, n_meta, n_tok, wts, hooks):
    tp, d = h0.shape
    tm = tp // 8 if (tp // 8) % 16 == 0 else tp
    tmb = _tile(tp, (1088, 512, 320, 256, 128))
    tq = 128
    r_dim = wts["a_vecs"].shape[2]
    cb = wts["a_bd_r"].shape[-1]
    sb = LANES
    n_b = N_LAYERS - N_A_LAYERS

    h, h_bf = h0, h0.astype(BF16)
    saved = []
    kvs = None
    for layer in range(N_LAYERS):
        lw = {}
        sv = {"h_bf": h_bf, "w": lw}
        if layer < N_A_LAYERS:
            lw["in"] = hooks.weight(layer, "in", h)
            sv["gr"] = _proj_in(h_bf, lw["in"], shard_major=False, name="a_in_proj")
            sv["rec"] = _conv_a_fwd(sv["gr"], wts["a_cwb"][layer], cb=cb, name="a_conv_fwd")
            a, u, sv["r"], sv["i"] = _gates_fwd(sv["rec"], wts["a_bd_r"][layer], wts["a_bd_i"][layer],
                                                wts["a_vecs"][layer], tm=tm, name="a_gates_fwd")
            sv["a"] = a
            sv["hr"], y3 = _scan_fwd(a, u, sv["gr"], cb=sb, name="a_scan_fwd")
        else:
            j = layer - N_A_LAYERS
            if j == 0:
                kv_w = _kv_layout(hooks.weight(layer, "kv_w", h), d)
                kvs = {"h_bf": h_bf, "w": kv_w}
                kvs["kv"] = _mm_nn(h_bf, kv_w[:, :2 * d], tn=_tile(2 * d, (512, 256, 128)), out_dtype=BF16,
                                   name="kv_proj")
                kvs["fp"] = _mm_nn(h_bf, kv_w[:, 2 * d:], tn=LANES, out_dtype=F32, name="f_proj")
                kvs["c"], ct = _fgate_fwd(kvs["fp"], wts["kv_fb"], tq=tq, name="fgate_fwd")
                kvs["ct"] = ct[:N_HEADS]
            lw["in"] = hooks.weight(layer, "in", kvs["c"] if j == 0 else h)
            sv["qg"] = _proj_in(h_bf, lw["in"], shard_major=False, name="b_in_proj")
            sv["o"], y3 = _attn_fwd(sv["qg"], kvs["kv"], kvs["c"], kvs["ct"], tq=tq, name="attn_fwd")
        sv["y3"] = y3
        lw["out"] = hooks.weight(layer, "out", y3)
        sv["s1"], h, h_bf = _out_ln(y3, lw["out"], h, wts["ln1_g"][layer], wts["ln1_b"][layer], n_valid=n_tok,
                                    tm=tm, name="mix_out_ln")
        sv["h1_bf"] = h_bf
        lw["f_in"] = hooks.weight(layer, "f_in", h)
        sv["z3"] = _proj_in(h_bf, lw["f_in"], shard_major=True, transposed=True, name="f_in_proj")
        sv["yf3"] = _convglu_fwd(sv["z3"], wts["f_cwb3"][layer], name="f_convglu_fwd")
        lw["f_out"] = hooks.weight(layer, "f_out", sv["yf3"])
        sv["s2"], h, h_bf = _out_ln(sv["yf3"], lw["f_out"], h, wts["ln2_g"][layer], wts["ln2_b"][layer],
                                    n_valid=n_tok, tm=tm, name="ffn_out_ln")
        saved.append(sv)

    loss_tile, dh = _loss_bwd(h, tgt, lo=n_meta, hi=n_tok, tm=tm, name="loss")

    grads = {k: [None] * N_LAYERS for k in ("f_cwb3", "ln1_gb", "ln2_gb")}
    grads.update({k: [None] * N_A_LAYERS for k in ("a_cwb", "a_bd_r", "a_bd_i", "a_vecs")})
    dkv = []
    token = jnp.zeros((), F32)
    for layer in reversed(range(N_LAYERS)):
        sv = saved[layer]
        lw = sv["w"]
        big = {}
        ds, ds_bf, grads["ln2_gb"][layer] = _ln_bwd(dh, sv["s2"], wts["ln2_g"][layer] + token, tm=tm, name="ln_bwd")
        dz, dcw = _ffn_bwd_mid(ds_bf, lw["f_out"], sv["z3"], wts["f_cwb3"][layer], name="f_bwd_mid")
        grads["f_cwb3"][layer] = dcw.reshape((N_DEV,) + dcw.shape[2:])
        dz3 = dz
        big["f_out"] = _w_out_grad(sv["yf3"], ds_bf, lw["f_out"].shape[1], name="f_w_out_grad")
        dh = _in_bwd(dz3, lw["f_in"], ds, tm=tmb, transposed=True, name="f_in_bwd")
        big["f_in"] = _w_in_grad(sv["h1_bf"], dz3, transposed=True, name="f_w_in_grad")
        token = hooks.grads_ready(layer, "ffn", big)
        big = {}
        ds, ds_bf, grads["ln1_gb"][layer] = _ln_bwd(dh, sv["s1"], wts["ln1_g"][layer] + token, tm=tm, name="ln_bwd")
        if layer < N_A_LAYERS:
            dy = _out_bwd(ds_bf, lw["out"], tm=tmb // 2, name="a_out_bwd")
            big["out"] = _w_out_grad(sv["y3"], ds_bf, lw["out"].shape[1], name="a_w_out_grad")
            d_h, d_a, dgate = _scan_bwd(dy, sv["gr"], sv["hr"], sv["a"], cb=sb, name="a_scan_bwd")
            d_rec, dpr, dpi, grads["a_vecs"][layer] = _gates_bwd(
                sv["rec"], sv["r"], sv["i"], sv["a"], d_h, d_a, wts["a_bd_r"][layer], wts["a_bd_i"][layer],
                wts["a_vecs"][layer], tm=tm, name="a_gates_bwd")
            grads["a_bd_r"][layer], grads["a_bd_i"][layer] = _bd_grad(sv["rec"], dpr, dpi, cb=cb, name="a_bd_grad")
            dact, grads["a_cwb"][layer] = _conv_a_bwd(d_rec, sv["gr"], dgate, wts["a_cwb"][layer], cb=cb,
                                                      name="a_conv_bwd")
            dh = _in_bwd(dact, lw["in"], ds, tm=tmb, name="a_in_bwd")
            big["in"] = _w_in_grad(sv["h_bf"], dact, name="a_w_in_grad")
        else:
            j = layer - N_A_LAYERS
            dy = _out_bwd(ds_bf, lw["out"], tm=tmb // 2, name="b_out_bwd")
            big["out"] = _w_out_grad(sv["y3"], ds_bf, lw["out"].shape[1], name="b_w_out_grad")
            dqg, dk, dv, dc = _attn_bwd(dy, sv["qg"], sv["o"], kvs["kv"], kvs["c"], kvs["ct"], tq=tq,
                                        name="attn_bwd")
            dkv.append((dk, dv, dc))
            dh = _in_bwd(dqg, lw["in"], ds, tm=tmb, name="b_in_bwd")
            big["in"] = _w_in_grad(sv["h_bf"], dqg, name="b_w_in_grad")
            if j == 0:
                hpb = LANES // (d // N_HEADS)
                dct = (dkv[0][2] + dkv[1][2])[:, :hpb, :].reshape(N_HEADS, tp)
                dct = jnp.concatenate([dct, jnp.zeros((LANES - N_HEADS, tp), F32)])
                df_bf, grads["kv_fb"] = _fgate_bwd(dct, kvs["fp"], wts["kv_fb"], tq=tq, name="fgate_bwd")
                dkvz = jnp.concatenate([_pair_sum(dkv[0][0], dkv[1][0], tm=tm, name="kv_pair_sum"),
                                        _pair_sum(dkv[0][1], dkv[1][1], tm=tm, name="kv_pair_sum"), df_bf], axis=1)
                dh = _mm_nt_full(dkvz, kvs["w"], dh, tm=tmb // 2, name="kv_in_bwd")
                big["kv_w"] = _mm_tn_cols(kvs["h_bf"], dkvz, tn=LANES, name="kv_w_grad")
        token = hooks.grads_ready(layer, "mix", big)
    return loss_tile, dh, grads


def _finish_small_grads(grads, d_h0, n_meta):
    r_dim = grads["a_vecs"][0].shape[1]
    per = _lru_block_cols(r_dim) // (r_dim // N_LRU_BLOCKS)
    a_cwb = jnp.stack(grads["a_cwb"])
    a_vecs = jnp.stack(grads["a_vecs"])
    f_cwb3 = jnp.stack(grads["f_cwb3"])
    ln1 = jnp.stack(grads["ln1_gb"])
    ln2 = jnp.stack(grads["ln2_gb"])
    f_rows = f_cwb3.transpose(0, 2, 1, 3).reshape(N_LAYERS, 8, -1)
    return {
        "meta": d_h0[:n_meta],
        "a_conv_w": a_cwb[:, :4], "a_conv_b": a_cwb[:, 4],
        "a_w_r": jnp.stack([_block_diag_extract(g, per) for g in grads["a_bd_r"]]),
        "a_b_r": a_vecs[:, 0],
        "a_w_i": jnp.stack([_block_diag_extract(g, per) for g in grads["a_bd_i"]]),
        "a_b_i": a_vecs[:, 1], "a_lambda": a_vecs[:, 2],
        "kv_f_b": grads["kv_fb"][0, :N_HEADS],
        "f_conv_w": f_rows[:, :3], "f_conv_b": f_rows[:, 3],
        "ln1_g": ln1[:, 0], "ln1_b": ln1[:, 1], "ln2_g": ln2[:, 0], "ln2_b": ln2[:, 1],
    }


def kernel(x, meta, a_w_in, a_conv_w, a_conv_b, a_w_r, a_b_r, a_w_i, a_b_i, a_lambda, a_w_out, kv_w, kv_f_b, b_w_in, b_w_out, f_w_in, f_conv_w, f_conv_b, f_w_out, ln1_g, ln1_b, ln2_g, ln2_b, loss_target, m_meta, m_a_w_in, m_a_conv_w, m_a_conv_b, m_a_w_r, m_a_b_r, m_a_w_i, m_a_b_i, m_a_lambda, m_a_w_out, m_kv_w, m_kv_f_b, m_b_w_in, m_b_w_out, m_f_w_in, m_f_conv_w, m_f_conv_b, m_f_w_out, m_ln1_g, m_ln1_b, m_ln2_g, m_ln2_b, v_meta, v_a_w_in, v_a_conv_w, v_a_conv_b, v_a_w_r, v_a_b_r, v_a_w_i, v_a_b_i, v_a_lambda, v_a_w_out, v_kv_w, v_kv_f_b, v_b_w_in, v_b_w_out, v_f_w_in, v_f_conv_w, v_f_conv_b, v_f_w_out, v_ln1_g, v_ln1_b, v_ln2_g, v_ln2_b):
    w = dict(meta=meta, a_w_in=a_w_in, a_conv_w=a_conv_w, a_conv_b=a_conv_b, a_w_r=a_w_r, a_b_r=a_b_r, a_w_i=a_w_i,
             a_b_i=a_b_i, a_lambda=a_lambda, a_w_out=a_w_out, kv_w=kv_w, kv_f_b=kv_f_b, b_w_in=b_w_in,
             b_w_out=b_w_out, f_w_in=f_w_in, f_conv_w=f_conv_w, f_conv_b=f_conv_b, f_w_out=f_w_out, ln1_g=ln1_g,
             ln1_b=ln1_b, ln2_g=ln2_g, ln2_b=ln2_b)
    m = dict(meta=m_meta, a_w_in=m_a_w_in, a_conv_w=m_a_conv_w, a_conv_b=m_a_conv_b, a_w_r=m_a_w_r, a_b_r=m_a_b_r,
             a_w_i=m_a_w_i, a_b_i=m_a_b_i, a_lambda=m_a_lambda, a_w_out=m_a_w_out, kv_w=m_kv_w, kv_f_b=m_kv_f_b,
             b_w_in=m_b_w_in, b_w_out=m_b_w_out, f_w_in=m_f_w_in, f_conv_w=m_f_conv_w, f_conv_b=m_f_conv_b,
             f_w_out=m_f_w_out, ln1_g=m_ln1_g, ln1_b=m_ln1_b, ln2_g=m_ln2_g, ln2_b=m_ln2_b)
    v = dict(meta=v_meta, a_w_in=v_a_w_in, a_conv_w=v_a_conv_w, a_conv_b=v_a_conv_b, a_w_r=v_a_w_r, a_b_r=v_a_b_r,
             a_w_i=v_a_w_i, a_b_i=v_a_b_i, a_lambda=v_a_lambda, a_w_out=v_a_w_out, kv_w=v_kv_w, kv_f_b=v_kv_f_b,
             b_w_in=v_b_w_in, b_w_out=v_b_w_out, f_w_in=v_f_w_in, f_conv_w=v_f_conv_w, f_conv_b=v_f_conv_b,
             f_w_out=v_f_w_out, ln1_g=v_ln1_g, ln1_b=v_ln1_b, ln2_g=v_ln2_g, ln2_b=v_ln2_b)
    shapes = {n: w[n].shape for n in WEIGHT_NAMES}

    me = jnp.reshape(_my_index(), (1,)).astype(jnp.int32)

    def as_stored(name, a):
        return jnp.swapaxes(a, 1, 2) if name == "f_w_in" else a

    param_of = {"in": ("a_w_in", "b_w_in"), "out": ("a_w_out", "b_w_out"), "f_in": ("f_w_in",) * 2,
                "f_out": ("f_w_out",) * 2}
    order = [("small", None, None)]
    for layer in range(N_LAYERS):
        if layer == N_A_LAYERS:
            order.append(("kv_w", layer, 0))
        for key in ("in", "out", "f_in", "f_out"):
            order.append((key, layer, layer if key[0] == "f" or layer < N_A_LAYERS else layer - N_A_LAYERS))
    def place(key, layer, idx):
        if key == "small":
            return _place_own(_pack([w[n] for n in COL_F32], F32)[None], 0, me, out_dtype=F32, name="place_small")
        if key == "kv_w":
            return _place_own(w["kv_w"][None], 0, me, out_dtype=BF16, name="place_kv_w")
        name = param_of[key][0 if layer < N_A_LAYERS else 1]
        return _place_own(as_stored(name, w[name]), idx, me, out_dtype=BF16, name=f"place_{name}_{idx}")

    n_first = 5
    lands = [place(*o) for o in order[:n_first]]
    gather_handles, gather_token = _split_start([([l], [l]) for l in lands], scatter=False, name="gather_start_0")
    lands = [place(*o) for o in order[n_first:]]
    more_handles, _ = _split_start([([l], [l]) for l in lands], scatter=False, name="gather_start_1")
    gather_handles = gather_handles + more_handles
    group_of = {(key, layer): g for g, (key, layer, _) in enumerate(order)}
    (got_s,) = _split_wait(gather_handles[0], gather_token, scatter=False, name="gather_wait_small")
    small = {n: w[n] for n in REPLICATED}
    for n, part in zip(COL_F32, _unpack(got_s, [w[n].shape for n in COL_F32], 1)):
        small[n] = _cols_full(part)
    n_meta, d = small["meta"].shape

    class Hooks:
        pending = None
        received = {}
        sent = {}

        @staticmethod
        def weight(layer, key, after):
            (got,) = _split_wait(gather_handles[group_of[(key, layer)]], after, scatter=False,
                                 name=f"gather_wait_{key}_{layer}")
            return got

        @staticmethod
        def collect(after):
            if Hooks.pending is not None:
                tag, names, handle = Hooks.pending
                got = _split_wait(handle, after, scatter=True, name=f"scatter_wait_{tag}")
                Hooks.received.update(zip(names, got))
                Hooks.pending = None

        @staticmethod
        def grads_ready(layer, part, big):
            if "kv_w" in big:
                big["kv_w"] = _cols_split(big["kv_w"][:, :shapes["kv_w"][1] * N_DEV]).astype(BF16)
            names = [(key, layer) for key in big]
            send = [big[key] for key in big]
            Hooks.collect(send[0])
            empty = [lax.empty(s.shape, s.dtype) for s in send]
            handles, token = _split_start([(send, empty)], scatter=True, name=f"scatter_start_{part}_{layer}")
            Hooks.pending = (f"{part}_{layer}", names, handles[0])
            Hooks.sent.update(zip(names, handles[0][2]))
            return token[0, 0]

    Hooks.pending, Hooks.received, Hooks.sent = None, {}, {}

    n_tok = n_meta + x.shape[1]
    tp = -(-n_tok // ROW_ALIGN) * ROW_ALIGN
    pad = jnp.zeros((tp - n_tok, d), F32)
    h0 = jnp.concatenate([small["meta"], x[0], pad])
    tgt = jnp.concatenate([jnp.zeros((n_meta, d), F32), loss_target[0], pad])
    loss_tile, d_h0, grads = _local_step(h0, tgt, n_meta, n_tok, _small_layouts(small), Hooks)
    g_small = _finish_small_grads(grads, d_h0, n_meta)
    loss = lax.psum(loss_tile[0, 0], MESH_AXES)
    grad_x = d_h0[n_meta:n_tok][None]

    rep = _pack([g_small[n] for n in REPLICATED], F32, row_quantum=16 * N_DEV)
    send = [_pack8([_cols_split(g_small[n]) for n in COL_F32], F32), rep.reshape(N_DEV, -1, PACK_COLS)]
    lands = _own_blocks(send, name="scatter_own_small")
    handles, token = _split_start([(send, lands)], scatter=True, name="scatter_start_small")

    g, delta, new_m, new_v = {}, {}, {}, {}
    layers_of = {
        "a_w_in": [("in", l) for l in range(N_A_LAYERS)], "a_w_out": [("out", l) for l in range(N_A_LAYERS)],
        "b_w_in": [("in", l) for l in range(N_A_LAYERS, N_LAYERS)],
        "b_w_out": [("out", l) for l in range(N_A_LAYERS, N_LAYERS)],
        "f_w_in": [("f_in", l) for l in range(N_LAYERS)], "f_w_out": [("f_out", l) for l in range(N_LAYERS)],
        "kv_w": [("kv_w", N_A_LAYERS)],
    }
    ready = [n for n in BIG + ("kv_w",) if all(t in Hooks.received for t in layers_of[n])]

    def done(names):
        return jnp.stack([g[n][(0,) * g[n].ndim] for n in names])

    for n in ready + [n for n in BIG + ("kv_w",) if n not in ready]:
        if n not in ready and Hooks.pending is not None:
            Hooks.collect(done(ready))
        lift = (lambda a: a[None]) if n == "kv_w" else (lambda a, n=n: as_stored(n, a))
        outs = _sum_adamw([Hooks.received[t] for t in layers_of[n]], [Hooks.sent[t] for t in layers_of[n]], me,
                          lift(w[n]), lift(m[n]), lift(v[n]), name="sum_adamw_" + n)
        g[n], delta[n], new_m[n], new_v[n] = [as_stored(n, o).reshape(shapes[n]) for o in outs]
    recv_s, recv_r = _split_wait(handles[0], done(BIG + ("kv_w",)), scatter=True, name="scatter_wait_small")
    sum_s = _sum8(recv_s, name="sum_grads_f32")
    g.update(zip(COL_F32, _unpack(sum_s, [shapes[n] for n in COL_F32], 0)))
    (got_r,) = _all_gather([_sum8(recv_r, name="sum_grads_replicated")], name="gather_replicated_sums")
    g.update(zip(REPLICATED, _unpack(got_r.reshape(-1, PACK_COLS), [shapes[n] for n in REPLICATED], 0)))

    for n in COL_F32 + REPLICATED:
        shp = shapes[n]
        dl, nm, nv = _adamw(_slab2d(w[n]), _slab2d(g[n]), _slab2d(m[n]), _slab2d(v[n]), name="adamw")
        delta[n], new_m[n], new_v[n] = dl.reshape(shp), nm.reshape(shp), nv.reshape(shp)
    return (loss, grad_x, *[g[n] for n in WEIGHT_NAMES], *[delta[n] for n in WEIGHT_NAMES],
            *[new_m[n] for n in WEIGHT_NAMES], *[new_v[n] for n in WEIGHT_NAMES])
```

```python
import math

import jax
import jax.numpy as jnp
from jax import lax
from jax.experimental import pallas as pl
from jax.experimental.pallas import tpu as pltpu

F32 = jnp.float32
BF16 = jnp.bfloat16

N_DEV = 8
MESH_AXES = ("x", "y", "c")
N_LAYERS = 4
N_A_LAYERS = 2
N_LRU_BLOCKS = 16
N_HEADS = 16
LRU_C = 8.0
DN_ALPHA = (2 * N_LAYERS) ** 0.25
LN_EPS = 1e-5
ADAM_LR, ADAM_B1, ADAM_B2, ADAM_EPS, ADAM_WD, ADAM_STEP = 0.001, 0.9, 0.999, 1e-08, 0.01, 10

LANES = 128
SUBLANES = 8
ROW_ALIGN = 128
VMEM_LIMIT_BYTES = 56 * 1024 * 1024
GELU_K = math.sqrt(2.0 / math.pi)
GELU_C = 0.044715
PACK_COLS = 1024


def _params(*sem):
    return pltpu.CompilerParams(dimension_semantics=sem, vmem_limit_bytes=VMEM_LIMIT_BYTES)


def _gelu(x):
    th = jnp.tanh(GELU_K * (x + GELU_C * x * x * x))
    return 0.5 * x * (1.0 + th)


def _gelu_and_grad(x):
    x2 = x * x
    th = jnp.tanh(GELU_K * (x + GELU_C * x2 * x))
    g = 0.5 * x * (1.0 + th)
    dg = 0.5 * (1.0 + th) + 0.5 * x * (1.0 - th * th) * (GELU_K * (1.0 + 3.0 * GELU_C * x2))
    return g, dg


def _sigmoid(x):
    return 1.0 / (1.0 + jnp.exp(-x))


def _expm1(x):
    small = x * (1.0 + 0.5 * x * (1.0 + (1.0 / 3.0) * x * (1.0 + 0.25 * x)))
    return jnp.where(jnp.abs(x) < 1e-2, small, jnp.exp(x) - 1.0)


def _softplus(x):
    e = jnp.exp(-jnp.abs(x))
    small = e * (1.0 - 0.5 * e * (1.0 - (2.0 / 3.0) * e))
    return jnp.maximum(x, 0.0) + jnp.where(e < 1e-2, small, jnp.log(1.0 + e))


def _shift_down(x, s):
    if s == 0:
        return x
    rows = lax.broadcasted_iota(jnp.int32, x.shape, 0)
    return jnp.where(rows >= s, pltpu.roll(x, s, 0), 0.0)


def _shift_up(x, s):
    if s == 0:
        return x
    n = x.shape[0]
    rows = lax.broadcasted_iota(jnp.int32, x.shape, 0)
    return jnp.where(rows < n - s, pltpu.roll(x, n - s, 0), 0.0)


def _dot_nn(a, b):
    return lax.dot_general(a, b, (((1,), (0,)), ((), ())), preferred_element_type=F32)


def _dot_nt(a, b):
    return lax.dot_general(a, b, (((1,), (1,)), ((), ())), preferred_element_type=F32)


def _dot_tn(a, b):
    return lax.dot_general(a, b, (((0,), (0,)), ((), ())), preferred_element_type=F32)


def _rows8(vals, width):
    rows = lax.broadcasted_iota(jnp.int32, (8, width), 0)
    out = jnp.zeros((8, width), F32)
    for k, v in enumerate(vals):
        out = jnp.where(rows == k, jnp.broadcast_to(v, (8, width)), out)
    return out


def _tile(n, prefer):
    for c in prefer:
        if n % c == 0:
            return c
    return n


def _mm_nn(a, b, *, tn, out_dtype, name):
    m, k = a.shape
    n = b.shape[1]

    def body(a_ref, b_ref, o_ref):
        o_ref[...] = _dot_nn(a_ref[...], b_ref[...]).astype(o_ref.dtype)

    return pl.pallas_call(
        body, name=name, grid=(n // tn,),
        in_specs=[pl.BlockSpec((m, k), lambda j: (0, 0)), pl.BlockSpec((k, tn), lambda j: (0, j))],
        out_specs=pl.BlockSpec((m, tn), lambda j: (0, j)),
        out_shape=jax.ShapeDtypeStruct((m, n), out_dtype),
        compiler_params=_params("parallel"),
    )(a, b)


def _proj_in(h_bf, g_in, *, shard_major, name, transposed=False):
    t, k = h_bf.shape
    n = g_in.shape[1] if transposed else g_in.shape[2]

    def body(a_ref, b_ref, o_ref):
        o_ref[...] = _dot_nt(a_ref[...], b_ref[...]) if transposed else _dot_nn(a_ref[...], b_ref[...])

    if shard_major:
        out_spec = pl.BlockSpec((None, t, n), lambda j: (j, 0, 0))
        out_shape = jax.ShapeDtypeStruct((N_DEV, t, n), F32)
    else:
        out_spec = pl.BlockSpec((t, n), lambda j: (0, j))
        out_shape = jax.ShapeDtypeStruct((t, N_DEV * n), F32)
    return pl.pallas_call(
        body, name=name, grid=(N_DEV,),
        in_specs=[pl.BlockSpec((t, k), lambda j: (0, 0)),
                  pl.BlockSpec((None,) + g_in.shape[1:], lambda j: (j, 0, 0))],
        out_specs=out_spec, out_shape=out_shape,
        compiler_params=_params("parallel"),
    )(h_bf, g_in)


def _out_ln(y3, g_out, hin, g, b, *, n_valid, tm, name):
    nj, t, kj = y3.shape
    _, r, d = g_out.shape

    def body(y_ref, w_ref, hin_ref, g_ref, b_ref, s_ref, h_ref, hb_ref):
        w = w_ref[...].reshape(N_DEV * r, d)
        s = DN_ALPHA * hin_ref[...]
        for jj in range(nj):
            s = s + _dot_nn(y_ref[jj], w[jj * kj:(jj + 1) * kj])
        mu = jnp.mean(s, axis=-1, keepdims=True)
        xc = s - mu
        var = jnp.mean(xc * xc, axis=-1, keepdims=True)
        h = xc * lax.rsqrt(var + LN_EPS) * g_ref[...] + b_ref[...]
        s_ref[...] = s
        h_ref[...] = h
        rows = pl.program_id(0) * tm + lax.broadcasted_iota(jnp.int32, (tm, d), 0)
        hb_ref[...] = jnp.where(rows < n_valid, h, 0.0).astype(BF16)

    row = pl.BlockSpec((tm, d), lambda i: (i, 0))
    vec = pl.BlockSpec((1, d), lambda i: (0, 0))
    return pl.pallas_call(
        body, name=name, grid=(t // tm,),
        in_specs=[pl.BlockSpec((nj, tm, kj), lambda i: (0, i, 0)),
                  pl.BlockSpec((N_DEV, r, d), lambda i: (0, 0, 0)), row, vec, vec],
        out_specs=[row, row, row],
        out_shape=[jax.ShapeDtypeStruct((t, d), F32), jax.ShapeDtypeStruct((t, d), F32),
                   jax.ShapeDtypeStruct((t, d), BF16)],
        compiler_params=_params("parallel"),
    )(y3, g_out, hin, g, b)


def _out_bwd(ds_bf, g_out, *, tm, name):
    t, d = ds_bf.shape
    r = g_out.shape[1]

    def body(a_ref, w_ref, o_ref):
        o_ref[...] = _dot_nt(a_ref[...], w_ref[...].reshape(N_DEV * r, d))

    return pl.pallas_call(
        body, name=name, grid=(t // tm,),
        in_specs=[pl.BlockSpec((tm, d), lambda i: (i, 0)),
                  pl.BlockSpec((N_DEV, r, d), lambda i: (0, 0, 0))],
        out_specs=pl.BlockSpec((tm, N_DEV * r), lambda i: (i, 0)),
        out_shape=jax.ShapeDtypeStruct((t, N_DEV * r), F32),
        compiler_params=_params("parallel"),
    )(ds_bf, g_out)


def _in_bwd(dact, g_in, add, *, tm, name, alpha=DN_ALPHA, transposed=False):
    t = dact.shape[-2]
    _, k, n = g_in.shape
    if transposed:
        k, n = n, k
    halves = dact.shape[0] == 2 and dact.ndim == 3
    per = N_DEV // 2

    def body(a_ref, b_ref, add_ref, o_ref, acc_ref):
        j = pl.program_id(1)

        @pl.when(j == 0)
        def _():
            acc_ref[...] = alpha * add_ref[...]

        acc_ref[...] += _dot_nn(a_ref[...], b_ref[...]) if transposed else _dot_nt(a_ref[...], b_ref[...])

        @pl.when(j == N_DEV - 1)
        def _():
            o_ref[...] = acc_ref[...]

    if halves:
        a_spec = pl.BlockSpec((None, tm, n), lambda i, j: (j // per, i, j % per))
    elif dact.ndim == 4:
        a_spec = pl.BlockSpec((None, None, tm, n), lambda i, j: (j // per, j % per, i, 0))
    else:
        a_spec = pl.BlockSpec((None, tm, n), lambda i, j: (j, i, 0))
    return pl.pallas_call(
        body, name=name, grid=(t // tm, N_DEV),
        in_specs=[a_spec, pl.BlockSpec((None,) + g_in.shape[1:], lambda i, j: (j, 0, 0)),
                  pl.BlockSpec((tm, k), lambda i, j: (i, 0))],
        out_specs=pl.BlockSpec((tm, k), lambda i, j: (i, 0)),
        out_shape=jax.ShapeDtypeStruct((t, k), F32),
        scratch_shapes=[pltpu.VMEM((tm, k), F32)],
        compiler_params=_params("parallel", "arbitrary"),
    )(dact, g_in, add)


def _mm_nt_full(a, b, add, *, tm, name):
    t, n = a.shape
    k = b.shape[0]

    def body(a_ref, b_ref, add_ref, o_ref):
        o_ref[...] = add_ref[...] + _dot_nt(a_ref[...], b_ref[...])

    return pl.pallas_call(
        body, name=name, grid=(t // tm,),
        in_specs=[pl.BlockSpec((tm, n), lambda i: (i, 0)), pl.BlockSpec((k, n), lambda i: (0, 0)),
                  pl.BlockSpec((tm, k), lambda i: (i, 0))],
        out_specs=pl.BlockSpec((tm, k), lambda i: (i, 0)),
        out_shape=jax.ShapeDtypeStruct((t, k), F32),
        compiler_params=_params("parallel"),
    )(a, b, add)


def _w_in_grad(h_bf, dact, *, name, transposed=False):
    t, k = h_bf.shape
    halves = dact.shape[0] == 2 and dact.ndim == 3
    per = N_DEV // 2
    n = dact.shape[-1] // per if halves else dact.shape[-1]

    def body(a_ref, b_ref, o_ref):
        if transposed:
            o_ref[...] = _dot_tn(b_ref[...], a_ref[...]).astype(BF16)
        else:
            o_ref[...] = _dot_tn(a_ref[...], b_ref[...]).astype(BF16)

    if halves:
        b_spec = pl.BlockSpec((None, t, n), lambda j: (j // per, 0, j % per))
    elif dact.ndim == 4:
        b_spec = pl.BlockSpec((None, None, t, n), lambda j: (j // per, j % per, 0, 0))
    else:
        b_spec = pl.BlockSpec((None, t, n), lambda j: (j, 0, 0))
    return pl.pallas_call(
        body, name=name, grid=(N_DEV,),
        in_specs=[pl.BlockSpec((t, k), lambda j: (0, 0)), b_spec],
        out_specs=pl.BlockSpec((None, n, k) if transposed else (None, k, n), lambda j: (j, 0, 0)),
        out_shape=jax.ShapeDtypeStruct((N_DEV, n, k) if transposed else (N_DEV, k, n), BF16),
        compiler_params=_params("parallel"),
    )(h_bf, dact)


def _w_out_grad(y3, ds_bf, r, *, name):
    nj, t, kj = y3.shape
    d = ds_bf.shape[1]
    unit = r * LANES // math.gcd(r, LANES)
    ks = max([c for c in range(unit, min(kj, 768) + 1, unit) if kj % c == 0], default=kj)
    gsz = ks // r
    per = kj // ks

    def body(a_ref, b_ref, o_ref):
        o_ref[...] = _dot_tn(a_ref[...], b_ref[...]).reshape(gsz, r, d).astype(BF16)

    return pl.pallas_call(
        body, name=name, grid=(nj * per,),
        in_specs=[pl.BlockSpec((None, t, ks), lambda j: (j // per, 0, j % per)),
                  pl.BlockSpec((t, d), lambda j: (0, 0))],
        out_specs=pl.BlockSpec((gsz, r, d), lambda j: (j, 0, 0)),
        out_shape=jax.ShapeDtypeStruct((N_DEV, r, d), BF16),
        compiler_params=_params("parallel"),
    )(y3, ds_bf)


def _mm_tn_cols(a, b, *, tn, name):
    t, m = a.shape
    n = b.shape[1]

    def body(a_ref, b_ref, o_ref):
        o_ref[...] = _dot_tn(a_ref[...], b_ref[...])

    return pl.pallas_call(
        body, name=name, grid=(n // tn,),
        in_specs=[pl.BlockSpec((t, m), lambda j: (0, 0)), pl.BlockSpec((t, tn), lambda j: (0, j))],
        out_specs=pl.BlockSpec((m, tn), lambda j: (0, j)),
        out_shape=jax.ShapeDtypeStruct((m, n), F32),
        compiler_params=_params("parallel"),
    )(a, b)


def _ln_bwd(dout, s, g, *, tm, name):
    t, d = s.shape

    def body(do_ref, s_ref, g_ref, ds_ref, dsb_ref, gb_ref):
        i = pl.program_id(0)
        sv = s_ref[...]
        do = do_ref[...]
        mu = jnp.mean(sv, axis=-1, keepdims=True)
        xc = sv - mu
        var = jnp.mean(xc * xc, axis=-1, keepdims=True)
        rstd = lax.rsqrt(var + LN_EPS)
        xhat = xc * rstd
        dxhat = do * g_ref[...]
        m1 = jnp.mean(dxhat, axis=-1, keepdims=True)
        m2 = jnp.mean(dxhat * xhat, axis=-1, keepdims=True)
        ds = rstd * (dxhat - m1 - xhat * m2)
        ds_ref[...] = ds
        dsb_ref[...] = ds.astype(BF16)
        upd = _rows8([jnp.sum(do * xhat, axis=0, keepdims=True), jnp.sum(do, axis=0, keepdims=True)], d)

        @pl.when(i == 0)
        def _():
            gb_ref[...] = upd

        @pl.when(i > 0)
        def _():
            gb_ref[...] += upd

    row = pl.BlockSpec((tm, d), lambda i: (i, 0))
    return pl.pallas_call(
        body, name=name, grid=(t // tm,),
        in_specs=[row, row, pl.BlockSpec((1, d), lambda i: (0, 0))],
        out_specs=[row, row, pl.BlockSpec((8, d), lambda i: (0, 0))],
        out_shape=[jax.ShapeDtypeStruct((t, d), F32), jax.ShapeDtypeStruct((t, d), BF16),
                   jax.ShapeDtypeStruct((8, d), F32)],
        compiler_params=_params("arbitrary"),
    )(dout, s, g)


def _roll_down(x, s):
    return x if s == 0 else pltpu.roll(x, s, 0)


def _conv_taps(x, wb, width):
    y = jnp.broadcast_to(wb[width:width + 1, :], x.shape)
    for k in range(width):
        y = y + _roll_down(x, width - 1 - k) * wb[k:k + 1, :]
    return y


def _conv_taps_bwd(dy, x, wb, width):
    n = dy.shape[0]
    dx = jnp.zeros_like(dy)
    rows = []
    for k in range(width):
        s = width - 1 - k
        dy_up = dy if s == 0 else pltpu.roll(dy, n - s, 0)
        dx = dx + dy_up * wb[k:k + 1, :]
        rows.append(jnp.sum(dy_up * x, axis=0, keepdims=True))
    rows.append(jnp.sum(dy, axis=0, keepdims=True))
    t_idx = lax.broadcasted_iota(jnp.int32, dy.shape, 0)
    return jnp.where(t_idx < n - (width - 1), dx, 0.0), _rows8(rows, dy.shape[1])


def _convglu_fwd(z3, fwb3, *, name):
    _, t, n = z3.shape
    half = N_DEV // 2
    nc = pl.cdiv(n, LANES)

    def body(zg_ref, zv_ref, wg_ref, wv_ref, y_ref):
        gate = _conv_taps(zg_ref[...], wg_ref[...], 3)
        val = _conv_taps(zv_ref[...], wv_ref[...], 3)
        y_ref[...] = (_gelu(gate) * val).astype(BF16)

    zblk = lambda off: pl.BlockSpec((None, t, LANES), lambda j, c: (j + off, 0, c))
    wblk = lambda off: pl.BlockSpec((None, 8, LANES), lambda j, c: (j + off, 0, c))
    return pl.pallas_call(
        body, name=name, grid=(half, nc),
        in_specs=[zblk(0), zblk(half), wblk(0), wblk(half)],
        out_specs=zblk(0),
        out_shape=jax.ShapeDtypeStruct((half, t, n), BF16),
        compiler_params=_params("parallel", "parallel"),
    )(z3, z3, fwb3, fwb3)


def _ffn_bwd_mid(ds_bf, g_out, z3, fwb3, *, name):
    t, d = ds_bf.shape
    r = g_out.shape[1]
    n = z3.shape[2]
    half = N_DEV // 2
    nc = pl.cdiv(n, LANES)
    assert n == 2 * r

    def body(ds_ref, w_ref, zg_ref, zv_ref, wg_ref, wv_ref, dz_ref, dwb_ref, wsc_ref):
        c = pl.program_id(1)

        @pl.when(c == 0)
        def _():
            wsc_ref[0:r, :] = w_ref[0]
            wsc_ref[r:2 * r, :] = w_ref[1]
            if nc * LANES > n:
                wsc_ref[n:nc * LANES, :] = jnp.zeros((nc * LANES - n, d), BF16)

        w = wsc_ref[pl.ds(pl.multiple_of(c * LANES, LANES), LANES), :]
        dyf = _dot_nt(ds_ref[...], w)
        zg, zv = zg_ref[...], zv_ref[...]
        wg, wv = wg_ref[...], wv_ref[...]
        gate = _conv_taps(zg, wg, 3)
        val = _conv_taps(zv, wv, 3)
        gl, dgl = _gelu_and_grad(gate)
        dzg, dwg = _conv_taps_bwd(dyf * val * dgl, zg, wg, 3)
        dzv, dwv = _conv_taps_bwd(dyf * gl, zv, wv, 3)
        dz_ref[0] = dzg.astype(BF16)
        dz_ref[1] = dzv.astype(BF16)
        dwb_ref[0] = dwg
        dwb_ref[1] = dwv

    zblk = lambda off: pl.BlockSpec((None, t, LANES), lambda j, c: (j + off, 0, c))
    wblk = lambda off: pl.BlockSpec((None, 8, LANES), lambda j, c: (j + off, 0, c))
    return pl.pallas_call(
        body, name=name, grid=(half, nc),
        in_specs=[pl.BlockSpec((t, d), lambda j, c: (0, 0)),
                  pl.BlockSpec((2, r, d), lambda j, c: (j, 0, 0)),
                  zblk(0), zblk(half), wblk(0), wblk(half)],
        out_specs=[pl.BlockSpec((2, None, t, LANES), lambda j, c: (0, j, 0, c)),
                   pl.BlockSpec((2, None, 8, LANES), lambda j, c: (0, j, 0, c))],
        out_shape=[jax.ShapeDtypeStruct((2, half, t, n), BF16), jax.ShapeDtypeStruct((2, half, 8, n), F32)],
        scratch_shapes=[pltpu.VMEM((nc * LANES, d), BF16)],
        compiler_params=_params("parallel", "arbitrary"),
    )(ds_bf, g_out, z3, z3, fwb3, fwb3)


def _conv_a_fwd(gr, cwb, *, cb, name):
    t, r2 = gr.shape
    r = r2 // 2
    nb = r // cb

    def body(x_ref, w_ref, o_ref):
        o_ref[...] = _conv_taps(x_ref[...], w_ref[...], 4)

    return pl.pallas_call(
        body, name=name, grid=(nb,),
        in_specs=[pl.BlockSpec((t, cb), lambda j: (0, j + nb)), pl.BlockSpec((8, cb), lambda j: (0, j))],
        out_specs=pl.BlockSpec((t, cb), lambda j: (0, j)),
        out_shape=jax.ShapeDtypeStruct((t, r), F32),
        compiler_params=_params("parallel"),
    )(gr, cwb)


def _gates_fwd(rec, bd_r, bd_i, vecs, *, tm, name):
    t, r_dim = rec.shape
    nb, cb, _ = bd_r.shape

    def body(x_ref, wr_ref, wi_ref, v_ref, a_ref, u_ref, r_ref, i_ref):
        x = x_ref[...]
        xb = x.astype(BF16)
        v = v_ref[...]
        r = _sigmoid(_dot_nn(xb, wr_ref[...]) + v[0:1, :])
        i = _sigmoid(_dot_nn(xb, wi_ref[...]) + v[1:2, :])
        log_a = (-LRU_C) * r * _softplus(-v[2:3, :])
        a_ref[...] = jnp.exp(log_a)
        u_ref[...] = jnp.sqrt(-_expm1(2.0 * log_a)) * (i * x)
        r_ref[...] = r
        i_ref[...] = i

    blk = pl.BlockSpec((tm, cb), lambda j, i: (i, j))
    wspec = pl.BlockSpec((None, cb, cb), lambda j, i: (j, 0, 0))
    out = jax.ShapeDtypeStruct((t, r_dim), F32)
    return pl.pallas_call(
        body, name=name, grid=(nb, t // tm),
        in_specs=[blk, wspec, wspec, pl.BlockSpec((8, cb), lambda j, i: (0, j))],
        out_specs=[blk, blk, blk, blk],
        out_shape=[out, out, out, out],
        compiler_params=_params("parallel", "parallel"),
    )(rec, bd_r, bd_i, vecs)


def _scan_fwd(a, u, gr, *, cb, name):
    t, r = a.shape
    nb = r // cb
    seg = t // SUBLANES

    def body(a_ref, u_ref, g_ref, h_ref, y_ref, p_ref):
        def step(k, carry):
            h, p = carry
            rows = pl.ds(k, SUBLANES, stride=seg)
            av = a_ref[rows, :]
            h = av * h + u_ref[rows, :]
            p = av * p
            h_ref[rows, :] = h
            p_ref[rows, :] = p
            return h, p

        h_fin, p_fin = lax.fori_loop(0, seg, step, (jnp.zeros((SUBLANES, cb), F32), jnp.ones((SUBLANES, cb), F32)),
                                     unroll=4)
        carry = h_fin[0:1, :]
        for s in range(1, SUBLANES):
            rows = slice(s * seg, (s + 1) * seg)
            h_ref[rows, :] = h_ref[rows, :] + p_ref[rows, :] * carry
            carry = h_fin[s:s + 1, :] + p_fin[s:s + 1, :] * carry
        y_ref[...] = (_gelu(g_ref[...]) * h_ref[...]).astype(BF16)

    blk = pl.BlockSpec((t, cb), lambda j: (0, j))
    return pl.pallas_call(
        body, name=name, grid=(nb,),
        in_specs=[blk, blk, blk],
        out_specs=[blk, pl.BlockSpec((None, t, cb), lambda j: (0, 0, j))],
        out_shape=[jax.ShapeDtypeStruct((t, r), F32), jax.ShapeDtypeStruct((1, t, r), BF16)],
        scratch_shapes=[pltpu.VMEM((t, cb), F32)],
        compiler_params=_params("parallel"),
    )(a, u, gr)


def _scan_bwd(dy, gr, hr, a, *, cb, name):
    t, r = a.shape
    nb = r // cb
    seg = t // SUBLANES

    def body(dy_ref, g_ref, h_ref, a_ref, dh_ref, da_ref, dg_ref, q_ref):
        gl, dgl = _gelu_and_grad(g_ref[...])
        dyv = dy_ref[...]
        dh_ref[...] = dyv * gl
        dg_ref[...] = (dyv * h_ref[...] * dgl).astype(BF16)

        def step(k, carry):
            cin, q = carry
            rows = pl.ds(seg - 1 - k, SUBLANES, stride=seg)
            dh = dh_ref[rows, :] + cin
            dh_ref[rows, :] = dh
            q_ref[rows, :] = q
            av = a_ref[rows, :]
            return av * dh, av * q

        c_fin, q_fin = lax.fori_loop(0, seg, step, (jnp.zeros((SUBLANES, cb), F32), jnp.ones((SUBLANES, cb), F32)),
                                     unroll=4)
        carry = c_fin[SUBLANES - 1:SUBLANES, :]
        for s in range(SUBLANES - 2, -1, -1):
            rows = slice(s * seg, (s + 1) * seg)
            dh_ref[rows, :] = dh_ref[rows, :] + q_ref[rows, :] * carry
            carry = c_fin[s:s + 1, :] + q_fin[s:s + 1, :] * carry
        da_ref[...] = dh_ref[...] * _shift_down(h_ref[...], 1)

    blk = pl.BlockSpec((t, cb), lambda j: (0, j))
    return pl.pallas_call(
        body, name=name, grid=(nb,),
        in_specs=[blk, blk, blk, blk],
        out_specs=[blk, blk, blk],
        out_shape=[jax.ShapeDtypeStruct((t, r), F32), jax.ShapeDtypeStruct((t, r), F32),
                   jax.ShapeDtypeStruct((t, r), BF16)],
        scratch_shapes=[pltpu.VMEM((t, cb), F32)],
        compiler_params=_params("parallel"),
    )(dy, gr, hr, a)


def _gates_bwd(rec, r, i, a, dh, da, bd_r, bd_i, vecs, *, tm, name):
    t, r_dim = rec.shape
    nb, cb, _ = bd_r.shape

    def body(x_ref, r_ref, i_ref, a_ref, dh_ref, da_ref, wr_ref, wi_ref, v_ref, dx_ref, dpr_ref, dpi_ref, dv_ref):
        step = pl.program_id(1)
        x, r, i, a, dh, da = x_ref[...], r_ref[...], i_ref[...], a_ref[...], dh_ref[...], da_ref[...]
        lam = v_ref[...][2:3, :]
        sp = _softplus(-lam)
        a2 = a * a
        mult = jnp.sqrt(-_expm1(2.0 * (-LRU_C) * r * sp))
        d_i = dh * mult * x
        d_log_a = da * a - (dh * i * x) * a2 / mult
        d_r = d_log_a * ((-LRU_C) * sp)
        d_sp = jnp.sum(d_log_a * ((-LRU_C) * r), axis=0, keepdims=True)
        d_pre_r = d_r * r * (1.0 - r)
        d_pre_i = d_i * i * (1.0 - i)
        dprb = d_pre_r.astype(BF16)
        dpib = d_pre_i.astype(BF16)
        dx_ref[...] = dh * mult * i + _dot_nt(dprb, wr_ref[...]) + _dot_nt(dpib, wi_ref[...])
        dpr_ref[...] = dprb
        dpi_ref[...] = dpib
        upd = _rows8([jnp.sum(d_pre_r, axis=0, keepdims=True), jnp.sum(d_pre_i, axis=0, keepdims=True),
                      -d_sp * _sigmoid(-lam)], cb)

        @pl.when(step == 0)
        def _():
            dv_ref[...] = upd

        @pl.when(step > 0)
        def _():
            dv_ref[...] += upd

    blk = pl.BlockSpec((tm, cb), lambda j, i: (i, j))
    wspec = pl.BlockSpec((None, cb, cb), lambda j, i: (j, 0, 0))
    vspec = pl.BlockSpec((8, cb), lambda j, i: (0, j))
    return pl.pallas_call(
        body, name=name, grid=(nb, t // tm),
        in_specs=[blk] * 6 + [wspec, wspec, vspec],
        out_specs=[blk, blk, blk, vspec],
        out_shape=[jax.ShapeDtypeStruct((t, r_dim), F32), jax.ShapeDtypeStruct((t, r_dim), BF16),
                   jax.ShapeDtypeStruct((t, r_dim), BF16), jax.ShapeDtypeStruct((8, r_dim), F32)],
        compiler_params=_params("parallel", "arbitrary"),
    )(rec, r, i, a, dh, da, bd_r, bd_i, vecs)


def _bd_grad(rec, dpr, dpi, *, cb, name):
    t, r = rec.shape
    nb = r // cb

    def body(x_ref, dr_ref, di_ref, gr_ref, gi_ref):
        xb = x_ref[...].astype(BF16)
        gr_ref[...] = _dot_tn(xb, dr_ref[...])
        gi_ref[...] = _dot_tn(xb, di_ref[...])

    blk = pl.BlockSpec((t, cb), lambda j: (0, j))
    wspec = pl.BlockSpec((None, cb, cb), lambda j: (j, 0, 0))
    out = jax.ShapeDtypeStruct((nb, cb, cb), F32)
    return pl.pallas_call(
        body, name=name, grid=(nb,),
        in_specs=[blk, blk, blk], out_specs=[wspec, wspec], out_shape=[out, out],
        compiler_params=_params("parallel"),
    )(rec, dpr, dpi)


def _conv_a_bwd(d_rec, gr, dgate, cwb, *, cb, name):
    t, r = d_rec.shape
    nb = r // cb

    def body(dy_ref, x_ref, dg_ref, w_ref, dact_ref, dw_ref):
        dx, dw = _conv_taps_bwd(dy_ref[...], x_ref[...], w_ref[...], 4)
        dact_ref[0] = dg_ref[...]
        dact_ref[1] = dx.astype(BF16)
        dw_ref[...] = dw

    blk = pl.BlockSpec((t, cb), lambda j: (0, j))
    vspec = pl.BlockSpec((8, cb), lambda j: (0, j))
    return pl.pallas_call(
        body, name=name, grid=(nb,),
        in_specs=[blk, pl.BlockSpec((t, cb), lambda j: (0, j + nb)), blk, vspec],
        out_specs=[pl.BlockSpec((2, t, cb), lambda j: (0, 0, j)), vspec],
        out_shape=[jax.ShapeDtypeStruct((2, t, r), BF16), jax.ShapeDtypeStruct((8, r), F32)],
        compiler_params=_params("parallel"),
    )(d_rec, gr, dgate, cwb)


def _split3(x):
    p0 = x.astype(BF16)
    r1 = x - p0.astype(F32)
    p1 = r1.astype(BF16)
    p2 = (r1 - p1.astype(F32)).astype(BF16)
    return p0, p1, p2


def _fgate_fwd(fp, fb, *, tq, name):
    t = fp.shape[0]

    def body(f_ref, b_ref, c_ref, ct_ref):
        logf = -_softplus(-(f_ref[...] + b_ref[...]))
        rows = pl.program_id(0) * tq + lax.broadcasted_iota(jnp.int32, (tq, t), 0)
        cols = lax.broadcasted_iota(jnp.int32, (tq, t), 1)
        tri = (cols <= rows).astype(BF16)
        p0, p1, p2 = _split3(logf)
        c = _dot_nn(tri, p0) + _dot_nn(tri, p1) + _dot_nn(tri, p2)
        c_ref[...] = c
        ct_ref[...] = c.T

    return pl.pallas_call(
        body, name=name, grid=(t // tq,),
        in_specs=[pl.BlockSpec((t, LANES), lambda i: (0, 0)), pl.BlockSpec((1, LANES), lambda i: (0, 0))],
        out_specs=[pl.BlockSpec((tq, LANES), lambda i: (i, 0)), pl.BlockSpec((LANES, tq), lambda i: (0, i))],
        out_shape=[jax.ShapeDtypeStruct((t, LANES), F32), jax.ShapeDtypeStruct((LANES, t), F32)],
        compiler_params=_params("parallel"),
    )(fp, fb)


def _fgate_bwd(dct, fp, fb, *, tq, name):
    t = fp.shape[0]

    def body(d_ref, f_ref, b_ref, o_ref, db_ref):
        i = pl.program_id(0)
        rows = lax.broadcasted_iota(jnp.int32, (t, tq), 0)
        cols = i * tq + lax.broadcasted_iota(jnp.int32, (t, tq), 1)
        tri = (rows >= cols).astype(BF16)
        p0, p1, p2 = _split3(d_ref[...])
        dlogf = (_dot_nn(p0, tri) + _dot_nn(p1, tri) + _dot_nn(p2, tri)).T
        df = dlogf * _sigmoid(-(f_ref[...] + b_ref[...]))
        o_ref[...] = df.astype(BF16)
        upd = _rows8([jnp.sum(df, axis=0, keepdims=True)], LANES)

        @pl.when(i == 0)
        def _():
            db_ref[...] = upd

        @pl.when(i > 0)
        def _():
            db_ref[...] += upd

    return pl.pallas_call(
        body, name=name, grid=(t // tq,),
        in_specs=[pl.BlockSpec((LANES, t), lambda i: (0, 0)), pl.BlockSpec((tq, LANES), lambda i: (i, 0)),
                  pl.BlockSpec((1, LANES), lambda i: (0, 0))],
        out_specs=[pl.BlockSpec((tq, LANES), lambda i: (i, 0)), pl.BlockSpec((8, LANES), lambda i: (0, 0))],
        out_shape=[jax.ShapeDtypeStruct((t, LANES), BF16), jax.ShapeDtypeStruct((8, LANES), F32)],
        compiler_params=_params("arbitrary"),
    )(dct, fp, fb)


def _pair_sum(a, b, *, tm, name):
    t, d = a.shape

    def body(a_ref, b_ref, o_ref):
        o_ref[...] = (a_ref[...] + b_ref[...]).astype(BF16)

    row = pl.BlockSpec((tm, d), lambda i: (i, 0))
    return pl.pallas_call(
        body, name=name, grid=(t // tm,), in_specs=[row, row], out_specs=row,
        out_shape=jax.ShapeDtypeStruct((t, d), BF16), compiler_params=_params("parallel"),
    )(a, b)


def _head_block_width(dh):
    return 2 * LANES if 2 * LANES // dh <= 8 else LANES


def _head_masks(dh, bw):
    lane = lax.broadcasted_iota(jnp.int32, (1, bw), 1)
    return [((lane >= e * dh) & (lane < (e + 1) * dh)) for e in range(bw // dh)]


def _head_c_row(ct_blk, head):
    sub = lax.broadcasted_iota(jnp.int32, ct_blk.shape, 0)
    return jnp.sum(jnp.where(sub == head, ct_blk, 0.0), axis=0, keepdims=True)


def _attn_weights(qm, k, c_row, q0):
    tq, t = qm.shape[0], k.shape[0]
    s = _dot_nt(qm, k) - c_row
    qi = q0 + lax.broadcasted_iota(jnp.int32, (tq, t), 0)
    ki = lax.broadcasted_iota(jnp.int32, (tq, t), 1)
    s = jnp.where(ki <= qi, s, -jnp.inf)
    e = jnp.exp(s - jnp.max(s, axis=-1, keepdims=True))
    return e, 1.0 / jnp.sum(e, axis=-1, keepdims=True)


def _key_buckets(t, tq):
    step = 3 * tq
    return tuple(range(step, t, step)) + (t,)


def _for_prefix(needed, buckets, fn):
    prev = 0
    for length in buckets:
        pl.when((needed > prev) & (needed <= length))(lambda length=length: fn(length))
        prev = length


def _attn_fwd(qg, kv, ct, *, tq, name):
    t, d2 = qg.shape
    d = d2 // 2
    dh = d // N_HEADS
    bw = _head_block_width(dh)
    hpb = bw // dh
    nhb = d // bw
    scale = dh ** -0.5
    buckets = _key_buckets(t, tq)

    def body(q_ref, og_ref, k_ref, v_ref, ct_ref, o_ref, y_ref):
        hb = pl.program_id(0)
        q0 = pl.program_id(1) * tq

        def run(length):
            qs = q_ref[...] * scale
            k = k_ref[0:length, :]
            v = v_ref[0:length, :]
            o = jnp.zeros((tq, bw), F32)
            for e, msk in enumerate(_head_masks(dh, bw)):
                c_row = _head_c_row(ct_ref[:, 0:length], hb * hpb + e)
                w, inv = _attn_weights(jnp.where(msk, qs, 0.0).astype(BF16), k, c_row, q0)
                o = o + _dot_nn(w.astype(BF16), jnp.where(msk, v, jnp.zeros_like(v))) * inv
            o_ref[...] = o
            y_ref[...] = (o * _sigmoid(og_ref[...])).astype(BF16)

        _for_prefix(q0 + tq, buckets, run)

    qblk = pl.BlockSpec((tq, bw), lambda h, i: (i, h))
    return pl.pallas_call(
        body, name=name, grid=(nhb, t // tq),
        in_specs=[qblk, pl.BlockSpec((tq, bw), lambda h, i: (i, h + nhb)),
                  pl.BlockSpec((t, bw), lambda h, i: (0, h)), pl.BlockSpec((t, bw), lambda h, i: (0, h + nhb)),
                  pl.BlockSpec((N_HEADS, t), lambda h, i: (0, 0))],
        out_specs=[qblk, pl.BlockSpec((None, tq, bw), lambda h, i: (0, i, h))],
        out_shape=[jax.ShapeDtypeStruct((t, d), F32), jax.ShapeDtypeStruct((1, t, d), BF16)],
        compiler_params=_params("parallel", "parallel"),
    )(qg, qg, kv, kv, ct)


def _attn_bwd(dy, qg, o, kv, ct, *, tq, name):
    t, d2 = qg.shape
    d = d2 // 2
    dh = d // N_HEADS
    bw = _head_block_width(dh)
    hpb = bw // dh
    nhb = d // bw
    scale = dh ** -0.5
    buckets = _key_buckets(t, tq)

    def body(dy_ref, q_ref, og_ref, o_ref, k_ref, v_ref, ct_ref, dqg_ref, dk_ref, dv_ref, dc_ref):
        hb = pl.program_id(0)
        step = pl.program_id(1)
        q0 = step * tq

        @pl.when(step == 0)
        def _():
            dk_ref[...] = jnp.zeros((t, bw), F32)
            dv_ref[...] = jnp.zeros((t, bw), F32)
            dc_ref[...] = jnp.zeros((8, t), F32)

        def run(length):
            qs = q_ref[...] * scale
            k = k_ref[0:length, :]
            v = v_ref[0:length, :]
            sg = _sigmoid(og_ref[...])
            dyv = dy_ref[...]
            do = dyv * sg
            dqg_ref[1] = (dyv * o_ref[...] * sg * (1.0 - sg)).astype(BF16)
            dq = jnp.zeros((tq, bw), F32)
            dk = jnp.zeros((length, bw), F32)
            dv = jnp.zeros((length, bw), F32)
            dc_rows = []
            for e, msk in enumerate(_head_masks(dh, bw)):
                c_row = _head_c_row(ct_ref[:, 0:length], hb * hpb + e)
                qm = jnp.where(msk, qs, 0.0).astype(BF16)
                dom = jnp.where(msk, do, 0.0).astype(BF16)
                w, inv = _attn_weights(qm, k, c_row, q0)
                p = w * inv
                dp = _dot_nt(dom, v)
                dsc = p * (dp - jnp.sum(p * dp, axis=-1, keepdims=True))
                dsb = dsc.astype(BF16)
                dq = dq + _dot_nn(dsb, jnp.where(msk, k, jnp.zeros_like(k)))
                dk = dk + _dot_tn(dsb, qm)
                dv = dv + _dot_tn(p.astype(BF16), dom)
                dc_rows.append(-jnp.sum(dsc, axis=0, keepdims=True))
            dqg_ref[0] = (dq * scale).astype(BF16)
            dk_ref[0:length, :] += dk
            dv_ref[0:length, :] += dv
            dc_ref[:, 0:length] += _rows8(dc_rows, length)

        _for_prefix(q0 + tq, buckets, run)

    qblk = pl.BlockSpec((tq, bw), lambda h, i: (i, h))
    kblk = pl.BlockSpec((t, bw), lambda h, i: (0, h))
    return pl.pallas_call(
        body, name=name, grid=(nhb, t // tq),
        in_specs=[qblk, qblk, pl.BlockSpec((tq, bw), lambda h, i: (i, h + nhb)), qblk,
                  kblk, pl.BlockSpec((t, bw), lambda h, i: (0, h + nhb)),
                  pl.BlockSpec((N_HEADS, t), lambda h, i: (0, 0))],
        out_specs=[pl.BlockSpec((2, tq, bw), lambda h, i: (0, i, h)), kblk, kblk,
                   pl.BlockSpec((None, 8, t), lambda h, i: (h, 0, 0))],
        out_shape=[jax.ShapeDtypeStruct((2, t, d), BF16), jax.ShapeDtypeStruct((t, d), F32),
                   jax.ShapeDtypeStruct((t, d), F32), jax.ShapeDtypeStruct((nhb, 8, t), F32)],
        compiler_params=_params("parallel", "arbitrary"),
    )(dy, qg, qg, o, kv, kv, ct)


def _loss_bwd(h, tgt, *, lo, hi, tm, name):
    t, d = h.shape

    def body(h_ref, t_ref, l_ref, dy_ref):
        i = pl.program_id(0)
        rows = i * tm + lax.broadcasted_iota(jnp.int32, (tm, d), 0)
        err = jnp.where((rows >= lo) & (rows < hi), h_ref[...] - t_ref[...], 0.0)
        dy_ref[...] = err * (1.0 / d)
        part = jnp.sum(jnp.sum(err * err, axis=0, keepdims=True), axis=1, keepdims=True) * (0.5 / d)
        upd = jnp.broadcast_to(part, (8, LANES))

        @pl.when(i == 0)
        def _():
            l_ref[...] = upd

        @pl.when(i > 0)
        def _():
            l_ref[...] += upd

    row = pl.BlockSpec((tm, d), lambda i: (i, 0))
    return pl.pallas_call(
        body, name=name, grid=(t // tm,),
        in_specs=[row, row],
        out_specs=[pl.BlockSpec((8, LANES), lambda i: (0, 0)), row],
        out_shape=[jax.ShapeDtypeStruct((8, LANES), F32), jax.ShapeDtypeStruct((t, d), F32)],
        compiler_params=_params("arbitrary"),
    )(h, tgt)


def _adamw_math(w, gv, m, v):
    bc1 = 1.0 / (1.0 - ADAM_B1 ** ADAM_STEP)
    bc2 = 1.0 / (1.0 - ADAM_B2 ** ADAM_STEP)
    nm = ADAM_B1 * m + (1.0 - ADAM_B1) * gv
    nv = ADAM_B2 * v + (1.0 - ADAM_B2) * (gv * gv)
    delta = (-ADAM_LR) * ((nm * bc1) / (jnp.sqrt(nv * bc2) + ADAM_EPS) + ADAM_WD * w)
    return delta, nm, nv


def _adamw(w, g, m, v, *, name):
    r, c = w.shape
    tr = r
    for cand in (512, 256, 128, 64, 32, 16, 8):
        if r % cand == 0 and r > cand:
            tr = cand
            break

    def body(w_ref, g_ref, m_ref, v_ref, d_ref, nm_ref, nv_ref):
        d_ref[...], nm_ref[...], nv_ref[...] = _adamw_math(w_ref[...], g_ref[...], m_ref[...], v_ref[...])

    blk = pl.BlockSpec((tr, c), lambda i: (i, 0))
    out = jax.ShapeDtypeStruct((r, c), F32)
    return pl.pallas_call(
        body, name=name, grid=(r // tr,),
        in_specs=[blk] * 4, out_specs=[blk] * 3, out_shape=[out] * 3,
        compiler_params=_params("parallel"),
    )(w, g, m, v)


def _sum_adamw(recvs, sends, me, w, m, v, *, name):
    n_l = len(recvs)
    _, r, c = recvs[0].shape
    tr = _tile(r, (256, 192, 176, 128, 96, 64, 48, 32, 16))

    def body(me_ref, *refs):
        p_refs, own_refs = refs[:n_l], refs[n_l:2 * n_l]
        w_ref, m_ref, v_ref, g_ref, d_ref, nm_ref, nv_ref, acc_ref = refs[2 * n_l:]
        layer = pl.program_id(0)
        mine = me_ref[0]
        for k in range(n_l):
            @pl.when(layer == k)
            def _(k=k):
                acc_ref[...] = jnp.zeros((tr, c), F32)
                for dev in range(N_DEV):
                    @pl.when(mine == dev)
                    def _():
                        acc_ref[...] += own_refs[k][...].astype(F32)

                    @pl.when(mine != dev)
                    def _(dev=dev):
                        acc_ref[...] += p_refs[k][dev].astype(F32)
                acc = acc_ref[...]
                g_ref[...] = acc
                d_ref[...], nm_ref[...], nv_ref[...] = _adamw_math(w_ref[...], acc, m_ref[...], v_ref[...])

    p_specs = [pl.BlockSpec((N_DEV, tr, c), lambda l, i, me_ref, k=k: (0, jnp.where(l == k, i, 0), 0))
               for k in range(n_l)]
    own_specs = [pl.BlockSpec((None, tr, c), lambda l, i, me_ref, k=k: (me_ref[0], jnp.where(l == k, i, 0), 0))
                 for k in range(n_l)]
    blk = pl.BlockSpec((None, tr, c), lambda l, i, me_ref: (l, i, 0))
    out = jax.ShapeDtypeStruct((n_l, r, c), F32)
    return pl.pallas_call(
        body, name=name,
        grid_spec=pltpu.PrefetchScalarGridSpec(
            num_scalar_prefetch=1, grid=(n_l, r // tr),
            in_specs=p_specs + own_specs + [blk] * 3, out_specs=[blk] * 4,
            scratch_shapes=[pltpu.VMEM((tr, c), F32)]),
        out_shape=[out] * 4,
        compiler_params=_params("arbitrary", "arbitrary"),
    )(me, *recvs, *sends, w, m, v)


def _sum8(parts, *, name):
    _, r, c = parts.shape
    tr = r
    for cand in (512, 256, 128, 64, 32, 16):
        if r % cand == 0 and r > cand:
            tr = cand
            break

    def body(p_ref, o_ref):
        acc = p_ref[0].astype(F32)
        for k in range(1, N_DEV):
            acc = acc + p_ref[k].astype(F32)
        o_ref[...] = acc

    return pl.pallas_call(
        body, name=name, grid=(r // tr,),
        in_specs=[pl.BlockSpec((N_DEV, tr, c), lambda i: (0, i, 0))],
        out_specs=pl.BlockSpec((tr, c), lambda i: (i, 0)),
        out_shape=jax.ShapeDtypeStruct((r, c), F32),
        compiler_params=_params("parallel"),
    )(parts)


def _my_index():
    return 4 * lax.axis_index("x") + 2 * lax.axis_index("y") + lax.axis_index("c")


def _peer(k):
    x, y, c = lax.axis_index("x"), lax.axis_index("y"), lax.axis_index("c")
    px = x ^ ((k >> 2) & 1)
    py = y ^ ((k >> 1) & 1)
    pc = c ^ (k & 1)
    return (px, py, pc), 4 * px + 2 * py + pc


def _all_gather(shards, *, name):
    n_arr = len(shards)

    def body(*refs):
        ins, outs = refs[:n_arr], refs[n_arr:2 * n_arr]
        send_sems, recv_sems, local_sems = refs[2 * n_arr:]
        me = _my_index()
        local = [pltpu.make_async_copy(ins[n], outs[n].at[me], local_sems.at[n]) for n in range(n_arr)]
        for cp in local:
            cp.start()
        sends = []
        for k in range(1, N_DEV):
            peer, _ = _peer(k)
            for n in range(n_arr):
                cp = pltpu.make_async_remote_copy(
                    src_ref=ins[n], dst_ref=outs[n].at[me], send_sem=send_sems.at[n, k - 1],
                    recv_sem=recv_sems.at[n, k - 1], device_id=peer, device_id_type=pl.DeviceIdType.MESH)
                cp.start()
                sends.append(cp)
        for k in range(1, N_DEV):
            peer, pidx = _peer(k)
            for n in range(n_arr):
                pltpu.make_async_remote_copy(
                    src_ref=ins[n], dst_ref=outs[n].at[pidx], send_sem=send_sems.at[n, k - 1],
                    recv_sem=recv_sems.at[n, k - 1], device_id=peer, device_id_type=pl.DeviceIdType.MESH).wait_recv()
        for cp in sends:
            cp.wait_send()
        for cp in local:
            cp.wait()

    hbm = pl.BlockSpec(memory_space=pl.ANY)
    return pl.pallas_call(
        body, name=name,
        in_specs=[hbm] * n_arr, out_specs=[hbm] * n_arr,
        out_shape=[jax.ShapeDtypeStruct((N_DEV,) + s.shape, s.dtype) for s in shards],
        scratch_shapes=[pltpu.SemaphoreType.DMA((n_arr, N_DEV - 1)), pltpu.SemaphoreType.DMA((n_arr, N_DEV - 1)),
                        pltpu.SemaphoreType.DMA((n_arr,))],
        compiler_params=pltpu.CompilerParams(has_side_effects=True),
    )(*shards)


_HBM = pl.BlockSpec(memory_space=pltpu.HBM)
_SEM = pl.BlockSpec(memory_space=pltpu.SEMAPHORE)
_EFFECT = pltpu.SideEffectType.DATAFLOW_SIDE_EFFECTING


def _remote(src, dst, send_sem, recv_sem, peer):
    return pltpu.make_async_remote_copy(src_ref=src, dst_ref=dst, send_sem=send_sem, recv_sem=recv_sem,
                                        device_id=peer, device_id_type=pl.DeviceIdType.MESH)


def _place_own(src, layer, me, *, out_dtype, name):
    _, r, c = src.shape
    tr = _tile(r, (256, 192, 176, 128, 96, 64, 48, 32, 16))

    def body(me_ref, s_ref, o_ref):
        o_ref[...] = s_ref[...].astype(out_dtype)

    return pl.pallas_call(
        body, name=name,
        grid_spec=pltpu.PrefetchScalarGridSpec(
            num_scalar_prefetch=1, grid=(r // tr,),
            in_specs=[pl.BlockSpec((None, tr, c), lambda i, me_ref: (layer, i, 0))],
            out_specs=pl.BlockSpec((None, tr, c), lambda i, me_ref: (me_ref[0], i, 0))),
        out_shape=jax.ShapeDtypeStruct((N_DEV, r, c), out_dtype),
        compiler_params=_params("parallel"),
    )(me, src)


def _own_blocks(srcs, *, name):
    n = len(srcs)

    def body(*refs):
        ins, outs, sems = refs[:n], refs[n:2 * n], refs[2 * n]
        me = _my_index()
        cps = [pltpu.make_async_copy(ins[t].at[me], outs[t].at[me], sems.at[t]) for t in range(n)]
        for cp in cps:
            cp.start()
        for cp in cps:
            cp.wait()

    return pl.pallas_call(
        body, name=name, in_specs=[_HBM] * n, out_specs=[_HBM] * n,
        out_shape=[jax.ShapeDtypeStruct(s.shape, s.dtype) for s in srcs],
        scratch_shapes=[pltpu.SemaphoreType.DMA((n,))],
    )(*srcs)


def _split_start(groups, *, scatter, name):
    sizes = [len(srcs) for srcs, _ in groups]
    flat_src = [s for srcs, _ in groups for s in srcs]
    flat_land = [l for _, lands in groups for l in lands]
    n, n_g = len(flat_land), len(groups)
    if not scatter:
        flat_src = []
    n_in = len(flat_src) + n

    def body(*refs):
        lands = refs[n_in - n:n_in]
        ins = refs[:n] if scatter else lands
        sems = refs[n_in:n_in + 2 * n_g]
        token = refs[-1]
        me = _my_index()
        t = 0
        for g in range(n_g):
            for q in range(sizes[g]):
                for k in range(1, N_DEV):
                    peer, pidx = _peer(k)
                    src = ins[t].at[pidx] if scatter else ins[t].at[me]
                    slot = q * (N_DEV - 1) + k - 1
                    _remote(src, lands[t].at[me], sems[2 * g].at[slot], sems[2 * g + 1].at[slot], peer).start()
                t += 1
        token[...] = jnp.zeros_like(token)

    sem_shapes = []
    for sz in sizes:
        sem_shapes += [pltpu.SemaphoreType.DMA((sz * (N_DEV - 1),)), pltpu.SemaphoreType.DMA((sz * (N_DEV - 1),))]
    outs = pl.pallas_call(
        body, name=name,
        in_specs=[_HBM] * n_in,
        out_specs=[_SEM] * (2 * n_g) + [_HBM] * n_in + [pl.BlockSpec(memory_space=pltpu.VMEM)],
        out_shape=sem_shapes + [pltpu.HBM(a.shape, a.dtype) for a in flat_src + flat_land]
        + [jax.ShapeDtypeStruct((8, LANES), F32)],
        input_output_aliases={i: 2 * n_g + i for i in range(n_in)},
        compiler_params=pltpu.CompilerParams(has_side_effects=_EFFECT),
    )(*[pltpu.with_memory_space_constraint(a, pltpu.HBM) for a in flat_src + flat_land])
    sems, thru, token = outs[:2 * n_g], outs[2 * n_g:2 * n_g + n_in], outs[-1]
    handles, pos = [], 0
    for g, sz in enumerate(sizes):
        lands_g = thru[n_in - n + pos:n_in - n + pos + sz]
        handles.append((sems[2 * g], sems[2 * g + 1], thru[pos:pos + sz] if scatter else [], lands_g))
        pos += sz
    return handles, token


def _split_wait(handle, after, *, scatter, name):
    send_sems, recv_sems, srcs, lands = handle
    n, n_src = len(lands), len(srcs)

    def body(*refs):
        lnd = refs[n_src:n_src + n]
        ins = refs[:n_src] if scatter else lnd
        ssem, rsem = refs[n_src + n], refs[n_src + n + 1]
        me = _my_index()
        for t in range(n):
            for k in range(1, N_DEV):
                peer, pidx = _peer(k)
                block = ins[t].at[me]
                slot = t * (N_DEV - 1) + k - 1
                _remote(block, lnd[t].at[me], ssem.at[slot], rsem.at[slot], peer).wait_send()
                _remote(block, lnd[t].at[pidx], ssem.at[slot], rsem.at[slot], peer).wait_recv()

    return pl.pallas_call(
        body, name=name,
        in_specs=[_HBM] * (n_src + n) + [_SEM, _SEM, pl.BlockSpec(memory_space=pl.ANY)],
        out_specs=[_HBM] * n,
        out_shape=[pltpu.HBM(l.shape, l.dtype) for l in lands],
        input_output_aliases={n_src + t: t for t in range(n)},
        compiler_params=pltpu.CompilerParams(has_side_effects=_EFFECT),
    )(*srcs, *lands, send_sems, recv_sems, after)


def _pack(arrs, dtype, row_quantum=16):
    flat = jnp.concatenate([a.astype(dtype).reshape(-1) for a in arrs])
    pad = (-flat.shape[0]) % (row_quantum * PACK_COLS)
    if pad:
        flat = jnp.concatenate([flat, jnp.zeros((pad,), dtype)])
    return flat.reshape(-1, PACK_COLS)


def _pack8(arrs, dtype):
    flat = jnp.concatenate([a.astype(dtype).reshape(N_DEV, -1) for a in arrs], axis=1)
    pad = (-flat.shape[1]) % (16 * PACK_COLS)
    if pad:
        flat = jnp.concatenate([flat, jnp.zeros((N_DEV, pad), dtype)], axis=1)
    return flat.reshape(N_DEV, -1, PACK_COLS)


def _unpack(slab, shapes, lead):
    lead_shape = slab.shape[:lead]
    flat = slab.reshape(lead_shape + (-1,))
    outs, off = [], 0
    for shp in shapes:
        size = math.prod(shp)
        outs.append(flat[..., off:off + size].reshape(lead_shape + tuple(shp)))
        off += size
    return outs


def _cols_full(g):
    g = jnp.moveaxis(g, 0, -2)
    return g.reshape(g.shape[:-2] + (g.shape[-2] * g.shape[-1],))


def _cols_split(full):
    n = full.shape[-1] // N_DEV
    return jnp.moveaxis(full.reshape(full.shape[:-1] + (N_DEV, n)), -2, 0)


def _block_diag(w, per):
    n, b, _ = w.shape
    w4 = w.reshape(n // per, per, b, b)
    eye = jnp.eye(per, dtype=w.dtype)
    return jnp.einsum('gpab,pq->gpaqb', w4, eye).reshape(n // per, per * b, per * b)


def _block_diag_extract(g, per):
    gn, cb, _ = g.shape
    b = cb // per
    g5 = g.reshape(gn, per, b, per, b)
    return jnp.stack([g5[:, p, :, p, :] for p in range(per)], axis=1).reshape(gn * per, b, b)


def _slab2d(a):
    return a.reshape(-1, a.shape[-1])


def _lru_block_cols(r_dim):
    lru = r_dim // N_LRU_BLOCKS
    return lru * LANES // math.gcd(lru, LANES)


BIG = ("a_w_in", "a_w_out", "b_w_in", "b_w_out", "f_w_in", "f_w_out")
COL_F32 = ("meta", "a_conv_w", "a_conv_b", "a_b_r", "a_b_i", "a_lambda", "f_conv_w")
REPLICATED = ("a_w_r", "a_w_i", "kv_f_b", "f_conv_b", "ln1_g", "ln1_b", "ln2_g", "ln2_b")
WEIGHT_NAMES = ("meta", "a_w_in", "a_conv_w", "a_conv_b", "a_w_r", "a_b_r", "a_w_i", "a_b_i", "a_lambda", "a_w_out",
                "kv_w", "kv_f_b", "b_w_in", "b_w_out", "f_w_in", "f_conv_w", "f_conv_b", "f_w_out",
                "ln1_g", "ln1_b", "ln2_g", "ln2_b")


def _kv_layout(kv_gathered, d):
    kv_full = _cols_full(kv_gathered)
    kv_pad = 2 * d + LANES - kv_full.shape[1]
    return jnp.concatenate([kv_full, jnp.zeros((d, kv_pad), kv_full.dtype)], axis=1)


def _small_layouts(small):
    r_dim = small["a_lambda"].shape[1]
    n_f = small["f_conv_b"].shape[1] // N_DEV
    cb = _lru_block_cols(r_dim)
    per = cb // (r_dim // N_LRU_BLOCKS)
    n_a = small["a_lambda"].shape[0]
    f_conv_w3 = small["f_conv_w"].reshape(N_LAYERS, 3, N_DEV, n_f).transpose(0, 2, 1, 3)
    f_conv_b3 = small["f_conv_b"].reshape(N_LAYERS, N_DEV, 1, n_f)
    return {
        "kv_fb": jnp.concatenate([small["kv_f_b"], jnp.zeros((LANES - N_HEADS,), F32)])[None],
        "a_cwb": jnp.concatenate([small["a_conv_w"], small["a_conv_b"][:, None],
                                  jnp.zeros((n_a, 3, r_dim), F32)], axis=1),
        "a_vecs": jnp.concatenate([jnp.stack([small["a_b_r"], small["a_b_i"], small["a_lambda"]], axis=1),
                                   jnp.zeros((n_a, 5, r_dim), F32)], axis=1),
        "a_bd_r": jnp.stack([_block_diag(small["a_w_r"][l], per) for l in range(n_a)]).astype(BF16),
        "a_bd_i": jnp.stack([_block_diag(small["a_w_i"][l], per) for l in range(n_a)]).astype(BF16),
        "f_cwb3": jnp.concatenate([f_conv_w3, f_conv_b3, jnp.zeros((N_LAYERS, N_DEV, 4, n_f), F32)], axis=2),
        "ln1_g": small["ln1_g"][:, None], "ln1_b": small["ln1_b"][:, None],
        "ln2_g": small["ln2_g"][:, None], "ln2_b": small["ln2_b"][:, None],
    }


def _local_step(h0, tgt, n_meta, n_tok, wts, hooks):
    tp, d = h0.shape
    tm = tp // 8 if (tp // 8) % 16 == 0 else tp
    tmb = _tile(tp, (1088, 512, 320, 256, 128))
    tq = 128
    r_dim = wts["a_vecs"].shape[2]
    cb = wts["a_bd_r"].shape[-1]
    sb = LANES
    n_b = N_LAYERS - N_A_LAYERS

    h, h_bf = h0, h0.astype(BF16)
    saved = []
    kvs = None
    for layer in range(N_LAYERS):
        lw = {}
        sv = {"h_bf": h_bf, "w": lw}
        if layer < N_A_LAYERS:
            lw["in"] = hooks.weight(layer, "in", h)
            sv["gr"] = _proj_in(h_bf, lw["in"], shard_major=False, name="a_in_proj")
            sv["rec"] = _conv_a_fwd(sv["gr"], wts["a_cwb"][layer], cb=cb, name="a_conv_fwd")
            a, u, sv["r"], sv["i"] = _gates_fwd(sv["rec"], wts["a_bd_r"][layer], wts["a_bd_i"][layer],
                                                wts["a_vecs"][layer], tm=tm, name="a_gates_fwd")
            sv["a"] = a
            sv["hr"], y3 = _scan_fwd(a, u, sv["gr"], cb=sb, name="a_scan_fwd")
        else:
            j = layer - N_A_LAYERS
            if j == 0:
                kv_w = _kv_layout(hooks.weight(layer, "kv_w", h), d)
                kvs = {"h_bf": h_bf, "w": kv_w}
                kvs["kv"] = _mm_nn(h_bf, kv_w[:, :2 * d], tn=_tile(2 * d, (512, 256, 128)), out_dtype=BF16,
                                   name="kv_proj")
                kvs["fp"] = _mm_nn(h_bf, kv_w[:, 2 * d:], tn=LANES, out_dtype=F32, name="f_proj")
                kvs["c"], ct = _fgate_fwd(kvs["fp"], wts["kv_fb"], tq=tq, name="fgate_fwd")
                kvs["ct"] = ct[:N_HEADS]
            lw["in"] = hooks.weight(layer, "in", kvs["c"] if j == 0 else h)
            sv["qg"] = _proj_in(h_bf, lw["in"], shard_major=False, name="b_in_proj")
            sv["o"], y3 = _attn_fwd(sv["qg"], kvs["kv"], kvs["ct"], tq=tq, name="attn_fwd")
        sv["y3"] = y3
        lw["out"] = hooks.weight(layer, "out", y3)
        sv["s1"], h, h_bf = _out_ln(y3, lw["out"], h, wts["ln1_g"][layer], wts["ln1_b"][layer], n_valid=n_tok,
                                    tm=tm, name="mix_out_ln")
        sv["h1_bf"] = h_bf
        lw["f_in"] = hooks.weight(layer, "f_in", h)
        sv["z3"] = _proj_in(h_bf, lw["f_in"], shard_major=True, transposed=True, name="f_in_proj")
        sv["yf3"] = _convglu_fwd(sv["z3"], wts["f_cwb3"][layer], name="f_convglu_fwd")
        lw["f_out"] = hooks.weight(layer, "f_out", sv["yf3"])
        sv["s2"], h, h_bf = _out_ln(sv["yf3"], lw["f_out"], h, wts["ln2_g"][layer], wts["ln2_b"][layer],
                                    n_valid=n_tok, tm=tm, name="ffn_out_ln")
        saved.append(sv)

    loss_tile, dh = _loss_bwd(h, tgt, lo=n_meta, hi=n_tok, tm=tm, name="loss")

    grads = {k: [None] * N_LAYERS for k in ("f_cwb3", "ln1_gb", "ln2_gb")}
    grads.update({k: [None] * N_A_LAYERS for k in ("a_cwb", "a_bd_r", "a_bd_i", "a_vecs")})
    dkv = []
    token = jnp.zeros((), F32)
    for layer in reversed(range(N_LAYERS)):
        sv = saved[layer]
        lw = sv["w"]
        big = {}
        ds, ds_bf, grads["ln2_gb"][layer] = _ln_bwd(dh, sv["s2"], wts["ln2_g"][layer] + token, tm=tm, name="ln_bwd")
        dz, dcw = _ffn_bwd_mid(ds_bf, lw["f_out"], sv["z3"], wts["f_cwb3"][layer], name="f_bwd_mid")
        grads["f_cwb3"][layer] = dcw.reshape((N_DEV,) + dcw.shape[2:])
        dz3 = dz
        big["f_out"] = _w_out_grad(sv["yf3"], ds_bf, lw["f_out"].shape[1], name="f_w_out_grad")
        dh = _in_bwd(dz3, lw["f_in"], ds, tm=tmb, transposed=True, name="f_in_bwd")
        big["f_in"] = _w_in_grad(sv["h1_bf"], dz3, transposed=True, name="f_w_in_grad")
        token = hooks.grads_ready(layer, "ffn", big)
        big = {}
        ds, ds_bf, grads["ln1_gb"][layer] = _ln_bwd(dh, sv["s1"], wts["ln1_g"][layer] + token, tm=tm, name="ln_bwd")
        if layer < N_A_LAYERS:
            dy = _out_bwd(ds_bf, lw["out"], tm=tmb // 2, name="a_out_bwd")
            big["out"] = _w_out_grad(sv["y3"], ds_bf, lw["out"].shape[1], name="a_w_out_grad")
            d_h, d_a, dgate = _scan_bwd(dy, sv["gr"], sv["hr"], sv["a"], cb=sb, name="a_scan_bwd")
            d_rec, dpr, dpi, grads["a_vecs"][layer] = _gates_bwd(
                sv["rec"], sv["r"], sv["i"], sv["a"], d_h, d_a, wts["a_bd_r"][layer], wts["a_bd_i"][layer],
                wts["a_vecs"][layer], tm=tm, name="a_gates_bwd")
            grads["a_bd_r"][layer], grads["a_bd_i"][layer] = _bd_grad(sv["rec"], dpr, dpi, cb=cb, name="a_bd_grad")
            dact, grads["a_cwb"][layer] = _conv_a_bwd(d_rec, sv["gr"], dgate, wts["a_cwb"][layer], cb=cb,
                                                      name="a_conv_bwd")
            dh = _in_bwd(dact, lw["in"], ds, tm=tmb, name="a_in_bwd")
            big["in"] = _w_in_grad(sv["h_bf"], dact, name="a_w_in_grad")
        else:
            j = layer - N_A_LAYERS
            dy = _out_bwd(ds_bf, lw["out"], tm=tmb // 2, name="b_out_bwd")
            big["out"] = _w_out_grad(sv["y3"], ds_bf, lw["out"].shape[1], name="b_w_out_grad")
            dqg, dk, dv, dc = _attn_bwd(dy, sv["qg"], sv["o"], kvs["kv"], kvs["ct"], tq=tq,
                                        name="attn_bwd")
            dkv.append((dk, dv, dc))
            dh = _in_bwd(dqg, lw["in"], ds, tm=tmb, name="b_in_bwd")
            big["in"] = _w_in_grad(sv["h_bf"], dqg, name="b_w_in_grad")
            if j == 0:
                hpb = _head_block_width(d // N_HEADS) // (d // N_HEADS)
                dct = (dkv[0][2] + dkv[1][2])[:, :hpb, :].reshape(N_HEADS, tp)
                dct = jnp.concatenate([dct, jnp.zeros((LANES - N_HEADS, tp), F32)])
                df_bf, grads["kv_fb"] = _fgate_bwd(dct, kvs["fp"], wts["kv_fb"], tq=tq, name="fgate_bwd")
                dkvz = jnp.concatenate([_pair_sum(dkv[0][0], dkv[1][0], tm=tm, name="kv_pair_sum"),
                                        _pair_sum(dkv[0][1], dkv[1][1], tm=tm, name="kv_pair_sum"), df_bf], axis=1)
                dh = _mm_nt_full(dkvz, kvs["w"], dh, tm=tmb // 2, name="kv_in_bwd")
                big["kv_w"] = _mm_tn_cols(kvs["h_bf"], dkvz, tn=LANES, name="kv_w_grad")
        token = hooks.grads_ready(layer, "mix", big)
    return loss_tile, dh, grads


def _finish_small_grads(grads, d_h0, n_meta):
    r_dim = grads["a_vecs"][0].shape[1]
    per = _lru_block_cols(r_dim) // (r_dim // N_LRU_BLOCKS)
    a_cwb = jnp.stack(grads["a_cwb"])
    a_vecs = jnp.stack(grads["a_vecs"])
    f_cwb3 = jnp.stack(grads["f_cwb3"])
    ln1 = jnp.stack(grads["ln1_gb"])
    ln2 = jnp.stack(grads["ln2_gb"])
    f_rows = f_cwb3.transpose(0, 2, 1, 3).reshape(N_LAYERS, 8, -1)
    return {
        "meta": d_h0[:n_meta],
        "a_conv_w": a_cwb[:, :4], "a_conv_b": a_cwb[:, 4],
        "a_w_r": jnp.stack([_block_diag_extract(g, per) for g in grads["a_bd_r"]]),
        "a_b_r": a_vecs[:, 0],
        "a_w_i": jnp.stack([_block_diag_extract(g, per) for g in grads["a_bd_i"]]),
        "a_b_i": a_vecs[:, 1], "a_lambda": a_vecs[:, 2],
        "kv_f_b": grads["kv_fb"][0, :N_HEADS],
        "f_conv_w": f_rows[:, :3], "f_conv_b": f_rows[:, 3],
        "ln1_g": ln1[:, 0], "ln1_b": ln1[:, 1], "ln2_g": ln2[:, 0], "ln2_b": ln2[:, 1],
    }


def kernel(x, meta, a_w_in, a_conv_w, a_conv_b, a_w_r, a_b_r, a_w_i, a_b_i, a_lambda, a_w_out, kv_w, kv_f_b, b_w_in, b_w_out, f_w_in, f_conv_w, f_conv_b, f_w_out, ln1_g, ln1_b, ln2_g, ln2_b, loss_target, m_meta, m_a_w_in, m_a_conv_w, m_a_conv_b, m_a_w_r, m_a_b_r, m_a_w_i, m_a_b_i, m_a_lambda, m_a_w_out, m_kv_w, m_kv_f_b, m_b_w_in, m_b_w_out, m_f_w_in, m_f_conv_w, m_f_conv_b, m_f_w_out, m_ln1_g, m_ln1_b, m_ln2_g, m_ln2_b, v_meta, v_a_w_in, v_a_conv_w, v_a_conv_b, v_a_w_r, v_a_b_r, v_a_w_i, v_a_b_i, v_a_lambda, v_a_w_out, v_kv_w, v_kv_f_b, v_b_w_in, v_b_w_out, v_f_w_in, v_f_conv_w, v_f_conv_b, v_f_w_out, v_ln1_g, v_ln1_b, v_ln2_g, v_ln2_b):
    w = dict(meta=meta, a_w_in=a_w_in, a_conv_w=a_conv_w, a_conv_b=a_conv_b, a_w_r=a_w_r, a_b_r=a_b_r, a_w_i=a_w_i,
             a_b_i=a_b_i, a_lambda=a_lambda, a_w_out=a_w_out, kv_w=kv_w, kv_f_b=kv_f_b, b_w_in=b_w_in,
             b_w_out=b_w_out, f_w_in=f_w_in, f_conv_w=f_conv_w, f_conv_b=f_conv_b, f_w_out=f_w_out, ln1_g=ln1_g,
             ln1_b=ln1_b, ln2_g=ln2_g, ln2_b=ln2_b)
    m = dict(meta=m_meta, a_w_in=m_a_w_in, a_conv_w=m_a_conv_w, a_conv_b=m_a_conv_b, a_w_r=m_a_w_r, a_b_r=m_a_b_r,
             a_w_i=m_a_w_i, a_b_i=m_a_b_i, a_lambda=m_a_lambda, a_w_out=m_a_w_out, kv_w=m_kv_w, kv_f_b=m_kv_f_b,
             b_w_in=m_b_w_in, b_w_out=m_b_w_out, f_w_in=m_f_w_in, f_conv_w=m_f_conv_w, f_conv_b=m_f_conv_b,
             f_w_out=m_f_w_out, ln1_g=m_ln1_g, ln1_b=m_ln1_b, ln2_g=m_ln2_g, ln2_b=m_ln2_b)
    v = dict(meta=v_meta, a_w_in=v_a_w_in, a_conv_w=v_a_conv_w, a_conv_b=v_a_conv_b, a_w_r=v_a_w_r, a_b_r=v_a_b_r,
             a_w_i=v_a_w_i, a_b_i=v_a_b_i, a_lambda=v_a_lambda, a_w_out=v_a_w_out, kv_w=v_kv_w, kv_f_b=v_kv_f_b,
             b_w_in=v_b_w_in, b_w_out=v_b_w_out, f_w_in=v_f_w_in, f_conv_w=v_f_conv_w, f_conv_b=v_f_conv_b,
             f_w_out=v_f_w_out, ln1_g=v_ln1_g, ln1_b=v_ln1_b, ln2_g=v_ln2_g, ln2_b=v_ln2_b)
    shapes = {n: w[n].shape for n in WEIGHT_NAMES}

    me = jnp.reshape(_my_index(), (1,)).astype(jnp.int32)

    def as_stored(name, a):
        return jnp.swapaxes(a, 1, 2) if name == "f_w_in" else a

    param_of = {"in": ("a_w_in", "b_w_in"), "out": ("a_w_out", "b_w_out"), "f_in": ("f_w_in",) * 2,
                "f_out": ("f_w_out",) * 2}
    order = [("small", None, None)]
    for layer in range(N_LAYERS):
        if layer == N_A_LAYERS:
            order.append(("kv_w", layer, 0))
        for key in ("in", "out", "f_in", "f_out"):
            order.append((key, layer, layer if key[0] == "f" or layer < N_A_LAYERS else layer - N_A_LAYERS))
    def place(key, layer, idx):
        if key == "small":
            return _place_own(_pack([w[n] for n in COL_F32], F32)[None], 0, me, out_dtype=F32, name="place_small")
        if key == "kv_w":
            return _place_own(w["kv_w"][None], 0, me, out_dtype=BF16, name="place_kv_w")
        name = param_of[key][0 if layer < N_A_LAYERS else 1]
        return _place_own(as_stored(name, w[name]), idx, me, out_dtype=BF16, name=f"place_{name}_{idx}")

    lands = [place(*o) for o in order]
    gather_handles, gather_token = _split_start([([l], [l]) for l in lands], scatter=False, name="gather_start")
    group_of = {(key, layer): g for g, (key, layer, _) in enumerate(order)}
    (got_s,) = _split_wait(gather_handles[0], gather_token, scatter=False, name="gather_wait_small")
    small = {n: w[n] for n in REPLICATED}
    for n, part in zip(COL_F32, _unpack(got_s, [w[n].shape for n in COL_F32], 1)):
        small[n] = _cols_full(part)
    n_meta, d = small["meta"].shape

    class Hooks:
        pending = None
        received = {}
        sent = {}

        @staticmethod
        def weight(layer, key, after):
            (got,) = _split_wait(gather_handles[group_of[(key, layer)]], after, scatter=False,
                                 name=f"gather_wait_{key}_{layer}")
            return got

        @staticmethod
        def collect(after):
            if Hooks.pending is not None:
                tag, names, handle = Hooks.pending
                got = _split_wait(handle, after, scatter=True, name=f"scatter_wait_{tag}")
                Hooks.received.update(zip(names, got))
                Hooks.pending = None

        @staticmethod
        def grads_ready(layer, part, big):
            if "kv_w" in big:
                big["kv_w"] = _cols_split(big["kv_w"][:, :shapes["kv_w"][1] * N_DEV]).astype(BF16)
            names = [(key, layer) for key in big]
            send = [big[key] for key in big]
            Hooks.collect(send[0])
            empty = [lax.empty(s.shape, s.dtype) for s in send]
            handles, token = _split_start([(send, empty)], scatter=True, name=f"scatter_start_{part}_{layer}")
            Hooks.pending = (f"{part}_{layer}", names, handles[0])
            Hooks.sent.update(zip(names, handles[0][2]))
            return token[0, 0]

    Hooks.pending, Hooks.received, Hooks.sent = None, {}, {}

    n_tok = n_meta + x.shape[1]
    tp = -(-n_tok // ROW_ALIGN) * ROW_ALIGN
    pad = jnp.zeros((tp - n_tok, d), F32)
    h0 = jnp.concatenate([small["meta"], x[0], pad])
    tgt = jnp.concatenate([jnp.zeros((n_meta, d), F32), loss_target[0], pad])
    loss_tile, d_h0, grads = _local_step(h0, tgt, n_meta, n_tok, _small_layouts(small), Hooks)
    g_small = _finish_small_grads(grads, d_h0, n_meta)
    loss = lax.psum(loss_tile[0, 0], MESH_AXES)
    grad_x = d_h0[n_meta:n_tok][None]

    rep = _pack([g_small[n] for n in REPLICATED], F32, row_quantum=16 * N_DEV)
    send = [_pack8([_cols_split(g_small[n]) for n in COL_F32], F32), rep.reshape(N_DEV, -1, PACK_COLS)]
    lands = _own_blocks(send, name="scatter_own_small")
    handles, token = _split_start([(send, lands)], scatter=True, name="scatter_start_small")

    g, delta, new_m, new_v = {}, {}, {}, {}
    layers_of = {
        "a_w_in": [("in", l) for l in range(N_A_LAYERS)], "a_w_out": [("out", l) for l in range(N_A_LAYERS)],
        "b_w_in": [("in", l) for l in range(N_A_LAYERS, N_LAYERS)],
        "b_w_out": [("out", l) for l in range(N_A_LAYERS, N_LAYERS)],
        "f_w_in": [("f_in", l) for l in range(N_LAYERS)], "f_w_out": [("f_out", l) for l in range(N_LAYERS)],
        "kv_w": [("kv_w", N_A_LAYERS)],
    }
    ready = [n for n in BIG + ("kv_w",) if all(t in Hooks.received for t in layers_of[n])]

    def done(names):
        return jnp.stack([g[n][(0,) * g[n].ndim] for n in names])

    for n in ready + [n for n in BIG + ("kv_w",) if n not in ready]:
        if n not in ready and Hooks.pending is not None:
            Hooks.collect(done(ready))
        lift = (lambda a: a[None]) if n == "kv_w" else (lambda a, n=n: as_stored(n, a))
        outs = _sum_adamw([Hooks.received[t] for t in layers_of[n]], [Hooks.sent[t] for t in layers_of[n]], me,
                          lift(w[n]), lift(m[n]), lift(v[n]), name="sum_adamw_" + n)
        g[n], delta[n], new_m[n], new_v[n] = [as_stored(n, o).reshape(shapes[n]) for o in outs]
    recv_s, recv_r = _split_wait(handles[0], done(BIG + ("kv_w",)), scatter=True, name="scatter_wait_small")
    sum_s = _sum8(recv_s, name="sum_grads_f32")
    g.update(zip(COL_F32, _unpack(sum_s, [shapes[n] for n in COL_F32], 0)))
    (got_r,) = _all_gather([_sum8(recv_r, name="sum_grads_replicated")], name="gather_replicated_sums")
    g.update(zip(REPLICATED, _unpack(got_r.reshape(-1, PACK_COLS), [shapes[n] for n in REPLICATED], 0)))

    for n in COL_F32 + REPLICATED:
        shp = shapes[n]
        dl, nm, nv = _adamw(_slab2d(w[n]), _slab2d(g[n]), _slab2d(m[n]), _slab2d(v[n]), name="adamw")
        delta[n], new_m[n], new_v[n] = dl.reshape(shp), nm.reshape(shp), nv.reshape(shp)
    return (loss, grad_x, *[g[n] for n in WEIGHT_NAMES], *[delta[n] for n in WEIGHT_NAMES],
            *[new_m[n] for n in WEIGHT_NAMES], *[new_v[n] for n in WEIGHT_NAMES])
```

```python
import math

import jax
import jax.numpy as jnp
from jax import lax
from jax.experimental import pallas as pl
from jax.experimental.pallas import tpu as pltpu

F32 = jnp.float32
BF16 = jnp.bfloat16

N_DEV = 8
MESH_AXES = ("x", "y", "c")
N_LAYERS = 4
N_A_LAYERS = 2
N_LRU_BLOCKS = 16
N_HEADS = 16
LRU_C = 8.0
DN_ALPHA = (2 * N_LAYERS) ** 0.25
LN_EPS = 1e-5
ADAM_LR, ADAM_B1, ADAM_B2, ADAM_EPS, ADAM_WD, ADAM_STEP = 0.001, 0.9, 0.999, 1e-08, 0.01, 10

LANES = 128
SUBLANES = 8
ROW_ALIGN = 128
VMEM_LIMIT_BYTES = 56 * 1024 * 1024
GELU_K = math.sqrt(2.0 / math.pi)
GELU_C = 0.044715
PACK_COLS = 1024


def _params(*sem):
    return pltpu.CompilerParams(dimension_semantics=sem, vmem_limit_bytes=VMEM_LIMIT_BYTES)


def _gelu(x):
    th = jnp.tanh(GELU_K * (x + GELU_C * x * x * x))
    return 0.5 * x * (1.0 + th)


def _gelu_and_grad(x):
    x2 = x * x
    th = jnp.tanh(GELU_K * (x + GELU_C * x2 * x))
    g = 0.5 * x * (1.0 + th)
    dg = 0.5 * (1.0 + th) + 0.5 * x * (1.0 - th * th) * (GELU_K * (1.0 + 3.0 * GELU_C * x2))
    return g, dg


def _sigmoid(x):
    return 1.0 / (1.0 + jnp.exp(-x))


def _expm1(x):
    small = x * (1.0 + 0.5 * x * (1.0 + (1.0 / 3.0) * x * (1.0 + 0.25 * x)))
    return jnp.where(jnp.abs(x) < 1e-2, small, jnp.exp(x) - 1.0)


def _softplus(x):
    e = jnp.exp(-jnp.abs(x))
    small = e * (1.0 - 0.5 * e * (1.0 - (2.0 / 3.0) * e))
    return jnp.maximum(x, 0.0) + jnp.where(e < 1e-2, small, jnp.log(1.0 + e))


def _shift_down(x, s):
    if s == 0:
        return x
    rows = lax.broadcasted_iota(jnp.int32, x.shape, 0)
    return jnp.where(rows >= s, pltpu.roll(x, s, 0), 0.0)


def _shift_up(x, s):
    if s == 0:
        return x
    n = x.shape[0]
    rows = lax.broadcasted_iota(jnp.int32, x.shape, 0)
    return jnp.where(rows < n - s, pltpu.roll(x, n - s, 0), 0.0)


def _dot_nn(a, b):
    return lax.dot_general(a, b, (((1,), (0,)), ((), ())), preferred_element_type=F32)


def _dot_nt(a, b):
    return lax.dot_general(a, b, (((1,), (1,)), ((), ())), preferred_element_type=F32)


def _dot_tn(a, b):
    return lax.dot_general(a, b, (((0,), (0,)), ((), ())), preferred_element_type=F32)


def _rows8(vals, width):
    rows = lax.broadcasted_iota(jnp.int32, (8, width), 0)
    out = jnp.zeros((8, width), F32)
    for k, v in enumerate(vals):
        out = jnp.where(rows == k, jnp.broadcast_to(v, (8, width)), out)
    return out


def _tile(n, prefer):
    for c in prefer:
        if n % c == 0:
            return c
    return n


def _mm_nn(a, b, *, tn, out_dtype, name):
    m, k = a.shape
    n = b.shape[1]

    def body(a_ref, b_ref, o_ref):
        o_ref[...] = _dot_nn(a_ref[...], b_ref[...]).astype(o_ref.dtype)

    return pl.pallas_call(
        body, name=name, grid=(n // tn,),
        in_specs=[pl.BlockSpec((m, k), lambda j: (0, 0)), pl.BlockSpec((k, tn), lambda j: (0, j))],
        out_specs=pl.BlockSpec((m, tn), lambda j: (0, j)),
        out_shape=jax.ShapeDtypeStruct((m, n), out_dtype),
        compiler_params=_params("parallel"),
    )(a, b)


def _proj_in(h_bf, g_in, *, shard_major, name, transposed=False):
    t, k = h_bf.shape
    n = g_in.shape[1] if transposed else g_in.shape[2]

    def body(a_ref, b_ref, o_ref):
        o_ref[...] = _dot_nt(a_ref[...], b_ref[...]) if transposed else _dot_nn(a_ref[...], b_ref[...])

    if shard_major:
        out_spec = pl.BlockSpec((None, t, n), lambda j: (j, 0, 0))
        out_shape = jax.ShapeDtypeStruct((N_DEV, t, n), F32)
    else:
        out_spec = pl.BlockSpec((t, n), lambda j: (0, j))
        out_shape = jax.ShapeDtypeStruct((t, N_DEV * n), F32)
    return pl.pallas_call(
        body, name=name, grid=(N_DEV,),
        in_specs=[pl.BlockSpec((t, k), lambda j: (0, 0)),
                  pl.BlockSpec((None,) + g_in.shape[1:], lambda j: (j, 0, 0))],
        out_specs=out_spec, out_shape=out_shape,
        compiler_params=_params("parallel"),
    )(h_bf, g_in)


def _out_ln(y3, g_out, hin, g, b, *, n_valid, tm, name):
    nj, t, kj = y3.shape
    _, r, d = g_out.shape

    def body(y_ref, w_ref, hin_ref, g_ref, b_ref, s_ref, h_ref, hb_ref):
        w = w_ref[...].reshape(N_DEV * r, d)
        s = DN_ALPHA * hin_ref[...]
        for jj in range(nj):
            s = s + _dot_nn(y_ref[jj], w[jj * kj:(jj + 1) * kj])
        mu = jnp.mean(s, axis=-1, keepdims=True)
        xc = s - mu
        var = jnp.mean(xc * xc, axis=-1, keepdims=True)
        h = xc * lax.rsqrt(var + LN_EPS) * g_ref[...] + b_ref[...]
        s_ref[...] = s
        h_ref[...] = h
        rows = pl.program_id(0) * tm + lax.broadcasted_iota(jnp.int32, (tm, d), 0)
        hb_ref[...] = jnp.where(rows < n_valid, h, 0.0).astype(BF16)

    row = pl.BlockSpec((tm, d), lambda i: (i, 0))
    vec = pl.BlockSpec((1, d), lambda i: (0, 0))
    return pl.pallas_call(
        body, name=name, grid=(t // tm,),
        in_specs=[pl.BlockSpec((nj, tm, kj), lambda i: (0, i, 0)),
                  pl.BlockSpec((N_DEV, r, d), lambda i: (0, 0, 0)), row, vec, vec],
        out_specs=[row, row, row],
        out_shape=[jax.ShapeDtypeStruct((t, d), F32), jax.ShapeDtypeStruct((t, d), F32),
                   jax.ShapeDtypeStruct((t, d), BF16)],
        compiler_params=_params("parallel"),
    )(y3, g_out, hin, g, b)


def _out_bwd(ds_bf, g_out, *, tm, name):
    t, d = ds_bf.shape
    r = g_out.shape[1]

    def body(a_ref, w_ref, o_ref):
        o_ref[...] = _dot_nt(a_ref[...], w_ref[...].reshape(N_DEV * r, d))

    return pl.pallas_call(
        body, name=name, grid=(t // tm,),
        in_specs=[pl.BlockSpec((tm, d), lambda i: (i, 0)),
                  pl.BlockSpec((N_DEV, r, d), lambda i: (0, 0, 0))],
        out_specs=pl.BlockSpec((tm, N_DEV * r), lambda i: (i, 0)),
        out_shape=jax.ShapeDtypeStruct((t, N_DEV * r), F32),
        compiler_params=_params("parallel"),
    )(ds_bf, g_out)


def _in_bwd(dact, g_in, add, *, tm, name, alpha=DN_ALPHA, transposed=False):
    t = dact.shape[-2]
    _, k, n = g_in.shape
    if transposed:
        k, n = n, k
    halves = dact.shape[0] == 2 and dact.ndim == 3
    per = N_DEV // 2

    def body(a_ref, b_ref, add_ref, o_ref, acc_ref):
        j = pl.program_id(1)

        @pl.when(j == 0)
        def _():
            acc_ref[...] = alpha * add_ref[...]

        acc_ref[...] += _dot_nn(a_ref[...], b_ref[...]) if transposed else _dot_nt(a_ref[...], b_ref[...])

        @pl.when(j == N_DEV - 1)
        def _():
            o_ref[...] = acc_ref[...]

    if halves:
        a_spec = pl.BlockSpec((None, tm, n), lambda i, j: (j // per, i, j % per))
    elif dact.ndim == 4:
        a_spec = pl.BlockSpec((None, None, tm, n), lambda i, j: (j // per, j % per, i, 0))
    else:
        a_spec = pl.BlockSpec((None, tm, n), lambda i, j: (j, i, 0))
    return pl.pallas_call(
        body, name=name, grid=(t // tm, N_DEV),
        in_specs=[a_spec, pl.BlockSpec((None,) + g_in.shape[1:], lambda i, j: (j, 0, 0)),
                  pl.BlockSpec((tm, k), lambda i, j: (i, 0))],
        out_specs=pl.BlockSpec((tm, k), lambda i, j: (i, 0)),
        out_shape=jax.ShapeDtypeStruct((t, k), F32),
        scratch_shapes=[pltpu.VMEM((tm, k), F32)],
        compiler_params=_params("parallel", "arbitrary"),
    )(dact, g_in, add)


def _mm_nt_full(a, b, add, *, tm, name):
    t, n = a.shape
    k = b.shape[0]

    def body(a_ref, b_ref, add_ref, o_ref):
        o_ref[...] = add_ref[...] + _dot_nt(a_ref[...], b_ref[...])

    return pl.pallas_call(
        body, name=name, grid=(t // tm,),
        in_specs=[pl.BlockSpec((tm, n), lambda i: (i, 0)), pl.BlockSpec((k, n), lambda i: (0, 0)),
                  pl.BlockSpec((tm, k), lambda i: (i, 0))],
        out_specs=pl.BlockSpec((tm, k), lambda i: (i, 0)),
        out_shape=jax.ShapeDtypeStruct((t, k), F32),
        compiler_params=_params("parallel"),
    )(a, b, add)


def _w_in_grad(h_bf, dact, *, name, transposed=False):
    t, k = h_bf.shape
    halves = dact.shape[0] == 2 and dact.ndim == 3
    per = N_DEV // 2
    n = dact.shape[-1] // per if halves else dact.shape[-1]

    def body(a_ref, b_ref, o_ref):
        if transposed:
            o_ref[...] = _dot_tn(b_ref[...], a_ref[...]).astype(BF16)
        else:
            o_ref[...] = _dot_tn(a_ref[...], b_ref[...]).astype(BF16)

    if halves:
        b_spec = pl.BlockSpec((None, t, n), lambda j: (j // per, 0, j % per))
    elif dact.ndim == 4:
        b_spec = pl.BlockSpec((None, None, t, n), lambda j: (j // per, j % per, 0, 0))
    else:
        b_spec = pl.BlockSpec((None, t, n), lambda j: (j, 0, 0))
    return pl.pallas_call(
        body, name=name, grid=(N_DEV,),
        in_specs=[pl.BlockSpec((t, k), lambda j: (0, 0)), b_spec],
        out_specs=pl.BlockSpec((None, n, k) if transposed else (None, k, n), lambda j: (j, 0, 0)),
        out_shape=jax.ShapeDtypeStruct((N_DEV, n, k) if transposed else (N_DEV, k, n), BF16),
        compiler_params=_params("parallel"),
    )(h_bf, dact)


def _w_out_grad(y3, ds_bf, r, *, name):
    nj, t, kj = y3.shape
    d = ds_bf.shape[1]
    unit = r * LANES // math.gcd(r, LANES)
    ks = max([c for c in range(unit, min(kj, 768) + 1, unit) if kj % c == 0], default=kj)
    gsz = ks // r
    per = kj // ks

    def body(a_ref, b_ref, o_ref):
        o_ref[...] = _dot_tn(a_ref[...], b_ref[...]).reshape(gsz, r, d).astype(BF16)

    return pl.pallas_call(
        body, name=name, grid=(nj * per,),
        in_specs=[pl.BlockSpec((None, t, ks), lambda j: (j // per, 0, j % per)),
                  pl.BlockSpec((t, d), lambda j: (0, 0))],
        out_specs=pl.BlockSpec((gsz, r, d), lambda j: (j, 0, 0)),
        out_shape=jax.ShapeDtypeStruct((N_DEV, r, d), BF16),
        compiler_params=_params("parallel"),
    )(y3, ds_bf)


def _mm_tn_cols(a, b, *, tn, name):
    t, m = a.shape
    n = b.shape[1]

    def body(a_ref, b_ref, o_ref):
        o_ref[...] = _dot_tn(a_ref[...], b_ref[...])

    return pl.pallas_call(
        body, name=name, grid=(n // tn,),
        in_specs=[pl.BlockSpec((t, m), lambda j: (0, 0)), pl.BlockSpec((t, tn), lambda j: (0, j))],
        out_specs=pl.BlockSpec((m, tn), lambda j: (0, j)),
        out_shape=jax.ShapeDtypeStruct((m, n), F32),
        compiler_params=_params("parallel"),
    )(a, b)


def _ln_bwd(dout, s, g, *, tm, name):
    t, d = s.shape

    def body(do_ref, s_ref, g_ref, ds_ref, dsb_ref, gb_ref):
        i = pl.program_id(0)
        sv = s_ref[...]
        do = do_ref[...]
        mu = jnp.mean(sv, axis=-1, keepdims=True)
        xc = sv - mu
        var = jnp.mean(xc * xc, axis=-1, keepdims=True)
        rstd = lax.rsqrt(var + LN_EPS)
        xhat = xc * rstd
        dxhat = do * g_ref[...]
        m1 = jnp.mean(dxhat, axis=-1, keepdims=True)
        m2 = jnp.mean(dxhat * xhat, axis=-1, keepdims=True)
        ds = rstd * (dxhat - m1 - xhat * m2)
        ds_ref[...] = ds
        dsb_ref[...] = ds.astype(BF16)
        upd = _rows8([jnp.sum(do * xhat, axis=0, keepdims=True), jnp.sum(do, axis=0, keepdims=True)], d)

        @pl.when(i == 0)
        def _():
            gb_ref[...] = upd

        @pl.when(i > 0)
        def _():
            gb_ref[...] += upd

    row = pl.BlockSpec((tm, d), lambda i: (i, 0))
    return pl.pallas_call(
        body, name=name, grid=(t // tm,),
        in_specs=[row, row, pl.BlockSpec((1, d), lambda i: (0, 0))],
        out_specs=[row, row, pl.BlockSpec((8, d), lambda i: (0, 0))],
        out_shape=[jax.ShapeDtypeStruct((t, d), F32), jax.ShapeDtypeStruct((t, d), BF16),
                   jax.ShapeDtypeStruct((8, d), F32)],
        compiler_params=_params("arbitrary"),
    )(dout, s, g)


def _roll_down(x, s):
    return x if s == 0 else pltpu.roll(x, s, 0)


def _conv_taps(x, wb, width):
    y = jnp.broadcast_to(wb[width:width + 1, :], x.shape)
    for k in range(width):
        y = y + _roll_down(x, width - 1 - k) * wb[k:k + 1, :]
    return y


def _conv_taps_bwd(dy, x, wb, width):
    n = dy.shape[0]
    dx = jnp.zeros_like(dy)
    rows = []
    for k in range(width):
        s = width - 1 - k
        dy_up = dy if s == 0 else pltpu.roll(dy, n - s, 0)
        dx = dx + dy_up * wb[k:k + 1, :]
        rows.append(jnp.sum(dy_up * x, axis=0, keepdims=True))
    rows.append(jnp.sum(dy, axis=0, keepdims=True))
    t_idx = lax.broadcasted_iota(jnp.int32, dy.shape, 0)
    return jnp.where(t_idx < n - (width - 1), dx, 0.0), _rows8(rows, dy.shape[1])


def _convglu_fwd(z3, fwb3, *, name):
    _, t, n = z3.shape
    half = N_DEV // 2
    nc = pl.cdiv(n, LANES)

    def body(zg_ref, zv_ref, wg_ref, wv_ref, y_ref):
        gate = _conv_taps(zg_ref[...], wg_ref[...], 3)
        val = _conv_taps(zv_ref[...], wv_ref[...], 3)
        y_ref[...] = (_gelu(gate) * val).astype(BF16)

    zblk = lambda off: pl.BlockSpec((None, t, LANES), lambda j, c: (j + off, 0, c))
    wblk = lambda off: pl.BlockSpec((None, 8, LANES), lambda j, c: (j + off, 0, c))
    return pl.pallas_call(
        body, name=name, grid=(half, nc),
        in_specs=[zblk(0), zblk(half), wblk(0), wblk(half)],
        out_specs=zblk(0),
        out_shape=jax.ShapeDtypeStruct((half, t, n), BF16),
        compiler_params=_params("parallel", "parallel"),
    )(z3, z3, fwb3, fwb3)


def _ffn_bwd_mid(ds_bf, g_out, z3, fwb3, *, name):
    t, d = ds_bf.shape
    r = g_out.shape[1]
    n = z3.shape[2]
    half = N_DEV // 2
    nc = pl.cdiv(n, LANES)
    assert n == 2 * r

    def body(ds_ref, w_ref, zg_ref, zv_ref, wg_ref, wv_ref, dz_ref, dwb_ref, wsc_ref):
        c = pl.program_id(1)

        @pl.when(c == 0)
        def _():
            wsc_ref[0:r, :] = w_ref[0]
            wsc_ref[r:2 * r, :] = w_ref[1]
            if nc * LANES > n:
                wsc_ref[n:nc * LANES, :] = jnp.zeros((nc * LANES - n, d), BF16)

        w = wsc_ref[pl.ds(pl.multiple_of(c * LANES, LANES), LANES), :]
        dyf = _dot_nt(ds_ref[...], w)
        zg, zv = zg_ref[...], zv_ref[...]
        wg, wv = wg_ref[...], wv_ref[...]
        gate = _conv_taps(zg, wg, 3)
        val = _conv_taps(zv, wv, 3)
        gl, dgl = _gelu_and_grad(gate)
        dzg, dwg = _conv_taps_bwd(dyf * val * dgl, zg, wg, 3)
        dzv, dwv = _conv_taps_bwd(dyf * gl, zv, wv, 3)
        dz_ref[0] = dzg.astype(BF16)
        dz_ref[1] = dzv.astype(BF16)
        dwb_ref[0] = dwg
        dwb_ref[1] = dwv

    zblk = lambda off: pl.BlockSpec((None, t, LANES), lambda j, c: (j + off, 0, c))
    wblk = lambda off: pl.BlockSpec((None, 8, LANES), lambda j, c: (j + off, 0, c))
    return pl.pallas_call(
        body, name=name, grid=(half, nc),
        in_specs=[pl.BlockSpec((t, d), lambda j, c: (0, 0)),
                  pl.BlockSpec((2, r, d), lambda j, c: (j, 0, 0)),
                  zblk(0), zblk(half), wblk(0), wblk(half)],
        out_specs=[pl.BlockSpec((2, None, t, LANES), lambda j, c: (0, j, 0, c)),
                   pl.BlockSpec((2, None, 8, LANES), lambda j, c: (0, j, 0, c))],
        out_shape=[jax.ShapeDtypeStruct((2, half, t, n), BF16), jax.ShapeDtypeStruct((2, half, 8, n), F32)],
        scratch_shapes=[pltpu.VMEM((nc * LANES, d), BF16)],
        compiler_params=_params("parallel", "arbitrary"),
    )(ds_bf, g_out, z3, z3, fwb3, fwb3)


def _conv_a_fwd(gr, cwb, *, cb, name):
    t, r2 = gr.shape
    r = r2 // 2
    nb = r // cb

    def body(x_ref, w_ref, o_ref):
        o_ref[...] = _conv_taps(x_ref[...], w_ref[...], 4)

    return pl.pallas_call(
        body, name=name, grid=(nb,),
        in_specs=[pl.BlockSpec((t, cb), lambda j: (0, j + nb)), pl.BlockSpec((8, cb), lambda j: (0, j))],
        out_specs=pl.BlockSpec((t, cb), lambda j: (0, j)),
        out_shape=jax.ShapeDtypeStruct((t, r), F32),
        compiler_params=_params("parallel"),
    )(gr, cwb)


def _gates_fwd(rec, bd_r, bd_i, vecs, *, tm, name):
    t, r_dim = rec.shape
    nb, cb, _ = bd_r.shape

    def body(x_ref, wr_ref, wi_ref, v_ref, a_ref, u_ref, r_ref, i_ref):
        x = x_ref[...]
        xb = x.astype(BF16)
        v = v_ref[...]
        r = _sigmoid(_dot_nn(xb, wr_ref[...]) + v[0:1, :])
        i = _sigmoid(_dot_nn(xb, wi_ref[...]) + v[1:2, :])
        log_a = (-LRU_C) * r * _softplus(-v[2:3, :])
        a_ref[...] = jnp.exp(log_a)
        u_ref[...] = jnp.sqrt(-_expm1(2.0 * log_a)) * (i * x)
        r_ref[...] = r
        i_ref[...] = i

    blk = pl.BlockSpec((tm, cb), lambda j, i: (i, j))
    wspec = pl.BlockSpec((None, cb, cb), lambda j, i: (j, 0, 0))
    out = jax.ShapeDtypeStruct((t, r_dim), F32)
    return pl.pallas_call(
        body, name=name, grid=(nb, t // tm),
        in_specs=[blk, wspec, wspec, pl.BlockSpec((8, cb), lambda j, i: (0, j))],
        out_specs=[blk, blk, blk, blk],
        out_shape=[out, out, out, out],
        compiler_params=_params("parallel", "parallel"),
    )(rec, bd_r, bd_i, vecs)


def _scan_fwd(a, u, gr, *, cb, name):
    t, r = a.shape
    nb = r // cb
    seg = t // SUBLANES

    def body(a_ref, u_ref, g_ref, h_ref, y_ref, p_ref):
        def step(k, carry):
            h, p = carry
            rows = pl.ds(k, SUBLANES, stride=seg)
            av = a_ref[rows, :]
            h = av * h + u_ref[rows, :]
            p = av * p
            h_ref[rows, :] = h
            p_ref[rows, :] = p
            return h, p

        h_fin, p_fin = lax.fori_loop(0, seg, step, (jnp.zeros((SUBLANES, cb), F32), jnp.ones((SUBLANES, cb), F32)),
                                     unroll=4)
        carry = h_fin[0:1, :]
        for s in range(1, SUBLANES):
            rows = slice(s * seg, (s + 1) * seg)
            h_ref[rows, :] = h_ref[rows, :] + p_ref[rows, :] * carry
            carry = h_fin[s:s + 1, :] + p_fin[s:s + 1, :] * carry
        y_ref[...] = (_gelu(g_ref[...]) * h_ref[...]).astype(BF16)

    blk = pl.BlockSpec((t, cb), lambda j: (0, j))
    return pl.pallas_call(
        body, name=name, grid=(nb,),
        in_specs=[blk, blk, blk],
        out_specs=[blk, pl.BlockSpec((None, t, cb), lambda j: (0, 0, j))],
        out_shape=[jax.ShapeDtypeStruct((t, r), F32), jax.ShapeDtypeStruct((1, t, r), BF16)],
        scratch_shapes=[pltpu.VMEM((t, cb), F32)],
        compiler_params=_params("parallel"),
    )(a, u, gr)


def _scan_bwd(dy, gr, hr, a, *, cb, name):
    t, r = a.shape
    nb = r // cb
    seg = t // SUBLANES

    def body(dy_ref, g_ref, h_ref, a_ref, dh_ref, da_ref, dg_ref, q_ref):
        gl, dgl = _gelu_and_grad(g_ref[...])
        dyv = dy_ref[...]
        dh_ref[...] = dyv * gl
        dg_ref[...] = (dyv * h_ref[...] * dgl).astype(BF16)

        def step(k, carry):
            cin, q = carry
            rows = pl.ds(seg - 1 - k, SUBLANES, stride=seg)
            dh = dh_ref[rows, :] + cin
            dh_ref[rows, :] = dh
            q_ref[rows, :] = q
            av = a_ref[rows, :]
            return av * dh, av * q

        c_fin, q_fin = lax.fori_loop(0, seg, step, (jnp.zeros((SUBLANES, cb), F32), jnp.ones((SUBLANES, cb), F32)),
                                     unroll=4)
        carry = c_fin[SUBLANES - 1:SUBLANES, :]
        for s in range(SUBLANES - 2, -1, -1):
            rows = slice(s * seg, (s + 1) * seg)
            dh_ref[rows, :] = dh_ref[rows, :] + q_ref[rows, :] * carry
            carry = c_fin[s:s + 1, :] + q_fin[s:s + 1, :] * carry
        da_ref[...] = dh_ref[...] * _shift_down(h_ref[...], 1)

    blk = pl.BlockSpec((t, cb), lambda j: (0, j))
    return pl.pallas_call(
        body, name=name, grid=(nb,),
        in_specs=[blk, blk, blk, blk],
        out_specs=[blk, blk, blk],
        out_shape=[jax.ShapeDtypeStruct((t, r), F32), jax.ShapeDtypeStruct((t, r), F32),
                   jax.ShapeDtypeStruct((t, r), BF16)],
        scratch_shapes=[pltpu.VMEM((t, cb), F32)],
        compiler_params=_params("parallel"),
    )(dy, gr, hr, a)


def _gates_bwd(rec, r, i, a, dh, da, bd_r, bd_i, vecs, *, tm, name):
    t, r_dim = rec.shape
    nb, cb, _ = bd_r.shape

    def body(x_ref, r_ref, i_ref, a_ref, dh_ref, da_ref, wr_ref, wi_ref, v_ref, dx_ref, dpr_ref, dpi_ref, dv_ref):
        step = pl.program_id(1)
        x, r, i, a, dh, da = x_ref[...], r_ref[...], i_ref[...], a_ref[...], dh_ref[...], da_ref[...]
        lam = v_ref[...][2:3, :]
        sp = _softplus(-lam)
        a2 = a * a
        mult = jnp.sqrt(-_expm1(2.0 * (-LRU_C) * r * sp))
        d_i = dh * mult * x
        d_log_a = da * a - (dh * i * x) * a2 / mult
        d_r = d_log_a * ((-LRU_C) * sp)
        d_sp = jnp.sum(d_log_a * ((-LRU_C) * r), axis=0, keepdims=True)
        d_pre_r = d_r * r * (1.0 - r)
        d_pre_i = d_i * i * (1.0 - i)
        dprb = d_pre_r.astype(BF16)
        dpib = d_pre_i.astype(BF16)
        dx_ref[...] = dh * mult * i + _dot_nt(dprb, wr_ref[...]) + _dot_nt(dpib, wi_ref[...])
        dpr_ref[...] = dprb
        dpi_ref[...] = dpib
        upd = _rows8([jnp.sum(d_pre_r, axis=0, keepdims=True), jnp.sum(d_pre_i, axis=0, keepdims=True),
                      -d_sp * _sigmoid(-lam)], cb)

        @pl.when(step == 0)
        def _():
            dv_ref[...] = upd

        @pl.when(step > 0)
        def _():
            dv_ref[...] += upd

    blk = pl.BlockSpec((tm, cb), lambda j, i: (i, j))
    wspec = pl.BlockSpec((None, cb, cb), lambda j, i: (j, 0, 0))
    vspec = pl.BlockSpec((8, cb), lambda j, i: (0, j))
    return pl.pallas_call(
        body, name=name, grid=(nb, t // tm),
        in_specs=[blk] * 6 + [wspec, wspec, vspec],
        out_specs=[blk, blk, blk, vspec],
        out_shape=[jax.ShapeDtypeStruct((t, r_dim), F32), jax.ShapeDtypeStruct((t, r_dim), BF16),
                   jax.ShapeDtypeStruct((t, r_dim), BF16), jax.ShapeDtypeStruct((8, r_dim), F32)],
        compiler_params=_params("parallel", "arbitrary"),
    )(rec, r, i, a, dh, da, bd_r, bd_i, vecs)


def _bd_grad(rec, dpr, dpi, *, cb, name):
    t, r = rec.shape
    nb = r // cb

    def body(x_ref, dr_ref, di_ref, gr_ref, gi_ref):
        xb = x_ref[...].astype(BF16)
        gr_ref[...] = _dot_tn(xb, dr_ref[...])
        gi_ref[...] = _dot_tn(xb, di_ref[...])

    blk = pl.BlockSpec((t, cb), lambda j: (0, j))
    wspec = pl.BlockSpec((None, cb, cb), lambda j: (j, 0, 0))
    out = jax.ShapeDtypeStruct((nb, cb, cb), F32)
    return pl.pallas_call(
        body, name=name, grid=(nb,),
        in_specs=[blk, blk, blk], out_specs=[wspec, wspec], out_shape=[out, out],
        compiler_params=_params("parallel"),
    )(rec, dpr, dpi)


def _conv_a_bwd(d_rec, gr, dgate, cwb, *, cb, name):
    t, r = d_rec.shape
    nb = r // cb

    def body(dy_ref, x_ref, dg_ref, w_ref, dact_ref, dw_ref):
        dx, dw = _conv_taps_bwd(dy_ref[...], x_ref[...], w_ref[...], 4)
        dact_ref[0] = dg_ref[...]
        dact_ref[1] = dx.astype(BF16)
        dw_ref[...] = dw

    blk = pl.BlockSpec((t, cb), lambda j: (0, j))
    vspec = pl.BlockSpec((8, cb), lambda j: (0, j))
    return pl.pallas_call(
        body, name=name, grid=(nb,),
        in_specs=[blk, pl.BlockSpec((t, cb), lambda j: (0, j + nb)), blk, vspec],
        out_specs=[pl.BlockSpec((2, t, cb), lambda j: (0, 0, j)), vspec],
        out_shape=[jax.ShapeDtypeStruct((2, t, r), BF16), jax.ShapeDtypeStruct((8, r), F32)],
        compiler_params=_params("parallel"),
    )(d_rec, gr, dgate, cwb)


def _split3(x):
    p0 = x.astype(BF16)
    r1 = x - p0.astype(F32)
    p1 = r1.astype(BF16)
    p2 = (r1 - p1.astype(F32)).astype(BF16)
    return p0, p1, p2


def _fgate_fwd(fp, fb, *, tq, name):
    t = fp.shape[0]

    def body(f_ref, b_ref, c_ref, ct_ref):
        logf = -_softplus(-(f_ref[...] + b_ref[...]))
        rows = pl.program_id(0) * tq + lax.broadcasted_iota(jnp.int32, (tq, t), 0)
        cols = lax.broadcasted_iota(jnp.int32, (tq, t), 1)
        tri = (cols <= rows).astype(BF16)
        p0, p1, p2 = _split3(logf)
        c = _dot_nn(tri, p0) + _dot_nn(tri, p1) + _dot_nn(tri, p2)
        c_ref[...] = c
        ct_ref[...] = c.T

    return pl.pallas_call(
        body, name=name, grid=(t // tq,),
        in_specs=[pl.BlockSpec((t, LANES), lambda i: (0, 0)), pl.BlockSpec((1, LANES), lambda i: (0, 0))],
        out_specs=[pl.BlockSpec((tq, LANES), lambda i: (i, 0)), pl.BlockSpec((LANES, tq), lambda i: (0, i))],
        out_shape=[jax.ShapeDtypeStruct((t, LANES), F32), jax.ShapeDtypeStruct((LANES, t), F32)],
        compiler_params=_params("parallel"),
    )(fp, fb)


def _fgate_bwd(dct, fp, fb, *, tq, name):
    t = fp.shape[0]

    def body(d_ref, f_ref, b_ref, o_ref, db_ref):
        i = pl.program_id(0)
        rows = lax.broadcasted_iota(jnp.int32, (t, tq), 0)
        cols = i * tq + lax.broadcasted_iota(jnp.int32, (t, tq), 1)
        tri = (rows >= cols).astype(BF16)
        p0, p1, p2 = _split3(d_ref[...])
        dlogf = (_dot_nn(p0, tri) + _dot_nn(p1, tri) + _dot_nn(p2, tri)).T
        df = dlogf * _sigmoid(-(f_ref[...] + b_ref[...]))
        o_ref[...] = df.astype(BF16)
        upd = _rows8([jnp.sum(df, axis=0, keepdims=True)], LANES)

        @pl.when(i == 0)
        def _():
            db_ref[...] = upd

        @pl.when(i > 0)
        def _():
            db_ref[...] += upd

    return pl.pallas_call(
        body, name=name, grid=(t // tq,),
        in_specs=[pl.BlockSpec((LANES, t), lambda i: (0, 0)), pl.BlockSpec((tq, LANES), lambda i: (i, 0)),
                  pl.BlockSpec((1, LANES), lambda i: (0, 0))],
        out_specs=[pl.BlockSpec((tq, LANES), lambda i: (i, 0)), pl.BlockSpec((8, LANES), lambda i: (0, 0))],
        out_shape=[jax.ShapeDtypeStruct((t, LANES), BF16), jax.ShapeDtypeStruct((8, LANES), F32)],
        compiler_params=_params("arbitrary"),
    )(dct, fp, fb)


def _pair_sum(a, b, *, tm, name):
    t, d = a.shape

    def body(a_ref, b_ref, o_ref):
        o_ref[...] = (a_ref[...] + b_ref[...]).astype(BF16)

    row = pl.BlockSpec((tm, d), lambda i: (i, 0))
    return pl.pallas_call(
        body, name=name, grid=(t // tm,), in_specs=[row, row], out_specs=row,
        out_shape=jax.ShapeDtypeStruct((t, d), BF16), compiler_params=_params("parallel"),
    )(a, b)


FWD_HEAD_TILES = 2
BWD_HEAD_TILES = 1


def _head_block_width(dh, tiles):
    return tiles * LANES if tiles * LANES // dh <= 8 else LANES


def _head_masks(dh, bw):
    lane = lax.broadcasted_iota(jnp.int32, (1, bw), 1)
    return [((lane >= e * dh) & (lane < (e + 1) * dh)) for e in range(bw // dh)]


def _head_c_row(ct_blk, head):
    sub = lax.broadcasted_iota(jnp.int32, ct_blk.shape, 0)
    return jnp.sum(jnp.where(sub == head, ct_blk, 0.0), axis=0, keepdims=True)


def _attn_weights(qm, k, c_row, q0):
    tq, t = qm.shape[0], k.shape[0]
    s = _dot_nt(qm, k) - c_row
    qi = q0 + lax.broadcasted_iota(jnp.int32, (tq, t), 0)
    ki = lax.broadcasted_iota(jnp.int32, (tq, t), 1)
    s = jnp.where(ki <= qi, s, -jnp.inf)
    e = jnp.exp(s - jnp.max(s, axis=-1, keepdims=True))
    return e, 1.0 / jnp.sum(e, axis=-1, keepdims=True)


def _key_buckets(t, tq):
    return tuple(range(tq, t, tq)) + (t,)


def _for_prefix(needed, buckets, fn):
    prev = 0
    for length in buckets:
        pl.when((needed > prev) & (needed <= length))(lambda length=length: fn(length))
        prev = length


def _attn_fwd(qg, kv, ct, *, tq, name):
    t, d2 = qg.shape
    d = d2 // 2
    dh = d // N_HEADS
    bw = _head_block_width(dh, FWD_HEAD_TILES)
    hpb = bw // dh
    nhb = d // bw
    scale = dh ** -0.5
    buckets = _key_buckets(t, tq)

    def body(q_ref, og_ref, k_ref, v_ref, ct_ref, o_ref, y_ref):
        hb = pl.program_id(0)
        q0 = pl.program_id(1) * tq

        def run(length):
            qs = q_ref[...] * scale
            k = k_ref[0:length, :]
            v = v_ref[0:length, :]
            o = jnp.zeros((tq, bw), F32)
            for e, msk in enumerate(_head_masks(dh, bw)):
                c_row = _head_c_row(ct_ref[:, 0:length], hb * hpb + e)
                w, inv = _attn_weights(jnp.where(msk, qs, 0.0).astype(BF16), k, c_row, q0)
                o = o + _dot_nn(w.astype(BF16), jnp.where(msk, v, jnp.zeros_like(v))) * inv
            o_ref[...] = o
            y_ref[...] = (o * _sigmoid(og_ref[...])).astype(BF16)

        _for_prefix(q0 + tq, buckets, run)

    qblk = pl.BlockSpec((tq, bw), lambda h, i: (i, h))
    return pl.pallas_call(
        body, name=name, grid=(nhb, t // tq),
        in_specs=[qblk, pl.BlockSpec((tq, bw), lambda h, i: (i, h + nhb)),
                  pl.BlockSpec((t, bw), lambda h, i: (0, h)), pl.BlockSpec((t, bw), lambda h, i: (0, h + nhb)),
                  pl.BlockSpec((N_HEADS, t), lambda h, i: (0, 0))],
        out_specs=[qblk, pl.BlockSpec((None, tq, bw), lambda h, i: (0, i, h))],
        out_shape=[jax.ShapeDtypeStruct((t, d), F32), jax.ShapeDtypeStruct((1, t, d), BF16)],
        compiler_params=_params("parallel", "parallel"),
    )(qg, qg, kv, kv, ct)


def _attn_bwd(dy, qg, o, kv, ct, *, tq, name):
    t, d2 = qg.shape
    d = d2 // 2
    dh = d // N_HEADS
    bw = _head_block_width(dh, BWD_HEAD_TILES)
    hpb = bw // dh
    nhb = d // bw
    scale = dh ** -0.5
    buckets = _key_buckets(t, tq)

    def body(dy_ref, q_ref, og_ref, o_ref, k_ref, v_ref, ct_ref, dqg_ref, dk_ref, dv_ref, dc_ref):
        hb = pl.program_id(0)
        step = pl.program_id(1)
        q0 = step * tq

        @pl.when(step == 0)
        def _():
            dk_ref[...] = jnp.zeros((t, bw), F32)
            dv_ref[...] = jnp.zeros((t, bw), F32)
            dc_ref[...] = jnp.zeros((8, t), F32)

        def run(length):
            qs = q_ref[...] * scale
            k = k_ref[0:length, :]
            v = v_ref[0:length, :]
            sg = _sigmoid(og_ref[...])
            dyv = dy_ref[...]
            do = dyv * sg
            dqg_ref[1] = (dyv * o_ref[...] * sg * (1.0 - sg)).astype(BF16)
            dq = jnp.zeros((tq, bw), F32)
            dk = jnp.zeros((length, bw), F32)
            dv = jnp.zeros((length, bw), F32)
            dc_rows = []
            for e, msk in enumerate(_head_masks(dh, bw)):
                c_row = _head_c_row(ct_ref[:, 0:length], hb * hpb + e)
                qm = jnp.where(msk, qs, 0.0).astype(BF16)
                dom = jnp.where(msk, do, 0.0).astype(BF16)
                w, inv = _attn_weights(qm, k, c_row, q0)
                p = w * inv
                dp = _dot_nt(dom, v)
                dsc = p * (dp - jnp.sum(p * dp, axis=-1, keepdims=True))
                dsb = dsc.astype(BF16)
                dq = dq + _dot_nn(dsb, jnp.where(msk, k, jnp.zeros_like(k)))
                dk = dk + _dot_tn(dsb, qm)
                dv = dv + _dot_tn(p.astype(BF16), dom)
                dc_rows.append(-jnp.sum(dsc, axis=0, keepdims=True))
            dqg_ref[0] = (dq * scale).astype(BF16)
            dk_ref[0:length, :] += dk
            dv_ref[0:length, :] += dv
            dc_ref[:, 0:length] += _rows8(dc_rows, length)

        _for_prefix(q0 + tq, buckets, run)

    qblk = pl.BlockSpec((tq, bw), lambda h, i: (i, h))
    kblk = pl.BlockSpec((t, bw), lambda h, i: (0, h))
    return pl.pallas_call(
        body, name=name, grid=(nhb, t // tq),
        in_specs=[qblk, qblk, pl.BlockSpec((tq, bw), lambda h, i: (i, h + nhb)), qblk,
                  kblk, pl.BlockSpec((t, bw), lambda h, i: (0, h + nhb)),
                  pl.BlockSpec((N_HEADS, t), lambda h, i: (0, 0))],
        out_specs=[pl.BlockSpec((2, tq, bw), lambda h, i: (0, i, h)), kblk, kblk,
                   pl.BlockSpec((None, 8, t), lambda h, i: (h, 0, 0))],
        out_shape=[jax.ShapeDtypeStruct((2, t, d), BF16), jax.ShapeDtypeStruct((t, d), F32),
                   jax.ShapeDtypeStruct((t, d), F32), jax.ShapeDtypeStruct((nhb, 8, t), F32)],
        compiler_params=_params("parallel", "arbitrary"),
    )(dy, qg, qg, o, kv, kv, ct)


def _loss_bwd(h, tgt, *, lo, hi, tm, name):
    t, d = h.shape

    def body(h_ref, t_ref, l_ref, dy_ref):
        i = pl.program_id(0)
        rows = i * tm + lax.broadcasted_iota(jnp.int32, (tm, d), 0)
        err = jnp.where((rows >= lo) & (rows < hi), h_ref[...] - t_ref[...], 0.0)
        dy_ref[...] = err * (1.0 / d)
        part = jnp.sum(jnp.sum(err * err, axis=0, keepdims=True), axis=1, keepdims=True) * (0.5 / d)
        upd = jnp.broadcast_to(part, (8, LANES))

        @pl.when(i == 0)
        def _():
            l_ref[...] = upd

        @pl.when(i > 0)
        def _():
            l_ref[...] += upd

    row = pl.BlockSpec((tm, d), lambda i: (i, 0))
    return pl.pallas_call(
        body, name=name, grid=(t // tm,),
        in_specs=[row, row],
        out_specs=[pl.BlockSpec((8, LANES), lambda i: (0, 0)), row],
        out_shape=[jax.ShapeDtypeStruct((8, LANES), F32), jax.ShapeDtypeStruct((t, d), F32)],
        compiler_params=_params("arbitrary"),
    )(h, tgt)


def _adamw_math(w, gv, m, v):
    bc1 = 1.0 / (1.0 - ADAM_B1 ** ADAM_STEP)
    bc2 = 1.0 / (1.0 - ADAM_B2 ** ADAM_STEP)
    nm = ADAM_B1 * m + (1.0 - ADAM_B1) * gv
    nv = ADAM_B2 * v + (1.0 - ADAM_B2) * (gv * gv)
    delta = (-ADAM_LR) * ((nm * bc1) / (jnp.sqrt(nv * bc2) + ADAM_EPS) + ADAM_WD * w)
    return delta, nm, nv


def _adamw(w, g, m, v, *, name):
    r, c = w.shape
    tr = r
    for cand in (512, 256, 128, 64, 32, 16, 8):
        if r % cand == 0 and r > cand:
            tr = cand
            break

    def body(w_ref, g_ref, m_ref, v_ref, d_ref, nm_ref, nv_ref):
        d_ref[...], nm_ref[...], nv_ref[...] = _adamw_math(w_ref[...], g_ref[...], m_ref[...], v_ref[...])

    blk = pl.BlockSpec((tr, c), lambda i: (i, 0))
    out = jax.ShapeDtypeStruct((r, c), F32)
    return pl.pallas_call(
        body, name=name, grid=(r // tr,),
        in_specs=[blk] * 4, out_specs=[blk] * 3, out_shape=[out] * 3,
        compiler_params=_params("parallel"),
    )(w, g, m, v)


def _sum_adamw(recvs, sends, me, w, m, v, *, name):
    n_l = len(recvs)
    _, r, c = recvs[0].shape
    tr = _tile(r, (256, 192, 176, 128, 96, 64, 48, 32, 16))

    def body(me_ref, *refs):
        p_refs, own_refs = refs[:n_l], refs[n_l:2 * n_l]
        w_ref, m_ref, v_ref, g_ref, d_ref, nm_ref, nv_ref, acc_ref = refs[2 * n_l:]
        layer = pl.program_id(0)
        mine = me_ref[0]
        for k in range(n_l):
            @pl.when(layer == k)
            def _(k=k):
                acc_ref[...] = jnp.zeros((tr, c), F32)
                for dev in range(N_DEV):
                    @pl.when(mine == dev)
                    def _():
                        acc_ref[...] += own_refs[k][...].astype(F32)

                    @pl.when(mine != dev)
                    def _(dev=dev):
                        acc_ref[...] += p_refs[k][dev].astype(F32)
                acc = acc_ref[...]
                g_ref[...] = acc
                d_ref[...], nm_ref[...], nv_ref[...] = _adamw_math(w_ref[...], acc, m_ref[...], v_ref[...])

    p_specs = [pl.BlockSpec((N_DEV, tr, c), lambda l, i, me_ref, k=k: (0, jnp.where(l == k, i, 0), 0))
               for k in range(n_l)]
    own_specs = [pl.BlockSpec((None, tr, c), lambda l, i, me_ref, k=k: (me_ref[0], jnp.where(l == k, i, 0), 0))
                 for k in range(n_l)]
    blk = pl.BlockSpec((None, tr, c), lambda l, i, me_ref: (l, i, 0))
    out = jax.ShapeDtypeStruct((n_l, r, c), F32)
    return pl.pallas_call(
        body, name=name,
        grid_spec=pltpu.PrefetchScalarGridSpec(
            num_scalar_prefetch=1, grid=(n_l, r // tr),
            in_specs=p_specs + own_specs + [blk] * 3, out_specs=[blk] * 4,
            scratch_shapes=[pltpu.VMEM((tr, c), F32)]),
        out_shape=[out] * 4,
        compiler_params=_params("arbitrary", "arbitrary"),
    )(me, *recvs, *sends, w, m, v)


def _sum8(parts, *, name):
    _, r, c = parts.shape
    tr = r
    for cand in (512, 256, 128, 64, 32, 16):
        if r % cand == 0 and r > cand:
            tr = cand
            break

    def body(p_ref, o_ref):
        acc = p_ref[0].astype(F32)
        for k in range(1, N_DEV):
            acc = acc + p_ref[k].astype(F32)
        o_ref[...] = acc

    return pl.pallas_call(
        body, name=name, grid=(r // tr,),
        in_specs=[pl.BlockSpec((N_DEV, tr, c), lambda i: (0, i, 0))],
        out_specs=pl.BlockSpec((tr, c), lambda i: (i, 0)),
        out_shape=jax.ShapeDtypeStruct((r, c), F32),
        compiler_params=_params("parallel"),
    )(parts)


def _my_index():
    return 4 * lax.axis_index("x") + 2 * lax.axis_index("y") + lax.axis_index("c")


def _peer(k):
    x, y, c = lax.axis_index("x"), lax.axis_index("y"), lax.axis_index("c")
    px = x ^ ((k >> 2) & 1)
    py = y ^ ((k >> 1) & 1)
    pc = c ^ (k & 1)
    return (px, py, pc), 4 * px + 2 * py + pc


def _all_gather(shards, *, name):
    n_arr = len(shards)

    def body(*refs):
        ins, outs = refs[:n_arr], refs[n_arr:2 * n_arr]
        send_sems, recv_sems, local_sems = refs[2 * n_arr:]
        me = _my_index()
        local = [pltpu.make_async_copy(ins[n], outs[n].at[me], local_sems.at[n]) for n in range(n_arr)]
        for cp in local:
            cp.start()
        sends = []
        for k in range(1, N_DEV):
            peer, _ = _peer(k)
            for n in range(n_arr):
                cp = pltpu.make_async_remote_copy(
                    src_ref=ins[n], dst_ref=outs[n].at[me], send_sem=send_sems.at[n, k - 1],
                    recv_sem=recv_sems.at[n, k - 1], device_id=peer, device_id_type=pl.DeviceIdType.MESH)
                cp.start()
                sends.append(cp)
        for k in range(1, N_DEV):
            peer, pidx = _peer(k)
            for n in range(n_arr):
                pltpu.make_async_remote_copy(
                    src_ref=ins[n], dst_ref=outs[n].at[pidx], send_sem=send_sems.at[n, k - 1],
                    recv_sem=recv_sems.at[n, k - 1], device_id=peer, device_id_type=pl.DeviceIdType.MESH).wait_recv()
        for cp in sends:
            cp.wait_send()
        for cp in local:
            cp.wait()

    hbm = pl.BlockSpec(memory_space=pl.ANY)
    return pl.pallas_call(
        body, name=name,
        in_specs=[hbm] * n_arr, out_specs=[hbm] * n_arr,
        out_shape=[jax.ShapeDtypeStruct((N_DEV,) + s.shape, s.dtype) for s in shards],
        scratch_shapes=[pltpu.SemaphoreType.DMA((n_arr, N_DEV - 1)), pltpu.SemaphoreType.DMA((n_arr, N_DEV - 1)),
                        pltpu.SemaphoreType.DMA((n_arr,))],
        compiler_params=pltpu.CompilerParams(has_side_effects=True),
    )(*shards)


_HBM = pl.BlockSpec(memory_space=pltpu.HBM)
_SEM = pl.BlockSpec(memory_space=pltpu.SEMAPHORE)
_EFFECT = pltpu.SideEffectType.DATAFLOW_SIDE_EFFECTING


def _remote(src, dst, send_sem, recv_sem, peer):
    return pltpu.make_async_remote_copy(src_ref=src, dst_ref=dst, send_sem=send_sem, recv_sem=recv_sem,
                                        device_id=peer, device_id_type=pl.DeviceIdType.MESH)


def _place_own(src, layer, me, *, out_dtype, name):
    _, r, c = src.shape
    tr = _tile(r, (256, 192, 176, 128, 96, 64, 48, 32, 16))

    def body(me_ref, s_ref, o_ref):
        o_ref[...] = s_ref[...].astype(out_dtype)

    return pl.pallas_call(
        body, name=name,
        grid_spec=pltpu.PrefetchScalarGridSpec(
            num_scalar_prefetch=1, grid=(r // tr,),
            in_specs=[pl.BlockSpec((None, tr, c), lambda i, me_ref: (layer, i, 0))],
            out_specs=pl.BlockSpec((None, tr, c), lambda i, me_ref: (me_ref[0], i, 0))),
        out_shape=jax.ShapeDtypeStruct((N_DEV, r, c), out_dtype),
        compiler_params=_params("parallel"),
    )(me, src)


def _own_blocks(srcs, *, name):
    n = len(srcs)

    def body(*refs):
        ins, outs, sems = refs[:n], refs[n:2 * n], refs[2 * n]
        me = _my_index()
        cps = [pltpu.make_async_copy(ins[t].at[me], outs[t].at[me], sems.at[t]) for t in range(n)]
        for cp in cps:
            cp.start()
        for cp in cps:
            cp.wait()

    return pl.pallas_call(
        body, name=name, in_specs=[_HBM] * n, out_specs=[_HBM] * n,
        out_shape=[jax.ShapeDtypeStruct(s.shape, s.dtype) for s in srcs],
        scratch_shapes=[pltpu.SemaphoreType.DMA((n,))],
    )(*srcs)


def _split_start(groups, *, scatter, name):
    sizes = [len(srcs) for srcs, _ in groups]
    flat_src = [s for srcs, _ in groups for s in srcs]
    flat_land = [l for _, lands in groups for l in lands]
    n, n_g = len(flat_land), len(groups)
    if not scatter:
        flat_src = []
    n_in = len(flat_src) + n

    def body(*refs):
        lands = refs[n_in - n:n_in]
        ins = refs[:n] if scatter else lands
        sems = refs[n_in:n_in + 2 * n_g]
        token = refs[-1]
        me = _my_index()
        t = 0
        for g in range(n_g):
            for q in range(sizes[g]):
                for k in range(1, N_DEV):
                    peer, pidx = _peer(k)
                    src = ins[t].at[pidx] if scatter else ins[t].at[me]
                    slot = q * (N_DEV - 1) + k - 1
                    _remote(src, lands[t].at[me], sems[2 * g].at[slot], sems[2 * g + 1].at[slot], peer).start()
                t += 1
        token[...] = jnp.zeros_like(token)

    sem_shapes = []
    for sz in sizes:
        sem_shapes += [pltpu.SemaphoreType.DMA((sz * (N_DEV - 1),)), pltpu.SemaphoreType.DMA((sz * (N_DEV - 1),))]
    outs = pl.pallas_call(
        body, name=name,
        in_specs=[_HBM] * n_in,
        out_specs=[_SEM] * (2 * n_g) + [_HBM] * n_in + [pl.BlockSpec(memory_space=pltpu.VMEM)],
        out_shape=sem_shapes + [pltpu.HBM(a.shape, a.dtype) for a in flat_src + flat_land]
        + [jax.ShapeDtypeStruct((8, LANES), F32)],
        input_output_aliases={i: 2 * n_g + i for i in range(n_in)},
        compiler_params=pltpu.CompilerParams(has_side_effects=_EFFECT),
    )(*[pltpu.with_memory_space_constraint(a, pltpu.HBM) for a in flat_src + flat_land])
    sems, thru, token = outs[:2 * n_g], outs[2 * n_g:2 * n_g + n_in], outs[-1]
    handles, pos = [], 0
    for g, sz in enumerate(sizes):
        lands_g = thru[n_in - n + pos:n_in - n + pos + sz]
        handles.append((sems[2 * g], sems[2 * g + 1], thru[pos:pos + sz] if scatter else [], lands_g))
        pos += sz
    return handles, token


def _split_wait(handle, after, *, scatter, name):
    send_sems, recv_sems, srcs, lands = handle
    n, n_src = len(lands), len(srcs)

    def body(*refs):
        lnd = refs[n_src:n_src + n]
        ins = refs[:n_src] if scatter else lnd
        ssem, rsem = refs[n_src + n], refs[n_src + n + 1]
        me = _my_index()
        for t in range(n):
            for k in range(1, N_DEV):
                peer, pidx = _peer(k)
                block = ins[t].at[me]
                slot = t * (N_DEV - 1) + k - 1
                _remote(block, lnd[t].at[me], ssem.at[slot], rsem.at[slot], peer).wait_send()
                _remote(block, lnd[t].at[pidx], ssem.at[slot], rsem.at[slot], peer).wait_recv()

    return pl.pallas_call(
        body, name=name,
        in_specs=[_HBM] * (n_src + n) + [_SEM, _SEM, pl.BlockSpec(memory_space=pl.ANY)],
        out_specs=[_HBM] * n,
        out_shape=[pltpu.HBM(l.shape, l.dtype) for l in lands],
        input_output_aliases={n_src + t: t for t in range(n)},
        compiler_params=pltpu.CompilerParams(has_side_effects=_EFFECT),
    )(*srcs, *lands, send_sems, recv_sems, after)


def _pack(arrs, dtype, row_quantum=16):
    flat = jnp.concatenate([a.astype(dtype).reshape(-1) for a in arrs])
    pad = (-flat.shape[0]) % (row_quantum * PACK_COLS)
    if pad:
        flat = jnp.concatenate([flat, jnp.zeros((pad,), dtype)])
    return flat.reshape(-1, PACK_COLS)


def _pack8(arrs, dtype):
    flat = jnp.concatenate([a.astype(dtype).reshape(N_DEV, -1) for a in arrs], axis=1)
    pad = (-flat.shape[1]) % (16 * PACK_COLS)
    if pad:
        flat = jnp.concatenate([flat, jnp.zeros((N_DEV, pad), dtype)], axis=1)
    return flat.reshape(N_DEV, -1, PACK_COLS)


def _unpack(slab, shapes, lead):
    lead_shape = slab.shape[:lead]
    flat = slab.reshape(lead_shape + (-1,))
    outs, off = [], 0
    for shp in shapes:
        size = math.prod(shp)
        outs.append(flat[..., off:off + size].reshape(lead_shape + tuple(shp)))
        off += size
    return outs


def _cols_full(g):
    g = jnp.moveaxis(g, 0, -2)
    return g.reshape(g.shape[:-2] + (g.shape[-2] * g.shape[-1],))


def _cols_split(full):
    n = full.shape[-1] // N_DEV
    return jnp.moveaxis(full.reshape(full.shape[:-1] + (N_DEV, n)), -2, 0)


def _block_diag(w, per):
    n, b, _ = w.shape
    w4 = w.reshape(n // per, per, b, b)
    eye = jnp.eye(per, dtype=w.dtype)
    return jnp.einsum('gpab,pq->gpaqb', w4, eye).reshape(n // per, per * b, per * b)


def _block_diag_extract(g, per):
    gn, cb, _ = g.shape
    b = cb // per
    g5 = g.reshape(gn, per, b, per, b)
    return jnp.stack([g5[:, p, :, p, :] for p in range(per)], axis=1).reshape(gn * per, b, b)


def _slab2d(a):
    return a.reshape(-1, a.shape[-1])


def _lru_block_cols(r_dim):
    lru = r_dim // N_LRU_BLOCKS
    return lru * LANES // math.gcd(lru, LANES)


BIG = ("a_w_in", "a_w_out", "b_w_in", "b_w_out", "f_w_in", "f_w_out")
COL_F32 = ("meta", "a_conv_w", "a_conv_b", "a_b_r", "a_b_i", "a_lambda", "f_conv_w")
REPLICATED = ("a_w_r", "a_w_i", "kv_f_b", "f_conv_b", "ln1_g", "ln1_b", "ln2_g", "ln2_b")
WEIGHT_NAMES = ("meta", "a_w_in", "a_conv_w", "a_conv_b", "a_w_r", "a_b_r", "a_w_i", "a_b_i", "a_lambda", "a_w_out",
                "kv_w", "kv_f_b", "b_w_in", "b_w_out", "f_w_in", "f_conv_w", "f_conv_b", "f_w_out",
                "ln1_g", "ln1_b", "ln2_g", "ln2_b")


def _kv_layout(kv_gathered, d):
    kv_full = _cols_full(kv_gathered)
    kv_pad = 2 * d + LANES - kv_full.shape[1]
    return jnp.concatenate([kv_full, jnp.zeros((d, kv_pad), kv_full.dtype)], axis=1)


def _small_layouts(small):
    r_dim = small["a_lambda"].shape[1]
    n_f = small["f_conv_b"].shape[1] // N_DEV
    cb = _lru_block_cols(r_dim)
    per = cb // (r_dim // N_LRU_BLOCKS)
    n_a = small["a_lambda"].shape[0]
    f_conv_w3 = small["f_conv_w"].reshape(N_LAYERS, 3, N_DEV, n_f).transpose(0, 2, 1, 3)
    f_conv_b3 = small["f_conv_b"].reshape(N_LAYERS, N_DEV, 1, n_f)
    return {
        "kv_fb": jnp.concatenate([small["kv_f_b"], jnp.zeros((LANES - N_HEADS,), F32)])[None],
        "a_cwb": jnp.concatenate([small["a_conv_w"], small["a_conv_b"][:, None],
                                  jnp.zeros((n_a, 3, r_dim), F32)], axis=1),
        "a_vecs": jnp.concatenate([jnp.stack([small["a_b_r"], small["a_b_i"], small["a_lambda"]], axis=1),
                                   jnp.zeros((n_a, 5, r_dim), F32)], axis=1),
        "a_bd_r": jnp.stack([_block_diag(small["a_w_r"][l], per) for l in range(n_a)]).astype(BF16),
        "a_bd_i": jnp.stack([_block_diag(small["a_w_i"][l], per) for l in range(n_a)]).astype(BF16),
        "f_cwb3": jnp.concatenate([f_conv_w3, f_conv_b3, jnp.zeros((N_LAYERS, N_DEV, 4, n_f), F32)], axis=2),
        "ln1_g": small["ln1_g"][:, None], "ln1_b": small["ln1_b"][:, None],
        "ln2_g": small["ln2_g"][:, None], "ln2_b": small["ln2_b"][:, None],
    }


def _local_step(h0, tgt, n_meta, n_tok, wts, hooks):
    tp, d = h0.shape
    tm = tp // 8 if (tp // 8) % 16 == 0 else tp
    tmb = _tile(tp, (1088, 512, 320, 256, 128))
    tq = 128
    r_dim = wts["a_vecs"].shape[2]
    cb = wts["a_bd_r"].shape[-1]
    sb = LANES
    n_b = N_LAYERS - N_A_LAYERS

    h, h_bf = h0, h0.astype(BF16)
    saved = []
    kvs = None
    for layer in range(N_LAYERS):
        lw = {}
        sv = {"h_bf": h_bf, "w": lw}
        if layer < N_A_LAYERS:
            lw["in"] = hooks.weight(layer, "in", h)
            sv["gr"] = _proj_in(h_bf, lw["in"], shard_major=False, name="a_in_proj")
            sv["rec"] = _conv_a_fwd(sv["gr"], wts["a_cwb"][layer], cb=cb, name="a_conv_fwd")
            a, u, sv["r"], sv["i"] = _gates_fwd(sv["rec"], wts["a_bd_r"][layer], wts["a_bd_i"][layer],
                                                wts["a_vecs"][layer], tm=tm, name="a_gates_fwd")
            sv["a"] = a
            sv["hr"], y3 = _scan_fwd(a, u, sv["gr"], cb=sb, name="a_scan_fwd")
        else:
            j = layer - N_A_LAYERS
            if j == 0:
                kv_w = _kv_layout(hooks.weight(layer, "kv_w", h), d)
                kvs = {"h_bf": h_bf, "w": kv_w}
                kvs["kv"] = _mm_nn(h_bf, kv_w[:, :2 * d], tn=_tile(2 * d, (512, 256, 128)), out_dtype=BF16,
                                   name="kv_proj")
                kvs["fp"] = _mm_nn(h_bf, kv_w[:, 2 * d:], tn=LANES, out_dtype=F32, name="f_proj")
                kvs["c"], ct = _fgate_fwd(kvs["fp"], wts["kv_fb"], tq=tq, name="fgate_fwd")
                kvs["ct"] = ct[:N_HEADS]
            lw["in"] = hooks.weight(layer, "in", kvs["c"] if j == 0 else h)
            sv["qg"] = _proj_in(h_bf, lw["in"], shard_major=False, name="b_in_proj")
            sv["o"], y3 = _attn_fwd(sv["qg"], kvs["kv"], kvs["ct"], tq=tq, name="attn_fwd")
        sv["y3"] = y3
        lw["out"] = hooks.weight(layer, "out", y3)
        sv["s1"], h, h_bf = _out_ln(y3, lw["out"], h, wts["ln1_g"][layer], wts["ln1_b"][layer], n_valid=n_tok,
                                    tm=tm, name="mix_out_ln")
        sv["h1_bf"] = h_bf
        lw["f_in"] = hooks.weight(layer, "f_in", h)
        sv["z3"] = _proj_in(h_bf, lw["f_in"], shard_major=True, transposed=True, name="f_in_proj")
        sv["yf3"] = _convglu_fwd(sv["z3"], wts["f_cwb3"][layer], name="f_convglu_fwd")
        lw["f_out"] = hooks.weight(layer, "f_out", sv["yf3"])
        sv["s2"], h, h_bf = _out_ln(sv["yf3"], lw["f_out"], h, wts["ln2_g"][layer], wts["ln2_b"][layer],
                                    n_valid=n_tok, tm=tm, name="ffn_out_ln")
        saved.append(sv)

    loss_tile, dh = _loss_bwd(h, tgt, lo=n_meta, hi=n_tok, tm=tm, name="loss")

    grads = {k: [None] * N_LAYERS for k in ("f_cwb3", "ln1_gb", "ln2_gb")}
    grads.update({k: [None] * N_A_LAYERS for k in ("a_cwb", "a_bd_r", "a_bd_i", "a_vecs")})
    dkv = []
    token = jnp.zeros((), F32)
    for layer in reversed(range(N_LAYERS)):
        sv = saved[layer]
        lw = sv["w"]
        big = {}
        ds, ds_bf, grads["ln2_gb"][layer] = _ln_bwd(dh, sv["s2"], wts["ln2_g"][layer] + token, tm=tm, name="ln_bwd")
        dz, dcw = _ffn_bwd_mid(ds_bf, lw["f_out"], sv["z3"], wts["f_cwb3"][layer], name="f_bwd_mid")
        grads["f_cwb3"][layer] = dcw.reshape((N_DEV,) + dcw.shape[2:])
        dz3 = dz
        big["f_out"] = _w_out_grad(sv["yf3"], ds_bf, lw["f_out"].shape[1], name="f_w_out_grad")
        dh = _in_bwd(dz3, lw["f_in"], ds, tm=tmb, transposed=True, name="f_in_bwd")
        big["f_in"] = _w_in_grad(sv["h1_bf"], dz3, transposed=True, name="f_w_in_grad")
        token = hooks.grads_ready(layer, "ffn", big)
        big = {}
        ds, ds_bf, grads["ln1_gb"][layer] = _ln_bwd(dh, sv["s1"], wts["ln1_g"][layer] + token, tm=tm, name="ln_bwd")
        if layer < N_A_LAYERS:
            dy = _out_bwd(ds_bf, lw["out"], tm=tmb // 2, name="a_out_bwd")
            big["out"] = _w_out_grad(sv["y3"], ds_bf, lw["out"].shape[1], name="a_w_out_grad")
            d_h, d_a, dgate = _scan_bwd(dy, sv["gr"], sv["hr"], sv["a"], cb=sb, name="a_scan_bwd")
            d_rec, dpr, dpi, grads["a_vecs"][layer] = _gates_bwd(
                sv["rec"], sv["r"], sv["i"], sv["a"], d_h, d_a, wts["a_bd_r"][layer], wts["a_bd_i"][layer],
                wts["a_vecs"][layer], tm=tm, name="a_gates_bwd")
            grads["a_bd_r"][layer], grads["a_bd_i"][layer] = _bd_grad(sv["rec"], dpr, dpi, cb=cb, name="a_bd_grad")
            dact, grads["a_cwb"][layer] = _conv_a_bwd(d_rec, sv["gr"], dgate, wts["a_cwb"][layer], cb=cb,
                                                      name="a_conv_bwd")
            dh = _in_bwd(dact, lw["in"], ds, tm=tmb, name="a_in_bwd")
            big["in"] = _w_in_grad(sv["h_bf"], dact, name="a_w_in_grad")
        else:
            j = layer - N_A_LAYERS
            dy = _out_bwd(ds_bf, lw["out"], tm=tmb // 2, name="b_out_bwd")
            big["out"] = _w_out_grad(sv["y3"], ds_bf, lw["out"].shape[1], name="b_w_out_grad")
            dqg, dk, dv, dc = _attn_bwd(dy, sv["qg"], sv["o"], kvs["kv"], kvs["ct"], tq=tq,
                                        name="attn_bwd")
            dkv.append((dk, dv, dc))
            dh = _in_bwd(dqg, lw["in"], ds, tm=tmb, name="b_in_bwd")
            big["in"] = _w_in_grad(sv["h_bf"], dqg, name="b_w_in_grad")
            if j == 0:
                hpb = _head_block_width(d // N_HEADS, BWD_HEAD_TILES) // (d // N_HEADS)
                dct = (dkv[0][2] + dkv[1][2])[:, :hpb, :].reshape(N_HEADS, tp)
                dct = jnp.concatenate([dct, jnp.zeros((LANES - N_HEADS, tp), F32)])
                df_bf, grads["kv_fb"] = _fgate_bwd(dct, kvs["fp"], wts["kv_fb"], tq=tq, name="fgate_bwd")
                dkvz = jnp.concatenate([_pair_sum(dkv[0][0], dkv[1][0], tm=tm, name="kv_pair_sum"),
                                        _pair_sum(dkv[0][1], dkv[1][1], tm=tm, name="kv_pair_sum"), df_bf], axis=1)
                dh = _mm_nt_full(dkvz, kvs["w"], dh, tm=tmb // 2, name="kv_in_bwd")
                big["kv_w"] = _mm_tn_cols(kvs["h_bf"], dkvz, tn=LANES, name="kv_w_grad")
        token = hooks.grads_ready(layer, "mix", big)
    return loss_tile, dh, grads


def _finish_small_grads(grads, d_h0, n_meta):
    r_dim = grads["a_vecs"][0].shape[1]
    per = _lru_block_cols(r_dim) // (r_dim // N_LRU_BLOCKS)
    a_cwb = jnp.stack(grads["a_cwb"])
    a_vecs = jnp.stack(grads["a_vecs"])
    f_cwb3 = jnp.stack(grads["f_cwb3"])
    ln1 = jnp.stack(grads["ln1_gb"])
    ln2 = jnp.stack(grads["ln2_gb"])
    f_rows = f_cwb3.transpose(0, 2, 1, 3).reshape(N_LAYERS, 8, -1)
    return {
        "meta": d_h0[:n_meta],
        "a_conv_w": a_cwb[:, :4], "a_conv_b": a_cwb[:, 4],
        "a_w_r": jnp.stack([_block_diag_extract(g, per) for g in grads["a_bd_r"]]),
        "a_b_r": a_vecs[:, 0],
        "a_w_i": jnp.stack([_block_diag_extract(g, per) for g in grads["a_bd_i"]]),
        "a_b_i": a_vecs[:, 1], "a_lambda": a_vecs[:, 2],
        "kv_f_b": grads["kv_fb"][0, :N_HEADS],
        "f_conv_w": f_rows[:, :3], "f_conv_b": f_rows[:, 3],
        "ln1_g": ln1[:, 0], "ln1_b": ln1[:, 1], "ln2_g": ln2[:, 0], "ln2_b": ln2[:, 1],
    }


def kernel(x, meta, a_w_in, a_conv_w, a_conv_b, a_w_r, a_b_r, a_w_i, a_b_i, a_lambda, a_w_out, kv_w, kv_f_b, b_w_in, b_w_out, f_w_in, f_conv_w, f_conv_b, f_w_out, ln1_g, ln1_b, ln2_g, ln2_b, loss_target, m_meta, m_a_w_in, m_a_conv_w, m_a_conv_b, m_a_w_r, m_a_b_r, m_a_w_i, m_a_b_i, m_a_lambda, m_a_w_out, m_kv_w, m_kv_f_b, m_b_w_in, m_b_w_out, m_f_w_in, m_f_conv_w, m_f_conv_b, m_f_w_out, m_ln1_g, m_ln1_b, m_ln2_g, m_ln2_b, v_meta, v_a_w_in, v_a_conv_w, v_a_conv_b, v_a_w_r, v_a_b_r, v_a_w_i, v_a_b_i, v_a_lambda, v_a_w_out, v_kv_w, v_kv_f_b, v_b_w_in, v_b_w_out, v_f_w_in, v_f_conv_w, v_f_conv_b, v_f_w_out, v_ln1_g, v_ln1_b, v_ln2_g, v_ln2_b):
    w = dict(meta=meta, a_w_in=a_w_in, a_conv_w=a_conv_w, a_conv_b=a_conv_b, a_w_r=a_w_r, a_b_r=a_b_r, a_w_i=a_w_i,
             a_b_i=a_b_i, a_lambda=a_lambda, a_w_out=a_w_out, kv_w=kv_w, kv_f_b=kv_f_b, b_w_in=b_w_in,
             b_w_out=b_w_out, f_w_in=f_w_in, f_conv_w=f_conv_w, f_conv_b=f_conv_b, f_w_out=f_w_out, ln1_g=ln1_g,
             ln1_b=ln1_b, ln2_g=ln2_g, ln2_b=ln2_b)
    m = dict(meta=m_meta, a_w_in=m_a_w_in, a_conv_w=m_a_conv_w, a_conv_b=m_a_conv_b, a_w_r=m_a_w_r, a_b_r=m_a_b_r,
             a_w_i=m_a_w_i, a_b_i=m_a_b_i, a_lambda=m_a_lambda, a_w_out=m_a_w_out, kv_w=m_kv_w, kv_f_b=m_kv_f_b,
             b_w_in=m_b_w_in, b_w_out=m_b_w_out, f_w_in=m_f_w_in, f_conv_w=m_f_conv_w, f_conv_b=m_f_conv_b,
             f_w_out=m_f_w_out, ln1_g=m_ln1_g, ln1_b=m_ln1_b, ln2_g=m_ln2_g, ln2_b=m_ln2_b)
    v = dict(meta=v_meta, a_w_in=v_a_w_in, a_conv_w=v_a_conv_w, a_conv_b=v_a_conv_b, a_w_r=v_a_w_r, a_b_r=v_a_b_r,
             a_w_i=v_a_w_i, a_b_i=v_a_b_i, a_lambda=v_a_lambda, a_w_out=v_a_w_out, kv_w=v_kv_w, kv_f_b=v_kv_f_b,
             b_w_in=v_b_w_in, b_w_out=v_b_w_out, f_w_in=v_f_w_in, f_conv_w=v_f_conv_w, f_conv_b=v_f_conv_b,
             f_w_out=v_f_w_out, ln1_g=v_ln1_g, ln1_b=v_ln1_b, ln2_g=v_ln2_g, ln2_b=v_ln2_b)
    shapes = {n: w[n].shape for n in WEIGHT_NAMES}

    me = jnp.reshape(_my_index(), (1,)).astype(jnp.int32)

    def as_stored(name, a):
        return jnp.swapaxes(a, 1, 2) if name == "f_w_in" else a

    param_of = {"in": ("a_w_in", "b_w_in"), "out": ("a_w_out", "b_w_out"), "f_in": ("f_w_in",) * 2,
                "f_out": ("f_w_out",) * 2}
    order = [("small", None, None)]
    for layer in range(N_LAYERS):
        if layer == N_A_LAYERS:
            order.append(("kv_w", layer, 0))
        for key in ("in", "out", "f_in", "f_out"):
            order.append((key, layer, layer if key[0] == "f" or layer < N_A_LAYERS else layer - N_A_LAYERS))
    def place(key, layer, idx):
        if key == "small":
            return _place_own(_pack([w[n] for n in COL_F32], F32)[None], 0, me, out_dtype=F32, name="place_small")
        if key == "kv_w":
            return _place_own(w["kv_w"][None], 0, me, out_dtype=BF16, name="place_kv_w")
        name = param_of[key][0 if layer < N_A_LAYERS else 1]
        return _place_own(as_stored(name, w[name]), idx, me, out_dtype=BF16, name=f"place_{name}_{idx}")

    lands = [place(*o) for o in order]
    gather_handles, gather_token = _split_start([([l], [l]) for l in lands], scatter=False, name="gather_start")
    group_of = {(key, layer): g for g, (key, layer, _) in enumerate(order)}
    (got_s,) = _split_wait(gather_handles[0], gather_token, scatter=False, name="gather_wait_small")
    small = {n: w[n] for n in REPLICATED}
    for n, part in zip(COL_F32, _unpack(got_s, [w[n].shape for n in COL_F32], 1)):
        small[n] = _cols_full(part)
    n_meta, d = small["meta"].shape

    class Hooks:
        pending = None
        received = {}
        sent = {}

        @staticmethod
        def weight(layer, key, after):
            (got,) = _split_wait(gather_handles[group_of[(key, layer)]], after, scatter=False,
                                 name=f"gather_wait_{key}_{layer}")
            return got

        @staticmethod
        def collect(after):
            if Hooks.pending is not None:
                tag, names, handle = Hooks.pending
                got = _split_wait(handle, after, scatter=True, name=f"scatter_wait_{tag}")
                Hooks.received.update(zip(names, got))
                Hooks.pending = None

        @staticmethod
        def grads_ready(layer, part, big):
            if "kv_w" in big:
                big["kv_w"] = _cols_split(big["kv_w"][:, :shapes["kv_w"][1] * N_DEV]).astype(BF16)
            names = [(key, layer) for key in big]
            send = [big[key] for key in big]
            Hooks.collect(send[0])
            empty = [lax.empty(s.shape, s.dtype) for s in send]
            handles, token = _split_start([(send, empty)], scatter=True, name=f"scatter_start_{part}_{layer}")
            Hooks.pending = (f"{part}_{layer}", names, handles[0])
            Hooks.sent.update(zip(names, handles[0][2]))
            return token[0, 0]

    Hooks.pending, Hooks.received, Hooks.sent = None, {}, {}

    n_tok = n_meta + x.shape[1]
    tp = -(-n_tok // ROW_ALIGN) * ROW_ALIGN
    pad = jnp.zeros((tp - n_tok, d), F32)
    h0 = jnp.concatenate([small["meta"], x[0], pad])
    tgt = jnp.concatenate([jnp.zeros((n_meta, d), F32), loss_target[0], pad])
    loss_tile, d_h0, grads = _local_step(h0, tgt, n_meta, n_tok, _small_layouts(small), Hooks)
    g_small = _finish_small_grads(grads, d_h0, n_meta)
    loss = lax.psum(loss_tile[0, 0], MESH_AXES)
    grad_x = d_h0[n_meta:n_tok][None]

    rep = _pack([g_small[n] for n in REPLICATED], F32, row_quantum=16 * N_DEV)
    send = [_pack8([_cols_split(g_small[n]) for n in COL_F32], F32), rep.reshape(N_DEV, -1, PACK_COLS)]
    lands = _own_blocks(send, name="scatter_own_small")
    handles, token = _split_start([(send, lands)], scatter=True, name="scatter_start_small")

    g, delta, new_m, new_v = {}, {}, {}, {}
    layers_of = {
        "a_w_in": [("in", l) for l in range(N_A_LAYERS)], "a_w_out": [("out", l) for l in range(N_A_LAYERS)],
        "b_w_in": [("in", l) for l in range(N_A_LAYERS, N_LAYERS)],
        "b_w_out": [("out", l) for l in range(N_A_LAYERS, N_LAYERS)],
        "f_w_in": [("f_in", l) for l in range(N_LAYERS)], "f_w_out": [("f_out", l) for l in range(N_LAYERS)],
        "kv_w": [("kv_w", N_A_LAYERS)],
    }
    ready = [n for n in BIG + ("kv_w",) if all(t in Hooks.received for t in layers_of[n])]

    def done(names):
        return jnp.stack([g[n][(0,) * g[n].ndim] for n in names])

    for n in ready + [n for n in BIG + ("kv_w",) if n not in ready]:
        if n not in ready and Hooks.pending is not None:
            Hooks.collect(done(ready))
        lift = (lambda a: a[None]) if n == "kv_w" else (lambda a, n=n: as_stored(n, a))
        outs = _sum_adamw([Hooks.received[t] for t in layers_of[n]], [Hooks.sent[t] for t in layers_of[n]], me,
                          lift(w[n]), lift(m[n]), lift(v[n]), name="sum_adamw_" + n)
        g[n], delta[n], new_m[n], new_v[n] = [as_stored(n, o).reshape(shapes[n]) for o in outs]
    recv_s, recv_r = _split_wait(handles[0], done(BIG + ("kv_w",)), scatter=True, name="scatter_wait_small")
    sum_s = _sum8(recv_s, name="sum_grads_f32")
    g.update(zip(COL_F32, _unpack(sum_s, [shapes[n] for n in COL_F32], 0)))
    (got_r,) = _all_gather([_sum8(recv_r, name="sum_grads_replicated")], name="gather_replicated_sums")
    g.update(zip(REPLICATED, _unpack(got_r.reshape(-1, PACK_COLS), [shapes[n] for n in REPLICATED], 0)))

    for n in COL_F32 + REPLICATED:
        shp = shapes[n]
        dl, nm, nv = _adamw(_slab2d(w[n]), _slab2d(g[n]), _slab2d(m[n]), _slab2d(v[n]), name="adamw")
        delta[n], new_m[n], new_v[n] = dl.reshape(shp), nm.reshape(shp), nv.reshape(shp)
    return (loss, grad_x, *[g[n] for n in WEIGHT_NAMES], *[delta[n] for n in WEIGHT_NAMES],
            *[new_m[n] for n in WEIGHT_NAMES], *[new_v[n] for n in WEIGHT_NAMES])
```

```python
import math

import jax
import jax.numpy as jnp
from jax import lax
from jax.experimental import pallas as pl
from jax.experimental.pallas import tpu as pltpu

F32 = jnp.float32
BF16 = jnp.bfloat16

N_DEV = 8
MESH_AXES = ("x", "y", "c")
N_LAYERS = 4
N_A_LAYERS = 2
N_LRU_BLOCKS = 16
N_HEADS = 16
LRU_C = 8.0
DN_ALPHA = (2 * N_LAYERS) ** 0.25
LN_EPS = 1e-5
ADAM_LR, ADAM_B1, ADAM_B2, ADAM_EPS, ADAM_WD, ADAM_STEP = 0.001, 0.9, 0.999, 1e-08, 0.01, 10

LANES = 128
SUBLANES = 8
ROW_ALIGN = 128
VMEM_LIMIT_BYTES = 56 * 1024 * 1024
GELU_K = math.sqrt(2.0 / math.pi)
GELU_C = 0.044715
PACK_COLS = 1024


def _params(*sem):
    return pltpu.CompilerParams(dimension_semantics=sem, vmem_limit_bytes=VMEM_LIMIT_BYTES)


def _gelu(x):
    th = jnp.tanh(GELU_K * (x + GELU_C * x * x * x))
    return 0.5 * x * (1.0 + th)


def _gelu_and_grad(x):
    x2 = x * x
    th = jnp.tanh(GELU_K * (x + GELU_C * x2 * x))
    g = 0.5 * x * (1.0 + th)
    dg = 0.5 * (1.0 + th) + 0.5 * x * (1.0 - th * th) * (GELU_K * (1.0 + 3.0 * GELU_C * x2))
    return g, dg


def _sigmoid(x):
    return 1.0 / (1.0 + jnp.exp(-x))


def _expm1(x):
    small = x * (1.0 + 0.5 * x * (1.0 + (1.0 / 3.0) * x * (1.0 + 0.25 * x)))
    return jnp.where(jnp.abs(x) < 1e-2, small, jnp.exp(x) - 1.0)


def _softplus(x):
    e = jnp.exp(-jnp.abs(x))
    small = e * (1.0 - 0.5 * e * (1.0 - (2.0 / 3.0) * e))
    return jnp.maximum(x, 0.0) + jnp.where(e < 1e-2, small, jnp.log(1.0 + e))


def _shift_down(x, s):
    if s == 0:
        return x
    rows = lax.broadcasted_iota(jnp.int32, x.shape, 0)
    return jnp.where(rows >= s, pltpu.roll(x, s, 0), 0.0)


def _shift_up(x, s):
    if s == 0:
        return x
    n = x.shape[0]
    rows = lax.broadcasted_iota(jnp.int32, x.shape, 0)
    return jnp.where(rows < n - s, pltpu.roll(x, n - s, 0), 0.0)


def _dot_nn(a, b):
    return lax.dot_general(a, b, (((1,), (0,)), ((), ())), preferred_element_type=F32)


def _dot_nt(a, b):
    return lax.dot_general(a, b, (((1,), (1,)), ((), ())), preferred_element_type=F32)


def _dot_tn(a, b):
    return lax.dot_general(a, b, (((0,), (0,)), ((), ())), preferred_element_type=F32)


def _rows8(vals, width):
    rows = lax.broadcasted_iota(jnp.int32, (8, width), 0)
    out = jnp.zeros((8, width), F32)
    for k, v in enumerate(vals):
        out = jnp.where(rows == k, jnp.broadcast_to(v, (8, width)), out)
    return out


def _tile(n, prefer):
    for c in prefer:
        if n % c == 0:
            return c
    return n


def _mm_nn(a, b, *, tn, out_dtype, name):
    m, k = a.shape
    n = b.shape[1]

    def body(a_ref, b_ref, o_ref):
        o_ref[...] = _dot_nn(a_ref[...], b_ref[...]).astype(o_ref.dtype)

    return pl.pallas_call(
        body, name=name, grid=(n // tn,),
        in_specs=[pl.BlockSpec((m, k), lambda j: (0, 0)), pl.BlockSpec((k, tn), lambda j: (0, j))],
        out_specs=pl.BlockSpec((m, tn), lambda j: (0, j)),
        out_shape=jax.ShapeDtypeStruct((m, n), out_dtype),
        compiler_params=_params("parallel"),
    )(a, b)


def _proj_in(h_bf, g_in, *, shard_major, name, transposed=False):
    t, k = h_bf.shape
    n = g_in.shape[1] if transposed else g_in.shape[2]

    def body(a_ref, b_ref, o_ref):
        o_ref[...] = _dot_nt(a_ref[...], b_ref[...]) if transposed else _dot_nn(a_ref[...], b_ref[...])

    if shard_major:
        out_spec = pl.BlockSpec((None, t, n), lambda j: (j, 0, 0))
        out_shape = jax.ShapeDtypeStruct((N_DEV, t, n), F32)
    else:
        out_spec = pl.BlockSpec((t, n), lambda j: (0, j))
        out_shape = jax.ShapeDtypeStruct((t, N_DEV * n), F32)
    return pl.pallas_call(
        body, name=name, grid=(N_DEV,),
        in_specs=[pl.BlockSpec((t, k), lambda j: (0, 0)),
                  pl.BlockSpec((None,) + g_in.shape[1:], lambda j: (j, 0, 0))],
        out_specs=out_spec, out_shape=out_shape,
        compiler_params=_params("parallel"),
    )(h_bf, g_in)


def _out_ln(y3, g_out, hin, g, b, *, n_valid, tm, name):
    nj, t, kj = y3.shape
    _, r, d = g_out.shape

    def body(y_ref, w_ref, hin_ref, g_ref, b_ref, s_ref, h_ref, hb_ref):
        w = w_ref[...].reshape(N_DEV * r, d)
        s = DN_ALPHA * hin_ref[...]
        for jj in range(nj):
            s = s + _dot_nn(y_ref[jj], w[jj * kj:(jj + 1) * kj])
        mu = jnp.mean(s, axis=-1, keepdims=True)
        xc = s - mu
        var = jnp.mean(xc * xc, axis=-1, keepdims=True)
        h = xc * lax.rsqrt(var + LN_EPS) * g_ref[...] + b_ref[...]
        s_ref[...] = s
        h_ref[...] = h
        rows = pl.program_id(0) * tm + lax.broadcasted_iota(jnp.int32, (tm, d), 0)
        hb_ref[...] = jnp.where(rows < n_valid, h, 0.0).astype(BF16)

    row = pl.BlockSpec((tm, d), lambda i: (i, 0))
    vec = pl.BlockSpec((1, d), lambda i: (0, 0))
    return pl.pallas_call(
        body, name=name, grid=(t // tm,),
        in_specs=[pl.BlockSpec((nj, tm, kj), lambda i: (0, i, 0)),
                  pl.BlockSpec((N_DEV, r, d), lambda i: (0, 0, 0)), row, vec, vec],
        out_specs=[row, row, row],
        out_shape=[jax.ShapeDtypeStruct((t, d), F32), jax.ShapeDtypeStruct((t, d), F32),
                   jax.ShapeDtypeStruct((t, d), BF16)],
        compiler_params=_params("parallel"),
    )(y3, g_out, hin, g, b)


def _out_bwd(ds_bf, g_out, *, tm, name):
    t, d = ds_bf.shape
    r = g_out.shape[1]

    def body(a_ref, w_ref, o_ref):
        o_ref[...] = _dot_nt(a_ref[...], w_ref[...].reshape(N_DEV * r, d))

    return pl.pallas_call(
        body, name=name, grid=(t // tm,),
        in_specs=[pl.BlockSpec((tm, d), lambda i: (i, 0)),
                  pl.BlockSpec((N_DEV, r, d), lambda i: (0, 0, 0))],
        out_specs=pl.BlockSpec((tm, N_DEV * r), lambda i: (i, 0)),
        out_shape=jax.ShapeDtypeStruct((t, N_DEV * r), F32),
        compiler_params=_params("parallel"),
    )(ds_bf, g_out)


def _in_bwd(dact, g_in, add, *, tm, name, alpha=DN_ALPHA, transposed=False):
    t = dact.shape[-2]
    _, k, n = g_in.shape
    if transposed:
        k, n = n, k
    halves = dact.shape[0] == 2 and dact.ndim == 3
    per = N_DEV // 2

    def body(a_ref, b_ref, add_ref, o_ref, acc_ref):
        j = pl.program_id(1)

        @pl.when(j == 0)
        def _():
            acc_ref[...] = alpha * add_ref[...]

        acc_ref[...] += _dot_nn(a_ref[...], b_ref[...]) if transposed else _dot_nt(a_ref[...], b_ref[...])

        @pl.when(j == N_DEV - 1)
        def _():
            o_ref[...] = acc_ref[...]

    if halves:
        a_spec = pl.BlockSpec((None, tm, n), lambda i, j: (j // per, i, j % per))
    elif dact.ndim == 4:
        a_spec = pl.BlockSpec((None, None, tm, n), lambda i, j: (j // per, j % per, i, 0))
    else:
        a_spec = pl.BlockSpec((None, tm, n), lambda i, j: (j, i, 0))
    return pl.pallas_call(
        body, name=name, grid=(t // tm, N_DEV),
        in_specs=[a_spec, pl.BlockSpec((None,) + g_in.shape[1:], lambda i, j: (j, 0, 0)),
                  pl.BlockSpec((tm, k), lambda i, j: (i, 0))],
        out_specs=pl.BlockSpec((tm, k), lambda i, j: (i, 0)),
        out_shape=jax.ShapeDtypeStruct((t, k), F32),
        scratch_shapes=[pltpu.VMEM((tm, k), F32)],
        compiler_params=_params("parallel", "arbitrary"),
    )(dact, g_in, add)


def _mm_nt_full(a, b, add, *, tm, name):
    t, n = a.shape
    k = b.shape[0]

    def body(a_ref, b_ref, add_ref, o_ref):
        o_ref[...] = add_ref[...] + _dot_nt(a_ref[...], b_ref[...])

    return pl.pallas_call(
        body, name=name, grid=(t // tm,),
        in_specs=[pl.BlockSpec((tm, n), lambda i: (i, 0)), pl.BlockSpec((k, n), lambda i: (0, 0)),
                  pl.BlockSpec((tm, k), lambda i: (i, 0))],
        out_specs=pl.BlockSpec((tm, k), lambda i: (i, 0)),
        out_shape=jax.ShapeDtypeStruct((t, k), F32),
        compiler_params=_params("parallel"),
    )(a, b, add)


def _w_in_grad(h_bf, dact, *, name, transposed=False):
    t, k = h_bf.shape
    halves = dact.shape[0] == 2 and dact.ndim == 3
    per = N_DEV // 2
    n = dact.shape[-1] // per if halves else dact.shape[-1]

    def body(a_ref, b_ref, o_ref):
        if transposed:
            o_ref[...] = _dot_tn(b_ref[...], a_ref[...]).astype(BF16)
        else:
            o_ref[...] = _dot_tn(a_ref[...], b_ref[...]).astype(BF16)

    if halves:
        b_spec = pl.BlockSpec((None, t, n), lambda j: (j // per, 0, j % per))
    elif dact.ndim == 4:
        b_spec = pl.BlockSpec((None, None, t, n), lambda j: (j // per, j % per, 0, 0))
    else:
        b_spec = pl.BlockSpec((None, t, n), lambda j: (j, 0, 0))
    return pl.pallas_call(
        body, name=name, grid=(N_DEV,),
        in_specs=[pl.BlockSpec((t, k), lambda j: (0, 0)), b_spec],
        out_specs=pl.BlockSpec((None, n, k) if transposed else (None, k, n), lambda j: (j, 0, 0)),
        out_shape=jax.ShapeDtypeStruct((N_DEV, n, k) if transposed else (N_DEV, k, n), BF16),
        compiler_params=_params("parallel"),
    )(h_bf, dact)


def _w_out_grad(y3, ds_bf, r, *, name):
    nj, t, kj = y3.shape
    d = ds_bf.shape[1]
    unit = r * LANES // math.gcd(r, LANES)
    ks = max([c for c in range(unit, min(kj, 768) + 1, unit) if kj % c == 0], default=kj)
    gsz = ks // r
    per = kj // ks

    def body(a_ref, b_ref, o_ref):
        o_ref[...] = _dot_tn(a_ref[...], b_ref[...]).reshape(gsz, r, d).astype(BF16)

    return pl.pallas_call(
        body, name=name, grid=(nj * per,),
        in_specs=[pl.BlockSpec((None, t, ks), lambda j: (j // per, 0, j % per)),
                  pl.BlockSpec((t, d), lambda j: (0, 0))],
        out_specs=pl.BlockSpec((gsz, r, d), lambda j: (j, 0, 0)),
        out_shape=jax.ShapeDtypeStruct((N_DEV, r, d), BF16),
        compiler_params=_params("parallel"),
    )(y3, ds_bf)


def _mm_tn_cols(a, b, *, tn, name):
    t, m = a.shape
    n = b.shape[1]

    def body(a_ref, b_ref, o_ref):
        o_ref[...] = _dot_tn(a_ref[...], b_ref[...])

    return pl.pallas_call(
        body, name=name, grid=(n // tn,),
        in_specs=[pl.BlockSpec((t, m), lambda j: (0, 0)), pl.BlockSpec((t, tn), lambda j: (0, j))],
        out_specs=pl.BlockSpec((m, tn), lambda j: (0, j)),
        out_shape=jax.ShapeDtypeStruct((m, n), F32),
        compiler_params=_params("parallel"),
    )(a, b)


def _ln_bwd(dout, s, g, *, tm, name):
    t, d = s.shape

    def body(do_ref, s_ref, g_ref, ds_ref, dsb_ref, gb_ref):
        i = pl.program_id(0)
        sv = s_ref[...]
        do = do_ref[...]
        mu = jnp.mean(sv, axis=-1, keepdims=True)
        xc = sv - mu
        var = jnp.mean(xc * xc, axis=-1, keepdims=True)
        rstd = lax.rsqrt(var + LN_EPS)
        xhat = xc * rstd
        dxhat = do * g_ref[...]
        m1 = jnp.mean(dxhat, axis=-1, keepdims=True)
        m2 = jnp.mean(dxhat * xhat, axis=-1, keepdims=True)
        ds = rstd * (dxhat - m1 - xhat * m2)
        ds_ref[...] = ds
        dsb_ref[...] = ds.astype(BF16)
        upd = _rows8([jnp.sum(do * xhat, axis=0, keepdims=True), jnp.sum(do, axis=0, keepdims=True)], d)

        @pl.when(i == 0)
        def _():
            gb_ref[...] = upd

        @pl.when(i > 0)
        def _():
            gb_ref[...] += upd

    row = pl.BlockSpec((tm, d), lambda i: (i, 0))
    return pl.pallas_call(
        body, name=name, grid=(t // tm,),
        in_specs=[row, row, pl.BlockSpec((1, d), lambda i: (0, 0))],
        out_specs=[row, row, pl.BlockSpec((8, d), lambda i: (0, 0))],
        out_shape=[jax.ShapeDtypeStruct((t, d), F32), jax.ShapeDtypeStruct((t, d), BF16),
                   jax.ShapeDtypeStruct((8, d), F32)],
        compiler_params=_params("arbitrary"),
    )(dout, s, g)


def _roll_down(x, s):
    return x if s == 0 else pltpu.roll(x, s, 0)


def _conv_taps(x, wb, width):
    y = jnp.broadcast_to(wb[width:width + 1, :], x.shape)
    for k in range(width):
        y = y + _roll_down(x, width - 1 - k) * wb[k:k + 1, :]
    return y


def _conv_taps_bwd(dy, x, wb, width):
    n = dy.shape[0]
    dx = jnp.zeros_like(dy)
    rows = []
    for k in range(width):
        s = width - 1 - k
        dy_up = dy if s == 0 else pltpu.roll(dy, n - s, 0)
        dx = dx + dy_up * wb[k:k + 1, :]
        rows.append(jnp.sum(dy_up * x, axis=0, keepdims=True))
    rows.append(jnp.sum(dy, axis=0, keepdims=True))
    t_idx = lax.broadcasted_iota(jnp.int32, dy.shape, 0)
    return jnp.where(t_idx < n - (width - 1), dx, 0.0), _rows8(rows, dy.shape[1])


def _convglu_fwd(z3, fwb3, *, name):
    _, t, n = z3.shape
    half = N_DEV // 2
    nc = pl.cdiv(n, LANES)

    def body(zg_ref, zv_ref, wg_ref, wv_ref, y_ref):
        gate = _conv_taps(zg_ref[...], wg_ref[...], 3)
        val = _conv_taps(zv_ref[...], wv_ref[...], 3)
        y_ref[...] = (_gelu(gate) * val).astype(BF16)

    zblk = lambda off: pl.BlockSpec((None, t, LANES), lambda j, c: (j + off, 0, c))
    wblk = lambda off: pl.BlockSpec((None, 8, LANES), lambda j, c: (j + off, 0, c))
    return pl.pallas_call(
        body, name=name, grid=(half, nc),
        in_specs=[zblk(0), zblk(half), wblk(0), wblk(half)],
        out_specs=zblk(0),
        out_shape=jax.ShapeDtypeStruct((half, t, n), BF16),
        compiler_params=_params("parallel", "parallel"),
    )(z3, z3, fwb3, fwb3)


def _ffn_bwd_mid(ds_bf, g_out, z3, fwb3, *, name):
    t, d = ds_bf.shape
    r = g_out.shape[1]
    n = z3.shape[2]
    half = N_DEV // 2
    nc = pl.cdiv(n, LANES)
    assert n == 2 * r

    def body(ds_ref, w_ref, zg_ref, zv_ref, wg_ref, wv_ref, dz_ref, dwb_ref, wsc_ref):
        c = pl.program_id(1)

        @pl.when(c == 0)
        def _():
            wsc_ref[0:r, :] = w_ref[0]
            wsc_ref[r:2 * r, :] = w_ref[1]
            if nc * LANES > n:
                wsc_ref[n:nc * LANES, :] = jnp.zeros((nc * LANES - n, d), BF16)

        w = wsc_ref[pl.ds(pl.multiple_of(c * LANES, LANES), LANES), :]
        dyf = _dot_nt(ds_ref[...], w)
        zg, zv = zg_ref[...], zv_ref[...]
        wg, wv = wg_ref[...], wv_ref[...]
        gate = _conv_taps(zg, wg, 3)
        val = _conv_taps(zv, wv, 3)
        gl, dgl = _gelu_and_grad(gate)
        dzg, dwg = _conv_taps_bwd(dyf * val * dgl, zg, wg, 3)
        dzv, dwv = _conv_taps_bwd(dyf * gl, zv, wv, 3)
        dz_ref[0] = dzg.astype(BF16)
        dz_ref[1] = dzv.astype(BF16)
        dwb_ref[0] = dwg
        dwb_ref[1] = dwv

    zblk = lambda off: pl.BlockSpec((None, t, LANES), lambda j, c: (j + off, 0, c))
    wblk = lambda off: pl.BlockSpec((None, 8, LANES), lambda j, c: (j + off, 0, c))
    return pl.pallas_call(
        body, name=name, grid=(half, nc),
        in_specs=[pl.BlockSpec((t, d), lambda j, c: (0, 0)),
                  pl.BlockSpec((2, r, d), lambda j, c: (j, 0, 0)),
                  zblk(0), zblk(half), wblk(0), wblk(half)],
        out_specs=[pl.BlockSpec((2, None, t, LANES), lambda j, c: (0, j, 0, c)),
                   pl.BlockSpec((2, None, 8, LANES), lambda j, c: (0, j, 0, c))],
        out_shape=[jax.ShapeDtypeStruct((2, half, t, n), BF16), jax.ShapeDtypeStruct((2, half, 8, n), F32)],
        scratch_shapes=[pltpu.VMEM((nc * LANES, d), BF16)],
        compiler_params=_params("parallel", "arbitrary"),
    )(ds_bf, g_out, z3, z3, fwb3, fwb3)


def _conv_a_fwd(gr, cwb, *, cb, name):
    t, r2 = gr.shape
    r = r2 // 2
    nb = r // cb

    def body(x_ref, w_ref, o_ref):
        o_ref[...] = _conv_taps(x_ref[...], w_ref[...], 4)

    return pl.pallas_call(
        body, name=name, grid=(nb,),
        in_specs=[pl.BlockSpec((t, cb), lambda j: (0, j + nb)), pl.BlockSpec((8, cb), lambda j: (0, j))],
        out_specs=pl.BlockSpec((t, cb), lambda j: (0, j)),
        out_shape=jax.ShapeDtypeStruct((t, r), F32),
        compiler_params=_params("parallel"),
    )(gr, cwb)


def _gates_fwd(rec, bd_r, bd_i, vecs, *, tm, name):
    t, r_dim = rec.shape
    nb, cb, _ = bd_r.shape

    def body(x_ref, wr_ref, wi_ref, v_ref, a_ref, u_ref, r_ref, i_ref):
        x = x_ref[...]
        xb = x.astype(BF16)
        v = v_ref[...]
        r = _sigmoid(_dot_nn(xb, wr_ref[...]) + v[0:1, :])
        i = _sigmoid(_dot_nn(xb, wi_ref[...]) + v[1:2, :])
        log_a = (-LRU_C) * r * _softplus(-v[2:3, :])
        a_ref[...] = jnp.exp(log_a)
        u_ref[...] = jnp.sqrt(-_expm1(2.0 * log_a)) * (i * x)
        r_ref[...] = r
        i_ref[...] = i

    blk = pl.BlockSpec((tm, cb), lambda j, i: (i, j))
    wspec = pl.BlockSpec((None, cb, cb), lambda j, i: (j, 0, 0))
    out = jax.ShapeDtypeStruct((t, r_dim), F32)
    return pl.pallas_call(
        body, name=name, grid=(nb, t // tm),
        in_specs=[blk, wspec, wspec, pl.BlockSpec((8, cb), lambda j, i: (0, j))],
        out_specs=[blk, blk, blk, blk],
        out_shape=[out, out, out, out],
        compiler_params=_params("parallel", "parallel"),
    )(rec, bd_r, bd_i, vecs)


def _scan_fwd(a, u, gr, *, cb, name):
    t, r = a.shape
    nb = r // cb
    seg = t // SUBLANES

    def body(a_ref, u_ref, g_ref, h_ref, y_ref, p_ref):
        def step(k, carry):
            h, p = carry
            rows = pl.ds(k, SUBLANES, stride=seg)
            av = a_ref[rows, :]
            h = av * h + u_ref[rows, :]
            p = av * p
            h_ref[rows, :] = h
            p_ref[rows, :] = p
            return h, p

        h_fin, p_fin = lax.fori_loop(0, seg, step, (jnp.zeros((SUBLANES, cb), F32), jnp.ones((SUBLANES, cb), F32)),
                                     unroll=4)
        carry = h_fin[0:1, :]
        for s in range(1, SUBLANES):
            rows = slice(s * seg, (s + 1) * seg)
            h_ref[rows, :] = h_ref[rows, :] + p_ref[rows, :] * carry
            carry = h_fin[s:s + 1, :] + p_fin[s:s + 1, :] * carry
        y_ref[...] = (_gelu(g_ref[...]) * h_ref[...]).astype(BF16)

    blk = pl.BlockSpec((t, cb), lambda j: (0, j))
    return pl.pallas_call(
        body, name=name, grid=(nb,),
        in_specs=[blk, blk, blk],
        out_specs=[blk, pl.BlockSpec((None, t, cb), lambda j: (0, 0, j))],
        out_shape=[jax.ShapeDtypeStruct((t, r), F32), jax.ShapeDtypeStruct((1, t, r), BF16)],
        scratch_shapes=[pltpu.VMEM((t, cb), F32)],
        compiler_params=_params("parallel"),
    )(a, u, gr)


def _scan_bwd(dy, gr, hr, a, *, cb, name):
    t, r = a.shape
    nb = r // cb
    seg = t // SUBLANES

    def body(dy_ref, g_ref, h_ref, a_ref, dh_ref, da_ref, dg_ref, q_ref):
        gl, dgl = _gelu_and_grad(g_ref[...])
        dyv = dy_ref[...]
        dh_ref[...] = dyv * gl
        dg_ref[...] = (dyv * h_ref[...] * dgl).astype(BF16)

        def step(k, carry):
            cin, q = carry
            rows = pl.ds(seg - 1 - k, SUBLANES, stride=seg)
            dh = dh_ref[rows, :] + cin
            dh_ref[rows, :] = dh
            q_ref[rows, :] = q
            av = a_ref[rows, :]
            return av * dh, av * q

        c_fin, q_fin = lax.fori_loop(0, seg, step, (jnp.zeros((SUBLANES, cb), F32), jnp.ones((SUBLANES, cb), F32)),
                                     unroll=4)
        carry = c_fin[SUBLANES - 1:SUBLANES, :]
        for s in range(SUBLANES - 2, -1, -1):
            rows = slice(s * seg, (s + 1) * seg)
            dh_ref[rows, :] = dh_ref[rows, :] + q_ref[rows, :] * carry
            carry = c_fin[s:s + 1, :] + q_fin[s:s + 1, :] * carry
        da_ref[...] = dh_ref[...] * _shift_down(h_ref[...], 1)

    blk = pl.BlockSpec((t, cb), lambda j: (0, j))
    return pl.pallas_call(
        body, name=name, grid=(nb,),
        in_specs=[blk, blk, blk, blk],
        out_specs=[blk, blk, blk],
        out_shape=[jax.ShapeDtypeStruct((t, r), F32), jax.ShapeDtypeStruct((t, r), F32),
                   jax.ShapeDtypeStruct((t, r), BF16)],
        scratch_shapes=[pltpu.VMEM((t, cb), F32)],
        compiler_params=_params("parallel"),
    )(dy, gr, hr, a)


def _gates_bwd(rec, r, i, a, dh, da, bd_r, bd_i, vecs, *, tm, name):
    t, r_dim = rec.shape
    nb, cb, _ = bd_r.shape

    def body(x_ref, r_ref, i_ref, a_ref, dh_ref, da_ref, wr_ref, wi_ref, v_ref, dx_ref, dpr_ref, dpi_ref, dv_ref):
        step = pl.program_id(1)
        x, r, i, a, dh, da = x_ref[...], r_ref[...], i_ref[...], a_ref[...], dh_ref[...], da_ref[...]
        lam = v_ref[...][2:3, :]
        sp = _softplus(-lam)
        a2 = a * a
        mult = jnp.sqrt(-_expm1(2.0 * (-LRU_C) * r * sp))
        d_i = dh * mult * x
        d_log_a = da * a - (dh * i * x) * a2 / mult
        d_r = d_log_a * ((-LRU_C) * sp)
        d_sp = jnp.sum(d_log_a * ((-LRU_C) * r), axis=0, keepdims=True)
        d_pre_r = d_r * r * (1.0 - r)
        d_pre_i = d_i * i * (1.0 - i)
        dprb = d_pre_r.astype(BF16)
        dpib = d_pre_i.astype(BF16)
        dx_ref[...] = dh * mult * i + _dot_nt(dprb, wr_ref[...]) + _dot_nt(dpib, wi_ref[...])
        dpr_ref[...] = dprb
        dpi_ref[...] = dpib
        upd = _rows8([jnp.sum(d_pre_r, axis=0, keepdims=True), jnp.sum(d_pre_i, axis=0, keepdims=True),
                      -d_sp * _sigmoid(-lam)], cb)

        @pl.when(step == 0)
        def _():
            dv_ref[...] = upd

        @pl.when(step > 0)
        def _():
            dv_ref[...] += upd

    blk = pl.BlockSpec((tm, cb), lambda j, i: (i, j))
    wspec = pl.BlockSpec((None, cb, cb), lambda j, i: (j, 0, 0))
    vspec = pl.BlockSpec((8, cb), lambda j, i: (0, j))
    return pl.pallas_call(
        body, name=name, grid=(nb, t // tm),
        in_specs=[blk] * 6 + [wspec, wspec, vspec],
        out_specs=[blk, blk, blk, vspec],
        out_shape=[jax.ShapeDtypeStruct((t, r_dim), F32), jax.ShapeDtypeStruct((t, r_dim), BF16),
                   jax.ShapeDtypeStruct((t, r_dim), BF16), jax.ShapeDtypeStruct((8, r_dim), F32)],
        compiler_params=_params("parallel", "arbitrary"),
    )(rec, r, i, a, dh, da, bd_r, bd_i, vecs)


def _bd_grad(rec, dpr, dpi, *, cb, name):
    t, r = rec.shape
    nb = r // cb

    def body(x_ref, dr_ref, di_ref, gr_ref, gi_ref):
        xb = x_ref[...].astype(BF16)
        gr_ref[...] = _dot_tn(xb, dr_ref[...])
        gi_ref[...] = _dot_tn(xb, di_ref[...])

    blk = pl.BlockSpec((t, cb), lambda j: (0, j))
    wspec = pl.BlockSpec((None, cb, cb), lambda j: (j, 0, 0))
    out = jax.ShapeDtypeStruct((nb, cb, cb), F32)
    return pl.pallas_call(
        body, name=name, grid=(nb,),
        in_specs=[blk, blk, blk], out_specs=[wspec, wspec], out_shape=[out, out],
        compiler_params=_params("parallel"),
    )(rec, dpr, dpi)


def _conv_a_bwd(d_rec, gr, dgate, cwb, *, cb, name):
    t, r = d_rec.shape
    nb = r // cb

    def body(dy_ref, x_ref, dg_ref, w_ref, dact_ref, dw_ref):
        dx, dw = _conv_taps_bwd(dy_ref[...], x_ref[...], w_ref[...], 4)
        dact_ref[0] = dg_ref[...]
        dact_ref[1] = dx.astype(BF16)
        dw_ref[...] = dw

    blk = pl.BlockSpec((t, cb), lambda j: (0, j))
    vspec = pl.BlockSpec((8, cb), lambda j: (0, j))
    return pl.pallas_call(
        body, name=name, grid=(nb,),
        in_specs=[blk, pl.BlockSpec((t, cb), lambda j: (0, j + nb)), blk, vspec],
        out_specs=[pl.BlockSpec((2, t, cb), lambda j: (0, 0, j)), vspec],
        out_shape=[jax.ShapeDtypeStruct((2, t, r), BF16), jax.ShapeDtypeStruct((8, r), F32)],
        compiler_params=_params("parallel"),
    )(d_rec, gr, dgate, cwb)


def _split3(x):
    p0 = x.astype(BF16)
    r1 = x - p0.astype(F32)
    p1 = r1.astype(BF16)
    p2 = (r1 - p1.astype(F32)).astype(BF16)
    return p0, p1, p2


def _fgate_fwd(fp, fb, *, tq, name):
    t = fp.shape[0]

    def body(f_ref, b_ref, c_ref, ct_ref):
        logf = -_softplus(-(f_ref[...] + b_ref[...]))
        rows = pl.program_id(0) * tq + lax.broadcasted_iota(jnp.int32, (tq, t), 0)
        cols = lax.broadcasted_iota(jnp.int32, (tq, t), 1)
        tri = (cols <= rows).astype(BF16)
        p0, p1, p2 = _split3(logf)
        c = _dot_nn(tri, p0) + _dot_nn(tri, p1) + _dot_nn(tri, p2)
        c_ref[...] = c
        ct_ref[...] = c.T

    return pl.pallas_call(
        body, name=name, grid=(t // tq,),
        in_specs=[pl.BlockSpec((t, LANES), lambda i: (0, 0)), pl.BlockSpec((1, LANES), lambda i: (0, 0))],
        out_specs=[pl.BlockSpec((tq, LANES), lambda i: (i, 0)), pl.BlockSpec((LANES, tq), lambda i: (0, i))],
        out_shape=[jax.ShapeDtypeStruct((t, LANES), F32), jax.ShapeDtypeStruct((LANES, t), F32)],
        compiler_params=_params("parallel"),
    )(fp, fb)


def _fgate_bwd(dct, fp, fb, *, tq, name):
    t = fp.shape[0]

    def body(d_ref, f_ref, b_ref, o_ref, db_ref):
        i = pl.program_id(0)
        rows = lax.broadcasted_iota(jnp.int32, (t, tq), 0)
        cols = i * tq + lax.broadcasted_iota(jnp.int32, (t, tq), 1)
        tri = (rows >= cols).astype(BF16)
        p0, p1, p2 = _split3(d_ref[...])
        dlogf = (_dot_nn(p0, tri) + _dot_nn(p1, tri) + _dot_nn(p2, tri)).T
        df = dlogf * _sigmoid(-(f_ref[...] + b_ref[...]))
        o_ref[...] = df.astype(BF16)
        upd = _rows8([jnp.sum(df, axis=0, keepdims=True)], LANES)

        @pl.when(i == 0)
        def _():
            db_ref[...] = upd

        @pl.when(i > 0)
        def _():
            db_ref[...] += upd

    return pl.pallas_call(
        body, name=name, grid=(t // tq,),
        in_specs=[pl.BlockSpec((LANES, t), lambda i: (0, 0)), pl.BlockSpec((tq, LANES), lambda i: (i, 0)),
                  pl.BlockSpec((1, LANES), lambda i: (0, 0))],
        out_specs=[pl.BlockSpec((tq, LANES), lambda i: (i, 0)), pl.BlockSpec((8, LANES), lambda i: (0, 0))],
        out_shape=[jax.ShapeDtypeStruct((t, LANES), BF16), jax.ShapeDtypeStruct((8, LANES), F32)],
        compiler_params=_params("arbitrary"),
    )(dct, fp, fb)


def _pair_sum(a, b, *, tm, name):
    t, d = a.shape

    def body(a_ref, b_ref, o_ref):
        o_ref[...] = (a_ref[...] + b_ref[...]).astype(BF16)

    row = pl.BlockSpec((tm, d), lambda i: (i, 0))
    return pl.pallas_call(
        body, name=name, grid=(t // tm,), in_specs=[row, row], out_specs=row,
        out_shape=jax.ShapeDtypeStruct((t, d), BF16), compiler_params=_params("parallel"),
    )(a, b)


FWD_HEAD_TILES = 2
BWD_HEAD_TILES = 1


def _head_block_width(dh, tiles):
    return tiles * LANES if tiles * LANES // dh <= 8 else LANES


def _head_masks(dh, bw):
    lane = lax.broadcasted_iota(jnp.int32, (1, bw), 1)
    return [((lane >= e * dh) & (lane < (e + 1) * dh)) for e in range(bw // dh)]


def _head_c_row(ct_blk, head):
    sub = lax.broadcasted_iota(jnp.int32, ct_blk.shape, 0)
    return jnp.sum(jnp.where(sub == head, ct_blk, 0.0), axis=0, keepdims=True)


def _attn_weights(qm, k, c_row, q0):
    tq, t = qm.shape[0], k.shape[0]
    s = _dot_nt(qm, k) - c_row
    qi = q0 + lax.broadcasted_iota(jnp.int32, (tq, t), 0)
    ki = lax.broadcasted_iota(jnp.int32, (tq, t), 1)
    s = jnp.where(ki <= qi, s, -jnp.inf)
    e = jnp.exp(s - jnp.max(s, axis=-1, keepdims=True))
    return e, 1.0 / jnp.sum(e, axis=-1, keepdims=True)


def _key_buckets(t, tq):
    return tuple(sorted({min(-(-(i * tq) // LANES) * LANES, t) for i in range(1, t // tq + 1)}))


def _for_prefix(needed, buckets, fn):
    prev = 0
    for length in buckets:
        pl.when((needed > prev) & (needed <= length))(lambda length=length: fn(length))
        prev = length


def _attn_fwd(qg, kv, ct, *, tq, name):
    t, d2 = qg.shape
    d = d2 // 2
    dh = d // N_HEADS
    bw = _head_block_width(dh, FWD_HEAD_TILES)
    hpb = bw // dh
    nhb = d // bw
    scale = dh ** -0.5
    buckets = _key_buckets(t, tq)

    def body(q_ref, og_ref, k_ref, v_ref, ct_ref, o_ref, y_ref):
        hb = pl.program_id(0)
        q0 = pl.program_id(1) * tq

        def run(length):
            qs = q_ref[...] * scale
            k = k_ref[0:length, :]
            v = v_ref[0:length, :]
            o = jnp.zeros((tq, bw), F32)
            for e, msk in enumerate(_head_masks(dh, bw)):
                c_row = _head_c_row(ct_ref[:, 0:length], hb * hpb + e)
                w, inv = _attn_weights(jnp.where(msk, qs, 0.0).astype(BF16), k, c_row, q0)
                o = o + _dot_nn(w.astype(BF16), jnp.where(msk, v, jnp.zeros_like(v))) * inv
            o_ref[...] = o
            y_ref[...] = (o * _sigmoid(og_ref[...])).astype(BF16)

        _for_prefix(q0 + tq, buckets, run)

    qblk = pl.BlockSpec((tq, bw), lambda h, i: (i, h))
    return pl.pallas_call(
        body, name=name, grid=(nhb, t // tq),
        in_specs=[qblk, pl.BlockSpec((tq, bw), lambda h, i: (i, h + nhb)),
                  pl.BlockSpec((t, bw), lambda h, i: (0, h)), pl.BlockSpec((t, bw), lambda h, i: (0, h + nhb)),
                  pl.BlockSpec((N_HEADS, t), lambda h, i: (0, 0))],
        out_specs=[qblk, pl.BlockSpec((None, tq, bw), lambda h, i: (0, i, h))],
        out_shape=[jax.ShapeDtypeStruct((t, d), F32), jax.ShapeDtypeStruct((1, t, d), BF16)],
        compiler_params=_params("parallel", "parallel"),
    )(qg, qg, kv, kv, ct)


def _attn_bwd(dy, qg, o, kv, ct, *, tq, name):
    t, d2 = qg.shape
    d = d2 // 2
    dh = d // N_HEADS
    bw = _head_block_width(dh, BWD_HEAD_TILES)
    hpb = bw // dh
    nhb = d // bw
    scale = dh ** -0.5
    buckets = _key_buckets(t, tq)

    def body(dy_ref, q_ref, og_ref, o_ref, k_ref, v_ref, ct_ref, dqg_ref, dk_ref, dv_ref, dc_ref):
        hb = pl.program_id(0)
        step = pl.program_id(1)
        q0 = step * tq

        @pl.when(step == 0)
        def _():
            dk_ref[...] = jnp.zeros((t, bw), F32)
            dv_ref[...] = jnp.zeros((t, bw), F32)
            dc_ref[...] = jnp.zeros((8, t), F32)

        def run(length):
            qs = q_ref[...] * scale
            k = k_ref[0:length, :]
            v = v_ref[0:length, :]
            sg = _sigmoid(og_ref[...])
            dyv = dy_ref[...]
            do = dyv * sg
            dqg_ref[1] = (dyv * o_ref[...] * sg * (1.0 - sg)).astype(BF16)
            dq = jnp.zeros((tq, bw), F32)
            dk = jnp.zeros((length, bw), F32)
            dv = jnp.zeros((length, bw), F32)
            dc_rows = []
            for e, msk in enumerate(_head_masks(dh, bw)):
                c_row = _head_c_row(ct_ref[:, 0:length], hb * hpb + e)
                qm = jnp.where(msk, qs, 0.0).astype(BF16)
                dom = jnp.where(msk, do, 0.0).astype(BF16)
                w, inv = _attn_weights(qm, k, c_row, q0)
                p = w * inv
                dp = _dot_nt(dom, v)
                dsc = p * (dp - jnp.sum(p * dp, axis=-1, keepdims=True))
                dsb = dsc.astype(BF16)
                dq = dq + _dot_nn(dsb, jnp.where(msk, k, jnp.zeros_like(k)))
                dk = dk + _dot_tn(dsb, qm)
                dv = dv + _dot_tn(p.astype(BF16), dom)
                dc_rows.append(-jnp.sum(dsc, axis=0, keepdims=True))
            dqg_ref[0] = (dq * scale).astype(BF16)
            dk_ref[0:length, :] += dk
            dv_ref[0:length, :] += dv
            dc_ref[:, 0:length] += _rows8(dc_rows, length)

        _for_prefix(q0 + tq, buckets, run)

    qblk = pl.BlockSpec((tq, bw), lambda h, i: (i, h))
    kblk = pl.BlockSpec((t, bw), lambda h, i: (0, h))
    return pl.pallas_call(
        body, name=name, grid=(nhb, t // tq),
        in_specs=[qblk, qblk, pl.BlockSpec((tq, bw), lambda h, i: (i, h + nhb)), qblk,
                  kblk, pl.BlockSpec((t, bw), lambda h, i: (0, h + nhb)),
                  pl.BlockSpec((N_HEADS, t), lambda h, i: (0, 0))],
        out_specs=[pl.BlockSpec((2, tq, bw), lambda h, i: (0, i, h)), kblk, kblk,
                   pl.BlockSpec((None, 8, t), lambda h, i: (h, 0, 0))],
        out_shape=[jax.ShapeDtypeStruct((2, t, d), BF16), jax.ShapeDtypeStruct((t, d), F32),
                   jax.ShapeDtypeStruct((t, d), F32), jax.ShapeDtypeStruct((nhb, 8, t), F32)],
        compiler_params=_params("parallel", "arbitrary"),
    )(dy, qg, qg, o, kv, kv, ct)


def _loss_bwd(h, tgt, *, lo, hi, tm, name):
    t, d = h.shape

    def body(h_ref, t_ref, l_ref, dy_ref):
        i = pl.program_id(0)
        rows = i * tm + lax.broadcasted_iota(jnp.int32, (tm, d), 0)
        err = jnp.where((rows >= lo) & (rows < hi), h_ref[...] - t_ref[...], 0.0)
        dy_ref[...] = err * (1.0 / d)
        part = jnp.sum(jnp.sum(err * err, axis=0, keepdims=True), axis=1, keepdims=True) * (0.5 / d)
        upd = jnp.broadcast_to(part, (8, LANES))

        @pl.when(i == 0)
        def _():
            l_ref[...] = upd

        @pl.when(i > 0)
        def _():
            l_ref[...] += upd

    row = pl.BlockSpec((tm, d), lambda i: (i, 0))
    return pl.pallas_call(
        body, name=name, grid=(t // tm,),
        in_specs=[row, row],
        out_specs=[pl.BlockSpec((8, LANES), lambda i: (0, 0)), row],
        out_shape=[jax.ShapeDtypeStruct((8, LANES), F32), jax.ShapeDtypeStruct((t, d), F32)],
        compiler_params=_params("arbitrary"),
    )(h, tgt)


def _adamw_math(w, gv, m, v):
    bc1 = 1.0 / (1.0 - ADAM_B1 ** ADAM_STEP)
    bc2 = 1.0 / (1.0 - ADAM_B2 ** ADAM_STEP)
    nm = ADAM_B1 * m + (1.0 - ADAM_B1) * gv
    nv = ADAM_B2 * v + (1.0 - ADAM_B2) * (gv * gv)
    delta = (-ADAM_LR) * ((nm * bc1) / (jnp.sqrt(nv * bc2) + ADAM_EPS) + ADAM_WD * w)
    return delta, nm, nv


def _adamw(w, g, m, v, *, name):
    r, c = w.shape
    tr = r
    for cand in (512, 256, 128, 64, 32, 16, 8):
        if r % cand == 0 and r > cand:
            tr = cand
            break

    def body(w_ref, g_ref, m_ref, v_ref, d_ref, nm_ref, nv_ref):
        d_ref[...], nm_ref[...], nv_ref[...] = _adamw_math(w_ref[...], g_ref[...], m_ref[...], v_ref[...])

    blk = pl.BlockSpec((tr, c), lambda i: (i, 0))
    out = jax.ShapeDtypeStruct((r, c), F32)
    return pl.pallas_call(
        body, name=name, grid=(r // tr,),
        in_specs=[blk] * 4, out_specs=[blk] * 3, out_shape=[out] * 3,
        compiler_params=_params("parallel"),
    )(w, g, m, v)


def _sum_adamw(recvs, sends, me, w, m, v, *, name):
    n_l = len(recvs)
    _, r, c = recvs[0].shape
    tr = _tile(r, (256, 192, 176, 128, 96, 64, 48, 32, 16))

    def body(me_ref, *refs):
        p_refs, own_refs = refs[:n_l], refs[n_l:2 * n_l]
        w_ref, m_ref, v_ref, g_ref, d_ref, nm_ref, nv_ref, acc_ref = refs[2 * n_l:]
        layer = pl.program_id(0)
        mine = me_ref[0]
        for k in range(n_l):
            @pl.when(layer == k)
            def _(k=k):
                acc_ref[...] = jnp.zeros((tr, c), F32)
                for dev in range(N_DEV):
                    @pl.when(mine == dev)
                    def _():
                        acc_ref[...] += own_refs[k][...].astype(F32)

                    @pl.when(mine != dev)
                    def _(dev=dev):
                        acc_ref[...] += p_refs[k][dev].astype(F32)
                acc = acc_ref[...]
                g_ref[...] = acc
                d_ref[...], nm_ref[...], nv_ref[...] = _adamw_math(w_ref[...], acc, m_ref[...], v_ref[...])

    p_specs = [pl.BlockSpec((N_DEV, tr, c), lambda l, i, me_ref, k=k: (0, jnp.where(l == k, i, 0), 0))
               for k in range(n_l)]
    own_specs = [pl.BlockSpec((None, tr, c), lambda l, i, me_ref, k=k: (me_ref[0], jnp.where(l == k, i, 0), 0))
                 for k in range(n_l)]
    blk = pl.BlockSpec((None, tr, c), lambda l, i, me_ref: (l, i, 0))
    out = jax.ShapeDtypeStruct((n_l, r, c), F32)
    return pl.pallas_call(
        body, name=name,
        grid_spec=pltpu.PrefetchScalarGridSpec(
            num_scalar_prefetch=1, grid=(n_l, r // tr),
            in_specs=p_specs + own_specs + [blk] * 3, out_specs=[blk] * 4,
            scratch_shapes=[pltpu.VMEM((tr, c), F32)]),
        out_shape=[out] * 4,
        compiler_params=_params("arbitrary", "arbitrary"),
    )(me, *recvs, *sends, w, m, v)


def _sum8(parts, *, name):
    _, r, c = parts.shape
    tr = r
    for cand in (512, 256, 128, 64, 32, 16):
        if r % cand == 0 and r > cand:
            tr = cand
            break

    def body(p_ref, o_ref):
        acc = p_ref[0].astype(F32)
        for k in range(1, N_DEV):
            acc = acc + p_ref[k].astype(F32)
        o_ref[...] = acc

    return pl.pallas_call(
        body, name=name, grid=(r // tr,),
        in_specs=[pl.BlockSpec((N_DEV, tr, c), lambda i: (0, i, 0))],
        out_specs=pl.BlockSpec((tr, c), lambda i: (i, 0)),
        out_shape=jax.ShapeDtypeStruct((r, c), F32),
        compiler_params=_params("parallel"),
    )(parts)


def _my_index():
    return 4 * lax.axis_index("x") + 2 * lax.axis_index("y") + lax.axis_index("c")


def _peer(k):
    x, y, c = lax.axis_index("x"), lax.axis_index("y"), lax.axis_index("c")
    px = x ^ ((k >> 2) & 1)
    py = y ^ ((k >> 1) & 1)
    pc = c ^ (k & 1)
    return (px, py, pc), 4 * px + 2 * py + pc


def _all_gather(shards, *, name):
    n_arr = len(shards)

    def body(*refs):
        ins, outs = refs[:n_arr], refs[n_arr:2 * n_arr]
        send_sems, recv_sems, local_sems = refs[2 * n_arr:]
        me = _my_index()
        local = [pltpu.make_async_copy(ins[n], outs[n].at[me], local_sems.at[n]) for n in range(n_arr)]
        for cp in local:
            cp.start()
        sends = []
        for k in range(1, N_DEV):
            peer, _ = _peer(k)
            for n in range(n_arr):
                cp = pltpu.make_async_remote_copy(
                    src_ref=ins[n], dst_ref=outs[n].at[me], send_sem=send_sems.at[n, k - 1],
                    recv_sem=recv_sems.at[n, k - 1], device_id=peer, device_id_type=pl.DeviceIdType.MESH)
                cp.start()
                sends.append(cp)
        for k in range(1, N_DEV):
            peer, pidx = _peer(k)
            for n in range(n_arr):
                pltpu.make_async_remote_copy(
                    src_ref=ins[n], dst_ref=outs[n].at[pidx], send_sem=send_sems.at[n, k - 1],
                    recv_sem=recv_sems.at[n, k - 1], device_id=peer, device_id_type=pl.DeviceIdType.MESH).wait_recv()
        for cp in sends:
            cp.wait_send()
        for cp in local:
            cp.wait()

    hbm = pl.BlockSpec(memory_space=pl.ANY)
    return pl.pallas_call(
        body, name=name,
        in_specs=[hbm] * n_arr, out_specs=[hbm] * n_arr,
        out_shape=[jax.ShapeDtypeStruct((N_DEV,) + s.shape, s.dtype) for s in shards],
        scratch_shapes=[pltpu.SemaphoreType.DMA((n_arr, N_DEV - 1)), pltpu.SemaphoreType.DMA((n_arr, N_DEV - 1)),
                        pltpu.SemaphoreType.DMA((n_arr,))],
        compiler_params=pltpu.CompilerParams(has_side_effects=True),
    )(*shards)


_HBM = pl.BlockSpec(memory_space=pltpu.HBM)
_SEM = pl.BlockSpec(memory_space=pltpu.SEMAPHORE)
_EFFECT = pltpu.SideEffectType.DATAFLOW_SIDE_EFFECTING


def _remote(src, dst, send_sem, recv_sem, peer):
    return pltpu.make_async_remote_copy(src_ref=src, dst_ref=dst, send_sem=send_sem, recv_sem=recv_sem,
                                        device_id=peer, device_id_type=pl.DeviceIdType.MESH)


def _place_own(src, layer, me, *, out_dtype, name):
    _, r, c = src.shape
    tr = _tile(r, (256, 192, 176, 128, 96, 64, 48, 32, 16))

    def body(me_ref, s_ref, o_ref):
        o_ref[...] = s_ref[...].astype(out_dtype)

    return pl.pallas_call(
        body, name=name,
        grid_spec=pltpu.PrefetchScalarGridSpec(
            num_scalar_prefetch=1, grid=(r // tr,),
            in_specs=[pl.BlockSpec((None, tr, c), lambda i, me_ref: (layer, i, 0))],
            out_specs=pl.BlockSpec((None, tr, c), lambda i, me_ref: (me_ref[0], i, 0))),
        out_shape=jax.ShapeDtypeStruct((N_DEV, r, c), out_dtype),
        compiler_params=_params("parallel"),
    )(me, src)


def _own_blocks(srcs, *, name):
    n = len(srcs)

    def body(*refs):
        ins, outs, sems = refs[:n], refs[n:2 * n], refs[2 * n]
        me = _my_index()
        cps = [pltpu.make_async_copy(ins[t].at[me], outs[t].at[me], sems.at[t]) for t in range(n)]
        for cp in cps:
            cp.start()
        for cp in cps:
            cp.wait()

    return pl.pallas_call(
        body, name=name, in_specs=[_HBM] * n, out_specs=[_HBM] * n,
        out_shape=[jax.ShapeDtypeStruct(s.shape, s.dtype) for s in srcs],
        scratch_shapes=[pltpu.SemaphoreType.DMA((n,))],
    )(*srcs)


def _split_start(groups, *, scatter, name):
    sizes = [len(srcs) for srcs, _ in groups]
    flat_src = [s for srcs, _ in groups for s in srcs]
    flat_land = [l for _, lands in groups for l in lands]
    n, n_g = len(flat_land), len(groups)
    if not scatter:
        flat_src = []
    n_in = len(flat_src) + n

    def body(*refs):
        lands = refs[n_in - n:n_in]
        ins = refs[:n] if scatter else lands
        sems = refs[n_in:n_in + 2 * n_g]
        token = refs[-1]
        me = _my_index()
        t = 0
        for g in range(n_g):
            for q in range(sizes[g]):
                for k in range(1, N_DEV):
                    peer, pidx = _peer(k)
                    src = ins[t].at[pidx] if scatter else ins[t].at[me]
                    slot = q * (N_DEV - 1) + k - 1
                    _remote(src, lands[t].at[me], sems[2 * g].at[slot], sems[2 * g + 1].at[slot], peer).start()
                t += 1
        token[...] = jnp.zeros_like(token)

    sem_shapes = []
    for sz in sizes:
        sem_shapes += [pltpu.SemaphoreType.DMA((sz * (N_DEV - 1),)), pltpu.SemaphoreType.DMA((sz * (N_DEV - 1),))]
    outs = pl.pallas_call(
        body, name=name,
        in_specs=[_HBM] * n_in,
        out_specs=[_SEM] * (2 * n_g) + [_HBM] * n_in + [pl.BlockSpec(memory_space=pltpu.VMEM)],
        out_shape=sem_shapes + [pltpu.HBM(a.shape, a.dtype) for a in flat_src + flat_land]
        + [jax.ShapeDtypeStruct((8, LANES), F32)],
        input_output_aliases={i: 2 * n_g + i for i in range(n_in)},
        compiler_params=pltpu.CompilerParams(has_side_effects=_EFFECT),
    )(*[pltpu.with_memory_space_constraint(a, pltpu.HBM) for a in flat_src + flat_land])
    sems, thru, token = outs[:2 * n_g], outs[2 * n_g:2 * n_g + n_in], outs[-1]
    handles, pos = [], 0
    for g, sz in enumerate(sizes):
        lands_g = thru[n_in - n + pos:n_in - n + pos + sz]
        handles.append((sems[2 * g], sems[2 * g + 1], thru[pos:pos + sz] if scatter else [], lands_g))
        pos += sz
    return handles, token


def _split_wait(handle, after, *, scatter, name):
    send_sems, recv_sems, srcs, lands = handle
    n, n_src = len(lands), len(srcs)

    def body(*refs):
        lnd = refs[n_src:n_src + n]
        ins = refs[:n_src] if scatter else lnd
        ssem, rsem = refs[n_src + n], refs[n_src + n + 1]
        me = _my_index()
        for t in range(n):
            for k in range(1, N_DEV):
                peer, pidx = _peer(k)
                block = ins[t].at[me]
                slot = t * (N_DEV - 1) + k - 1
                _remote(block, lnd[t].at[me], ssem.at[slot], rsem.at[slot], peer).wait_send()
                _remote(block, lnd[t].at[pidx], ssem.at[slot], rsem.at[slot], peer).wait_recv()

    return pl.pallas_call(
        body, name=name,
        in_specs=[_HBM] * (n_src + n) + [_SEM, _SEM, pl.BlockSpec(memory_space=pl.ANY)],
        out_specs=[_HBM] * n,
        out_shape=[pltpu.HBM(l.shape, l.dtype) for l in lands],
        input_output_aliases={n_src + t: t for t in range(n)},
        compiler_params=pltpu.CompilerParams(has_side_effects=_EFFECT),
    )(*srcs, *lands, send_sems, recv_sems, after)


def _pack(arrs, dtype, row_quantum=16):
    flat = jnp.concatenate([a.astype(dtype).reshape(-1) for a in arrs])
    pad = (-flat.shape[0]) % (row_quantum * PACK_COLS)
    if pad:
        flat = jnp.concatenate([flat, jnp.zeros((pad,), dtype)])
    return flat.reshape(-1, PACK_COLS)


def _pack8(arrs, dtype):
    flat = jnp.concatenate([a.astype(dtype).reshape(N_DEV, -1) for a in arrs], axis=1)
    pad = (-flat.shape[1]) % (16 * PACK_COLS)
    if pad:
        flat = jnp.concatenate([flat, jnp.zeros((N_DEV, pad), dtype)], axis=1)
    return flat.reshape(N_DEV, -1, PACK_COLS)


def _unpack(slab, shapes, lead):
    lead_shape = slab.shape[:lead]
    flat = slab.reshape(lead_shape + (-1,))
    outs, off = [], 0
    for shp in shapes:
        size = math.prod(shp)
        outs.append(flat[..., off:off + size].reshape(lead_shape + tuple(shp)))
        off += size
    return outs


def _cols_full(g):
    g = jnp.moveaxis(g, 0, -2)
    return g.reshape(g.shape[:-2] + (g.shape[-2] * g.shape[-1],))


def _cols_split(full):
    n = full.shape[-1] // N_DEV
    return jnp.moveaxis(full.reshape(full.shape[:-1] + (N_DEV, n)), -2, 0)


def _block_diag(w, per):
    n, b, _ = w.shape
    w4 = w.reshape(n // per, per, b, b)
    eye = jnp.eye(per, dtype=w.dtype)
    return jnp.einsum('gpab,pq->gpaqb', w4, eye).reshape(n // per, per * b, per * b)


def _block_diag_extract(g, per):
    gn, cb, _ = g.shape
    b = cb // per
    g5 = g.reshape(gn, per, b, per, b)
    return jnp.stack([g5[:, p, :, p, :] for p in range(per)], axis=1).reshape(gn * per, b, b)


def _slab2d(a):
    return a.reshape(-1, a.shape[-1])


def _lru_block_cols(r_dim):
    lru = r_dim // N_LRU_BLOCKS
    return lru * LANES // math.gcd(lru, LANES)


BIG = ("a_w_in", "a_w_out", "b_w_in", "b_w_out", "f_w_in", "f_w_out")
COL_F32 = ("meta", "a_conv_w", "a_conv_b", "a_b_r", "a_b_i", "a_lambda", "f_conv_w")
REPLICATED = ("a_w_r", "a_w_i", "kv_f_b", "f_conv_b", "ln1_g", "ln1_b", "ln2_g", "ln2_b")
WEIGHT_NAMES = ("meta", "a_w_in", "a_conv_w", "a_conv_b", "a_w_r", "a_b_r", "a_w_i", "a_b_i", "a_lambda", "a_w_out",
                "kv_w", "kv_f_b", "b_w_in", "b_w_out", "f_w_in", "f_conv_w", "f_conv_b", "f_w_out",
                "ln1_g", "ln1_b", "ln2_g", "ln2_b")


def _kv_layout(kv_gathered, d):
    kv_full = _cols_full(kv_gathered)
    kv_pad = 2 * d + LANES - kv_full.shape[1]
    return jnp.concatenate([kv_full, jnp.zeros((d, kv_pad), kv_full.dtype)], axis=1)


def _small_layouts(small):
    r_dim = small["a_lambda"].shape[1]
    n_f = small["f_conv_b"].shape[1] // N_DEV
    cb = _lru_block_cols(r_dim)
    per = cb // (r_dim // N_LRU_BLOCKS)
    n_a = small["a_lambda"].shape[0]
    f_conv_w3 = small["f_conv_w"].reshape(N_LAYERS, 3, N_DEV, n_f).transpose(0, 2, 1, 3)
    f_conv_b3 = small["f_conv_b"].reshape(N_LAYERS, N_DEV, 1, n_f)
    return {
        "kv_fb": jnp.concatenate([small["kv_f_b"], jnp.zeros((LANES - N_HEADS,), F32)])[None],
        "a_cwb": jnp.concatenate([small["a_conv_w"], small["a_conv_b"][:, None],
                                  jnp.zeros((n_a, 3, r_dim), F32)], axis=1),
        "a_vecs": jnp.concatenate([jnp.stack([small["a_b_r"], small["a_b_i"], small["a_lambda"]], axis=1),
                                   jnp.zeros((n_a, 5, r_dim), F32)], axis=1),
        "a_bd_r": jnp.stack([_block_diag(small["a_w_r"][l], per) for l in range(n_a)]).astype(BF16),
        "a_bd_i": jnp.stack([_block_diag(small["a_w_i"][l], per) for l in range(n_a)]).astype(BF16),
        "f_cwb3": jnp.concatenate([f_conv_w3, f_conv_b3, jnp.zeros((N_LAYERS, N_DEV, 4, n_f), F32)], axis=2),
        "ln1_g": small["ln1_g"][:, None], "ln1_b": small["ln1_b"][:, None],
        "ln2_g": small["ln2_g"][:, None], "ln2_b": small["ln2_b"][:, None],
    }


def _local_step(h0, tgt, n_meta, n_tok, wts, hooks):
    tp, d = h0.shape
    tm = tp // 8 if (tp // 8) % 16 == 0 else tp
    tmb = _tile(tp, (1088, 512, 320, 256, 128))
    tq = 128
    tqa_fwd = tp // 4 if tp % 64 == 0 else tq
    tqa_bwd = tp // 8 if tp % 128 == 0 else tq
    r_dim = wts["a_vecs"].shape[2]
    cb = wts["a_bd_r"].shape[-1]
    sb = LANES
    n_b = N_LAYERS - N_A_LAYERS

    h, h_bf = h0, h0.astype(BF16)
    saved = []
    kvs = None
    for layer in range(N_LAYERS):
        lw = {}
        sv = {"h_bf": h_bf, "w": lw}
        if layer < N_A_LAYERS:
            lw["in"] = hooks.weight(layer, "in", h)
            sv["gr"] = _proj_in(h_bf, lw["in"], shard_major=False, name="a_in_proj")
            sv["rec"] = _conv_a_fwd(sv["gr"], wts["a_cwb"][layer], cb=cb, name="a_conv_fwd")
            a, u, sv["r"], sv["i"] = _gates_fwd(sv["rec"], wts["a_bd_r"][layer], wts["a_bd_i"][layer],
                                                wts["a_vecs"][layer], tm=tm, name="a_gates_fwd")
            sv["a"] = a
            sv["hr"], y3 = _scan_fwd(a, u, sv["gr"], cb=sb, name="a_scan_fwd")
        else:
            j = layer - N_A_LAYERS
            if j == 0:
                kv_w = _kv_layout(hooks.weight(layer, "kv_w", h), d)
                kvs = {"h_bf": h_bf, "w": kv_w}
                kvs["kv"] = _mm_nn(h_bf, kv_w[:, :2 * d], tn=_tile(2 * d, (512, 256, 128)), out_dtype=BF16,
                                   name="kv_proj")
                kvs["fp"] = _mm_nn(h_bf, kv_w[:, 2 * d:], tn=LANES, out_dtype=F32, name="f_proj")
                kvs["c"], ct = _fgate_fwd(kvs["fp"], wts["kv_fb"], tq=tq, name="fgate_fwd")
                kvs["ct"] = ct[:N_HEADS]
            lw["in"] = hooks.weight(layer, "in", kvs["c"] if j == 0 else h)
            sv["qg"] = _proj_in(h_bf, lw["in"], shard_major=False, name="b_in_proj")
            sv["o"], y3 = _attn_fwd(sv["qg"], kvs["kv"], kvs["ct"], tq=tqa_fwd, name="attn_fwd")
        sv["y3"] = y3
        lw["out"] = hooks.weight(layer, "out", y3)
        sv["s1"], h, h_bf = _out_ln(y3, lw["out"], h, wts["ln1_g"][layer], wts["ln1_b"][layer], n_valid=n_tok,
                                    tm=tm, name="mix_out_ln")
        sv["h1_bf"] = h_bf
        lw["f_in"] = hooks.weight(layer, "f_in", h)
        sv["z3"] = _proj_in(h_bf, lw["f_in"], shard_major=True, transposed=True, name="f_in_proj")
        sv["yf3"] = _convglu_fwd(sv["z3"], wts["f_cwb3"][layer], name="f_convglu_fwd")
        lw["f_out"] = hooks.weight(layer, "f_out", sv["yf3"])
        sv["s2"], h, h_bf = _out_ln(sv["yf3"], lw["f_out"], h, wts["ln2_g"][layer], wts["ln2_b"][layer],
                                    n_valid=n_tok, tm=tm, name="ffn_out_ln")
        saved.append(sv)

    loss_tile, dh = _loss_bwd(h, tgt, lo=n_meta, hi=n_tok, tm=tm, name="loss")

    grads = {k: [None] * N_LAYERS for k in ("f_cwb3", "ln1_gb", "ln2_gb")}
    grads.update({k: [None] * N_A_LAYERS for k in ("a_cwb", "a_bd_r", "a_bd_i", "a_vecs")})
    dkv = []
    token = jnp.zeros((), F32)
    for layer in reversed(range(N_LAYERS)):
        sv = saved[layer]
        lw = sv["w"]
        big = {}
        ds, ds_bf, grads["ln2_gb"][layer] = _ln_bwd(dh, sv["s2"], wts["ln2_g"][layer] + token, tm=tm, name="ln_bwd")
        dz, dcw = _ffn_bwd_mid(ds_bf, lw["f_out"], sv["z3"], wts["f_cwb3"][layer], name="f_bwd_mid")
        grads["f_cwb3"][layer] = dcw.reshape((N_DEV,) + dcw.shape[2:])
        dz3 = dz
        big["f_out"] = _w_out_grad(sv["yf3"], ds_bf, lw["f_out"].shape[1], name="f_w_out_grad")
        dh = _in_bwd(dz3, lw["f_in"], ds, tm=tmb, transposed=True, name="f_in_bwd")
        big["f_in"] = _w_in_grad(sv["h1_bf"], dz3, transposed=True, name="f_w_in_grad")
        token = hooks.grads_ready(layer, "ffn", big)
        big = {}
        ds, ds_bf, grads["ln1_gb"][layer] = _ln_bwd(dh, sv["s1"], wts["ln1_g"][layer] + token, tm=tm, name="ln_bwd")
        if layer < N_A_LAYERS:
            dy = _out_bwd(ds_bf, lw["out"], tm=tmb // 2, name="a_out_bwd")
            big["out"] = _w_out_grad(sv["y3"], ds_bf, lw["out"].shape[1], name="a_w_out_grad")
            d_h, d_a, dgate = _scan_bwd(dy, sv["gr"], sv["hr"], sv["a"], cb=sb, name="a_scan_bwd")
            d_rec, dpr, dpi, grads["a_vecs"][layer] = _gates_bwd(
                sv["rec"], sv["r"], sv["i"], sv["a"], d_h, d_a, wts["a_bd_r"][layer], wts["a_bd_i"][layer],
                wts["a_vecs"][layer], tm=tm, name="a_gates_bwd")
            grads["a_bd_r"][layer], grads["a_bd_i"][layer] = _bd_grad(sv["rec"], dpr, dpi, cb=cb, name="a_bd_grad")
            dact, grads["a_cwb"][layer] = _conv_a_bwd(d_rec, sv["gr"], dgate, wts["a_cwb"][layer], cb=cb,
                                                      name="a_conv_bwd")
            dh = _in_bwd(dact, lw["in"], ds, tm=tmb, name="a_in_bwd")
            big["in"] = _w_in_grad(sv["h_bf"], dact, name="a_w_in_grad")
        else:
            j = layer - N_A_LAYERS
            dy = _out_bwd(ds_bf, lw["out"], tm=tmb // 2, name="b_out_bwd")
            big["out"] = _w_out_grad(sv["y3"], ds_bf, lw["out"].shape[1], name="b_w_out_grad")
            dqg, dk, dv, dc = _attn_bwd(dy, sv["qg"], sv["o"], kvs["kv"], kvs["ct"], tq=tqa_bwd,
                                        name="attn_bwd")
            dkv.append((dk, dv, dc))
            dh = _in_bwd(dqg, lw["in"], ds, tm=tmb, name="b_in_bwd")
            big["in"] = _w_in_grad(sv["h_bf"], dqg, name="b_w_in_grad")
            if j == 0:
                hpb = _head_block_width(d // N_HEADS, BWD_HEAD_TILES) // (d // N_HEADS)
                dct = (dkv[0][2] + dkv[1][2])[:, :hpb, :].reshape(N_HEADS, tp)
                dct = jnp.concatenate([dct, jnp.zeros((LANES - N_HEADS, tp), F32)])
                df_bf, grads["kv_fb"] = _fgate_bwd(dct, kvs["fp"], wts["kv_fb"], tq=tq, name="fgate_bwd")
                dkvz = jnp.concatenate([_pair_sum(dkv[0][0], dkv[1][0], tm=tm, name="kv_pair_sum"),
                                        _pair_sum(dkv[0][1], dkv[1][1], tm=tm, name="kv_pair_sum"), df_bf], axis=1)
                dh = _mm_nt_full(dkvz, kvs["w"], dh, tm=tmb // 2, name="kv_in_bwd")
                big["kv_w"] = _mm_tn_cols(kvs["h_bf"], dkvz, tn=LANES, name="kv_w_grad")
        token = hooks.grads_ready(layer, "mix", big)
    return loss_tile, dh, grads


def _finish_small_grads(grads, d_h0, n_meta):
    r_dim = grads["a_vecs"][0].shape[1]
    per = _lru_block_cols(r_dim) // (r_dim // N_LRU_BLOCKS)
    a_cwb = jnp.stack(grads["a_cwb"])
    a_vecs = jnp.stack(grads["a_vecs"])
    f_cwb3 = jnp.stack(grads["f_cwb3"])
    ln1 = jnp.stack(grads["ln1_gb"])
    ln2 = jnp.stack(grads["ln2_gb"])
    f_rows = f_cwb3.transpose(0, 2, 1, 3).reshape(N_LAYERS, 8, -1)
    return {
        "meta": d_h0[:n_meta],
        "a_conv_w": a_cwb[:, :4], "a_conv_b": a_cwb[:, 4],
        "a_w_r": jnp.stack([_block_diag_extract(g, per) for g in grads["a_bd_r"]]),
        "a_b_r": a_vecs[:, 0],
        "a_w_i": jnp.stack([_block_diag_extract(g, per) for g in grads["a_bd_i"]]),
        "a_b_i": a_vecs[:, 1], "a_lambda": a_vecs[:, 2],
        "kv_f_b": grads["kv_fb"][0, :N_HEADS],
        "f_conv_w": f_rows[:, :3], "f_conv_b": f_rows[:, 3],
        "ln1_g": ln1[:, 0], "ln1_b": ln1[:, 1], "ln2_g": ln2[:, 0], "ln2_b": ln2[:, 1],
    }


def kernel(x, meta, a_w_in, a_conv_w, a_conv_b, a_w_r, a_b_r, a_w_i, a_b_i, a_lambda, a_w_out, kv_w, kv_f_b, b_w_in, b_w_out, f_w_in, f_conv_w, f_conv_b, f_w_out, ln1_g, ln1_b, ln2_g, ln2_b, loss_target, m_meta, m_a_w_in, m_a_conv_w, m_a_conv_b, m_a_w_r, m_a_b_r, m_a_w_i, m_a_b_i, m_a_lambda, m_a_w_out, m_kv_w, m_kv_f_b, m_b_w_in, m_b_w_out, m_f_w_in, m_f_conv_w, m_f_conv_b, m_f_w_out, m_ln1_g, m_ln1_b, m_ln2_g, m_ln2_b, v_meta, v_a_w_in, v_a_conv_w, v_a_conv_b, v_a_w_r, v_a_b_r, v_a_w_i, v_a_b_i, v_a_lambda, v_a_w_out, v_kv_w, v_kv_f_b, v_b_w_in, v_b_w_out, v_f_w_in, v_f_conv_w, v_f_conv_b, v_f_w_out, v_ln1_g, v_ln1_b, v_ln2_g, v_ln2_b):
    w = dict(meta=meta, a_w_in=a_w_in, a_conv_w=a_conv_w, a_conv_b=a_conv_b, a_w_r=a_w_r, a_b_r=a_b_r, a_w_i=a_w_i,
             a_b_i=a_b_i, a_lambda=a_lambda, a_w_out=a_w_out, kv_w=kv_w, kv_f_b=kv_f_b, b_w_in=b_w_in,
             b_w_out=b_w_out, f_w_in=f_w_in, f_conv_w=f_conv_w, f_conv_b=f_conv_b, f_w_out=f_w_out, ln1_g=ln1_g,
             ln1_b=ln1_b, ln2_g=ln2_g, ln2_b=ln2_b)
    m = dict(meta=m_meta, a_w_in=m_a_w_in, a_conv_w=m_a_conv_w, a_conv_b=m_a_conv_b, a_w_r=m_a_w_r, a_b_r=m_a_b_r,
             a_w_i=m_a_w_i, a_b_i=m_a_b_i, a_lambda=m_a_lambda, a_w_out=m_a_w_out, kv_w=m_kv_w, kv_f_b=m_kv_f_b,
             b_w_in=m_b_w_in, b_w_out=m_b_w_out, f_w_in=m_f_w_in, f_conv_w=m_f_conv_w, f_conv_b=m_f_conv_b,
             f_w_out=m_f_w_out, ln1_g=m_ln1_g, ln1_b=m_ln1_b, ln2_g=m_ln2_g, ln2_b=m_ln2_b)
    v = dict(meta=v_meta, a_w_in=v_a_w_in, a_conv_w=v_a_conv_w, a_conv_b=v_a_conv_b, a_w_r=v_a_w_r, a_b_r=v_a_b_r,
             a_w_i=v_a_w_i, a_b_i=v_a_b_i, a_lambda=v_a_lambda, a_w_out=v_a_w_out, kv_w=v_kv_w, kv_f_b=v_kv_f_b,
             b_w_in=v_b_w_in, b_w_out=v_b_w_out, f_w_in=v_f_w_in, f_conv_w=v_f_conv_w, f_conv_b=v_f_conv_b,
             f_w_out=v_f_w_out, ln1_g=v_ln1_g, ln1_b=v_ln1_b, ln2_g=v_ln2_g, ln2_b=v_ln2_b)
    shapes = {n: w[n].shape for n in WEIGHT_NAMES}

    me = jnp.reshape(_my_index(), (1,)).astype(jnp.int32)

    def as_stored(name, a):
        return jnp.swapaxes(a, 1, 2) if name == "f_w_in" else a

    param_of = {"in": ("a_w_in", "b_w_in"), "out": ("a_w_out", "b_w_out"), "f_in": ("f_w_in",) * 2,
                "f_out": ("f_w_out",) * 2}
    order = [("small", None, None)]
    for layer in range(N_LAYERS):
        if layer == N_A_LAYERS:
            order.append(("kv_w", layer, 0))
        for key in ("in", "out", "f_in", "f_out"):
            order.append((key, layer, layer if key[0] == "f" or layer < N_A_LAYERS else layer - N_A_LAYERS))
    def place(key, layer, idx):
        if key == "small":
            return _place_own(_pack([w[n] for n in COL_F32], F32)[None], 0, me, out_dtype=F32, name="place_small")
        if key == "kv_w":
            return _place_own(w["kv_w"][None], 0, me, out_dtype=BF16, name="place_kv_w")
        name = param_of[key][0 if layer < N_A_LAYERS else 1]
        return _place_own(as_stored(name, w[name]), idx, me, out_dtype=BF16, name=f"place_{name}_{idx}")

    lands = [place(*o) for o in order]
    gather_handles, gather_token = _split_start([([l], [l]) for l in lands], scatter=False, name="gather_start")
    group_of = {(key, layer): g for g, (key, layer, _) in enumerate(order)}
    (got_s,) = _split_wait(gather_handles[0], gather_token, scatter=False, name="gather_wait_small")
    small = {n: w[n] for n in REPLICATED}
    for n, part in zip(COL_F32, _unpack(got_s, [w[n].shape for n in COL_F32], 1)):
        small[n] = _cols_full(part)
    n_meta, d = small["meta"].shape

    class Hooks:
        pending = None
        received = {}
        sent = {}

        @staticmethod
        def weight(layer, key, after):
            (got,) = _split_wait(gather_handles[group_of[(key, layer)]], after, scatter=False,
                                 name=f"gather_wait_{key}_{layer}")
            return got

        @staticmethod
        def collect(after):
            if Hooks.pending is not None:
                tag, names, handle = Hooks.pending
                got = _split_wait(handle, after, scatter=True, name=f"scatter_wait_{tag}")
                Hooks.received.update(zip(names, got))
                Hooks.pending = None

        @staticmethod
        def grads_ready(layer, part, big):
            if "kv_w" in big:
                big["kv_w"] = _cols_split(big["kv_w"][:, :shapes["kv_w"][1] * N_DEV]).astype(BF16)
            names = [(key, layer) for key in big]
            send = [big[key] for key in big]
            Hooks.collect(send[0])
            empty = [lax.empty(s.shape, s.dtype) for s in send]
            handles, token = _split_start([(send, empty)], scatter=True, name=f"scatter_start_{part}_{layer}")
            Hooks.pending = (f"{part}_{layer}", names, handles[0])
            Hooks.sent.update(zip(names, handles[0][2]))
            return token[0, 0]

    Hooks.pending, Hooks.received, Hooks.sent = None, {}, {}

    n_tok = n_meta + x.shape[1]
    tp = -(-n_tok // ROW_ALIGN) * ROW_ALIGN
    pad = jnp.zeros((tp - n_tok, d), F32)
    h0 = jnp.concatenate([small["meta"], x[0], pad])
    tgt = jnp.concatenate([jnp.zeros((n_meta, d), F32), loss_target[0], pad])
    loss_tile, d_h0, grads = _local_step(h0, tgt, n_meta, n_tok, _small_layouts(small), Hooks)
    g_small = _finish_small_grads(grads, d_h0, n_meta)
    loss = lax.psum(loss_tile[0, 0], MESH_AXES)
    grad_x = d_h0[n_meta:n_tok][None]

    rep = _pack([g_small[n] for n in REPLICATED], F32, row_quantum=16 * N_DEV)
    send = [_pack8([_cols_split(g_small[n]) for n in COL_F32], F32), rep.reshape(N_DEV, -1, PACK_COLS)]
    lands = _own_blocks(send, name="scatter_own_small")
    handles, token = _split_start([(send, lands)], scatter=True, name="scatter_start_small")

    g, delta, new_m, new_v = {}, {}, {}, {}
    layers_of = {
        "a_w_in": [("in", l) for l in range(N_A_LAYERS)], "a_w_out": [("out", l) for l in range(N_A_LAYERS)],
        "b_w_in": [("in", l) for l in range(N_A_LAYERS, N_LAYERS)],
        "b_w_out": [("out", l) for l in range(N_A_LAYERS, N_LAYERS)],
        "f_w_in": [("f_in", l) for l in range(N_LAYERS)], "f_w_out": [("f_out", l) for l in range(N_LAYERS)],
        "kv_w": [("kv_w", N_A_LAYERS)],
    }
    ready = [n for n in BIG + ("kv_w",) if all(t in Hooks.received for t in layers_of[n])]

    def done(names):
        return jnp.stack([g[n][(0,) * g[n].ndim] for n in names])

    for n in ready + [n for n in BIG + ("kv_w",) if n not in ready]:
        if n not in ready and Hooks.pending is not None:
            Hooks.collect(done(ready))
        lift = (lambda a: a[None]) if n == "kv_w" else (lambda a, n=n: as_stored(n, a))
        outs = _sum_adamw([Hooks.received[t] for t in layers_of[n]], [Hooks.sent[t] for t in layers_of[n]], me,
                          lift(w[n]), lift(m[n]), lift(v[n]), name="sum_adamw_" + n)
        g[n], delta[n], new_m[n], new_v[n] = [as_stored(n, o).reshape(shapes[n]) for o in outs]
    recv_s, recv_r = _split_wait(handles[0], done(BIG + ("kv_w",)), scatter=True, name="scatter_wait_small")
    sum_s = _sum8(recv_s, name="sum_grads_f32")
    g.update(zip(COL_F32, _unpack(sum_s, [shapes[n] for n in COL_F32], 0)))
    (got_r,) = _all_gather([_sum8(recv_r, name="sum_grads_replicated")], name="gather_replicated_sums")
    g.update(zip(REPLICATED, _unpack(got_r.reshape(-1, PACK_COLS), [shapes[n] for n in REPLICATED], 0)))

    for n in COL_F32 + REPLICATED:
        shp = shapes[n]
        dl, nm, nv = _adamw(_slab2d(w[n]), _slab2d(g[n]), _slab2d(m[n]), _slab2d(v[n]), name="adamw")
        delta[n], new_m[n], new_v[n] = dl.reshape(shp), nm.reshape(shp), nv.reshape(shp)
    return (loss, grad_x, *[g[n] for n in WEIGHT_NAMES], *[delta[n] for n in WEIGHT_NAMES],
            *[new_m[n] for n in WEIGHT_NAMES], *[new_v[n] for n in WEIGHT_NAMES])
```

```python
import math

import jax
import jax.numpy as jnp
from jax import lax
from jax.experimental import pallas as pl
from jax.experimental.pallas import tpu as pltpu

F32 = jnp.float32
BF16 = jnp.bfloat16

N_DEV = 8
MESH_AXES = ("x", "y", "c")
N_LAYERS = 4
N_A_LAYERS = 2
N_LRU_BLOCKS = 16
N_HEADS = 16
LRU_C = 8.0
DN_ALPHA = (2 * N_LAYERS) ** 0.25
LN_EPS = 1e-5
ADAM_LR, ADAM_B1, ADAM_B2, ADAM_EPS, ADAM_WD, ADAM_STEP = 0.001, 0.9, 0.999, 1e-08, 0.01, 10

LANES = 128
SUBLANES = 8
ROW_ALIGN = 128
VMEM_LIMIT_BYTES = 56 * 1024 * 1024
GELU_K = math.sqrt(2.0 / math.pi)
GELU_C = 0.044715
PACK_COLS = 1024


def _params(*sem):
    return pltpu.CompilerParams(dimension_semantics=sem, vmem_limit_bytes=VMEM_LIMIT_BYTES)


def _gelu(x):
    th = jnp.tanh(GELU_K * (x + GELU_C * x * x * x))
    return 0.5 * x * (1.0 + th)


def _gelu_and_grad(x):
    x2 = x * x
    th = jnp.tanh(GELU_K * (x + GELU_C * x2 * x))
    g = 0.5 * x * (1.0 + th)
    dg = 0.5 * (1.0 + th) + 0.5 * x * (1.0 - th * th) * (GELU_K * (1.0 + 3.0 * GELU_C * x2))
    return g, dg


def _sigmoid(x):
    return 1.0 / (1.0 + jnp.exp(-x))


def _expm1(x):
    small = x * (1.0 + 0.5 * x * (1.0 + (1.0 / 3.0) * x * (1.0 + 0.25 * x)))
    return jnp.where(jnp.abs(x) < 1e-2, small, jnp.exp(x) - 1.0)


def _softplus(x):
    e = jnp.exp(-jnp.abs(x))
    small = e * (1.0 - 0.5 * e * (1.0 - (2.0 / 3.0) * e))
    return jnp.maximum(x, 0.0) + jnp.where(e < 1e-2, small, jnp.log(1.0 + e))


def _shift_down(x, s):
    if s == 0:
        return x
    rows = lax.broadcasted_iota(jnp.int32, x.shape, 0)
    return jnp.where(rows >= s, pltpu.roll(x, s, 0), 0.0)


def _shift_up(x, s):
    if s == 0:
        return x
    n = x.shape[0]
    rows = lax.broadcasted_iota(jnp.int32, x.shape, 0)
    return jnp.where(rows < n - s, pltpu.roll(x, n - s, 0), 0.0)


def _dot_nn(a, b):
    return lax.dot_general(a, b, (((1,), (0,)), ((), ())), preferred_element_type=F32)


def _dot_nt(a, b):
    return lax.dot_general(a, b, (((1,), (1,)), ((), ())), preferred_element_type=F32)


def _dot_tn(a, b):
    return lax.dot_general(a, b, (((0,), (0,)), ((), ())), preferred_element_type=F32)


def _rows8(vals, width):
    rows = lax.broadcasted_iota(jnp.int32, (8, width), 0)
    out = jnp.zeros((8, width), F32)
    for k, v in enumerate(vals):
        out = jnp.where(rows == k, jnp.broadcast_to(v, (8, width)), out)
    return out


def _tile(n, prefer):
    for c in prefer:
        if n % c == 0:
            return c
    return n


def _mm_nn(a, b, *, tn, out_dtype, name):
    m, k = a.shape
    n = b.shape[1]

    def body(a_ref, b_ref, o_ref):
        o_ref[...] = _dot_nn(a_ref[...], b_ref[...]).astype(o_ref.dtype)

    return pl.pallas_call(
        body, name=name, grid=(n // tn,),
        in_specs=[pl.BlockSpec((m, k), lambda j: (0, 0)), pl.BlockSpec((k, tn), lambda j: (0, j))],
        out_specs=pl.BlockSpec((m, tn), lambda j: (0, j)),
        out_shape=jax.ShapeDtypeStruct((m, n), out_dtype),
        compiler_params=_params("parallel"),
    )(a, b)


def _proj_in(h_bf, g_in, *, shard_major, name, transposed=False):
    t, k = h_bf.shape
    n = g_in.shape[1] if transposed else g_in.shape[2]

    def body(a_ref, b_ref, o_ref):
        o_ref[...] = _dot_nt(a_ref[...], b_ref[...]) if transposed else _dot_nn(a_ref[...], b_ref[...])

    if shard_major:
        out_spec = pl.BlockSpec((None, t, n), lambda j: (j, 0, 0))
        out_shape = jax.ShapeDtypeStruct((N_DEV, t, n), F32)
    else:
        out_spec = pl.BlockSpec((t, n), lambda j: (0, j))
        out_shape = jax.ShapeDtypeStruct((t, N_DEV * n), F32)
    return pl.pallas_call(
        body, name=name, grid=(N_DEV,),
        in_specs=[pl.BlockSpec((t, k), lambda j: (0, 0)),
                  pl.BlockSpec((None,) + g_in.shape[1:], lambda j: (j, 0, 0))],
        out_specs=out_spec, out_shape=out_shape,
        compiler_params=_params("parallel"),
    )(h_bf, g_in)


def _out_ln(y3, g_out, hin, g, b, *, n_valid, tm, name):
    nj, t, kj = y3.shape
    _, r, d = g_out.shape

    def body(y_ref, w_ref, hin_ref, g_ref, b_ref, s_ref, h_ref, hb_ref):
        w = w_ref[...].reshape(N_DEV * r, d)
        s = DN_ALPHA * hin_ref[...]
        for jj in range(nj):
            s = s + _dot_nn(y_ref[jj], w[jj * kj:(jj + 1) * kj])
        mu = jnp.mean(s, axis=-1, keepdims=True)
        xc = s - mu
        var = jnp.mean(xc * xc, axis=-1, keepdims=True)
        h = xc * lax.rsqrt(var + LN_EPS) * g_ref[...] + b_ref[...]
        s_ref[...] = s
        h_ref[...] = h
        rows = pl.program_id(0) * tm + lax.broadcasted_iota(jnp.int32, (tm, d), 0)
        hb_ref[...] = jnp.where(rows < n_valid, h, 0.0).astype(BF16)

    row = pl.BlockSpec((tm, d), lambda i: (i, 0))
    vec = pl.BlockSpec((1, d), lambda i: (0, 0))
    return pl.pallas_call(
        body, name=name, grid=(t // tm,),
        in_specs=[pl.BlockSpec((nj, tm, kj), lambda i: (0, i, 0)),
                  pl.BlockSpec((N_DEV, r, d), lambda i: (0, 0, 0)), row, vec, vec],
        out_specs=[row, row, row],
        out_shape=[jax.ShapeDtypeStruct((t, d), F32), jax.ShapeDtypeStruct((t, d), F32),
                   jax.ShapeDtypeStruct((t, d), BF16)],
        compiler_params=_params("parallel"),
    )(y3, g_out, hin, g, b)


def _out_bwd(ds_bf, g_out, *, tm, name):
    t, d = ds_bf.shape
    r = g_out.shape[1]

    def body(a_ref, w_ref, o_ref):
        o_ref[...] = _dot_nt(a_ref[...], w_ref[...].reshape(N_DEV * r, d))

    return pl.pallas_call(
        body, name=name, grid=(t // tm,),
        in_specs=[pl.BlockSpec((tm, d), lambda i: (i, 0)),
                  pl.BlockSpec((N_DEV, r, d), lambda i: (0, 0, 0))],
        out_specs=pl.BlockSpec((tm, N_DEV * r), lambda i: (i, 0)),
        out_shape=jax.ShapeDtypeStruct((t, N_DEV * r), F32),
        compiler_params=_params("parallel"),
    )(ds_bf, g_out)


def _in_bwd(dact, g_in, add, *, tm, name, alpha=DN_ALPHA, transposed=False):
    t = dact.shape[-2]
    _, k, n = g_in.shape
    if transposed:
        k, n = n, k
    halves = dact.shape[0] == 2 and dact.ndim == 3
    per = N_DEV // 2

    def body(a_ref, b_ref, add_ref, o_ref, acc_ref):
        j = pl.program_id(1)

        @pl.when(j == 0)
        def _():
            acc_ref[...] = alpha * add_ref[...]

        acc_ref[...] += _dot_nn(a_ref[...], b_ref[...]) if transposed else _dot_nt(a_ref[...], b_ref[...])

        @pl.when(j == N_DEV - 1)
        def _():
            o_ref[...] = acc_ref[...]

    if halves:
        a_spec = pl.BlockSpec((None, tm, n), lambda i, j: (j // per, i, j % per))
    elif dact.ndim == 4:
        a_spec = pl.BlockSpec((None, None, tm, n), lambda i, j: (j // per, j % per, i, 0))
    else:
        a_spec = pl.BlockSpec((None, tm, n), lambda i, j: (j, i, 0))
    return pl.pallas_call(
        body, name=name, grid=(t // tm, N_DEV),
        in_specs=[a_spec, pl.BlockSpec((None,) + g_in.shape[1:], lambda i, j: (j, 0, 0)),
                  pl.BlockSpec((tm, k), lambda i, j: (i, 0))],
        out_specs=pl.BlockSpec((tm, k), lambda i, j: (i, 0)),
        out_shape=jax.ShapeDtypeStruct((t, k), F32),
        scratch_shapes=[pltpu.VMEM((tm, k), F32)],
        compiler_params=_params("parallel", "arbitrary"),
    )(dact, g_in, add)


def _mm_nt_full(a, b, add, *, tm, name):
    t, n = a.shape
    k = b.shape[0]

    def body(a_ref, b_ref, add_ref, o_ref):
        o_ref[...] = add_ref[...] + _dot_nt(a_ref[...], b_ref[...])

    return pl.pallas_call(
        body, name=name, grid=(t // tm,),
        in_specs=[pl.BlockSpec((tm, n), lambda i: (i, 0)), pl.BlockSpec((k, n), lambda i: (0, 0)),
                  pl.BlockSpec((tm, k), lambda i: (i, 0))],
        out_specs=pl.BlockSpec((tm, k), lambda i: (i, 0)),
        out_shape=jax.ShapeDtypeStruct((t, k), F32),
        compiler_params=_params("parallel"),
    )(a, b, add)


def _w_in_grad(h_bf, dact, *, name, transposed=False):
    t, k = h_bf.shape
    halves = dact.shape[0] == 2 and dact.ndim == 3
    per = N_DEV // 2
    n = dact.shape[-1] // per if halves else dact.shape[-1]

    def body(a_ref, b_ref, o_ref):
        if transposed:
            o_ref[...] = _dot_tn(b_ref[...], a_ref[...]).astype(BF16)
        else:
            o_ref[...] = _dot_tn(a_ref[...], b_ref[...]).astype(BF16)

    if halves:
        b_spec = pl.BlockSpec((None, t, n), lambda j: (j // per, 0, j % per))
    elif dact.ndim == 4:
        b_spec = pl.BlockSpec((None, None, t, n), lambda j: (j // per, j % per, 0, 0))
    else:
        b_spec = pl.BlockSpec((None, t, n), lambda j: (j, 0, 0))
    return pl.pallas_call(
        body, name=name, grid=(N_DEV,),
        in_specs=[pl.BlockSpec((t, k), lambda j: (0, 0)), b_spec],
        out_specs=pl.BlockSpec((None, n, k) if transposed else (None, k, n), lambda j: (j, 0, 0)),
        out_shape=jax.ShapeDtypeStruct((N_DEV, n, k) if transposed else (N_DEV, k, n), BF16),
        compiler_params=_params("parallel"),
    )(h_bf, dact)


def _w_out_grad(y3, ds_bf, r, *, name):
    nj, t, kj = y3.shape
    d = ds_bf.shape[1]
    unit = r * LANES // math.gcd(r, LANES)
    ks = max([c for c in range(unit, min(kj, 768) + 1, unit) if kj % c == 0], default=kj)
    gsz = ks // r
    per = kj // ks

    def body(a_ref, b_ref, o_ref):
        o_ref[...] = _dot_tn(a_ref[...], b_ref[...]).reshape(gsz, r, d).astype(BF16)

    return pl.pallas_call(
        body, name=name, grid=(nj * per,),
        in_specs=[pl.BlockSpec((None, t, ks), lambda j: (j // per, 0, j % per)),
                  pl.BlockSpec((t, d), lambda j: (0, 0))],
        out_specs=pl.BlockSpec((gsz, r, d), lambda j: (j, 0, 0)),
        out_shape=jax.ShapeDtypeStruct((N_DEV, r, d), BF16),
        compiler_params=_params("parallel"),
    )(y3, ds_bf)


def _mm_tn_cols(a, b, *, tn, name):
    t, m = a.shape
    n = b.shape[1]

    def body(a_ref, b_ref, o_ref):
        o_ref[...] = _dot_tn(a_ref[...], b_ref[...])

    return pl.pallas_call(
        body, name=name, grid=(n // tn,),
        in_specs=[pl.BlockSpec((t, m), lambda j: (0, 0)), pl.BlockSpec((t, tn), lambda j: (0, j))],
        out_specs=pl.BlockSpec((m, tn), lambda j: (0, j)),
        out_shape=jax.ShapeDtypeStruct((m, n), F32),
        compiler_params=_params("parallel"),
    )(a, b)


def _ln_bwd(dout, s, g, *, tm, name):
    t, d = s.shape

    def body(do_ref, s_ref, g_ref, ds_ref, dsb_ref, gb_ref):
        i = pl.program_id(0)
        sv = s_ref[...]
        do = do_ref[...]
        mu = jnp.mean(sv, axis=-1, keepdims=True)
        xc = sv - mu
        var = jnp.mean(xc * xc, axis=-1, keepdims=True)
        rstd = lax.rsqrt(var + LN_EPS)
        xhat = xc * rstd
        dxhat = do * g_ref[...]
        m1 = jnp.mean(dxhat, axis=-1, keepdims=True)
        m2 = jnp.mean(dxhat * xhat, axis=-1, keepdims=True)
        ds = rstd * (dxhat - m1 - xhat * m2)
        ds_ref[...] = ds
        dsb_ref[...] = ds.astype(BF16)
        upd = _rows8([jnp.sum(do * xhat, axis=0, keepdims=True), jnp.sum(do, axis=0, keepdims=True)], d)

        @pl.when(i == 0)
        def _():
            gb_ref[...] = upd

        @pl.when(i > 0)
        def _():
            gb_ref[...] += upd

    row = pl.BlockSpec((tm, d), lambda i: (i, 0))
    return pl.pallas_call(
        body, name=name, grid=(t // tm,),
        in_specs=[row, row, pl.BlockSpec((1, d), lambda i: (0, 0))],
        out_specs=[row, row, pl.BlockSpec((8, d), lambda i: (0, 0))],
        out_shape=[jax.ShapeDtypeStruct((t, d), F32), jax.ShapeDtypeStruct((t, d), BF16),
                   jax.ShapeDtypeStruct((8, d), F32)],
        compiler_params=_params("arbitrary"),
    )(dout, s, g)


def _roll_down(x, s):
    return x if s == 0 else pltpu.roll(x, s, 0)


def _conv_taps(x, wb, width):
    y = jnp.broadcast_to(wb[width:width + 1, :], x.shape)
    for k in range(width):
        y = y + _roll_down(x, width - 1 - k) * wb[k:k + 1, :]
    return y


def _conv_taps_bwd(dy, x, wb, width):
    n = dy.shape[0]
    dx = jnp.zeros_like(dy)
    rows = []
    for k in range(width):
        s = width - 1 - k
        dy_up = dy if s == 0 else pltpu.roll(dy, n - s, 0)
        dx = dx + dy_up * wb[k:k + 1, :]
        rows.append(jnp.sum(dy_up * x, axis=0, keepdims=True))
    rows.append(jnp.sum(dy, axis=0, keepdims=True))
    t_idx = lax.broadcasted_iota(jnp.int32, dy.shape, 0)
    return jnp.where(t_idx < n - (width - 1), dx, 0.0), _rows8(rows, dy.shape[1])


def _convglu_fwd(z3, fwb3, *, name):
    _, t, n = z3.shape
    half = N_DEV // 2
    nc = pl.cdiv(n, LANES)

    def body(zg_ref, zv_ref, wg_ref, wv_ref, y_ref):
        gate = _conv_taps(zg_ref[...], wg_ref[...], 3)
        val = _conv_taps(zv_ref[...], wv_ref[...], 3)
        y_ref[...] = (_gelu(gate) * val).astype(BF16)

    zblk = lambda off: pl.BlockSpec((None, t, LANES), lambda j, c: (j + off, 0, c))
    wblk = lambda off: pl.BlockSpec((None, 8, LANES), lambda j, c: (j + off, 0, c))
    return pl.pallas_call(
        body, name=name, grid=(half, nc),
        in_specs=[zblk(0), zblk(half), wblk(0), wblk(half)],
        out_specs=zblk(0),
        out_shape=jax.ShapeDtypeStruct((half, t, n), BF16),
        compiler_params=_params("parallel", "parallel"),
    )(z3, z3, fwb3, fwb3)


def _ffn_bwd_mid(ds_bf, g_out, z3, fwb3, *, name):
    t, d = ds_bf.shape
    r = g_out.shape[1]
    n = z3.shape[2]
    half = N_DEV // 2
    nc = pl.cdiv(n, LANES)
    assert n == 2 * r

    def body(ds_ref, w_ref, zg_ref, zv_ref, wg_ref, wv_ref, dz_ref, dwb_ref, wsc_ref):
        c = pl.program_id(1)

        @pl.when(c == 0)
        def _():
            wsc_ref[0:r, :] = w_ref[0]
            wsc_ref[r:2 * r, :] = w_ref[1]
            if nc * LANES > n:
                wsc_ref[n:nc * LANES, :] = jnp.zeros((nc * LANES - n, d), BF16)

        w = wsc_ref[pl.ds(pl.multiple_of(c * LANES, LANES), LANES), :]
        dyf = _dot_nt(ds_ref[...], w)
        zg, zv = zg_ref[...], zv_ref[...]
        wg, wv = wg_ref[...], wv_ref[...]
        gate = _conv_taps(zg, wg, 3)
        val = _conv_taps(zv, wv, 3)
        gl, dgl = _gelu_and_grad(gate)
        dzg, dwg = _conv_taps_bwd(dyf * val * dgl, zg, wg, 3)
        dzv, dwv = _conv_taps_bwd(dyf * gl, zv, wv, 3)
        dz_ref[0] = dzg.astype(BF16)
        dz_ref[1] = dzv.astype(BF16)
        dwb_ref[0] = dwg
        dwb_ref[1] = dwv

    zblk = lambda off: pl.BlockSpec((None, t, LANES), lambda j, c: (j + off, 0, c))
    wblk = lambda off: pl.BlockSpec((None, 8, LANES), lambda j, c: (j + off, 0, c))
    return pl.pallas_call(
        body, name=name, grid=(half, nc),
        in_specs=[pl.BlockSpec((t, d), lambda j, c: (0, 0)),
                  pl.BlockSpec((2, r, d), lambda j, c: (j, 0, 0)),
                  zblk(0), zblk(half), wblk(0), wblk(half)],
        out_specs=[pl.BlockSpec((2, None, t, LANES), lambda j, c: (0, j, 0, c)),
                   pl.BlockSpec((2, None, 8, LANES), lambda j, c: (0, j, 0, c))],
        out_shape=[jax.ShapeDtypeStruct((2, half, t, n), BF16), jax.ShapeDtypeStruct((2, half, 8, n), F32)],
        scratch_shapes=[pltpu.VMEM((nc * LANES, d), BF16)],
        compiler_params=_params("parallel", "arbitrary"),
    )(ds_bf, g_out, z3, z3, fwb3, fwb3)


def _conv_a_fwd(gr, cwb, *, cb, name):
    t, r2 = gr.shape
    r = r2 // 2
    nb = r // cb

    def body(x_ref, w_ref, o_ref):
        o_ref[...] = _conv_taps(x_ref[...], w_ref[...], 4)

    return pl.pallas_call(
        body, name=name, grid=(nb,),
        in_specs=[pl.BlockSpec((t, cb), lambda j: (0, j + nb)), pl.BlockSpec((8, cb), lambda j: (0, j))],
        out_specs=pl.BlockSpec((t, cb), lambda j: (0, j)),
        out_shape=jax.ShapeDtypeStruct((t, r), F32),
        compiler_params=_params("parallel"),
    )(gr, cwb)


def _gates_fwd(rec, bd_r, bd_i, vecs, *, tm, name):
    t, r_dim = rec.shape
    nb, cb, _ = bd_r.shape

    def body(x_ref, wr_ref, wi_ref, v_ref, a_ref, u_ref, r_ref, i_ref):
        x = x_ref[...]
        xb = x.astype(BF16)
        v = v_ref[...]
        r = _sigmoid(_dot_nn(xb, wr_ref[...]) + v[0:1, :])
        i = _sigmoid(_dot_nn(xb, wi_ref[...]) + v[1:2, :])
        log_a = (-LRU_C) * r * _softplus(-v[2:3, :])
        a_ref[...] = jnp.exp(log_a)
        u_ref[...] = jnp.sqrt(-_expm1(2.0 * log_a)) * (i * x)
        r_ref[...] = r
        i_ref[...] = i

    blk = pl.BlockSpec((tm, cb), lambda j, i: (i, j))
    wspec = pl.BlockSpec((None, cb, cb), lambda j, i: (j, 0, 0))
    out = jax.ShapeDtypeStruct((t, r_dim), F32)
    return pl.pallas_call(
        body, name=name, grid=(nb, t // tm),
        in_specs=[blk, wspec, wspec, pl.BlockSpec((8, cb), lambda j, i: (0, j))],
        out_specs=[blk, blk, blk, blk],
        out_shape=[out, out, out, out],
        compiler_params=_params("parallel", "parallel"),
    )(rec, bd_r, bd_i, vecs)


def _scan_fwd(a, u, gr, *, cb, name):
    t, r = a.shape
    nb = r // cb
    seg = t // SUBLANES

    def body(a_ref, u_ref, g_ref, h_ref, y_ref, p_ref):
        def step(k, carry):
            h, p = carry
            rows = pl.ds(k, SUBLANES, stride=seg)
            av = a_ref[rows, :]
            h = av * h + u_ref[rows, :]
            p = av * p
            h_ref[rows, :] = h
            p_ref[rows, :] = p
            return h, p

        h_fin, p_fin = lax.fori_loop(0, seg, step, (jnp.zeros((SUBLANES, cb), F32), jnp.ones((SUBLANES, cb), F32)),
                                     unroll=4)
        carry = h_fin[0:1, :]
        for s in range(1, SUBLANES):
            rows = slice(s * seg, (s + 1) * seg)
            h_ref[rows, :] = h_ref[rows, :] + p_ref[rows, :] * carry
            carry = h_fin[s:s + 1, :] + p_fin[s:s + 1, :] * carry
        y_ref[...] = (_gelu(g_ref[...]) * h_ref[...]).astype(BF16)

    blk = pl.BlockSpec((t, cb), lambda j: (0, j))
    return pl.pallas_call(
        body, name=name, grid=(nb,),
        in_specs=[blk, blk, blk],
        out_specs=[blk, pl.BlockSpec((None, t, cb), lambda j: (0, 0, j))],
        out_shape=[jax.ShapeDtypeStruct((t, r), F32), jax.ShapeDtypeStruct((1, t, r), BF16)],
        scratch_shapes=[pltpu.VMEM((t, cb), F32)],
        compiler_params=_params("parallel"),
    )(a, u, gr)


def _scan_bwd(dy, gr, hr, a, *, cb, name):
    t, r = a.shape
    nb = r // cb
    seg = t // SUBLANES

    def body(dy_ref, g_ref, h_ref, a_ref, dh_ref, da_ref, dg_ref, q_ref):
        gl, dgl = _gelu_and_grad(g_ref[...])
        dyv = dy_ref[...]
        dh_ref[...] = dyv * gl
        dg_ref[...] = (dyv * h_ref[...] * dgl).astype(BF16)

        def step(k, carry):
            cin, q = carry
            rows = pl.ds(seg - 1 - k, SUBLANES, stride=seg)
            dh = dh_ref[rows, :] + cin
            dh_ref[rows, :] = dh
            q_ref[rows, :] = q
            av = a_ref[rows, :]
            return av * dh, av * q

        c_fin, q_fin = lax.fori_loop(0, seg, step, (jnp.zeros((SUBLANES, cb), F32), jnp.ones((SUBLANES, cb), F32)),
                                     unroll=4)
        carry = c_fin[SUBLANES - 1:SUBLANES, :]
        for s in range(SUBLANES - 2, -1, -1):
            rows = slice(s * seg, (s + 1) * seg)
            dh_ref[rows, :] = dh_ref[rows, :] + q_ref[rows, :] * carry
            carry = c_fin[s:s + 1, :] + q_fin[s:s + 1, :] * carry
        da_ref[...] = dh_ref[...] * _shift_down(h_ref[...], 1)

    blk = pl.BlockSpec((t, cb), lambda j: (0, j))
    return pl.pallas_call(
        body, name=name, grid=(nb,),
        in_specs=[blk, blk, blk, blk],
        out_specs=[blk, blk, blk],
        out_shape=[jax.ShapeDtypeStruct((t, r), F32), jax.ShapeDtypeStruct((t, r), F32),
                   jax.ShapeDtypeStruct((t, r), BF16)],
        scratch_shapes=[pltpu.VMEM((t, cb), F32)],
        compiler_params=_params("parallel"),
    )(dy, gr, hr, a)


def _gates_bwd(rec, r, i, a, dh, da, bd_r, bd_i, vecs, *, tm, name):
    t, r_dim = rec.shape
    nb, cb, _ = bd_r.shape

    def body(x_ref, r_ref, i_ref, a_ref, dh_ref, da_ref, wr_ref, wi_ref, v_ref, dx_ref, dpr_ref, dpi_ref, dv_ref):
        step = pl.program_id(1)
        x, r, i, a, dh, da = x_ref[...], r_ref[...], i_ref[...], a_ref[...], dh_ref[...], da_ref[...]
        lam = v_ref[...][2:3, :]
        sp = _softplus(-lam)
        a2 = a * a
        mult = jnp.sqrt(-_expm1(2.0 * (-LRU_C) * r * sp))
        d_i = dh * mult * x
        d_log_a = da * a - (dh * i * x) * a2 / mult
        d_r = d_log_a * ((-LRU_C) * sp)
        d_sp = jnp.sum(d_log_a * ((-LRU_C) * r), axis=0, keepdims=True)
        d_pre_r = d_r * r * (1.0 - r)
        d_pre_i = d_i * i * (1.0 - i)
        dprb = d_pre_r.astype(BF16)
        dpib = d_pre_i.astype(BF16)
        dx_ref[...] = dh * mult * i + _dot_nt(dprb, wr_ref[...]) + _dot_nt(dpib, wi_ref[...])
        dpr_ref[...] = dprb
        dpi_ref[...] = dpib
        upd = _rows8([jnp.sum(d_pre_r, axis=0, keepdims=True), jnp.sum(d_pre_i, axis=0, keepdims=True),
                      -d_sp * _sigmoid(-lam)], cb)

        @pl.when(step == 0)
        def _():
            dv_ref[...] = upd

        @pl.when(step > 0)
        def _():
            dv_ref[...] += upd

    blk = pl.BlockSpec((tm, cb), lambda j, i: (i, j))
    wspec = pl.BlockSpec((None, cb, cb), lambda j, i: (j, 0, 0))
    vspec = pl.BlockSpec((8, cb), lambda j, i: (0, j))
    return pl.pallas_call(
        body, name=name, grid=(nb, t // tm),
        in_specs=[blk] * 6 + [wspec, wspec, vspec],
        out_specs=[blk, blk, blk, vspec],
        out_shape=[jax.ShapeDtypeStruct((t, r_dim), F32), jax.ShapeDtypeStruct((t, r_dim), BF16),
                   jax.ShapeDtypeStruct((t, r_dim), BF16), jax.ShapeDtypeStruct((8, r_dim), F32)],
        compiler_params=_params("parallel", "arbitrary"),
    )(rec, r, i, a, dh, da, bd_r, bd_i, vecs)


def _bd_grad(rec, dpr, dpi, *, cb, name):
    t, r = rec.shape
    nb = r // cb

    def body(x_ref, dr_ref, di_ref, gr_ref, gi_ref):
        xb = x_ref[...].astype(BF16)
        gr_ref[...] = _dot_tn(xb, dr_ref[...])
        gi_ref[...] = _dot_tn(xb, di_ref[...])

    blk = pl.BlockSpec((t, cb), lambda j: (0, j))
    wspec = pl.BlockSpec((None, cb, cb), lambda j: (j, 0, 0))
    out = jax.ShapeDtypeStruct((nb, cb, cb), F32)
    return pl.pallas_call(
        body, name=name, grid=(nb,),
        in_specs=[blk, blk, blk], out_specs=[wspec, wspec], out_shape=[out, out],
        compiler_params=_params("parallel"),
    )(rec, dpr, dpi)


def _conv_a_bwd(d_rec, gr, dgate, cwb, *, cb, name):
    t, r = d_rec.shape
    nb = r // cb

    def body(dy_ref, x_ref, dg_ref, w_ref, dact_ref, dw_ref):
        dx, dw = _conv_taps_bwd(dy_ref[...], x_ref[...], w_ref[...], 4)
        dact_ref[0] = dg_ref[...]
        dact_ref[1] = dx.astype(BF16)
        dw_ref[...] = dw

    blk = pl.BlockSpec((t, cb), lambda j: (0, j))
    vspec = pl.BlockSpec((8, cb), lambda j: (0, j))
    return pl.pallas_call(
        body, name=name, grid=(nb,),
        in_specs=[blk, pl.BlockSpec((t, cb), lambda j: (0, j + nb)), blk, vspec],
        out_specs=[pl.BlockSpec((2, t, cb), lambda j: (0, 0, j)), vspec],
        out_shape=[jax.ShapeDtypeStruct((2, t, r), BF16), jax.ShapeDtypeStruct((8, r), F32)],
        compiler_params=_params("parallel"),
    )(d_rec, gr, dgate, cwb)


def _split3(x):
    p0 = x.astype(BF16)
    r1 = x - p0.astype(F32)
    p1 = r1.astype(BF16)
    p2 = (r1 - p1.astype(F32)).astype(BF16)
    return p0, p1, p2


def _fgate_fwd(fp, fb, *, tq, name):
    t = fp.shape[0]

    def body(f_ref, b_ref, c_ref, ct_ref):
        logf = -_softplus(-(f_ref[...] + b_ref[...]))
        rows = pl.program_id(0) * tq + lax.broadcasted_iota(jnp.int32, (tq, t), 0)
        cols = lax.broadcasted_iota(jnp.int32, (tq, t), 1)
        tri = (cols <= rows).astype(BF16)
        p0, p1, p2 = _split3(logf)
        c = _dot_nn(tri, p0) + _dot_nn(tri, p1) + _dot_nn(tri, p2)
        c_ref[...] = c
        ct_ref[...] = c.T

    return pl.pallas_call(
        body, name=name, grid=(t // tq,),
        in_specs=[pl.BlockSpec((t, LANES), lambda i: (0, 0)), pl.BlockSpec((1, LANES), lambda i: (0, 0))],
        out_specs=[pl.BlockSpec((tq, LANES), lambda i: (i, 0)), pl.BlockSpec((LANES, tq), lambda i: (0, i))],
        out_shape=[jax.ShapeDtypeStruct((t, LANES), F32), jax.ShapeDtypeStruct((LANES, t), F32)],
        compiler_params=_params("parallel"),
    )(fp, fb)


def _fgate_bwd(dct, fp, fb, *, tq, name):
    t = fp.shape[0]

    def body(d_ref, f_ref, b_ref, o_ref, db_ref):
        i = pl.program_id(0)
        rows = lax.broadcasted_iota(jnp.int32, (t, tq), 0)
        cols = i * tq + lax.broadcasted_iota(jnp.int32, (t, tq), 1)
        tri = (rows >= cols).astype(BF16)
        p0, p1, p2 = _split3(d_ref[...])
        dlogf = (_dot_nn(p0, tri) + _dot_nn(p1, tri) + _dot_nn(p2, tri)).T
        df = dlogf * _sigmoid(-(f_ref[...] + b_ref[...]))
        o_ref[...] = df.astype(BF16)
        upd = _rows8([jnp.sum(df, axis=0, keepdims=True)], LANES)

        @pl.when(i == 0)
        def _():
            db_ref[...] = upd

        @pl.when(i > 0)
        def _():
            db_ref[...] += upd

    return pl.pallas_call(
        body, name=name, grid=(t // tq,),
        in_specs=[pl.BlockSpec((LANES, t), lambda i: (0, 0)), pl.BlockSpec((tq, LANES), lambda i: (i, 0)),
                  pl.BlockSpec((1, LANES), lambda i: (0, 0))],
        out_specs=[pl.BlockSpec((tq, LANES), lambda i: (i, 0)), pl.BlockSpec((8, LANES), lambda i: (0, 0))],
        out_shape=[jax.ShapeDtypeStruct((t, LANES), BF16), jax.ShapeDtypeStruct((8, LANES), F32)],
        compiler_params=_params("arbitrary"),
    )(dct, fp, fb)


def _pair_sum(a, b, *, tm, name):
    t, d = a.shape

    def body(a_ref, b_ref, o_ref):
        o_ref[...] = (a_ref[...] + b_ref[...]).astype(BF16)

    row = pl.BlockSpec((tm, d), lambda i: (i, 0))
    return pl.pallas_call(
        body, name=name, grid=(t // tm,), in_specs=[row, row], out_specs=row,
        out_shape=jax.ShapeDtypeStruct((t, d), BF16), compiler_params=_params("parallel"),
    )(a, b)


FWD_HEAD_TILES = 2
BWD_HEAD_TILES = 1


def _head_block_width(dh, tiles):
    return tiles * LANES if tiles * LANES // dh <= 8 else LANES


def _head_masks(dh, bw):
    lane = lax.broadcasted_iota(jnp.int32, (1, bw), 1)
    return [((lane >= e * dh) & (lane < (e + 1) * dh)) for e in range(bw // dh)]


def _head_c_row(ct_blk, head):
    sub = lax.broadcasted_iota(jnp.int32, ct_blk.shape, 0)
    return jnp.sum(jnp.where(sub == head, ct_blk, 0.0), axis=0, keepdims=True)


def _attn_weights(qm, k, c_row, q0):
    tq, t = qm.shape[0], k.shape[0]
    s = _dot_nt(qm, k) - c_row
    qi = q0 + lax.broadcasted_iota(jnp.int32, (tq, t), 0)
    ki = lax.broadcasted_iota(jnp.int32, (tq, t), 1)
    s = jnp.where(ki <= qi, s, -jnp.inf)
    e = jnp.exp(s - jnp.max(s, axis=-1, keepdims=True))
    return e, 1.0 / jnp.sum(e, axis=-1, keepdims=True)


def _key_buckets(t, tq):
    return tuple(sorted({min(-(-(i * tq) // LANES) * LANES, t) for i in range(1, t // tq + 1)}))


def _for_prefix(needed, buckets, fn):
    prev = 0
    for length in buckets:
        pl.when((needed > prev) & (needed <= length))(lambda length=length: fn(length))
        prev = length


def _attn_fwd(qg, kv, ct, *, tq, name):
    t, d2 = qg.shape
    d = d2 // 2
    dh = d // N_HEADS
    bw = _head_block_width(dh, FWD_HEAD_TILES)
    hpb = bw // dh
    nhb = d // bw
    scale = dh ** -0.5
    buckets = _key_buckets(t, tq)

    def body(q_ref, og_ref, k_ref, v_ref, ct_ref, o_ref, y_ref):
        hb = pl.program_id(0)
        q0 = pl.program_id(1) * tq

        def run(length):
            qs = q_ref[...] * scale
            k = k_ref[0:length, :]
            v = v_ref[0:length, :]
            o = jnp.zeros((tq, bw), F32)
            for e, msk in enumerate(_head_masks(dh, bw)):
                c_row = _head_c_row(ct_ref[:, 0:length], hb * hpb + e)
                w, inv = _attn_weights(jnp.where(msk, qs, 0.0).astype(BF16), k, c_row, q0)
                o = o + _dot_nn(w.astype(BF16), jnp.where(msk, v, jnp.zeros_like(v))) * inv
            o_ref[...] = o
            y_ref[...] = (o * _sigmoid(og_ref[...])).astype(BF16)

        _for_prefix(q0 + tq, buckets, run)

    qblk = pl.BlockSpec((tq, bw), lambda h, i: (i, h))
    return pl.pallas_call(
        body, name=name, grid=(nhb, t // tq),
        in_specs=[qblk, pl.BlockSpec((tq, bw), lambda h, i: (i, h + nhb)),
                  pl.BlockSpec((t, bw), lambda h, i: (0, h)), pl.BlockSpec((t, bw), lambda h, i: (0, h + nhb)),
                  pl.BlockSpec((N_HEADS, t), lambda h, i: (0, 0))],
        out_specs=[qblk, pl.BlockSpec((None, tq, bw), lambda h, i: (0, i, h))],
        out_shape=[jax.ShapeDtypeStruct((t, d), F32), jax.ShapeDtypeStruct((1, t, d), BF16)],
        compiler_params=_params("parallel", "parallel"),
    )(qg, qg, kv, kv, ct)


def _attn_bwd(dy, qg, o, kv, ct, *, tq, name):
    t, d2 = qg.shape
    d = d2 // 2
    dh = d // N_HEADS
    bw = _head_block_width(dh, BWD_HEAD_TILES)
    hpb = bw // dh
    nhb = d // bw
    scale = dh ** -0.5
    buckets = _key_buckets(t, tq)

    def body(dy_ref, q_ref, og_ref, o_ref, k_ref, v_ref, ct_ref, dqg_ref, dk_ref, dv_ref, dc_ref):
        hb = pl.program_id(0)
        step = pl.program_id(1)
        q0 = step * tq

        @pl.when(step == 0)
        def _():
            dk_ref[...] = jnp.zeros((t, bw), F32)
            dv_ref[...] = jnp.zeros((t, bw), F32)
            dc_ref[...] = jnp.zeros((8, t), F32)

        def run(length):
            qs = q_ref[...] * scale
            k = k_ref[0:length, :]
            v = v_ref[0:length, :]
            sg = _sigmoid(og_ref[...])
            dyv = dy_ref[...]
            do = dyv * sg
            dqg_ref[1] = (dyv * o_ref[...] * sg * (1.0 - sg)).astype(BF16)
            dq = jnp.zeros((tq, bw), F32)
            dk = jnp.zeros((length, bw), F32)
            dv = jnp.zeros((length, bw), F32)
            dc_rows = []
            for e, msk in enumerate(_head_masks(dh, bw)):
                c_row = _head_c_row(ct_ref[:, 0:length], hb * hpb + e)
                qm = jnp.where(msk, qs, 0.0).astype(BF16)
                dom = jnp.where(msk, do, 0.0).astype(BF16)
                w, inv = _attn_weights(qm, k, c_row, q0)
                p = w * inv
                dp = _dot_nt(dom, v)
                dsc = p * (dp - jnp.sum(p * dp, axis=-1, keepdims=True))
                dsb = dsc.astype(BF16)
                dq = dq + _dot_nn(dsb, jnp.where(msk, k, jnp.zeros_like(k)))
                dk = dk + _dot_tn(dsb, qm)
                dv = dv + _dot_tn(p.astype(BF16), dom)
                dc_rows.append(-jnp.sum(dsc, axis=0, keepdims=True))
            dqg_ref[0] = (dq * scale).astype(BF16)
            dk_ref[0:length, :] += dk
            dv_ref[0:length, :] += dv
            dc_ref[:, 0:length] += _rows8(dc_rows, length)

        _for_prefix(q0 + tq, buckets, run)

    qblk = pl.BlockSpec((tq, bw), lambda h, i: (i, h))
    kblk = pl.BlockSpec((t, bw), lambda h, i: (0, h))
    return pl.pallas_call(
        body, name=name, grid=(nhb, t // tq),
        in_specs=[qblk, qblk, pl.BlockSpec((tq, bw), lambda h, i: (i, h + nhb)), qblk,
                  kblk, pl.BlockSpec((t, bw), lambda h, i: (0, h + nhb)),
                  pl.BlockSpec((N_HEADS, t), lambda h, i: (0, 0))],
        out_specs=[pl.BlockSpec((2, tq, bw), lambda h, i: (0, i, h)), kblk, kblk,
                   pl.BlockSpec((None, 8, t), lambda h, i: (h, 0, 0))],
        out_shape=[jax.ShapeDtypeStruct((2, t, d), BF16), jax.ShapeDtypeStruct((t, d), F32),
                   jax.ShapeDtypeStruct((t, d), F32), jax.ShapeDtypeStruct((nhb, 8, t), F32)],
        compiler_params=_params("parallel", "arbitrary"),
    )(dy, qg, qg, o, kv, kv, ct)


def _loss_bwd(h, tgt, *, lo, hi, tm, name):
    t, d = h.shape

    def body(h_ref, t_ref, l_ref, dy_ref):
        i = pl.program_id(0)
        rows = i * tm + lax.broadcasted_iota(jnp.int32, (tm, d), 0)
        err = jnp.where((rows >= lo) & (rows < hi), h_ref[...] - t_ref[...], 0.0)
        dy_ref[...] = err * (1.0 / d)
        part = jnp.sum(jnp.sum(err * err, axis=0, keepdims=True), axis=1, keepdims=True) * (0.5 / d)
        upd = jnp.broadcast_to(part, (8, LANES))

        @pl.when(i == 0)
        def _():
            l_ref[...] = upd

        @pl.when(i > 0)
        def _():
            l_ref[...] += upd

    row = pl.BlockSpec((tm, d), lambda i: (i, 0))
    return pl.pallas_call(
        body, name=name, grid=(t // tm,),
        in_specs=[row, row],
        out_specs=[pl.BlockSpec((8, LANES), lambda i: (0, 0)), row],
        out_shape=[jax.ShapeDtypeStruct((8, LANES), F32), jax.ShapeDtypeStruct((t, d), F32)],
        compiler_params=_params("arbitrary"),
    )(h, tgt)


def _adamw_math(w, gv, m, v):
    bc1 = 1.0 / (1.0 - ADAM_B1 ** ADAM_STEP)
    bc2 = 1.0 / (1.0 - ADAM_B2 ** ADAM_STEP)
    nm = ADAM_B1 * m + (1.0 - ADAM_B1) * gv
    nv = ADAM_B2 * v + (1.0 - ADAM_B2) * (gv * gv)
    delta = (-ADAM_LR) * ((nm * bc1) / (jnp.sqrt(nv * bc2) + ADAM_EPS) + ADAM_WD * w)
    return delta, nm, nv


def _adamw(w, g, m, v, *, name):
    r, c = w.shape
    tr = r
    for cand in (512, 256, 128, 64, 32, 16, 8):
        if r % cand == 0 and r > cand:
            tr = cand
            break

    def body(w_ref, g_ref, m_ref, v_ref, d_ref, nm_ref, nv_ref):
        d_ref[...], nm_ref[...], nv_ref[...] = _adamw_math(w_ref[...], g_ref[...], m_ref[...], v_ref[...])

    blk = pl.BlockSpec((tr, c), lambda i: (i, 0))
    out = jax.ShapeDtypeStruct((r, c), F32)
    return pl.pallas_call(
        body, name=name, grid=(r // tr,),
        in_specs=[blk] * 4, out_specs=[blk] * 3, out_shape=[out] * 3,
        compiler_params=_params("parallel"),
    )(w, g, m, v)


def _sum_adamw(recvs, sends, me, w, m, v, *, name):
    n_l = len(recvs)
    _, r, c = recvs[0].shape
    tr = _tile(r, (256, 192, 176, 128, 96, 64, 48, 32, 16))

    def body(me_ref, *refs):
        p_refs, own_refs = refs[:n_l], refs[n_l:2 * n_l]
        w_ref, m_ref, v_ref, g_ref, d_ref, nm_ref, nv_ref, acc_ref = refs[2 * n_l:]
        layer = pl.program_id(0)
        mine = me_ref[0]
        for k in range(n_l):
            @pl.when(layer == k)
            def _(k=k):
                acc_ref[...] = jnp.zeros((tr, c), F32)
                for dev in range(N_DEV):
                    @pl.when(mine == dev)
                    def _():
                        acc_ref[...] += own_refs[k][...].astype(F32)

                    @pl.when(mine != dev)
                    def _(dev=dev):
                        acc_ref[...] += p_refs[k][dev].astype(F32)
                acc = acc_ref[...]
                g_ref[...] = acc
                d_ref[...], nm_ref[...], nv_ref[...] = _adamw_math(w_ref[...], acc, m_ref[...], v_ref[...])

    p_specs = [pl.BlockSpec((N_DEV, tr, c), lambda l, i, me_ref, k=k: (0, jnp.where(l == k, i, 0), 0))
               for k in range(n_l)]
    own_specs = [pl.BlockSpec((None, tr, c), lambda l, i, me_ref, k=k: (me_ref[0], jnp.where(l == k, i, 0), 0))
                 for k in range(n_l)]
    blk = pl.BlockSpec((None, tr, c), lambda l, i, me_ref: (l, i, 0))
    out = jax.ShapeDtypeStruct((n_l, r, c), F32)
    return pl.pallas_call(
        body, name=name,
        grid_spec=pltpu.PrefetchScalarGridSpec(
            num_scalar_prefetch=1, grid=(n_l, r // tr),
            in_specs=p_specs + own_specs + [blk] * 3, out_specs=[blk] * 4,
            scratch_shapes=[pltpu.VMEM((tr, c), F32)]),
        out_shape=[out] * 4,
        compiler_params=_params("arbitrary", "arbitrary"),
    )(me, *recvs, *sends, w, m, v)


def _sum8(parts, *, name):
    _, r, c = parts.shape
    tr = r
    for cand in (512, 256, 128, 64, 32, 16):
        if r % cand == 0 and r > cand:
            tr = cand
            break

    def body(p_ref, o_ref):
        acc = p_ref[0].astype(F32)
        for k in range(1, N_DEV):
            acc = acc + p_ref[k].astype(F32)
        o_ref[...] = acc

    return pl.pallas_call(
        body, name=name, grid=(r // tr,),
        in_specs=[pl.BlockSpec((N_DEV, tr, c), lambda i: (0, i, 0))],
        out_specs=pl.BlockSpec((tr, c), lambda i: (i, 0)),
        out_shape=jax.ShapeDtypeStruct((r, c), F32),
        compiler_params=_params("parallel"),
    )(parts)


def _my_index():
    return 4 * lax.axis_index("x") + 2 * lax.axis_index("y") + lax.axis_index("c")


def _peer(k):
    x, y, c = lax.axis_index("x"), lax.axis_index("y"), lax.axis_index("c")
    px = x ^ ((k >> 2) & 1)
    py = y ^ ((k >> 1) & 1)
    pc = c ^ (k & 1)
    return (px, py, pc), 4 * px + 2 * py + pc


def _all_gather(shards, *, name):
    n_arr = len(shards)

    def body(*refs):
        ins, outs = refs[:n_arr], refs[n_arr:2 * n_arr]
        send_sems, recv_sems, local_sems = refs[2 * n_arr:]
        me = _my_index()
        local = [pltpu.make_async_copy(ins[n], outs[n].at[me], local_sems.at[n]) for n in range(n_arr)]
        for cp in local:
            cp.start()
        sends = []
        for k in range(1, N_DEV):
            peer, _ = _peer(k)
            for n in range(n_arr):
                cp = pltpu.make_async_remote_copy(
                    src_ref=ins[n], dst_ref=outs[n].at[me], send_sem=send_sems.at[n, k - 1],
                    recv_sem=recv_sems.at[n, k - 1], device_id=peer, device_id_type=pl.DeviceIdType.MESH)
                cp.start()
                sends.append(cp)
        for k in range(1, N_DEV):
            peer, pidx = _peer(k)
            for n in range(n_arr):
                pltpu.make_async_remote_copy(
                    src_ref=ins[n], dst_ref=outs[n].at[pidx], send_sem=send_sems.at[n, k - 1],
                    recv_sem=recv_sems.at[n, k - 1], device_id=peer, device_id_type=pl.DeviceIdType.MESH).wait_recv()
        for cp in sends:
            cp.wait_send()
        for cp in local:
            cp.wait()

    hbm = pl.BlockSpec(memory_space=pl.ANY)
    return pl.pallas_call(
        body, name=name,
        in_specs=[hbm] * n_arr, out_specs=[hbm] * n_arr,
        out_shape=[jax.ShapeDtypeStruct((N_DEV,) + s.shape, s.dtype) for s in shards],
        scratch_shapes=[pltpu.SemaphoreType.DMA((n_arr, N_DEV - 1)), pltpu.SemaphoreType.DMA((n_arr, N_DEV - 1)),
                        pltpu.SemaphoreType.DMA((n_arr,))],
        compiler_params=pltpu.CompilerParams(has_side_effects=True),
    )(*shards)


_HBM = pl.BlockSpec(memory_space=pltpu.HBM)
_SEM = pl.BlockSpec(memory_space=pltpu.SEMAPHORE)
_EFFECT = pltpu.SideEffectType.DATAFLOW_SIDE_EFFECTING


def _remote(src, dst, send_sem, recv_sem, peer):
    return pltpu.make_async_remote_copy(src_ref=src, dst_ref=dst, send_sem=send_sem, recv_sem=recv_sem,
                                        device_id=peer, device_id_type=pl.DeviceIdType.MESH)


def _place_own(src, layer, me, *, out_dtype, name):
    _, r, c = src.shape
    tr = _tile(r, (256, 192, 176, 128, 96, 64, 48, 32, 16))

    def body(me_ref, s_ref, o_ref):
        o_ref[...] = s_ref[...].astype(out_dtype)

    return pl.pallas_call(
        body, name=name,
        grid_spec=pltpu.PrefetchScalarGridSpec(
            num_scalar_prefetch=1, grid=(r // tr,),
            in_specs=[pl.BlockSpec((None, tr, c), lambda i, me_ref: (layer, i, 0))],
            out_specs=pl.BlockSpec((None, tr, c), lambda i, me_ref: (me_ref[0], i, 0))),
        out_shape=jax.ShapeDtypeStruct((N_DEV, r, c), out_dtype),
        compiler_params=_params("parallel"),
    )(me, src)


def _own_blocks(srcs, *, name):
    n = len(srcs)

    def body(*refs):
        ins, outs, sems = refs[:n], refs[n:2 * n], refs[2 * n]
        me = _my_index()
        cps = [pltpu.make_async_copy(ins[t].at[me], outs[t].at[me], sems.at[t]) for t in range(n)]
        for cp in cps:
            cp.start()
        for cp in cps:
            cp.wait()

    return pl.pallas_call(
        body, name=name, in_specs=[_HBM] * n, out_specs=[_HBM] * n,
        out_shape=[jax.ShapeDtypeStruct(s.shape, s.dtype) for s in srcs],
        scratch_shapes=[pltpu.SemaphoreType.DMA((n,))],
    )(*srcs)


def _split_start(groups, *, scatter, name):
    sizes = [len(srcs) for srcs, _ in groups]
    flat_src = [s for srcs, _ in groups for s in srcs]
    flat_land = [l for _, lands in groups for l in lands]
    n, n_g = len(flat_land), len(groups)
    if not scatter:
        flat_src = []
    n_in = len(flat_src) + n

    def body(*refs):
        lands = refs[n_in - n:n_in]
        ins = refs[:n] if scatter else lands
        sems = refs[n_in:n_in + 2 * n_g]
        token = refs[-1]
        me = _my_index()
        t = 0
        for g in range(n_g):
            for q in range(sizes[g]):
                for k in range(1, N_DEV):
                    peer, pidx = _peer(k)
                    src = ins[t].at[pidx] if scatter else ins[t].at[me]
                    slot = q * (N_DEV - 1) + k - 1
                    _remote(src, lands[t].at[me], sems[2 * g].at[slot], sems[2 * g + 1].at[slot], peer).start()
                t += 1
        token[...] = jnp.zeros_like(token)

    sem_shapes = []
    for sz in sizes:
        sem_shapes += [pltpu.SemaphoreType.DMA((sz * (N_DEV - 1),)), pltpu.SemaphoreType.DMA((sz * (N_DEV - 1),))]
    outs = pl.pallas_call(
        body, name=name,
        in_specs=[_HBM] * n_in,
        out_specs=[_SEM] * (2 * n_g) + [_HBM] * n_in + [pl.BlockSpec(memory_space=pltpu.VMEM)],
        out_shape=sem_shapes + [pltpu.HBM(a.shape, a.dtype) for a in flat_src + flat_land]
        + [jax.ShapeDtypeStruct((8, LANES), F32)],
        input_output_aliases={i: 2 * n_g + i for i in range(n_in)},
        compiler_params=pltpu.CompilerParams(has_side_effects=_EFFECT),
    )(*[pltpu.with_memory_space_constraint(a, pltpu.HBM) for a in flat_src + flat_land])
    sems, thru, token = outs[:2 * n_g], outs[2 * n_g:2 * n_g + n_in], outs[-1]
    handles, pos = [], 0
    for g, sz in enumerate(sizes):
        lands_g = thru[n_in - n + pos:n_in - n + pos + sz]
        handles.append((sems[2 * g], sems[2 * g + 1], thru[pos:pos + sz] if scatter else [], lands_g))
        pos += sz
    return handles, token


def _split_wait(handle, after, *, scatter, name):
    send_sems, recv_sems, srcs, lands = handle
    n, n_src = len(lands), len(srcs)

    def body(*refs):
        lnd = refs[n_src:n_src + n]
        ins = refs[:n_src] if scatter else lnd
        ssem, rsem = refs[n_src + n], refs[n_src + n + 1]
        me = _my_index()
        for t in range(n):
            for k in range(1, N_DEV):
                peer, pidx = _peer(k)
                block = ins[t].at[me]
                slot = t * (N_DEV - 1) + k - 1
                _remote(block, lnd[t].at[me], ssem.at[slot], rsem.at[slot], peer).wait_send()
                _remote(block, lnd[t].at[pidx], ssem.at[slot], rsem.at[slot], peer).wait_recv()

    return pl.pallas_call(
        body, name=name,
        in_specs=[_HBM] * (n_src + n) + [_SEM, _SEM, pl.BlockSpec(memory_space=pl.ANY)],
        out_specs=[_HBM] * n,
        out_shape=[pltpu.HBM(l.shape, l.dtype) for l in lands],
        input_output_aliases={n_src + t: t for t in range(n)},
        compiler_params=pltpu.CompilerParams(has_side_effects=_EFFECT),
    )(*srcs, *lands, send_sems, recv_sems, after)


def _pack(arrs, dtype, row_quantum=16):
    flat = jnp.concatenate([a.astype(dtype).reshape(-1) for a in arrs])
    pad = (-flat.shape[0]) % (row_quantum * PACK_COLS)
    if pad:
        flat = jnp.concatenate([flat, jnp.zeros((pad,), dtype)])
    return flat.reshape(-1, PACK_COLS)


def _pack8(arrs, dtype):
    flat = jnp.concatenate([a.astype(dtype).reshape(N_DEV, -1) for a in arrs], axis=1)
    pad = (-flat.shape[1]) % (16 * PACK_COLS)
    if pad:
        flat = jnp.concatenate([flat, jnp.zeros((N_DEV, pad), dtype)], axis=1)
    return flat.reshape(N_DEV, -1, PACK_COLS)


def _unpack(slab, shapes, lead):
    lead_shape = slab.shape[:lead]
    flat = slab.reshape(lead_shape + (-1,))
    outs, off = [], 0
    for shp in shapes:
        size = math.prod(shp)
        outs.append(flat[..., off:off + size].reshape(lead_shape + tuple(shp)))
        off += size
    return outs


def _cols_full(g):
    g = jnp.moveaxis(g, 0, -2)
    return g.reshape(g.shape[:-2] + (g.shape[-2] * g.shape[-1],))


def _cols_split(full):
    n = full.shape[-1] // N_DEV
    return jnp.moveaxis(full.reshape(full.shape[:-1] + (N_DEV, n)), -2, 0)


def _block_diag(w, per):
    n, b, _ = w.shape
    w4 = w.reshape(n // per, per, b, b)
    eye = jnp.eye(per, dtype=w.dtype)
    return jnp.einsum('gpab,pq->gpaqb', w4, eye).reshape(n // per, per * b, per * b)


def _block_diag_extract(g, per):
    gn, cb, _ = g.shape
    b = cb // per
    g5 = g.reshape(gn, per, b, per, b)
    return jnp.stack([g5[:, p, :, p, :] for p in range(per)], axis=1).reshape(gn * per, b, b)


def _slab2d(a):
    return a.reshape(-1, a.shape[-1])


def _lru_block_cols(r_dim):
    lru = r_dim // N_LRU_BLOCKS
    return lru * LANES // math.gcd(lru, LANES)


BIG = ("a_w_in", "a_w_out", "b_w_in", "b_w_out", "f_w_in", "f_w_out")
COL_F32 = ("meta", "a_conv_w", "a_conv_b", "a_b_r", "a_b_i", "a_lambda", "f_conv_w")
REPLICATED = ("a_w_r", "a_w_i", "kv_f_b", "f_conv_b", "ln1_g", "ln1_b", "ln2_g", "ln2_b")
WEIGHT_NAMES = ("meta", "a_w_in", "a_conv_w", "a_conv_b", "a_w_r", "a_b_r", "a_w_i", "a_b_i", "a_lambda", "a_w_out",
                "kv_w", "kv_f_b", "b_w_in", "b_w_out", "f_w_in", "f_conv_w", "f_conv_b", "f_w_out",
                "ln1_g", "ln1_b", "ln2_g", "ln2_b")


def _kv_layout(kv_gathered, d):
    kv_full = _cols_full(kv_gathered)
    kv_pad = 2 * d + LANES - kv_full.shape[1]
    return jnp.concatenate([kv_full, jnp.zeros((d, kv_pad), kv_full.dtype)], axis=1)


def _small_layouts(small):
    r_dim = small["a_lambda"].shape[1]
    n_f = small["f_conv_b"].shape[1] // N_DEV
    cb = _lru_block_cols(r_dim)
    per = cb // (r_dim // N_LRU_BLOCKS)
    n_a = small["a_lambda"].shape[0]
    f_conv_w3 = small["f_conv_w"].reshape(N_LAYERS, 3, N_DEV, n_f).transpose(0, 2, 1, 3)
    f_conv_b3 = small["f_conv_b"].reshape(N_LAYERS, N_DEV, 1, n_f)
    return {
        "kv_fb": jnp.concatenate([small["kv_f_b"], jnp.zeros((LANES - N_HEADS,), F32)])[None],
        "a_cwb": jnp.concatenate([small["a_conv_w"], small["a_conv_b"][:, None],
                                  jnp.zeros((n_a, 3, r_dim), F32)], axis=1),
        "a_vecs": jnp.concatenate([jnp.stack([small["a_b_r"], small["a_b_i"], small["a_lambda"]], axis=1),
                                   jnp.zeros((n_a, 5, r_dim), F32)], axis=1),
        "a_bd_r": jnp.stack([_block_diag(small["a_w_r"][l], per) for l in range(n_a)]).astype(BF16),
        "a_bd_i": jnp.stack([_block_diag(small["a_w_i"][l], per) for l in range(n_a)]).astype(BF16),
        "f_cwb3": jnp.concatenate([f_conv_w3, f_conv_b3, jnp.zeros((N_LAYERS, N_DEV, 4, n_f), F32)], axis=2),
        "ln1_g": small["ln1_g"][:, None], "ln1_b": small["ln1_b"][:, None],
        "ln2_g": small["ln2_g"][:, None], "ln2_b": small["ln2_b"][:, None],
    }


def _local_step(h0, tgt, n_meta, n_tok, wts, hooks):
    tp, d = h0.shape
    tm = tp // 8 if (tp // 8) % 16 == 0 else tp
    tmb = _tile(tp, (1088, 512, 320, 256, 128))
    tq = 128
    tqa_fwd = tp // 4 if tp % 64 == 0 else tq
    tqa_bwd = tp // 4 if tp % 64 == 0 else tq
    r_dim = wts["a_vecs"].shape[2]
    cb = wts["a_bd_r"].shape[-1]
    sb = LANES
    n_b = N_LAYERS - N_A_LAYERS

    h, h_bf = h0, h0.astype(BF16)
    saved = []
    kvs = None
    for layer in range(N_LAYERS):
        lw = {}
        sv = {"h_bf": h_bf, "w": lw}
        if layer < N_A_LAYERS:
            lw["in"] = hooks.weight(layer, "in", h)
            sv["gr"] = _proj_in(h_bf, lw["in"], shard_major=False, name="a_in_proj")
            sv["rec"] = _conv_a_fwd(sv["gr"], wts["a_cwb"][layer], cb=cb, name="a_conv_fwd")
            a, u, sv["r"], sv["i"] = _gates_fwd(sv["rec"], wts["a_bd_r"][layer], wts["a_bd_i"][layer],
                                                wts["a_vecs"][layer], tm=tm, name="a_gates_fwd")
            sv["a"] = a
            sv["hr"], y3 = _scan_fwd(a, u, sv["gr"], cb=sb, name="a_scan_fwd")
        else:
            j = layer - N_A_LAYERS
            if j == 0:
                kv_w = _kv_layout(hooks.weight(layer, "kv_w", h), d)
                kvs = {"h_bf": h_bf, "w": kv_w}
                kvs["kv"] = _mm_nn(h_bf, kv_w[:, :2 * d], tn=_tile(2 * d, (512, 256, 128)), out_dtype=BF16,
                                   name="kv_proj")
                kvs["fp"] = _mm_nn(h_bf, kv_w[:, 2 * d:], tn=LANES, out_dtype=F32, name="f_proj")
                kvs["c"], ct = _fgate_fwd(kvs["fp"], wts["kv_fb"], tq=tq, name="fgate_fwd")
                kvs["ct"] = ct[:N_HEADS]
            lw["in"] = hooks.weight(layer, "in", kvs["c"] if j == 0 else h)
            sv["qg"] = _proj_in(h_bf, lw["in"], shard_major=False, name="b_in_proj")
            sv["o"], y3 = _attn_fwd(sv["qg"], kvs["kv"], kvs["ct"], tq=tqa_fwd, name="attn_fwd")
        sv["y3"] = y3
        lw["out"] = hooks.weight(layer, "out", y3)
        sv["s1"], h, h_bf = _out_ln(y3, lw["out"], h, wts["ln1_g"][layer], wts["ln1_b"][layer], n_valid=n_tok,
                                    tm=tmb // 2, name="mix_out_ln")
        sv["h1_bf"] = h_bf
        lw["f_in"] = hooks.weight(layer, "f_in", h)
        sv["z3"] = _proj_in(h_bf, lw["f_in"], shard_major=True, transposed=True, name="f_in_proj")
        sv["yf3"] = _convglu_fwd(sv["z3"], wts["f_cwb3"][layer], name="f_convglu_fwd")
        lw["f_out"] = hooks.weight(layer, "f_out", sv["yf3"])
        sv["s2"], h, h_bf = _out_ln(sv["yf3"], lw["f_out"], h, wts["ln2_g"][layer], wts["ln2_b"][layer],
                                    n_valid=n_tok, tm=tmb // 2, name="ffn_out_ln")
        saved.append(sv)

    loss_tile, dh = _loss_bwd(h, tgt, lo=n_meta, hi=n_tok, tm=tm, name="loss")

    grads = {k: [None] * N_LAYERS for k in ("f_cwb3", "ln1_gb", "ln2_gb")}
    grads.update({k: [None] * N_A_LAYERS for k in ("a_cwb", "a_bd_r", "a_bd_i", "a_vecs")})
    dkv = []
    token = jnp.zeros((), F32)
    for layer in reversed(range(N_LAYERS)):
        sv = saved[layer]
        lw = sv["w"]
        big = {}
        ds, ds_bf, grads["ln2_gb"][layer] = _ln_bwd(dh, sv["s2"], wts["ln2_g"][layer] + token, tm=tm, name="ln_bwd")
        dz, dcw = _ffn_bwd_mid(ds_bf, lw["f_out"], sv["z3"], wts["f_cwb3"][layer], name="f_bwd_mid")
        grads["f_cwb3"][layer] = dcw.reshape((N_DEV,) + dcw.shape[2:])
        dz3 = dz
        big["f_out"] = _w_out_grad(sv["yf3"], ds_bf, lw["f_out"].shape[1], name="f_w_out_grad")
        dh = _in_bwd(dz3, lw["f_in"], ds, tm=tmb, transposed=True, name="f_in_bwd")
        big["f_in"] = _w_in_grad(sv["h1_bf"], dz3, transposed=True, name="f_w_in_grad")
        token = hooks.grads_ready(layer, "ffn", big)
        big = {}
        ds, ds_bf, grads["ln1_gb"][layer] = _ln_bwd(dh, sv["s1"], wts["ln1_g"][layer] + token, tm=tm, name="ln_bwd")
        if layer < N_A_LAYERS:
            dy = _out_bwd(ds_bf, lw["out"], tm=tmb // 2, name="a_out_bwd")
            big["out"] = _w_out_grad(sv["y3"], ds_bf, lw["out"].shape[1], name="a_w_out_grad")
            d_h, d_a, dgate = _scan_bwd(dy, sv["gr"], sv["hr"], sv["a"], cb=sb, name="a_scan_bwd")
            d_rec, dpr, dpi, grads["a_vecs"][layer] = _gates_bwd(
                sv["rec"], sv["r"], sv["i"], sv["a"], d_h, d_a, wts["a_bd_r"][layer], wts["a_bd_i"][layer],
                wts["a_vecs"][layer], tm=tm, name="a_gates_bwd")
            grads["a_bd_r"][layer], grads["a_bd_i"][layer] = _bd_grad(sv["rec"], dpr, dpi, cb=cb, name="a_bd_grad")
            dact, grads["a_cwb"][layer] = _conv_a_bwd(d_rec, sv["gr"], dgate, wts["a_cwb"][layer], cb=cb,
                                                      name="a_conv_bwd")
            dh = _in_bwd(dact, lw["in"], ds, tm=tmb, name="a_in_bwd")
            big["in"] = _w_in_grad(sv["h_bf"], dact, name="a_w_in_grad")
        else:
            j = layer - N_A_LAYERS
            dy = _out_bwd(ds_bf, lw["out"], tm=tmb // 2, name="b_out_bwd")
            big["out"] = _w_out_grad(sv["y3"], ds_bf, lw["out"].shape[1], name="b_w_out_grad")
            dqg, dk, dv, dc = _attn_bwd(dy, sv["qg"], sv["o"], kvs["kv"], kvs["ct"], tq=tqa_bwd,
                                        name="attn_bwd")
            dkv.append((dk, dv, dc))
            dh = _in_bwd(dqg, lw["in"], ds, tm=tmb, name="b_in_bwd")
            big["in"] = _w_in_grad(sv["h_bf"], dqg, name="b_w_in_grad")
            if j == 0:
                hpb = _head_block_width(d // N_HEADS, BWD_HEAD_TILES) // (d // N_HEADS)
                dct = (dkv[0][2] + dkv[1][2])[:, :hpb, :].reshape(N_HEADS, tp)
                dct = jnp.concatenate([dct, jnp.zeros((LANES - N_HEADS, tp), F32)])
                df_bf, grads["kv_fb"] = _fgate_bwd(dct, kvs["fp"], wts["kv_fb"], tq=tq, name="fgate_bwd")
                dkvz = jnp.concatenate([_pair_sum(dkv[0][0], dkv[1][0], tm=tm, name="kv_pair_sum"),
                                        _pair_sum(dkv[0][1], dkv[1][1], tm=tm, name="kv_pair_sum"), df_bf], axis=1)
                dh = _mm_nt_full(dkvz, kvs["w"], dh, tm=tmb // 2, name="kv_in_bwd")
                big["kv_w"] = _mm_tn_cols(kvs["h_bf"], dkvz, tn=LANES, name="kv_w_grad")
        token = hooks.grads_ready(layer, "mix", big)
    return loss_tile, dh, grads


def _finish_small_grads(grads, d_h0, n_meta):
    r_dim = grads["a_vecs"][0].shape[1]
    per = _lru_block_cols(r_dim) // (r_dim // N_LRU_BLOCKS)
    a_cwb = jnp.stack(grads["a_cwb"])
    a_vecs = jnp.stack(grads["a_vecs"])
    f_cwb3 = jnp.stack(grads["f_cwb3"])
    ln1 = jnp.stack(grads["ln1_gb"])
    ln2 = jnp.stack(grads["ln2_gb"])
    f_rows = f_cwb3.transpose(0, 2, 1, 3).reshape(N_LAYERS, 8, -1)
    return {
        "meta": d_h0[:n_meta],
        "a_conv_w": a_cwb[:, :4], "a_conv_b": a_cwb[:, 4],
        "a_w_r": jnp.stack([_block_diag_extract(g, per) for g in grads["a_bd_r"]]),
        "a_b_r": a_vecs[:, 0],
        "a_w_i": jnp.stack([_block_diag_extract(g, per) for g in grads["a_bd_i"]]),
        "a_b_i": a_vecs[:, 1], "a_lambda": a_vecs[:, 2],
        "kv_f_b": grads["kv_fb"][0, :N_HEADS],
        "f_conv_w": f_rows[:, :3], "f_conv_b": f_rows[:, 3],
        "ln1_g": ln1[:, 0], "ln1_b": ln1[:, 1], "ln2_g": ln2[:, 0], "ln2_b": ln2[:, 1],
    }


def kernel(x, meta, a_w_in, a_conv_w, a_conv_b, a_w_r, a_b_r, a_w_i, a_b_i, a_lambda, a_w_out, kv_w, kv_f_b, b_w_in, b_w_out, f_w_in, f_conv_w, f_conv_b, f_w_out, ln1_g, ln1_b, ln2_g, ln2_b, loss_target, m_meta, m_a_w_in, m_a_conv_w, m_a_conv_b, m_a_w_r, m_a_b_r, m_a_w_i, m_a_b_i, m_a_lambda, m_a_w_out, m_kv_w, m_kv_f_b, m_b_w_in, m_b_w_out, m_f_w_in, m_f_conv_w, m_f_conv_b, m_f_w_out, m_ln1_g, m_ln1_b, m_ln2_g, m_ln2_b, v_meta, v_a_w_in, v_a_conv_w, v_a_conv_b, v_a_w_r, v_a_b_r, v_a_w_i, v_a_b_i, v_a_lambda, v_a_w_out, v_kv_w, v_kv_f_b, v_b_w_in, v_b_w_out, v_f_w_in, v_f_conv_w, v_f_conv_b, v_f_w_out, v_ln1_g, v_ln1_b, v_ln2_g, v_ln2_b):
    w = dict(meta=meta, a_w_in=a_w_in, a_conv_w=a_conv_w, a_conv_b=a_conv_b, a_w_r=a_w_r, a_b_r=a_b_r, a_w_i=a_w_i,
             a_b_i=a_b_i, a_lambda=a_lambda, a_w_out=a_w_out, kv_w=kv_w, kv_f_b=kv_f_b, b_w_in=b_w_in,
             b_w_out=b_w_out, f_w_in=f_w_in, f_conv_w=f_conv_w, f_conv_b=f_conv_b, f_w_out=f_w_out, ln1_g=ln1_g,
             ln1_b=ln1_b, ln2_g=ln2_g, ln2_b=ln2_b)
    m = dict(meta=m_meta, a_w_in=m_a_w_in, a_conv_w=m_a_conv_w, a_conv_b=m_a_conv_b, a_w_r=m_a_w_r, a_b_r=m_a_b_r,
             a_w_i=m_a_w_i, a_b_i=m_a_b_i, a_lambda=m_a_lambda, a_w_out=m_a_w_out, kv_w=m_kv_w, kv_f_b=m_kv_f_b,
             b_w_in=m_b_w_in, b_w_out=m_b_w_out, f_w_in=m_f_w_in, f_conv_w=m_f_conv_w, f_conv_b=m_f_conv_b,
             f_w_out=m_f_w_out, ln1_g=m_ln1_g, ln1_b=m_ln1_b, ln2_g=m_ln2_g, ln2_b=m_ln2_b)
    v = dict(meta=v_meta, a_w_in=v_a_w_in, a_conv_w=v_a_conv_w, a_conv_b=v_a_conv_b, a_w_r=v_a_w_r, a_b_r=v_a_b_r,
             a_w_i=v_a_w_i, a_b_i=v_a_b_i, a_lambda=v_a_lambda, a_w_out=v_a_w_out, kv_w=v_kv_w, kv_f_b=v_kv_f_b,
             b_w_in=v_b_w_in, b_w_out=v_b_w_out, f_w_in=v_f_w_in, f_conv_w=v_f_conv_w, f_conv_b=v_f_conv_b,
             f_w_out=v_f_w_out, ln1_g=v_ln1_g, ln1_b=v_ln1_b, ln2_g=v_ln2_g, ln2_b=v_ln2_b)
    shapes = {n: w[n].shape for n in WEIGHT_NAMES}

    me = jnp.reshape(_my_index(), (1,)).astype(jnp.int32)

    def as_stored(name, a):
        return jnp.swapaxes(a, 1, 2) if name == "f_w_in" else a

    param_of = {"in": ("a_w_in", "b_w_in"), "out": ("a_w_out", "b_w_out"), "f_in": ("f_w_in",) * 2,
                "f_out": ("f_w_out",) * 2}
    order = [("small", None, None)]
    for layer in range(N_LAYERS):
        if layer == N_A_LAYERS:
            order.append(("kv_w", layer, 0))
        for key in ("in", "out", "f_in", "f_out"):
            order.append((key, layer, layer if key[0] == "f" or layer < N_A_LAYERS else layer - N_A_LAYERS))
    def place(key, layer, idx):
        if key == "small":
            return _place_own(_pack([w[n] for n in COL_F32], F32)[None], 0, me, out_dtype=F32, name="place_small")
        if key == "kv_w":
            return _place_own(w["kv_w"][None], 0, me, out_dtype=BF16, name="place_kv_w")
        name = param_of[key][0 if layer < N_A_LAYERS else 1]
        return _place_own(as_stored(name, w[name]), idx, me, out_dtype=BF16, name=f"place_{name}_{idx}")

    lands = [place(*o) for o in order]
    gather_handles, gather_token = _split_start([([l], [l]) for l in lands], scatter=False, name="gather_start")
    group_of = {(key, layer): g for g, (key, layer, _) in enumerate(order)}
    (got_s,) = _split_wait(gather_handles[0], gather_token, scatter=False, name="gather_wait_small")
    small = {n: w[n] for n in REPLICATED}
    for n, part in zip(COL_F32, _unpack(got_s, [w[n].shape for n in COL_F32], 1)):
        small[n] = _cols_full(part)
    n_meta, d = small["meta"].shape

    class Hooks:
        pending = None
        received = {}
        sent = {}

        @staticmethod
        def weight(layer, key, after):
            (got,) = _split_wait(gather_handles[group_of[(key, layer)]], after, scatter=False,
                                 name=f"gather_wait_{key}_{layer}")
            return got

        @staticmethod
        def collect(after):
            if Hooks.pending is not None:
                tag, names, handle = Hooks.pending
                got = _split_wait(handle, after, scatter=True, name=f"scatter_wait_{tag}")
                Hooks.received.update(zip(names, got))
                Hooks.pending = None

        @staticmethod
        def grads_ready(layer, part, big):
            if "kv_w" in big:
                big["kv_w"] = _cols_split(big["kv_w"][:, :shapes["kv_w"][1] * N_DEV]).astype(BF16)
            names = [(key, layer) for key in big]
            send = [big[key] for key in big]
            Hooks.collect(send[0])
            empty = [lax.empty(s.shape, s.dtype) for s in send]
            handles, token = _split_start([(send, empty)], scatter=True, name=f"scatter_start_{part}_{layer}")
            Hooks.pending = (f"{part}_{layer}", names, handles[0])
            Hooks.sent.update(zip(names, handles[0][2]))
            return token[0, 0]

    Hooks.pending, Hooks.received, Hooks.sent = None, {}, {}

    n_tok = n_meta + x.shape[1]
    tp = -(-n_tok // ROW_ALIGN) * ROW_ALIGN
    pad = jnp.zeros((tp - n_tok, d), F32)
    h0 = jnp.concatenate([small["meta"], x[0], pad])
    tgt = jnp.concatenate([jnp.zeros((n_meta, d), F32), loss_target[0], pad])
    loss_tile, d_h0, grads = _local_step(h0, tgt, n_meta, n_tok, _small_layouts(small), Hooks)
    g_small = _finish_small_grads(grads, d_h0, n_meta)
    loss = lax.psum(loss_tile[0, 0], MESH_AXES)
    grad_x = d_h0[n_meta:n_tok][None]

    rep = _pack([g_small[n] for n in REPLICATED], F32, row_quantum=16 * N_DEV)
    send = [_pack8([_cols_split(g_small[n]) for n in COL_F32], F32), rep.reshape(N_DEV, -1, PACK_COLS)]
    lands = _own_blocks(send, name="scatter_own_small")
    handles, token = _split_start([(send, lands)], scatter=True, name="scatter_start_small")

    g, delta, new_m, new_v = {}, {}, {}, {}
    layers_of = {
        "a_w_in": [("in", l) for l in range(N_A_LAYERS)], "a_w_out": [("out", l) for l in range(N_A_LAYERS)],
        "b_w_in": [("in", l) for l in range(N_A_LAYERS, N_LAYERS)],
        "b_w_out": [("out", l) for l in range(N_A_LAYERS, N_LAYERS)],
        "f_w_in": [("f_in", l) for l in range(N_LAYERS)], "f_w_out": [("f_out", l) for l in range(N_LAYERS)],
        "kv_w": [("kv_w", N_A_LAYERS)],
    }
    ready = [n for n in BIG + ("kv_w",) if all(t in Hooks.received for t in layers_of[n])]

    def done(names):
        return jnp.stack([g[n][(0,) * g[n].ndim] for n in names])

    for n in ready + [n for n in BIG + ("kv_w",) if n not in ready]:
        if n not in ready and Hooks.pending is not None:
            Hooks.collect(done(ready))
        lift = (lambda a: a[None]) if n == "kv_w" else (lambda a, n=n: as_stored(n, a))
        outs = _sum_adamw([Hooks.received[t] for t in layers_of[n]], [Hooks.sent[t] for t in layers_of[n]], me,
                          lift(w[n]), lift(m[n]), lift(v[n]), name="sum_adamw_" + n)
        g[n], delta[n], new_m[n], new_v[n] = [as_stored(n, o).reshape(shapes[n]) for o in outs]
    recv_s, recv_r = _split_wait(handles[0], done(BIG + ("kv_w",)), scatter=True, name="scatter_wait_small")
    sum_s = _sum8(recv_s, name="sum_grads_f32")
    g.update(zip(COL_F32, _unpack(sum_s, [shapes[n] for n in COL_F32], 0)))
    (got_r,) = _all_gather([_sum8(recv_r, name="sum_grads_replicated")], name="gather_replicated_sums")
    g.update(zip(REPLICATED, _unpack(got_r.reshape(-1, PACK_COLS), [shapes[n] for n in REPLICATED], 0)))

    for n in COL_F32 + REPLICATED:
        shp = shapes[n]
        dl, nm, nv = _adamw(_slab2d(w[n]), _slab2d(g[n]), _slab2d(m[n]), _slab2d(v[n]), name="adamw")
        delta[n], new_m[n], new_v[n] = dl.reshape(shp), nm.reshape(shp), nv.reshape(shp)
    return (loss, grad_x, *[g[n] for n in WEIGHT_NAMES], *[delta[n] for n in WEIGHT_NAMES],
            *[new_m[n] for n in WEIGHT_NAMES], *[new_v[n] for n in WEIGHT_NAMES])
```

```python
import math

import jax
import jax.numpy as jnp
from jax import lax
from jax.experimental import pallas as pl
from jax.experimental.pallas import tpu as pltpu

F32 = jnp.float32
BF16 = jnp.bfloat16

N_DEV = 8
MESH_AXES = ("x", "y", "c")
N_LAYERS = 4
N_A_LAYERS = 2
N_LRU_BLOCKS = 16
N_HEADS = 16
LRU_C = 8.0
DN_ALPHA = (2 * N_LAYERS) ** 0.25
LN_EPS = 1e-5
ADAM_LR, ADAM_B1, ADAM_B2, ADAM_EPS, ADAM_WD, ADAM_STEP = 0.001, 0.9, 0.999, 1e-08, 0.01, 10

LANES = 128
SUBLANES = 8
ROW_ALIGN = 128
VMEM_LIMIT_BYTES = 56 * 1024 * 1024
GELU_K = math.sqrt(2.0 / math.pi)
GELU_C = 0.044715
PACK_COLS = 1024


def _params(*sem):
    return pltpu.CompilerParams(dimension_semantics=sem, vmem_limit_bytes=VMEM_LIMIT_BYTES)


def _gelu(x):
    th = jnp.tanh(GELU_K * (x + GELU_C * x * x * x))
    return 0.5 * x * (1.0 + th)


def _gelu_and_grad(x):
    x2 = x * x
    th = jnp.tanh(GELU_K * (x + GELU_C * x2 * x))
    g = 0.5 * x * (1.0 + th)
    dg = 0.5 * (1.0 + th) + 0.5 * x * (1.0 - th * th) * (GELU_K * (1.0 + 3.0 * GELU_C * x2))
    return g, dg


def _sigmoid(x):
    return 1.0 / (1.0 + jnp.exp(-x))


def _expm1(x):
    small = x * (1.0 + 0.5 * x * (1.0 + (1.0 / 3.0) * x * (1.0 + 0.25 * x)))
    return jnp.where(jnp.abs(x) < 1e-2, small, jnp.exp(x) - 1.0)


def _softplus(x):
    e = jnp.exp(-jnp.abs(x))
    small = e * (1.0 - 0.5 * e * (1.0 - (2.0 / 3.0) * e))
    return jnp.maximum(x, 0.0) + jnp.where(e < 1e-2, small, jnp.log(1.0 + e))


def _shift_down(x, s):
    if s == 0:
        return x
    rows = lax.broadcasted_iota(jnp.int32, x.shape, 0)
    return jnp.where(rows >= s, pltpu.roll(x, s, 0), 0.0)


def _shift_up(x, s):
    if s == 0:
        return x
    n = x.shape[0]
    rows = lax.broadcasted_iota(jnp.int32, x.shape, 0)
    return jnp.where(rows < n - s, pltpu.roll(x, n - s, 0), 0.0)


def _dot_nn(a, b):
    return lax.dot_general(a, b, (((1,), (0,)), ((), ())), preferred_element_type=F32)


def _dot_nt(a, b):
    return lax.dot_general(a, b, (((1,), (1,)), ((), ())), preferred_element_type=F32)


def _dot_tn(a, b):
    return lax.dot_general(a, b, (((0,), (0,)), ((), ())), preferred_element_type=F32)


def _rows8(vals, width):
    rows = lax.broadcasted_iota(jnp.int32, (8, width), 0)
    out = jnp.zeros((8, width), F32)
    for k, v in enumerate(vals):
        out = jnp.where(rows == k, jnp.broadcast_to(v, (8, width)), out)
    return out


def _tile(n, prefer):
    for c in prefer:
        if n % c == 0:
            return c
    return n


def _mm_nn(a, b, *, tn, out_dtype, name):
    m, k = a.shape
    n = b.shape[1]

    def body(a_ref, b_ref, o_ref):
        o_ref[...] = _dot_nn(a_ref[...], b_ref[...]).astype(o_ref.dtype)

    return pl.pallas_call(
        body, name=name, grid=(n // tn,),
        in_specs=[pl.BlockSpec((m, k), lambda j: (0, 0)), pl.BlockSpec((k, tn), lambda j: (0, j))],
        out_specs=pl.BlockSpec((m, tn), lambda j: (0, j)),
        out_shape=jax.ShapeDtypeStruct((m, n), out_dtype),
        compiler_params=_params("parallel"),
    )(a, b)


def _proj_in(h_bf, g_in, *, shard_major, name, transposed=False):
    t, k = h_bf.shape
    n = g_in.shape[1] if transposed else g_in.shape[2]

    def body(a_ref, b_ref, o_ref):
        o_ref[...] = _dot_nt(a_ref[...], b_ref[...]) if transposed else _dot_nn(a_ref[...], b_ref[...])

    if shard_major:
        out_spec = pl.BlockSpec((None, t, n), lambda j: (j, 0, 0))
        out_shape = jax.ShapeDtypeStruct((N_DEV, t, n), F32)
    else:
        out_spec = pl.BlockSpec((t, n), lambda j: (0, j))
        out_shape = jax.ShapeDtypeStruct((t, N_DEV * n), F32)
    return pl.pallas_call(
        body, name=name, grid=(N_DEV,),
        in_specs=[pl.BlockSpec((t, k), lambda j: (0, 0)),
                  pl.BlockSpec((None,) + g_in.shape[1:], lambda j: (j, 0, 0))],
        out_specs=out_spec, out_shape=out_shape,
        compiler_params=_params("parallel"),
    )(h_bf, g_in)


def _out_ln(y3, g_out, hin, g, b, *, n_valid, tm, name):
    nj, t, kj = y3.shape
    _, r, d = g_out.shape

    def body(y_ref, w_ref, hin_ref, g_ref, b_ref, s_ref, h_ref, hb_ref):
        w = w_ref[...].reshape(N_DEV * r, d)
        s = DN_ALPHA * hin_ref[...]
        for jj in range(nj):
            s = s + _dot_nn(y_ref[jj], w[jj * kj:(jj + 1) * kj])
        mu = jnp.mean(s, axis=-1, keepdims=True)
        xc = s - mu
        var = jnp.mean(xc * xc, axis=-1, keepdims=True)
        h = xc * lax.rsqrt(var + LN_EPS) * g_ref[...] + b_ref[...]
        s_ref[...] = s
        h_ref[...] = h
        rows = pl.program_id(0) * tm + lax.broadcasted_iota(jnp.int32, (tm, d), 0)
        hb_ref[...] = jnp.where(rows < n_valid, h, 0.0).astype(BF16)

    row = pl.BlockSpec((tm, d), lambda i: (i, 0))
    vec = pl.BlockSpec((1, d), lambda i: (0, 0))
    return pl.pallas_call(
        body, name=name, grid=(t // tm,),
        in_specs=[pl.BlockSpec((nj, tm, kj), lambda i: (0, i, 0)),
                  pl.BlockSpec((N_DEV, r, d), lambda i: (0, 0, 0)), row, vec, vec],
        out_specs=[row, row, row],
        out_shape=[jax.ShapeDtypeStruct((t, d), F32), jax.ShapeDtypeStruct((t, d), F32),
                   jax.ShapeDtypeStruct((t, d), BF16)],
        compiler_params=_params("parallel"),
    )(y3, g_out, hin, g, b)


def _out_bwd(ds_bf, g_out, *, tm, name):
    t, d = ds_bf.shape
    r = g_out.shape[1]

    def body(a_ref, w_ref, o_ref):
        o_ref[...] = _dot_nt(a_ref[...], w_ref[...].reshape(N_DEV * r, d))

    return pl.pallas_call(
        body, name=name, grid=(t // tm,),
        in_specs=[pl.BlockSpec((tm, d), lambda i: (i, 0)),
                  pl.BlockSpec((N_DEV, r, d), lambda i: (0, 0, 0))],
        out_specs=pl.BlockSpec((tm, N_DEV * r), lambda i: (i, 0)),
        out_shape=jax.ShapeDtypeStruct((t, N_DEV * r), F32),
        compiler_params=_params("parallel"),
    )(ds_bf, g_out)


def _in_bwd(dact, g_in, add, *, tm, name, alpha=DN_ALPHA, transposed=False):
    t = dact.shape[-2]
    _, k, n = g_in.shape
    if transposed:
        k, n = n, k
    halves = dact.shape[0] == 2 and dact.ndim == 3
    per = N_DEV // 2

    def body(a_ref, b_ref, add_ref, o_ref, acc_ref):
        j = pl.program_id(1)

        @pl.when(j == 0)
        def _():
            acc_ref[...] = alpha * add_ref[...]

        acc_ref[...] += _dot_nn(a_ref[...], b_ref[...]) if transposed else _dot_nt(a_ref[...], b_ref[...])

        @pl.when(j == N_DEV - 1)
        def _():
            o_ref[...] = acc_ref[...]

    if halves:
        a_spec = pl.BlockSpec((None, tm, n), lambda i, j: (j // per, i, j % per))
    elif dact.ndim == 4:
        a_spec = pl.BlockSpec((None, None, tm, n), lambda i, j: (j // per, j % per, i, 0))
    else:
        a_spec = pl.BlockSpec((None, tm, n), lambda i, j: (j, i, 0))
    return pl.pallas_call(
        body, name=name, grid=(t // tm, N_DEV),
        in_specs=[a_spec, pl.BlockSpec((None,) + g_in.shape[1:], lambda i, j: (j, 0, 0)),
                  pl.BlockSpec((tm, k), lambda i, j: (i, 0))],
        out_specs=pl.BlockSpec((tm, k), lambda i, j: (i, 0)),
        out_shape=jax.ShapeDtypeStruct((t, k), F32),
        scratch_shapes=[pltpu.VMEM((tm, k), F32)],
        compiler_params=_params("parallel", "arbitrary"),
    )(dact, g_in, add)


def _mm_nt_full(a, b, add, *, tm, name):
    t, n = a.shape
    k = b.shape[0]

    def body(a_ref, b_ref, add_ref, o_ref):
        o_ref[...] = add_ref[...] + _dot_nt(a_ref[...], b_ref[...])

    return pl.pallas_call(
        body, name=name, grid=(t // tm,),
        in_specs=[pl.BlockSpec((tm, n), lambda i: (i, 0)), pl.BlockSpec((k, n), lambda i: (0, 0)),
                  pl.BlockSpec((tm, k), lambda i: (i, 0))],
        out_specs=pl.BlockSpec((tm, k), lambda i: (i, 0)),
        out_shape=jax.ShapeDtypeStruct((t, k), F32),
        compiler_params=_params("parallel"),
    )(a, b, add)


def _w_in_grad(h_bf, dact, *, name, transposed=False):
    t, k = h_bf.shape
    halves = dact.shape[0] == 2 and dact.ndim == 3
    per = N_DEV // 2
    n = dact.shape[-1] // per if halves else dact.shape[-1]

    def body(a_ref, b_ref, o_ref):
        if transposed:
            o_ref[...] = _dot_tn(b_ref[...], a_ref[...]).astype(BF16)
        else:
            o_ref[...] = _dot_tn(a_ref[...], b_ref[...]).astype(BF16)

    if halves:
        b_spec = pl.BlockSpec((None, t, n), lambda j: (j // per, 0, j % per))
    elif dact.ndim == 4:
        b_spec = pl.BlockSpec((None, None, t, n), lambda j: (j // per, j % per, 0, 0))
    else:
        b_spec = pl.BlockSpec((None, t, n), lambda j: (j, 0, 0))
    return pl.pallas_call(
        body, name=name, grid=(N_DEV,),
        in_specs=[pl.BlockSpec((t, k), lambda j: (0, 0)), b_spec],
        out_specs=pl.BlockSpec((None, n, k) if transposed else (None, k, n), lambda j: (j, 0, 0)),
        out_shape=jax.ShapeDtypeStruct((N_DEV, n, k) if transposed else (N_DEV, k, n), BF16),
        compiler_params=_params("parallel"),
    )(h_bf, dact)


def _w_out_grad(y3, ds_bf, r, *, name):
    nj, t, kj = y3.shape
    d = ds_bf.shape[1]
    unit = r * LANES // math.gcd(r, LANES)
    ks = max([c for c in range(unit, min(kj, 768) + 1, unit) if kj % c == 0], default=kj)
    gsz = ks // r
    per = kj // ks

    def body(a_ref, b_ref, o_ref):
        o_ref[...] = _dot_tn(a_ref[...], b_ref[...]).reshape(gsz, r, d).astype(BF16)

    return pl.pallas_call(
        body, name=name, grid=(nj * per,),
        in_specs=[pl.BlockSpec((None, t, ks), lambda j: (j // per, 0, j % per)),
                  pl.BlockSpec((t, d), lambda j: (0, 0))],
        out_specs=pl.BlockSpec((gsz, r, d), lambda j: (j, 0, 0)),
        out_shape=jax.ShapeDtypeStruct((N_DEV, r, d), BF16),
        compiler_params=_params("parallel"),
    )(y3, ds_bf)


def _mm_tn_cols(a, b, *, tn, name):
    t, m = a.shape
    n = b.shape[1]

    def body(a_ref, b_ref, o_ref):
        o_ref[...] = _dot_tn(a_ref[...], b_ref[...])

    return pl.pallas_call(
        body, name=name, grid=(n // tn,),
        in_specs=[pl.BlockSpec((t, m), lambda j: (0, 0)), pl.BlockSpec((t, tn), lambda j: (0, j))],
        out_specs=pl.BlockSpec((m, tn), lambda j: (0, j)),
        out_shape=jax.ShapeDtypeStruct((m, n), F32),
        compiler_params=_params("parallel"),
    )(a, b)


def _ln_bwd(dout, s, g, *, tm, name):
    t, d = s.shape

    def body(do_ref, s_ref, g_ref, ds_ref, dsb_ref, gb_ref):
        i = pl.program_id(0)
        sv = s_ref[...]
        do = do_ref[...]
        mu = jnp.mean(sv, axis=-1, keepdims=True)
        xc = sv - mu
        var = jnp.mean(xc * xc, axis=-1, keepdims=True)
        rstd = lax.rsqrt(var + LN_EPS)
        xhat = xc * rstd
        dxhat = do * g_ref[...]
        m1 = jnp.mean(dxhat, axis=-1, keepdims=True)
        m2 = jnp.mean(dxhat * xhat, axis=-1, keepdims=True)
        ds = rstd * (dxhat - m1 - xhat * m2)
        ds_ref[...] = ds
        dsb_ref[...] = ds.astype(BF16)
        upd = _rows8([jnp.sum(do * xhat, axis=0, keepdims=True), jnp.sum(do, axis=0, keepdims=True)], d)

        @pl.when(i == 0)
        def _():
            gb_ref[...] = upd

        @pl.when(i > 0)
        def _():
            gb_ref[...] += upd

    row = pl.BlockSpec((tm, d), lambda i: (i, 0))
    return pl.pallas_call(
        body, name=name, grid=(t // tm,),
        in_specs=[row, row, pl.BlockSpec((1, d), lambda i: (0, 0))],
        out_specs=[row, row, pl.BlockSpec((8, d), lambda i: (0, 0))],
        out_shape=[jax.ShapeDtypeStruct((t, d), F32), jax.ShapeDtypeStruct((t, d), BF16),
                   jax.ShapeDtypeStruct((8, d), F32)],
        compiler_params=_params("arbitrary"),
    )(dout, s, g)


def _roll_down(x, s):
    return x if s == 0 else pltpu.roll(x, s, 0)


def _conv_taps(x, wb, width):
    y = jnp.broadcast_to(wb[width:width + 1, :], x.shape)
    for k in range(width):
        y = y + _roll_down(x, width - 1 - k) * wb[k:k + 1, :]
    return y


def _conv_taps_bwd(dy, x, wb, width):
    n = dy.shape[0]
    dx = jnp.zeros_like(dy)
    rows = []
    for k in range(width):
        s = width - 1 - k
        dy_up = dy if s == 0 else pltpu.roll(dy, n - s, 0)
        dx = dx + dy_up * wb[k:k + 1, :]
        rows.append(jnp.sum(dy_up * x, axis=0, keepdims=True))
    rows.append(jnp.sum(dy, axis=0, keepdims=True))
    t_idx = lax.broadcasted_iota(jnp.int32, dy.shape, 0)
    return jnp.where(t_idx < n - (width - 1), dx, 0.0), _rows8(rows, dy.shape[1])


def _convglu_fwd(z3, fwb3, *, name):
    _, t, n = z3.shape
    half = N_DEV // 2
    nc = pl.cdiv(n, LANES)

    def body(zg_ref, zv_ref, wg_ref, wv_ref, y_ref):
        gate = _conv_taps(zg_ref[...], wg_ref[...], 3)
        val = _conv_taps(zv_ref[...], wv_ref[...], 3)
        y_ref[...] = (_gelu(gate) * val).astype(BF16)

    zblk = lambda off: pl.BlockSpec((None, t, LANES), lambda j, c: (j + off, 0, c))
    wblk = lambda off: pl.BlockSpec((None, 8, LANES), lambda j, c: (j + off, 0, c))
    return pl.pallas_call(
        body, name=name, grid=(half, nc),
        in_specs=[zblk(0), zblk(half), wblk(0), wblk(half)],
        out_specs=zblk(0),
        out_shape=jax.ShapeDtypeStruct((half, t, n), BF16),
        compiler_params=_params("parallel", "parallel"),
    )(z3, z3, fwb3, fwb3)


def _ffn_bwd_mid(ds_bf, g_out, z3, fwb3, *, name):
    t, d = ds_bf.shape
    r = g_out.shape[1]
    n = z3.shape[2]
    half = N_DEV // 2
    nc = pl.cdiv(n, LANES)
    assert n == 2 * r

    def body(ds_ref, w_ref, zg_ref, zv_ref, wg_ref, wv_ref, dz_ref, dwb_ref, wsc_ref):
        c = pl.program_id(1)

        @pl.when(c == 0)
        def _():
            wsc_ref[0:r, :] = w_ref[0]
            wsc_ref[r:2 * r, :] = w_ref[1]
            if nc * LANES > n:
                wsc_ref[n:nc * LANES, :] = jnp.zeros((nc * LANES - n, d), BF16)

        w = wsc_ref[pl.ds(pl.multiple_of(c * LANES, LANES), LANES), :]
        dyf = _dot_nt(ds_ref[...], w)
        zg, zv = zg_ref[...], zv_ref[...]
        wg, wv = wg_ref[...], wv_ref[...]
        gate = _conv_taps(zg, wg, 3)
        val = _conv_taps(zv, wv, 3)
        gl, dgl = _gelu_and_grad(gate)
        dzg, dwg = _conv_taps_bwd(dyf * val * dgl, zg, wg, 3)
        dzv, dwv = _conv_taps_bwd(dyf * gl, zv, wv, 3)
        dz_ref[0] = dzg.astype(BF16)
        dz_ref[1] = dzv.astype(BF16)
        dwb_ref[0] = dwg
        dwb_ref[1] = dwv

    zblk = lambda off: pl.BlockSpec((None, t, LANES), lambda j, c: (j + off, 0, c))
    wblk = lambda off: pl.BlockSpec((None, 8, LANES), lambda j, c: (j + off, 0, c))
    return pl.pallas_call(
        body, name=name, grid=(half, nc),
        in_specs=[pl.BlockSpec((t, d), lambda j, c: (0, 0)),
                  pl.BlockSpec((2, r, d), lambda j, c: (j, 0, 0)),
                  zblk(0), zblk(half), wblk(0), wblk(half)],
        out_specs=[pl.BlockSpec((2, None, t, LANES), lambda j, c: (0, j, 0, c)),
                   pl.BlockSpec((2, None, 8, LANES), lambda j, c: (0, j, 0, c))],
        out_shape=[jax.ShapeDtypeStruct((2, half, t, n), BF16), jax.ShapeDtypeStruct((2, half, 8, n), F32)],
        scratch_shapes=[pltpu.VMEM((nc * LANES, d), BF16)],
        compiler_params=_params("parallel", "arbitrary"),
    )(ds_bf, g_out, z3, z3, fwb3, fwb3)


def _conv_a_fwd(gr, cwb, *, cb, name):
    t, r2 = gr.shape
    r = r2 // 2
    nb = r // cb

    def body(x_ref, w_ref, o_ref):
        o_ref[...] = _conv_taps(x_ref[...], w_ref[...], 4)

    return pl.pallas_call(
        body, name=name, grid=(nb,),
        in_specs=[pl.BlockSpec((t, cb), lambda j: (0, j + nb)), pl.BlockSpec((8, cb), lambda j: (0, j))],
        out_specs=pl.BlockSpec((t, cb), lambda j: (0, j)),
        out_shape=jax.ShapeDtypeStruct((t, r), F32),
        compiler_params=_params("parallel"),
    )(gr, cwb)


def _gates_fwd(rec, bd_r, bd_i, vecs, *, tm, name):
    t, r_dim = rec.shape
    nb, cb, _ = bd_r.shape

    def body(x_ref, wr_ref, wi_ref, v_ref, a_ref, u_ref, r_ref, i_ref):
        x = x_ref[...]
        xb = x.astype(BF16)
        v = v_ref[...]
        r = _sigmoid(_dot_nn(xb, wr_ref[...]) + v[0:1, :])
        i = _sigmoid(_dot_nn(xb, wi_ref[...]) + v[1:2, :])
        log_a = (-LRU_C) * r * _softplus(-v[2:3, :])
        a_ref[...] = jnp.exp(log_a)
        u_ref[...] = jnp.sqrt(-_expm1(2.0 * log_a)) * (i * x)
        r_ref[...] = r
        i_ref[...] = i

    blk = pl.BlockSpec((tm, cb), lambda j, i: (i, j))
    wspec = pl.BlockSpec((None, cb, cb), lambda j, i: (j, 0, 0))
    out = jax.ShapeDtypeStruct((t, r_dim), F32)
    return pl.pallas_call(
        body, name=name, grid=(nb, t // tm),
        in_specs=[blk, wspec, wspec, pl.BlockSpec((8, cb), lambda j, i: (0, j))],
        out_specs=[blk, blk, blk, blk],
        out_shape=[out, out, out, out],
        compiler_params=_params("parallel", "parallel"),
    )(rec, bd_r, bd_i, vecs)


def _scan_fwd(a, u, gr, *, cb, name):
    t, r = a.shape
    nb = r // cb
    seg = t // SUBLANES

    def body(a_ref, u_ref, g_ref, h_ref, y_ref, p_ref):
        def step(k, carry):
            h, p = carry
            rows = pl.ds(k, SUBLANES, stride=seg)
            av = a_ref[rows, :]
            h = av * h + u_ref[rows, :]
            p = av * p
            h_ref[rows, :] = h
            p_ref[rows, :] = p
            return h, p

        h_fin, p_fin = lax.fori_loop(0, seg, step, (jnp.zeros((SUBLANES, cb), F32), jnp.ones((SUBLANES, cb), F32)),
                                     unroll=4)
        carry = h_fin[0:1, :]
        for s in range(1, SUBLANES):
            rows = slice(s * seg, (s + 1) * seg)
            h_ref[rows, :] = h_ref[rows, :] + p_ref[rows, :] * carry
            carry = h_fin[s:s + 1, :] + p_fin[s:s + 1, :] * carry
        y_ref[...] = (_gelu(g_ref[...]) * h_ref[...]).astype(BF16)

    blk = pl.BlockSpec((t, cb), lambda j: (0, j))
    return pl.pallas_call(
        body, name=name, grid=(nb,),
        in_specs=[blk, blk, blk],
        out_specs=[blk, pl.BlockSpec((None, t, cb), lambda j: (0, 0, j))],
        out_shape=[jax.ShapeDtypeStruct((t, r), F32), jax.ShapeDtypeStruct((1, t, r), BF16)],
        scratch_shapes=[pltpu.VMEM((t, cb), F32)],
        compiler_params=_params("parallel"),
    )(a, u, gr)


def _scan_bwd(dy, gr, hr, a, *, cb, name):
    t, r = a.shape
    nb = r // cb
    seg = t // SUBLANES

    def body(dy_ref, g_ref, h_ref, a_ref, dh_ref, da_ref, dg_ref, q_ref):
        gl, dgl = _gelu_and_grad(g_ref[...])
        dyv = dy_ref[...]
        dh_ref[...] = dyv * gl
        dg_ref[...] = (dyv * h_ref[...] * dgl).astype(BF16)

        def step(k, carry):
            cin, q = carry
            rows = pl.ds(seg - 1 - k, SUBLANES, stride=seg)
            dh = dh_ref[rows, :] + cin
            dh_ref[rows, :] = dh
            q_ref[rows, :] = q
            av = a_ref[rows, :]
            return av * dh, av * q

        c_fin, q_fin = lax.fori_loop(0, seg, step, (jnp.zeros((SUBLANES, cb), F32), jnp.ones((SUBLANES, cb), F32)),
                                     unroll=4)
        carry = c_fin[SUBLANES - 1:SUBLANES, :]
        for s in range(SUBLANES - 2, -1, -1):
            rows = slice(s * seg, (s + 1) * seg)
            dh_ref[rows, :] = dh_ref[rows, :] + q_ref[rows, :] * carry
            carry = c_fin[s:s + 1, :] + q_fin[s:s + 1, :] * carry
        da_ref[...] = dh_ref[...] * _shift_down(h_ref[...], 1)

    blk = pl.BlockSpec((t, cb), lambda j: (0, j))
    return pl.pallas_call(
        body, name=name, grid=(nb,),
        in_specs=[blk, blk, blk, blk],
        out_specs=[blk, blk, blk],
        out_shape=[jax.ShapeDtypeStruct((t, r), F32), jax.ShapeDtypeStruct((t, r), F32),
                   jax.ShapeDtypeStruct((t, r), BF16)],
        scratch_shapes=[pltpu.VMEM((t, cb), F32)],
        compiler_params=_params("parallel"),
    )(dy, gr, hr, a)


def _gates_bwd(rec, r, i, a, dh, da, bd_r, bd_i, vecs, *, tm, name):
    t, r_dim = rec.shape
    nb, cb, _ = bd_r.shape

    def body(x_ref, r_ref, i_ref, a_ref, dh_ref, da_ref, wr_ref, wi_ref, v_ref, dx_ref, dpr_ref, dpi_ref, dv_ref):
        step = pl.program_id(1)
        x, r, i, a, dh, da = x_ref[...], r_ref[...], i_ref[...], a_ref[...], dh_ref[...], da_ref[...]
        lam = v_ref[...][2:3, :]
        sp = _softplus(-lam)
        a2 = a * a
        mult = jnp.sqrt(-_expm1(2.0 * (-LRU_C) * r * sp))
        d_i = dh * mult * x
        d_log_a = da * a - (dh * i * x) * a2 / mult
        d_r = d_log_a * ((-LRU_C) * sp)
        d_sp = jnp.sum(d_log_a * ((-LRU_C) * r), axis=0, keepdims=True)
        d_pre_r = d_r * r * (1.0 - r)
        d_pre_i = d_i * i * (1.0 - i)
        dprb = d_pre_r.astype(BF16)
        dpib = d_pre_i.astype(BF16)
        dx_ref[...] = dh * mult * i + _dot_nt(dprb, wr_ref[...]) + _dot_nt(dpib, wi_ref[...])
        dpr_ref[...] = dprb
        dpi_ref[...] = dpib
        upd = _rows8([jnp.sum(d_pre_r, axis=0, keepdims=True), jnp.sum(d_pre_i, axis=0, keepdims=True),
                      -d_sp * _sigmoid(-lam)], cb)

        @pl.when(step == 0)
        def _():
            dv_ref[...] = upd

        @pl.when(step > 0)
        def _():
            dv_ref[...] += upd

    blk = pl.BlockSpec((tm, cb), lambda j, i: (i, j))
    wspec = pl.BlockSpec((None, cb, cb), lambda j, i: (j, 0, 0))
    vspec = pl.BlockSpec((8, cb), lambda j, i: (0, j))
    return pl.pallas_call(
        body, name=name, grid=(nb, t // tm),
        in_specs=[blk] * 6 + [wspec, wspec, vspec],
        out_specs=[blk, blk, blk, vspec],
        out_shape=[jax.ShapeDtypeStruct((t, r_dim), F32), jax.ShapeDtypeStruct((t, r_dim), BF16),
                   jax.ShapeDtypeStruct((t, r_dim), BF16), jax.ShapeDtypeStruct((8, r_dim), F32)],
        compiler_params=_params("parallel", "arbitrary"),
    )(rec, r, i, a, dh, da, bd_r, bd_i, vecs)


def _bd_grad(rec, dpr, dpi, *, cb, name):
    t, r = rec.shape
    nb = r // cb

    def body(x_ref, dr_ref, di_ref, gr_ref, gi_ref):
        xb = x_ref[...].astype(BF16)
        gr_ref[...] = _dot_tn(xb, dr_ref[...])
        gi_ref[...] = _dot_tn(xb, di_ref[...])

    blk = pl.BlockSpec((t, cb), lambda j: (0, j))
    wspec = pl.BlockSpec((None, cb, cb), lambda j: (j, 0, 0))
    out = jax.ShapeDtypeStruct((nb, cb, cb), F32)
    return pl.pallas_call(
        body, name=name, grid=(nb,),
        in_specs=[blk, blk, blk], out_specs=[wspec, wspec], out_shape=[out, out],
        compiler_params=_params("parallel"),
    )(rec, dpr, dpi)


def _conv_a_bwd(d_rec, gr, dgate, cwb, *, cb, name):
    t, r = d_rec.shape
    nb = r // cb

    def body(dy_ref, x_ref, dg_ref, w_ref, dact_ref, dw_ref):
        dx, dw = _conv_taps_bwd(dy_ref[...], x_ref[...], w_ref[...], 4)
        dact_ref[0] = dg_ref[...]
        dact_ref[1] = dx.astype(BF16)
        dw_ref[...] = dw

    blk = pl.BlockSpec((t, cb), lambda j: (0, j))
    vspec = pl.BlockSpec((8, cb), lambda j: (0, j))
    return pl.pallas_call(
        body, name=name, grid=(nb,),
        in_specs=[blk, pl.BlockSpec((t, cb), lambda j: (0, j + nb)), blk, vspec],
        out_specs=[pl.BlockSpec((2, t, cb), lambda j: (0, 0, j)), vspec],
        out_shape=[jax.ShapeDtypeStruct((2, t, r), BF16), jax.ShapeDtypeStruct((8, r), F32)],
        compiler_params=_params("parallel"),
    )(d_rec, gr, dgate, cwb)


def _split3(x):
    p0 = x.astype(BF16)
    r1 = x - p0.astype(F32)
    p1 = r1.astype(BF16)
    p2 = (r1 - p1.astype(F32)).astype(BF16)
    return p0, p1, p2


def _fgate_fwd(fp, fb, *, tq, name):
    t = fp.shape[0]

    def body(f_ref, b_ref, c_ref, ct_ref):
        logf = -_softplus(-(f_ref[...] + b_ref[...]))
        rows = pl.program_id(0) * tq + lax.broadcasted_iota(jnp.int32, (tq, t), 0)
        cols = lax.broadcasted_iota(jnp.int32, (tq, t), 1)
        tri = (cols <= rows).astype(BF16)
        p0, p1, p2 = _split3(logf)
        c = _dot_nn(tri, p0) + _dot_nn(tri, p1) + _dot_nn(tri, p2)
        c_ref[...] = c
        ct_ref[...] = c.T

    return pl.pallas_call(
        body, name=name, grid=(t // tq,),
        in_specs=[pl.BlockSpec((t, LANES), lambda i: (0, 0)), pl.BlockSpec((1, LANES), lambda i: (0, 0))],
        out_specs=[pl.BlockSpec((tq, LANES), lambda i: (i, 0)), pl.BlockSpec((LANES, tq), lambda i: (0, i))],
        out_shape=[jax.ShapeDtypeStruct((t, LANES), F32), jax.ShapeDtypeStruct((LANES, t), F32)],
        compiler_params=_params("parallel"),
    )(fp, fb)


def _fgate_bwd(dct, fp, fb, *, tq, name):
    t = fp.shape[0]

    def body(d_ref, f_ref, b_ref, o_ref, db_ref):
        i = pl.program_id(0)
        rows = lax.broadcasted_iota(jnp.int32, (t, tq), 0)
        cols = i * tq + lax.broadcasted_iota(jnp.int32, (t, tq), 1)
        tri = (rows >= cols).astype(BF16)
        p0, p1, p2 = _split3(d_ref[...])
        dlogf = (_dot_nn(p0, tri) + _dot_nn(p1, tri) + _dot_nn(p2, tri)).T
        df = dlogf * _sigmoid(-(f_ref[...] + b_ref[...]))
        o_ref[...] = df.astype(BF16)
        upd = _rows8([jnp.sum(df, axis=0, keepdims=True)], LANES)

        @pl.when(i == 0)
        def _():
            db_ref[...] = upd

        @pl.when(i > 0)
        def _():
            db_ref[...] += upd

    return pl.pallas_call(
        body, name=name, grid=(t // tq,),
        in_specs=[pl.BlockSpec((LANES, t), lambda i: (0, 0)), pl.BlockSpec((tq, LANES), lambda i: (i, 0)),
                  pl.BlockSpec((1, LANES), lambda i: (0, 0))],
        out_specs=[pl.BlockSpec((tq, LANES), lambda i: (i, 0)), pl.BlockSpec((8, LANES), lambda i: (0, 0))],
        out_shape=[jax.ShapeDtypeStruct((t, LANES), BF16), jax.ShapeDtypeStruct((8, LANES), F32)],
        compiler_params=_params("arbitrary"),
    )(dct, fp, fb)


def _pair_sum(a, b, *, tm, name):
    t, d = a.shape

    def body(a_ref, b_ref, o_ref):
        o_ref[...] = (a_ref[...] + b_ref[...]).astype(BF16)

    row = pl.BlockSpec((tm, d), lambda i: (i, 0))
    return pl.pallas_call(
        body, name=name, grid=(t // tm,), in_specs=[row, row], out_specs=row,
        out_shape=jax.ShapeDtypeStruct((t, d), BF16), compiler_params=_params("parallel"),
    )(a, b)


FWD_HEAD_TILES = 2
BWD_HEAD_TILES = 1


def _head_block_width(dh, tiles):
    return tiles * LANES if tiles * LANES // dh <= 8 else LANES


def _head_masks(dh, bw):
    lane = lax.broadcasted_iota(jnp.int32, (1, bw), 1)
    return [((lane >= e * dh) & (lane < (e + 1) * dh)) for e in range(bw // dh)]


def _head_c_row(ct_blk, head):
    sub = lax.broadcasted_iota(jnp.int32, ct_blk.shape, 0)
    return jnp.sum(jnp.where(sub == head, ct_blk, 0.0), axis=0, keepdims=True)


def _attn_weights(qm, k, c_row, q0):
    tq, t = qm.shape[0], k.shape[0]
    s = _dot_nt(qm, k) - c_row
    qi = q0 + lax.broadcasted_iota(jnp.int32, (tq, t), 0)
    ki = lax.broadcasted_iota(jnp.int32, (tq, t), 1)
    s = jnp.where(ki <= qi, s, -jnp.inf)
    m = jnp.max(s, axis=-1, keepdims=True)
    e = jnp.exp(s - m)
    return e, m, 1.0 / jnp.sum(e, axis=-1, keepdims=True)


def _key_buckets(t, tq):
    return tuple(sorted({min(-(-(i * tq) // LANES) * LANES, t) for i in range(1, t // tq + 1)}))


def _for_prefix(needed, buckets, fn):
    prev = 0
    for length in buckets:
        pl.when((needed > prev) & (needed <= length))(lambda length=length: fn(length))
        prev = length


def _attn_fwd(qg, kv, ct, *, tq, name):
    t, d2 = qg.shape
    d = d2 // 2
    dh = d // N_HEADS
    bw = _head_block_width(dh, FWD_HEAD_TILES)
    hpb = bw // dh
    nhb = d // bw
    scale = dh ** -0.5
    buckets = _key_buckets(t, tq)

    def body(q_ref, og_ref, k_ref, v_ref, ct_ref, o_ref, y_ref, st_ref):
        hb = pl.program_id(0)
        q0 = pl.program_id(1) * tq

        def run(length):
            qs = q_ref[...] * scale
            k = k_ref[0:length, :]
            v = v_ref[0:length, :]
            o = jnp.zeros((tq, bw), F32)
            lane = lax.broadcasted_iota(jnp.int32, (tq, LANES), 1)
            stats = jnp.zeros((tq, LANES), F32)
            for e, msk in enumerate(_head_masks(dh, bw)):
                c_row = _head_c_row(ct_ref[:, 0:length], hb * hpb + e)
                w, m, inv = _attn_weights(jnp.where(msk, qs, 0.0).astype(BF16), k, c_row, q0)
                o = o + _dot_nn(w.astype(BF16), jnp.where(msk, v, jnp.zeros_like(v))) * inv
                stats = jnp.where(lane == e, m, jnp.where(lane == hpb + e, inv, stats))
            o_ref[...] = o
            y_ref[...] = (o * _sigmoid(og_ref[...])).astype(BF16)
            st_ref[...] = stats

        _for_prefix(q0 + tq, buckets, run)

    qblk = pl.BlockSpec((tq, bw), lambda h, i: (i, h))
    return pl.pallas_call(
        body, name=name, grid=(nhb, t // tq),
        in_specs=[qblk, pl.BlockSpec((tq, bw), lambda h, i: (i, h + nhb)),
                  pl.BlockSpec((t, bw), lambda h, i: (0, h)), pl.BlockSpec((t, bw), lambda h, i: (0, h + nhb)),
                  pl.BlockSpec((N_HEADS, t), lambda h, i: (0, 0))],
        out_specs=[qblk, pl.BlockSpec((None, tq, bw), lambda h, i: (0, i, h)),
                   pl.BlockSpec((None, tq, LANES), lambda h, i: (h, i, 0))],
        out_shape=[jax.ShapeDtypeStruct((t, d), F32), jax.ShapeDtypeStruct((1, t, d), BF16),
                   jax.ShapeDtypeStruct((nhb, t, LANES), F32)],
        compiler_params=_params("parallel", "parallel"),
    )(qg, qg, kv, kv, ct)


def _attn_bwd(dy, qg, o, stats, kv, ct, *, tq, name):
    t, d2 = qg.shape
    d = d2 // 2
    dh = d // N_HEADS
    bw = _head_block_width(dh, BWD_HEAD_TILES)
    hpb = bw // dh
    nhb = d // bw
    scale = dh ** -0.5
    n_q = t // tq
    hpb_f = _head_block_width(dh, FWD_HEAD_TILES) // dh
    ratio = hpb_f // hpb
    chunk = 2 * LANES

    def body(dy_ref, q_ref, og_ref, o_ref, st_ref, k_ref, v_ref, ct_ref, dqg_ref, dk_ref, dv_ref, dc_ref, dcq_ref):
        hb = pl.program_id(0)
        step = pl.program_id(1)

        @pl.when(step == 0)
        def _():
            dk_ref[...] = jnp.zeros((t, bw), F32)
            dv_ref[...] = jnp.zeros((t, bw), F32)
            dc_ref[...] = jnp.zeros((8, t), F32)

        def run(i):
            q0 = i * tq
            length = min(-(-(q0 + tq) // LANES) * LANES, t)
            qs = q_ref[...] * scale
            sg = _sigmoid(og_ref[...])
            dyv = dy_ref[...]
            ov = o_ref[...]
            do = dyv * sg
            dqg_ref[1] = (dyv * ov * sg * (1.0 - sg)).astype(BF16)
            lane = lax.broadcasted_iota(jnp.int32, (tq, LANES), 1)
            stats = st_ref[...]
            masks = _head_masks(dh, bw)
            heads = []
            for e, msk in enumerate(masks):
                pos = (hb % ratio) * hpb + e
                m = jnp.sum(jnp.where(lane == pos, stats, 0.0), axis=1, keepdims=True)
                inv = jnp.sum(jnp.where(lane == hpb_f + pos, stats, 0.0), axis=1, keepdims=True)
                delta = jnp.sum(jnp.where(msk, do * ov, 0.0), axis=1, keepdims=True)
                heads.append((msk, m, inv, delta, jnp.where(msk, qs, 0.0).astype(BF16),
                              jnp.where(msk, do, 0.0).astype(BF16)))
            dq = jnp.zeros((tq, bw), F32)
            dcq = jnp.zeros((tq, LANES), F32)
            for c0 in range(0, length, chunk):
                ch = min(chunk, length - c0)
                k = k_ref[c0:c0 + ch, :]
                v = v_ref[c0:c0 + ch, :]
                dk = jnp.zeros((ch, bw), F32)
                dv = jnp.zeros((ch, bw), F32)
                dc_rows = []
                for e, (msk, m, inv, delta, qm, dom) in enumerate(heads):
                    c_row = _head_c_row(ct_ref[:, c0:c0 + ch], hb * hpb + e)
                    s = _dot_nt(qm, k) - c_row
                    if c0 + ch - 1 > q0:
                        qi = q0 + lax.broadcasted_iota(jnp.int32, (tq, ch), 0)
                        ki = c0 + lax.broadcasted_iota(jnp.int32, (tq, ch), 1)
                        s = jnp.where(ki <= qi, s, -jnp.inf)
                    p = jnp.exp(s - m) * inv
                    dsc = p * (_dot_nt(dom, v) - delta)
                    dsb = dsc.astype(BF16)
                    dq = dq + _dot_nn(dsb, jnp.where(msk, k, jnp.zeros_like(k)))
                    dk = dk + _dot_tn(dsb, qm)
                    dv = dv + _dot_tn(p.astype(BF16), dom)
                    dc_rows.append(-jnp.sum(dsc, axis=0, keepdims=True))
                    dcq = dcq + jnp.where(lane == e, jnp.sum(dsc, axis=1, keepdims=True), 0.0)
                dk_ref[c0:c0 + ch, :] += dk
                dv_ref[c0:c0 + ch, :] += dv
                dc_ref[:, c0:c0 + ch] += _rows8(dc_rows, ch)
            dqg_ref[0] = (dq * scale).astype(BF16)
            dcq_ref[...] = dcq

        for i in range(n_q):
            pl.when(step == i)(lambda i=i: run(i))

    qblk = pl.BlockSpec((tq, bw), lambda h, i: (i, h))
    kblk = pl.BlockSpec((t, bw), lambda h, i: (0, h))
    return pl.pallas_call(
        body, name=name, grid=(nhb, n_q),
        in_specs=[qblk, qblk, pl.BlockSpec((tq, bw), lambda h, i: (i, h + nhb)), qblk,
                  pl.BlockSpec((None, tq, LANES), lambda h, i: (h // ratio, i, 0)),
                  kblk, pl.BlockSpec((t, bw), lambda h, i: (0, h + nhb)),
                  pl.BlockSpec((N_HEADS, t), lambda h, i: (0, 0))],
        out_specs=[pl.BlockSpec((2, tq, bw), lambda h, i: (0, i, h)), kblk, kblk,
                   pl.BlockSpec((None, 8, t), lambda h, i: (h, 0, 0)),
                   pl.BlockSpec((None, tq, LANES), lambda h, i: (h, i, 0))],
        out_shape=[jax.ShapeDtypeStruct((2, t, d), BF16), jax.ShapeDtypeStruct((t, d), F32),
                   jax.ShapeDtypeStruct((t, d), F32), jax.ShapeDtypeStruct((nhb, 8, t), F32),
                   jax.ShapeDtypeStruct((nhb, t, LANES), F32)],
        compiler_params=_params("parallel", "arbitrary"),
    )(dy, qg, qg, o, stats, kv, kv, ct)


def _loss_bwd(h, tgt, *, lo, hi, tm, name):
    t, d = h.shape

    def body(h_ref, t_ref, l_ref, dy_ref):
        i = pl.program_id(0)
        rows = i * tm + lax.broadcasted_iota(jnp.int32, (tm, d), 0)
        err = jnp.where((rows >= lo) & (rows < hi), h_ref[...] - t_ref[...], 0.0)
        dy_ref[...] = err * (1.0 / d)
        part = jnp.sum(jnp.sum(err * err, axis=0, keepdims=True), axis=1, keepdims=True) * (0.5 / d)
        upd = jnp.broadcast_to(part, (8, LANES))

        @pl.when(i == 0)
        def _():
            l_ref[...] = upd

        @pl.when(i > 0)
        def _():
            l_ref[...] += upd

    row = pl.BlockSpec((tm, d), lambda i: (i, 0))
    return pl.pallas_call(
        body, name=name, grid=(t // tm,),
        in_specs=[row, row],
        out_specs=[pl.BlockSpec((8, LANES), lambda i: (0, 0)), row],
        out_shape=[jax.ShapeDtypeStruct((8, LANES), F32), jax.ShapeDtypeStruct((t, d), F32)],
        compiler_params=_params("arbitrary"),
    )(h, tgt)


def _adamw_math(w, gv, m, v):
    bc1 = 1.0 / (1.0 - ADAM_B1 ** ADAM_STEP)
    bc2 = 1.0 / (1.0 - ADAM_B2 ** ADAM_STEP)
    nm = ADAM_B1 * m + (1.0 - ADAM_B1) * gv
    nv = ADAM_B2 * v + (1.0 - ADAM_B2) * (gv * gv)
    delta = (-ADAM_LR) * ((nm * bc1) / (jnp.sqrt(nv * bc2) + ADAM_EPS) + ADAM_WD * w)
    return delta, nm, nv


def _adamw(w, g, m, v, *, name):
    r, c = w.shape
    tr = r
    for cand in (512, 256, 128, 64, 32, 16, 8):
        if r % cand == 0 and r > cand:
            tr = cand
            break

    def body(w_ref, g_ref, m_ref, v_ref, d_ref, nm_ref, nv_ref):
        d_ref[...], nm_ref[...], nv_ref[...] = _adamw_math(w_ref[...], g_ref[...], m_ref[...], v_ref[...])

    blk = pl.BlockSpec((tr, c), lambda i: (i, 0))
    out = jax.ShapeDtypeStruct((r, c), F32)
    return pl.pallas_call(
        body, name=name, grid=(r // tr,),
        in_specs=[blk] * 4, out_specs=[blk] * 3, out_shape=[out] * 3,
        compiler_params=_params("parallel"),
    )(w, g, m, v)


def _sum_adamw(recvs, sends, me, w, m, v, *, name):
    n_l = len(recvs)
    _, r, c = recvs[0].shape
    tr = _tile(r, (256, 192, 176, 128, 96, 64, 48, 32, 16))

    def body(me_ref, *refs):
        p_refs, own_refs = refs[:n_l], refs[n_l:2 * n_l]
        w_ref, m_ref, v_ref, g_ref, d_ref, nm_ref, nv_ref, acc_ref = refs[2 * n_l:]
        layer = pl.program_id(0)
        mine = me_ref[0]
        for k in range(n_l):
            @pl.when(layer == k)
            def _(k=k):
                acc_ref[...] = jnp.zeros((tr, c), F32)
                for dev in range(N_DEV):
                    @pl.when(mine == dev)
                    def _():
                        acc_ref[...] += own_refs[k][...].astype(F32)

                    @pl.when(mine != dev)
                    def _(dev=dev):
                        acc_ref[...] += p_refs[k][dev].astype(F32)
                acc = acc_ref[...]
                g_ref[...] = acc
                d_ref[...], nm_ref[...], nv_ref[...] = _adamw_math(w_ref[...], acc, m_ref[...], v_ref[...])

    p_specs = [pl.BlockSpec((N_DEV, tr, c), lambda l, i, me_ref, k=k: (0, jnp.where(l == k, i, 0), 0))
               for k in range(n_l)]
    own_specs = [pl.BlockSpec((None, tr, c), lambda l, i, me_ref, k=k: (me_ref[0], jnp.where(l == k, i, 0), 0))
                 for k in range(n_l)]
    blk = pl.BlockSpec((None, tr, c), lambda l, i, me_ref: (l, i, 0))
    out = jax.ShapeDtypeStruct((n_l, r, c), F32)
    return pl.pallas_call(
        body, name=name,
        grid_spec=pltpu.PrefetchScalarGridSpec(
            num_scalar_prefetch=1, grid=(n_l, r // tr),
            in_specs=p_specs + own_specs + [blk] * 3, out_specs=[blk] * 4,
            scratch_shapes=[pltpu.VMEM((tr, c), F32)]),
        out_shape=[out] * 4,
        compiler_params=_params("arbitrary", "arbitrary"),
    )(me, *recvs, *sends, w, m, v)


def _sum8(parts, *, name):
    _, r, c = parts.shape
    tr = r
    for cand in (512, 256, 128, 64, 32, 16):
        if r % cand == 0 and r > cand:
            tr = cand
            break

    def body(p_ref, o_ref):
        acc = p_ref[0].astype(F32)
        for k in range(1, N_DEV):
            acc = acc + p_ref[k].astype(F32)
        o_ref[...] = acc

    return pl.pallas_call(
        body, name=name, grid=(r // tr,),
        in_specs=[pl.BlockSpec((N_DEV, tr, c), lambda i: (0, i, 0))],
        out_specs=pl.BlockSpec((tr, c), lambda i: (i, 0)),
        out_shape=jax.ShapeDtypeStruct((r, c), F32),
        compiler_params=_params("parallel"),
    )(parts)


def _my_index():
    return 4 * lax.axis_index("x") + 2 * lax.axis_index("y") + lax.axis_index("c")


def _peer(k):
    x, y, c = lax.axis_index("x"), lax.axis_index("y"), lax.axis_index("c")
    px = x ^ ((k >> 2) & 1)
    py = y ^ ((k >> 1) & 1)
    pc = c ^ (k & 1)
    return (px, py, pc), 4 * px + 2 * py + pc


def _all_gather(shards, *, name):
    n_arr = len(shards)

    def body(*refs):
        ins, outs = refs[:n_arr], refs[n_arr:2 * n_arr]
        send_sems, recv_sems, local_sems = refs[2 * n_arr:]
        me = _my_index()
        local = [pltpu.make_async_copy(ins[n], outs[n].at[me], local_sems.at[n]) for n in range(n_arr)]
        for cp in local:
            cp.start()
        sends = []
        for k in range(1, N_DEV):
            peer, _ = _peer(k)
            for n in range(n_arr):
                cp = pltpu.make_async_remote_copy(
                    src_ref=ins[n], dst_ref=outs[n].at[me], send_sem=send_sems.at[n, k - 1],
                    recv_sem=recv_sems.at[n, k - 1], device_id=peer, device_id_type=pl.DeviceIdType.MESH)
                cp.start()
                sends.append(cp)
        for k in range(1, N_DEV):
            peer, pidx = _peer(k)
            for n in range(n_arr):
                pltpu.make_async_remote_copy(
                    src_ref=ins[n], dst_ref=outs[n].at[pidx], send_sem=send_sems.at[n, k - 1],
                    recv_sem=recv_sems.at[n, k - 1], device_id=peer, device_id_type=pl.DeviceIdType.MESH).wait_recv()
        for cp in sends:
            cp.wait_send()
        for cp in local:
            cp.wait()

    hbm = pl.BlockSpec(memory_space=pl.ANY)
    return pl.pallas_call(
        body, name=name,
        in_specs=[hbm] * n_arr, out_specs=[hbm] * n_arr,
        out_shape=[jax.ShapeDtypeStruct((N_DEV,) + s.shape, s.dtype) for s in shards],
        scratch_shapes=[pltpu.SemaphoreType.DMA((n_arr, N_DEV - 1)), pltpu.SemaphoreType.DMA((n_arr, N_DEV - 1)),
                        pltpu.SemaphoreType.DMA((n_arr,))],
        compiler_params=pltpu.CompilerParams(has_side_effects=True),
    )(*shards)


_HBM = pl.BlockSpec(memory_space=pltpu.HBM)
_SEM = pl.BlockSpec(memory_space=pltpu.SEMAPHORE)
_EFFECT = pltpu.SideEffectType.DATAFLOW_SIDE_EFFECTING


def _remote(src, dst, send_sem, recv_sem, peer):
    return pltpu.make_async_remote_copy(src_ref=src, dst_ref=dst, send_sem=send_sem, recv_sem=recv_sem,
                                        device_id=peer, device_id_type=pl.DeviceIdType.MESH)


def _place_own(src, layer, me, *, out_dtype, name):
    _, r, c = src.shape
    tr = _tile(r, (256, 192, 176, 128, 96, 64, 48, 32, 16))

    def body(me_ref, s_ref, o_ref):
        o_ref[...] = s_ref[...].astype(out_dtype)

    return pl.pallas_call(
        body, name=name,
        grid_spec=pltpu.PrefetchScalarGridSpec(
            num_scalar_prefetch=1, grid=(r // tr,),
            in_specs=[pl.BlockSpec((None, tr, c), lambda i, me_ref: (layer, i, 0))],
            out_specs=pl.BlockSpec((None, tr, c), lambda i, me_ref: (me_ref[0], i, 0))),
        out_shape=jax.ShapeDtypeStruct((N_DEV, r, c), out_dtype),
        compiler_params=_params("parallel"),
    )(me, src)


def _own_blocks(srcs, *, name):
    n = len(srcs)

    def body(*refs):
        ins, outs, sems = refs[:n], refs[n:2 * n], refs[2 * n]
        me = _my_index()
        cps = [pltpu.make_async_copy(ins[t].at[me], outs[t].at[me], sems.at[t]) for t in range(n)]
        for cp in cps:
            cp.start()
        for cp in cps:
            cp.wait()

    return pl.pallas_call(
        body, name=name, in_specs=[_HBM] * n, out_specs=[_HBM] * n,
        out_shape=[jax.ShapeDtypeStruct(s.shape, s.dtype) for s in srcs],
        scratch_shapes=[pltpu.SemaphoreType.DMA((n,))],
    )(*srcs)


def _split_start(groups, *, scatter, name):
    sizes = [len(srcs) for srcs, _ in groups]
    flat_src = [s for srcs, _ in groups for s in srcs]
    flat_land = [l for _, lands in groups for l in lands]
    n, n_g = len(flat_land), len(groups)
    if not scatter:
        flat_src = []
    n_in = len(flat_src) + n

    def body(*refs):
        lands = refs[n_in - n:n_in]
        ins = refs[:n] if scatter else lands
        sems = refs[n_in:n_in + 2 * n_g]
        token = refs[-1]
        me = _my_index()
        t = 0
        for g in range(n_g):
            for q in range(sizes[g]):
                for k in range(1, N_DEV):
                    peer, pidx = _peer(k)
                    src = ins[t].at[pidx] if scatter else ins[t].at[me]
                    slot = q * (N_DEV - 1) + k - 1
                    _remote(src, lands[t].at[me], sems[2 * g].at[slot], sems[2 * g + 1].at[slot], peer).start()
                t += 1
        token[...] = jnp.zeros_like(token)

    sem_shapes = []
    for sz in sizes:
        sem_shapes += [pltpu.SemaphoreType.DMA((sz * (N_DEV - 1),)), pltpu.SemaphoreType.DMA((sz * (N_DEV - 1),))]
    outs = pl.pallas_call(
        body, name=name,
        in_specs=[_HBM] * n_in,
        out_specs=[_SEM] * (2 * n_g) + [_HBM] * n_in + [pl.BlockSpec(memory_space=pltpu.VMEM)],
        out_shape=sem_shapes + [pltpu.HBM(a.shape, a.dtype) for a in flat_src + flat_land]
        + [jax.ShapeDtypeStruct((8, LANES), F32)],
        input_output_aliases={i: 2 * n_g + i for i in range(n_in)},
        compiler_params=pltpu.CompilerParams(has_side_effects=_EFFECT),
    )(*[pltpu.with_memory_space_constraint(a, pltpu.HBM) for a in flat_src + flat_land])
    sems, thru, token = outs[:2 * n_g], outs[2 * n_g:2 * n_g + n_in], outs[-1]
    handles, pos = [], 0
    for g, sz in enumerate(sizes):
        lands_g = thru[n_in - n + pos:n_in - n + pos + sz]
        handles.append((sems[2 * g], sems[2 * g + 1], thru[pos:pos + sz] if scatter else [], lands_g))
        pos += sz
    return handles, token


def _split_wait(handle, after, *, scatter, name):
    send_sems, recv_sems, srcs, lands = handle
    n, n_src = len(lands), len(srcs)

    def body(*refs):
        lnd = refs[n_src:n_src + n]
        ins = refs[:n_src] if scatter else lnd
        ssem, rsem = refs[n_src + n], refs[n_src + n + 1]
        me = _my_index()
        for t in range(n):
            for k in range(1, N_DEV):
                peer, pidx = _peer(k)
                block = ins[t].at[me]
                slot = t * (N_DEV - 1) + k - 1
                _remote(block, lnd[t].at[me], ssem.at[slot], rsem.at[slot], peer).wait_send()
                _remote(block, lnd[t].at[pidx], ssem.at[slot], rsem.at[slot], peer).wait_recv()

    return pl.pallas_call(
        body, name=name,
        in_specs=[_HBM] * (n_src + n) + [_SEM, _SEM, pl.BlockSpec(memory_space=pl.ANY)],
        out_specs=[_HBM] * n,
        out_shape=[pltpu.HBM(l.shape, l.dtype) for l in lands],
        input_output_aliases={n_src + t: t for t in range(n)},
        compiler_params=pltpu.CompilerParams(has_side_effects=_EFFECT),
    )(*srcs, *lands, send_sems, recv_sems, after)


def _pack(arrs, dtype, row_quantum=16):
    flat = jnp.concatenate([a.astype(dtype).reshape(-1) for a in arrs])
    pad = (-flat.shape[0]) % (row_quantum * PACK_COLS)
    if pad:
        flat = jnp.concatenate([flat, jnp.zeros((pad,), dtype)])
    return flat.reshape(-1, PACK_COLS)


def _pack8(arrs, dtype):
    flat = jnp.concatenate([a.astype(dtype).reshape(N_DEV, -1) for a in arrs], axis=1)
    pad = (-flat.shape[1]) % (16 * PACK_COLS)
    if pad:
        flat = jnp.concatenate([flat, jnp.zeros((N_DEV, pad), dtype)], axis=1)
    return flat.reshape(N_DEV, -1, PACK_COLS)


def _unpack(slab, shapes, lead):
    lead_shape = slab.shape[:lead]
    flat = slab.reshape(lead_shape + (-1,))
    outs, off = [], 0
    for shp in shapes:
        size = math.prod(shp)
        outs.append(flat[..., off:off + size].reshape(lead_shape + tuple(shp)))
        off += size
    return outs


def _cols_full(g):
    g = jnp.moveaxis(g, 0, -2)
    return g.reshape(g.shape[:-2] + (g.shape[-2] * g.shape[-1],))


def _cols_split(full):
    n = full.shape[-1] // N_DEV
    return jnp.moveaxis(full.reshape(full.shape[:-1] + (N_DEV, n)), -2, 0)


def _block_diag(w, per):
    n, b, _ = w.shape
    w4 = w.reshape(n // per, per, b, b)
    eye = jnp.eye(per, dtype=w.dtype)
    return jnp.einsum('gpab,pq->gpaqb', w4, eye).reshape(n // per, per * b, per * b)


def _block_diag_extract(g, per):
    gn, cb, _ = g.shape
    b = cb // per
    g5 = g.reshape(gn, per, b, per, b)
    return jnp.stack([g5[:, p, :, p, :] for p in range(per)], axis=1).reshape(gn * per, b, b)


def _slab2d(a):
    return a.reshape(-1, a.shape[-1])


def _lru_block_cols(r_dim):
    lru = r_dim // N_LRU_BLOCKS
    return lru * LANES // math.gcd(lru, LANES)


BIG = ("a_w_in", "a_w_out", "b_w_in", "b_w_out", "f_w_in", "f_w_out")
COL_F32 = ("meta", "a_conv_w", "a_conv_b", "a_b_r", "a_b_i", "a_lambda", "f_conv_w")
REPLICATED = ("a_w_r", "a_w_i", "kv_f_b", "f_conv_b", "ln1_g", "ln1_b", "ln2_g", "ln2_b")
WEIGHT_NAMES = ("meta", "a_w_in", "a_conv_w", "a_conv_b", "a_w_r", "a_b_r", "a_w_i", "a_b_i", "a_lambda", "a_w_out",
                "kv_w", "kv_f_b", "b_w_in", "b_w_out", "f_w_in", "f_conv_w", "f_conv_b", "f_w_out",
                "ln1_g", "ln1_b", "ln2_g", "ln2_b")


def _kv_layout(kv_gathered, d):
    kv_full = _cols_full(kv_gathered)
    kv_pad = 2 * d + LANES - kv_full.shape[1]
    return jnp.concatenate([kv_full, jnp.zeros((d, kv_pad), kv_full.dtype)], axis=1)


def _small_layouts(small):
    r_dim = small["a_lambda"].shape[1]
    n_f = small["f_conv_b"].shape[1] // N_DEV
    cb = _lru_block_cols(r_dim)
    per = cb // (r_dim // N_LRU_BLOCKS)
    n_a = small["a_lambda"].shape[0]
    f_conv_w3 = small["f_conv_w"].reshape(N_LAYERS, 3, N_DEV, n_f).transpose(0, 2, 1, 3)
    f_conv_b3 = small["f_conv_b"].reshape(N_LAYERS, N_DEV, 1, n_f)
    return {
        "kv_fb": jnp.concatenate([small["kv_f_b"], jnp.zeros((LANES - N_HEADS,), F32)])[None],
        "a_cwb": jnp.concatenate([small["a_conv_w"], small["a_conv_b"][:, None],
                                  jnp.zeros((n_a, 3, r_dim), F32)], axis=1),
        "a_vecs": jnp.concatenate([jnp.stack([small["a_b_r"], small["a_b_i"], small["a_lambda"]], axis=1),
                                   jnp.zeros((n_a, 5, r_dim), F32)], axis=1),
        "a_bd_r": jnp.stack([_block_diag(small["a_w_r"][l], per) for l in range(n_a)]).astype(BF16),
        "a_bd_i": jnp.stack([_block_diag(small["a_w_i"][l], per) for l in range(n_a)]).astype(BF16),
        "f_cwb3": jnp.concatenate([f_conv_w3, f_conv_b3, jnp.zeros((N_LAYERS, N_DEV, 4, n_f), F32)], axis=2),
        "ln1_g": small["ln1_g"][:, None], "ln1_b": small["ln1_b"][:, None],
        "ln2_g": small["ln2_g"][:, None], "ln2_b": small["ln2_b"][:, None],
    }


def _local_step(h0, tgt, n_meta, n_tok, wts, hooks):
    tp, d = h0.shape
    tm = tp // 8 if (tp // 8) % 16 == 0 else tp
    tmb = _tile(tp, (1088, 512, 320, 256, 128))
    tq = 128
    tqa_fwd = tp // 4 if tp % 64 == 0 else tq
    tqa_bwd = tp // 4 if tp % 64 == 0 else tq
    r_dim = wts["a_vecs"].shape[2]
    cb = wts["a_bd_r"].shape[-1]
    sb = LANES
    n_b = N_LAYERS - N_A_LAYERS

    h, h_bf = h0, h0.astype(BF16)
    saved = []
    kvs = None
    for layer in range(N_LAYERS):
        lw = {}
        sv = {"h_bf": h_bf, "w": lw}
        if layer < N_A_LAYERS:
            lw["in"] = hooks.weight(layer, "in", h)
            sv["gr"] = _proj_in(h_bf, lw["in"], shard_major=False, name="a_in_proj")
            sv["rec"] = _conv_a_fwd(sv["gr"], wts["a_cwb"][layer], cb=cb, name="a_conv_fwd")
            a, u, sv["r"], sv["i"] = _gates_fwd(sv["rec"], wts["a_bd_r"][layer], wts["a_bd_i"][layer],
                                                wts["a_vecs"][layer], tm=tm, name="a_gates_fwd")
            sv["a"] = a
            sv["hr"], y3 = _scan_fwd(a, u, sv["gr"], cb=sb, name="a_scan_fwd")
        else:
            j = layer - N_A_LAYERS
            if j == 0:
                kv_w = _kv_layout(hooks.weight(layer, "kv_w", h), d)
                kvs = {"h_bf": h_bf, "w": kv_w}
                kvs["kv"] = _mm_nn(h_bf, kv_w[:, :2 * d], tn=_tile(2 * d, (512, 256, 128)), out_dtype=BF16,
                                   name="kv_proj")
                kvs["fp"] = _mm_nn(h_bf, kv_w[:, 2 * d:], tn=LANES, out_dtype=F32, name="f_proj")
                kvs["c"], ct = _fgate_fwd(kvs["fp"], wts["kv_fb"], tq=tq, name="fgate_fwd")
                kvs["ct"] = ct[:N_HEADS]
            lw["in"] = hooks.weight(layer, "in", kvs["c"] if j == 0 else h)
            sv["qg"] = _proj_in(h_bf, lw["in"], shard_major=False, name="b_in_proj")
            sv["o"], y3, sv["st"] = _attn_fwd(sv["qg"], kvs["kv"], kvs["ct"], tq=tqa_fwd, name="attn_fwd")
        sv["y3"] = y3
        lw["out"] = hooks.weight(layer, "out", y3)
        sv["s1"], h, h_bf = _out_ln(y3, lw["out"], h, wts["ln1_g"][layer], wts["ln1_b"][layer], n_valid=n_tok,
                                    tm=tmb // 2, name="mix_out_ln")
        sv["h1_bf"] = h_bf
        lw["f_in"] = hooks.weight(layer, "f_in", h)
        sv["z3"] = _proj_in(h_bf, lw["f_in"], shard_major=True, transposed=True, name="f_in_proj")
        sv["yf3"] = _convglu_fwd(sv["z3"], wts["f_cwb3"][layer], name="f_convglu_fwd")
        lw["f_out"] = hooks.weight(layer, "f_out", sv["yf3"])
        sv["s2"], h, h_bf = _out_ln(sv["yf3"], lw["f_out"], h, wts["ln2_g"][layer], wts["ln2_b"][layer],
                                    n_valid=n_tok, tm=tmb // 2, name="ffn_out_ln")
        saved.append(sv)

    loss_tile, dh = _loss_bwd(h, tgt, lo=n_meta, hi=n_tok, tm=tm, name="loss")

    grads = {k: [None] * N_LAYERS for k in ("f_cwb3", "ln1_gb", "ln2_gb")}
    grads.update({k: [None] * N_A_LAYERS for k in ("a_cwb", "a_bd_r", "a_bd_i", "a_vecs")})
    dkv = []
    token = jnp.zeros((), F32)
    for layer in reversed(range(N_LAYERS)):
        sv = saved[layer]
        lw = sv["w"]
        big = {}
        ds, ds_bf, grads["ln2_gb"][layer] = _ln_bwd(dh, sv["s2"], wts["ln2_g"][layer] + token, tm=tm, name="ln_bwd")
        dz, dcw = _ffn_bwd_mid(ds_bf, lw["f_out"], sv["z3"], wts["f_cwb3"][layer], name="f_bwd_mid")
        grads["f_cwb3"][layer] = dcw.reshape((N_DEV,) + dcw.shape[2:])
        dz3 = dz
        big["f_out"] = _w_out_grad(sv["yf3"], ds_bf, lw["f_out"].shape[1], name="f_w_out_grad")
        dh = _in_bwd(dz3, lw["f_in"], ds, tm=tmb, transposed=True, name="f_in_bwd")
        big["f_in"] = _w_in_grad(sv["h1_bf"], dz3, transposed=True, name="f_w_in_grad")
        token = hooks.grads_ready(layer, "ffn", big)
        big = {}
        ds, ds_bf, grads["ln1_gb"][layer] = _ln_bwd(dh, sv["s1"], wts["ln1_g"][layer] + token, tm=tm, name="ln_bwd")
        if layer < N_A_LAYERS:
            dy = _out_bwd(ds_bf, lw["out"], tm=tmb // 2, name="a_out_bwd")
            big["out"] = _w_out_grad(sv["y3"], ds_bf, lw["out"].shape[1], name="a_w_out_grad")
            d_h, d_a, dgate = _scan_bwd(dy, sv["gr"], sv["hr"], sv["a"], cb=sb, name="a_scan_bwd")
            d_rec, dpr, dpi, grads["a_vecs"][layer] = _gates_bwd(
                sv["rec"], sv["r"], sv["i"], sv["a"], d_h, d_a, wts["a_bd_r"][layer], wts["a_bd_i"][layer],
                wts["a_vecs"][layer], tm=tm, name="a_gates_bwd")
            grads["a_bd_r"][layer], grads["a_bd_i"][layer] = _bd_grad(sv["rec"], dpr, dpi, cb=cb, name="a_bd_grad")
            dact, grads["a_cwb"][layer] = _conv_a_bwd(d_rec, sv["gr"], dgate, wts["a_cwb"][layer], cb=cb,
                                                      name="a_conv_bwd")
            dh = _in_bwd(dact, lw["in"], ds, tm=tmb, name="a_in_bwd")
            big["in"] = _w_in_grad(sv["h_bf"], dact, name="a_w_in_grad")
        else:
            j = layer - N_A_LAYERS
            dy = _out_bwd(ds_bf, lw["out"], tm=tmb // 2, name="b_out_bwd")
            big["out"] = _w_out_grad(sv["y3"], ds_bf, lw["out"].shape[1], name="b_w_out_grad")
            dqg, dk, dv, dc, dcq = _attn_bwd(dy, sv["qg"], sv["o"], sv["st"], kvs["kv"], kvs["ct"], tq=tqa_bwd,
                                             name="attn_bwd")
            dkv.append((dk, dv, dc, dcq))
            dh = _in_bwd(dqg, lw["in"], ds, tm=tmb, name="b_in_bwd")
            big["in"] = _w_in_grad(sv["h_bf"], dqg, name="b_w_in_grad")
            if j == 0:
                hpb = _head_block_width(d // N_HEADS, BWD_HEAD_TILES) // (d // N_HEADS)
                dct = (dkv[0][2] + dkv[1][2])[:, :hpb, :].reshape(N_HEADS, tp)
                dcq = (dkv[0][3] + dkv[1][3])[:, :, :hpb]
                dct = dct + jnp.transpose(dcq, (0, 2, 1)).reshape(N_HEADS, tp)
                dct = jnp.concatenate([dct, jnp.zeros((LANES - N_HEADS, tp), F32)])
                df_bf, grads["kv_fb"] = _fgate_bwd(dct, kvs["fp"], wts["kv_fb"], tq=tq, name="fgate_bwd")
                dkvz = jnp.concatenate([_pair_sum(dkv[0][0], dkv[1][0], tm=tm, name="kv_pair_sum"),
                                        _pair_sum(dkv[0][1], dkv[1][1], tm=tm, name="kv_pair_sum"), df_bf], axis=1)
                dh = _mm_nt_full(dkvz, kvs["w"], dh, tm=tmb // 2, name="kv_in_bwd")
                big["kv_w"] = _mm_tn_cols(kvs["h_bf"], dkvz, tn=LANES, name="kv_w_grad")
        token = hooks.grads_ready(layer, "mix", big)
    return loss_tile, dh, grads


def _finish_small_grads(grads, d_h0, n_meta):
    r_dim = grads["a_vecs"][0].shape[1]
    per = _lru_block_cols(r_dim) // (r_dim // N_LRU_BLOCKS)
    a_cwb = jnp.stack(grads["a_cwb"])
    a_vecs = jnp.stack(grads["a_vecs"])
    f_cwb3 = jnp.stack(grads["f_cwb3"])
    ln1 = jnp.stack(grads["ln1_gb"])
    ln2 = jnp.stack(grads["ln2_gb"])
    f_rows = f_cwb3.transpose(0, 2, 1, 3).reshape(N_LAYERS, 8, -1)
    return {
        "meta": d_h0[:n_meta],
        "a_conv_w": a_cwb[:, :4], "a_conv_b": a_cwb[:, 4],
        "a_w_r": jnp.stack([_block_diag_extract(g, per) for g in grads["a_bd_r"]]),
        "a_b_r": a_vecs[:, 0],
        "a_w_i": jnp.stack([_block_diag_extract(g, per) for g in grads["a_bd_i"]]),
        "a_b_i": a_vecs[:, 1], "a_lambda": a_vecs[:, 2],
        "kv_f_b": grads["kv_fb"][0, :N_HEADS],
        "f_conv_w": f_rows[:, :3], "f_conv_b": f_rows[:, 3],
        "ln1_g": ln1[:, 0], "ln1_b": ln1[:, 1], "ln2_g": ln2[:, 0], "ln2_b": ln2[:, 1],
    }


def kernel(x, meta, a_w_in, a_conv_w, a_conv_b, a_w_r, a_b_r, a_w_i, a_b_i, a_lambda, a_w_out, kv_w, kv_f_b, b_w_in, b_w_out, f_w_in, f_conv_w, f_conv_b, f_w_out, ln1_g, ln1_b, ln2_g, ln2_b, loss_target, m_meta, m_a_w_in, m_a_conv_w, m_a_conv_b, m_a_w_r, m_a_b_r, m_a_w_i, m_a_b_i, m_a_lambda, m_a_w_out, m_kv_w, m_kv_f_b, m_b_w_in, m_b_w_out, m_f_w_in, m_f_conv_w, m_f_conv_b, m_f_w_out, m_ln1_g, m_ln1_b, m_ln2_g, m_ln2_b, v_meta, v_a_w_in, v_a_conv_w, v_a_conv_b, v_a_w_r, v_a_b_r, v_a_w_i, v_a_b_i, v_a_lambda, v_a_w_out, v_kv_w, v_kv_f_b, v_b_w_in, v_b_w_out, v_f_w_in, v_f_conv_w, v_f_conv_b, v_f_w_out, v_ln1_g, v_ln1_b, v_ln2_g, v_ln2_b):
    w = dict(meta=meta, a_w_in=a_w_in, a_conv_w=a_conv_w, a_conv_b=a_conv_b, a_w_r=a_w_r, a_b_r=a_b_r, a_w_i=a_w_i,
             a_b_i=a_b_i, a_lambda=a_lambda, a_w_out=a_w_out, kv_w=kv_w, kv_f_b=kv_f_b, b_w_in=b_w_in,
             b_w_out=b_w_out, f_w_in=f_w_in, f_conv_w=f_conv_w, f_conv_b=f_conv_b, f_w_out=f_w_out, ln1_g=ln1_g,
             ln1_b=ln1_b, ln2_g=ln2_g, ln2_b=ln2_b)
    m = dict(meta=m_meta, a_w_in=m_a_w_in, a_conv_w=m_a_conv_w, a_conv_b=m_a_conv_b, a_w_r=m_a_w_r, a_b_r=m_a_b_r,
             a_w_i=m_a_w_i, a_b_i=m_a_b_i, a_lambda=m_a_lambda, a_w_out=m_a_w_out, kv_w=m_kv_w, kv_f_b=m_kv_f_b,
             b_w_in=m_b_w_in, b_w_out=m_b_w_out, f_w_in=m_f_w_in, f_conv_w=m_f_conv_w, f_conv_b=m_f_conv_b,
             f_w_out=m_f_w_out, ln1_g=m_ln1_g, ln1_b=m_ln1_b, ln2_g=m_ln2_g, ln2_b=m_ln2_b)
    v = dict(meta=v_meta, a_w_in=v_a_w_in, a_conv_w=v_a_conv_w, a_conv_b=v_a_conv_b, a_w_r=v_a_w_r, a_b_r=v_a_b_r,
             a_w_i=v_a_w_i, a_b_i=v_a_b_i, a_lambda=v_a_lambda, a_w_out=v_a_w_out, kv_w=v_kv_w, kv_f_b=v_kv_f_b,
             b_w_in=v_b_w_in, b_w_out=v_b_w_out, f_w_in=v_f_w_in, f_conv_w=v_f_conv_w, f_conv_b=v_f_conv_b,
             f_w_out=v_f_w_out, ln1_g=v_ln1_g, ln1_b=v_ln1_b, ln2_g=v_ln2_g, ln2_b=v_ln2_b)
    shapes = {n: w[n].shape for n in WEIGHT_NAMES}

    me = jnp.reshape(_my_index(), (1,)).astype(jnp.int32)

    def as_stored(name, a):
        return jnp.swapaxes(a, 1, 2) if name == "f_w_in" else a

    param_of = {"in": ("a_w_in", "b_w_in"), "out": ("a_w_out", "b_w_out"), "f_in": ("f_w_in",) * 2,
                "f_out": ("f_w_out",) * 2}
    order = [("small", None, None)]
    for layer in range(N_LAYERS):
        if layer == N_A_LAYERS:
            order.append(("kv_w", layer, 0))
        for key in ("in", "out", "f_in", "f_out"):
            order.append((key, layer, layer if key[0] == "f" or layer < N_A_LAYERS else layer - N_A_LAYERS))
    def place(key, layer, idx):
        if key == "small":
            return _place_own(_pack([w[n] for n in COL_F32], F32)[None], 0, me, out_dtype=F32, name="place_small")
        if key == "kv_w":
            return _place_own(w["kv_w"][None], 0, me, out_dtype=BF16, name="place_kv_w")
        name = param_of[key][0 if layer < N_A_LAYERS else 1]
        return _place_own(as_stored(name, w[name]), idx, me, out_dtype=BF16, name=f"place_{name}_{idx}")

    lands = [place(*o) for o in order]
    gather_handles, gather_token = _split_start([([l], [l]) for l in lands], scatter=False, name="gather_start")
    group_of = {(key, layer): g for g, (key, layer, _) in enumerate(order)}
    (got_s,) = _split_wait(gather_handles[0], gather_token, scatter=False, name="gather_wait_small")
    small = {n: w[n] for n in REPLICATED}
    for n, part in zip(COL_F32, _unpack(got_s, [w[n].shape for n in COL_F32], 1)):
        small[n] = _cols_full(part)
    n_meta, d = small["meta"].shape

    class Hooks:
        pending = None
        received = {}
        sent = {}

        @staticmethod
        def weight(layer, key, after):
            (got,) = _split_wait(gather_handles[group_of[(key, layer)]], after, scatter=False,
                                 name=f"gather_wait_{key}_{layer}")
            return got

        @staticmethod
        def collect(after):
            if Hooks.pending is not None:
                tag, names, handle = Hooks.pending
                got = _split_wait(handle, after, scatter=True, name=f"scatter_wait_{tag}")
                Hooks.received.update(zip(names, got))
                Hooks.pending = None

        @staticmethod
        def grads_ready(layer, part, big):
            if "kv_w" in big:
                big["kv_w"] = _cols_split(big["kv_w"][:, :shapes["kv_w"][1] * N_DEV]).astype(BF16)
            names = [(key, layer) for key in big]
            send = [big[key] for key in big]
            Hooks.collect(send[0])
            empty = [lax.empty(s.shape, s.dtype) for s in send]
            handles, token = _split_start([(send, empty)], scatter=True, name=f"scatter_start_{part}_{layer}")
            Hooks.pending = (f"{part}_{layer}", names, handles[0])
            Hooks.sent.update(zip(names, handles[0][2]))
            return token[0, 0]

    Hooks.pending, Hooks.received, Hooks.sent = None, {}, {}

    n_tok = n_meta + x.shape[1]
    tp = -(-n_tok // ROW_ALIGN) * ROW_ALIGN
    pad = jnp.zeros((tp - n_tok, d), F32)
    h0 = jnp.concatenate([small["meta"], x[0], pad])
    tgt = jnp.concatenate([jnp.zeros((n_meta, d), F32), loss_target[0], pad])
    loss_tile, d_h0, grads = _local_step(h0, tgt, n_meta, n_tok, _small_layouts(small), Hooks)
    g_small = _finish_small_grads(grads, d_h0, n_meta)
    loss = lax.psum(loss_tile[0, 0], MESH_AXES)
    grad_x = d_h0[n_meta:n_tok][None]

    rep = _pack([g_small[n] for n in REPLICATED], F32, row_quantum=16 * N_DEV)
    send = [_pack8([_cols_split(g_small[n]) for n in COL_F32], F32), rep.reshape(N_DEV, -1, PACK_COLS)]
    lands = _own_blocks(send, name="scatter_own_small")
    handles, token = _split_start([(send, lands)], scatter=True, name="scatter_start_small")

    g, delta, new_m, new_v = {}, {}, {}, {}
    layers_of = {
        "a_w_in": [("in", l) for l in range(N_A_LAYERS)], "a_w_out": [("out", l) for l in range(N_A_LAYERS)],
        "b_w_in": [("in", l) for l in range(N_A_LAYERS, N_LAYERS)],
        "b_w_out": [("out", l) for l in range(N_A_LAYERS, N_LAYERS)],
        "f_w_in": [("f_in", l) for l in range(N_LAYERS)], "f_w_out": [("f_out", l) for l in range(N_LAYERS)],
        "kv_w": [("kv_w", N_A_LAYERS)],
    }
    ready = [n for n in BIG + ("kv_w",) if all(t in Hooks.received for t in layers_of[n])]

    def done(names):
        return jnp.stack([g[n][(0,) * g[n].ndim] for n in names])

    for n in ready + [n for n in BIG + ("kv_w",) if n not in ready]:
        if n not in ready and Hooks.pending is not None:
            Hooks.collect(done(ready))
        lift = (lambda a: a[None]) if n == "kv_w" else (lambda a, n=n: as_stored(n, a))
        outs = _sum_adamw([Hooks.received[t] for t in layers_of[n]], [Hooks.sent[t] for t in layers_of[n]], me,
                          lift(w[n]), lift(m[n]), lift(v[n]), name="sum_adamw_" + n)
        g[n], delta[n], new_m[n], new_v[n] = [as_stored(n, o).reshape(shapes[n]) for o in outs]
    recv_s, recv_r = _split_wait(handles[0], done(BIG + ("kv_w",)), scatter=True, name="scatter_wait_small")
    sum_s = _sum8(recv_s, name="sum_grads_f32")
    g.update(zip(COL_F32, _unpack(sum_s, [shapes[n] for n in COL_F32], 0)))
    (got_r,) = _all_gather([_sum8(recv_r, name="sum_grads_replicated")], name="gather_replicated_sums")
    g.update(zip(REPLICATED, _unpack(got_r.reshape(-1, PACK_COLS), [shapes[n] for n in REPLICATED], 0)))

    for n in COL_F32 + REPLICATED:
        shp = shapes[n]
        dl, nm, nv = _adamw(_slab2d(w[n]), _slab2d(g[n]), _slab2d(m[n]), _slab2d(v[n]), name="adamw")
        delta[n], new_m[n], new_v[n] = dl.reshape(shp), nm.reshape(shp), nv.reshape(shp)
    return (loss, grad_x, *[g[n] for n in WEIGHT_NAMES], *[delta[n] for n in WEIGHT_NAMES],
            *[new_m[n] for n in WEIGHT_NAMES], *[new_v[n] for n in WEIGHT_NAMES])
```

```python
import math

import jax
import jax.numpy as jnp
from jax import lax
from jax.experimental import pallas as pl
from jax.experimental.pallas import tpu as pltpu

F32 = jnp.float32
BF16 = jnp.bfloat16

N_DEV = 8
MESH_AXES = ("x", "y", "c")
N_LAYERS = 4
N_A_LAYERS = 2
N_LRU_BLOCKS = 16
N_HEADS = 16
LRU_C = 8.0
DN_ALPHA = (2 * N_LAYERS) ** 0.25
LN_EPS = 1e-5
ADAM_LR, ADAM_B1, ADAM_B2, ADAM_EPS, ADAM_WD, ADAM_STEP = 0.001, 0.9, 0.999, 1e-08, 0.01, 10

LANES = 128
SUBLANES = 8
ROW_ALIGN = 128
VMEM_LIMIT_BYTES = 56 * 1024 * 1024
GELU_K = math.sqrt(2.0 / math.pi)
GELU_C = 0.044715
PACK_COLS = 1024


def _params(*sem):
    return pltpu.CompilerParams(dimension_semantics=sem, vmem_limit_bytes=VMEM_LIMIT_BYTES)


def _gelu(x):
    th = jnp.tanh(GELU_K * (x + GELU_C * x * x * x))
    return 0.5 * x * (1.0 + th)


def _gelu_and_grad(x):
    x2 = x * x
    th = jnp.tanh(GELU_K * (x + GELU_C * x2 * x))
    g = 0.5 * x * (1.0 + th)
    dg = 0.5 * (1.0 + th) + 0.5 * x * (1.0 - th * th) * (GELU_K * (1.0 + 3.0 * GELU_C * x2))
    return g, dg


def _sigmoid(x):
    return 0.5 * jnp.tanh(0.5 * x) + 0.5


def _expm1(x):
    small = x * (1.0 + 0.5 * x * (1.0 + (1.0 / 3.0) * x * (1.0 + 0.25 * x)))
    return jnp.where(jnp.abs(x) < 1e-2, small, jnp.exp(x) - 1.0)


def _softplus(x):
    e = jnp.exp(-jnp.abs(x))
    small = e * (1.0 - 0.5 * e * (1.0 - (2.0 / 3.0) * e))
    return jnp.maximum(x, 0.0) + jnp.where(e < 1e-2, small, jnp.log(1.0 + e))


def _shift_down(x, s):
    if s == 0:
        return x
    rows = lax.broadcasted_iota(jnp.int32, x.shape, 0)
    return jnp.where(rows >= s, pltpu.roll(x, s, 0), 0.0)


def _shift_up(x, s):
    if s == 0:
        return x
    n = x.shape[0]
    rows = lax.broadcasted_iota(jnp.int32, x.shape, 0)
    return jnp.where(rows < n - s, pltpu.roll(x, n - s, 0), 0.0)


def _dot_nn(a, b):
    return lax.dot_general(a, b, (((1,), (0,)), ((), ())), preferred_element_type=F32)


def _dot_nt(a, b):
    return lax.dot_general(a, b, (((1,), (1,)), ((), ())), preferred_element_type=F32)


def _dot_tn(a, b):
    return lax.dot_general(a, b, (((0,), (0,)), ((), ())), preferred_element_type=F32)


def _rows8(vals, width):
    rows = lax.broadcasted_iota(jnp.int32, (8, width), 0)
    out = jnp.zeros((8, width), F32)
    for k, v in enumerate(vals):
        out = jnp.where(rows == k, jnp.broadcast_to(v, (8, width)), out)
    return out


def _tile(n, prefer):
    for c in prefer:
        if n % c == 0:
            return c
    return n


def _mm_nn(a, b, *, tn, out_dtype, name):
    m, k = a.shape
    n = b.shape[1]

    def body(a_ref, b_ref, o_ref):
        o_ref[...] = _dot_nn(a_ref[...], b_ref[...]).astype(o_ref.dtype)

    return pl.pallas_call(
        body, name=name, grid=(n // tn,),
        in_specs=[pl.BlockSpec((m, k), lambda j: (0, 0)), pl.BlockSpec((k, tn), lambda j: (0, j))],
        out_specs=pl.BlockSpec((m, tn), lambda j: (0, j)),
        out_shape=jax.ShapeDtypeStruct((m, n), out_dtype),
        compiler_params=_params("parallel"),
    )(a, b)


def _proj_in(h_bf, g_in, *, shard_major, name, transposed=False):
    t, k = h_bf.shape
    n = g_in.shape[1] if transposed else g_in.shape[2]

    def body(a_ref, b_ref, o_ref):
        o_ref[...] = _dot_nt(a_ref[...], b_ref[...]) if transposed else _dot_nn(a_ref[...], b_ref[...])

    if shard_major:
        out_spec = pl.BlockSpec((None, t, n), lambda j: (j, 0, 0))
        out_shape = jax.ShapeDtypeStruct((N_DEV, t, n), F32)
    else:
        out_spec = pl.BlockSpec((t, n), lambda j: (0, j))
        out_shape = jax.ShapeDtypeStruct((t, N_DEV * n), F32)
    return pl.pallas_call(
        body, name=name, grid=(N_DEV,),
        in_specs=[pl.BlockSpec((t, k), lambda j: (0, 0)),
                  pl.BlockSpec((None,) + g_in.shape[1:], lambda j: (j, 0, 0))],
        out_specs=out_spec, out_shape=out_shape,
        compiler_params=_params("parallel"),
    )(h_bf, g_in)


def _out_ln(y3, g_out, hin, g, b, *, n_valid, tm, name):
    nj, t, kj = y3.shape
    _, r, d = g_out.shape

    def body(y_ref, w_ref, hin_ref, g_ref, b_ref, s_ref, h_ref, hb_ref):
        w = w_ref[...].reshape(N_DEV * r, d)
        s = DN_ALPHA * hin_ref[...]
        for jj in range(nj):
            s = s + _dot_nn(y_ref[jj], w[jj * kj:(jj + 1) * kj])
        mu = jnp.mean(s, axis=-1, keepdims=True)
        xc = s - mu
        var = jnp.mean(xc * xc, axis=-1, keepdims=True)
        h = xc * lax.rsqrt(var + LN_EPS) * g_ref[...] + b_ref[...]
        s_ref[...] = s
        h_ref[...] = h
        rows = pl.program_id(0) * tm + lax.broadcasted_iota(jnp.int32, (tm, d), 0)
        hb_ref[...] = jnp.where(rows < n_valid, h, 0.0).astype(BF16)

    row = pl.BlockSpec((tm, d), lambda i: (i, 0))
    vec = pl.BlockSpec((1, d), lambda i: (0, 0))
    return pl.pallas_call(
        body, name=name, grid=(t // tm,),
        in_specs=[pl.BlockSpec((nj, tm, kj), lambda i: (0, i, 0)),
                  pl.BlockSpec((N_DEV, r, d), lambda i: (0, 0, 0)), row, vec, vec],
        out_specs=[row, row, row],
        out_shape=[jax.ShapeDtypeStruct((t, d), F32), jax.ShapeDtypeStruct((t, d), F32),
                   jax.ShapeDtypeStruct((t, d), BF16)],
        compiler_params=_params("parallel"),
    )(y3, g_out, hin, g, b)


def _out_bwd(ds_bf, g_out, *, tm, name):
    t, d = ds_bf.shape
    r = g_out.shape[1]

    def body(a_ref, w_ref, o_ref):
        o_ref[...] = _dot_nt(a_ref[...], w_ref[...].reshape(N_DEV * r, d))

    return pl.pallas_call(
        body, name=name, grid=(t // tm,),
        in_specs=[pl.BlockSpec((tm, d), lambda i: (i, 0)),
                  pl.BlockSpec((N_DEV, r, d), lambda i: (0, 0, 0))],
        out_specs=pl.BlockSpec((tm, N_DEV * r), lambda i: (i, 0)),
        out_shape=jax.ShapeDtypeStruct((t, N_DEV * r), F32),
        compiler_params=_params("parallel"),
    )(ds_bf, g_out)


def _in_bwd(dact, g_in, add, *, tm, name, alpha=DN_ALPHA, transposed=False):
    t = dact.shape[-2]
    _, k, n = g_in.shape
    if transposed:
        k, n = n, k
    halves = dact.shape[0] == 2 and dact.ndim == 3
    per = N_DEV // 2

    def body(a_ref, b_ref, add_ref, o_ref, acc_ref):
        j = pl.program_id(1)

        @pl.when(j == 0)
        def _():
            acc_ref[...] = alpha * add_ref[...]

        acc_ref[...] += _dot_nn(a_ref[...], b_ref[...]) if transposed else _dot_nt(a_ref[...], b_ref[...])

        @pl.when(j == N_DEV - 1)
        def _():
            o_ref[...] = acc_ref[...]

    if halves:
        a_spec = pl.BlockSpec((None, tm, n), lambda i, j: (j // per, i, j % per))
    elif dact.ndim == 4:
        a_spec = pl.BlockSpec((None, None, tm, n), lambda i, j: (j // per, j % per, i, 0))
    else:
        a_spec = pl.BlockSpec((None, tm, n), lambda i, j: (j, i, 0))
    return pl.pallas_call(
        body, name=name, grid=(t // tm, N_DEV),
        in_specs=[a_spec, pl.BlockSpec((None,) + g_in.shape[1:], lambda i, j: (j, 0, 0)),
                  pl.BlockSpec((tm, k), lambda i, j: (i, 0))],
        out_specs=pl.BlockSpec((tm, k), lambda i, j: (i, 0)),
        out_shape=jax.ShapeDtypeStruct((t, k), F32),
        scratch_shapes=[pltpu.VMEM((tm, k), F32)],
        compiler_params=_params("parallel", "arbitrary"),
    )(dact, g_in, add)


def _mm_nt_full(a, b, add, *, tm, name):
    t, n = a.shape
    k = b.shape[0]

    def body(a_ref, b_ref, add_ref, o_ref):
        o_ref[...] = add_ref[...] + _dot_nt(a_ref[...], b_ref[...])

    return pl.pallas_call(
        body, name=name, grid=(t // tm,),
        in_specs=[pl.BlockSpec((tm, n), lambda i: (i, 0)), pl.BlockSpec((k, n), lambda i: (0, 0)),
                  pl.BlockSpec((tm, k), lambda i: (i, 0))],
        out_specs=pl.BlockSpec((tm, k), lambda i: (i, 0)),
        out_shape=jax.ShapeDtypeStruct((t, k), F32),
        compiler_params=_params("parallel"),
    )(a, b, add)


def _w_in_grad(h_bf, dact, *, name, transposed=False):
    t, k = h_bf.shape
    halves = dact.shape[0] == 2 and dact.ndim == 3
    per = N_DEV // 2
    n = dact.shape[-1] // per if halves else dact.shape[-1]

    def body(a_ref, b_ref, o_ref):
        if transposed:
            o_ref[...] = _dot_tn(b_ref[...], a_ref[...]).astype(BF16)
        else:
            o_ref[...] = _dot_tn(a_ref[...], b_ref[...]).astype(BF16)

    if halves:
        b_spec = pl.BlockSpec((None, t, n), lambda j: (j // per, 0, j % per))
    elif dact.ndim == 4:
        b_spec = pl.BlockSpec((None, None, t, n), lambda j: (j // per, j % per, 0, 0))
    else:
        b_spec = pl.BlockSpec((None, t, n), lambda j: (j, 0, 0))
    return pl.pallas_call(
        body, name=name, grid=(N_DEV,),
        in_specs=[pl.BlockSpec((t, k), lambda j: (0, 0)), b_spec],
        out_specs=pl.BlockSpec((None, n, k) if transposed else (None, k, n), lambda j: (j, 0, 0)),
        out_shape=jax.ShapeDtypeStruct((N_DEV, n, k) if transposed else (N_DEV, k, n), BF16),
        compiler_params=_params("parallel"),
    )(h_bf, dact)


def _w_out_grad(y3, ds_bf, r, *, name):
    nj, t, kj = y3.shape
    d = ds_bf.shape[1]
    unit = r * LANES // math.gcd(r, LANES)
    ks = max([c for c in range(unit, min(kj, 768) + 1, unit) if kj % c == 0], default=kj)
    gsz = ks // r
    per = kj // ks

    def body(a_ref, b_ref, o_ref):
        o_ref[...] = _dot_tn(a_ref[...], b_ref[...]).reshape(gsz, r, d).astype(BF16)

    return pl.pallas_call(
        body, name=name, grid=(nj * per,),
        in_specs=[pl.BlockSpec((None, t, ks), lambda j: (j // per, 0, j % per)),
                  pl.BlockSpec((t, d), lambda j: (0, 0))],
        out_specs=pl.BlockSpec((gsz, r, d), lambda j: (j, 0, 0)),
        out_shape=jax.ShapeDtypeStruct((N_DEV, r, d), BF16),
        compiler_params=_params("parallel"),
    )(y3, ds_bf)


def _mm_tn_cols(a, b, *, tn, name):
    t, m = a.shape
    n = b.shape[1]

    def body(a_ref, b_ref, o_ref):
        o_ref[...] = _dot_tn(a_ref[...], b_ref[...])

    return pl.pallas_call(
        body, name=name, grid=(n // tn,),
        in_specs=[pl.BlockSpec((t, m), lambda j: (0, 0)), pl.BlockSpec((t, tn), lambda j: (0, j))],
        out_specs=pl.BlockSpec((m, tn), lambda j: (0, j)),
        out_shape=jax.ShapeDtypeStruct((m, n), F32),
        compiler_params=_params("parallel"),
    )(a, b)


def _ln_bwd(dout, s, g, *, tm, name):
    t, d = s.shape

    def body(do_ref, s_ref, g_ref, ds_ref, dsb_ref, gb_ref):
        i = pl.program_id(0)
        sv = s_ref[...]
        do = do_ref[...]
        mu = jnp.mean(sv, axis=-1, keepdims=True)
        xc = sv - mu
        var = jnp.mean(xc * xc, axis=-1, keepdims=True)
        rstd = lax.rsqrt(var + LN_EPS)
        xhat = xc * rstd
        dxhat = do * g_ref[...]
        m1 = jnp.mean(dxhat, axis=-1, keepdims=True)
        m2 = jnp.mean(dxhat * xhat, axis=-1, keepdims=True)
        ds = rstd * (dxhat - m1 - xhat * m2)
        ds_ref[...] = ds
        dsb_ref[...] = ds.astype(BF16)
        upd = _rows8([jnp.sum(do * xhat, axis=0, keepdims=True), jnp.sum(do, axis=0, keepdims=True)], d)

        @pl.when(i == 0)
        def _():
            gb_ref[...] = upd

        @pl.when(i > 0)
        def _():
            gb_ref[...] += upd

    row = pl.BlockSpec((tm, d), lambda i: (i, 0))
    return pl.pallas_call(
        body, name=name, grid=(t // tm,),
        in_specs=[row, row, pl.BlockSpec((1, d), lambda i: (0, 0))],
        out_specs=[row, row, pl.BlockSpec((8, d), lambda i: (0, 0))],
        out_shape=[jax.ShapeDtypeStruct((t, d), F32), jax.ShapeDtypeStruct((t, d), BF16),
                   jax.ShapeDtypeStruct((8, d), F32)],
        compiler_params=_params("arbitrary"),
    )(dout, s, g)


def _roll_down(x, s):
    return x if s == 0 else pltpu.roll(x, s, 0)


def _conv_taps(x, wb, width):
    y = jnp.broadcast_to(wb[width:width + 1, :], x.shape)
    for k in range(width):
        y = y + _roll_down(x, width - 1 - k) * wb[k:k + 1, :]
    return y


def _conv_taps_bwd(dy, x, wb, width):
    n = dy.shape[0]
    dx = jnp.zeros_like(dy)
    rows = []
    for k in range(width):
        s = width - 1 - k
        dy_up = dy if s == 0 else pltpu.roll(dy, n - s, 0)
        dx = dx + dy_up * wb[k:k + 1, :]
        rows.append(jnp.sum(dy_up * x, axis=0, keepdims=True))
    rows.append(jnp.sum(dy, axis=0, keepdims=True))
    t_idx = lax.broadcasted_iota(jnp.int32, dy.shape, 0)
    return jnp.where(t_idx < n - (width - 1), dx, 0.0), _rows8(rows, dy.shape[1])


def _convglu_fwd(z3, fwb3, *, name):
    _, t, n = z3.shape
    half = N_DEV // 2
    nc = pl.cdiv(n, LANES)

    def body(zg_ref, zv_ref, wg_ref, wv_ref, y_ref):
        gate = _conv_taps(zg_ref[...], wg_ref[...], 3)
        val = _conv_taps(zv_ref[...], wv_ref[...], 3)
        y_ref[...] = (_gelu(gate) * val).astype(BF16)

    zblk = lambda off: pl.BlockSpec((None, t, LANES), lambda j, c: (j + off, 0, c))
    wblk = lambda off: pl.BlockSpec((None, 8, LANES), lambda j, c: (j + off, 0, c))
    return pl.pallas_call(
        body, name=name, grid=(half, nc),
        in_specs=[zblk(0), zblk(half), wblk(0), wblk(half)],
        out_specs=zblk(0),
        out_shape=jax.ShapeDtypeStruct((half, t, n), BF16),
        compiler_params=_params("parallel", "parallel"),
    )(z3, z3, fwb3, fwb3)


def _ffn_bwd_mid(ds_bf, g_out, z3, fwb3, *, name):
    t, d = ds_bf.shape
    r = g_out.shape[1]
    n = z3.shape[2]
    half = N_DEV // 2
    nc = pl.cdiv(n, LANES)
    assert n == 2 * r

    def body(ds_ref, w_ref, zg_ref, zv_ref, wg_ref, wv_ref, dz_ref, dwb_ref, wsc_ref):
        c = pl.program_id(1)

        @pl.when(c == 0)
        def _():
            wsc_ref[0:r, :] = w_ref[0]
            wsc_ref[r:2 * r, :] = w_ref[1]
            if nc * LANES > n:
                wsc_ref[n:nc * LANES, :] = jnp.zeros((nc * LANES - n, d), BF16)

        w = wsc_ref[pl.ds(pl.multiple_of(c * LANES, LANES), LANES), :]
        dyf = _dot_nt(ds_ref[...], w)
        zg, zv = zg_ref[...], zv_ref[...]
        wg, wv = wg_ref[...], wv_ref[...]
        gate = _conv_taps(zg, wg, 3)
        val = _conv_taps(zv, wv, 3)
        gl, dgl = _gelu_and_grad(gate)
        dzg, dwg = _conv_taps_bwd(dyf * val * dgl, zg, wg, 3)
        dzv, dwv = _conv_taps_bwd(dyf * gl, zv, wv, 3)
        dz_ref[0] = dzg.astype(BF16)
        dz_ref[1] = dzv.astype(BF16)
        dwb_ref[0] = dwg
        dwb_ref[1] = dwv

    zblk = lambda off: pl.BlockSpec((None, t, LANES), lambda j, c: (j + off, 0, c))
    wblk = lambda off: pl.BlockSpec((None, 8, LANES), lambda j, c: (j + off, 0, c))
    return pl.pallas_call(
        body, name=name, grid=(half, nc),
        in_specs=[pl.BlockSpec((t, d), lambda j, c: (0, 0)),
                  pl.BlockSpec((2, r, d), lambda j, c: (j, 0, 0)),
                  zblk(0), zblk(half), wblk(0), wblk(half)],
        out_specs=[pl.BlockSpec((2, None, t, LANES), lambda j, c: (0, j, 0, c)),
                   pl.BlockSpec((2, None, 8, LANES), lambda j, c: (0, j, 0, c))],
        out_shape=[jax.ShapeDtypeStruct((2, half, t, n), BF16), jax.ShapeDtypeStruct((2, half, 8, n), F32)],
        scratch_shapes=[pltpu.VMEM((nc * LANES, d), BF16)],
        compiler_params=_params("parallel", "arbitrary"),
    )(ds_bf, g_out, z3, z3, fwb3, fwb3)


def _conv_a_fwd(gr, cwb, *, cb, name):
    t, r2 = gr.shape
    r = r2 // 2
    nb = r // cb

    def body(x_ref, w_ref, o_ref):
        o_ref[...] = _conv_taps(x_ref[...], w_ref[...], 4)

    return pl.pallas_call(
        body, name=name, grid=(nb,),
        in_specs=[pl.BlockSpec((t, cb), lambda j: (0, j + nb)), pl.BlockSpec((8, cb), lambda j: (0, j))],
        out_specs=pl.BlockSpec((t, cb), lambda j: (0, j)),
        out_shape=jax.ShapeDtypeStruct((t, r), F32),
        compiler_params=_params("parallel"),
    )(gr, cwb)


def _gates_fwd(rec, bd_r, bd_i, vecs, *, tm, name):
    t, r_dim = rec.shape
    nb, cb, _ = bd_r.shape

    def body(x_ref, wr_ref, wi_ref, v_ref, a_ref, u_ref, r_ref, i_ref):
        x = x_ref[...]
        xb = x.astype(BF16)
        v = v_ref[...]
        r = _sigmoid(_dot_nn(xb, wr_ref[...]) + v[0:1, :])
        i = _sigmoid(_dot_nn(xb, wi_ref[...]) + v[1:2, :])
        log_a = (-LRU_C) * r * _softplus(-v[2:3, :])
        a_ref[...] = jnp.exp(log_a)
        u_ref[...] = jnp.sqrt(-_expm1(2.0 * log_a)) * (i * x)
        r_ref[...] = r
        i_ref[...] = i

    blk = pl.BlockSpec((tm, cb), lambda j, i: (i, j))
    wspec = pl.BlockSpec((None, cb, cb), lambda j, i: (j, 0, 0))
    out = jax.ShapeDtypeStruct((t, r_dim), F32)
    return pl.pallas_call(
        body, name=name, grid=(nb, t // tm),
        in_specs=[blk, wspec, wspec, pl.BlockSpec((8, cb), lambda j, i: (0, j))],
        out_specs=[blk, blk, blk, blk],
        out_shape=[out, out, out, out],
        compiler_params=_params("parallel", "parallel"),
    )(rec, bd_r, bd_i, vecs)


def _scan_fwd(a, u, gr, *, cb, name):
    t, r = a.shape
    nb = r // cb
    seg = t // SUBLANES

    def body(a_ref, u_ref, g_ref, h_ref, y_ref, p_ref):
        def step(k, carry):
            h, p = carry
            rows = pl.ds(k, SUBLANES, stride=seg)
            av = a_ref[rows, :]
            h = av * h + u_ref[rows, :]
            p = av * p
            h_ref[rows, :] = h
            p_ref[rows, :] = p
            return h, p

        h_fin, p_fin = lax.fori_loop(0, seg, step, (jnp.zeros((SUBLANES, cb), F32), jnp.ones((SUBLANES, cb), F32)),
                                     unroll=4)
        carry = h_fin[0:1, :]
        for s in range(1, SUBLANES):
            rows = slice(s * seg, (s + 1) * seg)
            h_ref[rows, :] = h_ref[rows, :] + p_ref[rows, :] * carry
            carry = h_fin[s:s + 1, :] + p_fin[s:s + 1, :] * carry
        y_ref[...] = (_gelu(g_ref[...]) * h_ref[...]).astype(BF16)

    blk = pl.BlockSpec((t, cb), lambda j: (0, j))
    return pl.pallas_call(
        body, name=name, grid=(nb,),
        in_specs=[blk, blk, blk],
        out_specs=[blk, pl.BlockSpec((None, t, cb), lambda j: (0, 0, j))],
        out_shape=[jax.ShapeDtypeStruct((t, r), F32), jax.ShapeDtypeStruct((1, t, r), BF16)],
        scratch_shapes=[pltpu.VMEM((t, cb), F32)],
        compiler_params=_params("parallel"),
    )(a, u, gr)


def _scan_bwd(dy, gr, hr, a, *, cb, name):
    t, r = a.shape
    nb = r // cb
    seg = t // SUBLANES

    def body(dy_ref, g_ref, h_ref, a_ref, dh_ref, da_ref, dg_ref, q_ref):
        gl, dgl = _gelu_and_grad(g_ref[...])
        dyv = dy_ref[...]
        dh_ref[...] = dyv * gl
        dg_ref[...] = (dyv * h_ref[...] * dgl).astype(BF16)

        def step(k, carry):
            cin, q = carry
            rows = pl.ds(seg - 1 - k, SUBLANES, stride=seg)
            dh = dh_ref[rows, :] + cin
            dh_ref[rows, :] = dh
            q_ref[rows, :] = q
            av = a_ref[rows, :]
            return av * dh, av * q

        c_fin, q_fin = lax.fori_loop(0, seg, step, (jnp.zeros((SUBLANES, cb), F32), jnp.ones((SUBLANES, cb), F32)),
                                     unroll=4)
        carry = c_fin[SUBLANES - 1:SUBLANES, :]
        for s in range(SUBLANES - 2, -1, -1):
            rows = slice(s * seg, (s + 1) * seg)
            dh_ref[rows, :] = dh_ref[rows, :] + q_ref[rows, :] * carry
            carry = c_fin[s:s + 1, :] + q_fin[s:s + 1, :] * carry
        da_ref[...] = dh_ref[...] * _shift_down(h_ref[...], 1)

    blk = pl.BlockSpec((t, cb), lambda j: (0, j))
    return pl.pallas_call(
        body, name=name, grid=(nb,),
        in_specs=[blk, blk, blk, blk],
        out_specs=[blk, blk, blk],
        out_shape=[jax.ShapeDtypeStruct((t, r), F32), jax.ShapeDtypeStruct((t, r), F32),
                   jax.ShapeDtypeStruct((t, r), BF16)],
        scratch_shapes=[pltpu.VMEM((t, cb), F32)],
        compiler_params=_params("parallel"),
    )(dy, gr, hr, a)


def _gates_bwd(rec, r, i, a, dh, da, bd_r, bd_i, vecs, *, tm, name):
    t, r_dim = rec.shape
    nb, cb, _ = bd_r.shape

    def body(x_ref, r_ref, i_ref, a_ref, dh_ref, da_ref, wr_ref, wi_ref, v_ref, dx_ref, dpr_ref, dpi_ref, dv_ref):
        step = pl.program_id(1)
        x, r, i, a, dh, da = x_ref[...], r_ref[...], i_ref[...], a_ref[...], dh_ref[...], da_ref[...]
        lam = v_ref[...][2:3, :]
        sp = _softplus(-lam)
        a2 = a * a
        mult = jnp.sqrt(-_expm1(2.0 * (-LRU_C) * r * sp))
        d_i = dh * mult * x
        d_log_a = da * a - (dh * i * x) * a2 / mult
        d_r = d_log_a * ((-LRU_C) * sp)
        d_sp = jnp.sum(d_log_a * ((-LRU_C) * r), axis=0, keepdims=True)
        d_pre_r = d_r * r * (1.0 - r)
        d_pre_i = d_i * i * (1.0 - i)
        dprb = d_pre_r.astype(BF16)
        dpib = d_pre_i.astype(BF16)
        dx_ref[...] = dh * mult * i + _dot_nt(dprb, wr_ref[...]) + _dot_nt(dpib, wi_ref[...])
        dpr_ref[...] = dprb
        dpi_ref[...] = dpib
        upd = _rows8([jnp.sum(d_pre_r, axis=0, keepdims=True), jnp.sum(d_pre_i, axis=0, keepdims=True),
                      -d_sp * _sigmoid(-lam)], cb)

        @pl.when(step == 0)
        def _():
            dv_ref[...] = upd

        @pl.when(step > 0)
        def _():
            dv_ref[...] += upd

    blk = pl.BlockSpec((tm, cb), lambda j, i: (i, j))
    wspec = pl.BlockSpec((None, cb, cb), lambda j, i: (j, 0, 0))
    vspec = pl.BlockSpec((8, cb), lambda j, i: (0, j))
    return pl.pallas_call(
        body, name=name, grid=(nb, t // tm),
        in_specs=[blk] * 6 + [wspec, wspec, vspec],
        out_specs=[blk, blk, blk, vspec],
        out_shape=[jax.ShapeDtypeStruct((t, r_dim), F32), jax.ShapeDtypeStruct((t, r_dim), BF16),
                   jax.ShapeDtypeStruct((t, r_dim), BF16), jax.ShapeDtypeStruct((8, r_dim), F32)],
        compiler_params=_params("parallel", "arbitrary"),
    )(rec, r, i, a, dh, da, bd_r, bd_i, vecs)


def _bd_grad(rec, dpr, dpi, *, cb, name):
    t, r = rec.shape
    nb = r // cb

    def body(x_ref, dr_ref, di_ref, gr_ref, gi_ref):
        xb = x_ref[...].astype(BF16)
        gr_ref[...] = _dot_tn(xb, dr_ref[...])
        gi_ref[...] = _dot_tn(xb, di_ref[...])

    blk = pl.BlockSpec((t, cb), lambda j: (0, j))
    wspec = pl.BlockSpec((None, cb, cb), lambda j: (j, 0, 0))
    out = jax.ShapeDtypeStruct((nb, cb, cb), F32)
    return pl.pallas_call(
        body, name=name, grid=(nb,),
        in_specs=[blk, blk, blk], out_specs=[wspec, wspec], out_shape=[out, out],
        compiler_params=_params("parallel"),
    )(rec, dpr, dpi)


def _conv_a_bwd(d_rec, gr, dgate, cwb, *, cb, name):
    t, r = d_rec.shape
    nb = r // cb

    def body(dy_ref, x_ref, dg_ref, w_ref, dact_ref, dw_ref):
        dx, dw = _conv_taps_bwd(dy_ref[...], x_ref[...], w_ref[...], 4)
        dact_ref[0] = dg_ref[...]
        dact_ref[1] = dx.astype(BF16)
        dw_ref[...] = dw

    blk = pl.BlockSpec((t, cb), lambda j: (0, j))
    vspec = pl.BlockSpec((8, cb), lambda j: (0, j))
    return pl.pallas_call(
        body, name=name, grid=(nb,),
        in_specs=[blk, pl.BlockSpec((t, cb), lambda j: (0, j + nb)), blk, vspec],
        out_specs=[pl.BlockSpec((2, t, cb), lambda j: (0, 0, j)), vspec],
        out_shape=[jax.ShapeDtypeStruct((2, t, r), BF16), jax.ShapeDtypeStruct((8, r), F32)],
        compiler_params=_params("parallel"),
    )(d_rec, gr, dgate, cwb)


def _split3(x):
    p0 = x.astype(BF16)
    r1 = x - p0.astype(F32)
    p1 = r1.astype(BF16)
    p2 = (r1 - p1.astype(F32)).astype(BF16)
    return p0, p1, p2


def _fgate_fwd(fp, fb, *, tq, name):
    t = fp.shape[0]

    def body(f_ref, b_ref, c_ref, ct_ref):
        logf = -_softplus(-(f_ref[...] + b_ref[...]))
        rows = pl.program_id(0) * tq + lax.broadcasted_iota(jnp.int32, (tq, t), 0)
        cols = lax.broadcasted_iota(jnp.int32, (tq, t), 1)
        tri = (cols <= rows).astype(BF16)
        p0, p1, p2 = _split3(logf)
        c = _dot_nn(tri, p0) + _dot_nn(tri, p1) + _dot_nn(tri, p2)
        c_ref[...] = c
        ct_ref[...] = c.T

    return pl.pallas_call(
        body, name=name, grid=(t // tq,),
        in_specs=[pl.BlockSpec((t, LANES), lambda i: (0, 0)), pl.BlockSpec((1, LANES), lambda i: (0, 0))],
        out_specs=[pl.BlockSpec((tq, LANES), lambda i: (i, 0)), pl.BlockSpec((LANES, tq), lambda i: (0, i))],
        out_shape=[jax.ShapeDtypeStruct((t, LANES), F32), jax.ShapeDtypeStruct((LANES, t), F32)],
        compiler_params=_params("parallel"),
    )(fp, fb)


def _fgate_bwd(dct, fp, fb, *, tq, name):
    t = fp.shape[0]

    def body(d_ref, f_ref, b_ref, o_ref, db_ref):
        i = pl.program_id(0)
        rows = lax.broadcasted_iota(jnp.int32, (t, tq), 0)
        cols = i * tq + lax.broadcasted_iota(jnp.int32, (t, tq), 1)
        tri = (rows >= cols).astype(BF16)
        p0, p1, p2 = _split3(d_ref[...])
        dlogf = (_dot_nn(p0, tri) + _dot_nn(p1, tri) + _dot_nn(p2, tri)).T
        df = dlogf * _sigmoid(-(f_ref[...] + b_ref[...]))
        o_ref[...] = df.astype(BF16)
        upd = _rows8([jnp.sum(df, axis=0, keepdims=True)], LANES)

        @pl.when(i == 0)
        def _():
            db_ref[...] = upd

        @pl.when(i > 0)
        def _():
            db_ref[...] += upd

    return pl.pallas_call(
        body, name=name, grid=(t // tq,),
        in_specs=[pl.BlockSpec((LANES, t), lambda i: (0, 0)), pl.BlockSpec((tq, LANES), lambda i: (i, 0)),
                  pl.BlockSpec((1, LANES), lambda i: (0, 0))],
        out_specs=[pl.BlockSpec((tq, LANES), lambda i: (i, 0)), pl.BlockSpec((8, LANES), lambda i: (0, 0))],
        out_shape=[jax.ShapeDtypeStruct((t, LANES), BF16), jax.ShapeDtypeStruct((8, LANES), F32)],
        compiler_params=_params("arbitrary"),
    )(dct, fp, fb)


def _pair_sum(a, b, *, tm, name):
    t, d = a.shape

    def body(a_ref, b_ref, o_ref):
        o_ref[...] = (a_ref[...] + b_ref[...]).astype(BF16)

    row = pl.BlockSpec((tm, d), lambda i: (i, 0))
    return pl.pallas_call(
        body, name=name, grid=(t // tm,), in_specs=[row, row], out_specs=row,
        out_shape=jax.ShapeDtypeStruct((t, d), BF16), compiler_params=_params("parallel"),
    )(a, b)


FWD_HEAD_TILES = 2
BWD_HEAD_TILES = 1


def _head_block_width(dh, tiles):
    return tiles * LANES if tiles * LANES // dh <= 8 else LANES


def _head_masks(dh, bw):
    lane = lax.broadcasted_iota(jnp.int32, (1, bw), 1)
    return [((lane >= e * dh) & (lane < (e + 1) * dh)) for e in range(bw // dh)]


def _head_c_row(ct_blk, head):
    sub = lax.broadcasted_iota(jnp.int32, ct_blk.shape, 0)
    return jnp.sum(jnp.where(sub == head, ct_blk, 0.0), axis=0, keepdims=True)


def _attn_weights(qm, k, c_row, q0):
    tq, t = qm.shape[0], k.shape[0]
    s = _dot_nt(qm, k) - c_row
    qi = q0 + lax.broadcasted_iota(jnp.int32, (tq, t), 0)
    ki = lax.broadcasted_iota(jnp.int32, (tq, t), 1)
    s = jnp.where(ki <= qi, s, -jnp.inf)
    m = jnp.max(s, axis=-1, keepdims=True)
    e = jnp.exp(s - m)
    return e, m, 1.0 / jnp.sum(e, axis=-1, keepdims=True)


def _key_buckets(t, tq):
    return tuple(sorted({min(-(-(i * tq) // LANES) * LANES, t) for i in range(1, t // tq + 1)}))


def _for_prefix(needed, buckets, fn):
    prev = 0
    for length in buckets:
        pl.when((needed > prev) & (needed <= length))(lambda length=length: fn(length))
        prev = length


def _attn_fwd(qg, kv, ct, *, tq, name):
    t, d2 = qg.shape
    d = d2 // 2
    dh = d // N_HEADS
    bw = _head_block_width(dh, FWD_HEAD_TILES)
    hpb = bw // dh
    nhb = d // bw
    scale = dh ** -0.5
    buckets = _key_buckets(t, tq)

    def body(q_ref, og_ref, k_ref, v_ref, ct_ref, o_ref, y_ref, st_ref):
        hb = pl.program_id(0)
        q0 = pl.program_id(1) * tq

        def run(length):
            qs = q_ref[...] * scale
            k = k_ref[0:length, :]
            v = v_ref[0:length, :]
            o = jnp.zeros((tq, bw), F32)
            lane = lax.broadcasted_iota(jnp.int32, (tq, LANES), 1)
            stats = jnp.zeros((tq, LANES), F32)
            for e, msk in enumerate(_head_masks(dh, bw)):
                c_row = _head_c_row(ct_ref[:, 0:length], hb * hpb + e)
                w, m, inv = _attn_weights(jnp.where(msk, qs, 0.0).astype(BF16), k, c_row, q0)
                o = o + _dot_nn(w.astype(BF16), jnp.where(msk, v, jnp.zeros_like(v))) * inv
                stats = jnp.where(lane == e, m, jnp.where(lane == hpb + e, inv, stats))
            o_ref[...] = o
            y_ref[...] = (o * _sigmoid(og_ref[...])).astype(BF16)
            st_ref[...] = stats

        _for_prefix(q0 + tq, buckets, run)

    qblk = pl.BlockSpec((tq, bw), lambda h, i: (i, h))
    return pl.pallas_call(
        body, name=name, grid=(nhb, t // tq),
        in_specs=[qblk, pl.BlockSpec((tq, bw), lambda h, i: (i, h + nhb)),
                  pl.BlockSpec((t, bw), lambda h, i: (0, h)), pl.BlockSpec((t, bw), lambda h, i: (0, h + nhb)),
                  pl.BlockSpec((N_HEADS, t), lambda h, i: (0, 0))],
        out_specs=[qblk, pl.BlockSpec((None, tq, bw), lambda h, i: (0, i, h)),
                   pl.BlockSpec((None, tq, LANES), lambda h, i: (h, i, 0))],
        out_shape=[jax.ShapeDtypeStruct((t, d), F32), jax.ShapeDtypeStruct((1, t, d), BF16),
                   jax.ShapeDtypeStruct((nhb, t, LANES), F32)],
        compiler_params=_params("parallel", "parallel"),
    )(qg, qg, kv, kv, ct)


def _attn_bwd(dy, qg, o, stats, kv, ct, *, tq, name):
    t, d2 = qg.shape
    d = d2 // 2
    dh = d // N_HEADS
    bw = _head_block_width(dh, BWD_HEAD_TILES)
    hpb = bw // dh
    nhb = d // bw
    scale = dh ** -0.5
    n_q = t // tq
    hpb_f = _head_block_width(dh, FWD_HEAD_TILES) // dh
    ratio = hpb_f // hpb
    chunk = 2 * LANES

    def body(dy_ref, q_ref, og_ref, o_ref, st_ref, k_ref, v_ref, ct_ref, dqg_ref, dk_ref, dv_ref, dc_ref, dcq_ref):
        hb = pl.program_id(0)
        step = pl.program_id(1)

        @pl.when(step == 0)
        def _():
            dk_ref[...] = jnp.zeros((t, bw), F32)
            dv_ref[...] = jnp.zeros((t, bw), F32)
            dc_ref[...] = jnp.zeros((8, t), F32)

        def run(i):
            q0 = i * tq
            length = min(-(-(q0 + tq) // LANES) * LANES, t)
            qs = q_ref[...] * scale
            sg = _sigmoid(og_ref[...])
            dyv = dy_ref[...]
            ov = o_ref[...]
            do = dyv * sg
            dqg_ref[1] = (dyv * ov * sg * (1.0 - sg)).astype(BF16)
            lane = lax.broadcasted_iota(jnp.int32, (tq, LANES), 1)
            stats = st_ref[...]
            masks = _head_masks(dh, bw)
            heads = []
            for e, msk in enumerate(masks):
                pos = (hb % ratio) * hpb + e
                m = jnp.sum(jnp.where(lane == pos, stats, 0.0), axis=1, keepdims=True)
                inv = jnp.sum(jnp.where(lane == hpb_f + pos, stats, 0.0), axis=1, keepdims=True)
                delta = jnp.sum(jnp.where(msk, do * ov, 0.0), axis=1, keepdims=True)
                heads.append((msk, m, inv, delta, jnp.where(msk, qs, 0.0).astype(BF16),
                              jnp.where(msk, do, 0.0).astype(BF16)))
            dq = jnp.zeros((tq, bw), F32)
            dcq = jnp.zeros((tq, LANES), F32)
            row_acc = [jnp.zeros((tq, chunk), F32) for _ in heads]
            for c0 in range(0, length, chunk):
                ch = min(chunk, length - c0)
                k = k_ref[c0:c0 + ch, :]
                v = v_ref[c0:c0 + ch, :]
                dk = jnp.zeros((ch, bw), F32)
                dv = jnp.zeros((ch, bw), F32)
                dc_rows = []
                for e, (msk, m, inv, delta, qm, dom) in enumerate(heads):
                    c_row = _head_c_row(ct_ref[:, c0:c0 + ch], hb * hpb + e)
                    s = _dot_nt(qm, k) - c_row
                    if c0 + ch - 1 > q0:
                        qi = q0 + lax.broadcasted_iota(jnp.int32, (tq, ch), 0)
                        ki = c0 + lax.broadcasted_iota(jnp.int32, (tq, ch), 1)
                        s = jnp.where(ki <= qi, s, -jnp.inf)
                    p = jnp.exp(s - m) * inv
                    dsc = p * (_dot_nt(dom, v) - delta)
                    dsb = dsc.astype(BF16)
                    dq = dq + _dot_nn(dsb, jnp.where(msk, k, jnp.zeros_like(k)))
                    dk = dk + _dot_tn(dsb, qm)
                    dv = dv + _dot_tn(p.astype(BF16), dom)
                    dc_rows.append(-jnp.sum(dsc, axis=0, keepdims=True))
                    if ch == chunk:
                        row_acc[e] = row_acc[e] + dsc
                    else:
                        dcq = dcq + jnp.where(lane == e, jnp.sum(dsc, axis=1, keepdims=True), 0.0)
                dk_ref[c0:c0 + ch, :] += dk
                dv_ref[c0:c0 + ch, :] += dv
                dc_ref[:, c0:c0 + ch] += _rows8(dc_rows, ch)
            dqg_ref[0] = (dq * scale).astype(BF16)
            for e in range(len(heads)):
                dcq = dcq + jnp.where(lane == e, jnp.sum(row_acc[e], axis=1, keepdims=True), 0.0)
            dcq_ref[...] = dcq

        for i in range(n_q):
            pl.when(step == i)(lambda i=i: run(i))

    qblk = pl.BlockSpec((tq, bw), lambda h, i: (i, h))
    kblk = pl.BlockSpec((t, bw), lambda h, i: (0, h))
    return pl.pallas_call(
        body, name=name, grid=(nhb, n_q),
        in_specs=[qblk, qblk, pl.BlockSpec((tq, bw), lambda h, i: (i, h + nhb)), qblk,
                  pl.BlockSpec((None, tq, LANES), lambda h, i: (h // ratio, i, 0)),
                  kblk, pl.BlockSpec((t, bw), lambda h, i: (0, h + nhb)),
                  pl.BlockSpec((N_HEADS, t), lambda h, i: (0, 0))],
        out_specs=[pl.BlockSpec((2, tq, bw), lambda h, i: (0, i, h)), kblk, kblk,
                   pl.BlockSpec((None, 8, t), lambda h, i: (h, 0, 0)),
                   pl.BlockSpec((None, tq, LANES), lambda h, i: (h, i, 0))],
        out_shape=[jax.ShapeDtypeStruct((2, t, d), BF16), jax.ShapeDtypeStruct((t, d), F32),
                   jax.ShapeDtypeStruct((t, d), F32), jax.ShapeDtypeStruct((nhb, 8, t), F32),
                   jax.ShapeDtypeStruct((nhb, t, LANES), F32)],
        compiler_params=_params("parallel", "arbitrary"),
    )(dy, qg, qg, o, stats, kv, kv, ct)


def _loss_bwd(h, tgt, *, lo, hi, tm, name):
    t, d = h.shape

    def body(h_ref, t_ref, l_ref, dy_ref):
        i = pl.program_id(0)
        rows = i * tm + lax.broadcasted_iota(jnp.int32, (tm, d), 0)
        err = jnp.where((rows >= lo) & (rows < hi), h_ref[...] - t_ref[...], 0.0)
        dy_ref[...] = err * (1.0 / d)
        part = jnp.sum(jnp.sum(err * err, axis=0, keepdims=True), axis=1, keepdims=True) * (0.5 / d)
        upd = jnp.broadcast_to(part, (8, LANES))

        @pl.when(i == 0)
        def _():
            l_ref[...] = upd

        @pl.when(i > 0)
        def _():
            l_ref[...] += upd

    row = pl.BlockSpec((tm, d), lambda i: (i, 0))
    return pl.pallas_call(
        body, name=name, grid=(t // tm,),
        in_specs=[row, row],
        out_specs=[pl.BlockSpec((8, LANES), lambda i: (0, 0)), row],
        out_shape=[jax.ShapeDtypeStruct((8, LANES), F32), jax.ShapeDtypeStruct((t, d), F32)],
        compiler_params=_params("arbitrary"),
    )(h, tgt)


def _adamw_math(w, gv, m, v):
    bc1 = 1.0 / (1.0 - ADAM_B1 ** ADAM_STEP)
    bc2 = 1.0 / (1.0 - ADAM_B2 ** ADAM_STEP)
    nm = ADAM_B1 * m + (1.0 - ADAM_B1) * gv
    nv = ADAM_B2 * v + (1.0 - ADAM_B2) * (gv * gv)
    delta = (-ADAM_LR) * ((nm * bc1) / (jnp.sqrt(nv * bc2) + ADAM_EPS) + ADAM_WD * w)
    return delta, nm, nv


def _adamw(w, g, m, v, *, name):
    r, c = w.shape
    tr = r
    for cand in (512, 256, 128, 64, 32, 16, 8):
        if r % cand == 0 and r > cand:
            tr = cand
            break

    def body(w_ref, g_ref, m_ref, v_ref, d_ref, nm_ref, nv_ref):
        d_ref[...], nm_ref[...], nv_ref[...] = _adamw_math(w_ref[...], g_ref[...], m_ref[...], v_ref[...])

    blk = pl.BlockSpec((tr, c), lambda i: (i, 0))
    out = jax.ShapeDtypeStruct((r, c), F32)
    return pl.pallas_call(
        body, name=name, grid=(r // tr,),
        in_specs=[blk] * 4, out_specs=[blk] * 3, out_shape=[out] * 3,
        compiler_params=_params("parallel"),
    )(w, g, m, v)


def _sum_adamw(recvs, sends, me, w, m, v, *, name):
    n_l = len(recvs)
    _, r, c = recvs[0].shape
    tr = _tile(r, (256, 192, 176, 128, 96, 64, 48, 32, 16))

    def body(me_ref, *refs):
        p_refs, own_refs = refs[:n_l], refs[n_l:2 * n_l]
        w_ref, m_ref, v_ref, g_ref, d_ref, nm_ref, nv_ref, acc_ref = refs[2 * n_l:]
        layer = pl.program_id(0)
        mine = me_ref[0]
        for k in range(n_l):
            @pl.when(layer == k)
            def _(k=k):
                acc_ref[...] = jnp.zeros((tr, c), F32)
                for dev in range(N_DEV):
                    @pl.when(mine == dev)
                    def _():
                        acc_ref[...] += own_refs[k][...].astype(F32)

                    @pl.when(mine != dev)
                    def _(dev=dev):
                        acc_ref[...] += p_refs[k][dev].astype(F32)
                acc = acc_ref[...]
                g_ref[...] = acc
                d_ref[...], nm_ref[...], nv_ref[...] = _adamw_math(w_ref[...], acc, m_ref[...], v_ref[...])

    p_specs = [pl.BlockSpec((N_DEV, tr, c), lambda l, i, me_ref, k=k: (0, jnp.where(l == k, i, 0), 0))
               for k in range(n_l)]
    own_specs = [pl.BlockSpec((None, tr, c), lambda l, i, me_ref, k=k: (me_ref[0], jnp.where(l == k, i, 0), 0))
                 for k in range(n_l)]
    blk = pl.BlockSpec((None, tr, c), lambda l, i, me_ref: (l, i, 0))
    out = jax.ShapeDtypeStruct((n_l, r, c), F32)
    return pl.pallas_call(
        body, name=name,
        grid_spec=pltpu.PrefetchScalarGridSpec(
            num_scalar_prefetch=1, grid=(n_l, r // tr),
            in_specs=p_specs + own_specs + [blk] * 3, out_specs=[blk] * 4,
            scratch_shapes=[pltpu.VMEM((tr, c), F32)]),
        out_shape=[out] * 4,
        compiler_params=_params("arbitrary", "arbitrary"),
    )(me, *recvs, *sends, w, m, v)


def _sum8(parts, *, name):
    _, r, c = parts.shape
    tr = r
    for cand in (512, 256, 128, 64, 32, 16):
        if r % cand == 0 and r > cand:
            tr = cand
            break

    def body(p_ref, o_ref):
        acc = p_ref[0].astype(F32)
        for k in range(1, N_DEV):
            acc = acc + p_ref[k].astype(F32)
        o_ref[...] = acc

    return pl.pallas_call(
        body, name=name, grid=(r // tr,),
        in_specs=[pl.BlockSpec((N_DEV, tr, c), lambda i: (0, i, 0))],
        out_specs=pl.BlockSpec((tr, c), lambda i: (i, 0)),
        out_shape=jax.ShapeDtypeStruct((r, c), F32),
        compiler_params=_params("parallel"),
    )(parts)


def _my_index():
    return 4 * lax.axis_index("x") + 2 * lax.axis_index("y") + lax.axis_index("c")


def _peer(k):
    x, y, c = lax.axis_index("x"), lax.axis_index("y"), lax.axis_index("c")
    px = x ^ ((k >> 2) & 1)
    py = y ^ ((k >> 1) & 1)
    pc = c ^ (k & 1)
    return (px, py, pc), 4 * px + 2 * py + pc


def _all_gather(shards, *, name):
    n_arr = len(shards)

    def body(*refs):
        ins, outs = refs[:n_arr], refs[n_arr:2 * n_arr]
        send_sems, recv_sems, local_sems = refs[2 * n_arr:]
        me = _my_index()
        local = [pltpu.make_async_copy(ins[n], outs[n].at[me], local_sems.at[n]) for n in range(n_arr)]
        for cp in local:
            cp.start()
        sends = []
        for k in range(1, N_DEV):
            peer, _ = _peer(k)
            for n in range(n_arr):
                cp = pltpu.make_async_remote_copy(
                    src_ref=ins[n], dst_ref=outs[n].at[me], send_sem=send_sems.at[n, k - 1],
                    recv_sem=recv_sems.at[n, k - 1], device_id=peer, device_id_type=pl.DeviceIdType.MESH)
                cp.start()
                sends.append(cp)
        for k in range(1, N_DEV):
            peer, pidx = _peer(k)
            for n in range(n_arr):
                pltpu.make_async_remote_copy(
                    src_ref=ins[n], dst_ref=outs[n].at[pidx], send_sem=send_sems.at[n, k - 1],
                    recv_sem=recv_sems.at[n, k - 1], device_id=peer, device_id_type=pl.DeviceIdType.MESH).wait_recv()
        for cp in sends:
            cp.wait_send()
        for cp in local:
            cp.wait()

    hbm = pl.BlockSpec(memory_space=pl.ANY)
    return pl.pallas_call(
        body, name=name,
        in_specs=[hbm] * n_arr, out_specs=[hbm] * n_arr,
        out_shape=[jax.ShapeDtypeStruct((N_DEV,) + s.shape, s.dtype) for s in shards],
        scratch_shapes=[pltpu.SemaphoreType.DMA((n_arr, N_DEV - 1)), pltpu.SemaphoreType.DMA((n_arr, N_DEV - 1)),
                        pltpu.SemaphoreType.DMA((n_arr,))],
        compiler_params=pltpu.CompilerParams(has_side_effects=True),
    )(*shards)


_HBM = pl.BlockSpec(memory_space=pltpu.HBM)
_SEM = pl.BlockSpec(memory_space=pltpu.SEMAPHORE)
_EFFECT = pltpu.SideEffectType.DATAFLOW_SIDE_EFFECTING


def _remote(src, dst, send_sem, recv_sem, peer):
    return pltpu.make_async_remote_copy(src_ref=src, dst_ref=dst, send_sem=send_sem, recv_sem=recv_sem,
                                        device_id=peer, device_id_type=pl.DeviceIdType.MESH)


def _place_own(src, layer, me, *, out_dtype, name):
    _, r, c = src.shape
    tr = _tile(r, (256, 192, 176, 128, 96, 64, 48, 32, 16))

    def body(me_ref, s_ref, o_ref):
        o_ref[...] = s_ref[...].astype(out_dtype)

    return pl.pallas_call(
        body, name=name,
        grid_spec=pltpu.PrefetchScalarGridSpec(
            num_scalar_prefetch=1, grid=(r // tr,),
            in_specs=[pl.BlockSpec((None, tr, c), lambda i, me_ref: (layer, i, 0))],
            out_specs=pl.BlockSpec((None, tr, c), lambda i, me_ref: (me_ref[0], i, 0))),
        out_shape=jax.ShapeDtypeStruct((N_DEV, r, c), out_dtype),
        compiler_params=_params("parallel"),
    )(me, src)


def _own_blocks(srcs, *, name):
    n = len(srcs)

    def body(*refs):
        ins, outs, sems = refs[:n], refs[n:2 * n], refs[2 * n]
        me = _my_index()
        cps = [pltpu.make_async_copy(ins[t].at[me], outs[t].at[me], sems.at[t]) for t in range(n)]
        for cp in cps:
            cp.start()
        for cp in cps:
            cp.wait()

    return pl.pallas_call(
        body, name=name, in_specs=[_HBM] * n, out_specs=[_HBM] * n,
        out_shape=[jax.ShapeDtypeStruct(s.shape, s.dtype) for s in srcs],
        scratch_shapes=[pltpu.SemaphoreType.DMA((n,))],
    )(*srcs)


def _split_start(groups, *, scatter, name):
    sizes = [len(srcs) for srcs, _ in groups]
    flat_src = [s for srcs, _ in groups for s in srcs]
    flat_land = [l for _, lands in groups for l in lands]
    n, n_g = len(flat_land), len(groups)
    if not scatter:
        flat_src = []
    n_in = len(flat_src) + n

    def body(*refs):
        lands = refs[n_in - n:n_in]
        ins = refs[:n] if scatter else lands
        sems = refs[n_in:n_in + 2 * n_g]
        token = refs[-1]
        me = _my_index()
        t = 0
        for g in range(n_g):
            for q in range(sizes[g]):
                for k in range(1, N_DEV):
                    peer, pidx = _peer(k)
                    src = ins[t].at[pidx] if scatter else ins[t].at[me]
                    slot = q * (N_DEV - 1) + k - 1
                    _remote(src, lands[t].at[me], sems[2 * g].at[slot], sems[2 * g + 1].at[slot], peer).start()
                t += 1
        token[...] = jnp.zeros_like(token)

    sem_shapes = []
    for sz in sizes:
        sem_shapes += [pltpu.SemaphoreType.DMA((sz * (N_DEV - 1),)), pltpu.SemaphoreType.DMA((sz * (N_DEV - 1),))]
    outs = pl.pallas_call(
        body, name=name,
        in_specs=[_HBM] * n_in,
        out_specs=[_SEM] * (2 * n_g) + [_HBM] * n_in + [pl.BlockSpec(memory_space=pltpu.VMEM)],
        out_shape=sem_shapes + [pltpu.HBM(a.shape, a.dtype) for a in flat_src + flat_land]
        + [jax.ShapeDtypeStruct((8, LANES), F32)],
        input_output_aliases={i: 2 * n_g + i for i in range(n_in)},
        compiler_params=pltpu.CompilerParams(has_side_effects=_EFFECT),
    )(*[pltpu.with_memory_space_constraint(a, pltpu.HBM) for a in flat_src + flat_land])
    sems, thru, token = outs[:2 * n_g], outs[2 * n_g:2 * n_g + n_in], outs[-1]
    handles, pos = [], 0
    for g, sz in enumerate(sizes):
        lands_g = thru[n_in - n + pos:n_in - n + pos + sz]
        handles.append((sems[2 * g], sems[2 * g + 1], thru[pos:pos + sz] if scatter else [], lands_g))
        pos += sz
    return handles, token


def _split_wait(handle, after, *, scatter, name):
    send_sems, recv_sems, srcs, lands = handle
    n, n_src = len(lands), len(srcs)

    def body(*refs):
        lnd = refs[n_src:n_src + n]
        ins = refs[:n_src] if scatter else lnd
        ssem, rsem = refs[n_src + n], refs[n_src + n + 1]
        me = _my_index()
        for t in range(n):
            for k in range(1, N_DEV):
                peer, pidx = _peer(k)
                block = ins[t].at[me]
                slot = t * (N_DEV - 1) + k - 1
                _remote(block, lnd[t].at[me], ssem.at[slot], rsem.at[slot], peer).wait_send()
                _remote(block, lnd[t].at[pidx], ssem.at[slot], rsem.at[slot], peer).wait_recv()

    return pl.pallas_call(
        body, name=name,
        in_specs=[_HBM] * (n_src + n) + [_SEM, _SEM, pl.BlockSpec(memory_space=pl.ANY)],
        out_specs=[_HBM] * n,
        out_shape=[pltpu.HBM(l.shape, l.dtype) for l in lands],
        input_output_aliases={n_src + t: t for t in range(n)},
        compiler_params=pltpu.CompilerParams(has_side_effects=_EFFECT),
    )(*srcs, *lands, send_sems, recv_sems, after)


def _pack(arrs, dtype, row_quantum=16):
    flat = jnp.concatenate([a.astype(dtype).reshape(-1) for a in arrs])
    pad = (-flat.shape[0]) % (row_quantum * PACK_COLS)
    if pad:
        flat = jnp.concatenate([flat, jnp.zeros((pad,), dtype)])
    return flat.reshape(-1, PACK_COLS)


def _pack8(arrs, dtype):
    flat = jnp.concatenate([a.astype(dtype).reshape(N_DEV, -1) for a in arrs], axis=1)
    pad = (-flat.shape[1]) % (16 * PACK_COLS)
    if pad:
        flat = jnp.concatenate([flat, jnp.zeros((N_DEV, pad), dtype)], axis=1)
    return flat.reshape(N_DEV, -1, PACK_COLS)


def _unpack(slab, shapes, lead):
    lead_shape = slab.shape[:lead]
    flat = slab.reshape(lead_shape + (-1,))
    outs, off = [], 0
    for shp in shapes:
        size = math.prod(shp)
        outs.append(flat[..., off:off + size].reshape(lead_shape + tuple(shp)))
        off += size
    return outs


def _cols_full(g):
    g = jnp.moveaxis(g, 0, -2)
    return g.reshape(g.shape[:-2] + (g.shape[-2] * g.shape[-1],))


def _cols_split(full):
    n = full.shape[-1] // N_DEV
    return jnp.moveaxis(full.reshape(full.shape[:-1] + (N_DEV, n)), -2, 0)


def _block_diag(w, per):
    n, b, _ = w.shape
    w4 = w.reshape(n // per, per, b, b)
    eye = jnp.eye(per, dtype=w.dtype)
    return jnp.einsum('gpab,pq->gpaqb', w4, eye).reshape(n // per, per * b, per * b)


def _block_diag_extract(g, per):
    gn, cb, _ = g.shape
    b = cb // per
    g5 = g.reshape(gn, per, b, per, b)
    return jnp.stack([g5[:, p, :, p, :] for p in range(per)], axis=1).reshape(gn * per, b, b)


def _slab2d(a):
    return a.reshape(-1, a.shape[-1])


def _lru_block_cols(r_dim):
    lru = r_dim // N_LRU_BLOCKS
    return lru * LANES // math.gcd(lru, LANES)


BIG = ("a_w_in", "a_w_out", "b_w_in", "b_w_out", "f_w_in", "f_w_out")
COL_F32 = ("meta", "a_conv_w", "a_conv_b", "a_b_r", "a_b_i", "a_lambda", "f_conv_w")
REPLICATED = ("a_w_r", "a_w_i", "kv_f_b", "f_conv_b", "ln1_g", "ln1_b", "ln2_g", "ln2_b")
WEIGHT_NAMES = ("meta", "a_w_in", "a_conv_w", "a_conv_b", "a_w_r", "a_b_r", "a_w_i", "a_b_i", "a_lambda", "a_w_out",
                "kv_w", "kv_f_b", "b_w_in", "b_w_out", "f_w_in", "f_conv_w", "f_conv_b", "f_w_out",
                "ln1_g", "ln1_b", "ln2_g", "ln2_b")


def _kv_layout(kv_gathered, d):
    kv_full = _cols_full(kv_gathered)
    kv_pad = 2 * d + LANES - kv_full.shape[1]
    return jnp.concatenate([kv_full, jnp.zeros((d, kv_pad), kv_full.dtype)], axis=1)


def _small_layouts(small):
    r_dim = small["a_lambda"].shape[1]
    n_f = small["f_conv_b"].shape[1] // N_DEV
    cb = _lru_block_cols(r_dim)
    per = cb // (r_dim // N_LRU_BLOCKS)
    n_a = small["a_lambda"].shape[0]
    f_conv_w3 = small["f_conv_w"].reshape(N_LAYERS, 3, N_DEV, n_f).transpose(0, 2, 1, 3)
    f_conv_b3 = small["f_conv_b"].reshape(N_LAYERS, N_DEV, 1, n_f)
    return {
        "kv_fb": jnp.concatenate([small["kv_f_b"], jnp.zeros((LANES - N_HEADS,), F32)])[None],
        "a_cwb": jnp.concatenate([small["a_conv_w"], small["a_conv_b"][:, None],
                                  jnp.zeros((n_a, 3, r_dim), F32)], axis=1),
        "a_vecs": jnp.concatenate([jnp.stack([small["a_b_r"], small["a_b_i"], small["a_lambda"]], axis=1),
                                   jnp.zeros((n_a, 5, r_dim), F32)], axis=1),
        "a_bd_r": jnp.stack([_block_diag(small["a_w_r"][l], per) for l in range(n_a)]).astype(BF16),
        "a_bd_i": jnp.stack([_block_diag(small["a_w_i"][l], per) for l in range(n_a)]).astype(BF16),
        "f_cwb3": jnp.concatenate([f_conv_w3, f_conv_b3, jnp.zeros((N_LAYERS, N_DEV, 4, n_f), F32)], axis=2),
        "ln1_g": small["ln1_g"][:, None], "ln1_b": small["ln1_b"][:, None],
        "ln2_g": small["ln2_g"][:, None], "ln2_b": small["ln2_b"][:, None],
    }


def _local_step(h0, tgt, n_meta, n_tok, wts, hooks):
    tp, d = h0.shape
    tm = tp // 8 if (tp // 8) % 16 == 0 else tp
    tmb = _tile(tp, (1088, 512, 320, 256, 128))
    tq = 128
    tqa_fwd = tp // 4 if tp % 64 == 0 else tq
    tqa_bwd = tp // 4 if tp % 64 == 0 else tq
    r_dim = wts["a_vecs"].shape[2]
    cb = wts["a_bd_r"].shape[-1]
    sb = LANES
    n_b = N_LAYERS - N_A_LAYERS

    h, h_bf = h0, h0.astype(BF16)
    saved = []
    kvs = None
    for layer in range(N_LAYERS):
        lw = {}
        sv = {"h_bf": h_bf, "w": lw}
        if layer < N_A_LAYERS:
            lw["in"] = hooks.weight(layer, "in", h)
            sv["gr"] = _proj_in(h_bf, lw["in"], shard_major=False, name="a_in_proj")
            sv["rec"] = _conv_a_fwd(sv["gr"], wts["a_cwb"][layer], cb=cb, name="a_conv_fwd")
            a, u, sv["r"], sv["i"] = _gates_fwd(sv["rec"], wts["a_bd_r"][layer], wts["a_bd_i"][layer],
                                                wts["a_vecs"][layer], tm=tm, name="a_gates_fwd")
            sv["a"] = a
            sv["hr"], y3 = _scan_fwd(a, u, sv["gr"], cb=sb, name="a_scan_fwd")
        else:
            j = layer - N_A_LAYERS
            if j == 0:
                kv_w = _kv_layout(hooks.weight(layer, "kv_w", h), d)
                kvs = {"h_bf": h_bf, "w": kv_w}
                kvs["kv"] = _mm_nn(h_bf, kv_w[:, :2 * d], tn=_tile(2 * d, (512, 256, 128)), out_dtype=BF16,
                                   name="kv_proj")
                kvs["fp"] = _mm_nn(h_bf, kv_w[:, 2 * d:], tn=LANES, out_dtype=F32, name="f_proj")
                kvs["c"], ct = _fgate_fwd(kvs["fp"], wts["kv_fb"], tq=tq, name="fgate_fwd")
                kvs["ct"] = ct[:N_HEADS]
            lw["in"] = hooks.weight(layer, "in", kvs["c"] if j == 0 else h)
            sv["qg"] = _proj_in(h_bf, lw["in"], shard_major=False, name="b_in_proj")
            sv["o"], y3, sv["st"] = _attn_fwd(sv["qg"], kvs["kv"], kvs["ct"], tq=tqa_fwd, name="attn_fwd")
        sv["y3"] = y3
        lw["out"] = hooks.weight(layer, "out", y3)
        sv["s1"], h, h_bf = _out_ln(y3, lw["out"], h, wts["ln1_g"][layer], wts["ln1_b"][layer], n_valid=n_tok,
                                    tm=tmb // 2, name="mix_out_ln")
        sv["h1_bf"] = h_bf
        lw["f_in"] = hooks.weight(layer, "f_in", h)
        sv["z3"] = _proj_in(h_bf, lw["f_in"], shard_major=True, transposed=True, name="f_in_proj")
        sv["yf3"] = _convglu_fwd(sv["z3"], wts["f_cwb3"][layer], name="f_convglu_fwd")
        lw["f_out"] = hooks.weight(layer, "f_out", sv["yf3"])
        sv["s2"], h, h_bf = _out_ln(sv["yf3"], lw["f_out"], h, wts["ln2_g"][layer], wts["ln2_b"][layer],
                                    n_valid=n_tok, tm=tmb // 2, name="ffn_out_ln")
        saved.append(sv)

    loss_tile, dh = _loss_bwd(h, tgt, lo=n_meta, hi=n_tok, tm=tm, name="loss")

    grads = {k: [None] * N_LAYERS for k in ("f_cwb3", "ln1_gb", "ln2_gb")}
    grads.update({k: [None] * N_A_LAYERS for k in ("a_cwb", "a_bd_r", "a_bd_i", "a_vecs")})
    dkv = []
    token = jnp.zeros((), F32)
    for layer in reversed(range(N_LAYERS)):
        sv = saved[layer]
        lw = sv["w"]
        big = {}
        ds, ds_bf, grads["ln2_gb"][layer] = _ln_bwd(dh, sv["s2"], wts["ln2_g"][layer] + token, tm=tm, name="ln_bwd")
        dz, dcw = _ffn_bwd_mid(ds_bf, lw["f_out"], sv["z3"], wts["f_cwb3"][layer], name="f_bwd_mid")
        grads["f_cwb3"][layer] = dcw.reshape((N_DEV,) + dcw.shape[2:])
        dz3 = dz
        big["f_out"] = _w_out_grad(sv["yf3"], ds_bf, lw["f_out"].shape[1], name="f_w_out_grad")
        dh = _in_bwd(dz3, lw["f_in"], ds, tm=tmb, transposed=True, name="f_in_bwd")
        big["f_in"] = _w_in_grad(sv["h1_bf"], dz3, transposed=True, name="f_w_in_grad")
        token = hooks.grads_ready(layer, "ffn", big)
        big = {}
        ds, ds_bf, grads["ln1_gb"][layer] = _ln_bwd(dh, sv["s1"], wts["ln1_g"][layer] + token, tm=tm, name="ln_bwd")
        if layer < N_A_LAYERS:
            dy = _out_bwd(ds_bf, lw["out"], tm=tmb // 2, name="a_out_bwd")
            big["out"] = _w_out_grad(sv["y3"], ds_bf, lw["out"].shape[1], name="a_w_out_grad")
            d_h, d_a, dgate = _scan_bwd(dy, sv["gr"], sv["hr"], sv["a"], cb=sb, name="a_scan_bwd")
            d_rec, dpr, dpi, grads["a_vecs"][layer] = _gates_bwd(
                sv["rec"], sv["r"], sv["i"], sv["a"], d_h, d_a, wts["a_bd_r"][layer], wts["a_bd_i"][layer],
                wts["a_vecs"][layer], tm=tm, name="a_gates_bwd")
            grads["a_bd_r"][layer], grads["a_bd_i"][layer] = _bd_grad(sv["rec"], dpr, dpi, cb=cb, name="a_bd_grad")
            dact, grads["a_cwb"][layer] = _conv_a_bwd(d_rec, sv["gr"], dgate, wts["a_cwb"][layer], cb=cb,
                                                      name="a_conv_bwd")
            dh = _in_bwd(dact, lw["in"], ds, tm=tmb, name="a_in_bwd")
            big["in"] = _w_in_grad(sv["h_bf"], dact, name="a_w_in_grad")
        else:
            j = layer - N_A_LAYERS
            dy = _out_bwd(ds_bf, lw["out"], tm=tmb // 2, name="b_out_bwd")
            big["out"] = _w_out_grad(sv["y3"], ds_bf, lw["out"].shape[1], name="b_w_out_grad")
            dqg, dk, dv, dc, dcq = _attn_bwd(dy, sv["qg"], sv["o"], sv["st"], kvs["kv"], kvs["ct"], tq=tqa_bwd,
                                             name="attn_bwd")
            dkv.append((dk, dv, dc, dcq))
            dh = _in_bwd(dqg, lw["in"], ds, tm=tmb, name="b_in_bwd")
            big["in"] = _w_in_grad(sv["h_bf"], dqg, name="b_w_in_grad")
            if j == 0:
                hpb = _head_block_width(d // N_HEADS, BWD_HEAD_TILES) // (d // N_HEADS)
                dct = (dkv[0][2] + dkv[1][2])[:, :hpb, :].reshape(N_HEADS, tp)
                dcq = (dkv[0][3] + dkv[1][3])[:, :, :hpb]
                dct = dct + jnp.transpose(dcq, (0, 2, 1)).reshape(N_HEADS, tp)
                dct = jnp.concatenate([dct, jnp.zeros((LANES - N_HEADS, tp), F32)])
                df_bf, grads["kv_fb"] = _fgate_bwd(dct, kvs["fp"], wts["kv_fb"], tq=tq, name="fgate_bwd")
                dkvz = jnp.concatenate([_pair_sum(dkv[0][0], dkv[1][0], tm=tm, name="kv_pair_sum"),
                                        _pair_sum(dkv[0][1], dkv[1][1], tm=tm, name="kv_pair_sum"), df_bf], axis=1)
                dh = _mm_nt_full(dkvz, kvs["w"], dh, tm=tmb // 2, name="kv_in_bwd")
                big["kv_w"] = _mm_tn_cols(kvs["h_bf"], dkvz, tn=LANES, name="kv_w_grad")
        token = hooks.grads_ready(layer, "mix", big)
    return loss_tile, dh, grads


def _finish_small_grads(grads, d_h0, n_meta):
    r_dim = grads["a_vecs"][0].shape[1]
    per = _lru_block_cols(r_dim) // (r_dim // N_LRU_BLOCKS)
    a_cwb = jnp.stack(grads["a_cwb"])
    a_vecs = jnp.stack(grads["a_vecs"])
    f_cwb3 = jnp.stack(grads["f_cwb3"])
    ln1 = jnp.stack(grads["ln1_gb"])
    ln2 = jnp.stack(grads["ln2_gb"])
    f_rows = f_cwb3.transpose(0, 2, 1, 3).reshape(N_LAYERS, 8, -1)
    return {
        "meta": d_h0[:n_meta],
        "a_conv_w": a_cwb[:, :4], "a_conv_b": a_cwb[:, 4],
        "a_w_r": jnp.stack([_block_diag_extract(g, per) for g in grads["a_bd_r"]]),
        "a_b_r": a_vecs[:, 0],
        "a_w_i": jnp.stack([_block_diag_extract(g, per) for g in grads["a_bd_i"]]),
        "a_b_i": a_vecs[:, 1], "a_lambda": a_vecs[:, 2],
        "kv_f_b": grads["kv_fb"][0, :N_HEADS],
        "f_conv_w": f_rows[:, :3], "f_conv_b": f_rows[:, 3],
        "ln1_g": ln1[:, 0], "ln1_b": ln1[:, 1], "ln2_g": ln2[:, 0], "ln2_b": ln2[:, 1],
    }


def kernel(x, meta, a_w_in, a_conv_w, a_conv_b, a_w_r, a_b_r, a_w_i, a_b_i, a_lambda, a_w_out, kv_w, kv_f_b, b_w_in, b_w_out, f_w_in, f_conv_w, f_conv_b, f_w_out, ln1_g, ln1_b, ln2_g, ln2_b, loss_target, m_meta, m_a_w_in, m_a_conv_w, m_a_conv_b, m_a_w_r, m_a_b_r, m_a_w_i, m_a_b_i, m_a_lambda, m_a_w_out, m_kv_w, m_kv_f_b, m_b_w_in, m_b_w_out, m_f_w_in, m_f_conv_w, m_f_conv_b, m_f_w_out, m_ln1_g, m_ln1_b, m_ln2_g, m_ln2_b, v_meta, v_a_w_in, v_a_conv_w, v_a_conv_b, v_a_w_r, v_a_b_r, v_a_w_i, v_a_b_i, v_a_lambda, v_a_w_out, v_kv_w, v_kv_f_b, v_b_w_in, v_b_w_out, v_f_w_in, v_f_conv_w, v_f_conv_b, v_f_w_out, v_ln1_g, v_ln1_b, v_ln2_g, v_ln2_b):
    w = dict(meta=meta, a_w_in=a_w_in, a_conv_w=a_conv_w, a_conv_b=a_conv_b, a_w_r=a_w_r, a_b_r=a_b_r, a_w_i=a_w_i,
             a_b_i=a_b_i, a_lambda=a_lambda, a_w_out=a_w_out, kv_w=kv_w, kv_f_b=kv_f_b, b_w_in=b_w_in,
             b_w_out=b_w_out, f_w_in=f_w_in, f_conv_w=f_conv_w, f_conv_b=f_conv_b, f_w_out=f_w_out, ln1_g=ln1_g,
             ln1_b=ln1_b, ln2_g=ln2_g, ln2_b=ln2_b)
    m = dict(meta=m_meta, a_w_in=m_a_w_in, a_conv_w=m_a_conv_w, a_conv_b=m_a_conv_b, a_w_r=m_a_w_r, a_b_r=m_a_b_r,
             a_w_i=m_a_w_i, a_b_i=m_a_b_i, a_lambda=m_a_lambda, a_w_out=m_a_w_out, kv_w=m_kv_w, kv_f_b=m_kv_f_b,
             b_w_in=m_b_w_in, b_w_out=m_b_w_out, f_w_in=m_f_w_in, f_conv_w=m_f_conv_w, f_conv_b=m_f_conv_b,
             f_w_out=m_f_w_out, ln1_g=m_ln1_g, ln1_b=m_ln1_b, ln2_g=m_ln2_g, ln2_b=m_ln2_b)
    v = dict(meta=v_meta, a_w_in=v_a_w_in, a_conv_w=v_a_conv_w, a_conv_b=v_a_conv_b, a_w_r=v_a_w_r, a_b_r=v_a_b_r,
             a_w_i=v_a_w_i, a_b_i=v_a_b_i, a_lambda=v_a_lambda, a_w_out=v_a_w_out, kv_w=v_kv_w, kv_f_b=v_kv_f_b,
             b_w_in=v_b_w_in, b_w_out=v_b_w_out, f_w_in=v_f_w_in, f_conv_w=v_f_conv_w, f_conv_b=v_f_conv_b,
             f_w_out=v_f_w_out, ln1_g=v_ln1_g, ln1_b=v_ln1_b, ln2_g=v_ln2_g, ln2_b=v_ln2_b)
    shapes = {n: w[n].shape for n in WEIGHT_NAMES}

    me = jnp.reshape(_my_index(), (1,)).astype(jnp.int32)

    def as_stored(name, a):
        return jnp.swapaxes(a, 1, 2) if name == "f_w_in" else a

    param_of = {"in": ("a_w_in", "b_w_in"), "out": ("a_w_out", "b_w_out"), "f_in": ("f_w_in",) * 2,
                "f_out": ("f_w_out",) * 2}
    order = [("small", None, None)]
    for layer in range(N_LAYERS):
        if layer == N_A_LAYERS:
            order.append(("kv_w", layer, 0))
        for key in ("in", "out", "f_in", "f_out"):
            order.append((key, layer, layer if key[0] == "f" or layer < N_A_LAYERS else layer - N_A_LAYERS))
    def place(key, layer, idx):
        if key == "small":
            return _place_own(_pack([w[n] for n in COL_F32], F32)[None], 0, me, out_dtype=F32, name="place_small")
        if key == "kv_w":
            return _place_own(w["kv_w"][None], 0, me, out_dtype=BF16, name="place_kv_w")
        name = param_of[key][0 if layer < N_A_LAYERS else 1]
        return _place_own(as_stored(name, w[name]), idx, me, out_dtype=BF16, name=f"place_{name}_{idx}")

    lands = [place(*o) for o in order]
    gather_handles, gather_token = _split_start([([l], [l]) for l in lands], scatter=False, name="gather_start")
    group_of = {(key, layer): g for g, (key, layer, _) in enumerate(order)}
    (got_s,) = _split_wait(gather_handles[0], gather_token, scatter=False, name="gather_wait_small")
    small = {n: w[n] for n in REPLICATED}
    for n, part in zip(COL_F32, _unpack(got_s, [w[n].shape for n in COL_F32], 1)):
        small[n] = _cols_full(part)
    n_meta, d = small["meta"].shape

    class Hooks:
        pending = None
        received = {}
        sent = {}

        @staticmethod
        def weight(layer, key, after):
            (got,) = _split_wait(gather_handles[group_of[(key, layer)]], after, scatter=False,
                                 name=f"gather_wait_{key}_{layer}")
            return got

        @staticmethod
        def collect(after):
            if Hooks.pending is not None:
                tag, names, handle = Hooks.pending
                got = _split_wait(handle, after, scatter=True, name=f"scatter_wait_{tag}")
                Hooks.received.update(zip(names, got))
                Hooks.pending = None

        @staticmethod
        def grads_ready(layer, part, big):
            if "kv_w" in big:
                big["kv_w"] = _cols_split(big["kv_w"][:, :shapes["kv_w"][1] * N_DEV]).astype(BF16)
            names = [(key, layer) for key in big]
            send = [big[key] for key in big]
            Hooks.collect(send[0])
            empty = [lax.empty(s.shape, s.dtype) for s in send]
            handles, token = _split_start([(send, empty)], scatter=True, name=f"scatter_start_{part}_{layer}")
            Hooks.pending = (f"{part}_{layer}", names, handles[0])
            Hooks.sent.update(zip(names, handles[0][2]))
            return token[0, 0]

    Hooks.pending, Hooks.received, Hooks.sent = None, {}, {}

    n_tok = n_meta + x.shape[1]
    tp = -(-n_tok // ROW_ALIGN) * ROW_ALIGN
    pad = jnp.zeros((tp - n_tok, d), F32)
    h0 = jnp.concatenate([small["meta"], x[0], pad])
    tgt = jnp.concatenate([jnp.zeros((n_meta, d), F32), loss_target[0], pad])
    loss_tile, d_h0, grads = _local_step(h0, tgt, n_meta, n_tok, _small_layouts(small), Hooks)
    g_small = _finish_small_grads(grads, d_h0, n_meta)
    loss = lax.psum(loss_tile[0, 0], MESH_AXES)
    grad_x = d_h0[n_meta:n_tok][None]

    rep = _pack([g_small[n] for n in REPLICATED], F32, row_quantum=16 * N_DEV)
    send = [_pack8([_cols_split(g_small[n]) for n in COL_F32], F32), rep.reshape(N_DEV, -1, PACK_COLS)]
    lands = _own_blocks(send, name="scatter_own_small")
    handles, token = _split_start([(send, lands)], scatter=True, name="scatter_start_small")

    g, delta, new_m, new_v = {}, {}, {}, {}
    layers_of = {
        "a_w_in": [("in", l) for l in range(N_A_LAYERS)], "a_w_out": [("out", l) for l in range(N_A_LAYERS)],
        "b_w_in": [("in", l) for l in range(N_A_LAYERS, N_LAYERS)],
        "b_w_out": [("out", l) for l in range(N_A_LAYERS, N_LAYERS)],
        "f_w_in": [("f_in", l) for l in range(N_LAYERS)], "f_w_out": [("f_out", l) for l in range(N_LAYERS)],
        "kv_w": [("kv_w", N_A_LAYERS)],
    }
    ready = [n for n in BIG + ("kv_w",) if all(t in Hooks.received for t in layers_of[n])]

    def done(names):
        return jnp.stack([g[n][(0,) * g[n].ndim] for n in names])

    for n in ready + [n for n in BIG + ("kv_w",) if n not in ready]:
        if n not in ready and Hooks.pending is not None:
            Hooks.collect(done(ready))
        lift = (lambda a: a[None]) if n == "kv_w" else (lambda a, n=n: as_stored(n, a))
        outs = _sum_adamw([Hooks.received[t] for t in layers_of[n]], [Hooks.sent[t] for t in layers_of[n]], me,
                          lift(w[n]), lift(m[n]), lift(v[n]), name="sum_adamw_" + n)
        g[n], delta[n], new_m[n], new_v[n] = [as_stored(n, o).reshape(shapes[n]) for o in outs]
    recv_s, recv_r = _split_wait(handles[0], done(BIG + ("kv_w",)), scatter=True, name="scatter_wait_small")
    sum_s = _sum8(recv_s, name="sum_grads_f32")
    g.update(zip(COL_F32, _unpack(sum_s, [shapes[n] for n in COL_F32], 0)))
    (got_r,) = _all_gather([_sum8(recv_r, name="sum_grads_replicated")], name="gather_replicated_sums")
    g.update(zip(REPLICATED, _unpack(got_r.reshape(-1, PACK_COLS), [shapes[n] for n in REPLICATED], 0)))

    for n in COL_F32 + REPLICATED:
        shp = shapes[n]
        dl, nm, nv = _adamw(_slab2d(w[n]), _slab2d(g[n]), _slab2d(m[n]), _slab2d(v[n]), name="adamw")
        delta[n], new_m[n], new_v[n] = dl.reshape(shp), nm.reshape(shp), nv.reshape(shp)
    return (loss, grad_x, *[g[n] for n in WEIGHT_NAMES], *[delta[n] for n in WEIGHT_NAMES],
            *[new_m[n] for n in WEIGHT_NAMES], *[new_v[n] for n in WEIGHT_NAMES])
```

```python
import math

import jax
import jax.numpy as jnp
from jax import lax
from jax.experimental import pallas as pl
from jax.experimental.pallas import tpu as pltpu

F32 = jnp.float32
BF16 = jnp.bfloat16

N_DEV = 8
MESH_AXES = ("x", "y", "c")
N_LAYERS = 4
N_A_LAYERS = 2
N_LRU_BLOCKS = 16
N_HEADS = 16
LRU_C = 8.0
DN_ALPHA = (2 * N_LAYERS) ** 0.25
LN_EPS = 1e-5
ADAM_LR, ADAM_B1, ADAM_B2, ADAM_EPS, ADAM_WD, ADAM_STEP = 0.001, 0.9, 0.999, 1e-08, 0.01, 10

LANES = 128
SUBLANES = 8
ROW_ALIGN = 128
VMEM_LIMIT_BYTES = 56 * 1024 * 1024
GELU_K = math.sqrt(2.0 / math.pi)
GELU_C = 0.044715
PACK_COLS = 1024


def _params(*sem):
    return pltpu.CompilerParams(dimension_semantics=sem, vmem_limit_bytes=VMEM_LIMIT_BYTES)


def _gelu(x):
    th = jnp.tanh(GELU_K * (x + GELU_C * x * x * x))
    return 0.5 * x * (1.0 + th)


def _gelu_and_grad(x):
    x2 = x * x
    th = jnp.tanh(GELU_K * (x + GELU_C * x2 * x))
    g = 0.5 * x * (1.0 + th)
    dg = 0.5 * (1.0 + th) + 0.5 * x * (1.0 - th * th) * (GELU_K * (1.0 + 3.0 * GELU_C * x2))
    return g, dg


def _sigmoid(x):
    return 0.5 * jnp.tanh(0.5 * x) + 0.5


def _expm1(x):
    small = x * (1.0 + 0.5 * x * (1.0 + (1.0 / 3.0) * x * (1.0 + 0.25 * x)))
    return jnp.where(jnp.abs(x) < 1e-2, small, jnp.exp(x) - 1.0)


def _softplus(x):
    e = jnp.exp(-jnp.abs(x))
    small = e * (1.0 - 0.5 * e * (1.0 - (2.0 / 3.0) * e))
    return jnp.maximum(x, 0.0) + jnp.where(e < 1e-2, small, jnp.log(1.0 + e))


def _shift_down(x, s):
    if s == 0:
        return x
    rows = lax.broadcasted_iota(jnp.int32, x.shape, 0)
    return jnp.where(rows >= s, pltpu.roll(x, s, 0), 0.0)


def _shift_up(x, s):
    if s == 0:
        return x
    n = x.shape[0]
    rows = lax.broadcasted_iota(jnp.int32, x.shape, 0)
    return jnp.where(rows < n - s, pltpu.roll(x, n - s, 0), 0.0)


def _dot_nn(a, b):
    return lax.dot_general(a, b, (((1,), (0,)), ((), ())), preferred_element_type=F32)


def _dot_nt(a, b):
    return lax.dot_general(a, b, (((1,), (1,)), ((), ())), preferred_element_type=F32)


def _dot_tn(a, b):
    return lax.dot_general(a, b, (((0,), (0,)), ((), ())), preferred_element_type=F32)


def _rows8(vals, width):
    rows = lax.broadcasted_iota(jnp.int32, (8, width), 0)
    out = jnp.zeros((8, width), F32)
    for k, v in enumerate(vals):
        out = jnp.where(rows == k, jnp.broadcast_to(v, (8, width)), out)
    return out


def _tile(n, prefer):
    for c in prefer:
        if n % c == 0:
            return c
    return n


def _mm_nn(a, b, *, tn, out_dtype, name):
    m, k = a.shape
    n = b.shape[1]

    def body(a_ref, b_ref, o_ref):
        o_ref[...] = _dot_nn(a_ref[...], b_ref[...]).astype(o_ref.dtype)

    return pl.pallas_call(
        body, name=name, grid=(n // tn,),
        in_specs=[pl.BlockSpec((m, k), lambda j: (0, 0)), pl.BlockSpec((k, tn), lambda j: (0, j))],
        out_specs=pl.BlockSpec((m, tn), lambda j: (0, j)),
        out_shape=jax.ShapeDtypeStruct((m, n), out_dtype),
        compiler_params=_params("parallel"),
    )(a, b)


def _proj_in(h_bf, g_in, *, shard_major, name, transposed=False):
    t, k = h_bf.shape
    n = g_in.shape[1] if transposed else g_in.shape[2]

    def body(a_ref, b_ref, o_ref):
        o_ref[...] = _dot_nt(a_ref[...], b_ref[...]) if transposed else _dot_nn(a_ref[...], b_ref[...])

    if shard_major:
        out_spec = pl.BlockSpec((None, t, n), lambda j: (j, 0, 0))
        out_shape = jax.ShapeDtypeStruct((N_DEV, t, n), F32)
    else:
        out_spec = pl.BlockSpec((t, n), lambda j: (0, j))
        out_shape = jax.ShapeDtypeStruct((t, N_DEV * n), F32)
    return pl.pallas_call(
        body, name=name, grid=(N_DEV,),
        in_specs=[pl.BlockSpec((t, k), lambda j: (0, 0)),
                  pl.BlockSpec((None,) + g_in.shape[1:], lambda j: (j, 0, 0))],
        out_specs=out_spec, out_shape=out_shape,
        compiler_params=_params("parallel"),
    )(h_bf, g_in)


def _out_ln(y3, g_out, hin, g, b, *, n_valid, tm, name):
    nj, t, kj = y3.shape
    _, r, d = g_out.shape

    def body(y_ref, w_ref, hin_ref, g_ref, b_ref, s_ref, h_ref, hb_ref):
        w = w_ref[...].reshape(N_DEV * r, d)
        s = DN_ALPHA * hin_ref[...]
        for jj in range(nj):
            s = s + _dot_nn(y_ref[jj], w[jj * kj:(jj + 1) * kj])
        mu = jnp.mean(s, axis=-1, keepdims=True)
        xc = s - mu
        var = jnp.mean(xc * xc, axis=-1, keepdims=True)
        h = xc * lax.rsqrt(var + LN_EPS) * g_ref[...] + b_ref[...]
        s_ref[...] = s
        h_ref[...] = h
        rows = pl.program_id(0) * tm + lax.broadcasted_iota(jnp.int32, (tm, d), 0)
        hb_ref[...] = jnp.where(rows < n_valid, h, 0.0).astype(BF16)

    row = pl.BlockSpec((tm, d), lambda i: (i, 0))
    vec = pl.BlockSpec((1, d), lambda i: (0, 0))
    return pl.pallas_call(
        body, name=name, grid=(t // tm,),
        in_specs=[pl.BlockSpec((nj, tm, kj), lambda i: (0, i, 0)),
                  pl.BlockSpec((N_DEV, r, d), lambda i: (0, 0, 0)), row, vec, vec],
        out_specs=[row, row, row],
        out_shape=[jax.ShapeDtypeStruct((t, d), F32), jax.ShapeDtypeStruct((t, d), F32),
                   jax.ShapeDtypeStruct((t, d), BF16)],
        compiler_params=_params("parallel"),
    )(y3, g_out, hin, g, b)


def _out_bwd(ds_bf, g_out, *, tm, name):
    t, d = ds_bf.shape
    r = g_out.shape[1]

    def body(a_ref, w_ref, o_ref):
        o_ref[...] = _dot_nt(a_ref[...], w_ref[...].reshape(N_DEV * r, d))

    return pl.pallas_call(
        body, name=name, grid=(t // tm,),
        in_specs=[pl.BlockSpec((tm, d), lambda i: (i, 0)),
                  pl.BlockSpec((N_DEV, r, d), lambda i: (0, 0, 0))],
        out_specs=pl.BlockSpec((tm, N_DEV * r), lambda i: (i, 0)),
        out_shape=jax.ShapeDtypeStruct((t, N_DEV * r), F32),
        compiler_params=_params("parallel"),
    )(ds_bf, g_out)


def _in_bwd(dact, g_in, add, *, tm, name, alpha=DN_ALPHA, transposed=False):
    t = dact.shape[-2]
    _, k, n = g_in.shape
    if transposed:
        k, n = n, k
    halves = dact.shape[0] == 2 and dact.ndim == 3
    per = N_DEV // 2

    def body(a_ref, b_ref, add_ref, o_ref, acc_ref):
        j = pl.program_id(1)

        @pl.when(j == 0)
        def _():
            acc_ref[...] = alpha * add_ref[...]

        acc_ref[...] += _dot_nn(a_ref[...], b_ref[...]) if transposed else _dot_nt(a_ref[...], b_ref[...])

        @pl.when(j == N_DEV - 1)
        def _():
            o_ref[...] = acc_ref[...]

    if halves:
        a_spec = pl.BlockSpec((None, tm, n), lambda i, j: (j // per, i, j % per))
    elif dact.ndim == 4:
        a_spec = pl.BlockSpec((None, None, tm, n), lambda i, j: (j // per, j % per, i, 0))
    else:
        a_spec = pl.BlockSpec((None, tm, n), lambda i, j: (j, i, 0))
    return pl.pallas_call(
        body, name=name, grid=(t // tm, N_DEV),
        in_specs=[a_spec, pl.BlockSpec((None,) + g_in.shape[1:], lambda i, j: (j, 0, 0)),
                  pl.BlockSpec((tm, k), lambda i, j: (i, 0))],
        out_specs=pl.BlockSpec((tm, k), lambda i, j: (i, 0)),
        out_shape=jax.ShapeDtypeStruct((t, k), F32),
        scratch_shapes=[pltpu.VMEM((tm, k), F32)],
        compiler_params=_params("parallel", "arbitrary"),
    )(dact, g_in, add)


def _mm_nt_full(a, b, add, *, tm, name):
    t, n = a.shape
    k = b.shape[0]

    def body(a_ref, b_ref, add_ref, o_ref):
        o_ref[...] = add_ref[...] + _dot_nt(a_ref[...], b_ref[...])

    return pl.pallas_call(
        body, name=name, grid=(t // tm,),
        in_specs=[pl.BlockSpec((tm, n), lambda i: (i, 0)), pl.BlockSpec((k, n), lambda i: (0, 0)),
                  pl.BlockSpec((tm, k), lambda i: (i, 0))],
        out_specs=pl.BlockSpec((tm, k), lambda i: (i, 0)),
        out_shape=jax.ShapeDtypeStruct((t, k), F32),
        compiler_params=_params("parallel"),
    )(a, b, add)


def _w_in_grad(h_bf, dact, *, name, transposed=False):
    t, k = h_bf.shape
    halves = dact.shape[0] == 2 and dact.ndim == 3
    per = N_DEV // 2
    n = dact.shape[-1] // per if halves else dact.shape[-1]

    def body(a_ref, b_ref, o_ref):
        if transposed:
            o_ref[...] = _dot_tn(b_ref[...], a_ref[...]).astype(BF16)
        else:
            o_ref[...] = _dot_tn(a_ref[...], b_ref[...]).astype(BF16)

    if halves:
        b_spec = pl.BlockSpec((None, t, n), lambda j: (j // per, 0, j % per))
    elif dact.ndim == 4:
        b_spec = pl.BlockSpec((None, None, t, n), lambda j: (j // per, j % per, 0, 0))
    else:
        b_spec = pl.BlockSpec((None, t, n), lambda j: (j, 0, 0))
    return pl.pallas_call(
        body, name=name, grid=(N_DEV,),
        in_specs=[pl.BlockSpec((t, k), lambda j: (0, 0)), b_spec],
        out_specs=pl.BlockSpec((None, n, k) if transposed else (None, k, n), lambda j: (j, 0, 0)),
        out_shape=jax.ShapeDtypeStruct((N_DEV, n, k) if transposed else (N_DEV, k, n), BF16),
        compiler_params=_params("parallel"),
    )(h_bf, dact)


def _w_out_grad(y3, ds_bf, r, *, name):
    nj, t, kj = y3.shape
    d = ds_bf.shape[1]
    unit = r * LANES // math.gcd(r, LANES)
    ks = max([c for c in range(unit, min(kj, 768) + 1, unit) if kj % c == 0], default=kj)
    gsz = ks // r
    per = kj // ks

    def body(a_ref, b_ref, o_ref):
        o_ref[...] = _dot_tn(a_ref[...], b_ref[...]).reshape(gsz, r, d).astype(BF16)

    return pl.pallas_call(
        body, name=name, grid=(nj * per,),
        in_specs=[pl.BlockSpec((None, t, ks), lambda j: (j // per, 0, j % per)),
                  pl.BlockSpec((t, d), lambda j: (0, 0))],
        out_specs=pl.BlockSpec((gsz, r, d), lambda j: (j, 0, 0)),
        out_shape=jax.ShapeDtypeStruct((N_DEV, r, d), BF16),
        compiler_params=_params("parallel"),
    )(y3, ds_bf)


def _mm_tn_cols(a, b, *, tn, name):
    t, m = a.shape
    n = b.shape[1]

    def body(a_ref, b_ref, o_ref):
        o_ref[...] = _dot_tn(a_ref[...], b_ref[...])

    return pl.pallas_call(
        body, name=name, grid=(n // tn,),
        in_specs=[pl.BlockSpec((t, m), lambda j: (0, 0)), pl.BlockSpec((t, tn), lambda j: (0, j))],
        out_specs=pl.BlockSpec((m, tn), lambda j: (0, j)),
        out_shape=jax.ShapeDtypeStruct((m, n), F32),
        compiler_params=_params("parallel"),
    )(a, b)


def _ln_bwd(dout, s, g, *, tm, name):
    t, d = s.shape

    def body(do_ref, s_ref, g_ref, ds_ref, dsb_ref, gb_ref):
        i = pl.program_id(0)
        sv = s_ref[...]
        do = do_ref[...]
        mu = jnp.mean(sv, axis=-1, keepdims=True)
        xc = sv - mu
        var = jnp.mean(xc * xc, axis=-1, keepdims=True)
        rstd = lax.rsqrt(var + LN_EPS)
        xhat = xc * rstd
        dxhat = do * g_ref[...]
        m1 = jnp.mean(dxhat, axis=-1, keepdims=True)
        m2 = jnp.mean(dxhat * xhat, axis=-1, keepdims=True)
        ds = rstd * (dxhat - m1 - xhat * m2)
        ds_ref[...] = ds
        dsb_ref[...] = ds.astype(BF16)
        upd = _rows8([jnp.sum(do * xhat, axis=0, keepdims=True), jnp.sum(do, axis=0, keepdims=True)], d)

        @pl.when(i == 0)
        def _():
            gb_ref[...] = upd

        @pl.when(i > 0)
        def _():
            gb_ref[...] += upd

    row = pl.BlockSpec((tm, d), lambda i: (i, 0))
    return pl.pallas_call(
        body, name=name, grid=(t // tm,),
        in_specs=[row, row, pl.BlockSpec((1, d), lambda i: (0, 0))],
        out_specs=[row, row, pl.BlockSpec((8, d), lambda i: (0, 0))],
        out_shape=[jax.ShapeDtypeStruct((t, d), F32), jax.ShapeDtypeStruct((t, d), BF16),
                   jax.ShapeDtypeStruct((8, d), F32)],
        compiler_params=_params("arbitrary"),
    )(dout, s, g)


def _roll_down(x, s):
    return x if s == 0 else pltpu.roll(x, s, 0)


def _conv_taps(x, wb, width):
    y = jnp.broadcast_to(wb[width:width + 1, :], x.shape)
    for k in range(width):
        y = y + _roll_down(x, width - 1 - k) * wb[k:k + 1, :]
    return y


def _conv_taps_bwd(dy, x, wb, width):
    n = dy.shape[0]
    dx = jnp.zeros_like(dy)
    rows = []
    for k in range(width):
        s = width - 1 - k
        dy_up = dy if s == 0 else pltpu.roll(dy, n - s, 0)
        dx = dx + dy_up * wb[k:k + 1, :]
        rows.append(jnp.sum(dy_up * x, axis=0, keepdims=True))
    rows.append(jnp.sum(dy, axis=0, keepdims=True))
    t_idx = lax.broadcasted_iota(jnp.int32, dy.shape, 0)
    return jnp.where(t_idx < n - (width - 1), dx, 0.0), _rows8(rows, dy.shape[1])


def _convglu_fwd(z3, fwb3, *, name):
    _, t, n = z3.shape
    half = N_DEV // 2
    nc = pl.cdiv(n, LANES)

    def body(zg_ref, zv_ref, wg_ref, wv_ref, y_ref):
        gate = _conv_taps(zg_ref[...], wg_ref[...], 3)
        val = _conv_taps(zv_ref[...], wv_ref[...], 3)
        y_ref[...] = (_gelu(gate) * val).astype(BF16)

    zblk = lambda off: pl.BlockSpec((None, t, LANES), lambda j, c: (j + off, 0, c))
    wblk = lambda off: pl.BlockSpec((None, 8, LANES), lambda j, c: (j + off, 0, c))
    return pl.pallas_call(
        body, name=name, grid=(half, nc),
        in_specs=[zblk(0), zblk(half), wblk(0), wblk(half)],
        out_specs=zblk(0),
        out_shape=jax.ShapeDtypeStruct((half, t, n), BF16),
        compiler_params=_params("parallel", "parallel"),
    )(z3, z3, fwb3, fwb3)


def _ffn_bwd_mid(ds_bf, g_out, z3, fwb3, *, name):
    t, d = ds_bf.shape
    r = g_out.shape[1]
    n = z3.shape[2]
    half = N_DEV // 2
    nc = pl.cdiv(n, LANES)
    assert n == 2 * r

    def body(ds_ref, w_ref, zg_ref, zv_ref, wg_ref, wv_ref, dz_ref, dwb_ref, wsc_ref):
        c = pl.program_id(1)

        @pl.when(c == 0)
        def _():
            wsc_ref[0:r, :] = w_ref[0]
            wsc_ref[r:2 * r, :] = w_ref[1]
            if nc * LANES > n:
                wsc_ref[n:nc * LANES, :] = jnp.zeros((nc * LANES - n, d), BF16)

        w = wsc_ref[pl.ds(pl.multiple_of(c * LANES, LANES), LANES), :]
        dyf = _dot_nt(ds_ref[...], w)
        zg, zv = zg_ref[...], zv_ref[...]
        wg, wv = wg_ref[...], wv_ref[...]
        gate = _conv_taps(zg, wg, 3)
        val = _conv_taps(zv, wv, 3)
        gl, dgl = _gelu_and_grad(gate)
        dzg, dwg = _conv_taps_bwd(dyf * val * dgl, zg, wg, 3)
        dzv, dwv = _conv_taps_bwd(dyf * gl, zv, wv, 3)
        dz_ref[0] = dzg.astype(BF16)
        dz_ref[1] = dzv.astype(BF16)
        dwb_ref[0] = dwg
        dwb_ref[1] = dwv

    zblk = lambda off: pl.BlockSpec((None, t, LANES), lambda j, c: (j + off, 0, c))
    wblk = lambda off: pl.BlockSpec((None, 8, LANES), lambda j, c: (j + off, 0, c))
    return pl.pallas_call(
        body, name=name, grid=(half, nc),
        in_specs=[pl.BlockSpec((t, d), lambda j, c: (0, 0)),
                  pl.BlockSpec((2, r, d), lambda j, c: (j, 0, 0)),
                  zblk(0), zblk(half), wblk(0), wblk(half)],
        out_specs=[pl.BlockSpec((2, None, t, LANES), lambda j, c: (0, j, 0, c)),
                   pl.BlockSpec((2, None, 8, LANES), lambda j, c: (0, j, 0, c))],
        out_shape=[jax.ShapeDtypeStruct((2, half, t, n), BF16), jax.ShapeDtypeStruct((2, half, 8, n), F32)],
        scratch_shapes=[pltpu.VMEM((nc * LANES, d), BF16)],
        compiler_params=_params("parallel", "arbitrary"),
    )(ds_bf, g_out, z3, z3, fwb3, fwb3)


def _conv_a_fwd(gr, cwb, *, cb, name):
    t, r2 = gr.shape
    r = r2 // 2
    nb = r // cb

    def body(x_ref, w_ref, o_ref):
        o_ref[...] = _conv_taps(x_ref[...], w_ref[...], 4)

    return pl.pallas_call(
        body, name=name, grid=(nb,),
        in_specs=[pl.BlockSpec((t, cb), lambda j: (0, j + nb)), pl.BlockSpec((8, cb), lambda j: (0, j))],
        out_specs=pl.BlockSpec((t, cb), lambda j: (0, j)),
        out_shape=jax.ShapeDtypeStruct((t, r), F32),
        compiler_params=_params("parallel"),
    )(gr, cwb)


def _gates_fwd(rec, bd_r, bd_i, vecs, *, tm, name):
    t, r_dim = rec.shape
    nb, cb, _ = bd_r.shape

    def body(x_ref, wr_ref, wi_ref, v_ref, a_ref, u_ref, r_ref, i_ref):
        x = x_ref[...]
        xb = x.astype(BF16)
        v = v_ref[...]
        r = _sigmoid(_dot_nn(xb, wr_ref[...]) + v[0:1, :])
        i = _sigmoid(_dot_nn(xb, wi_ref[...]) + v[1:2, :])
        log_a = (-LRU_C) * r * _softplus(-v[2:3, :])
        a_ref[...] = jnp.exp(log_a)
        u_ref[...] = jnp.sqrt(-_expm1(2.0 * log_a)) * (i * x)
        r_ref[...] = r
        i_ref[...] = i

    blk = pl.BlockSpec((tm, cb), lambda j, i: (i, j))
    wspec = pl.BlockSpec((None, cb, cb), lambda j, i: (j, 0, 0))
    out = jax.ShapeDtypeStruct((t, r_dim), F32)
    return pl.pallas_call(
        body, name=name, grid=(nb, t // tm),
        in_specs=[blk, wspec, wspec, pl.BlockSpec((8, cb), lambda j, i: (0, j))],
        out_specs=[blk, blk, blk, blk],
        out_shape=[out, out, out, out],
        compiler_params=_params("parallel", "parallel"),
    )(rec, bd_r, bd_i, vecs)


def _scan_fwd(a, u, gr, *, cb, name):
    t, r = a.shape
    nb = r // cb
    seg = t // SUBLANES

    def body(a_ref, u_ref, g_ref, h_ref, y_ref, p_ref):
        def step(k, carry):
            h, p = carry
            rows = pl.ds(k, SUBLANES, stride=seg)
            av = a_ref[rows, :]
            h = av * h + u_ref[rows, :]
            p = av * p
            h_ref[rows, :] = h
            p_ref[rows, :] = p
            return h, p

        h_fin, p_fin = lax.fori_loop(0, seg, step, (jnp.zeros((SUBLANES, cb), F32), jnp.ones((SUBLANES, cb), F32)),
                                     unroll=8)
        carry = h_fin[0:1, :]
        for s in range(1, SUBLANES):
            rows = slice(s * seg, (s + 1) * seg)
            h_ref[rows, :] = h_ref[rows, :] + p_ref[rows, :] * carry
            carry = h_fin[s:s + 1, :] + p_fin[s:s + 1, :] * carry
        y_ref[...] = (_gelu(g_ref[...]) * h_ref[...]).astype(BF16)

    blk = pl.BlockSpec((t, cb), lambda j: (0, j))
    return pl.pallas_call(
        body, name=name, grid=(nb,),
        in_specs=[blk, blk, blk],
        out_specs=[blk, pl.BlockSpec((None, t, cb), lambda j: (0, 0, j))],
        out_shape=[jax.ShapeDtypeStruct((t, r), F32), jax.ShapeDtypeStruct((1, t, r), BF16)],
        scratch_shapes=[pltpu.VMEM((t, cb), F32)],
        compiler_params=_params("parallel"),
    )(a, u, gr)


def _scan_bwd(dy, gr, hr, a, *, cb, name):
    t, r = a.shape
    nb = r // cb
    seg = t // SUBLANES

    def body(dy_ref, g_ref, h_ref, a_ref, dh_ref, da_ref, dg_ref, q_ref):
        gl, dgl = _gelu_and_grad(g_ref[...])
        dyv = dy_ref[...]
        dh_ref[...] = dyv * gl
        dg_ref[...] = (dyv * h_ref[...] * dgl).astype(BF16)

        def step(k, carry):
            cin, q = carry
            rows = pl.ds(seg - 1 - k, SUBLANES, stride=seg)
            dh = dh_ref[rows, :] + cin
            dh_ref[rows, :] = dh
            q_ref[rows, :] = q
            av = a_ref[rows, :]
            return av * dh, av * q

        c_fin, q_fin = lax.fori_loop(0, seg, step, (jnp.zeros((SUBLANES, cb), F32), jnp.ones((SUBLANES, cb), F32)),
                                     unroll=8)
        carry = c_fin[SUBLANES - 1:SUBLANES, :]
        for s in range(SUBLANES - 2, -1, -1):
            rows = slice(s * seg, (s + 1) * seg)
            dh_ref[rows, :] = dh_ref[rows, :] + q_ref[rows, :] * carry
            carry = c_fin[s:s + 1, :] + q_fin[s:s + 1, :] * carry
        da_ref[...] = dh_ref[...] * _shift_down(h_ref[...], 1)

    blk = pl.BlockSpec((t, cb), lambda j: (0, j))
    return pl.pallas_call(
        body, name=name, grid=(nb,),
        in_specs=[blk, blk, blk, blk],
        out_specs=[blk, blk, blk],
        out_shape=[jax.ShapeDtypeStruct((t, r), F32), jax.ShapeDtypeStruct((t, r), F32),
                   jax.ShapeDtypeStruct((t, r), BF16)],
        scratch_shapes=[pltpu.VMEM((t, cb), F32)],
        compiler_params=_params("parallel"),
    )(dy, gr, hr, a)


def _gates_bwd(rec, r, i, a, dh, da, bd_r, bd_i, vecs, *, tm, name):
    t, r_dim = rec.shape
    nb, cb, _ = bd_r.shape

    def body(x_ref, r_ref, i_ref, a_ref, dh_ref, da_ref, wr_ref, wi_ref, v_ref, dx_ref, dpr_ref, dpi_ref, dv_ref):
        step = pl.program_id(1)
        x, r, i, a, dh, da = x_ref[...], r_ref[...], i_ref[...], a_ref[...], dh_ref[...], da_ref[...]
        lam = v_ref[...][2:3, :]
        sp = _softplus(-lam)
        a2 = a * a
        mult = jnp.sqrt(-_expm1(2.0 * (-LRU_C) * r * sp))
        d_i = dh * mult * x
        d_log_a = da * a - (dh * i * x) * a2 / mult
        d_r = d_log_a * ((-LRU_C) * sp)
        d_sp = jnp.sum(d_log_a * ((-LRU_C) * r), axis=0, keepdims=True)
        d_pre_r = d_r * r * (1.0 - r)
        d_pre_i = d_i * i * (1.0 - i)
        dprb = d_pre_r.astype(BF16)
        dpib = d_pre_i.astype(BF16)
        dx_ref[...] = dh * mult * i + _dot_nt(dprb, wr_ref[...]) + _dot_nt(dpib, wi_ref[...])
        dpr_ref[...] = dprb
        dpi_ref[...] = dpib
        upd = _rows8([jnp.sum(d_pre_r, axis=0, keepdims=True), jnp.sum(d_pre_i, axis=0, keepdims=True),
                      -d_sp * _sigmoid(-lam)], cb)

        @pl.when(step == 0)
        def _():
            dv_ref[...] = upd

        @pl.when(step > 0)
        def _():
            dv_ref[...] += upd

    blk = pl.BlockSpec((tm, cb), lambda j, i: (i, j))
    wspec = pl.BlockSpec((None, cb, cb), lambda j, i: (j, 0, 0))
    vspec = pl.BlockSpec((8, cb), lambda j, i: (0, j))
    return pl.pallas_call(
        body, name=name, grid=(nb, t // tm),
        in_specs=[blk] * 6 + [wspec, wspec, vspec],
        out_specs=[blk, blk, blk, vspec],
        out_shape=[jax.ShapeDtypeStruct((t, r_dim), F32), jax.ShapeDtypeStruct((t, r_dim), BF16),
                   jax.ShapeDtypeStruct((t, r_dim), BF16), jax.ShapeDtypeStruct((8, r_dim), F32)],
        compiler_params=_params("parallel", "arbitrary"),
    )(rec, r, i, a, dh, da, bd_r, bd_i, vecs)


def _bd_grad(rec, dpr, dpi, *, cb, name):
    t, r = rec.shape
    nb = r // cb

    def body(x_ref, dr_ref, di_ref, gr_ref, gi_ref):
        xb = x_ref[...].astype(BF16)
        gr_ref[...] = _dot_tn(xb, dr_ref[...])
        gi_ref[...] = _dot_tn(xb, di_ref[...])

    blk = pl.BlockSpec((t, cb), lambda j: (0, j))
    wspec = pl.BlockSpec((None, cb, cb), lambda j: (j, 0, 0))
    out = jax.ShapeDtypeStruct((nb, cb, cb), F32)
    return pl.pallas_call(
        body, name=name, grid=(nb,),
        in_specs=[blk, blk, blk], out_specs=[wspec, wspec], out_shape=[out, out],
        compiler_params=_params("parallel"),
    )(rec, dpr, dpi)


def _conv_a_bwd(d_rec, gr, dgate, cwb, *, cb, name):
    t, r = d_rec.shape
    nb = r // cb

    def body(dy_ref, x_ref, dg_ref, w_ref, dact_ref, dw_ref):
        dx, dw = _conv_taps_bwd(dy_ref[...], x_ref[...], w_ref[...], 4)
        dact_ref[0] = dg_ref[...]
        dact_ref[1] = dx.astype(BF16)
        dw_ref[...] = dw

    blk = pl.BlockSpec((t, cb), lambda j: (0, j))
    vspec = pl.BlockSpec((8, cb), lambda j: (0, j))
    return pl.pallas_call(
        body, name=name, grid=(nb,),
        in_specs=[blk, pl.BlockSpec((t, cb), lambda j: (0, j + nb)), blk, vspec],
        out_specs=[pl.BlockSpec((2, t, cb), lambda j: (0, 0, j)), vspec],
        out_shape=[jax.ShapeDtypeStruct((2, t, r), BF16), jax.ShapeDtypeStruct((8, r), F32)],
        compiler_params=_params("parallel"),
    )(d_rec, gr, dgate, cwb)


def _split3(x):
    p0 = x.astype(BF16)
    r1 = x - p0.astype(F32)
    p1 = r1.astype(BF16)
    p2 = (r1 - p1.astype(F32)).astype(BF16)
    return p0, p1, p2


def _fgate_fwd(fp, fb, *, tq, name):
    t = fp.shape[0]

    def body(f_ref, b_ref, c_ref, ct_ref):
        logf = -_softplus(-(f_ref[...] + b_ref[...]))
        rows = pl.program_id(0) * tq + lax.broadcasted_iota(jnp.int32, (tq, t), 0)
        cols = lax.broadcasted_iota(jnp.int32, (tq, t), 1)
        tri = (cols <= rows).astype(BF16)
        p0, p1, p2 = _split3(logf)
        c = _dot_nn(tri, p0) + _dot_nn(tri, p1) + _dot_nn(tri, p2)
        c_ref[...] = c
        ct_ref[...] = c.T

    return pl.pallas_call(
        body, name=name, grid=(t // tq,),
        in_specs=[pl.BlockSpec((t, LANES), lambda i: (0, 0)), pl.BlockSpec((1, LANES), lambda i: (0, 0))],
        out_specs=[pl.BlockSpec((tq, LANES), lambda i: (i, 0)), pl.BlockSpec((LANES, tq), lambda i: (0, i))],
        out_shape=[jax.ShapeDtypeStruct((t, LANES), F32), jax.ShapeDtypeStruct((LANES, t), F32)],
        compiler_params=_params("parallel"),
    )(fp, fb)


def _fgate_bwd(dct, fp, fb, *, tq, name):
    t = fp.shape[0]

    def body(d_ref, f_ref, b_ref, o_ref, db_ref):
        i = pl.program_id(0)
        rows = lax.broadcasted_iota(jnp.int32, (t, tq), 0)
        cols = i * tq + lax.broadcasted_iota(jnp.int32, (t, tq), 1)
        tri = (rows >= cols).astype(BF16)
        p0, p1, p2 = _split3(d_ref[...])
        dlogf = (_dot_nn(p0, tri) + _dot_nn(p1, tri) + _dot_nn(p2, tri)).T
        df = dlogf * _sigmoid(-(f_ref[...] + b_ref[...]))
        o_ref[...] = df.astype(BF16)
        upd = _rows8([jnp.sum(df, axis=0, keepdims=True)], LANES)

        @pl.when(i == 0)
        def _():
            db_ref[...] = upd

        @pl.when(i > 0)
        def _():
            db_ref[...] += upd

    return pl.pallas_call(
        body, name=name, grid=(t // tq,),
        in_specs=[pl.BlockSpec((LANES, t), lambda i: (0, 0)), pl.BlockSpec((tq, LANES), lambda i: (i, 0)),
                  pl.BlockSpec((1, LANES), lambda i: (0, 0))],
        out_specs=[pl.BlockSpec((tq, LANES), lambda i: (i, 0)), pl.BlockSpec((8, LANES), lambda i: (0, 0))],
        out_shape=[jax.ShapeDtypeStruct((t, LANES), BF16), jax.ShapeDtypeStruct((8, LANES), F32)],
        compiler_params=_params("arbitrary"),
    )(dct, fp, fb)


def _pair_sum(a, b, *, tm, name):
    t, d = a.shape

    def body(a_ref, b_ref, o_ref):
        o_ref[...] = (a_ref[...] + b_ref[...]).astype(BF16)

    row = pl.BlockSpec((tm, d), lambda i: (i, 0))
    return pl.pallas_call(
        body, name=name, grid=(t // tm,), in_specs=[row, row], out_specs=row,
        out_shape=jax.ShapeDtypeStruct((t, d), BF16), compiler_params=_params("parallel"),
    )(a, b)


FWD_HEAD_TILES = 2
BWD_HEAD_TILES = 1


def _head_block_width(dh, tiles):
    return tiles * LANES if tiles * LANES // dh <= 8 else LANES


def _head_masks(dh, bw):
    lane = lax.broadcasted_iota(jnp.int32, (1, bw), 1)
    return [((lane >= e * dh) & (lane < (e + 1) * dh)) for e in range(bw // dh)]


def _head_c_row(ct_blk, head):
    sub = lax.broadcasted_iota(jnp.int32, ct_blk.shape, 0)
    return jnp.sum(jnp.where(sub == head, ct_blk, 0.0), axis=0, keepdims=True)


def _attn_weights(qm, k, c_row, q0):
    tq, t = qm.shape[0], k.shape[0]
    s = _dot_nt(qm, k) - c_row
    qi = q0 + lax.broadcasted_iota(jnp.int32, (tq, t), 0)
    ki = lax.broadcasted_iota(jnp.int32, (tq, t), 1)
    s = jnp.where(ki <= qi, s, -jnp.inf)
    m = jnp.max(s, axis=-1, keepdims=True)
    e = jnp.exp(s - m)
    return e, m, 1.0 / jnp.sum(e, axis=-1, keepdims=True)


def _key_buckets(t, tq):
    return tuple(sorted({min(-(-(i * tq) // LANES) * LANES, t) for i in range(1, t // tq + 1)}))


def _for_prefix(needed, buckets, fn):
    prev = 0
    for length in buckets:
        pl.when((needed > prev) & (needed <= length))(lambda length=length: fn(length))
        prev = length


def _attn_fwd(qg, kv, ct, *, tq, name):
    t, d2 = qg.shape
    d = d2 // 2
    dh = d // N_HEADS
    bw = _head_block_width(dh, FWD_HEAD_TILES)
    hpb = bw // dh
    nhb = d // bw
    scale = dh ** -0.5
    buckets = _key_buckets(t, tq)

    def body(q_ref, og_ref, k_ref, v_ref, ct_ref, o_ref, y_ref, st_ref):
        hb = pl.program_id(0)
        q0 = pl.program_id(1) * tq

        def run(length):
            qs = q_ref[...] * scale
            k = k_ref[0:length, :]
            v = v_ref[0:length, :]
            o = jnp.zeros((tq, bw), F32)
            lane = lax.broadcasted_iota(jnp.int32, (tq, LANES), 1)
            stats = jnp.zeros((tq, LANES), F32)
            for e, msk in enumerate(_head_masks(dh, bw)):
                c_row = _head_c_row(ct_ref[:, 0:length], hb * hpb + e)
                w, m, inv = _attn_weights(jnp.where(msk, qs, 0.0).astype(BF16), k, c_row, q0)
                o = o + _dot_nn(w.astype(BF16), jnp.where(msk, v, jnp.zeros_like(v))) * inv
                stats = jnp.where(lane == e, m, jnp.where(lane == hpb + e, inv, stats))
            o_ref[...] = o
            y_ref[...] = (o * _sigmoid(og_ref[...])).astype(BF16)
            st_ref[...] = stats

        _for_prefix(q0 + tq, buckets, run)

    qblk = pl.BlockSpec((tq, bw), lambda h, i: (i, h))
    return pl.pallas_call(
        body, name=name, grid=(nhb, t // tq),
        in_specs=[qblk, pl.BlockSpec((tq, bw), lambda h, i: (i, h + nhb)),
                  pl.BlockSpec((t, bw), lambda h, i: (0, h)), pl.BlockSpec((t, bw), lambda h, i: (0, h + nhb)),
                  pl.BlockSpec((N_HEADS, t), lambda h, i: (0, 0))],
        out_specs=[qblk, pl.BlockSpec((None, tq, bw), lambda h, i: (0, i, h)),
                   pl.BlockSpec((None, tq, LANES), lambda h, i: (h, i, 0))],
        out_shape=[jax.ShapeDtypeStruct((t, d), F32), jax.ShapeDtypeStruct((1, t, d), BF16),
                   jax.ShapeDtypeStruct((nhb, t, LANES), F32)],
        compiler_params=_params("parallel", "parallel"),
    )(qg, qg, kv, kv, ct)


def _attn_bwd(dy, qg, o, stats, kv, ct, *, tq, name):
    t, d2 = qg.shape
    d = d2 // 2
    dh = d // N_HEADS
    bw = _head_block_width(dh, BWD_HEAD_TILES)
    hpb = bw // dh
    nhb = d // bw
    scale = dh ** -0.5
    n_q = t // tq
    hpb_f = _head_block_width(dh, FWD_HEAD_TILES) // dh
    ratio = hpb_f // hpb
    chunk = 4 * LANES

    def body(dy_ref, q_ref, og_ref, o_ref, st_ref, k_ref, v_ref, ct_ref, dqg_ref, dk_ref, dv_ref, dc_ref, dcq_ref):
        hb = pl.program_id(0)
        step = pl.program_id(1)

        @pl.when(step == 0)
        def _():
            dk_ref[...] = jnp.zeros((t, bw), F32)
            dv_ref[...] = jnp.zeros((t, bw), F32)
            dc_ref[...] = jnp.zeros((8, t), F32)

        def run(i):
            q0 = i * tq
            length = min(-(-(q0 + tq) // LANES) * LANES, t)
            qs = q_ref[...] * scale
            sg = _sigmoid(og_ref[...])
            dyv = dy_ref[...]
            ov = o_ref[...]
            do = dyv * sg
            dqg_ref[1] = (dyv * ov * sg * (1.0 - sg)).astype(BF16)
            lane = lax.broadcasted_iota(jnp.int32, (tq, LANES), 1)
            stats = st_ref[...]
            masks = _head_masks(dh, bw)
            heads = []
            for e, msk in enumerate(masks):
                pos = (hb % ratio) * hpb + e
                m = jnp.sum(jnp.where(lane == pos, stats, 0.0), axis=1, keepdims=True)
                inv = jnp.sum(jnp.where(lane == hpb_f + pos, stats, 0.0), axis=1, keepdims=True)
                delta = jnp.sum(jnp.where(msk, do * ov, 0.0), axis=1, keepdims=True)
                heads.append((msk, m, inv, delta, jnp.where(msk, qs, 0.0).astype(BF16),
                              jnp.where(msk, do, 0.0).astype(BF16)))
            dq = jnp.zeros((tq, bw), F32)
            dcq = jnp.zeros((tq, LANES), F32)
            row_acc = [jnp.zeros((tq, chunk), F32) for _ in heads]
            for c0 in range(0, length, chunk):
                ch = min(chunk, length - c0)
                k = k_ref[c0:c0 + ch, :]
                v = v_ref[c0:c0 + ch, :]
                dk = jnp.zeros((ch, bw), F32)
                dv = jnp.zeros((ch, bw), F32)
                dc_rows = []
                for e, (msk, m, inv, delta, qm, dom) in enumerate(heads):
                    c_row = _head_c_row(ct_ref[:, c0:c0 + ch], hb * hpb + e)
                    s = _dot_nt(qm, k) - c_row
                    if c0 + ch - 1 > q0:
                        qi = q0 + lax.broadcasted_iota(jnp.int32, (tq, ch), 0)
                        ki = c0 + lax.broadcasted_iota(jnp.int32, (tq, ch), 1)
                        s = jnp.where(ki <= qi, s, -jnp.inf)
                    p = jnp.exp(s - m) * inv
                    dsc = p * (_dot_nt(dom, v) - delta)
                    dsb = dsc.astype(BF16)
                    dq = dq + _dot_nn(dsb, jnp.where(msk, k, jnp.zeros_like(k)))
                    dk = dk + _dot_tn(dsb, qm)
                    dv = dv + _dot_tn(p.astype(BF16), dom)
                    dc_rows.append(-jnp.sum(dsc, axis=0, keepdims=True))
                    if ch == chunk:
                        row_acc[e] = row_acc[e] + dsc
                    else:
                        dcq = dcq + jnp.where(lane == e, jnp.sum(dsc, axis=1, keepdims=True), 0.0)
                dk_ref[c0:c0 + ch, :] += dk
                dv_ref[c0:c0 + ch, :] += dv
                dc_ref[:, c0:c0 + ch] += _rows8(dc_rows, ch)
            dqg_ref[0] = (dq * scale).astype(BF16)
            for e in range(len(heads)):
                dcq = dcq + jnp.where(lane == e, jnp.sum(row_acc[e], axis=1, keepdims=True), 0.0)
            dcq_ref[...] = dcq

        for i in range(n_q):
            pl.when(step == i)(lambda i=i: run(i))

    qblk = pl.BlockSpec((tq, bw), lambda h, i: (i, h))
    kblk = pl.BlockSpec((t, bw), lambda h, i: (0, h))
    return pl.pallas_call(
        body, name=name, grid=(nhb, n_q),
        in_specs=[qblk, qblk, pl.BlockSpec((tq, bw), lambda h, i: (i, h + nhb)), qblk,
                  pl.BlockSpec((None, tq, LANES), lambda h, i: (h // ratio, i, 0)),
                  kblk, pl.BlockSpec((t, bw), lambda h, i: (0, h + nhb)),
                  pl.BlockSpec((N_HEADS, t), lambda h, i: (0, 0))],
        out_specs=[pl.BlockSpec((2, tq, bw), lambda h, i: (0, i, h)), kblk, kblk,
                   pl.BlockSpec((None, 8, t), lambda h, i: (h, 0, 0)),
                   pl.BlockSpec((None, tq, LANES), lambda h, i: (h, i, 0))],
        out_shape=[jax.ShapeDtypeStruct((2, t, d), BF16), jax.ShapeDtypeStruct((t, d), F32),
                   jax.ShapeDtypeStruct((t, d), F32), jax.ShapeDtypeStruct((nhb, 8, t), F32),
                   jax.ShapeDtypeStruct((nhb, t, LANES), F32)],
        compiler_params=_params("parallel", "arbitrary"),
    )(dy, qg, qg, o, stats, kv, kv, ct)


def _loss_bwd(h, tgt, *, lo, hi, tm, name):
    t, d = h.shape

    def body(h_ref, t_ref, l_ref, dy_ref):
        i = pl.program_id(0)
        rows = i * tm + lax.broadcasted_iota(jnp.int32, (tm, d), 0)
        err = jnp.where((rows >= lo) & (rows < hi), h_ref[...] - t_ref[...], 0.0)
        dy_ref[...] = err * (1.0 / d)
        part = jnp.sum(jnp.sum(err * err, axis=0, keepdims=True), axis=1, keepdims=True) * (0.5 / d)
        upd = jnp.broadcast_to(part, (8, LANES))

        @pl.when(i == 0)
        def _():
            l_ref[...] = upd

        @pl.when(i > 0)
        def _():
            l_ref[...] += upd

    row = pl.BlockSpec((tm, d), lambda i: (i, 0))
    return pl.pallas_call(
        body, name=name, grid=(t // tm,),
        in_specs=[row, row],
        out_specs=[pl.BlockSpec((8, LANES), lambda i: (0, 0)), row],
        out_shape=[jax.ShapeDtypeStruct((8, LANES), F32), jax.ShapeDtypeStruct((t, d), F32)],
        compiler_params=_params("arbitrary"),
    )(h, tgt)


def _adamw_math(w, gv, m, v):
    bc1 = 1.0 / (1.0 - ADAM_B1 ** ADAM_STEP)
    bc2 = 1.0 / (1.0 - ADAM_B2 ** ADAM_STEP)
    nm = ADAM_B1 * m + (1.0 - ADAM_B1) * gv
    nv = ADAM_B2 * v + (1.0 - ADAM_B2) * (gv * gv)
    delta = (-ADAM_LR) * ((nm * bc1) / (jnp.sqrt(nv * bc2) + ADAM_EPS) + ADAM_WD * w)
    return delta, nm, nv


def _adamw(w, g, m, v, *, name):
    r, c = w.shape
    tr = r
    for cand in (512, 256, 128, 64, 32, 16, 8):
        if r % cand == 0 and r > cand:
            tr = cand
            break

    def body(w_ref, g_ref, m_ref, v_ref, d_ref, nm_ref, nv_ref):
        d_ref[...], nm_ref[...], nv_ref[...] = _adamw_math(w_ref[...], g_ref[...], m_ref[...], v_ref[...])

    blk = pl.BlockSpec((tr, c), lambda i: (i, 0))
    out = jax.ShapeDtypeStruct((r, c), F32)
    return pl.pallas_call(
        body, name=name, grid=(r // tr,),
        in_specs=[blk] * 4, out_specs=[blk] * 3, out_shape=[out] * 3,
        compiler_params=_params("parallel"),
    )(w, g, m, v)


def _sum_adamw(recvs, sends, me, w, m, v, *, name):
    n_l = len(recvs)
    _, r, c = recvs[0].shape
    tr = _tile(r, (256, 192, 176, 128, 96, 64, 48, 32, 16))

    def body(me_ref, *refs):
        p_refs, own_refs = refs[:n_l], refs[n_l:2 * n_l]
        w_ref, m_ref, v_ref, g_ref, d_ref, nm_ref, nv_ref, acc_ref = refs[2 * n_l:]
        layer = pl.program_id(0)
        mine = me_ref[0]
        for k in range(n_l):
            @pl.when(layer == k)
            def _(k=k):
                acc_ref[...] = jnp.zeros((tr, c), F32)
                for dev in range(N_DEV):
                    @pl.when(mine == dev)
                    def _():
                        acc_ref[...] += own_refs[k][...].astype(F32)

                    @pl.when(mine != dev)
                    def _(dev=dev):
                        acc_ref[...] += p_refs[k][dev].astype(F32)
                acc = acc_ref[...]
                g_ref[...] = acc
                d_ref[...], nm_ref[...], nv_ref[...] = _adamw_math(w_ref[...], acc, m_ref[...], v_ref[...])

    p_specs = [pl.BlockSpec((N_DEV, tr, c), lambda l, i, me_ref, k=k: (0, jnp.where(l == k, i, 0), 0))
               for k in range(n_l)]
    own_specs = [pl.BlockSpec((None, tr, c), lambda l, i, me_ref, k=k: (me_ref[0], jnp.where(l == k, i, 0), 0))
                 for k in range(n_l)]
    blk = pl.BlockSpec((None, tr, c), lambda l, i, me_ref: (l, i, 0))
    out = jax.ShapeDtypeStruct((n_l, r, c), F32)
    return pl.pallas_call(
        body, name=name,
        grid_spec=pltpu.PrefetchScalarGridSpec(
            num_scalar_prefetch=1, grid=(n_l, r // tr),
            in_specs=p_specs + own_specs + [blk] * 3, out_specs=[blk] * 4,
            scratch_shapes=[pltpu.VMEM((tr, c), F32)]),
        out_shape=[out] * 4,
        compiler_params=_params("arbitrary", "arbitrary"),
    )(me, *recvs, *sends, w, m, v)


def _sum8(parts, *, name):
    _, r, c = parts.shape
    tr = r
    for cand in (512, 256, 128, 64, 32, 16):
        if r % cand == 0 and r > cand:
            tr = cand
            break

    def body(p_ref, o_ref):
        acc = p_ref[0].astype(F32)
        for k in range(1, N_DEV):
            acc = acc + p_ref[k].astype(F32)
        o_ref[...] = acc

    return pl.pallas_call(
        body, name=name, grid=(r // tr,),
        in_specs=[pl.BlockSpec((N_DEV, tr, c), lambda i: (0, i, 0))],
        out_specs=pl.BlockSpec((tr, c), lambda i: (i, 0)),
        out_shape=jax.ShapeDtypeStruct((r, c), F32),
        compiler_params=_params("parallel"),
    )(parts)


def _my_index():
    return 4 * lax.axis_index("x") + 2 * lax.axis_index("y") + lax.axis_index("c")


def _peer(k):
    x, y, c = lax.axis_index("x"), lax.axis_index("y"), lax.axis_index("c")
    px = x ^ ((k >> 2) & 1)
    py = y ^ ((k >> 1) & 1)
    pc = c ^ (k & 1)
    return (px, py, pc), 4 * px + 2 * py + pc


def _all_gather(shards, *, name):
    n_arr = len(shards)

    def body(*refs):
        ins, outs = refs[:n_arr], refs[n_arr:2 * n_arr]
        send_sems, recv_sems, local_sems = refs[2 * n_arr:]
        me = _my_index()
        local = [pltpu.make_async_copy(ins[n], outs[n].at[me], local_sems.at[n]) for n in range(n_arr)]
        for cp in local:
            cp.start()
        sends = []
        for k in range(1, N_DEV):
            peer, _ = _peer(k)
            for n in range(n_arr):
                cp = pltpu.make_async_remote_copy(
                    src_ref=ins[n], dst_ref=outs[n].at[me], send_sem=send_sems.at[n, k - 1],
                    recv_sem=recv_sems.at[n, k - 1], device_id=peer, device_id_type=pl.DeviceIdType.MESH)
                cp.start()
                sends.append(cp)
        for k in range(1, N_DEV):
            peer, pidx = _peer(k)
            for n in range(n_arr):
                pltpu.make_async_remote_copy(
                    src_ref=ins[n], dst_ref=outs[n].at[pidx], send_sem=send_sems.at[n, k - 1],
                    recv_sem=recv_sems.at[n, k - 1], device_id=peer, device_id_type=pl.DeviceIdType.MESH).wait_recv()
        for cp in sends:
            cp.wait_send()
        for cp in local:
            cp.wait()

    hbm = pl.BlockSpec(memory_space=pl.ANY)
    return pl.pallas_call(
        body, name=name,
        in_specs=[hbm] * n_arr, out_specs=[hbm] * n_arr,
        out_shape=[jax.ShapeDtypeStruct((N_DEV,) + s.shape, s.dtype) for s in shards],
        scratch_shapes=[pltpu.SemaphoreType.DMA((n_arr, N_DEV - 1)), pltpu.SemaphoreType.DMA((n_arr, N_DEV - 1)),
                        pltpu.SemaphoreType.DMA((n_arr,))],
        compiler_params=pltpu.CompilerParams(has_side_effects=True),
    )(*shards)


_HBM = pl.BlockSpec(memory_space=pltpu.HBM)
_SEM = pl.BlockSpec(memory_space=pltpu.SEMAPHORE)
_EFFECT = pltpu.SideEffectType.DATAFLOW_SIDE_EFFECTING


def _remote(src, dst, send_sem, recv_sem, peer):
    return pltpu.make_async_remote_copy(src_ref=src, dst_ref=dst, send_sem=send_sem, recv_sem=recv_sem,
                                        device_id=peer, device_id_type=pl.DeviceIdType.MESH)


def _place_own(src, layer, me, *, out_dtype, name):
    _, r, c = src.shape
    tr = _tile(r, (256, 192, 176, 128, 96, 64, 48, 32, 16))

    def body(me_ref, s_ref, o_ref):
        o_ref[...] = s_ref[...].astype(out_dtype)

    return pl.pallas_call(
        body, name=name,
        grid_spec=pltpu.PrefetchScalarGridSpec(
            num_scalar_prefetch=1, grid=(r // tr,),
            in_specs=[pl.BlockSpec((None, tr, c), lambda i, me_ref: (layer, i, 0))],
            out_specs=pl.BlockSpec((None, tr, c), lambda i, me_ref: (me_ref[0], i, 0))),
        out_shape=jax.ShapeDtypeStruct((N_DEV, r, c), out_dtype),
        compiler_params=_params("parallel"),
    )(me, src)


def _own_blocks(srcs, *, name):
    n = len(srcs)

    def body(*refs):
        ins, outs, sems = refs[:n], refs[n:2 * n], refs[2 * n]
        me = _my_index()
        cps = [pltpu.make_async_copy(ins[t].at[me], outs[t].at[me], sems.at[t]) for t in range(n)]
        for cp in cps:
            cp.start()
        for cp in cps:
            cp.wait()

    return pl.pallas_call(
        body, name=name, in_specs=[_HBM] * n, out_specs=[_HBM] * n,
        out_shape=[jax.ShapeDtypeStruct(s.shape, s.dtype) for s in srcs],
        scratch_shapes=[pltpu.SemaphoreType.DMA((n,))],
    )(*srcs)


def _split_start(groups, *, scatter, name):
    sizes = [len(srcs) for srcs, _ in groups]
    flat_src = [s for srcs, _ in groups for s in srcs]
    flat_land = [l for _, lands in groups for l in lands]
    n, n_g = len(flat_land), len(groups)
    if not scatter:
        flat_src = []
    n_in = len(flat_src) + n

    def body(*refs):
        lands = refs[n_in - n:n_in]
        ins = refs[:n] if scatter else lands
        sems = refs[n_in:n_in + 2 * n_g]
        token = refs[-1]
        me = _my_index()
        t = 0
        for g in range(n_g):
            for q in range(sizes[g]):
                for k in range(1, N_DEV):
                    peer, pidx = _peer(k)
                    src = ins[t].at[pidx] if scatter else ins[t].at[me]
                    slot = q * (N_DEV - 1) + k - 1
                    _remote(src, lands[t].at[me], sems[2 * g].at[slot], sems[2 * g + 1].at[slot], peer).start()
                t += 1
        token[...] = jnp.zeros_like(token)

    sem_shapes = []
    for sz in sizes:
        sem_shapes += [pltpu.SemaphoreType.DMA((sz * (N_DEV - 1),)), pltpu.SemaphoreType.DMA((sz * (N_DEV - 1),))]
    outs = pl.pallas_call(
        body, name=name,
        in_specs=[_HBM] * n_in,
        out_specs=[_SEM] * (2 * n_g) + [_HBM] * n_in + [pl.BlockSpec(memory_space=pltpu.VMEM)],
        out_shape=sem_shapes + [pltpu.HBM(a.shape, a.dtype) for a in flat_src + flat_land]
        + [jax.ShapeDtypeStruct((8, LANES), F32)],
        input_output_aliases={i: 2 * n_g + i for i in range(n_in)},
        compiler_params=pltpu.CompilerParams(has_side_effects=_EFFECT),
    )(*[pltpu.with_memory_space_constraint(a, pltpu.HBM) for a in flat_src + flat_land])
    sems, thru, token = outs[:2 * n_g], outs[2 * n_g:2 * n_g + n_in], outs[-1]
    handles, pos = [], 0
    for g, sz in enumerate(sizes):
        lands_g = thru[n_in - n + pos:n_in - n + pos + sz]
        handles.append((sems[2 * g], sems[2 * g + 1], thru[pos:pos + sz] if scatter else [], lands_g))
        pos += sz
    return handles, token


def _split_wait(handle, after, *, scatter, name):
    send_sems, recv_sems, srcs, lands = handle
    n, n_src = len(lands), len(srcs)

    def body(*refs):
        lnd = refs[n_src:n_src + n]
        ins = refs[:n_src] if scatter else lnd
        ssem, rsem = refs[n_src + n], refs[n_src + n + 1]
        me = _my_index()
        for t in range(n):
            for k in range(1, N_DEV):
                peer, pidx = _peer(k)
                block = ins[t].at[me]
                slot = t * (N_DEV - 1) + k - 1
                _remote(block, lnd[t].at[me], ssem.at[slot], rsem.at[slot], peer).wait_send()
                _remote(block, lnd[t].at[pidx], ssem.at[slot], rsem.at[slot], peer).wait_recv()

    return pl.pallas_call(
        body, name=name,
        in_specs=[_HBM] * (n_src + n) + [_SEM, _SEM, pl.BlockSpec(memory_space=pl.ANY)],
        out_specs=[_HBM] * n,
        out_shape=[pltpu.HBM(l.shape, l.dtype) for l in lands],
        input_output_aliases={n_src + t: t for t in range(n)},
        compiler_params=pltpu.CompilerParams(has_side_effects=_EFFECT),
    )(*srcs, *lands, send_sems, recv_sems, after)


def _pack(arrs, dtype, row_quantum=16):
    flat = jnp.concatenate([a.astype(dtype).reshape(-1) for a in arrs])
    pad = (-flat.shape[0]) % (row_quantum * PACK_COLS)
    if pad:
        flat = jnp.concatenate([flat, jnp.zeros((pad,), dtype)])
    return flat.reshape(-1, PACK_COLS)


def _pack8(arrs, dtype):
    flat = jnp.concatenate([a.astype(dtype).reshape(N_DEV, -1) for a in arrs], axis=1)
    pad = (-flat.shape[1]) % (16 * PACK_COLS)
    if pad:
        flat = jnp.concatenate([flat, jnp.zeros((N_DEV, pad), dtype)], axis=1)
    return flat.reshape(N_DEV, -1, PACK_COLS)


def _unpack(slab, shapes, lead):
    lead_shape = slab.shape[:lead]
    flat = slab.reshape(lead_shape + (-1,))
    outs, off = [], 0
    for shp in shapes:
        size = math.prod(shp)
        outs.append(flat[..., off:off + size].reshape(lead_shape + tuple(shp)))
        off += size
    return outs


def _cols_full(g):
    g = jnp.moveaxis(g, 0, -2)
    return g.reshape(g.shape[:-2] + (g.shape[-2] * g.shape[-1],))


def _cols_split(full):
    n = full.shape[-1] // N_DEV
    return jnp.moveaxis(full.reshape(full.shape[:-1] + (N_DEV, n)), -2, 0)


def _block_diag(w, per):
    n, b, _ = w.shape
    w4 = w.reshape(n // per, per, b, b)
    eye = jnp.eye(per, dtype=w.dtype)
    return jnp.einsum('gpab,pq->gpaqb', w4, eye).reshape(n // per, per * b, per * b)


def _block_diag_extract(g, per):
    gn, cb, _ = g.shape
    b = cb // per
    g5 = g.reshape(gn, per, b, per, b)
    return jnp.stack([g5[:, p, :, p, :] for p in range(per)], axis=1).reshape(gn * per, b, b)


def _slab2d(a):
    return a.reshape(-1, a.shape[-1])


def _lru_block_cols(r_dim):
    lru = r_dim // N_LRU_BLOCKS
    return lru * LANES // math.gcd(lru, LANES)


BIG = ("a_w_in", "a_w_out", "b_w_in", "b_w_out", "f_w_in", "f_w_out")
COL_F32 = ("meta", "a_conv_w", "a_conv_b", "a_b_r", "a_b_i", "a_lambda", "f_conv_w")
REPLICATED = ("a_w_r", "a_w_i", "kv_f_b", "f_conv_b", "ln1_g", "ln1_b", "ln2_g", "ln2_b")
WEIGHT_NAMES = ("meta", "a_w_in", "a_conv_w", "a_conv_b", "a_w_r", "a_b_r", "a_w_i", "a_b_i", "a_lambda", "a_w_out",
                "kv_w", "kv_f_b", "b_w_in", "b_w_out", "f_w_in", "f_conv_w", "f_conv_b", "f_w_out",
                "ln1_g", "ln1_b", "ln2_g", "ln2_b")


def _kv_layout(kv_gathered, d):
    kv_full = _cols_full(kv_gathered)
    kv_pad = 2 * d + LANES - kv_full.shape[1]
    return jnp.concatenate([kv_full, jnp.zeros((d, kv_pad), kv_full.dtype)], axis=1)


def _small_layouts(small):
    r_dim = small["a_lambda"].shape[1]
    n_f = small["f_conv_b"].shape[1] // N_DEV
    cb = _lru_block_cols(r_dim)
    per = cb // (r_dim // N_LRU_BLOCKS)
    n_a = small["a_lambda"].shape[0]
    f_conv_w3 = small["f_conv_w"].reshape(N_LAYERS, 3, N_DEV, n_f).transpose(0, 2, 1, 3)
    f_conv_b3 = small["f_conv_b"].reshape(N_LAYERS, N_DEV, 1, n_f)
    return {
        "kv_fb": jnp.concatenate([small["kv_f_b"], jnp.zeros((LANES - N_HEADS,), F32)])[None],
        "a_cwb": jnp.concatenate([small["a_conv_w"], small["a_conv_b"][:, None],
                                  jnp.zeros((n_a, 3, r_dim), F32)], axis=1),
        "a_vecs": jnp.concatenate([jnp.stack([small["a_b_r"], small["a_b_i"], small["a_lambda"]], axis=1),
                                   jnp.zeros((n_a, 5, r_dim), F32)], axis=1),
        "a_bd_r": jnp.stack([_block_diag(small["a_w_r"][l], per) for l in range(n_a)]).astype(BF16),
        "a_bd_i": jnp.stack([_block_diag(small["a_w_i"][l], per) for l in range(n_a)]).astype(BF16),
        "f_cwb3": jnp.concatenate([f_conv_w3, f_conv_b3, jnp.zeros((N_LAYERS, N_DEV, 4, n_f), F32)], axis=2),
        "ln1_g": small["ln1_g"][:, None], "ln1_b": small["ln1_b"][:, None],
        "ln2_g": small["ln2_g"][:, None], "ln2_b": small["ln2_b"][:, None],
    }


def _local_step(h0, tgt, n_meta, n_tok, wts, hooks):
    tp, d = h0.shape
    tm = tp // 8 if (tp // 8) % 16 == 0 else tp
    tmb = _tile(tp, (1088, 512, 320, 256, 128))
    tq = 128
    tqa_fwd = tp // 4 if tp % 64 == 0 else tq
    tqa_bwd = tp // 4 if tp % 64 == 0 else tq
    r_dim = wts["a_vecs"].shape[2]
    cb = wts["a_bd_r"].shape[-1]
    sb = LANES
    n_b = N_LAYERS - N_A_LAYERS

    h, h_bf = h0, h0.astype(BF16)
    saved = []
    kvs = None
    for layer in range(N_LAYERS):
        lw = {}
        sv = {"h_bf": h_bf, "w": lw}
        if layer < N_A_LAYERS:
            lw["in"] = hooks.weight(layer, "in", h)
            sv["gr"] = _proj_in(h_bf, lw["in"], shard_major=False, name="a_in_proj")
            sv["rec"] = _conv_a_fwd(sv["gr"], wts["a_cwb"][layer], cb=cb, name="a_conv_fwd")
            a, u, sv["r"], sv["i"] = _gates_fwd(sv["rec"], wts["a_bd_r"][layer], wts["a_bd_i"][layer],
                                                wts["a_vecs"][layer], tm=tm, name="a_gates_fwd")
            sv["a"] = a
            sv["hr"], y3 = _scan_fwd(a, u, sv["gr"], cb=sb, name="a_scan_fwd")
        else:
            j = layer - N_A_LAYERS
            if j == 0:
                kv_w = _kv_layout(hooks.weight(layer, "kv_w", h), d)
                kvs = {"h_bf": h_bf, "w": kv_w}
                kvs["kv"] = _mm_nn(h_bf, kv_w[:, :2 * d], tn=_tile(2 * d, (512, 256, 128)), out_dtype=BF16,
                                   name="kv_proj")
                kvs["fp"] = _mm_nn(h_bf, kv_w[:, 2 * d:], tn=LANES, out_dtype=F32, name="f_proj")
                kvs["c"], ct = _fgate_fwd(kvs["fp"], wts["kv_fb"], tq=tq, name="fgate_fwd")
                kvs["ct"] = ct[:N_HEADS]
            lw["in"] = hooks.weight(layer, "in", kvs["c"] if j == 0 else h)
            sv["qg"] = _proj_in(h_bf, lw["in"], shard_major=False, name="b_in_proj")
            sv["o"], y3, sv["st"] = _attn_fwd(sv["qg"], kvs["kv"], kvs["ct"], tq=tqa_fwd, name="attn_fwd")
        sv["y3"] = y3
        lw["out"] = hooks.weight(layer, "out", y3)
        sv["s1"], h, h_bf = _out_ln(y3, lw["out"], h, wts["ln1_g"][layer], wts["ln1_b"][layer], n_valid=n_tok,
                                    tm=tmb // 2, name="mix_out_ln")
        sv["h1_bf"] = h_bf
        lw["f_in"] = hooks.weight(layer, "f_in", h)
        sv["z3"] = _proj_in(h_bf, lw["f_in"], shard_major=True, transposed=True, name="f_in_proj")
        sv["yf3"] = _convglu_fwd(sv["z3"], wts["f_cwb3"][layer], name="f_convglu_fwd")
        lw["f_out"] = hooks.weight(layer, "f_out", sv["yf3"])
        sv["s2"], h, h_bf = _out_ln(sv["yf3"], lw["f_out"], h, wts["ln2_g"][layer], wts["ln2_b"][layer],
                                    n_valid=n_tok, tm=tmb // 2, name="ffn_out_ln")
        saved.append(sv)

    loss_tile, dh = _loss_bwd(h, tgt, lo=n_meta, hi=n_tok, tm=tm, name="loss")

    grads = {k: [None] * N_LAYERS for k in ("f_cwb3", "ln1_gb", "ln2_gb")}
    grads.update({k: [None] * N_A_LAYERS for k in ("a_cwb", "a_bd_r", "a_bd_i", "a_vecs")})
    dkv = []
    token = jnp.zeros((), F32)
    for layer in reversed(range(N_LAYERS)):
        sv = saved[layer]
        lw = sv["w"]
        big = {}
        ds, ds_bf, grads["ln2_gb"][layer] = _ln_bwd(dh, sv["s2"], wts["ln2_g"][layer] + token, tm=tm, name="ln_bwd")
        dz, dcw = _ffn_bwd_mid(ds_bf, lw["f_out"], sv["z3"], wts["f_cwb3"][layer], name="f_bwd_mid")
        grads["f_cwb3"][layer] = dcw.reshape((N_DEV,) + dcw.shape[2:])
        dz3 = dz
        big["f_out"] = _w_out_grad(sv["yf3"], ds_bf, lw["f_out"].shape[1], name="f_w_out_grad")
        dh = _in_bwd(dz3, lw["f_in"], ds, tm=tmb, transposed=True, name="f_in_bwd")
        big["f_in"] = _w_in_grad(sv["h1_bf"], dz3, transposed=True, name="f_w_in_grad")
        token = hooks.grads_ready(layer, "ffn", big)
        big = {}
        ds, ds_bf, grads["ln1_gb"][layer] = _ln_bwd(dh, sv["s1"], wts["ln1_g"][layer] + token, tm=tm, name="ln_bwd")
        if layer < N_A_LAYERS:
            dy = _out_bwd(ds_bf, lw["out"], tm=tmb // 2, name="a_out_bwd")
            big["out"] = _w_out_grad(sv["y3"], ds_bf, lw["out"].shape[1], name="a_w_out_grad")
            d_h, d_a, dgate = _scan_bwd(dy, sv["gr"], sv["hr"], sv["a"], cb=sb, name="a_scan_bwd")
            d_rec, dpr, dpi, grads["a_vecs"][layer] = _gates_bwd(
                sv["rec"], sv["r"], sv["i"], sv["a"], d_h, d_a, wts["a_bd_r"][layer], wts["a_bd_i"][layer],
                wts["a_vecs"][layer], tm=tm, name="a_gates_bwd")
            grads["a_bd_r"][layer], grads["a_bd_i"][layer] = _bd_grad(sv["rec"], dpr, dpi, cb=cb, name="a_bd_grad")
            dact, grads["a_cwb"][layer] = _conv_a_bwd(d_rec, sv["gr"], dgate, wts["a_cwb"][layer], cb=cb,
                                                      name="a_conv_bwd")
            dh = _in_bwd(dact, lw["in"], ds, tm=tmb, name="a_in_bwd")
            big["in"] = _w_in_grad(sv["h_bf"], dact, name="a_w_in_grad")
        else:
            j = layer - N_A_LAYERS
            dy = _out_bwd(ds_bf, lw["out"], tm=tmb // 2, name="b_out_bwd")
            big["out"] = _w_out_grad(sv["y3"], ds_bf, lw["out"].shape[1], name="b_w_out_grad")
            dqg, dk, dv, dc, dcq = _attn_bwd(dy, sv["qg"], sv["o"], sv["st"], kvs["kv"], kvs["ct"], tq=tqa_bwd,
                                             name="attn_bwd")
            dkv.append((dk, dv, dc, dcq))
            dh = _in_bwd(dqg, lw["in"], ds, tm=tmb, name="b_in_bwd")
            big["in"] = _w_in_grad(sv["h_bf"], dqg, name="b_w_in_grad")
            if j == 0:
                hpb = _head_block_width(d // N_HEADS, BWD_HEAD_TILES) // (d // N_HEADS)
                dct = (dkv[0][2] + dkv[1][2])[:, :hpb, :].reshape(N_HEADS, tp)
                dcq = (dkv[0][3] + dkv[1][3])[:, :, :hpb]
                dct = dct + jnp.transpose(dcq, (0, 2, 1)).reshape(N_HEADS, tp)
                dct = jnp.concatenate([dct, jnp.zeros((LANES - N_HEADS, tp), F32)])
                df_bf, grads["kv_fb"] = _fgate_bwd(dct, kvs["fp"], wts["kv_fb"], tq=tq, name="fgate_bwd")
                dkvz = jnp.concatenate([_pair_sum(dkv[0][0], dkv[1][0], tm=tm, name="kv_pair_sum"),
                                        _pair_sum(dkv[0][1], dkv[1][1], tm=tm, name="kv_pair_sum"), df_bf], axis=1)
                dh = _mm_nt_full(dkvz, kvs["w"], dh, tm=tmb // 2, name="kv_in_bwd")
                big["kv_w"] = _mm_tn_cols(kvs["h_bf"], dkvz, tn=LANES, name="kv_w_grad")
        token = hooks.grads_ready(layer, "mix", big)
    return loss_tile, dh, grads


def _finish_small_grads(grads, d_h0, n_meta):
    r_dim = grads["a_vecs"][0].shape[1]
    per = _lru_block_cols(r_dim) // (r_dim // N_LRU_BLOCKS)
    a_cwb = jnp.stack(grads["a_cwb"])
    a_vecs = jnp.stack(grads["a_vecs"])
    f_cwb3 = jnp.stack(grads["f_cwb3"])
    ln1 = jnp.stack(grads["ln1_gb"])
    ln2 = jnp.stack(grads["ln2_gb"])
    f_rows = f_cwb3.transpose(0, 2, 1, 3).reshape(N_LAYERS, 8, -1)
    return {
        "meta": d_h0[:n_meta],
        "a_conv_w": a_cwb[:, :4], "a_conv_b": a_cwb[:, 4],
        "a_w_r": jnp.stack([_block_diag_extract(g, per) for g in grads["a_bd_r"]]),
        "a_b_r": a_vecs[:, 0],
        "a_w_i": jnp.stack([_block_diag_extract(g, per) for g in grads["a_bd_i"]]),
        "a_b_i": a_vecs[:, 1], "a_lambda": a_vecs[:, 2],
        "kv_f_b": grads["kv_fb"][0, :N_HEADS],
        "f_conv_w": f_rows[:, :3], "f_conv_b": f_rows[:, 3],
        "ln1_g": ln1[:, 0], "ln1_b": ln1[:, 1], "ln2_g": ln2[:, 0], "ln2_b": ln2[:, 1],
    }


def kernel(x, meta, a_w_in, a_conv_w, a_conv_b, a_w_r, a_b_r, a_w_i, a_b_i, a_lambda, a_w_out, kv_w, kv_f_b, b_w_in, b_w_out, f_w_in, f_conv_w, f_conv_b, f_w_out, ln1_g, ln1_b, ln2_g, ln2_b, loss_target, m_meta, m_a_w_in, m_a_conv_w, m_a_conv_b, m_a_w_r, m_a_b_r, m_a_w_i, m_a_b_i, m_a_lambda, m_a_w_out, m_kv_w, m_kv_f_b, m_b_w_in, m_b_w_out, m_f_w_in, m_f_conv_w, m_f_conv_b, m_f_w_out, m_ln1_g, m_ln1_b, m_ln2_g, m_ln2_b, v_meta, v_a_w_in, v_a_conv_w, v_a_conv_b, v_a_w_r, v_a_b_r, v_a_w_i, v_a_b_i, v_a_lambda, v_a_w_out, v_kv_w, v_kv_f_b, v_b_w_in, v_b_w_out, v_f_w_in, v_f_conv_w, v_f_conv_b, v_f_w_out, v_ln1_g, v_ln1_b, v_ln2_g, v_ln2_b):
    w = dict(meta=meta, a_w_in=a_w_in, a_conv_w=a_conv_w, a_conv_b=a_conv_b, a_w_r=a_w_r, a_b_r=a_b_r, a_w_i=a_w_i,
             a_b_i=a_b_i, a_lambda=a_lambda, a_w_out=a_w_out, kv_w=kv_w, kv_f_b=kv_f_b, b_w_in=b_w_in,
             b_w_out=b_w_out, f_w_in=f_w_in, f_conv_w=f_conv_w, f_conv_b=f_conv_b, f_w_out=f_w_out, ln1_g=ln1_g,
             ln1_b=ln1_b, ln2_g=ln2_g, ln2_b=ln2_b)
    m = dict(meta=m_meta, a_w_in=m_a_w_in, a_conv_w=m_a_conv_w, a_conv_b=m_a_conv_b, a_w_r=m_a_w_r, a_b_r=m_a_b_r,
             a_w_i=m_a_w_i, a_b_i=m_a_b_i, a_lambda=m_a_lambda, a_w_out=m_a_w_out, kv_w=m_kv_w, kv_f_b=m_kv_f_b,
             b_w_in=m_b_w_in, b_w_out=m_b_w_out, f_w_in=m_f_w_in, f_conv_w=m_f_conv_w, f_conv_b=m_f_conv_b,
             f_w_out=m_f_w_out, ln1_g=m_ln1_g, ln1_b=m_ln1_b, ln2_g=m_ln2_g, ln2_b=m_ln2_b)
    v = dict(meta=v_meta, a_w_in=v_a_w_in, a_conv_w=v_a_conv_w, a_conv_b=v_a_conv_b, a_w_r=v_a_w_r, a_b_r=v_a_b_r,
             a_w_i=v_a_w_i, a_b_i=v_a_b_i, a_lambda=v_a_lambda, a_w_out=v_a_w_out, kv_w=v_kv_w, kv_f_b=v_kv_f_b,
             b_w_in=v_b_w_in, b_w_out=v_b_w_out, f_w_in=v_f_w_in, f_conv_w=v_f_conv_w, f_conv_b=v_f_conv_b,
             f_w_out=v_f_w_out, ln1_g=v_ln1_g, ln1_b=v_ln1_b, ln2_g=v_ln2_g, ln2_b=v_ln2_b)
    shapes = {n: w[n].shape for n in WEIGHT_NAMES}

    me = jnp.reshape(_my_index(), (1,)).astype(jnp.int32)

    def as_stored(name, a):
        return jnp.swapaxes(a, 1, 2) if name == "f_w_in" else a

    param_of = {"in": ("a_w_in", "b_w_in"), "out": ("a_w_out", "b_w_out"), "f_in": ("f_w_in",) * 2,
                "f_out": ("f_w_out",) * 2}
    order = [("small", None, None)]
    for layer in range(N_LAYERS):
        if layer == N_A_LAYERS:
            order.append(("kv_w", layer, 0))
        for key in ("in", "out", "f_in", "f_out"):
            order.append((key, layer, layer if key[0] == "f" or layer < N_A_LAYERS else layer - N_A_LAYERS))
    def place(key, layer, idx):
        if key == "small":
            return _place_own(_pack([w[n] for n in COL_F32], F32)[None], 0, me, out_dtype=F32, name="place_small")
        if key == "kv_w":
            return _place_own(w["kv_w"][None], 0, me, out_dtype=BF16, name="place_kv_w")
        name = param_of[key][0 if layer < N_A_LAYERS else 1]
        return _place_own(as_stored(name, w[name]), idx, me, out_dtype=BF16, name=f"place_{name}_{idx}")

    lands = [place(*o) for o in order]
    gather_handles, gather_token = _split_start([([l], [l]) for l in lands], scatter=False, name="gather_start")
    group_of = {(key, layer): g for g, (key, layer, _) in enumerate(order)}
    (got_s,) = _split_wait(gather_handles[0], gather_token, scatter=False, name="gather_wait_small")
    small = {n: w[n] for n in REPLICATED}
    for n, part in zip(COL_F32, _unpack(got_s, [w[n].shape for n in COL_F32], 1)):
        small[n] = _cols_full(part)
    n_meta, d = small["meta"].shape

    class Hooks:
        pending = None
        received = {}
        sent = {}

        @staticmethod
        def weight(layer, key, after):
            (got,) = _split_wait(gather_handles[group_of[(key, layer)]], after, scatter=False,
                                 name=f"gather_wait_{key}_{layer}")
            return got

        @staticmethod
        def collect(after):
            if Hooks.pending is not None:
                tag, names, handle = Hooks.pending
                got = _split_wait(handle, after, scatter=True, name=f"scatter_wait_{tag}")
                Hooks.received.update(zip(names, got))
                Hooks.pending = None

        @staticmethod
        def grads_ready(layer, part, big):
            if "kv_w" in big:
                big["kv_w"] = _cols_split(big["kv_w"][:, :shapes["kv_w"][1] * N_DEV]).astype(BF16)
            names = [(key, layer) for key in big]
            send = [big[key] for key in big]
            Hooks.collect(send[0])
            empty = [lax.empty(s.shape, s.dtype) for s in send]
            handles, token = _split_start([(send, empty)], scatter=True, name=f"scatter_start_{part}_{layer}")
            Hooks.pending = (f"{part}_{layer}", names, handles[0])
            Hooks.sent.update(zip(names, handles[0][2]))
            return token[0, 0]

    Hooks.pending, Hooks.received, Hooks.sent = None, {}, {}

    n_tok = n_meta + x.shape[1]
    tp = -(-n_tok // ROW_ALIGN) * ROW_ALIGN
    pad = jnp.zeros((tp - n_tok, d), F32)
    h0 = jnp.concatenate([small["meta"], x[0], pad])
    tgt = jnp.concatenate([jnp.zeros((n_meta, d), F32), loss_target[0], pad])
    loss_tile, d_h0, grads = _local_step(h0, tgt, n_meta, n_tok, _small_layouts(small), Hooks)
    g_small = _finish_small_grads(grads, d_h0, n_meta)
    loss = lax.psum(loss_tile[0, 0], MESH_AXES)
    grad_x = d_h0[n_meta:n_tok][None]

    rep = _pack([g_small[n] for n in REPLICATED], F32, row_quantum=16 * N_DEV)
    send = [_pack8([_cols_split(g_small[n]) for n in COL_F32], F32), rep.reshape(N_DEV, -1, PACK_COLS)]
    lands = _own_blocks(send, name="scatter_own_small")
    handles, token = _split_start([(send, lands)], scatter=True, name="scatter_start_small")

    g, delta, new_m, new_v = {}, {}, {}, {}
    layers_of = {
        "a_w_in": [("in", l) for l in range(N_A_LAYERS)], "a_w_out": [("out", l) for l in range(N_A_LAYERS)],
        "b_w_in": [("in", l) for l in range(N_A_LAYERS, N_LAYERS)],
        "b_w_out": [("out", l) for l in range(N_A_LAYERS, N_LAYERS)],
        "f_w_in": [("f_in", l) for l in range(N_LAYERS)], "f_w_out": [("f_out", l) for l in range(N_LAYERS)],
        "kv_w": [("kv_w", N_A_LAYERS)],
    }
    ready = [n for n in BIG + ("kv_w",) if all(t in Hooks.received for t in layers_of[n])]

    def done(names):
        return jnp.stack([g[n][(0,) * g[n].ndim] for n in names])

    for n in ready + [n for n in BIG + ("kv_w",) if n not in ready]:
        if n not in ready and Hooks.pending is not None:
            Hooks.collect(done(ready))
        lift = (lambda a: a[None]) if n == "kv_w" else (lambda a, n=n: as_stored(n, a))
        outs = _sum_adamw([Hooks.received[t] for t in layers_of[n]], [Hooks.sent[t] for t in layers_of[n]], me,
                          lift(w[n]), lift(m[n]), lift(v[n]), name="sum_adamw_" + n)
        g[n], delta[n], new_m[n], new_v[n] = [as_stored(n, o).reshape(shapes[n]) for o in outs]
    recv_s, recv_r = _split_wait(handles[0], done(BIG + ("kv_w",)), scatter=True, name="scatter_wait_small")
    sum_s = _sum8(recv_s, name="sum_grads_f32")
    g.update(zip(COL_F32, _unpack(sum_s, [shapes[n] for n in COL_F32], 0)))
    (got_r,) = _all_gather([_sum8(recv_r, name="sum_grads_replicated")], name="gather_replicated_sums")
    g.update(zip(REPLICATED, _unpack(got_r.reshape(-1, PACK_COLS), [shapes[n] for n in REPLICATED], 0)))

    for n in COL_F32 + REPLICATED:
        shp = shapes[n]
        dl, nm, nv = _adamw(_slab2d(w[n]), _slab2d(g[n]), _slab2d(m[n]), _slab2d(v[n]), name="adamw")
        delta[n], new_m[n], new_v[n] = dl.reshape(shp), nm.reshape(shp), nv.reshape(shp)
    return (loss, grad_x, *[g[n] for n in WEIGHT_NAMES], *[delta[n] for n in WEIGHT_NAMES],
            *[new_m[n] for n in WEIGHT_NAMES], *[new_v[n] for n in WEIGHT_NAMES])
```

```python
import math

import jax
import jax.numpy as jnp
from jax import lax
from jax.experimental import pallas as pl
from jax.experimental.pallas import tpu as pltpu

F32 = jnp.float32
BF16 = jnp.bfloat16

N_DEV = 8
MESH_AXES = ("x", "y", "c")
N_LAYERS = 4
N_A_LAYERS = 2
N_LRU_BLOCKS = 16
N_HEADS = 16
LRU_C = 8.0
DN_ALPHA = (2 * N_LAYERS) ** 0.25
LN_EPS = 1e-5
ADAM_LR, ADAM_B1, ADAM_B2, ADAM_EPS, ADAM_WD, ADAM_STEP = 0.001, 0.9, 0.999, 1e-08, 0.01, 10

LANES = 128
SUBLANES = 8
ROW_ALIGN = 128
VMEM_LIMIT_BYTES = 56 * 1024 * 1024
GELU_K = math.sqrt(2.0 / math.pi)
GELU_C = 0.044715
PACK_COLS = 1024


def _params(*sem):
    return pltpu.CompilerParams(dimension_semantics=sem, vmem_limit_bytes=VMEM_LIMIT_BYTES)


def _gelu(x):
    th = jnp.tanh(GELU_K * (x + GELU_C * x * x * x))
    return 0.5 * x * (1.0 + th)


def _gelu_and_grad(x):
    x2 = x * x
    th = jnp.tanh(GELU_K * (x + GELU_C * x2 * x))
    g = 0.5 * x * (1.0 + th)
    dg = 0.5 * (1.0 + th) + 0.5 * x * (1.0 - th * th) * (GELU_K * (1.0 + 3.0 * GELU_C * x2))
    return g, dg


def _sigmoid(x):
    return 0.5 * jnp.tanh(0.5 * x) + 0.5


def _expm1(x):
    small = x * (1.0 + 0.5 * x * (1.0 + (1.0 / 3.0) * x * (1.0 + 0.25 * x)))
    return jnp.where(jnp.abs(x) < 1e-2, small, jnp.exp(x) - 1.0)


def _softplus(x):
    e = jnp.exp(-jnp.abs(x))
    small = e * (1.0 - 0.5 * e * (1.0 - (2.0 / 3.0) * e))
    return jnp.maximum(x, 0.0) + jnp.where(e < 1e-2, small, jnp.log(1.0 + e))


def _shift_down(x, s):
    if s == 0:
        return x
    rows = lax.broadcasted_iota(jnp.int32, x.shape, 0)
    return jnp.where(rows >= s, pltpu.roll(x, s, 0), 0.0)


def _shift_up(x, s):
    if s == 0:
        return x
    n = x.shape[0]
    rows = lax.broadcasted_iota(jnp.int32, x.shape, 0)
    return jnp.where(rows < n - s, pltpu.roll(x, n - s, 0), 0.0)


def _dot_nn(a, b):
    return lax.dot_general(a, b, (((1,), (0,)), ((), ())), preferred_element_type=F32)


def _dot_nt(a, b):
    return lax.dot_general(a, b, (((1,), (1,)), ((), ())), preferred_element_type=F32)


def _dot_tn(a, b):
    return lax.dot_general(a, b, (((0,), (0,)), ((), ())), preferred_element_type=F32)


def _rows8(vals, width):
    rows = lax.broadcasted_iota(jnp.int32, (8, width), 0)
    out = jnp.zeros((8, width), F32)
    for k, v in enumerate(vals):
        out = jnp.where(rows == k, jnp.broadcast_to(v, (8, width)), out)
    return out


def _tile(n, prefer):
    for c in prefer:
        if n % c == 0:
            return c
    return n


def _mm_nn(a, b, *, tn, out_dtype, name):
    m, k = a.shape
    n = b.shape[1]

    def body(a_ref, b_ref, o_ref):
        o_ref[...] = _dot_nn(a_ref[...], b_ref[...]).astype(o_ref.dtype)

    return pl.pallas_call(
        body, name=name, grid=(n // tn,),
        in_specs=[pl.BlockSpec((m, k), lambda j: (0, 0)), pl.BlockSpec((k, tn), lambda j: (0, j))],
        out_specs=pl.BlockSpec((m, tn), lambda j: (0, j)),
        out_shape=jax.ShapeDtypeStruct((m, n), out_dtype),
        compiler_params=_params("parallel"),
    )(a, b)


def _proj_in(h_bf, g_in, *, shard_major, name, transposed=False):
    t, k = h_bf.shape
    n = g_in.shape[1] if transposed else g_in.shape[2]

    def body(a_ref, b_ref, o_ref):
        o_ref[...] = _dot_nt(a_ref[...], b_ref[...]) if transposed else _dot_nn(a_ref[...], b_ref[...])

    if shard_major:
        out_spec = pl.BlockSpec((None, t, n), lambda j: (j, 0, 0))
        out_shape = jax.ShapeDtypeStruct((N_DEV, t, n), F32)
    else:
        out_spec = pl.BlockSpec((t, n), lambda j: (0, j))
        out_shape = jax.ShapeDtypeStruct((t, N_DEV * n), F32)
    return pl.pallas_call(
        body, name=name, grid=(N_DEV,),
        in_specs=[pl.BlockSpec((t, k), lambda j: (0, 0)),
                  pl.BlockSpec((None,) + g_in.shape[1:], lambda j: (j, 0, 0))],
        out_specs=out_spec, out_shape=out_shape,
        compiler_params=_params("parallel"),
    )(h_bf, g_in)


def _out_ln(y3, g_out, hin, g, b, *, n_valid, tm, name):
    nj, t, kj = y3.shape
    _, r, d = g_out.shape

    def body(y_ref, w_ref, hin_ref, g_ref, b_ref, s_ref, h_ref, hb_ref):
        w = w_ref[...].reshape(N_DEV * r, d)
        s = DN_ALPHA * hin_ref[...]
        for jj in range(nj):
            s = s + _dot_nn(y_ref[jj], w[jj * kj:(jj + 1) * kj])
        mu = jnp.mean(s, axis=-1, keepdims=True)
        xc = s - mu
        var = jnp.mean(xc * xc, axis=-1, keepdims=True)
        h = xc * lax.rsqrt(var + LN_EPS) * g_ref[...] + b_ref[...]
        s_ref[...] = s
        h_ref[...] = h
        rows = pl.program_id(0) * tm + lax.broadcasted_iota(jnp.int32, (tm, d), 0)
        hb_ref[...] = jnp.where(rows < n_valid, h, 0.0).astype(BF16)

    row = pl.BlockSpec((tm, d), lambda i: (i, 0))
    vec = pl.BlockSpec((1, d), lambda i: (0, 0))
    return pl.pallas_call(
        body, name=name, grid=(t // tm,),
        in_specs=[pl.BlockSpec((nj, tm, kj), lambda i: (0, i, 0)),
                  pl.BlockSpec((N_DEV, r, d), lambda i: (0, 0, 0)), row, vec, vec],
        out_specs=[row, row, row],
        out_shape=[jax.ShapeDtypeStruct((t, d), F32), jax.ShapeDtypeStruct((t, d), F32),
                   jax.ShapeDtypeStruct((t, d), BF16)],
        compiler_params=_params("parallel"),
    )(y3, g_out, hin, g, b)


def _out_bwd(ds_bf, g_out, *, tm, name):
    t, d = ds_bf.shape
    r = g_out.shape[1]

    def body(a_ref, w_ref, o_ref):
        o_ref[...] = _dot_nt(a_ref[...], w_ref[...].reshape(N_DEV * r, d))

    return pl.pallas_call(
        body, name=name, grid=(t // tm,),
        in_specs=[pl.BlockSpec((tm, d), lambda i: (i, 0)),
                  pl.BlockSpec((N_DEV, r, d), lambda i: (0, 0, 0))],
        out_specs=pl.BlockSpec((tm, N_DEV * r), lambda i: (i, 0)),
        out_shape=jax.ShapeDtypeStruct((t, N_DEV * r), F32),
        compiler_params=_params("parallel"),
    )(ds_bf, g_out)


def _in_bwd(dact, g_in, add, *, tm, name, alpha=DN_ALPHA, transposed=False):
    t = dact.shape[-2]
    _, k, n = g_in.shape
    if transposed:
        k, n = n, k
    halves = dact.shape[0] == 2 and dact.ndim == 3
    per = N_DEV // 2

    def body(a_ref, b_ref, add_ref, o_ref, acc_ref):
        j = pl.program_id(1)

        @pl.when(j == 0)
        def _():
            acc_ref[...] = alpha * add_ref[...]

        acc_ref[...] += _dot_nn(a_ref[...], b_ref[...]) if transposed else _dot_nt(a_ref[...], b_ref[...])

        @pl.when(j == N_DEV - 1)
        def _():
            o_ref[...] = acc_ref[...]

    if halves:
        a_spec = pl.BlockSpec((None, tm, n), lambda i, j: (j // per, i, j % per))
    elif dact.ndim == 4:
        a_spec = pl.BlockSpec((None, None, tm, n), lambda i, j: (j // per, j % per, i, 0))
    else:
        a_spec = pl.BlockSpec((None, tm, n), lambda i, j: (j, i, 0))
    return pl.pallas_call(
        body, name=name, grid=(t // tm, N_DEV),
        in_specs=[a_spec, pl.BlockSpec((None,) + g_in.shape[1:], lambda i, j: (j, 0, 0)),
                  pl.BlockSpec((tm, k), lambda i, j: (i, 0))],
        out_specs=pl.BlockSpec((tm, k), lambda i, j: (i, 0)),
        out_shape=jax.ShapeDtypeStruct((t, k), F32),
        scratch_shapes=[pltpu.VMEM((tm, k), F32)],
        compiler_params=_params("parallel", "arbitrary"),
    )(dact, g_in, add)


def _mm_nt_full(a, b, add, *, tm, name):
    t, n = a.shape
    k = b.shape[0]

    def body(a_ref, b_ref, add_ref, o_ref):
        o_ref[...] = add_ref[...] + _dot_nt(a_ref[...], b_ref[...])

    return pl.pallas_call(
        body, name=name, grid=(t // tm,),
        in_specs=[pl.BlockSpec((tm, n), lambda i: (i, 0)), pl.BlockSpec((k, n), lambda i: (0, 0)),
                  pl.BlockSpec((tm, k), lambda i: (i, 0))],
        out_specs=pl.BlockSpec((tm, k), lambda i: (i, 0)),
        out_shape=jax.ShapeDtypeStruct((t, k), F32),
        compiler_params=_params("parallel"),
    )(a, b, add)


def _w_in_grad(h_bf, dact, *, name, transposed=False):
    t, k = h_bf.shape
    halves = dact.shape[0] == 2 and dact.ndim == 3
    per = N_DEV // 2
    n = dact.shape[-1] // per if halves else dact.shape[-1]

    def body(a_ref, b_ref, o_ref):
        if transposed:
            o_ref[...] = _dot_tn(b_ref[...], a_ref[...]).astype(BF16)
        else:
            o_ref[...] = _dot_tn(a_ref[...], b_ref[...]).astype(BF16)

    if halves:
        b_spec = pl.BlockSpec((None, t, n), lambda j: (j // per, 0, j % per))
    elif dact.ndim == 4:
        b_spec = pl.BlockSpec((None, None, t, n), lambda j: (j // per, j % per, 0, 0))
    else:
        b_spec = pl.BlockSpec((None, t, n), lambda j: (j, 0, 0))
    return pl.pallas_call(
        body, name=name, grid=(N_DEV,),
        in_specs=[pl.BlockSpec((t, k), lambda j: (0, 0)), b_spec],
        out_specs=pl.BlockSpec((None, n, k) if transposed else (None, k, n), lambda j: (j, 0, 0)),
        out_shape=jax.ShapeDtypeStruct((N_DEV, n, k) if transposed else (N_DEV, k, n), BF16),
        compiler_params=_params("parallel"),
    )(h_bf, dact)


def _w_out_grad(y3, ds_bf, r, *, name):
    nj, t, kj = y3.shape
    d = ds_bf.shape[1]
    unit = r * LANES // math.gcd(r, LANES)
    ks = max([c for c in range(unit, min(kj, 768) + 1, unit) if kj % c == 0], default=kj)
    gsz = ks // r
    per = kj // ks

    def body(a_ref, b_ref, o_ref):
        o_ref[...] = _dot_tn(a_ref[...], b_ref[...]).reshape(gsz, r, d).astype(BF16)

    return pl.pallas_call(
        body, name=name, grid=(nj * per,),
        in_specs=[pl.BlockSpec((None, t, ks), lambda j: (j // per, 0, j % per)),
                  pl.BlockSpec((t, d), lambda j: (0, 0))],
        out_specs=pl.BlockSpec((gsz, r, d), lambda j: (j, 0, 0)),
        out_shape=jax.ShapeDtypeStruct((N_DEV, r, d), BF16),
        compiler_params=_params("parallel"),
    )(y3, ds_bf)


def _mm_tn_cols(a, b, *, tn, name):
    t, m = a.shape
    n = b.shape[1]

    def body(a_ref, b_ref, o_ref):
        o_ref[...] = _dot_tn(a_ref[...], b_ref[...])

    return pl.pallas_call(
        body, name=name, grid=(n // tn,),
        in_specs=[pl.BlockSpec((t, m), lambda j: (0, 0)), pl.BlockSpec((t, tn), lambda j: (0, j))],
        out_specs=pl.BlockSpec((m, tn), lambda j: (0, j)),
        out_shape=jax.ShapeDtypeStruct((m, n), F32),
        compiler_params=_params("parallel"),
    )(a, b)


def _ln_bwd(dout, s, g, *, tm, name):
    t, d = s.shape

    def body(do_ref, s_ref, g_ref, ds_ref, dsb_ref, gb_ref):
        i = pl.program_id(0)
        sv = s_ref[...]
        do = do_ref[...]
        mu = jnp.mean(sv, axis=-1, keepdims=True)
        xc = sv - mu
        var = jnp.mean(xc * xc, axis=-1, keepdims=True)
        rstd = lax.rsqrt(var + LN_EPS)
        xhat = xc * rstd
        dxhat = do * g_ref[...]
        m1 = jnp.mean(dxhat, axis=-1, keepdims=True)
        m2 = jnp.mean(dxhat * xhat, axis=-1, keepdims=True)
        ds = rstd * (dxhat - m1 - xhat * m2)
        ds_ref[...] = ds
        dsb_ref[...] = ds.astype(BF16)
        upd = _rows8([jnp.sum(do * xhat, axis=0, keepdims=True), jnp.sum(do, axis=0, keepdims=True)], d)

        @pl.when(i == 0)
        def _():
            gb_ref[...] = upd

        @pl.when(i > 0)
        def _():
            gb_ref[...] += upd

    row = pl.BlockSpec((tm, d), lambda i: (i, 0))
    return pl.pallas_call(
        body, name=name, grid=(t // tm,),
        in_specs=[row, row, pl.BlockSpec((1, d), lambda i: (0, 0))],
        out_specs=[row, row, pl.BlockSpec((8, d), lambda i: (0, 0))],
        out_shape=[jax.ShapeDtypeStruct((t, d), F32), jax.ShapeDtypeStruct((t, d), BF16),
                   jax.ShapeDtypeStruct((8, d), F32)],
        compiler_params=_params("arbitrary"),
    )(dout, s, g)


def _roll_down(x, s):
    return x if s == 0 else pltpu.roll(x, s, 0)


def _conv_taps(x, wb, width):
    y = jnp.broadcast_to(wb[width:width + 1, :], x.shape)
    for k in range(width):
        y = y + _roll_down(x, width - 1 - k) * wb[k:k + 1, :]
    return y


def _conv_taps_bwd(dy, x, wb, width):
    n = dy.shape[0]
    dx = jnp.zeros_like(dy)
    rows = []
    for k in range(width):
        s = width - 1 - k
        dy_up = dy if s == 0 else pltpu.roll(dy, n - s, 0)
        dx = dx + dy_up * wb[k:k + 1, :]
        rows.append(jnp.sum(dy_up * x, axis=0, keepdims=True))
    rows.append(jnp.sum(dy, axis=0, keepdims=True))
    t_idx = lax.broadcasted_iota(jnp.int32, dy.shape, 0)
    return jnp.where(t_idx < n - (width - 1), dx, 0.0), _rows8(rows, dy.shape[1])


def _convglu_fwd(z3, fwb3, *, name):
    _, t, n = z3.shape
    half = N_DEV // 2
    nc = pl.cdiv(n, LANES)

    def body(zg_ref, zv_ref, wg_ref, wv_ref, y_ref):
        gate = _conv_taps(zg_ref[...], wg_ref[...], 3)
        val = _conv_taps(zv_ref[...], wv_ref[...], 3)
        y_ref[...] = (_gelu(gate) * val).astype(BF16)

    zblk = lambda off: pl.BlockSpec((None, t, LANES), lambda j, c: (j + off, 0, c))
    wblk = lambda off: pl.BlockSpec((None, 8, LANES), lambda j, c: (j + off, 0, c))
    return pl.pallas_call(
        body, name=name, grid=(half, nc),
        in_specs=[zblk(0), zblk(half), wblk(0), wblk(half)],
        out_specs=zblk(0),
        out_shape=jax.ShapeDtypeStruct((half, t, n), BF16),
        compiler_params=_params("parallel", "parallel"),
    )(z3, z3, fwb3, fwb3)


def _ffn_bwd_mid(ds_bf, g_out, z3, fwb3, *, name):
    t, d = ds_bf.shape
    r = g_out.shape[1]
    n = z3.shape[2]
    half = N_DEV // 2
    nc = pl.cdiv(n, LANES)
    assert n == 2 * r

    def body(ds_ref, w_ref, zg_ref, zv_ref, wg_ref, wv_ref, dz_ref, dwb_ref, wsc_ref):
        c = pl.program_id(1)

        @pl.when(c == 0)
        def _():
            wsc_ref[0:r, :] = w_ref[0]
            wsc_ref[r:2 * r, :] = w_ref[1]
            if nc * LANES > n:
                wsc_ref[n:nc * LANES, :] = jnp.zeros((nc * LANES - n, d), BF16)

        w = wsc_ref[pl.ds(pl.multiple_of(c * LANES, LANES), LANES), :]
        dyf = _dot_nt(ds_ref[...], w)
        zg, zv = zg_ref[...], zv_ref[...]
        wg, wv = wg_ref[...], wv_ref[...]
        gate = _conv_taps(zg, wg, 3)
        val = _conv_taps(zv, wv, 3)
        gl, dgl = _gelu_and_grad(gate)
        dzg, dwg = _conv_taps_bwd(dyf * val * dgl, zg, wg, 3)
        dzv, dwv = _conv_taps_bwd(dyf * gl, zv, wv, 3)
        dz_ref[0] = dzg.astype(BF16)
        dz_ref[1] = dzv.astype(BF16)
        dwb_ref[0] = dwg
        dwb_ref[1] = dwv

    zblk = lambda off: pl.BlockSpec((None, t, LANES), lambda j, c: (j + off, 0, c))
    wblk = lambda off: pl.BlockSpec((None, 8, LANES), lambda j, c: (j + off, 0, c))
    return pl.pallas_call(
        body, name=name, grid=(half, nc),
        in_specs=[pl.BlockSpec((t, d), lambda j, c: (0, 0)),
                  pl.BlockSpec((2, r, d), lambda j, c: (j, 0, 0)),
                  zblk(0), zblk(half), wblk(0), wblk(half)],
        out_specs=[pl.BlockSpec((2, None, t, LANES), lambda j, c: (0, j, 0, c)),
                   pl.BlockSpec((2, None, 8, LANES), lambda j, c: (0, j, 0, c))],
        out_shape=[jax.ShapeDtypeStruct((2, half, t, n), BF16), jax.ShapeDtypeStruct((2, half, 8, n), F32)],
        scratch_shapes=[pltpu.VMEM((nc * LANES, d), BF16)],
        compiler_params=_params("parallel", "arbitrary"),
    )(ds_bf, g_out, z3, z3, fwb3, fwb3)


def _conv_a_fwd(gr, cwb, *, cb, name):
    t, r2 = gr.shape
    r = r2 // 2
    nb = r // cb

    def body(x_ref, w_ref, o_ref):
        o_ref[...] = _conv_taps(x_ref[...], w_ref[...], 4)

    return pl.pallas_call(
        body, name=name, grid=(nb,),
        in_specs=[pl.BlockSpec((t, cb), lambda j: (0, j + nb)), pl.BlockSpec((8, cb), lambda j: (0, j))],
        out_specs=pl.BlockSpec((t, cb), lambda j: (0, j)),
        out_shape=jax.ShapeDtypeStruct((t, r), F32),
        compiler_params=_params("parallel"),
    )(gr, cwb)


def _gates_fwd(rec, bd_r, bd_i, vecs, *, tm, name):
    t, r_dim = rec.shape
    nb, cb, _ = bd_r.shape

    def body(x_ref, wr_ref, wi_ref, v_ref, a_ref, u_ref, r_ref, i_ref):
        x = x_ref[...]
        xb = x.astype(BF16)
        v = v_ref[...]
        r = _sigmoid(_dot_nn(xb, wr_ref[...]) + v[0:1, :])
        i = _sigmoid(_dot_nn(xb, wi_ref[...]) + v[1:2, :])
        log_a = (-LRU_C) * r * _softplus(-v[2:3, :])
        a_ref[...] = jnp.exp(log_a)
        u_ref[...] = jnp.sqrt(-_expm1(2.0 * log_a)) * (i * x)
        r_ref[...] = r
        i_ref[...] = i

    blk = pl.BlockSpec((tm, cb), lambda j, i: (i, j))
    wspec = pl.BlockSpec((None, cb, cb), lambda j, i: (j, 0, 0))
    out = jax.ShapeDtypeStruct((t, r_dim), F32)
    return pl.pallas_call(
        body, name=name, grid=(nb, t // tm),
        in_specs=[blk, wspec, wspec, pl.BlockSpec((8, cb), lambda j, i: (0, j))],
        out_specs=[blk, blk, blk, blk],
        out_shape=[out, out, out, out],
        compiler_params=_params("parallel", "parallel"),
    )(rec, bd_r, bd_i, vecs)


def _scan_fwd(a, u, gr, *, cb, name):
    t, r = a.shape
    nb = r // cb
    seg = t // SUBLANES

    def body(a_ref, u_ref, g_ref, h_ref, y_ref, p_ref):
        def step(k, carry):
            h, p = carry
            rows = pl.ds(k, SUBLANES, stride=seg)
            av = a_ref[rows, :]
            h = av * h + u_ref[rows, :]
            p = av * p
            h_ref[rows, :] = h
            p_ref[rows, :] = p
            return h, p

        h_fin, p_fin = lax.fori_loop(0, seg, step, (jnp.zeros((SUBLANES, cb), F32), jnp.ones((SUBLANES, cb), F32)),
                                     unroll=8)
        carry = h_fin[0:1, :]
        for s in range(1, SUBLANES):
            rows = slice(s * seg, (s + 1) * seg)
            h_ref[rows, :] = h_ref[rows, :] + p_ref[rows, :] * carry
            carry = h_fin[s:s + 1, :] + p_fin[s:s + 1, :] * carry
        y_ref[...] = (_gelu(g_ref[...]) * h_ref[...]).astype(BF16)

    blk = pl.BlockSpec((t, cb), lambda j: (0, j))
    return pl.pallas_call(
        body, name=name, grid=(nb,),
        in_specs=[blk, blk, blk],
        out_specs=[blk, pl.BlockSpec((None, t, cb), lambda j: (0, 0, j))],
        out_shape=[jax.ShapeDtypeStruct((t, r), F32), jax.ShapeDtypeStruct((1, t, r), BF16)],
        scratch_shapes=[pltpu.VMEM((t, cb), F32)],
        compiler_params=_params("parallel"),
    )(a, u, gr)


def _scan_bwd(dy, gr, hr, a, *, cb, name):
    t, r = a.shape
    nb = r // cb
    seg = t // SUBLANES

    def body(dy_ref, g_ref, h_ref, a_ref, dh_ref, da_ref, dg_ref, q_ref):
        gl, dgl = _gelu_and_grad(g_ref[...])
        dyv = dy_ref[...]
        dh_ref[...] = dyv * gl
        dg_ref[...] = (dyv * h_ref[...] * dgl).astype(BF16)

        def step(k, carry):
            cin, q = carry
            rows = pl.ds(seg - 1 - k, SUBLANES, stride=seg)
            dh = dh_ref[rows, :] + cin
            dh_ref[rows, :] = dh
            q_ref[rows, :] = q
            av = a_ref[rows, :]
            return av * dh, av * q

        c_fin, q_fin = lax.fori_loop(0, seg, step, (jnp.zeros((SUBLANES, cb), F32), jnp.ones((SUBLANES, cb), F32)),
                                     unroll=8)
        carry = c_fin[SUBLANES - 1:SUBLANES, :]
        for s in range(SUBLANES - 2, -1, -1):
            rows = slice(s * seg, (s + 1) * seg)
            dh_ref[rows, :] = dh_ref[rows, :] + q_ref[rows, :] * carry
            carry = c_fin[s:s + 1, :] + q_fin[s:s + 1, :] * carry
        da_ref[...] = dh_ref[...] * _shift_down(h_ref[...], 1)

    blk = pl.BlockSpec((t, cb), lambda j: (0, j))
    return pl.pallas_call(
        body, name=name, grid=(nb,),
        in_specs=[blk, blk, blk, blk],
        out_specs=[blk, blk, blk],
        out_shape=[jax.ShapeDtypeStruct((t, r), F32), jax.ShapeDtypeStruct((t, r), F32),
                   jax.ShapeDtypeStruct((t, r), BF16)],
        scratch_shapes=[pltpu.VMEM((t, cb), F32)],
        compiler_params=_params("parallel"),
    )(dy, gr, hr, a)


def _gates_bwd(rec, r, i, a, dh, da, bd_r, bd_i, vecs, *, tm, name):
    t, r_dim = rec.shape
    nb, cb, _ = bd_r.shape

    def body(x_ref, r_ref, i_ref, a_ref, dh_ref, da_ref, wr_ref, wi_ref, v_ref, dx_ref, dpr_ref, dpi_ref, dv_ref):
        step = pl.program_id(1)
        x, r, i, a, dh, da = x_ref[...], r_ref[...], i_ref[...], a_ref[...], dh_ref[...], da_ref[...]
        lam = v_ref[...][2:3, :]
        sp = _softplus(-lam)
        a2 = a * a
        mult = jnp.sqrt(-_expm1(2.0 * (-LRU_C) * r * sp))
        d_i = dh * mult * x
        d_log_a = da * a - (dh * i * x) * a2 / mult
        d_r = d_log_a * ((-LRU_C) * sp)
        d_sp = jnp.sum(d_log_a * ((-LRU_C) * r), axis=0, keepdims=True)
        d_pre_r = d_r * r * (1.0 - r)
        d_pre_i = d_i * i * (1.0 - i)
        dprb = d_pre_r.astype(BF16)
        dpib = d_pre_i.astype(BF16)
        dx_ref[...] = dh * mult * i + _dot_nt(dprb, wr_ref[...]) + _dot_nt(dpib, wi_ref[...])
        dpr_ref[...] = dprb
        dpi_ref[...] = dpib
        upd = _rows8([jnp.sum(d_pre_r, axis=0, keepdims=True), jnp.sum(d_pre_i, axis=0, keepdims=True),
                      -d_sp * _sigmoid(-lam)], cb)

        @pl.when(step == 0)
        def _():
            dv_ref[...] = upd

        @pl.when(step > 0)
        def _():
            dv_ref[...] += upd

    blk = pl.BlockSpec((tm, cb), lambda j, i: (i, j))
    wspec = pl.BlockSpec((None, cb, cb), lambda j, i: (j, 0, 0))
    vspec = pl.BlockSpec((8, cb), lambda j, i: (0, j))
    return pl.pallas_call(
        body, name=name, grid=(nb, t // tm),
        in_specs=[blk] * 6 + [wspec, wspec, vspec],
        out_specs=[blk, blk, blk, vspec],
        out_shape=[jax.ShapeDtypeStruct((t, r_dim), F32), jax.ShapeDtypeStruct((t, r_dim), BF16),
                   jax.ShapeDtypeStruct((t, r_dim), BF16), jax.ShapeDtypeStruct((8, r_dim), F32)],
        compiler_params=_params("parallel", "arbitrary"),
    )(rec, r, i, a, dh, da, bd_r, bd_i, vecs)


def _bd_grad(rec, dpr, dpi, *, cb, name):
    t, r = rec.shape
    nb = r // cb

    def body(x_ref, dr_ref, di_ref, gr_ref, gi_ref):
        xb = x_ref[...].astype(BF16)
        gr_ref[...] = _dot_tn(xb, dr_ref[...])
        gi_ref[...] = _dot_tn(xb, di_ref[...])

    blk = pl.BlockSpec((t, cb), lambda j: (0, j))
    wspec = pl.BlockSpec((None, cb, cb), lambda j: (j, 0, 0))
    out = jax.ShapeDtypeStruct((nb, cb, cb), F32)
    return pl.pallas_call(
        body, name=name, grid=(nb,),
        in_specs=[blk, blk, blk], out_specs=[wspec, wspec], out_shape=[out, out],
        compiler_params=_params("parallel"),
    )(rec, dpr, dpi)


def _conv_a_bwd(d_rec, gr, dgate, cwb, *, cb, name):
    t, r = d_rec.shape
    nb = r // cb

    def body(dy_ref, x_ref, dg_ref, w_ref, dact_ref, dw_ref):
        dx, dw = _conv_taps_bwd(dy_ref[...], x_ref[...], w_ref[...], 4)
        dact_ref[0] = dg_ref[...]
        dact_ref[1] = dx.astype(BF16)
        dw_ref[...] = dw

    blk = pl.BlockSpec((t, cb), lambda j: (0, j))
    vspec = pl.BlockSpec((8, cb), lambda j: (0, j))
    return pl.pallas_call(
        body, name=name, grid=(nb,),
        in_specs=[blk, pl.BlockSpec((t, cb), lambda j: (0, j + nb)), blk, vspec],
        out_specs=[pl.BlockSpec((2, t, cb), lambda j: (0, 0, j)), vspec],
        out_shape=[jax.ShapeDtypeStruct((2, t, r), BF16), jax.ShapeDtypeStruct((8, r), F32)],
        compiler_params=_params("parallel"),
    )(d_rec, gr, dgate, cwb)


def _split3(x):
    p0 = x.astype(BF16)
    r1 = x - p0.astype(F32)
    p1 = r1.astype(BF16)
    p2 = (r1 - p1.astype(F32)).astype(BF16)
    return p0, p1, p2


def _fgate_fwd(fp, fb, *, tq, name):
    t = fp.shape[0]

    def body(f_ref, b_ref, c_ref, ct_ref):
        logf = -_softplus(-(f_ref[...] + b_ref[...]))
        rows = pl.program_id(0) * tq + lax.broadcasted_iota(jnp.int32, (tq, t), 0)
        cols = lax.broadcasted_iota(jnp.int32, (tq, t), 1)
        tri = (cols <= rows).astype(BF16)
        p0, p1, p2 = _split3(logf)
        c = _dot_nn(tri, p0) + _dot_nn(tri, p1) + _dot_nn(tri, p2)
        c_ref[...] = c
        ct_ref[...] = c.T

    return pl.pallas_call(
        body, name=name, grid=(t // tq,),
        in_specs=[pl.BlockSpec((t, LANES), lambda i: (0, 0)), pl.BlockSpec((1, LANES), lambda i: (0, 0))],
        out_specs=[pl.BlockSpec((tq, LANES), lambda i: (i, 0)), pl.BlockSpec((LANES, tq), lambda i: (0, i))],
        out_shape=[jax.ShapeDtypeStruct((t, LANES), F32), jax.ShapeDtypeStruct((LANES, t), F32)],
        compiler_params=_params("parallel"),
    )(fp, fb)


def _fgate_bwd(dct, fp, fb, *, tq, name):
    t = fp.shape[0]

    def body(d_ref, f_ref, b_ref, o_ref, db_ref):
        i = pl.program_id(0)
        rows = lax.broadcasted_iota(jnp.int32, (t, tq), 0)
        cols = i * tq + lax.broadcasted_iota(jnp.int32, (t, tq), 1)
        tri = (rows >= cols).astype(BF16)
        p0, p1, p2 = _split3(d_ref[...])
        dlogf = (_dot_nn(p0, tri) + _dot_nn(p1, tri) + _dot_nn(p2, tri)).T
        df = dlogf * _sigmoid(-(f_ref[...] + b_ref[...]))
        o_ref[...] = df.astype(BF16)
        upd = _rows8([jnp.sum(df, axis=0, keepdims=True)], LANES)

        @pl.when(i == 0)
        def _():
            db_ref[...] = upd

        @pl.when(i > 0)
        def _():
            db_ref[...] += upd

    return pl.pallas_call(
        body, name=name, grid=(t // tq,),
        in_specs=[pl.BlockSpec((LANES, t), lambda i: (0, 0)), pl.BlockSpec((tq, LANES), lambda i: (i, 0)),
                  pl.BlockSpec((1, LANES), lambda i: (0, 0))],
        out_specs=[pl.BlockSpec((tq, LANES), lambda i: (i, 0)), pl.BlockSpec((8, LANES), lambda i: (0, 0))],
        out_shape=[jax.ShapeDtypeStruct((t, LANES), BF16), jax.ShapeDtypeStruct((8, LANES), F32)],
        compiler_params=_params("arbitrary"),
    )(dct, fp, fb)


def _pair_sum(a, b, *, tm, name):
    t, d = a.shape

    def body(a_ref, b_ref, o_ref):
        o_ref[...] = (a_ref[...] + b_ref[...]).astype(BF16)

    row = pl.BlockSpec((tm, d), lambda i: (i, 0))
    return pl.pallas_call(
        body, name=name, grid=(t // tm,), in_specs=[row, row], out_specs=row,
        out_shape=jax.ShapeDtypeStruct((t, d), BF16), compiler_params=_params("parallel"),
    )(a, b)


FWD_HEAD_TILES = 2
BWD_HEAD_TILES = 1


def _head_block_width(dh, tiles):
    return tiles * LANES if tiles * LANES // dh <= 8 else LANES


def _head_masks(dh, bw):
    lane = lax.broadcasted_iota(jnp.int32, (1, bw), 1)
    return [((lane >= e * dh) & (lane < (e + 1) * dh)) for e in range(bw // dh)]


def _head_c_row(ct_blk, head):
    sub = lax.broadcasted_iota(jnp.int32, ct_blk.shape, 0)
    return jnp.sum(jnp.where(sub == head, ct_blk, 0.0), axis=0, keepdims=True)


def _attn_weights(qm, k, c_row, q0):
    tq, t = qm.shape[0], k.shape[0]
    s = _dot_nt(qm, k) - c_row
    qi = q0 + lax.broadcasted_iota(jnp.int32, (tq, t), 0)
    ki = lax.broadcasted_iota(jnp.int32, (tq, t), 1)
    s = jnp.where(ki <= qi, s, -jnp.inf)
    m = jnp.max(s, axis=-1, keepdims=True)
    e = jnp.exp(s - m)
    return e, m, 1.0 / jnp.sum(e, axis=-1, keepdims=True)


def _key_buckets(t, tq):
    return tuple(sorted({min(-(-(i * tq) // LANES) * LANES, t) for i in range(1, t // tq + 1)}))


def _for_prefix(needed, buckets, fn):
    prev = 0
    for length in buckets:
        pl.when((needed > prev) & (needed <= length))(lambda length=length: fn(length))
        prev = length


def _attn_fwd(qg, kv, ct, *, tq, name):
    t, d2 = qg.shape
    d = d2 // 2
    dh = d // N_HEADS
    bw = _head_block_width(dh, FWD_HEAD_TILES)
    hpb = bw // dh
    nhb = d // bw
    scale = dh ** -0.5
    buckets = _key_buckets(t, tq)

    def body(q_ref, og_ref, k_ref, v_ref, ct_ref, o_ref, y_ref, st_ref):
        hb = pl.program_id(0)
        q0 = pl.program_id(1) * tq

        def run(length):
            qs = q_ref[...] * scale
            k = k_ref[0:length, :]
            v = v_ref[0:length, :]
            o = jnp.zeros((tq, bw), F32)
            lane = lax.broadcasted_iota(jnp.int32, (tq, LANES), 1)
            stats = jnp.zeros((tq, LANES), F32)
            for e, msk in enumerate(_head_masks(dh, bw)):
                c_row = _head_c_row(ct_ref[:, 0:length], hb * hpb + e)
                w, m, inv = _attn_weights(jnp.where(msk, qs, 0.0).astype(BF16), k, c_row, q0)
                o = o + _dot_nn(w.astype(BF16), jnp.where(msk, v, jnp.zeros_like(v))) * inv
                stats = jnp.where(lane == e, m, jnp.where(lane == hpb + e, inv, stats))
            o_ref[...] = o
            y_ref[...] = (o * _sigmoid(og_ref[...])).astype(BF16)
            st_ref[...] = stats

        _for_prefix(q0 + tq, buckets, run)

    qblk = pl.BlockSpec((tq, bw), lambda h, i: (i, h))
    return pl.pallas_call(
        body, name=name, grid=(nhb, t // tq),
        in_specs=[qblk, pl.BlockSpec((tq, bw), lambda h, i: (i, h + nhb)),
                  pl.BlockSpec((t, bw), lambda h, i: (0, h)), pl.BlockSpec((t, bw), lambda h, i: (0, h + nhb)),
                  pl.BlockSpec((N_HEADS, t), lambda h, i: (0, 0))],
        out_specs=[qblk, pl.BlockSpec((None, tq, bw), lambda h, i: (0, i, h)),
                   pl.BlockSpec((None, tq, LANES), lambda h, i: (h, i, 0))],
        out_shape=[jax.ShapeDtypeStruct((t, d), F32), jax.ShapeDtypeStruct((1, t, d), BF16),
                   jax.ShapeDtypeStruct((nhb, t, LANES), F32)],
        compiler_params=_params("parallel", "parallel"),
    )(qg, qg, kv, kv, ct)


def _attn_bwd(dy, qg, o, stats, kv, ct, *, tq, name):
    t, d2 = qg.shape
    d = d2 // 2
    dh = d // N_HEADS
    bw = _head_block_width(dh, BWD_HEAD_TILES)
    hpb = bw // dh
    nhb = d // bw
    scale = dh ** -0.5
    n_q = t // tq
    hpb_f = _head_block_width(dh, FWD_HEAD_TILES) // dh
    ratio = hpb_f // hpb
    chunk = 8 * LANES

    def body(dy_ref, q_ref, og_ref, o_ref, st_ref, k_ref, v_ref, ct_ref, dqg_ref, dk_ref, dv_ref, dc_ref, dcq_ref):
        hb = pl.program_id(0)
        step = pl.program_id(1)

        @pl.when(step == 0)
        def _():
            dk_ref[...] = jnp.zeros((t, bw), F32)
            dv_ref[...] = jnp.zeros((t, bw), F32)
            dc_ref[...] = jnp.zeros((8, t), F32)

        def run(i):
            q0 = i * tq
            length = min(-(-(q0 + tq) // LANES) * LANES, t)
            qs = q_ref[...] * scale
            sg = _sigmoid(og_ref[...])
            dyv = dy_ref[...]
            ov = o_ref[...]
            do = dyv * sg
            dqg_ref[1] = (dyv * ov * sg * (1.0 - sg)).astype(BF16)
            lane = lax.broadcasted_iota(jnp.int32, (tq, LANES), 1)
            stats = st_ref[...]
            masks = _head_masks(dh, bw)
            heads = []
            for e, msk in enumerate(masks):
                pos = (hb % ratio) * hpb + e
                m = jnp.sum(jnp.where(lane == pos, stats, 0.0), axis=1, keepdims=True)
                inv = jnp.sum(jnp.where(lane == hpb_f + pos, stats, 0.0), axis=1, keepdims=True)
                delta = jnp.sum(jnp.where(msk, do * ov, 0.0), axis=1, keepdims=True)
                heads.append((msk, m, inv, delta, jnp.where(msk, qs, 0.0).astype(BF16),
                              jnp.where(msk, do, 0.0).astype(BF16)))
            dq = jnp.zeros((tq, bw), F32)
            dcq = jnp.zeros((tq, LANES), F32)
            row_acc = [jnp.zeros((tq, chunk), F32) for _ in heads]
            for c0 in range(0, length, chunk):
                ch = min(chunk, length - c0)
                k = k_ref[c0:c0 + ch, :]
                v = v_ref[c0:c0 + ch, :]
                dk = jnp.zeros((ch, bw), F32)
                dv = jnp.zeros((ch, bw), F32)
                dc_rows = []
                for e, (msk, m, inv, delta, qm, dom) in enumerate(heads):
                    c_row = _head_c_row(ct_ref[:, c0:c0 + ch], hb * hpb + e)
                    s = _dot_nt(qm, k) - c_row
                    if c0 + ch - 1 > q0:
                        qi = q0 + lax.broadcasted_iota(jnp.int32, (tq, ch), 0)
                        ki = c0 + lax.broadcasted_iota(jnp.int32, (tq, ch), 1)
                        s = jnp.where(ki <= qi, s, -jnp.inf)
                    p = jnp.exp(s - m) * inv
                    dsc = p * (_dot_nt(dom, v) - delta)
                    dsb = dsc.astype(BF16)
                    dq = dq + _dot_nn(dsb, jnp.where(msk, k, jnp.zeros_like(k)))
                    dk = dk + _dot_tn(dsb, qm)
                    dv = dv + _dot_tn(p.astype(BF16), dom)
                    dc_rows.append(-jnp.sum(dsc, axis=0, keepdims=True))
                    if ch == chunk:
                        row_acc[e] = row_acc[e] + dsc
                    else:
                        dcq = dcq + jnp.where(lane == e, jnp.sum(dsc, axis=1, keepdims=True), 0.0)
                dk_ref[c0:c0 + ch, :] += dk
                dv_ref[c0:c0 + ch, :] += dv
                dc_ref[:, c0:c0 + ch] += _rows8(dc_rows, ch)
            dqg_ref[0] = (dq * scale).astype(BF16)
            for e in range(len(heads)):
                dcq = dcq + jnp.where(lane == e, jnp.sum(row_acc[e], axis=1, keepdims=True), 0.0)
            dcq_ref[...] = dcq

        for i in range(n_q):
            pl.when(step == i)(lambda i=i: run(i))

    qblk = pl.BlockSpec((tq, bw), lambda h, i: (i, h))
    kblk = pl.BlockSpec((t, bw), lambda h, i: (0, h))
    return pl.pallas_call(
        body, name=name, grid=(nhb, n_q),
        in_specs=[qblk, qblk, pl.BlockSpec((tq, bw), lambda h, i: (i, h + nhb)), qblk,
                  pl.BlockSpec((None, tq, LANES), lambda h, i: (h // ratio, i, 0)),
                  kblk, pl.BlockSpec((t, bw), lambda h, i: (0, h + nhb)),
                  pl.BlockSpec((N_HEADS, t), lambda h, i: (0, 0))],
        out_specs=[pl.BlockSpec((2, tq, bw), lambda h, i: (0, i, h)), kblk, kblk,
                   pl.BlockSpec((None, 8, t), lambda h, i: (h, 0, 0)),
                   pl.BlockSpec((None, tq, LANES), lambda h, i: (h, i, 0))],
        out_shape=[jax.ShapeDtypeStruct((2, t, d), BF16), jax.ShapeDtypeStruct((t, d), F32),
                   jax.ShapeDtypeStruct((t, d), F32), jax.ShapeDtypeStruct((nhb, 8, t), F32),
                   jax.ShapeDtypeStruct((nhb, t, LANES), F32)],
        compiler_params=_params("parallel", "arbitrary"),
    )(dy, qg, qg, o, stats, kv, kv, ct)


def _loss_bwd(h, tgt, *, lo, hi, tm, name):
    t, d = h.shape

    def body(h_ref, t_ref, l_ref, dy_ref):
        i = pl.program_id(0)
        rows = i * tm + lax.broadcasted_iota(jnp.int32, (tm, d), 0)
        err = jnp.where((rows >= lo) & (rows < hi), h_ref[...] - t_ref[...], 0.0)
        dy_ref[...] = err * (1.0 / d)
        part = jnp.sum(jnp.sum(err * err, axis=0, keepdims=True), axis=1, keepdims=True) * (0.5 / d)
        upd = jnp.broadcast_to(part, (8, LANES))

        @pl.when(i == 0)
        def _():
            l_ref[...] = upd

        @pl.when(i > 0)
        def _():
            l_ref[...] += upd

    row = pl.BlockSpec((tm, d), lambda i: (i, 0))
    return pl.pallas_call(
        body, name=name, grid=(t // tm,),
        in_specs=[row, row],
        out_specs=[pl.BlockSpec((8, LANES), lambda i: (0, 0)), row],
        out_shape=[jax.ShapeDtypeStruct((8, LANES), F32), jax.ShapeDtypeStruct((t, d), F32)],
        compiler_params=_params("arbitrary"),
    )(h, tgt)


def _adamw_math(w, gv, m, v):
    bc1 = 1.0 / (1.0 - ADAM_B1 ** ADAM_STEP)
    bc2 = 1.0 / (1.0 - ADAM_B2 ** ADAM_STEP)
    nm = ADAM_B1 * m + (1.0 - ADAM_B1) * gv
    nv = ADAM_B2 * v + (1.0 - ADAM_B2) * (gv * gv)
    delta = (-ADAM_LR) * ((nm * bc1) / (jnp.sqrt(nv * bc2) + ADAM_EPS) + ADAM_WD * w)
    return delta, nm, nv


def _adamw(w, g, m, v, *, name):
    r, c = w.shape
    tr = r
    for cand in (512, 256, 128, 64, 32, 16, 8):
        if r % cand == 0 and r > cand:
            tr = cand
            break

    def body(w_ref, g_ref, m_ref, v_ref, d_ref, nm_ref, nv_ref):
        d_ref[...], nm_ref[...], nv_ref[...] = _adamw_math(w_ref[...], g_ref[...], m_ref[...], v_ref[...])

    blk = pl.BlockSpec((tr, c), lambda i: (i, 0))
    out = jax.ShapeDtypeStruct((r, c), F32)
    return pl.pallas_call(
        body, name=name, grid=(r // tr,),
        in_specs=[blk] * 4, out_specs=[blk] * 3, out_shape=[out] * 3,
        compiler_params=_params("parallel"),
    )(w, g, m, v)


def _sum_adamw(recvs, sends, me, w, m, v, *, name):
    n_l = len(recvs)
    _, r, c = recvs[0].shape
    tr = _tile(r, (256, 192, 176, 128, 96, 64, 48, 32, 16))

    def body(me_ref, *refs):
        p_refs, own_refs = refs[:n_l], refs[n_l:2 * n_l]
        w_ref, m_ref, v_ref, g_ref, d_ref, nm_ref, nv_ref, acc_ref = refs[2 * n_l:]
        layer = pl.program_id(0)
        mine = me_ref[0]
        for k in range(n_l):
            @pl.when(layer == k)
            def _(k=k):
                acc_ref[...] = jnp.zeros((tr, c), F32)
                for dev in range(N_DEV):
                    @pl.when(mine == dev)
                    def _():
                        acc_ref[...] += own_refs[k][...].astype(F32)

                    @pl.when(mine != dev)
                    def _(dev=dev):
                        acc_ref[...] += p_refs[k][dev].astype(F32)
                acc = acc_ref[...]
                g_ref[...] = acc
                d_ref[...], nm_ref[...], nv_ref[...] = _adamw_math(w_ref[...], acc, m_ref[...], v_ref[...])

    p_specs = [pl.BlockSpec((N_DEV, tr, c), lambda l, i, me_ref, k=k: (0, jnp.where(l == k, i, 0), 0))
               for k in range(n_l)]
    own_specs = [pl.BlockSpec((None, tr, c), lambda l, i, me_ref, k=k: (me_ref[0], jnp.where(l == k, i, 0), 0))
                 for k in range(n_l)]
    blk = pl.BlockSpec((None, tr, c), lambda l, i, me_ref: (l, i, 0))
    out = jax.ShapeDtypeStruct((n_l, r, c), F32)
    return pl.pallas_call(
        body, name=name,
        grid_spec=pltpu.PrefetchScalarGridSpec(
            num_scalar_prefetch=1, grid=(n_l, r // tr),
            in_specs=p_specs + own_specs + [blk] * 3, out_specs=[blk] * 4,
            scratch_shapes=[pltpu.VMEM((tr, c), F32)]),
        out_shape=[out] * 4,
        compiler_params=_params("arbitrary", "arbitrary"),
    )(me, *recvs, *sends, w, m, v)


def _sum8(parts, *, name):
    _, r, c = parts.shape
    tr = r
    for cand in (512, 256, 128, 64, 32, 16):
        if r % cand == 0 and r > cand:
            tr = cand
            break

    def body(p_ref, o_ref):
        acc = p_ref[0].astype(F32)
        for k in range(1, N_DEV):
            acc = acc + p_ref[k].astype(F32)
        o_ref[...] = acc

    return pl.pallas_call(
        body, name=name, grid=(r // tr,),
        in_specs=[pl.BlockSpec((N_DEV, tr, c), lambda i: (0, i, 0))],
        out_specs=pl.BlockSpec((tr, c), lambda i: (i, 0)),
        out_shape=jax.ShapeDtypeStruct((r, c), F32),
        compiler_params=_params("parallel"),
    )(parts)


def _my_index():
    return 4 * lax.axis_index("x") + 2 * lax.axis_index("y") + lax.axis_index("c")


def _peer(k):
    x, y, c = lax.axis_index("x"), lax.axis_index("y"), lax.axis_index("c")
    px = x ^ ((k >> 2) & 1)
    py = y ^ ((k >> 1) & 1)
    pc = c ^ (k & 1)
    return (px, py, pc), 4 * px + 2 * py + pc


def _all_gather(shards, *, name):
    n_arr = len(shards)

    def body(*refs):
        ins, outs = refs[:n_arr], refs[n_arr:2 * n_arr]
        send_sems, recv_sems, local_sems = refs[2 * n_arr:]
        me = _my_index()
        local = [pltpu.make_async_copy(ins[n], outs[n].at[me], local_sems.at[n]) for n in range(n_arr)]
        for cp in local:
            cp.start()
        sends = []
        for k in range(1, N_DEV):
            peer, _ = _peer(k)
            for n in range(n_arr):
                cp = pltpu.make_async_remote_copy(
                    src_ref=ins[n], dst_ref=outs[n].at[me], send_sem=send_sems.at[n, k - 1],
                    recv_sem=recv_sems.at[n, k - 1], device_id=peer, device_id_type=pl.DeviceIdType.MESH)
                cp.start()
                sends.append(cp)
        for k in range(1, N_DEV):
            peer, pidx = _peer(k)
            for n in range(n_arr):
                pltpu.make_async_remote_copy(
                    src_ref=ins[n], dst_ref=outs[n].at[pidx], send_sem=send_sems.at[n, k - 1],
                    recv_sem=recv_sems.at[n, k - 1], device_id=peer, device_id_type=pl.DeviceIdType.MESH).wait_recv()
        for cp in sends:
            cp.wait_send()
        for cp in local:
            cp.wait()

    hbm = pl.BlockSpec(memory_space=pl.ANY)
    return pl.pallas_call(
        body, name=name,
        in_specs=[hbm] * n_arr, out_specs=[hbm] * n_arr,
        out_shape=[jax.ShapeDtypeStruct((N_DEV,) + s.shape, s.dtype) for s in shards],
        scratch_shapes=[pltpu.SemaphoreType.DMA((n_arr, N_DEV - 1)), pltpu.SemaphoreType.DMA((n_arr, N_DEV - 1)),
                        pltpu.SemaphoreType.DMA((n_arr,))],
        compiler_params=pltpu.CompilerParams(has_side_effects=True),
    )(*shards)


_HBM = pl.BlockSpec(memory_space=pltpu.HBM)
_SEM = pl.BlockSpec(memory_space=pltpu.SEMAPHORE)
_EFFECT = pltpu.SideEffectType.DATAFLOW_SIDE_EFFECTING


def _remote(src, dst, send_sem, recv_sem, peer):
    return pltpu.make_async_remote_copy(src_ref=src, dst_ref=dst, send_sem=send_sem, recv_sem=recv_sem,
                                        device_id=peer, device_id_type=pl.DeviceIdType.MESH)


def _place_own(src, layer, me, *, out_dtype, name):
    _, r, c = src.shape
    tr = _tile(r, (256, 192, 176, 128, 96, 64, 48, 32, 16))

    def body(me_ref, s_ref, o_ref):
        o_ref[...] = s_ref[...].astype(out_dtype)

    return pl.pallas_call(
        body, name=name,
        grid_spec=pltpu.PrefetchScalarGridSpec(
            num_scalar_prefetch=1, grid=(r // tr,),
            in_specs=[pl.BlockSpec((None, tr, c), lambda i, me_ref: (layer, i, 0))],
            out_specs=pl.BlockSpec((None, tr, c), lambda i, me_ref: (me_ref[0], i, 0))),
        out_shape=jax.ShapeDtypeStruct((N_DEV, r, c), out_dtype),
        compiler_params=_params("parallel"),
    )(me, src)


def _own_blocks(srcs, *, name):
    n = len(srcs)

    def body(*refs):
        ins, outs, sems = refs[:n], refs[n:2 * n], refs[2 * n]
        me = _my_index()
        cps = [pltpu.make_async_copy(ins[t].at[me], outs[t].at[me], sems.at[t]) for t in range(n)]
        for cp in cps:
            cp.start()
        for cp in cps:
            cp.wait()

    return pl.pallas_call(
        body, name=name, in_specs=[_HBM] * n, out_specs=[_HBM] * n,
        out_shape=[jax.ShapeDtypeStruct(s.shape, s.dtype) for s in srcs],
        scratch_shapes=[pltpu.SemaphoreType.DMA((n,))],
    )(*srcs)


def _split_start(groups, *, scatter, name):
    sizes = [len(srcs) for srcs, _ in groups]
    flat_src = [s for srcs, _ in groups for s in srcs]
    flat_land = [l for _, lands in groups for l in lands]
    n, n_g = len(flat_land), len(groups)
    if not scatter:
        flat_src = []
    n_in = len(flat_src) + n

    def body(*refs):
        lands = refs[n_in - n:n_in]
        ins = refs[:n] if scatter else lands
        sems = refs[n_in:n_in + 2 * n_g]
        token = refs[-1]
        me = _my_index()
        t = 0
        for g in range(n_g):
            for q in range(sizes[g]):
                for k in range(1, N_DEV):
                    peer, pidx = _peer(k)
                    src = ins[t].at[pidx] if scatter else ins[t].at[me]
                    slot = q * (N_DEV - 1) + k - 1
                    _remote(src, lands[t].at[me], sems[2 * g].at[slot], sems[2 * g + 1].at[slot], peer).start()
                t += 1
        token[...] = jnp.zeros_like(token)

    sem_shapes = []
    for sz in sizes:
        sem_shapes += [pltpu.SemaphoreType.DMA((sz * (N_DEV - 1),)), pltpu.SemaphoreType.DMA((sz * (N_DEV - 1),))]
    outs = pl.pallas_call(
        body, name=name,
        in_specs=[_HBM] * n_in,
        out_specs=[_SEM] * (2 * n_g) + [_HBM] * n_in + [pl.BlockSpec(memory_space=pltpu.VMEM)],
        out_shape=sem_shapes + [pltpu.HBM(a.shape, a.dtype) for a in flat_src + flat_land]
        + [jax.ShapeDtypeStruct((8, LANES), F32)],
        input_output_aliases={i: 2 * n_g + i for i in range(n_in)},
        compiler_params=pltpu.CompilerParams(has_side_effects=_EFFECT),
    )(*[pltpu.with_memory_space_constraint(a, pltpu.HBM) for a in flat_src + flat_land])
    sems, thru, token = outs[:2 * n_g], outs[2 * n_g:2 * n_g + n_in], outs[-1]
    handles, pos = [], 0
    for g, sz in enumerate(sizes):
        lands_g = thru[n_in - n + pos:n_in - n + pos + sz]
        handles.append((sems[2 * g], sems[2 * g + 1], thru[pos:pos + sz] if scatter else [], lands_g))
        pos += sz
    return handles, token


def _split_wait(handle, after, *, scatter, name):
    send_sems, recv_sems, srcs, lands = handle
    n, n_src = len(lands), len(srcs)

    def body(*refs):
        lnd = refs[n_src:n_src + n]
        ins = refs[:n_src] if scatter else lnd
        ssem, rsem = refs[n_src + n], refs[n_src + n + 1]
        me = _my_index()
        for t in range(n):
            for k in range(1, N_DEV):
                peer, pidx = _peer(k)
                block = ins[t].at[me]
                slot = t * (N_DEV - 1) + k - 1
                _remote(block, lnd[t].at[me], ssem.at[slot], rsem.at[slot], peer).wait_send()
                _remote(block, lnd[t].at[pidx], ssem.at[slot], rsem.at[slot], peer).wait_recv()

    return pl.pallas_call(
        body, name=name,
        in_specs=[_HBM] * (n_src + n) + [_SEM, _SEM, pl.BlockSpec(memory_space=pl.ANY)],
        out_specs=[_HBM] * n,
        out_shape=[pltpu.HBM(l.shape, l.dtype) for l in lands],
        input_output_aliases={n_src + t: t for t in range(n)},
        compiler_params=pltpu.CompilerParams(has_side_effects=_EFFECT),
    )(*srcs, *lands, send_sems, recv_sems, after)


def _pack(arrs, dtype, row_quantum=16):
    flat = jnp.concatenate([a.astype(dtype).reshape(-1) for a in arrs])
    pad = (-flat.shape[0]) % (row_quantum * PACK_COLS)
    if pad:
        flat = jnp.concatenate([flat, jnp.zeros((pad,), dtype)])
    return flat.reshape(-1, PACK_COLS)


def _pack8(arrs, dtype):
    flat = jnp.concatenate([a.astype(dtype).reshape(N_DEV, -1) for a in arrs], axis=1)
    pad = (-flat.shape[1]) % (16 * PACK_COLS)
    if pad:
        flat = jnp.concatenate([flat, jnp.zeros((N_DEV, pad), dtype)], axis=1)
    return flat.reshape(N_DEV, -1, PACK_COLS)


def _unpack(slab, shapes, lead):
    lead_shape = slab.shape[:lead]
    flat = slab.reshape(lead_shape + (-1,))
    outs, off = [], 0
    for shp in shapes:
        size = math.prod(shp)
        outs.append(flat[..., off:off + size].reshape(lead_shape + tuple(shp)))
        off += size
    return outs


def _cols_full(g):
    g = jnp.moveaxis(g, 0, -2)
    return g.reshape(g.shape[:-2] + (g.shape[-2] * g.shape[-1],))


def _cols_split(full):
    n = full.shape[-1] // N_DEV
    return jnp.moveaxis(full.reshape(full.shape[:-1] + (N_DEV, n)), -2, 0)


def _block_diag(w, per):
    n, b, _ = w.shape
    w4 = w.reshape(n // per, per, b, b)
    eye = jnp.eye(per, dtype=w.dtype)
    return jnp.einsum('gpab,pq->gpaqb', w4, eye).reshape(n // per, per * b, per * b)


def _block_diag_extract(g, per):
    gn, cb, _ = g.shape
    b = cb // per
    g5 = g.reshape(gn, per, b, per, b)
    return jnp.stack([g5[:, p, :, p, :] for p in range(per)], axis=1).reshape(gn * per, b, b)


def _slab2d(a):
    return a.reshape(-1, a.shape[-1])


def _lru_block_cols(r_dim):
    lru = r_dim // N_LRU_BLOCKS
    return lru * LANES // math.gcd(lru, LANES)


BIG = ("a_w_in", "a_w_out", "b_w_in", "b_w_out", "f_w_in", "f_w_out")
COL_F32 = ("meta", "a_conv_w", "a_conv_b", "a_b_r", "a_b_i", "a_lambda", "f_conv_w")
REPLICATED = ("a_w_r", "a_w_i", "kv_f_b", "f_conv_b", "ln1_g", "ln1_b", "ln2_g", "ln2_b")
WEIGHT_NAMES = ("meta", "a_w_in", "a_conv_w", "a_conv_b", "a_w_r", "a_b_r", "a_w_i", "a_b_i", "a_lambda", "a_w_out",
                "kv_w", "kv_f_b", "b_w_in", "b_w_out", "f_w_in", "f_conv_w", "f_conv_b", "f_w_out",
                "ln1_g", "ln1_b", "ln2_g", "ln2_b")


def _kv_layout(kv_gathered, d):
    kv_full = _cols_full(kv_gathered)
    kv_pad = 2 * d + LANES - kv_full.shape[1]
    return jnp.concatenate([kv_full, jnp.zeros((d, kv_pad), kv_full.dtype)], axis=1)


def _small_layouts(small):
    r_dim = small["a_lambda"].shape[1]
    n_f = small["f_conv_b"].shape[1] // N_DEV
    cb = _lru_block_cols(r_dim)
    per = cb // (r_dim // N_LRU_BLOCKS)
    n_a = small["a_lambda"].shape[0]
    f_conv_w3 = small["f_conv_w"].reshape(N_LAYERS, 3, N_DEV, n_f).transpose(0, 2, 1, 3)
    f_conv_b3 = small["f_conv_b"].reshape(N_LAYERS, N_DEV, 1, n_f)
    return {
        "kv_fb": jnp.concatenate([small["kv_f_b"], jnp.zeros((LANES - N_HEADS,), F32)])[None],
        "a_cwb": jnp.concatenate([small["a_conv_w"], small["a_conv_b"][:, None],
                                  jnp.zeros((n_a, 3, r_dim), F32)], axis=1),
        "a_vecs": jnp.concatenate([jnp.stack([small["a_b_r"], small["a_b_i"], small["a_lambda"]], axis=1),
                                   jnp.zeros((n_a, 5, r_dim), F32)], axis=1),
        "a_bd_r": jnp.stack([_block_diag(small["a_w_r"][l], per) for l in range(n_a)]).astype(BF16),
        "a_bd_i": jnp.stack([_block_diag(small["a_w_i"][l], per) for l in range(n_a)]).astype(BF16),
        "f_cwb3": jnp.concatenate([f_conv_w3, f_conv_b3, jnp.zeros((N_LAYERS, N_DEV, 4, n_f), F32)], axis=2),
        "ln1_g": small["ln1_g"][:, None], "ln1_b": small["ln1_b"][:, None],
        "ln2_g": small["ln2_g"][:, None], "ln2_b": small["ln2_b"][:, None],
    }


def _local_step(h0, tgt, n_meta, n_tok, wts, hooks):
    tp, d = h0.shape
    tm = tp // 8 if (tp // 8) % 16 == 0 else tp
    tmb = _tile(tp, (1088, 512, 320, 256, 128))
    tq = 128
    tqa_fwd = tp // 4 if tp % 64 == 0 else tq
    tqa_bwd = tp // 4 if tp % 64 == 0 else tq
    r_dim = wts["a_vecs"].shape[2]
    cb = wts["a_bd_r"].shape[-1]
    sb = LANES
    n_b = N_LAYERS - N_A_LAYERS

    h, h_bf = h0, h0.astype(BF16)
    saved = []
    kvs = None
    for layer in range(N_LAYERS):
        lw = {}
        sv = {"h_bf": h_bf, "w": lw}
        if layer < N_A_LAYERS:
            lw["in"] = hooks.weight(layer, "in", h)
            sv["gr"] = _proj_in(h_bf, lw["in"], shard_major=False, name="a_in_proj")
            sv["rec"] = _conv_a_fwd(sv["gr"], wts["a_cwb"][layer], cb=cb, name="a_conv_fwd")
            a, u, sv["r"], sv["i"] = _gates_fwd(sv["rec"], wts["a_bd_r"][layer], wts["a_bd_i"][layer],
                                                wts["a_vecs"][layer], tm=tm, name="a_gates_fwd")
            sv["a"] = a
            sv["hr"], y3 = _scan_fwd(a, u, sv["gr"], cb=sb, name="a_scan_fwd")
        else:
            j = layer - N_A_LAYERS
            if j == 0:
                kv_w = _kv_layout(hooks.weight(layer, "kv_w", h), d)
                kvs = {"h_bf": h_bf, "w": kv_w}
                kvs["kv"] = _mm_nn(h_bf, kv_w[:, :2 * d], tn=_tile(2 * d, (512, 256, 128)), out_dtype=BF16,
                                   name="kv_proj")
                kvs["fp"] = _mm_nn(h_bf, kv_w[:, 2 * d:], tn=LANES, out_dtype=F32, name="f_proj")
                kvs["c"], ct = _fgate_fwd(kvs["fp"], wts["kv_fb"], tq=tq, name="fgate_fwd")
                kvs["ct"] = ct[:N_HEADS]
            lw["in"] = hooks.weight(layer, "in", kvs["c"] if j == 0 else h)
            sv["qg"] = _proj_in(h_bf, lw["in"], shard_major=False, name="b_in_proj")
            sv["o"], y3, sv["st"] = _attn_fwd(sv["qg"], kvs["kv"], kvs["ct"], tq=tqa_fwd, name="attn_fwd")
        sv["y3"] = y3
        lw["out"] = hooks.weight(layer, "out", y3)
        sv["s1"], h, h_bf = _out_ln(y3, lw["out"], h, wts["ln1_g"][layer], wts["ln1_b"][layer], n_valid=n_tok,
                                    tm=tmb // 2, name="mix_out_ln")
        sv["h1_bf"] = h_bf
        lw["f_in"] = hooks.weight(layer, "f_in", h)
        sv["z3"] = _proj_in(h_bf, lw["f_in"], shard_major=True, transposed=True, name="f_in_proj")
        sv["yf3"] = _convglu_fwd(sv["z3"], wts["f_cwb3"][layer], name="f_convglu_fwd")
        lw["f_out"] = hooks.weight(layer, "f_out", sv["yf3"])
        sv["s2"], h, h_bf = _out_ln(sv["yf3"], lw["f_out"], h, wts["ln2_g"][layer], wts["ln2_b"][layer],
                                    n_valid=n_tok, tm=tmb // 2, name="ffn_out_ln")
        saved.append(sv)

    loss_tile, dh = _loss_bwd(h, tgt, lo=n_meta, hi=n_tok, tm=tm, name="loss")

    grads = {k: [None] * N_LAYERS for k in ("f_cwb3", "ln1_gb", "ln2_gb")}
    grads.update({k: [None] * N_A_LAYERS for k in ("a_cwb", "a_bd_r", "a_bd_i", "a_vecs")})
    dkv = []
    token = jnp.zeros((), F32)
    for layer in reversed(range(N_LAYERS)):
        sv = saved[layer]
        lw = sv["w"]
        big = {}
        ds, ds_bf, grads["ln2_gb"][layer] = _ln_bwd(dh, sv["s2"], wts["ln2_g"][layer] + token, tm=tm, name="ln_bwd")
        dz, dcw = _ffn_bwd_mid(ds_bf, lw["f_out"], sv["z3"], wts["f_cwb3"][layer], name="f_bwd_mid")
        grads["f_cwb3"][layer] = dcw.reshape((N_DEV,) + dcw.shape[2:])
        dz3 = dz
        big["f_out"] = _w_out_grad(sv["yf3"], ds_bf, lw["f_out"].shape[1], name="f_w_out_grad")
        dh = _in_bwd(dz3, lw["f_in"], ds, tm=tmb, transposed=True, name="f_in_bwd")
        big["f_in"] = _w_in_grad(sv["h1_bf"], dz3, transposed=True, name="f_w_in_grad")
        token = hooks.grads_ready(layer, "ffn", big)
        big = {}
        ds, ds_bf, grads["ln1_gb"][layer] = _ln_bwd(dh, sv["s1"], wts["ln1_g"][layer] + token, tm=tm, name="ln_bwd")
        if layer < N_A_LAYERS:
            dy = _out_bwd(ds_bf, lw["out"], tm=tmb // 2, name="a_out_bwd")
            big["out"] = _w_out_grad(sv["y3"], ds_bf, lw["out"].shape[1], name="a_w_out_grad")
            d_h, d_a, dgate = _scan_bwd(dy, sv["gr"], sv["hr"], sv["a"], cb=sb, name="a_scan_bwd")
            d_rec, dpr, dpi, grads["a_vecs"][layer] = _gates_bwd(
                sv["rec"], sv["r"], sv["i"], sv["a"], d_h, d_a, wts["a_bd_r"][layer], wts["a_bd_i"][layer],
                wts["a_vecs"][layer], tm=tm, name="a_gates_bwd")
            grads["a_bd_r"][layer], grads["a_bd_i"][layer] = _bd_grad(sv["rec"], dpr, dpi, cb=cb, name="a_bd_grad")
            dact, grads["a_cwb"][layer] = _conv_a_bwd(d_rec, sv["gr"], dgate, wts["a_cwb"][layer], cb=cb,
                                                      name="a_conv_bwd")
            dh = _in_bwd(dact, lw["in"], ds, tm=tmb, name="a_in_bwd")
            big["in"] = _w_in_grad(sv["h_bf"], dact, name="a_w_in_grad")
        else:
            j = layer - N_A_LAYERS
            dy = _out_bwd(ds_bf, lw["out"], tm=tmb // 2, name="b_out_bwd")
            big["out"] = _w_out_grad(sv["y3"], ds_bf, lw["out"].shape[1], name="b_w_out_grad")
            dqg, dk, dv, dc, dcq = _attn_bwd(dy, sv["qg"], sv["o"], sv["st"], kvs["kv"], kvs["ct"], tq=tqa_bwd,
                                             name="attn_bwd")
            dkv.append((dk, dv, dc, dcq))
            dh = _in_bwd(dqg, lw["in"], ds, tm=tmb, name="b_in_bwd")
            big["in"] = _w_in_grad(sv["h_bf"], dqg, name="b_w_in_grad")
            if j == 0:
                hpb = _head_block_width(d // N_HEADS, BWD_HEAD_TILES) // (d // N_HEADS)
                dct = (dkv[0][2] + dkv[1][2])[:, :hpb, :].reshape(N_HEADS, tp)
                dcq = (dkv[0][3] + dkv[1][3])[:, :, :hpb]
                dct = dct + jnp.transpose(dcq, (0, 2, 1)).reshape(N_HEADS, tp)
                dct = jnp.concatenate([dct, jnp.zeros((LANES - N_HEADS, tp), F32)])
                df_bf, grads["kv_fb"] = _fgate_bwd(dct, kvs["fp"], wts["kv_fb"], tq=tq, name="fgate_bwd")
                dkvz = jnp.concatenate([_pair_sum(dkv[0][0], dkv[1][0], tm=tm, name="kv_pair_sum"),
                                        _pair_sum(dkv[0][1], dkv[1][1], tm=tm, name="kv_pair_sum"), df_bf], axis=1)
                dh = _mm_nt_full(dkvz, kvs["w"], dh, tm=tmb // 2, name="kv_in_bwd")
                big["kv_w"] = _mm_tn_cols(kvs["h_bf"], dkvz, tn=LANES, name="kv_w_grad")
        token = hooks.grads_ready(layer, "mix", big)
    return loss_tile, dh, grads


def _finish_small_grads(grads, d_h0, n_meta):
    r_dim = grads["a_vecs"][0].shape[1]
    per = _lru_block_cols(r_dim) // (r_dim // N_LRU_BLOCKS)
    a_cwb = jnp.stack(grads["a_cwb"])
    a_vecs = jnp.stack(grads["a_vecs"])
    f_cwb3 = jnp.stack(grads["f_cwb3"])
    ln1 = jnp.stack(grads["ln1_gb"])
    ln2 = jnp.stack(grads["ln2_gb"])
    f_rows = f_cwb3.transpose(0, 2, 1, 3).reshape(N_LAYERS, 8, -1)
    return {
        "meta": d_h0[:n_meta],
        "a_conv_w": a_cwb[:, :4], "a_conv_b": a_cwb[:, 4],
        "a_w_r": jnp.stack([_block_diag_extract(g, per) for g in grads["a_bd_r"]]),
        "a_b_r": a_vecs[:, 0],
        "a_w_i": jnp.stack([_block_diag_extract(g, per) for g in grads["a_bd_i"]]),
        "a_b_i": a_vecs[:, 1], "a_lambda": a_vecs[:, 2],
        "kv_f_b": grads["kv_fb"][0, :N_HEADS],
        "f_conv_w": f_rows[:, :3], "f_conv_b": f_rows[:, 3],
        "ln1_g": ln1[:, 0], "ln1_b": ln1[:, 1], "ln2_g": ln2[:, 0], "ln2_b": ln2[:, 1],
    }


def kernel(x, meta, a_w_in, a_conv_w, a_conv_b, a_w_r, a_b_r, a_w_i, a_b_i, a_lambda, a_w_out, kv_w, kv_f_b, b_w_in, b_w_out, f_w_in, f_conv_w, f_conv_b, f_w_out, ln1_g, ln1_b, ln2_g, ln2_b, loss_target, m_meta, m_a_w_in, m_a_conv_w, m_a_conv_b, m_a_w_r, m_a_b_r, m_a_w_i, m_a_b_i, m_a_lambda, m_a_w_out, m_kv_w, m_kv_f_b, m_b_w_in, m_b_w_out, m_f_w_in, m_f_conv_w, m_f_conv_b, m_f_w_out, m_ln1_g, m_ln1_b, m_ln2_g, m_ln2_b, v_meta, v_a_w_in, v_a_conv_w, v_a_conv_b, v_a_w_r, v_a_b_r, v_a_w_i, v_a_b_i, v_a_lambda, v_a_w_out, v_kv_w, v_kv_f_b, v_b_w_in, v_b_w_out, v_f_w_in, v_f_conv_w, v_f_conv_b, v_f_w_out, v_ln1_g, v_ln1_b, v_ln2_g, v_ln2_b):
    w = dict(meta=meta, a_w_in=a_w_in, a_conv_w=a_conv_w, a_conv_b=a_conv_b, a_w_r=a_w_r, a_b_r=a_b_r, a_w_i=a_w_i,
             a_b_i=a_b_i, a_lambda=a_lambda, a_w_out=a_w_out, kv_w=kv_w, kv_f_b=kv_f_b, b_w_in=b_w_in,
             b_w_out=b_w_out, f_w_in=f_w_in, f_conv_w=f_conv_w, f_conv_b=f_conv_b, f_w_out=f_w_out, ln1_g=ln1_g,
             ln1_b=ln1_b, ln2_g=ln2_g, ln2_b=ln2_b)
    m = dict(meta=m_meta, a_w_in=m_a_w_in, a_conv_w=m_a_conv_w, a_conv_b=m_a_conv_b, a_w_r=m_a_w_r, a_b_r=m_a_b_r,
             a_w_i=m_a_w_i, a_b_i=m_a_b_i, a_lambda=m_a_lambda, a_w_out=m_a_w_out, kv_w=m_kv_w, kv_f_b=m_kv_f_b,
             b_w_in=m_b_w_in, b_w_out=m_b_w_out, f_w_in=m_f_w_in, f_conv_w=m_f_conv_w, f_conv_b=m_f_conv_b,
             f_w_out=m_f_w_out, ln1_g=m_ln1_g, ln1_b=m_ln1_b, ln2_g=m_ln2_g, ln2_b=m_ln2_b)
    v = dict(meta=v_meta, a_w_in=v_a_w_in, a_conv_w=v_a_conv_w, a_conv_b=v_a_conv_b, a_w_r=v_a_w_r, a_b_r=v_a_b_r,
             a_w_i=v_a_w_i, a_b_i=v_a_b_i, a_lambda=v_a_lambda, a_w_out=v_a_w_out, kv_w=v_kv_w, kv_f_b=v_kv_f_b,
             b_w_in=v_b_w_in, b_w_out=v_b_w_out, f_w_in=v_f_w_in, f_conv_w=v_f_conv_w, f_conv_b=v_f_conv_b,
             f_w_out=v_f_w_out, ln1_g=v_ln1_g, ln1_b=v_ln1_b, ln2_g=v_ln2_g, ln2_b=v_ln2_b)
    shapes = {n: w[n].shape for n in WEIGHT_NAMES}

    me = jnp.reshape(_my_index(), (1,)).astype(jnp.int32)

    def as_stored(name, a):
        return jnp.swapaxes(a, 1, 2) if name == "f_w_in" else a

    param_of = {"in": ("a_w_in", "b_w_in"), "out": ("a_w_out", "b_w_out"), "f_in": ("f_w_in",) * 2,
                "f_out": ("f_w_out",) * 2}
    order = [("small", None, None)]
    for layer in range(N_LAYERS):
        if layer == N_A_LAYERS:
            order.append(("kv_w", layer, 0))
        for key in ("in", "out", "f_in", "f_out"):
            order.append((key, layer, layer if key[0] == "f" or layer < N_A_LAYERS else layer - N_A_LAYERS))
    def place(key, layer, idx):
        if key == "small":
            return _place_own(_pack([w[n] for n in COL_F32], F32)[None], 0, me, out_dtype=F32, name="place_small")
        if key == "kv_w":
            return _place_own(w["kv_w"][None], 0, me, out_dtype=BF16, name="place_kv_w")
        name = param_of[key][0 if layer < N_A_LAYERS else 1]
        return _place_own(as_stored(name, w[name]), idx, me, out_dtype=BF16, name=f"place_{name}_{idx}")

    lands = [place(*o) for o in order]
    gather_handles, gather_token = _split_start([([l], [l]) for l in lands], scatter=False, name="gather_start")
    group_of = {(key, layer): g for g, (key, layer, _) in enumerate(order)}
    (got_s,) = _split_wait(gather_handles[0], gather_token, scatter=False, name="gather_wait_small")
    small = {n: w[n] for n in REPLICATED}
    for n, part in zip(COL_F32, _unpack(got_s, [w[n].shape for n in COL_F32], 1)):
        small[n] = _cols_full(part)
    n_meta, d = small["meta"].shape

    class Hooks:
        pending = None
        received = {}
        sent = {}

        @staticmethod
        def weight(layer, key, after):
            (got,) = _split_wait(gather_handles[group_of[(key, layer)]], after, scatter=False,
                                 name=f"gather_wait_{key}_{layer}")
            return got

        @staticmethod
        def collect(after):
            if Hooks.pending is not None:
                tag, names, handle = Hooks.pending
                got = _split_wait(handle, after, scatter=True, name=f"scatter_wait_{tag}")
                Hooks.received.update(zip(names, got))
                Hooks.pending = None

        @staticmethod
        def grads_ready(layer, part, big):
            if "kv_w" in big:
                big["kv_w"] = _cols_split(big["kv_w"][:, :shapes["kv_w"][1] * N_DEV]).astype(BF16)
            names = [(key, layer) for key in big]
            send = [big[key] for key in big]
            Hooks.collect(send[0])
            empty = [lax.empty(s.shape, s.dtype) for s in send]
            handles, token = _split_start([(send, empty)], scatter=True, name=f"scatter_start_{part}_{layer}")
            Hooks.pending = (f"{part}_{layer}", names, handles[0])
            Hooks.sent.update(zip(names, handles[0][2]))
            return token[0, 0]

    Hooks.pending, Hooks.received, Hooks.sent = None, {}, {}

    n_tok = n_meta + x.shape[1]
    tp = -(-n_tok // ROW_ALIGN) * ROW_ALIGN
    pad = jnp.zeros((tp - n_tok, d), F32)
    h0 = jnp.concatenate([small["meta"], x[0], pad])
    tgt = jnp.concatenate([jnp.zeros((n_meta, d), F32), loss_target[0], pad])
    loss_tile, d_h0, grads = _local_step(h0, tgt, n_meta, n_tok, _small_layouts(small), Hooks)
    g_small = _finish_small_grads(grads, d_h0, n_meta)
    loss = lax.psum(loss_tile[0, 0], MESH_AXES)
    grad_x = d_h0[n_meta:n_tok][None]

    rep = _pack([g_small[n] for n in REPLICATED], F32, row_quantum=16 * N_DEV)
    send = [_pack8([_cols_split(g_small[n]) for n in COL_F32], F32), rep.reshape(N_DEV, -1, PACK_COLS)]
    lands = _own_blocks(send, name="scatter_own_small")
    handles, token = _split_start([(send, lands)], scatter=True, name="scatter_start_small")

    g, delta, new_m, new_v = {}, {}, {}, {}
    layers_of = {
        "a_w_in": [("in", l) for l in range(N_A_LAYERS)], "a_w_out": [("out", l) for l in range(N_A_LAYERS)],
        "b_w_in": [("in", l) for l in range(N_A_LAYERS, N_LAYERS)],
        "b_w_out": [("out", l) for l in range(N_A_LAYERS, N_LAYERS)],
        "f_w_in": [("f_in", l) for l in range(N_LAYERS)], "f_w_out": [("f_out", l) for l in range(N_LAYERS)],
        "kv_w": [("kv_w", N_A_LAYERS)],
    }
    ready = [n for n in BIG + ("kv_w",) if all(t in Hooks.received for t in layers_of[n])]

    def done(names):
        return jnp.stack([g[n][(0,) * g[n].ndim] for n in names])

    for n in ready + [n for n in BIG + ("kv_w",) if n not in ready]:
        if n not in ready and Hooks.pending is not None:
            Hooks.collect(done(ready))
        lift = (lambda a: a[None]) if n == "kv_w" else (lambda a, n=n: as_stored(n, a))
        outs = _sum_adamw([Hooks.received[t] for t in layers_of[n]], [Hooks.sent[t] for t in layers_of[n]], me,
                          lift(w[n]), lift(m[n]), lift(v[n]), name="sum_adamw_" + n)
        g[n], delta[n], new_m[n], new_v[n] = [as_stored(n, o).reshape(shapes[n]) for o in outs]
    recv_s, recv_r = _split_wait(handles[0], done(BIG + ("kv_w",)), scatter=True, name="scatter_wait_small")
    sum_s = _sum8(recv_s, name="sum_grads_f32")
    g.update(zip(COL_F32, _unpack(sum_s, [shapes[n] for n in COL_F32], 0)))
    (got_r,) = _all_gather([_sum8(recv_r, name="sum_grads_replicated")], name="gather_replicated_sums")
    g.update(zip(REPLICATED, _unpack(got_r.reshape(-1, PACK_COLS), [shapes[n] for n in REPLICATED], 0)))

    for n in COL_F32 + REPLICATED:
        shp = shapes[n]
        dl, nm, nv = _adamw(_slab2d(w[n]), _slab2d(g[n]), _slab2d(m[n]), _slab2d(v[n]), name="adamw")
        delta[n], new_m[n], new_v[n] = dl.reshape(shp), nm.reshape(shp), nv.reshape(shp)
    return (loss, grad_x, *[g[n] for n in WEIGHT_NAMES], *[delta[n] for n in WEIGHT_NAMES],
            *[new_m[n] for n in WEIGHT_NAMES], *[new_v[n] for n in WEIGHT_NAMES])
```

```python
import math

import jax
import jax.numpy as jnp
from jax import lax
from jax.experimental import pallas as pl
from jax.experimental.pallas import tpu as pltpu

F32 = jnp.float32
BF16 = jnp.bfloat16

N_DEV = 8
MESH_AXES = ("x", "y", "c")
N_LAYERS = 4
N_A_LAYERS = 2
N_LRU_BLOCKS = 16
N_HEADS = 16
LRU_C = 8.0
DN_ALPHA = (2 * N_LAYERS) ** 0.25
LN_EPS = 1e-5
ADAM_LR, ADAM_B1, ADAM_B2, ADAM_EPS, ADAM_WD, ADAM_STEP = 0.001, 0.9, 0.999, 1e-08, 0.01, 10

LANES = 128
SUBLANES = 8
ROW_ALIGN = 128
VMEM_LIMIT_BYTES = 56 * 1024 * 1024
GELU_K = math.sqrt(2.0 / math.pi)
GELU_C = 0.044715
PACK_COLS = 1024


def _params(*sem):
    return pltpu.CompilerParams(dimension_semantics=sem, vmem_limit_bytes=VMEM_LIMIT_BYTES)


def _gelu(x):
    th = jnp.tanh(GELU_K * (x + GELU_C * x * x * x))
    return 0.5 * x * (1.0 + th)


def _gelu_and_grad(x):
    x2 = x * x
    th = jnp.tanh(GELU_K * (x + GELU_C * x2 * x))
    g = 0.5 * x * (1.0 + th)
    dg = 0.5 * (1.0 + th) + 0.5 * x * (1.0 - th * th) * (GELU_K * (1.0 + 3.0 * GELU_C * x2))
    return g, dg


def _sigmoid(x):
    return 0.5 * jnp.tanh(0.5 * x) + 0.5


def _expm1(x):
    small = x * (1.0 + 0.5 * x * (1.0 + (1.0 / 3.0) * x * (1.0 + 0.25 * x)))
    return jnp.where(jnp.abs(x) < 1e-2, small, jnp.exp(x) - 1.0)


def _softplus(x):
    e = jnp.exp(-jnp.abs(x))
    small = e * (1.0 - 0.5 * e * (1.0 - (2.0 / 3.0) * e))
    return jnp.maximum(x, 0.0) + jnp.where(e < 1e-2, small, jnp.log(1.0 + e))


def _shift_down(x, s):
    if s == 0:
        return x
    rows = lax.broadcasted_iota(jnp.int32, x.shape, 0)
    return jnp.where(rows >= s, pltpu.roll(x, s, 0), 0.0)


def _shift_up(x, s):
    if s == 0:
        return x
    n = x.shape[0]
    rows = lax.broadcasted_iota(jnp.int32, x.shape, 0)
    return jnp.where(rows < n - s, pltpu.roll(x, n - s, 0), 0.0)


def _dot_nn(a, b):
    return lax.dot_general(a, b, (((1,), (0,)), ((), ())), preferred_element_type=F32)


def _dot_nt(a, b):
    return lax.dot_general(a, b, (((1,), (1,)), ((), ())), preferred_element_type=F32)


def _dot_tn(a, b):
    return lax.dot_general(a, b, (((0,), (0,)), ((), ())), preferred_element_type=F32)


def _rows8(vals, width):
    rows = lax.broadcasted_iota(jnp.int32, (8, width), 0)
    out = jnp.zeros((8, width), F32)
    for k, v in enumerate(vals):
        out = jnp.where(rows == k, jnp.broadcast_to(v, (8, width)), out)
    return out


def _tile(n, prefer):
    for c in prefer:
        if n % c == 0:
            return c
    return n


def _mm_nn(a, b, *, tn, out_dtype, name):
    m, k = a.shape
    n = b.shape[1]

    def body(a_ref, b_ref, o_ref):
        o_ref[...] = _dot_nn(a_ref[...], b_ref[...]).astype(o_ref.dtype)

    return pl.pallas_call(
        body, name=name, grid=(n // tn,),
        in_specs=[pl.BlockSpec((m, k), lambda j: (0, 0)), pl.BlockSpec((k, tn), lambda j: (0, j))],
        out_specs=pl.BlockSpec((m, tn), lambda j: (0, j)),
        out_shape=jax.ShapeDtypeStruct((m, n), out_dtype),
        compiler_params=_params("parallel"),
    )(a, b)


def _proj_in(h_bf, g_in, *, shard_major, name, transposed=False):
    t, k = h_bf.shape
    n = g_in.shape[1] if transposed else g_in.shape[2]

    def body(a_ref, b_ref, o_ref):
        o_ref[...] = _dot_nt(a_ref[...], b_ref[...]) if transposed else _dot_nn(a_ref[...], b_ref[...])

    if shard_major:
        out_spec = pl.BlockSpec((None, t, n), lambda j: (j, 0, 0))
        out_shape = jax.ShapeDtypeStruct((N_DEV, t, n), F32)
    else:
        out_spec = pl.BlockSpec((t, n), lambda j: (0, j))
        out_shape = jax.ShapeDtypeStruct((t, N_DEV * n), F32)
    return pl.pallas_call(
        body, name=name, grid=(N_DEV,),
        in_specs=[pl.BlockSpec((t, k), lambda j: (0, 0)),
                  pl.BlockSpec((None,) + g_in.shape[1:], lambda j: (j, 0, 0))],
        out_specs=out_spec, out_shape=out_shape,
        compiler_params=_params("parallel"),
    )(h_bf, g_in)


def _out_ln(y3, g_out, hin, g, b, *, n_valid, tm, name):
    nj, t, kj = y3.shape
    _, r, d = g_out.shape

    def body(y_ref, w_ref, hin_ref, g_ref, b_ref, s_ref, h_ref, hb_ref):
        w = w_ref[...].reshape(N_DEV * r, d)
        s = DN_ALPHA * hin_ref[...]
        for jj in range(nj):
            s = s + _dot_nn(y_ref[jj], w[jj * kj:(jj + 1) * kj])
        mu = jnp.mean(s, axis=-1, keepdims=True)
        xc = s - mu
        var = jnp.mean(xc * xc, axis=-1, keepdims=True)
        h = xc * lax.rsqrt(var + LN_EPS) * g_ref[...] + b_ref[...]
        s_ref[...] = s
        h_ref[...] = h
        rows = pl.program_id(0) * tm + lax.broadcasted_iota(jnp.int32, (tm, d), 0)
        hb_ref[...] = jnp.where(rows < n_valid, h, 0.0).astype(BF16)

    row = pl.BlockSpec((tm, d), lambda i: (i, 0))
    vec = pl.BlockSpec((1, d), lambda i: (0, 0))
    return pl.pallas_call(
        body, name=name, grid=(t // tm,),
        in_specs=[pl.BlockSpec((nj, tm, kj), lambda i: (0, i, 0)),
                  pl.BlockSpec((N_DEV, r, d), lambda i: (0, 0, 0)), row, vec, vec],
        out_specs=[row, row, row],
        out_shape=[jax.ShapeDtypeStruct((t, d), F32), jax.ShapeDtypeStruct((t, d), F32),
                   jax.ShapeDtypeStruct((t, d), BF16)],
        compiler_params=_params("parallel"),
    )(y3, g_out, hin, g, b)


def _out_bwd(ds_bf, g_out, *, tm, name):
    t, d = ds_bf.shape
    r = g_out.shape[1]

    def body(a_ref, w_ref, o_ref):
        o_ref[...] = _dot_nt(a_ref[...], w_ref[...].reshape(N_DEV * r, d))

    return pl.pallas_call(
        body, name=name, grid=(t // tm,),
        in_specs=[pl.BlockSpec((tm, d), lambda i: (i, 0)),
                  pl.BlockSpec((N_DEV, r, d), lambda i: (0, 0, 0))],
        out_specs=pl.BlockSpec((tm, N_DEV * r), lambda i: (i, 0)),
        out_shape=jax.ShapeDtypeStruct((t, N_DEV * r), F32),
        compiler_params=_params("parallel"),
    )(ds_bf, g_out)


def _in_bwd(dact, g_in, add, *, tm, name, alpha=DN_ALPHA, transposed=False):
    t = dact.shape[-2]
    _, k, n = g_in.shape
    if transposed:
        k, n = n, k
    halves = dact.shape[0] == 2 and dact.ndim == 3
    per = N_DEV // 2

    def body(a_ref, b_ref, add_ref, o_ref, acc_ref):
        j = pl.program_id(1)

        @pl.when(j == 0)
        def _():
            acc_ref[...] = alpha * add_ref[...]

        acc_ref[...] += _dot_nn(a_ref[...], b_ref[...]) if transposed else _dot_nt(a_ref[...], b_ref[...])

        @pl.when(j == N_DEV - 1)
        def _():
            o_ref[...] = acc_ref[...]

    if halves:
        a_spec = pl.BlockSpec((None, tm, n), lambda i, j: (j // per, i, j % per))
    elif dact.ndim == 4:
        a_spec = pl.BlockSpec((None, None, tm, n), lambda i, j: (j // per, j % per, i, 0))
    else:
        a_spec = pl.BlockSpec((None, tm, n), lambda i, j: (j, i, 0))
    return pl.pallas_call(
        body, name=name, grid=(t // tm, N_DEV),
        in_specs=[a_spec, pl.BlockSpec((None,) + g_in.shape[1:], lambda i, j: (j, 0, 0)),
                  pl.BlockSpec((tm, k), lambda i, j: (i, 0))],
        out_specs=pl.BlockSpec((tm, k), lambda i, j: (i, 0)),
        out_shape=jax.ShapeDtypeStruct((t, k), F32),
        scratch_shapes=[pltpu.VMEM((tm, k), F32)],
        compiler_params=_params("parallel", "arbitrary"),
    )(dact, g_in, add)


def _mm_nt_full(a, b, add, *, tm, name):
    t, n = a.shape
    k = b.shape[0]

    def body(a_ref, b_ref, add_ref, o_ref):
        o_ref[...] = add_ref[...] + _dot_nt(a_ref[...], b_ref[...])

    return pl.pallas_call(
        body, name=name, grid=(t // tm,),
        in_specs=[pl.BlockSpec((tm, n), lambda i: (i, 0)), pl.BlockSpec((k, n), lambda i: (0, 0)),
                  pl.BlockSpec((tm, k), lambda i: (i, 0))],
        out_specs=pl.BlockSpec((tm, k), lambda i: (i, 0)),
        out_shape=jax.ShapeDtypeStruct((t, k), F32),
        compiler_params=_params("parallel"),
    )(a, b, add)


def _w_in_grad(h_bf, dact, *, name, transposed=False):
    t, k = h_bf.shape
    halves = dact.shape[0] == 2 and dact.ndim == 3
    per = N_DEV // 2
    n = dact.shape[-1] // per if halves else dact.shape[-1]

    def body(a_ref, b_ref, o_ref):
        if transposed:
            o_ref[...] = _dot_tn(b_ref[...], a_ref[...]).astype(BF16)
        else:
            o_ref[...] = _dot_tn(a_ref[...], b_ref[...]).astype(BF16)

    if halves:
        b_spec = pl.BlockSpec((None, t, n), lambda j: (j // per, 0, j % per))
    elif dact.ndim == 4:
        b_spec = pl.BlockSpec((None, None, t, n), lambda j: (j // per, j % per, 0, 0))
    else:
        b_spec = pl.BlockSpec((None, t, n), lambda j: (j, 0, 0))
    return pl.pallas_call(
        body, name=name, grid=(N_DEV,),
        in_specs=[pl.BlockSpec((t, k), lambda j: (0, 0)), b_spec],
        out_specs=pl.BlockSpec((None, n, k) if transposed else (None, k, n), lambda j: (j, 0, 0)),
        out_shape=jax.ShapeDtypeStruct((N_DEV, n, k) if transposed else (N_DEV, k, n), BF16),
        compiler_params=_params("parallel"),
    )(h_bf, dact)


def _w_out_grad(y3, ds_bf, r, *, name):
    nj, t, kj = y3.shape
    d = ds_bf.shape[1]
    unit = r * LANES // math.gcd(r, LANES)
    ks = max([c for c in range(unit, min(kj, 768) + 1, unit) if kj % c == 0], default=kj)
    gsz = ks // r
    per = kj // ks

    def body(a_ref, b_ref, o_ref):
        o_ref[...] = _dot_tn(a_ref[...], b_ref[...]).reshape(gsz, r, d).astype(BF16)

    return pl.pallas_call(
        body, name=name, grid=(nj * per,),
        in_specs=[pl.BlockSpec((None, t, ks), lambda j: (j // per, 0, j % per)),
                  pl.BlockSpec((t, d), lambda j: (0, 0))],
        out_specs=pl.BlockSpec((gsz, r, d), lambda j: (j, 0, 0)),
        out_shape=jax.ShapeDtypeStruct((N_DEV, r, d), BF16),
        compiler_params=_params("parallel"),
    )(y3, ds_bf)


def _mm_tn_cols(a, b, *, tn, name):
    t, m = a.shape
    n = b.shape[1]

    def body(a_ref, b_ref, o_ref):
        o_ref[...] = _dot_tn(a_ref[...], b_ref[...])

    return pl.pallas_call(
        body, name=name, grid=(n // tn,),
        in_specs=[pl.BlockSpec((t, m), lambda j: (0, 0)), pl.BlockSpec((t, tn), lambda j: (0, j))],
        out_specs=pl.BlockSpec((m, tn), lambda j: (0, j)),
        out_shape=jax.ShapeDtypeStruct((m, n), F32),
        compiler_params=_params("parallel"),
    )(a, b)


def _ln_bwd(dout, s, g, *, tm, name):
    t, d = s.shape

    def body(do_ref, s_ref, g_ref, ds_ref, dsb_ref, gb_ref):
        i = pl.program_id(0)
        sv = s_ref[...]
        do = do_ref[...]
        mu = jnp.mean(sv, axis=-1, keepdims=True)
        xc = sv - mu
        var = jnp.mean(xc * xc, axis=-1, keepdims=True)
        rstd = lax.rsqrt(var + LN_EPS)
        xhat = xc * rstd
        dxhat = do * g_ref[...]
        m1 = jnp.mean(dxhat, axis=-1, keepdims=True)
        m2 = jnp.mean(dxhat * xhat, axis=-1, keepdims=True)
        ds = rstd * (dxhat - m1 - xhat * m2)
        ds_ref[...] = ds
        dsb_ref[...] = ds.astype(BF16)
        upd = _rows8([jnp.sum(do * xhat, axis=0, keepdims=True), jnp.sum(do, axis=0, keepdims=True)], d)

        @pl.when(i == 0)
        def _():
            gb_ref[...] = upd

        @pl.when(i > 0)
        def _():
            gb_ref[...] += upd

    row = pl.BlockSpec((tm, d), lambda i: (i, 0))
    return pl.pallas_call(
        body, name=name, grid=(t // tm,),
        in_specs=[row, row, pl.BlockSpec((1, d), lambda i: (0, 0))],
        out_specs=[row, row, pl.BlockSpec((8, d), lambda i: (0, 0))],
        out_shape=[jax.ShapeDtypeStruct((t, d), F32), jax.ShapeDtypeStruct((t, d), BF16),
                   jax.ShapeDtypeStruct((8, d), F32)],
        compiler_params=_params("arbitrary"),
    )(dout, s, g)


def _roll_down(x, s):
    return x if s == 0 else pltpu.roll(x, s, 0)


def _conv_taps(x, wb, width):
    y = jnp.broadcast_to(wb[width:width + 1, :], x.shape)
    for k in range(width):
        y = y + _roll_down(x, width - 1 - k) * wb[k:k + 1, :]
    return y


def _conv_taps_bwd(dy, x, wb, width):
    n = dy.shape[0]
    dx = jnp.zeros_like(dy)
    rows = []
    for k in range(width):
        s = width - 1 - k
        dy_up = dy if s == 0 else pltpu.roll(dy, n - s, 0)
        dx = dx + dy_up * wb[k:k + 1, :]
        rows.append(jnp.sum(dy_up * x, axis=0, keepdims=True))
    rows.append(jnp.sum(dy, axis=0, keepdims=True))
    t_idx = lax.broadcasted_iota(jnp.int32, dy.shape, 0)
    return jnp.where(t_idx < n - (width - 1), dx, 0.0), _rows8(rows, dy.shape[1])


def _convglu_fwd(z3, fwb3, *, name):
    _, t, n = z3.shape
    half = N_DEV // 2
    nc = pl.cdiv(n, LANES)

    def body(zg_ref, zv_ref, wg_ref, wv_ref, y_ref):
        gate = _conv_taps(zg_ref[...], wg_ref[...], 3)
        val = _conv_taps(zv_ref[...], wv_ref[...], 3)
        y_ref[...] = (_gelu(gate) * val).astype(BF16)

    zblk = lambda off: pl.BlockSpec((None, t, LANES), lambda j, c: (j + off, 0, c))
    wblk = lambda off: pl.BlockSpec((None, 8, LANES), lambda j, c: (j + off, 0, c))
    return pl.pallas_call(
        body, name=name, grid=(half, nc),
        in_specs=[zblk(0), zblk(half), wblk(0), wblk(half)],
        out_specs=zblk(0),
        out_shape=jax.ShapeDtypeStruct((half, t, n), BF16),
        compiler_params=_params("parallel", "parallel"),
    )(z3, z3, fwb3, fwb3)


def _ffn_bwd_mid(ds_bf, g_out, z3, fwb3, *, name):
    t, d = ds_bf.shape
    r = g_out.shape[1]
    n = z3.shape[2]
    half = N_DEV // 2
    nc = pl.cdiv(n, LANES)
    assert n == 2 * r

    def body(ds_ref, w_ref, zg_ref, zv_ref, wg_ref, wv_ref, dz_ref, dwb_ref, wsc_ref):
        c = pl.program_id(1)

        @pl.when(c == 0)
        def _():
            wsc_ref[0:r, :] = w_ref[0]
            wsc_ref[r:2 * r, :] = w_ref[1]
            if nc * LANES > n:
                wsc_ref[n:nc * LANES, :] = jnp.zeros((nc * LANES - n, d), BF16)

        w = wsc_ref[pl.ds(pl.multiple_of(c * LANES, LANES), LANES), :]
        dyf = _dot_nt(ds_ref[...], w)
        zg, zv = zg_ref[...], zv_ref[...]
        wg, wv = wg_ref[...], wv_ref[...]
        gate = _conv_taps(zg, wg, 3)
        val = _conv_taps(zv, wv, 3)
        gl, dgl = _gelu_and_grad(gate)
        dzg, dwg = _conv_taps_bwd(dyf * val * dgl, zg, wg, 3)
        dzv, dwv = _conv_taps_bwd(dyf * gl, zv, wv, 3)
        dz_ref[0] = dzg.astype(BF16)
        dz_ref[1] = dzv.astype(BF16)
        dwb_ref[0] = dwg
        dwb_ref[1] = dwv

    zblk = lambda off: pl.BlockSpec((None, t, LANES), lambda j, c: (j + off, 0, c))
    wblk = lambda off: pl.BlockSpec((None, 8, LANES), lambda j, c: (j + off, 0, c))
    return pl.pallas_call(
        body, name=name, grid=(half, nc),
        in_specs=[pl.BlockSpec((t, d), lambda j, c: (0, 0)),
                  pl.BlockSpec((2, r, d), lambda j, c: (j, 0, 0)),
                  zblk(0), zblk(half), wblk(0), wblk(half)],
        out_specs=[pl.BlockSpec((2, None, t, LANES), lambda j, c: (0, j, 0, c)),
                   pl.BlockSpec((2, None, 8, LANES), lambda j, c: (0, j, 0, c))],
        out_shape=[jax.ShapeDtypeStruct((2, half, t, n), BF16), jax.ShapeDtypeStruct((2, half, 8, n), F32)],
        scratch_shapes=[pltpu.VMEM((nc * LANES, d), BF16)],
        compiler_params=_params("parallel", "arbitrary"),
    )(ds_bf, g_out, z3, z3, fwb3, fwb3)


def _conv_a_fwd(gr, cwb, *, cb, name):
    t, r2 = gr.shape
    r = r2 // 2
    nb = r // cb

    def body(x_ref, w_ref, o_ref):
        o_ref[...] = _conv_taps(x_ref[...], w_ref[...], 4)

    return pl.pallas_call(
        body, name=name, grid=(nb,),
        in_specs=[pl.BlockSpec((t, cb), lambda j: (0, j + nb)), pl.BlockSpec((8, cb), lambda j: (0, j))],
        out_specs=pl.BlockSpec((t, cb), lambda j: (0, j)),
        out_shape=jax.ShapeDtypeStruct((t, r), F32),
        compiler_params=_params("parallel"),
    )(gr, cwb)


def _gates_fwd(rec, bd_r, bd_i, vecs, *, tm, name):
    t, r_dim = rec.shape
    nb, cb, _ = bd_r.shape

    def body(x_ref, wr_ref, wi_ref, v_ref, a_ref, u_ref, r_ref, i_ref):
        x = x_ref[...]
        xb = x.astype(BF16)
        v = v_ref[...]
        r = _sigmoid(_dot_nn(xb, wr_ref[...]) + v[0:1, :])
        i = _sigmoid(_dot_nn(xb, wi_ref[...]) + v[1:2, :])
        log_a = (-LRU_C) * r * _softplus(-v[2:3, :])
        a_ref[...] = jnp.exp(log_a)
        u_ref[...] = jnp.sqrt(-_expm1(2.0 * log_a)) * (i * x)
        r_ref[...] = r
        i_ref[...] = i

    blk = pl.BlockSpec((tm, cb), lambda j, i: (i, j))
    wspec = pl.BlockSpec((None, cb, cb), lambda j, i: (j, 0, 0))
    out = jax.ShapeDtypeStruct((t, r_dim), F32)
    return pl.pallas_call(
        body, name=name, grid=(nb, t // tm),
        in_specs=[blk, wspec, wspec, pl.BlockSpec((8, cb), lambda j, i: (0, j))],
        out_specs=[blk, blk, blk, blk],
        out_shape=[out, out, out, out],
        compiler_params=_params("parallel", "parallel"),
    )(rec, bd_r, bd_i, vecs)


def _scan_fwd(a, u, gr, *, cb, name):
    t, r = a.shape
    nb = r // cb
    seg = t // SUBLANES

    def body(a_ref, u_ref, g_ref, h_ref, y_ref, p_ref):
        def step(k, carry):
            h, p = carry
            rows = pl.ds(k, SUBLANES, stride=seg)
            av = a_ref[rows, :]
            h = av * h + u_ref[rows, :]
            p = av * p
            h_ref[rows, :] = h
            p_ref[rows, :] = p
            return h, p

        h_fin, p_fin = lax.fori_loop(0, seg, step, (jnp.zeros((SUBLANES, cb), F32), jnp.ones((SUBLANES, cb), F32)),
                                     unroll=8)
        carry = h_fin[0:1, :]
        for s in range(1, SUBLANES):
            rows = slice(s * seg, (s + 1) * seg)
            h_ref[rows, :] = h_ref[rows, :] + p_ref[rows, :] * carry
            carry = h_fin[s:s + 1, :] + p_fin[s:s + 1, :] * carry
        y_ref[...] = (_gelu(g_ref[...]) * h_ref[...]).astype(BF16)

    blk = pl.BlockSpec((t, cb), lambda j: (0, j))
    return pl.pallas_call(
        body, name=name, grid=(nb,),
        in_specs=[blk, blk, blk],
        out_specs=[blk, pl.BlockSpec((None, t, cb), lambda j: (0, 0, j))],
        out_shape=[jax.ShapeDtypeStruct((t, r), F32), jax.ShapeDtypeStruct((1, t, r), BF16)],
        scratch_shapes=[pltpu.VMEM((t, cb), F32)],
        compiler_params=_params("parallel"),
    )(a, u, gr)


def _scan_bwd(dy, gr, hr, a, *, cb, name):
    t, r = a.shape
    nb = r // cb
    seg = t // SUBLANES

    def body(dy_ref, g_ref, h_ref, a_ref, dh_ref, da_ref, dg_ref, q_ref):
        gl, dgl = _gelu_and_grad(g_ref[...])
        dyv = dy_ref[...]
        dh_ref[...] = dyv * gl
        dg_ref[...] = (dyv * h_ref[...] * dgl).astype(BF16)

        def step(k, carry):
            cin, q = carry
            rows = pl.ds(seg - 1 - k, SUBLANES, stride=seg)
            dh = dh_ref[rows, :] + cin
            dh_ref[rows, :] = dh
            q_ref[rows, :] = q
            av = a_ref[rows, :]
            return av * dh, av * q

        c_fin, q_fin = lax.fori_loop(0, seg, step, (jnp.zeros((SUBLANES, cb), F32), jnp.ones((SUBLANES, cb), F32)),
                                     unroll=8)
        carry = c_fin[SUBLANES - 1:SUBLANES, :]
        for s in range(SUBLANES - 2, -1, -1):
            rows = slice(s * seg, (s + 1) * seg)
            dh_ref[rows, :] = dh_ref[rows, :] + q_ref[rows, :] * carry
            carry = c_fin[s:s + 1, :] + q_fin[s:s + 1, :] * carry
        da_ref[...] = dh_ref[...] * _shift_down(h_ref[...], 1)

    blk = pl.BlockSpec((t, cb), lambda j: (0, j))
    return pl.pallas_call(
        body, name=name, grid=(nb,),
        in_specs=[blk, blk, blk, blk],
        out_specs=[blk, blk, blk],
        out_shape=[jax.ShapeDtypeStruct((t, r), F32), jax.ShapeDtypeStruct((t, r), F32),
                   jax.ShapeDtypeStruct((t, r), BF16)],
        scratch_shapes=[pltpu.VMEM((t, cb), F32)],
        compiler_params=_params("parallel"),
    )(dy, gr, hr, a)


def _gates_bwd(rec, r, i, a, dh, da, bd_r, bd_i, vecs, *, tm, name):
    t, r_dim = rec.shape
    nb, cb, _ = bd_r.shape

    def body(x_ref, r_ref, i_ref, a_ref, dh_ref, da_ref, wr_ref, wi_ref, v_ref, dx_ref, dpr_ref, dpi_ref, dv_ref):
        step = pl.program_id(1)
        x, r, i, a, dh, da = x_ref[...], r_ref[...], i_ref[...], a_ref[...], dh_ref[...], da_ref[...]
        lam = v_ref[...][2:3, :]
        sp = _softplus(-lam)
        a2 = a * a
        mult = jnp.sqrt(-_expm1(2.0 * (-LRU_C) * r * sp))
        d_i = dh * mult * x
        d_log_a = da * a - (dh * i * x) * a2 / mult
        d_r = d_log_a * ((-LRU_C) * sp)
        d_sp = jnp.sum(d_log_a * ((-LRU_C) * r), axis=0, keepdims=True)
        d_pre_r = d_r * r * (1.0 - r)
        d_pre_i = d_i * i * (1.0 - i)
        dprb = d_pre_r.astype(BF16)
        dpib = d_pre_i.astype(BF16)
        dx_ref[...] = dh * mult * i + _dot_nt(dprb, wr_ref[...]) + _dot_nt(dpib, wi_ref[...])
        dpr_ref[...] = dprb
        dpi_ref[...] = dpib
        upd = _rows8([jnp.sum(d_pre_r, axis=0, keepdims=True), jnp.sum(d_pre_i, axis=0, keepdims=True),
                      -d_sp * _sigmoid(-lam)], cb)

        @pl.when(step == 0)
        def _():
            dv_ref[...] = upd

        @pl.when(step > 0)
        def _():
            dv_ref[...] += upd

    blk = pl.BlockSpec((tm, cb), lambda j, i: (i, j))
    wspec = pl.BlockSpec((None, cb, cb), lambda j, i: (j, 0, 0))
    vspec = pl.BlockSpec((8, cb), lambda j, i: (0, j))
    return pl.pallas_call(
        body, name=name, grid=(nb, t // tm),
        in_specs=[blk] * 6 + [wspec, wspec, vspec],
        out_specs=[blk, blk, blk, vspec],
        out_shape=[jax.ShapeDtypeStruct((t, r_dim), F32), jax.ShapeDtypeStruct((t, r_dim), BF16),
                   jax.ShapeDtypeStruct((t, r_dim), BF16), jax.ShapeDtypeStruct((8, r_dim), F32)],
        compiler_params=_params("parallel", "arbitrary"),
    )(rec, r, i, a, dh, da, bd_r, bd_i, vecs)


def _bd_grad(rec, dpr, dpi, *, cb, name):
    t, r = rec.shape
    nb = r // cb

    def body(x_ref, dr_ref, di_ref, gr_ref, gi_ref):
        xb = x_ref[...].astype(BF16)
        gr_ref[...] = _dot_tn(xb, dr_ref[...])
        gi_ref[...] = _dot_tn(xb, di_ref[...])

    blk = pl.BlockSpec((t, cb), lambda j: (0, j))
    wspec = pl.BlockSpec((None, cb, cb), lambda j: (j, 0, 0))
    out = jax.ShapeDtypeStruct((nb, cb, cb), F32)
    return pl.pallas_call(
        body, name=name, grid=(nb,),
        in_specs=[blk, blk, blk], out_specs=[wspec, wspec], out_shape=[out, out],
        compiler_params=_params("parallel"),
    )(rec, dpr, dpi)


def _conv_a_bwd(d_rec, gr, dgate, cwb, *, cb, name):
    t, r = d_rec.shape
    nb = r // cb

    def body(dy_ref, x_ref, dg_ref, w_ref, dact_ref, dw_ref):
        dx, dw = _conv_taps_bwd(dy_ref[...], x_ref[...], w_ref[...], 4)
        dact_ref[0] = dg_ref[...]
        dact_ref[1] = dx.astype(BF16)
        dw_ref[...] = dw

    blk = pl.BlockSpec((t, cb), lambda j: (0, j))
    vspec = pl.BlockSpec((8, cb), lambda j: (0, j))
    return pl.pallas_call(
        body, name=name, grid=(nb,),
        in_specs=[blk, pl.BlockSpec((t, cb), lambda j: (0, j + nb)), blk, vspec],
        out_specs=[pl.BlockSpec((2, t, cb), lambda j: (0, 0, j)), vspec],
        out_shape=[jax.ShapeDtypeStruct((2, t, r), BF16), jax.ShapeDtypeStruct((8, r), F32)],
        compiler_params=_params("parallel"),
    )(d_rec, gr, dgate, cwb)


def _split3(x):
    p0 = x.astype(BF16)
    r1 = x - p0.astype(F32)
    p1 = r1.astype(BF16)
    p2 = (r1 - p1.astype(F32)).astype(BF16)
    return p0, p1, p2


def _fgate_fwd(fp, fb, *, tq, name):
    t = fp.shape[0]

    def body(f_ref, b_ref, c_ref, ct_ref):
        logf = -_softplus(-(f_ref[...] + b_ref[...]))
        rows = pl.program_id(0) * tq + lax.broadcasted_iota(jnp.int32, (tq, t), 0)
        cols = lax.broadcasted_iota(jnp.int32, (tq, t), 1)
        tri = (cols <= rows).astype(BF16)
        p0, p1, p2 = _split3(logf)
        c = _dot_nn(tri, p0) + _dot_nn(tri, p1) + _dot_nn(tri, p2)
        c_ref[...] = c
        ct_ref[...] = c.T

    return pl.pallas_call(
        body, name=name, grid=(t // tq,),
        in_specs=[pl.BlockSpec((t, LANES), lambda i: (0, 0)), pl.BlockSpec((1, LANES), lambda i: (0, 0))],
        out_specs=[pl.BlockSpec((tq, LANES), lambda i: (i, 0)), pl.BlockSpec((LANES, tq), lambda i: (0, i))],
        out_shape=[jax.ShapeDtypeStruct((t, LANES), F32), jax.ShapeDtypeStruct((LANES, t), F32)],
        compiler_params=_params("parallel"),
    )(fp, fb)


def _fgate_bwd(dct, fp, fb, *, tq, name):
    t = fp.shape[0]

    def body(d_ref, f_ref, b_ref, o_ref, db_ref):
        i = pl.program_id(0)
        rows = lax.broadcasted_iota(jnp.int32, (t, tq), 0)
        cols = i * tq + lax.broadcasted_iota(jnp.int32, (t, tq), 1)
        tri = (rows >= cols).astype(BF16)
        p0, p1, p2 = _split3(d_ref[...])
        dlogf = (_dot_nn(p0, tri) + _dot_nn(p1, tri) + _dot_nn(p2, tri)).T
        df = dlogf * _sigmoid(-(f_ref[...] + b_ref[...]))
        o_ref[...] = df.astype(BF16)
        upd = _rows8([jnp.sum(df, axis=0, keepdims=True)], LANES)

        @pl.when(i == 0)
        def _():
            db_ref[...] = upd

        @pl.when(i > 0)
        def _():
            db_ref[...] += upd

    return pl.pallas_call(
        body, name=name, grid=(t // tq,),
        in_specs=[pl.BlockSpec((LANES, t), lambda i: (0, 0)), pl.BlockSpec((tq, LANES), lambda i: (i, 0)),
                  pl.BlockSpec((1, LANES), lambda i: (0, 0))],
        out_specs=[pl.BlockSpec((tq, LANES), lambda i: (i, 0)), pl.BlockSpec((8, LANES), lambda i: (0, 0))],
        out_shape=[jax.ShapeDtypeStruct((t, LANES), BF16), jax.ShapeDtypeStruct((8, LANES), F32)],
        compiler_params=_params("arbitrary"),
    )(dct, fp, fb)


def _pair_sum(a, b, *, tm, name):
    t, d = a.shape

    def body(a_ref, b_ref, o_ref):
        o_ref[...] = (a_ref[...] + b_ref[...]).astype(BF16)

    row = pl.BlockSpec((tm, d), lambda i: (i, 0))
    return pl.pallas_call(
        body, name=name, grid=(t // tm,), in_specs=[row, row], out_specs=row,
        out_shape=jax.ShapeDtypeStruct((t, d), BF16), compiler_params=_params("parallel"),
    )(a, b)


FWD_HEAD_TILES = 2
BWD_HEAD_TILES = 1


def _head_block_width(dh, tiles):
    return tiles * LANES if tiles * LANES // dh <= 8 else LANES


def _head_masks(dh, bw):
    lane = lax.broadcasted_iota(jnp.int32, (1, bw), 1)
    return [((lane >= e * dh) & (lane < (e + 1) * dh)) for e in range(bw // dh)]


def _head_c_row(ct_blk, head):
    sub = lax.broadcasted_iota(jnp.int32, ct_blk.shape, 0)
    return jnp.sum(jnp.where(sub == head, ct_blk, 0.0), axis=0, keepdims=True)


def _attn_weights(qm, k, c_row, q0):
    tq, t = qm.shape[0], k.shape[0]
    s = _dot_nt(qm, k) - c_row
    qi = q0 + lax.broadcasted_iota(jnp.int32, (tq, t), 0)
    ki = lax.broadcasted_iota(jnp.int32, (tq, t), 1)
    s = jnp.where(ki <= qi, s, -jnp.inf)
    m = jnp.max(s, axis=-1, keepdims=True)
    e = jnp.exp(s - m)
    return e, m, 1.0 / jnp.sum(e, axis=-1, keepdims=True)


def _key_buckets(t, tq):
    return tuple(sorted({min(-(-(i * tq) // LANES) * LANES, t) for i in range(1, t // tq + 1)}))


def _for_prefix(needed, buckets, fn):
    prev = 0
    for length in buckets:
        pl.when((needed > prev) & (needed <= length))(lambda length=length: fn(length))
        prev = length


def _attn_fwd(qg, kv, ct, *, tq, name):
    t, d2 = qg.shape
    d = d2 // 2
    dh = d // N_HEADS
    bw = _head_block_width(dh, FWD_HEAD_TILES)
    hpb = bw // dh
    nhb = d // bw
    scale = dh ** -0.5
    buckets = _key_buckets(t, tq)

    def body(q_ref, og_ref, k_ref, v_ref, ct_ref, o_ref, y_ref, st_ref):
        hb = pl.program_id(0)
        q0 = pl.program_id(1) * tq

        def run(length):
            qs = q_ref[...] * scale
            k = k_ref[0:length, :]
            v = v_ref[0:length, :]
            o = jnp.zeros((tq, bw), F32)
            lane = lax.broadcasted_iota(jnp.int32, (tq, LANES), 1)
            stats = jnp.zeros((tq, LANES), F32)
            for e, msk in enumerate(_head_masks(dh, bw)):
                c_row = _head_c_row(ct_ref[:, 0:length], hb * hpb + e)
                w, m, inv = _attn_weights(jnp.where(msk, qs, 0.0).astype(BF16), k, c_row, q0)
                o = o + _dot_nn(w.astype(BF16), jnp.where(msk, v, jnp.zeros_like(v))) * inv
                stats = jnp.where(lane == e, m, jnp.where(lane == hpb + e, inv, stats))
            o_ref[...] = o
            y_ref[...] = (o * _sigmoid(og_ref[...])).astype(BF16)
            st_ref[...] = stats

        _for_prefix(q0 + tq, buckets, run)

    qblk = pl.BlockSpec((tq, bw), lambda h, i: (i, h))
    return pl.pallas_call(
        body, name=name, grid=(nhb, t // tq),
        in_specs=[qblk, pl.BlockSpec((tq, bw), lambda h, i: (i, h + nhb)),
                  pl.BlockSpec((t, bw), lambda h, i: (0, h)), pl.BlockSpec((t, bw), lambda h, i: (0, h + nhb)),
                  pl.BlockSpec((N_HEADS, t), lambda h, i: (0, 0))],
        out_specs=[qblk, pl.BlockSpec((None, tq, bw), lambda h, i: (0, i, h)),
                   pl.BlockSpec((None, tq, LANES), lambda h, i: (h, i, 0))],
        out_shape=[jax.ShapeDtypeStruct((t, d), F32), jax.ShapeDtypeStruct((1, t, d), BF16),
                   jax.ShapeDtypeStruct((nhb, t, LANES), F32)],
        compiler_params=_params("parallel", "parallel"),
    )(qg, qg, kv, kv, ct)


def _attn_bwd(dy, qg, o, stats, kv, ct, *, tq, name):
    t, d2 = qg.shape
    d = d2 // 2
    dh = d // N_HEADS
    bw = _head_block_width(dh, BWD_HEAD_TILES)
    hpb = bw // dh
    nhb = d // bw
    scale = dh ** -0.5
    n_q = t // tq
    hpb_f = _head_block_width(dh, FWD_HEAD_TILES) // dh
    ratio = hpb_f // hpb
    chunk = 8 * LANES

    def body(dy_ref, q_ref, og_ref, o_ref, st_ref, k_ref, v_ref, ct_ref, dqg_ref, dk_ref, dv_ref, dc_ref, dcq_ref):
        hb = pl.program_id(0)
        step = pl.program_id(1)

        @pl.when(step == 0)
        def _():
            dk_ref[...] = jnp.zeros((t, bw), F32)
            dv_ref[...] = jnp.zeros((t, bw), F32)
            dc_ref[...] = jnp.zeros((8, t), F32)

        def run(i):
            q0 = i * tq
            length = min(-(-(q0 + tq) // LANES) * LANES, t)
            qs = q_ref[...] * scale
            sg = _sigmoid(og_ref[...])
            dyv = dy_ref[...]
            ov = o_ref[...]
            do = dyv * sg
            dqg_ref[1] = (dyv * ov * sg * (1.0 - sg)).astype(BF16)
            lane = lax.broadcasted_iota(jnp.int32, (tq, LANES), 1)
            stats = st_ref[...]
            masks = _head_masks(dh, bw)
            heads = []
            for e, msk in enumerate(masks):
                pos = (hb % ratio) * hpb + e
                m = jnp.sum(jnp.where(lane == pos, stats, 0.0), axis=1, keepdims=True)
                inv = jnp.sum(jnp.where(lane == hpb_f + pos, stats, 0.0), axis=1, keepdims=True)
                delta = jnp.sum(jnp.where(msk, do * ov, 0.0), axis=1, keepdims=True)
                heads.append((msk, m, inv, delta, jnp.where(msk, qs, 0.0).astype(BF16),
                              jnp.where(msk, do, 0.0).astype(BF16)))
            dq = jnp.zeros((tq, bw), F32)
            dcq = jnp.zeros((tq, LANES), F32)
            row_acc = [jnp.zeros((tq, chunk), F32) for _ in heads]
            for c0 in range(0, length, chunk):
                ch = min(chunk, length - c0)
                k = k_ref[c0:c0 + ch, :]
                v = v_ref[c0:c0 + ch, :]
                dk = jnp.zeros((ch, bw), F32)
                dv = jnp.zeros((ch, bw), F32)
                dc_rows = []
                for e, (msk, m, inv, delta, qm, dom) in enumerate(heads):
                    c_row = _head_c_row(ct_ref[:, c0:c0 + ch], hb * hpb + e)
                    s = _dot_nt(qm, k) - c_row
                    if c0 + ch - 1 > q0:
                        qi = q0 + lax.broadcasted_iota(jnp.int32, (tq, ch), 0)
                        ki = c0 + lax.broadcasted_iota(jnp.int32, (tq, ch), 1)
                        s = jnp.where(ki <= qi, s, -jnp.inf)
                    p = jnp.exp(s - m) * inv
                    dsc = p * (_dot_nt(dom, v) - delta)
                    dsb = dsc.astype(BF16)
                    dq = dq + _dot_nn(dsb, jnp.where(msk, k, jnp.zeros_like(k)))
                    dk = dk + _dot_tn(dsb, qm)
                    dv = dv + _dot_tn(p.astype(BF16), dom)
                    dc_rows.append(-jnp.sum(dsc, axis=0, keepdims=True))
                    if ch == chunk:
                        row_acc[e] = row_acc[e] + dsc
                    else:
                        dcq = dcq + jnp.where(lane == e, jnp.sum(dsc, axis=1, keepdims=True), 0.0)
                dk_ref[c0:c0 + ch, :] += dk
                dv_ref[c0:c0 + ch, :] += dv
                dc_ref[:, c0:c0 + ch] += _rows8(dc_rows, ch)
            dqg_ref[0] = (dq * scale).astype(BF16)
            for e in range(len(heads)):
                dcq = dcq + jnp.where(lane == e, jnp.sum(row_acc[e], axis=1, keepdims=True), 0.0)
            dcq_ref[...] = dcq

        for i in range(n_q):
            pl.when(step == i)(lambda i=i: run(i))

    qblk = pl.BlockSpec((tq, bw), lambda h, i: (i, h))
    kblk = pl.BlockSpec((t, bw), lambda h, i: (0, h))
    return pl.pallas_call(
        body, name=name, grid=(nhb, n_q),
        in_specs=[qblk, qblk, pl.BlockSpec((tq, bw), lambda h, i: (i, h + nhb)), qblk,
                  pl.BlockSpec((None, tq, LANES), lambda h, i: (h // ratio, i, 0)),
                  kblk, pl.BlockSpec((t, bw), lambda h, i: (0, h + nhb)),
                  pl.BlockSpec((N_HEADS, t), lambda h, i: (0, 0))],
        out_specs=[pl.BlockSpec((2, tq, bw), lambda h, i: (0, i, h)), kblk, kblk,
                   pl.BlockSpec((None, 8, t), lambda h, i: (h, 0, 0)),
                   pl.BlockSpec((None, tq, LANES), lambda h, i: (h, i, 0))],
        out_shape=[jax.ShapeDtypeStruct((2, t, d), BF16), jax.ShapeDtypeStruct((t, d), F32),
                   jax.ShapeDtypeStruct((t, d), F32), jax.ShapeDtypeStruct((nhb, 8, t), F32),
                   jax.ShapeDtypeStruct((nhb, t, LANES), F32)],
        compiler_params=_params("parallel", "arbitrary"),
    )(dy, qg, qg, o, stats, kv, kv, ct)


def _loss_bwd(h, tgt, *, lo, hi, tm, name):
    t, d = h.shape

    def body(h_ref, t_ref, l_ref, dy_ref):
        i = pl.program_id(0)
        rows = i * tm + lax.broadcasted_iota(jnp.int32, (tm, d), 0)
        err = jnp.where((rows >= lo) & (rows < hi), h_ref[...] - t_ref[...], 0.0)
        dy_ref[...] = err * (1.0 / d)
        part = jnp.sum(jnp.sum(err * err, axis=0, keepdims=True), axis=1, keepdims=True) * (0.5 / d)
        upd = jnp.broadcast_to(part, (8, LANES))

        @pl.when(i == 0)
        def _():
            l_ref[...] = upd

        @pl.when(i > 0)
        def _():
            l_ref[...] += upd

    row = pl.BlockSpec((tm, d), lambda i: (i, 0))
    return pl.pallas_call(
        body, name=name, grid=(t // tm,),
        in_specs=[row, row],
        out_specs=[pl.BlockSpec((8, LANES), lambda i: (0, 0)), row],
        out_shape=[jax.ShapeDtypeStruct((8, LANES), F32), jax.ShapeDtypeStruct((t, d), F32)],
        compiler_params=_params("arbitrary"),
    )(h, tgt)


def _adamw_math(w, gv, m, v):
    bc1 = 1.0 / (1.0 - ADAM_B1 ** ADAM_STEP)
    bc2 = 1.0 / (1.0 - ADAM_B2 ** ADAM_STEP)
    nm = ADAM_B1 * m + (1.0 - ADAM_B1) * gv
    nv = ADAM_B2 * v + (1.0 - ADAM_B2) * (gv * gv)
    delta = (-ADAM_LR) * ((nm * bc1) / (jnp.sqrt(nv * bc2) + ADAM_EPS) + ADAM_WD * w)
    return delta, nm, nv


def _adamw(w, g, m, v, *, name):
    r, c = w.shape
    tr = r
    for cand in (512, 256, 128, 64, 32, 16, 8):
        if r % cand == 0 and r > cand:
            tr = cand
            break

    def body(w_ref, g_ref, m_ref, v_ref, d_ref, nm_ref, nv_ref):
        d_ref[...], nm_ref[...], nv_ref[...] = _adamw_math(w_ref[...], g_ref[...], m_ref[...], v_ref[...])

    blk = pl.BlockSpec((tr, c), lambda i: (i, 0))
    out = jax.ShapeDtypeStruct((r, c), F32)
    return pl.pallas_call(
        body, name=name, grid=(r // tr,),
        in_specs=[blk] * 4, out_specs=[blk] * 3, out_shape=[out] * 3,
        compiler_params=_params("parallel"),
    )(w, g, m, v)


def _sum_adamw(recvs, sends, me, w, m, v, *, name):
    n_l = len(recvs)
    _, r, c = recvs[0].shape
    tr = _tile(r, (256, 192, 176, 128, 96, 64, 48, 32, 16))

    def body(me_ref, *refs):
        p_refs, own_refs = refs[:n_l], refs[n_l:2 * n_l]
        w_ref, m_ref, v_ref, g_ref, d_ref, nm_ref, nv_ref, acc_ref = refs[2 * n_l:]
        layer = pl.program_id(0)
        mine = me_ref[0]
        for k in range(n_l):
            @pl.when(layer == k)
            def _(k=k):
                acc_ref[...] = jnp.zeros((tr, c), F32)
                for dev in range(N_DEV):
                    @pl.when(mine == dev)
                    def _():
                        acc_ref[...] += own_refs[k][...].astype(F32)

                    @pl.when(mine != dev)
                    def _(dev=dev):
                        acc_ref[...] += p_refs[k][dev].astype(F32)
                acc = acc_ref[...]
                g_ref[...] = acc
                d_ref[...], nm_ref[...], nv_ref[...] = _adamw_math(w_ref[...], acc, m_ref[...], v_ref[...])

    p_specs = [pl.BlockSpec((N_DEV, tr, c), lambda l, i, me_ref, k=k: (0, jnp.where(l == k, i, 0), 0))
               for k in range(n_l)]
    own_specs = [pl.BlockSpec((None, tr, c), lambda l, i, me_ref, k=k: (me_ref[0], jnp.where(l == k, i, 0), 0))
                 for k in range(n_l)]
    blk = pl.BlockSpec((None, tr, c), lambda l, i, me_ref: (l, i, 0))
    out = jax.ShapeDtypeStruct((n_l, r, c), F32)
    return pl.pallas_call(
        body, name=name,
        grid_spec=pltpu.PrefetchScalarGridSpec(
            num_scalar_prefetch=1, grid=(n_l, r // tr),
            in_specs=p_specs + own_specs + [blk] * 3, out_specs=[blk] * 4,
            scratch_shapes=[pltpu.VMEM((tr, c), F32)]),
        out_shape=[out] * 4,
        compiler_params=_params("arbitrary", "arbitrary"),
    )(me, *recvs, *sends, w, m, v)


def _sum8(parts, *, name):
    _, r, c = parts.shape
    tr = r
    for cand in (512, 256, 128, 64, 32, 16):
        if r % cand == 0 and r > cand:
            tr = cand
            break

    def body(p_ref, o_ref):
        acc = p_ref[0].astype(F32)
        for k in range(1, N_DEV):
            acc = acc + p_ref[k].astype(F32)
        o_ref[...] = acc

    return pl.pallas_call(
        body, name=name, grid=(r // tr,),
        in_specs=[pl.BlockSpec((N_DEV, tr, c), lambda i: (0, i, 0))],
        out_specs=pl.BlockSpec((tr, c), lambda i: (i, 0)),
        out_shape=jax.ShapeDtypeStruct((r, c), F32),
        compiler_params=_params("parallel"),
    )(parts)


def _my_index():
    return 4 * lax.axis_index("x") + 2 * lax.axis_index("y") + lax.axis_index("c")


def _peer(k):
    x, y, c = lax.axis_index("x"), lax.axis_index("y"), lax.axis_index("c")
    px = x ^ ((k >> 2) & 1)
    py = y ^ ((k >> 1) & 1)
    pc = c ^ (k & 1)
    return (px, py, pc), 4 * px + 2 * py + pc


def _all_gather(shards, *, name):
    n_arr = len(shards)

    def body(*refs):
        ins, outs = refs[:n_arr], refs[n_arr:2 * n_arr]
        send_sems, recv_sems, local_sems = refs[2 * n_arr:]
        me = _my_index()
        local = [pltpu.make_async_copy(ins[n], outs[n].at[me], local_sems.at[n]) for n in range(n_arr)]
        for cp in local:
            cp.start()
        sends = []
        for k in range(1, N_DEV):
            peer, _ = _peer(k)
            for n in range(n_arr):
                cp = pltpu.make_async_remote_copy(
                    src_ref=ins[n], dst_ref=outs[n].at[me], send_sem=send_sems.at[n, k - 1],
                    recv_sem=recv_sems.at[n, k - 1], device_id=peer, device_id_type=pl.DeviceIdType.MESH)
                cp.start()
                sends.append(cp)
        for k in range(1, N_DEV):
            peer, pidx = _peer(k)
            for n in range(n_arr):
                pltpu.make_async_remote_copy(
                    src_ref=ins[n], dst_ref=outs[n].at[pidx], send_sem=send_sems.at[n, k - 1],
                    recv_sem=recv_sems.at[n, k - 1], device_id=peer, device_id_type=pl.DeviceIdType.MESH).wait_recv()
        for cp in sends:
            cp.wait_send()
        for cp in local:
            cp.wait()

    hbm = pl.BlockSpec(memory_space=pl.ANY)
    return pl.pallas_call(
        body, name=name,
        in_specs=[hbm] * n_arr, out_specs=[hbm] * n_arr,
        out_shape=[jax.ShapeDtypeStruct((N_DEV,) + s.shape, s.dtype) for s in shards],
        scratch_shapes=[pltpu.SemaphoreType.DMA((n_arr, N_DEV - 1)), pltpu.SemaphoreType.DMA((n_arr, N_DEV - 1)),
                        pltpu.SemaphoreType.DMA((n_arr,))],
        compiler_params=pltpu.CompilerParams(has_side_effects=True),
    )(*shards)


_HBM = pl.BlockSpec(memory_space=pltpu.HBM)
_SEM = pl.BlockSpec(memory_space=pltpu.SEMAPHORE)
_EFFECT = pltpu.SideEffectType.DATAFLOW_SIDE_EFFECTING


def _remote(src, dst, send_sem, recv_sem, peer):
    return pltpu.make_async_remote_copy(src_ref=src, dst_ref=dst, send_sem=send_sem, recv_sem=recv_sem,
                                        device_id=peer, device_id_type=pl.DeviceIdType.MESH)


def _place_own(src, layer, me, *, out_dtype, name):
    _, r, c = src.shape
    tr = _tile(r, (256, 192, 176, 128, 96, 64, 48, 32, 16))

    def body(me_ref, s_ref, o_ref):
        o_ref[...] = s_ref[...].astype(out_dtype)

    return pl.pallas_call(
        body, name=name,
        grid_spec=pltpu.PrefetchScalarGridSpec(
            num_scalar_prefetch=1, grid=(r // tr,),
            in_specs=[pl.BlockSpec((None, tr, c), lambda i, me_ref: (layer, i, 0))],
            out_specs=pl.BlockSpec((None, tr, c), lambda i, me_ref: (me_ref[0], i, 0))),
        out_shape=jax.ShapeDtypeStruct((N_DEV, r, c), out_dtype),
        compiler_params=_params("parallel"),
    )(me, src)


def _own_blocks(srcs, *, name):
    n = len(srcs)

    def body(*refs):
        ins, outs, sems = refs[:n], refs[n:2 * n], refs[2 * n]
        me = _my_index()
        cps = [pltpu.make_async_copy(ins[t].at[me], outs[t].at[me], sems.at[t]) for t in range(n)]
        for cp in cps:
            cp.start()
        for cp in cps:
            cp.wait()

    return pl.pallas_call(
        body, name=name, in_specs=[_HBM] * n, out_specs=[_HBM] * n,
        out_shape=[jax.ShapeDtypeStruct(s.shape, s.dtype) for s in srcs],
        scratch_shapes=[pltpu.SemaphoreType.DMA((n,))],
    )(*srcs)


def _split_start(groups, *, scatter, name):
    sizes = [len(srcs) for srcs, _ in groups]
    flat_src = [s for srcs, _ in groups for s in srcs]
    flat_land = [l for _, lands in groups for l in lands]
    n, n_g = len(flat_land), len(groups)
    if not scatter:
        flat_src = []
    n_in = len(flat_src) + n

    def body(*refs):
        lands = refs[n_in - n:n_in]
        ins = refs[:n] if scatter else lands
        sems = refs[n_in:n_in + 2 * n_g]
        token = refs[-1]
        me = _my_index()
        t = 0
        for g in range(n_g):
            for q in range(sizes[g]):
                for k in range(1, N_DEV):
                    peer, pidx = _peer(k)
                    src = ins[t].at[pidx] if scatter else ins[t].at[me]
                    slot = q * (N_DEV - 1) + k - 1
                    _remote(src, lands[t].at[me], sems[2 * g].at[slot], sems[2 * g + 1].at[slot], peer).start()
                t += 1
        token[...] = jnp.zeros_like(token)

    sem_shapes = []
    for sz in sizes:
        sem_shapes += [pltpu.SemaphoreType.DMA((sz * (N_DEV - 1),)), pltpu.SemaphoreType.DMA((sz * (N_DEV - 1),))]
    outs = pl.pallas_call(
        body, name=name,
        in_specs=[_HBM] * n_in,
        out_specs=[_SEM] * (2 * n_g) + [_HBM] * n_in + [pl.BlockSpec(memory_space=pltpu.VMEM)],
        out_shape=sem_shapes + [pltpu.HBM(a.shape, a.dtype) for a in flat_src + flat_land]
        + [jax.ShapeDtypeStruct((8, LANES), F32)],
        input_output_aliases={i: 2 * n_g + i for i in range(n_in)},
        compiler_params=pltpu.CompilerParams(has_side_effects=_EFFECT),
    )(*[pltpu.with_memory_space_constraint(a, pltpu.HBM) for a in flat_src + flat_land])
    sems, thru, token = outs[:2 * n_g], outs[2 * n_g:2 * n_g + n_in], outs[-1]
    handles, pos = [], 0
    for g, sz in enumerate(sizes):
        lands_g = thru[n_in - n + pos:n_in - n + pos + sz]
        handles.append((sems[2 * g], sems[2 * g + 1], thru[pos:pos + sz] if scatter else [], lands_g))
        pos += sz
    return handles, token


def _split_wait(handle, after, *, scatter, name):
    send_sems, recv_sems, srcs, lands = handle
    n, n_src = len(lands), len(srcs)

    def body(*refs):
        lnd = refs[n_src:n_src + n]
        ins = refs[:n_src] if scatter else lnd
        ssem, rsem = refs[n_src + n], refs[n_src + n + 1]
        me = _my_index()
        for t in range(n):
            for k in range(1, N_DEV):
                peer, pidx = _peer(k)
                block = ins[t].at[me]
                slot = t * (N_DEV - 1) + k - 1
                _remote(block, lnd[t].at[me], ssem.at[slot], rsem.at[slot], peer).wait_send()
                _remote(block, lnd[t].at[pidx], ssem.at[slot], rsem.at[slot], peer).wait_recv()

    return pl.pallas_call(
        body, name=name,
        in_specs=[_HBM] * (n_src + n) + [_SEM, _SEM, pl.BlockSpec(memory_space=pl.ANY)],
        out_specs=[_HBM] * n,
        out_shape=[pltpu.HBM(l.shape, l.dtype) for l in lands],
        input_output_aliases={n_src + t: t for t in range(n)},
        compiler_params=pltpu.CompilerParams(has_side_effects=_EFFECT),
    )(*srcs, *lands, send_sems, recv_sems, after)


def _pack(arrs, dtype, row_quantum=16):
    flat = jnp.concatenate([a.astype(dtype).reshape(-1) for a in arrs])
    pad = (-flat.shape[0]) % (row_quantum * PACK_COLS)
    if pad:
        flat = jnp.concatenate([flat, jnp.zeros((pad,), dtype)])
    return flat.reshape(-1, PACK_COLS)


def _pack8(arrs, dtype):
    flat = jnp.concatenate([a.astype(dtype).reshape(N_DEV, -1) for a in arrs], axis=1)
    pad = (-flat.shape[1]) % (16 * PACK_COLS)
    if pad:
        flat = jnp.concatenate([flat, jnp.zeros((N_DEV, pad), dtype)], axis=1)
    return flat.reshape(N_DEV, -1, PACK_COLS)


def _unpack(slab, shapes, lead):
    lead_shape = slab.shape[:lead]
    flat = slab.reshape(lead_shape + (-1,))
    outs, off = [], 0
    for shp in shapes:
        size = math.prod(shp)
        outs.append(flat[..., off:off + size].reshape(lead_shape + tuple(shp)))
        off += size
    return outs


def _cols_full(g):
    g = jnp.moveaxis(g, 0, -2)
    return g.reshape(g.shape[:-2] + (g.shape[-2] * g.shape[-1],))


def _cols_split(full):
    n = full.shape[-1] // N_DEV
    return jnp.moveaxis(full.reshape(full.shape[:-1] + (N_DEV, n)), -2, 0)


def _block_diag(w, per):
    n, b, _ = w.shape
    w4 = w.reshape(n // per, per, b, b)
    eye = jnp.eye(per, dtype=w.dtype)
    return jnp.einsum('gpab,pq->gpaqb', w4, eye).reshape(n // per, per * b, per * b)


def _block_diag_extract(g, per):
    gn, cb, _ = g.shape
    b = cb // per
    g5 = g.reshape(gn, per, b, per, b)
    return jnp.stack([g5[:, p, :, p, :] for p in range(per)], axis=1).reshape(gn * per, b, b)


def _slab2d(a):
    return a.reshape(-1, a.shape[-1])


def _lru_block_cols(r_dim):
    lru = r_dim // N_LRU_BLOCKS
    return lru * LANES // math.gcd(lru, LANES)


BIG = ("a_w_in", "a_w_out", "b_w_in", "b_w_out", "f_w_in", "f_w_out")
COL_F32 = ("meta", "a_conv_w", "a_conv_b", "a_b_r", "a_b_i", "a_lambda", "f_conv_w")
REPLICATED = ("a_w_r", "a_w_i", "kv_f_b", "f_conv_b", "ln1_g", "ln1_b", "ln2_g", "ln2_b")
WEIGHT_NAMES = ("meta", "a_w_in", "a_conv_w", "a_conv_b", "a_w_r", "a_b_r", "a_w_i", "a_b_i", "a_lambda", "a_w_out",
                "kv_w", "kv_f_b", "b_w_in", "b_w_out", "f_w_in", "f_conv_w", "f_conv_b", "f_w_out",
                "ln1_g", "ln1_b", "ln2_g", "ln2_b")


def _kv_layout(kv_gathered, d):
    kv_full = _cols_full(kv_gathered)
    kv_pad = 2 * d + LANES - kv_full.shape[1]
    return jnp.concatenate([kv_full, jnp.zeros((d, kv_pad), kv_full.dtype)], axis=1)


def _small_layouts(small):
    r_dim = small["a_lambda"].shape[1]
    n_f = small["f_conv_b"].shape[1] // N_DEV
    cb = _lru_block_cols(r_dim)
    per = cb // (r_dim // N_LRU_BLOCKS)
    n_a = small["a_lambda"].shape[0]
    f_conv_w3 = small["f_conv_w"].reshape(N_LAYERS, 3, N_DEV, n_f).transpose(0, 2, 1, 3)
    f_conv_b3 = small["f_conv_b"].reshape(N_LAYERS, N_DEV, 1, n_f)
    return {
        "kv_fb": jnp.concatenate([small["kv_f_b"], jnp.zeros((LANES - N_HEADS,), F32)])[None],
        "a_cwb": jnp.concatenate([small["a_conv_w"], small["a_conv_b"][:, None],
                                  jnp.zeros((n_a, 3, r_dim), F32)], axis=1),
        "a_vecs": jnp.concatenate([jnp.stack([small["a_b_r"], small["a_b_i"], small["a_lambda"]], axis=1),
                                   jnp.zeros((n_a, 5, r_dim), F32)], axis=1),
        "a_bd_r": jnp.stack([_block_diag(small["a_w_r"][l], per) for l in range(n_a)]).astype(BF16),
        "a_bd_i": jnp.stack([_block_diag(small["a_w_i"][l], per) for l in range(n_a)]).astype(BF16),
        "f_cwb3": jnp.concatenate([f_conv_w3, f_conv_b3, jnp.zeros((N_LAYERS, N_DEV, 4, n_f), F32)], axis=2),
        "ln1_g": small["ln1_g"][:, None], "ln1_b": small["ln1_b"][:, None],
        "ln2_g": small["ln2_g"][:, None], "ln2_b": small["ln2_b"][:, None],
    }


def _local_step(h0, tgt, n_meta, n_tok, wts, hooks):
    tp, d = h0.shape
    tm = tp // 8 if (tp // 8) % 16 == 0 else tp
    tmb = _tile(tp, (1088, 512, 320, 256, 128))
    tq = 128
    tqa_fwd = tp // 4 if tp % 64 == 0 else tq
    tqa_bwd = tp // 2 if tp % 64 == 0 else tq
    r_dim = wts["a_vecs"].shape[2]
    cb = wts["a_bd_r"].shape[-1]
    sb = LANES
    n_b = N_LAYERS - N_A_LAYERS

    h, h_bf = h0, h0.astype(BF16)
    saved = []
    kvs = None
    for layer in range(N_LAYERS):
        lw = {}
        sv = {"h_bf": h_bf, "w": lw}
        if layer < N_A_LAYERS:
            lw["in"] = hooks.weight(layer, "in", h)
            sv["gr"] = _proj_in(h_bf, lw["in"], shard_major=False, name="a_in_proj")
            sv["rec"] = _conv_a_fwd(sv["gr"], wts["a_cwb"][layer], cb=cb, name="a_conv_fwd")
            a, u, sv["r"], sv["i"] = _gates_fwd(sv["rec"], wts["a_bd_r"][layer], wts["a_bd_i"][layer],
                                                wts["a_vecs"][layer], tm=tm, name="a_gates_fwd")
            sv["a"] = a
            sv["hr"], y3 = _scan_fwd(a, u, sv["gr"], cb=sb, name="a_scan_fwd")
        else:
            j = layer - N_A_LAYERS
            if j == 0:
                kv_w = _kv_layout(hooks.weight(layer, "kv_w", h), d)
                kvs = {"h_bf": h_bf, "w": kv_w}
                kvs["kv"] = _mm_nn(h_bf, kv_w[:, :2 * d], tn=_tile(2 * d, (512, 256, 128)), out_dtype=BF16,
                                   name="kv_proj")
                kvs["fp"] = _mm_nn(h_bf, kv_w[:, 2 * d:], tn=LANES, out_dtype=F32, name="f_proj")
                kvs["c"], ct = _fgate_fwd(kvs["fp"], wts["kv_fb"], tq=tq, name="fgate_fwd")
                kvs["ct"] = ct[:N_HEADS]
            lw["in"] = hooks.weight(layer, "in", kvs["c"] if j == 0 else h)
            sv["qg"] = _proj_in(h_bf, lw["in"], shard_major=False, name="b_in_proj")
            sv["o"], y3, sv["st"] = _attn_fwd(sv["qg"], kvs["kv"], kvs["ct"], tq=tqa_fwd, name="attn_fwd")
        sv["y3"] = y3
        lw["out"] = hooks.weight(layer, "out", y3)
        sv["s1"], h, h_bf = _out_ln(y3, lw["out"], h, wts["ln1_g"][layer], wts["ln1_b"][layer], n_valid=n_tok,
                                    tm=tmb // 2, name="mix_out_ln")
        sv["h1_bf"] = h_bf
        lw["f_in"] = hooks.weight(layer, "f_in", h)
        sv["z3"] = _proj_in(h_bf, lw["f_in"], shard_major=True, transposed=True, name="f_in_proj")
        sv["yf3"] = _convglu_fwd(sv["z3"], wts["f_cwb3"][layer], name="f_convglu_fwd")
        lw["f_out"] = hooks.weight(layer, "f_out", sv["yf3"])
        sv["s2"], h, h_bf = _out_ln(sv["yf3"], lw["f_out"], h, wts["ln2_g"][layer], wts["ln2_b"][layer],
                                    n_valid=n_tok, tm=tmb // 2, name="ffn_out_ln")
        saved.append(sv)

    loss_tile, dh = _loss_bwd(h, tgt, lo=n_meta, hi=n_tok, tm=tm, name="loss")

    grads = {k: [None] * N_LAYERS for k in ("f_cwb3", "ln1_gb", "ln2_gb")}
    grads.update({k: [None] * N_A_LAYERS for k in ("a_cwb", "a_bd_r", "a_bd_i", "a_vecs")})
    dkv = []
    token = jnp.zeros((), F32)
    for layer in reversed(range(N_LAYERS)):
        sv = saved[layer]
        lw = sv["w"]
        big = {}
        ds, ds_bf, grads["ln2_gb"][layer] = _ln_bwd(dh, sv["s2"], wts["ln2_g"][layer] + token, tm=tm, name="ln_bwd")
        dz, dcw = _ffn_bwd_mid(ds_bf, lw["f_out"], sv["z3"], wts["f_cwb3"][layer], name="f_bwd_mid")
        grads["f_cwb3"][layer] = dcw.reshape((N_DEV,) + dcw.shape[2:])
        dz3 = dz
        big["f_out"] = _w_out_grad(sv["yf3"], ds_bf, lw["f_out"].shape[1], name="f_w_out_grad")
        dh = _in_bwd(dz3, lw["f_in"], ds, tm=tmb, transposed=True, name="f_in_bwd")
        big["f_in"] = _w_in_grad(sv["h1_bf"], dz3, transposed=True, name="f_w_in_grad")
        token = hooks.grads_ready(layer, "ffn", big)
        big = {}
        ds, ds_bf, grads["ln1_gb"][layer] = _ln_bwd(dh, sv["s1"], wts["ln1_g"][layer] + token, tm=tm, name="ln_bwd")
        if layer < N_A_LAYERS:
            dy = _out_bwd(ds_bf, lw["out"], tm=tmb // 2, name="a_out_bwd")
            big["out"] = _w_out_grad(sv["y3"], ds_bf, lw["out"].shape[1], name="a_w_out_grad")
            d_h, d_a, dgate = _scan_bwd(dy, sv["gr"], sv["hr"], sv["a"], cb=sb, name="a_scan_bwd")
            d_rec, dpr, dpi, grads["a_vecs"][layer] = _gates_bwd(
                sv["rec"], sv["r"], sv["i"], sv["a"], d_h, d_a, wts["a_bd_r"][layer], wts["a_bd_i"][layer],
                wts["a_vecs"][layer], tm=tm, name="a_gates_bwd")
            grads["a_bd_r"][layer], grads["a_bd_i"][layer] = _bd_grad(sv["rec"], dpr, dpi, cb=cb, name="a_bd_grad")
            dact, grads["a_cwb"][layer] = _conv_a_bwd(d_rec, sv["gr"], dgate, wts["a_cwb"][layer], cb=cb,
                                                      name="a_conv_bwd")
            dh = _in_bwd(dact, lw["in"], ds, tm=tmb, name="a_in_bwd")
            big["in"] = _w_in_grad(sv["h_bf"], dact, name="a_w_in_grad")
        else:
            j = layer - N_A_LAYERS
            dy = _out_bwd(ds_bf, lw["out"], tm=tmb // 2, name="b_out_bwd")
            big["out"] = _w_out_grad(sv["y3"], ds_bf, lw["out"].shape[1], name="b_w_out_grad")
            dqg, dk, dv, dc, dcq = _attn_bwd(dy, sv["qg"], sv["o"], sv["st"], kvs["kv"], kvs["ct"], tq=tqa_bwd,
                                             name="attn_bwd")
            dkv.append((dk, dv, dc, dcq))
            dh = _in_bwd(dqg, lw["in"], ds, tm=tmb, name="b_in_bwd")
            big["in"] = _w_in_grad(sv["h_bf"], dqg, name="b_w_in_grad")
            if j == 0:
                hpb = _head_block_width(d // N_HEADS, BWD_HEAD_TILES) // (d // N_HEADS)
                dct = (dkv[0][2] + dkv[1][2])[:, :hpb, :].reshape(N_HEADS, tp)
                dcq = (dkv[0][3] + dkv[1][3])[:, :, :hpb]
                dct = dct + jnp.transpose(dcq, (0, 2, 1)).reshape(N_HEADS, tp)
                dct = jnp.concatenate([dct, jnp.zeros((LANES - N_HEADS, tp), F32)])
                df_bf, grads["kv_fb"] = _fgate_bwd(dct, kvs["fp"], wts["kv_fb"], tq=tq, name="fgate_bwd")
                dkvz = jnp.concatenate([_pair_sum(dkv[0][0], dkv[1][0], tm=tm, name="kv_pair_sum"),
                                        _pair_sum(dkv[0][1], dkv[1][1], tm=tm, name="kv_pair_sum"), df_bf], axis=1)
                dh = _mm_nt_full(dkvz, kvs["w"], dh, tm=tmb // 2, name="kv_in_bwd")
                big["kv_w"] = _mm_tn_cols(kvs["h_bf"], dkvz, tn=LANES, name="kv_w_grad")
        token = hooks.grads_ready(layer, "mix", big)
    return loss_tile, dh, grads


def _finish_small_grads(grads, d_h0, n_meta):
    r_dim = grads["a_vecs"][0].shape[1]
    per = _lru_block_cols(r_dim) // (r_dim // N_LRU_BLOCKS)
    a_cwb = jnp.stack(grads["a_cwb"])
    a_vecs = jnp.stack(grads["a_vecs"])
    f_cwb3 = jnp.stack(grads["f_cwb3"])
    ln1 = jnp.stack(grads["ln1_gb"])
    ln2 = jnp.stack(grads["ln2_gb"])
    f_rows = f_cwb3.transpose(0, 2, 1, 3).reshape(N_LAYERS, 8, -1)
    return {
        "meta": d_h0[:n_meta],
        "a_conv_w": a_cwb[:, :4], "a_conv_b": a_cwb[:, 4],
        "a_w_r": jnp.stack([_block_diag_extract(g, per) for g in grads["a_bd_r"]]),
        "a_b_r": a_vecs[:, 0],
        "a_w_i": jnp.stack([_block_diag_extract(g, per) for g in grads["a_bd_i"]]),
        "a_b_i": a_vecs[:, 1], "a_lambda": a_vecs[:, 2],
        "kv_f_b": grads["kv_fb"][0, :N_HEADS],
        "f_conv_w": f_rows[:, :3], "f_conv_b": f_rows[:, 3],
        "ln1_g": ln1[:, 0], "ln1_b": ln1[:, 1], "ln2_g": ln2[:, 0], "ln2_b": ln2[:, 1],
    }


def kernel(x, meta, a_w_in, a_conv_w, a_conv_b, a_w_r, a_b_r, a_w_i, a_b_i, a_lambda, a_w_out, kv_w, kv_f_b, b_w_in, b_w_out, f_w_in, f_conv_w, f_conv_b, f_w_out, ln1_g, ln1_b, ln2_g, ln2_b, loss_target, m_meta, m_a_w_in, m_a_conv_w, m_a_conv_b, m_a_w_r, m_a_b_r, m_a_w_i, m_a_b_i, m_a_lambda, m_a_w_out, m_kv_w, m_kv_f_b, m_b_w_in, m_b_w_out, m_f_w_in, m_f_conv_w, m_f_conv_b, m_f_w_out, m_ln1_g, m_ln1_b, m_ln2_g, m_ln2_b, v_meta, v_a_w_in, v_a_conv_w, v_a_conv_b, v_a_w_r, v_a_b_r, v_a_w_i, v_a_b_i, v_a_lambda, v_a_w_out, v_kv_w, v_kv_f_b, v_b_w_in, v_b_w_out, v_f_w_in, v_f_conv_w, v_f_conv_b, v_f_w_out, v_ln1_g, v_ln1_b, v_ln2_g, v_ln2_b):
    w = dict(meta=meta, a_w_in=a_w_in, a_conv_w=a_conv_w, a_conv_b=a_conv_b, a_w_r=a_w_r, a_b_r=a_b_r, a_w_i=a_w_i,
             a_b_i=a_b_i, a_lambda=a_lambda, a_w_out=a_w_out, kv_w=kv_w, kv_f_b=kv_f_b, b_w_in=b_w_in,
             b_w_out=b_w_out, f_w_in=f_w_in, f_conv_w=f_conv_w, f_conv_b=f_conv_b, f_w_out=f_w_out, ln1_g=ln1_g,
             ln1_b=ln1_b, ln2_g=ln2_g, ln2_b=ln2_b)
    m = dict(meta=m_meta, a_w_in=m_a_w_in, a_conv_w=m_a_conv_w, a_conv_b=m_a_conv_b, a_w_r=m_a_w_r, a_b_r=m_a_b_r,
             a_w_i=m_a_w_i, a_b_i=m_a_b_i, a_lambda=m_a_lambda, a_w_out=m_a_w_out, kv_w=m_kv_w, kv_f_b=m_kv_f_b,
             b_w_in=m_b_w_in, b_w_out=m_b_w_out, f_w_in=m_f_w_in, f_conv_w=m_f_conv_w, f_conv_b=m_f_conv_b,
             f_w_out=m_f_w_out, ln1_g=m_ln1_g, ln1_b=m_ln1_b, ln2_g=m_ln2_g, ln2_b=m_ln2_b)
    v = dict(meta=v_meta, a_w_in=v_a_w_in, a_conv_w=v_a_conv_w, a_conv_b=v_a_conv_b, a_w_r=v_a_w_r, a_b_r=v_a_b_r,
             a_w_i=v_a_w_i, a_b_i=v_a_b_i, a_lambda=v_a_lambda, a_w_out=v_a_w_out, kv_w=v_kv_w, kv_f_b=v_kv_f_b,
             b_w_in=v_b_w_in, b_w_out=v_b_w_out, f_w_in=v_f_w_in, f_conv_w=v_f_conv_w, f_conv_b=v_f_conv_b,
             f_w_out=v_f_w_out, ln1_g=v_ln1_g, ln1_b=v_ln1_b, ln2_g=v_ln2_g, ln2_b=v_ln2_b)
    shapes = {n: w[n].shape for n in WEIGHT_NAMES}

    me = jnp.reshape(_my_index(), (1,)).astype(jnp.int32)

    def as_stored(name, a):
        return jnp.swapaxes(a, 1, 2) if name == "f_w_in" else a

    param_of = {"in": ("a_w_in", "b_w_in"), "out": ("a_w_out", "b_w_out"), "f_in": ("f_w_in",) * 2,
                "f_out": ("f_w_out",) * 2}
    order = [("small", None, None)]
    for layer in range(N_LAYERS):
        if layer == N_A_LAYERS:
            order.append(("kv_w", layer, 0))
        for key in ("in", "out", "f_in", "f_out"):
            order.append((key, layer, layer if key[0] == "f" or layer < N_A_LAYERS else layer - N_A_LAYERS))
    def place(key, layer, idx):
        if key == "small":
            return _place_own(_pack([w[n] for n in COL_F32], F32)[None], 0, me, out_dtype=F32, name="place_small")
        if key == "kv_w":
            return _place_own(w["kv_w"][None], 0, me, out_dtype=BF16, name="place_kv_w")
        name = param_of[key][0 if layer < N_A_LAYERS else 1]
        return _place_own(as_stored(name, w[name]), idx, me, out_dtype=BF16, name=f"place_{name}_{idx}")

    lands = [place(*o) for o in order]
    gather_handles, gather_token = _split_start([([l], [l]) for l in lands], scatter=False, name="gather_start")
    group_of = {(key, layer): g for g, (key, layer, _) in enumerate(order)}
    (got_s,) = _split_wait(gather_handles[0], gather_token, scatter=False, name="gather_wait_small")
    small = {n: w[n] for n in REPLICATED}
    for n, part in zip(COL_F32, _unpack(got_s, [w[n].shape for n in COL_F32], 1)):
        small[n] = _cols_full(part)
    n_meta, d = small["meta"].shape

    class Hooks:
        pending = None
        received = {}
        sent = {}

        @staticmethod
        def weight(layer, key, after):
            (got,) = _split_wait(gather_handles[group_of[(key, layer)]], after, scatter=False,
                                 name=f"gather_wait_{key}_{layer}")
            return got

        @staticmethod
        def collect(after):
            if Hooks.pending is not None:
                tag, names, handle = Hooks.pending
                got = _split_wait(handle, after, scatter=True, name=f"scatter_wait_{tag}")
                Hooks.received.update(zip(names, got))
                Hooks.pending = None

        @staticmethod
        def grads_ready(layer, part, big):
            if "kv_w" in big:
                big["kv_w"] = _cols_split(big["kv_w"][:, :shapes["kv_w"][1] * N_DEV]).astype(BF16)
            names = [(key, layer) for key in big]
            send = [big[key] for key in big]
            Hooks.collect(send[0])
            empty = [lax.empty(s.shape, s.dtype) for s in send]
            handles, token = _split_start([(send, empty)], scatter=True, name=f"scatter_start_{part}_{layer}")
            Hooks.pending = (f"{part}_{layer}", names, handles[0])
            Hooks.sent.update(zip(names, handles[0][2]))
            return token[0, 0]

    Hooks.pending, Hooks.received, Hooks.sent = None, {}, {}

    n_tok = n_meta + x.shape[1]
    tp = -(-n_tok // ROW_ALIGN) * ROW_ALIGN
    pad = jnp.zeros((tp - n_tok, d), F32)
    h0 = jnp.concatenate([small["meta"], x[0], pad])
    tgt = jnp.concatenate([jnp.zeros((n_meta, d), F32), loss_target[0], pad])
    loss_tile, d_h0, grads = _local_step(h0, tgt, n_meta, n_tok, _small_layouts(small), Hooks)
    g_small = _finish_small_grads(grads, d_h0, n_meta)
    loss = lax.psum(loss_tile[0, 0], MESH_AXES)
    grad_x = d_h0[n_meta:n_tok][None]

    rep = _pack([g_small[n] for n in REPLICATED], F32, row_quantum=16 * N_DEV)
    send = [_pack8([_cols_split(g_small[n]) for n in COL_F32], F32), rep.reshape(N_DEV, -1, PACK_COLS)]
    lands = _own_blocks(send, name="scatter_own_small")
    handles, token = _split_start([(send, lands)], scatter=True, name="scatter_start_small")

    g, delta, new_m, new_v = {}, {}, {}, {}
    layers_of = {
        "a_w_in": [("in", l) for l in range(N_A_LAYERS)], "a_w_out": [("out", l) for l in range(N_A_LAYERS)],
        "b_w_in": [("in", l) for l in range(N_A_LAYERS, N_LAYERS)],
        "b_w_out": [("out", l) for l in range(N_A_LAYERS, N_LAYERS)],
        "f_w_in": [("f_in", l) for l in range(N_LAYERS)], "f_w_out": [("f_out", l) for l in range(N_LAYERS)],
        "kv_w": [("kv_w", N_A_LAYERS)],
    }
    ready = [n for n in BIG + ("kv_w",) if all(t in Hooks.received for t in layers_of[n])]

    def done(names):
        return jnp.stack([g[n][(0,) * g[n].ndim] for n in names])

    for n in ready + [n for n in BIG + ("kv_w",) if n not in ready]:
        if n not in ready and Hooks.pending is not None:
            Hooks.collect(done(ready))
        lift = (lambda a: a[None]) if n == "kv_w" else (lambda a, n=n: as_stored(n, a))
        outs = _sum_adamw([Hooks.received[t] for t in layers_of[n]], [Hooks.sent[t] for t in layers_of[n]], me,
                          lift(w[n]), lift(m[n]), lift(v[n]), name="sum_adamw_" + n)
        g[n], delta[n], new_m[n], new_v[n] = [as_stored(n, o).reshape(shapes[n]) for o in outs]
    recv_s, recv_r = _split_wait(handles[0], done(BIG + ("kv_w",)), scatter=True, name="scatter_wait_small")
    sum_s = _sum8(recv_s, name="sum_grads_f32")
    g.update(zip(COL_F32, _unpack(sum_s, [shapes[n] for n in COL_F32], 0)))
    (got_r,) = _all_gather([_sum8(recv_r, name="sum_grads_replicated")], name="gather_replicated_sums")
    g.update(zip(REPLICATED, _unpack(got_r.reshape(-1, PACK_COLS), [shapes[n] for n in REPLICATED], 0)))

    for n in COL_F32 + REPLICATED:
        shp = shapes[n]
        dl, nm, nv = _adamw(_slab2d(w[n]), _slab2d(g[n]), _slab2d(m[n]), _slab2d(v[n]), name="adamw")
        delta[n], new_m[n], new_v[n] = dl.reshape(shp), nm.reshape(shp), nv.reshape(shp)
    return (loss, grad_x, *[g[n] for n in WEIGHT_NAMES], *[delta[n] for n in WEIGHT_NAMES],
            *[new_m[n] for n in WEIGHT_NAMES], *[new_v[n] for n in WEIGHT_NAMES])
```

```python
import math

import jax
import jax.numpy as jnp
from jax import lax
from jax.experimental import pallas as pl
from jax.experimental.pallas import tpu as pltpu

F32 = jnp.float32
BF16 = jnp.bfloat16

N_DEV = 8
MESH_AXES = ("x", "y", "c")
N_LAYERS = 4
N_A_LAYERS = 2
N_LRU_BLOCKS = 16
N_HEADS = 16
LRU_C = 8.0
DN_ALPHA = (2 * N_LAYERS) ** 0.25
LN_EPS = 1e-5
ADAM_LR, ADAM_B1, ADAM_B2, ADAM_EPS, ADAM_WD, ADAM_STEP = 0.001, 0.9, 0.999, 1e-08, 0.01, 10

LANES = 128
SUBLANES = 8
ROW_ALIGN = 128
VMEM_LIMIT_BYTES = 56 * 1024 * 1024
GELU_K = math.sqrt(2.0 / math.pi)
GELU_C = 0.044715
PACK_COLS = 1024


def _params(*sem):
    return pltpu.CompilerParams(dimension_semantics=sem, vmem_limit_bytes=VMEM_LIMIT_BYTES)


def _gelu(x):
    th = jnp.tanh(GELU_K * (x + GELU_C * x * x * x))
    return 0.5 * x * (1.0 + th)


def _gelu_and_grad(x):
    x2 = x * x
    th = jnp.tanh(GELU_K * (x + GELU_C * x2 * x))
    g = 0.5 * x * (1.0 + th)
    dg = 0.5 * (1.0 + th) + 0.5 * x * (1.0 - th * th) * (GELU_K * (1.0 + 3.0 * GELU_C * x2))
    return g, dg


def _sigmoid(x):
    return 0.5 * jnp.tanh(0.5 * x) + 0.5


def _expm1(x):
    small = x * (1.0 + 0.5 * x * (1.0 + (1.0 / 3.0) * x * (1.0 + 0.25 * x)))
    return jnp.where(jnp.abs(x) < 1e-2, small, jnp.exp(x) - 1.0)


def _softplus(x):
    e = jnp.exp(-jnp.abs(x))
    small = e * (1.0 - 0.5 * e * (1.0 - (2.0 / 3.0) * e))
    return jnp.maximum(x, 0.0) + jnp.where(e < 1e-2, small, jnp.log(1.0 + e))


def _shift_down(x, s):
    if s == 0:
        return x
    rows = lax.broadcasted_iota(jnp.int32, x.shape, 0)
    return jnp.where(rows >= s, pltpu.roll(x, s, 0), 0.0)


def _shift_up(x, s):
    if s == 0:
        return x
    n = x.shape[0]
    rows = lax.broadcasted_iota(jnp.int32, x.shape, 0)
    return jnp.where(rows < n - s, pltpu.roll(x, n - s, 0), 0.0)


def _dot_nn(a, b):
    return lax.dot_general(a, b, (((1,), (0,)), ((), ())), preferred_element_type=F32)


def _dot_nt(a, b):
    return lax.dot_general(a, b, (((1,), (1,)), ((), ())), preferred_element_type=F32)


def _dot_tn(a, b):
    return lax.dot_general(a, b, (((0,), (0,)), ((), ())), preferred_element_type=F32)


def _rows8(vals, width):
    rows = lax.broadcasted_iota(jnp.int32, (8, width), 0)
    out = jnp.zeros((8, width), F32)
    for k, v in enumerate(vals):
        out = jnp.where(rows == k, jnp.broadcast_to(v, (8, width)), out)
    return out


def _tile(n, prefer):
    for c in prefer:
        if n % c == 0:
            return c
    return n


def _mm_nn(a, b, *, tn, out_dtype, name):
    m, k = a.shape
    n = b.shape[1]

    def body(a_ref, b_ref, o_ref):
        o_ref[...] = _dot_nn(a_ref[...], b_ref[...]).astype(o_ref.dtype)

    return pl.pallas_call(
        body, name=name, grid=(n // tn,),
        in_specs=[pl.BlockSpec((m, k), lambda j: (0, 0)), pl.BlockSpec((k, tn), lambda j: (0, j))],
        out_specs=pl.BlockSpec((m, tn), lambda j: (0, j)),
        out_shape=jax.ShapeDtypeStruct((m, n), out_dtype),
        compiler_params=_params("parallel"),
    )(a, b)


def _proj_in(h_bf, g_in, *, shard_major, name, transposed=False):
    t, k = h_bf.shape
    n = g_in.shape[1] if transposed else g_in.shape[2]

    def body(a_ref, b_ref, o_ref):
        o_ref[...] = _dot_nt(a_ref[...], b_ref[...]) if transposed else _dot_nn(a_ref[...], b_ref[...])

    if shard_major:
        out_spec = pl.BlockSpec((None, t, n), lambda j: (j, 0, 0))
        out_shape = jax.ShapeDtypeStruct((N_DEV, t, n), F32)
    else:
        out_spec = pl.BlockSpec((t, n), lambda j: (0, j))
        out_shape = jax.ShapeDtypeStruct((t, N_DEV * n), F32)
    return pl.pallas_call(
        body, name=name, grid=(N_DEV,),
        in_specs=[pl.BlockSpec((t, k), lambda j: (0, 0)),
                  pl.BlockSpec((None,) + g_in.shape[1:], lambda j: (j, 0, 0))],
        out_specs=out_spec, out_shape=out_shape,
        compiler_params=_params("parallel"),
    )(h_bf, g_in)


def _out_ln(y3, g_out, hin, g, b, *, n_valid, tm, name):
    nj, t, kj = y3.shape
    _, r, d = g_out.shape

    def body(y_ref, w_ref, hin_ref, g_ref, b_ref, s_ref, h_ref, hb_ref):
        w = w_ref[...].reshape(N_DEV * r, d)
        s = DN_ALPHA * hin_ref[...]
        for jj in range(nj):
            s = s + _dot_nn(y_ref[jj], w[jj * kj:(jj + 1) * kj])
        mu = jnp.mean(s, axis=-1, keepdims=True)
        xc = s - mu
        var = jnp.mean(xc * xc, axis=-1, keepdims=True)
        h = xc * lax.rsqrt(var + LN_EPS) * g_ref[...] + b_ref[...]
        s_ref[...] = s
        h_ref[...] = h
        rows = pl.program_id(0) * tm + lax.broadcasted_iota(jnp.int32, (tm, d), 0)
        hb_ref[...] = jnp.where(rows < n_valid, h, 0.0).astype(BF16)

    row = pl.BlockSpec((tm, d), lambda i: (i, 0))
    vec = pl.BlockSpec((1, d), lambda i: (0, 0))
    return pl.pallas_call(
        body, name=name, grid=(t // tm,),
        in_specs=[pl.BlockSpec((nj, tm, kj), lambda i: (0, i, 0)),
                  pl.BlockSpec((N_DEV, r, d), lambda i: (0, 0, 0)), row, vec, vec],
        out_specs=[row, row, row],
        out_shape=[jax.ShapeDtypeStruct((t, d), F32), jax.ShapeDtypeStruct((t, d), F32),
                   jax.ShapeDtypeStruct((t, d), BF16)],
        compiler_params=_params("parallel"),
    )(y3, g_out, hin, g, b)


def _out_bwd(ds_bf, g_out, *, tm, name):
    t, d = ds_bf.shape
    r = g_out.shape[1]

    def body(a_ref, w_ref, o_ref):
        o_ref[...] = _dot_nt(a_ref[...], w_ref[...].reshape(N_DEV * r, d))

    return pl.pallas_call(
        body, name=name, grid=(t // tm,),
        in_specs=[pl.BlockSpec((tm, d), lambda i: (i, 0)),
                  pl.BlockSpec((N_DEV, r, d), lambda i: (0, 0, 0))],
        out_specs=pl.BlockSpec((tm, N_DEV * r), lambda i: (i, 0)),
        out_shape=jax.ShapeDtypeStruct((t, N_DEV * r), F32),
        compiler_params=_params("parallel"),
    )(ds_bf, g_out)


def _in_bwd(dact, g_in, add, *, tm, name, alpha=DN_ALPHA, transposed=False):
    t = dact.shape[-2]
    _, k, n = g_in.shape
    if transposed:
        k, n = n, k
    halves = dact.shape[0] == 2 and dact.ndim == 3
    per = N_DEV // 2

    def body(a_ref, b_ref, add_ref, o_ref, acc_ref):
        j = pl.program_id(1)

        @pl.when(j == 0)
        def _():
            acc_ref[...] = alpha * add_ref[...]

        acc_ref[...] += _dot_nn(a_ref[...], b_ref[...]) if transposed else _dot_nt(a_ref[...], b_ref[...])

        @pl.when(j == N_DEV - 1)
        def _():
            o_ref[...] = acc_ref[...]

    if halves:
        a_spec = pl.BlockSpec((None, tm, n), lambda i, j: (j // per, i, j % per))
    elif dact.ndim == 4:
        a_spec = pl.BlockSpec((None, None, tm, n), lambda i, j: (j // per, j % per, i, 0))
    else:
        a_spec = pl.BlockSpec((None, tm, n), lambda i, j: (j, i, 0))
    return pl.pallas_call(
        body, name=name, grid=(t // tm, N_DEV),
        in_specs=[a_spec, pl.BlockSpec((None,) + g_in.shape[1:], lambda i, j: (j, 0, 0)),
                  pl.BlockSpec((tm, k), lambda i, j: (i, 0))],
        out_specs=pl.BlockSpec((tm, k), lambda i, j: (i, 0)),
        out_shape=jax.ShapeDtypeStruct((t, k), F32),
        scratch_shapes=[pltpu.VMEM((tm, k), F32)],
        compiler_params=_params("parallel", "arbitrary"),
    )(dact, g_in, add)


def _mm_nt_full(a, b, add, *, tm, name):
    t, n = a.shape
    k = b.shape[0]

    def body(a_ref, b_ref, add_ref, o_ref):
        o_ref[...] = add_ref[...] + _dot_nt(a_ref[...], b_ref[...])

    return pl.pallas_call(
        body, name=name, grid=(t // tm,),
        in_specs=[pl.BlockSpec((tm, n), lambda i: (i, 0)), pl.BlockSpec((k, n), lambda i: (0, 0)),
                  pl.BlockSpec((tm, k), lambda i: (i, 0))],
        out_specs=pl.BlockSpec((tm, k), lambda i: (i, 0)),
        out_shape=jax.ShapeDtypeStruct((t, k), F32),
        compiler_params=_params("parallel"),
    )(a, b, add)


def _w_in_grad(h_bf, dact, *, name, transposed=False):
    t, k = h_bf.shape
    halves = dact.shape[0] == 2 and dact.ndim == 3
    per = N_DEV // 2
    n = dact.shape[-1] // per if halves else dact.shape[-1]

    def body(a_ref, b_ref, o_ref):
        if transposed:
            o_ref[...] = _dot_tn(b_ref[...], a_ref[...]).astype(BF16)
        else:
            o_ref[...] = _dot_tn(a_ref[...], b_ref[...]).astype(BF16)

    if halves:
        b_spec = pl.BlockSpec((None, t, n), lambda j: (j // per, 0, j % per))
    elif dact.ndim == 4:
        b_spec = pl.BlockSpec((None, None, t, n), lambda j: (j // per, j % per, 0, 0))
    else:
        b_spec = pl.BlockSpec((None, t, n), lambda j: (j, 0, 0))
    return pl.pallas_call(
        body, name=name, grid=(N_DEV,),
        in_specs=[pl.BlockSpec((t, k), lambda j: (0, 0)), b_spec],
        out_specs=pl.BlockSpec((None, n, k) if transposed else (None, k, n), lambda j: (j, 0, 0)),
        out_shape=jax.ShapeDtypeStruct((N_DEV, n, k) if transposed else (N_DEV, k, n), BF16),
        compiler_params=_params("parallel"),
    )(h_bf, dact)


def _w_out_grad(y3, ds_bf, r, *, name):
    nj, t, kj = y3.shape
    d = ds_bf.shape[1]
    unit = r * LANES // math.gcd(r, LANES)
    ks = max([c for c in range(unit, min(kj, 768) + 1, unit) if kj % c == 0], default=kj)
    gsz = ks // r
    per = kj // ks

    def body(a_ref, b_ref, o_ref):
        o_ref[...] = _dot_tn(a_ref[...], b_ref[...]).reshape(gsz, r, d).astype(BF16)

    return pl.pallas_call(
        body, name=name, grid=(nj * per,),
        in_specs=[pl.BlockSpec((None, t, ks), lambda j: (j // per, 0, j % per)),
                  pl.BlockSpec((t, d), lambda j: (0, 0))],
        out_specs=pl.BlockSpec((gsz, r, d), lambda j: (j, 0, 0)),
        out_shape=jax.ShapeDtypeStruct((N_DEV, r, d), BF16),
        compiler_params=_params("parallel"),
    )(y3, ds_bf)


def _mm_tn_cols(a, b, *, tn, name):
    t, m = a.shape
    n = b.shape[1]

    def body(a_ref, b_ref, o_ref):
        o_ref[...] = _dot_tn(a_ref[...], b_ref[...])

    return pl.pallas_call(
        body, name=name, grid=(n // tn,),
        in_specs=[pl.BlockSpec((t, m), lambda j: (0, 0)), pl.BlockSpec((t, tn), lambda j: (0, j))],
        out_specs=pl.BlockSpec((m, tn), lambda j: (0, j)),
        out_shape=jax.ShapeDtypeStruct((m, n), F32),
        compiler_params=_params("parallel"),
    )(a, b)


def _ln_bwd(dout, s, g, *, tm, name):
    t, d = s.shape

    def body(do_ref, s_ref, g_ref, ds_ref, dsb_ref, gb_ref):
        i = pl.program_id(0)
        sv = s_ref[...]
        do = do_ref[...]
        mu = jnp.mean(sv, axis=-1, keepdims=True)
        xc = sv - mu
        var = jnp.mean(xc * xc, axis=-1, keepdims=True)
        rstd = lax.rsqrt(var + LN_EPS)
        xhat = xc * rstd
        dxhat = do * g_ref[...]
        m1 = jnp.mean(dxhat, axis=-1, keepdims=True)
        m2 = jnp.mean(dxhat * xhat, axis=-1, keepdims=True)
        ds = rstd * (dxhat - m1 - xhat * m2)
        ds_ref[...] = ds
        dsb_ref[...] = ds.astype(BF16)
        upd = _rows8([jnp.sum(do * xhat, axis=0, keepdims=True), jnp.sum(do, axis=0, keepdims=True)], d)

        @pl.when(i == 0)
        def _():
            gb_ref[...] = upd

        @pl.when(i > 0)
        def _():
            gb_ref[...] += upd

    row = pl.BlockSpec((tm, d), lambda i: (i, 0))
    return pl.pallas_call(
        body, name=name, grid=(t // tm,),
        in_specs=[row, row, pl.BlockSpec((1, d), lambda i: (0, 0))],
        out_specs=[row, row, pl.BlockSpec((8, d), lambda i: (0, 0))],
        out_shape=[jax.ShapeDtypeStruct((t, d), F32), jax.ShapeDtypeStruct((t, d), BF16),
                   jax.ShapeDtypeStruct((8, d), F32)],
        compiler_params=_params("arbitrary"),
    )(dout, s, g)


def _roll_down(x, s):
    return x if s == 0 else pltpu.roll(x, s, 0)


def _conv_taps(x, wb, width):
    y = jnp.broadcast_to(wb[width:width + 1, :], x.shape)
    for k in range(width):
        y = y + _roll_down(x, width - 1 - k) * wb[k:k + 1, :]
    return y


def _conv_taps_bwd(dy, x, wb, width):
    n = dy.shape[0]
    dx = jnp.zeros_like(dy)
    rows = []
    for k in range(width):
        s = width - 1 - k
        dy_up = dy if s == 0 else pltpu.roll(dy, n - s, 0)
        dx = dx + dy_up * wb[k:k + 1, :]
        rows.append(jnp.sum(dy_up * x, axis=0, keepdims=True))
    rows.append(jnp.sum(dy, axis=0, keepdims=True))
    t_idx = lax.broadcasted_iota(jnp.int32, dy.shape, 0)
    return jnp.where(t_idx < n - (width - 1), dx, 0.0), _rows8(rows, dy.shape[1])


def _convglu_fwd(z3, fwb3, *, name):
    _, t, n = z3.shape
    half = N_DEV // 2
    nc = pl.cdiv(n, LANES)

    def body(zg_ref, zv_ref, wg_ref, wv_ref, y_ref):
        gate = _conv_taps(zg_ref[...], wg_ref[...], 3)
        val = _conv_taps(zv_ref[...], wv_ref[...], 3)
        y_ref[...] = (_gelu(gate) * val).astype(BF16)

    zblk = lambda off: pl.BlockSpec((None, t, LANES), lambda j, c: (j + off, 0, c))
    wblk = lambda off: pl.BlockSpec((None, 8, LANES), lambda j, c: (j + off, 0, c))
    return pl.pallas_call(
        body, name=name, grid=(half, nc),
        in_specs=[zblk(0), zblk(half), wblk(0), wblk(half)],
        out_specs=zblk(0),
        out_shape=jax.ShapeDtypeStruct((half, t, n), BF16),
        compiler_params=_params("parallel", "parallel"),
    )(z3, z3, fwb3, fwb3)


def _ffn_bwd_mid(ds_bf, g_out, z3, fwb3, *, name):
    t, d = ds_bf.shape
    r = g_out.shape[1]
    n = z3.shape[2]
    half = N_DEV // 2
    nc = pl.cdiv(n, LANES)
    assert n == 2 * r

    def body(ds_ref, w_ref, zg_ref, zv_ref, wg_ref, wv_ref, dz_ref, dwb_ref, wsc_ref):
        c = pl.program_id(1)

        @pl.when(c == 0)
        def _():
            wsc_ref[0:r, :] = w_ref[0]
            wsc_ref[r:2 * r, :] = w_ref[1]
            if nc * LANES > n:
                wsc_ref[n:nc * LANES, :] = jnp.zeros((nc * LANES - n, d), BF16)

        w = wsc_ref[pl.ds(pl.multiple_of(c * LANES, LANES), LANES), :]
        dyf = _dot_nt(ds_ref[...], w)
        zg, zv = zg_ref[...], zv_ref[...]
        wg, wv = wg_ref[...], wv_ref[...]
        gate = _conv_taps(zg, wg, 3)
        val = _conv_taps(zv, wv, 3)
        gl, dgl = _gelu_and_grad(gate)
        dzg, dwg = _conv_taps_bwd(dyf * val * dgl, zg, wg, 3)
        dzv, dwv = _conv_taps_bwd(dyf * gl, zv, wv, 3)
        dz_ref[0] = dzg.astype(BF16)
        dz_ref[1] = dzv.astype(BF16)
        dwb_ref[0] = dwg
        dwb_ref[1] = dwv

    zblk = lambda off: pl.BlockSpec((None, t, LANES), lambda j, c: (j + off, 0, c))
    wblk = lambda off: pl.BlockSpec((None, 8, LANES), lambda j, c: (j + off, 0, c))
    return pl.pallas_call(
        body, name=name, grid=(half, nc),
        in_specs=[pl.BlockSpec((t, d), lambda j, c: (0, 0)),
                  pl.BlockSpec((2, r, d), lambda j, c: (j, 0, 0)),
                  zblk(0), zblk(half), wblk(0), wblk(half)],
        out_specs=[pl.BlockSpec((2, None, t, LANES), lambda j, c: (0, j, 0, c)),
                   pl.BlockSpec((2, None, 8, LANES), lambda j, c: (0, j, 0, c))],
        out_shape=[jax.ShapeDtypeStruct((2, half, t, n), BF16), jax.ShapeDtypeStruct((2, half, 8, n), F32)],
        scratch_shapes=[pltpu.VMEM((nc * LANES, d), BF16)],
        compiler_params=_params("parallel", "arbitrary"),
    )(ds_bf, g_out, z3, z3, fwb3, fwb3)


def _conv_a_fwd(gr, cwb, *, cb, name):
    t, r2 = gr.shape
    r = r2 // 2
    nb = r // cb

    def body(x_ref, w_ref, o_ref):
        o_ref[...] = _conv_taps(x_ref[...], w_ref[...], 4)

    return pl.pallas_call(
        body, name=name, grid=(nb,),
        in_specs=[pl.BlockSpec((t, cb), lambda j: (0, j + nb)), pl.BlockSpec((8, cb), lambda j: (0, j))],
        out_specs=pl.BlockSpec((t, cb), lambda j: (0, j)),
        out_shape=jax.ShapeDtypeStruct((t, r), F32),
        compiler_params=_params("parallel"),
    )(gr, cwb)


def _gates_fwd(rec, bd_r, bd_i, vecs, *, tm, name):
    t, r_dim = rec.shape
    nb, cb, _ = bd_r.shape

    def body(x_ref, wr_ref, wi_ref, v_ref, a_ref, u_ref, r_ref, i_ref):
        x = x_ref[...]
        xb = x.astype(BF16)
        v = v_ref[...]
        r = _sigmoid(_dot_nn(xb, wr_ref[...]) + v[0:1, :])
        i = _sigmoid(_dot_nn(xb, wi_ref[...]) + v[1:2, :])
        log_a = (-LRU_C) * r * _softplus(-v[2:3, :])
        a_ref[...] = jnp.exp(log_a)
        u_ref[...] = jnp.sqrt(-_expm1(2.0 * log_a)) * (i * x)
        r_ref[...] = r
        i_ref[...] = i

    blk = pl.BlockSpec((tm, cb), lambda j, i: (i, j))
    wspec = pl.BlockSpec((None, cb, cb), lambda j, i: (j, 0, 0))
    out = jax.ShapeDtypeStruct((t, r_dim), F32)
    return pl.pallas_call(
        body, name=name, grid=(nb, t // tm),
        in_specs=[blk, wspec, wspec, pl.BlockSpec((8, cb), lambda j, i: (0, j))],
        out_specs=[blk, blk, blk, blk],
        out_shape=[out, out, out, out],
        compiler_params=_params("parallel", "parallel"),
    )(rec, bd_r, bd_i, vecs)


def _scan_fwd(a, u, gr, *, cb, name):
    t, r = a.shape
    nb = r // cb
    seg = t // SUBLANES

    def body(a_ref, u_ref, g_ref, h_ref, y_ref, p_ref):
        def step(k, carry):
            h, p = carry
            rows = pl.ds(k, SUBLANES, stride=seg)
            av = a_ref[rows, :]
            h = av * h + u_ref[rows, :]
            p = av * p
            h_ref[rows, :] = h
            p_ref[rows, :] = p
            return h, p

        h_fin, p_fin = lax.fori_loop(0, seg, step, (jnp.zeros((SUBLANES, cb), F32), jnp.ones((SUBLANES, cb), F32)),
                                     unroll=8)
        carry = h_fin[0:1, :]
        for s in range(1, SUBLANES):
            rows = slice(s * seg, (s + 1) * seg)
            h_ref[rows, :] = h_ref[rows, :] + p_ref[rows, :] * carry
            carry = h_fin[s:s + 1, :] + p_fin[s:s + 1, :] * carry
        y_ref[...] = (_gelu(g_ref[...]) * h_ref[...]).astype(BF16)

    blk = pl.BlockSpec((t, cb), lambda j: (0, j))
    return pl.pallas_call(
        body, name=name, grid=(nb,),
        in_specs=[blk, blk, blk],
        out_specs=[blk, pl.BlockSpec((None, t, cb), lambda j: (0, 0, j))],
        out_shape=[jax.ShapeDtypeStruct((t, r), F32), jax.ShapeDtypeStruct((1, t, r), BF16)],
        scratch_shapes=[pltpu.VMEM((t, cb), F32)],
        compiler_params=_params("parallel"),
    )(a, u, gr)


def _scan_bwd(dy, gr, hr, a, *, cb, name):
    t, r = a.shape
    nb = r // cb
    seg = t // SUBLANES

    def body(dy_ref, g_ref, h_ref, a_ref, dh_ref, da_ref, dg_ref, q_ref):
        gl, dgl = _gelu_and_grad(g_ref[...])
        dyv = dy_ref[...]
        dh_ref[...] = dyv * gl
        dg_ref[...] = (dyv * h_ref[...] * dgl).astype(BF16)

        def step(k, carry):
            cin, q = carry
            rows = pl.ds(seg - 1 - k, SUBLANES, stride=seg)
            dh = dh_ref[rows, :] + cin
            dh_ref[rows, :] = dh
            q_ref[rows, :] = q
            av = a_ref[rows, :]
            return av * dh, av * q

        c_fin, q_fin = lax.fori_loop(0, seg, step, (jnp.zeros((SUBLANES, cb), F32), jnp.ones((SUBLANES, cb), F32)),
                                     unroll=8)
        carry = c_fin[SUBLANES - 1:SUBLANES, :]
        for s in range(SUBLANES - 2, -1, -1):
            rows = slice(s * seg, (s + 1) * seg)
            dh_ref[rows, :] = dh_ref[rows, :] + q_ref[rows, :] * carry
            carry = c_fin[s:s + 1, :] + q_fin[s:s + 1, :] * carry
        da_ref[...] = dh_ref[...] * _shift_down(h_ref[...], 1)

    blk = pl.BlockSpec((t, cb), lambda j: (0, j))
    return pl.pallas_call(
        body, name=name, grid=(nb,),
        in_specs=[blk, blk, blk, blk],
        out_specs=[blk, blk, blk],
        out_shape=[jax.ShapeDtypeStruct((t, r), F32), jax.ShapeDtypeStruct((t, r), F32),
                   jax.ShapeDtypeStruct((t, r), BF16)],
        scratch_shapes=[pltpu.VMEM((t, cb), F32)],
        compiler_params=_params("parallel"),
    )(dy, gr, hr, a)


def _gates_bwd(rec, r, i, a, dh, da, bd_r, bd_i, vecs, *, tm, name):
    t, r_dim = rec.shape
    nb, cb, _ = bd_r.shape

    def body(x_ref, r_ref, i_ref, a_ref, dh_ref, da_ref, wr_ref, wi_ref, v_ref, dx_ref, dpr_ref, dpi_ref, dv_ref):
        step = pl.program_id(1)
        x, r, i, a, dh, da = x_ref[...], r_ref[...], i_ref[...], a_ref[...], dh_ref[...], da_ref[...]
        lam = v_ref[...][2:3, :]
        sp = _softplus(-lam)
        a2 = a * a
        mult = jnp.sqrt(-_expm1(2.0 * (-LRU_C) * r * sp))
        d_i = dh * mult * x
        d_log_a = da * a - (dh * i * x) * a2 / mult
        d_r = d_log_a * ((-LRU_C) * sp)
        d_sp = jnp.sum(d_log_a * ((-LRU_C) * r), axis=0, keepdims=True)
        d_pre_r = d_r * r * (1.0 - r)
        d_pre_i = d_i * i * (1.0 - i)
        dprb = d_pre_r.astype(BF16)
        dpib = d_pre_i.astype(BF16)
        dx_ref[...] = dh * mult * i + _dot_nt(dprb, wr_ref[...]) + _dot_nt(dpib, wi_ref[...])
        dpr_ref[...] = dprb
        dpi_ref[...] = dpib
        upd = _rows8([jnp.sum(d_pre_r, axis=0, keepdims=True), jnp.sum(d_pre_i, axis=0, keepdims=True),
                      -d_sp * _sigmoid(-lam)], cb)

        @pl.when(step == 0)
        def _():
            dv_ref[...] = upd

        @pl.when(step > 0)
        def _():
            dv_ref[...] += upd

    blk = pl.BlockSpec((tm, cb), lambda j, i: (i, j))
    wspec = pl.BlockSpec((None, cb, cb), lambda j, i: (j, 0, 0))
    vspec = pl.BlockSpec((8, cb), lambda j, i: (0, j))
    return pl.pallas_call(
        body, name=name, grid=(nb, t // tm),
        in_specs=[blk] * 6 + [wspec, wspec, vspec],
        out_specs=[blk, blk, blk, vspec],
        out_shape=[jax.ShapeDtypeStruct((t, r_dim), F32), jax.ShapeDtypeStruct((t, r_dim), BF16),
                   jax.ShapeDtypeStruct((t, r_dim), BF16), jax.ShapeDtypeStruct((8, r_dim), F32)],
        compiler_params=_params("parallel", "arbitrary"),
    )(rec, r, i, a, dh, da, bd_r, bd_i, vecs)


def _bd_grad(rec, dpr, dpi, *, cb, name):
    t, r = rec.shape
    nb = r // cb

    def body(x_ref, dr_ref, di_ref, gr_ref, gi_ref):
        xb = x_ref[...].astype(BF16)
        gr_ref[...] = _dot_tn(xb, dr_ref[...])
        gi_ref[...] = _dot_tn(xb, di_ref[...])

    blk = pl.BlockSpec((t, cb), lambda j: (0, j))
    wspec = pl.BlockSpec((None, cb, cb), lambda j: (j, 0, 0))
    out = jax.ShapeDtypeStruct((nb, cb, cb), F32)
    return pl.pallas_call(
        body, name=name, grid=(nb,),
        in_specs=[blk, blk, blk], out_specs=[wspec, wspec], out_shape=[out, out],
        compiler_params=_params("parallel"),
    )(rec, dpr, dpi)


def _conv_a_bwd(d_rec, gr, dgate, cwb, *, cb, name):
    t, r = d_rec.shape
    nb = r // cb

    def body(dy_ref, x_ref, dg_ref, w_ref, dact_ref, dw_ref):
        dx, dw = _conv_taps_bwd(dy_ref[...], x_ref[...], w_ref[...], 4)
        dact_ref[0] = dg_ref[...]
        dact_ref[1] = dx.astype(BF16)
        dw_ref[...] = dw

    blk = pl.BlockSpec((t, cb), lambda j: (0, j))
    vspec = pl.BlockSpec((8, cb), lambda j: (0, j))
    return pl.pallas_call(
        body, name=name, grid=(nb,),
        in_specs=[blk, pl.BlockSpec((t, cb), lambda j: (0, j + nb)), blk, vspec],
        out_specs=[pl.BlockSpec((2, t, cb), lambda j: (0, 0, j)), vspec],
        out_shape=[jax.ShapeDtypeStruct((2, t, r), BF16), jax.ShapeDtypeStruct((8, r), F32)],
        compiler_params=_params("parallel"),
    )(d_rec, gr, dgate, cwb)


def _split3(x):
    p0 = x.astype(BF16)
    r1 = x - p0.astype(F32)
    p1 = r1.astype(BF16)
    p2 = (r1 - p1.astype(F32)).astype(BF16)
    return p0, p1, p2


def _fgate_fwd(fp, fb, *, tq, name):
    t = fp.shape[0]

    def body(f_ref, b_ref, c_ref, ct_ref):
        logf = -_softplus(-(f_ref[...] + b_ref[...]))
        rows = pl.program_id(0) * tq + lax.broadcasted_iota(jnp.int32, (tq, t), 0)
        cols = lax.broadcasted_iota(jnp.int32, (tq, t), 1)
        tri = (cols <= rows).astype(BF16)
        p0, p1, p2 = _split3(logf)
        c = _dot_nn(tri, p0) + _dot_nn(tri, p1) + _dot_nn(tri, p2)
        c_ref[...] = c
        ct_ref[...] = c.T

    return pl.pallas_call(
        body, name=name, grid=(t // tq,),
        in_specs=[pl.BlockSpec((t, LANES), lambda i: (0, 0)), pl.BlockSpec((1, LANES), lambda i: (0, 0))],
        out_specs=[pl.BlockSpec((tq, LANES), lambda i: (i, 0)), pl.BlockSpec((LANES, tq), lambda i: (0, i))],
        out_shape=[jax.ShapeDtypeStruct((t, LANES), F32), jax.ShapeDtypeStruct((LANES, t), F32)],
        compiler_params=_params("parallel"),
    )(fp, fb)


def _fgate_bwd(dct, fp, fb, *, tq, name):
    t = fp.shape[0]

    def body(d_ref, f_ref, b_ref, o_ref, db_ref):
        i = pl.program_id(0)
        rows = lax.broadcasted_iota(jnp.int32, (t, tq), 0)
        cols = i * tq + lax.broadcasted_iota(jnp.int32, (t, tq), 1)
        tri = (rows >= cols).astype(BF16)
        p0, p1, p2 = _split3(d_ref[...])
        dlogf = (_dot_nn(p0, tri) + _dot_nn(p1, tri) + _dot_nn(p2, tri)).T
        df = dlogf * _sigmoid(-(f_ref[...] + b_ref[...]))
        o_ref[...] = df.astype(BF16)
        upd = _rows8([jnp.sum(df, axis=0, keepdims=True)], LANES)

        @pl.when(i == 0)
        def _():
            db_ref[...] = upd

        @pl.when(i > 0)
        def _():
            db_ref[...] += upd

    return pl.pallas_call(
        body, name=name, grid=(t // tq,),
        in_specs=[pl.BlockSpec((LANES, t), lambda i: (0, 0)), pl.BlockSpec((tq, LANES), lambda i: (i, 0)),
                  pl.BlockSpec((1, LANES), lambda i: (0, 0))],
        out_specs=[pl.BlockSpec((tq, LANES), lambda i: (i, 0)), pl.BlockSpec((8, LANES), lambda i: (0, 0))],
        out_shape=[jax.ShapeDtypeStruct((t, LANES), BF16), jax.ShapeDtypeStruct((8, LANES), F32)],
        compiler_params=_params("arbitrary"),
    )(dct, fp, fb)


def _pair_sum(a, b, *, tm, name):
    t, d = a.shape

    def body(a_ref, b_ref, o_ref):
        o_ref[...] = (a_ref[...] + b_ref[...]).astype(BF16)

    row = pl.BlockSpec((tm, d), lambda i: (i, 0))
    return pl.pallas_call(
        body, name=name, grid=(t // tm,), in_specs=[row, row], out_specs=row,
        out_shape=jax.ShapeDtypeStruct((t, d), BF16), compiler_params=_params("parallel"),
    )(a, b)


FWD_HEAD_TILES = 2
BWD_HEAD_TILES = 1


def _head_block_width(dh, tiles):
    return tiles * LANES if tiles * LANES // dh <= 8 else LANES


def _head_masks(dh, bw):
    lane = lax.broadcasted_iota(jnp.int32, (1, bw), 1)
    return [((lane >= e * dh) & (lane < (e + 1) * dh)) for e in range(bw // dh)]


def _head_c_row(ct_blk, head):
    sub = lax.broadcasted_iota(jnp.int32, ct_blk.shape, 0)
    return jnp.sum(jnp.where(sub == head, ct_blk, 0.0), axis=0, keepdims=True)


def _attn_weights(qm, k, c_row, q0):
    tq, t = qm.shape[0], k.shape[0]
    s = _dot_nt(qm, k) - c_row
    qi = q0 + lax.broadcasted_iota(jnp.int32, (tq, t), 0)
    ki = lax.broadcasted_iota(jnp.int32, (tq, t), 1)
    s = jnp.where(ki <= qi, s, -jnp.inf)
    m = jnp.max(s, axis=-1, keepdims=True)
    e = jnp.exp(s - m)
    return e, m, 1.0 / jnp.sum(e, axis=-1, keepdims=True)


def _key_buckets(t, tq):
    return tuple(sorted({min(-(-(i * tq) // LANES) * LANES, t) for i in range(1, t // tq + 1)}))


def _for_prefix(needed, buckets, fn):
    prev = 0
    for length in buckets:
        pl.when((needed > prev) & (needed <= length))(lambda length=length: fn(length))
        prev = length


def _attn_fwd(qg, kv, ct, *, tq, name):
    t, d2 = qg.shape
    d = d2 // 2
    dh = d // N_HEADS
    bw = _head_block_width(dh, FWD_HEAD_TILES)
    hpb = bw // dh
    nhb = d // bw
    scale = dh ** -0.5
    buckets = _key_buckets(t, tq)

    def body(q_ref, og_ref, k_ref, v_ref, ct_ref, o_ref, y_ref, st_ref):
        hb = pl.program_id(0)
        q0 = pl.program_id(1) * tq

        def run(length):
            qs = q_ref[...] * scale
            k = k_ref[0:length, :]
            v = v_ref[0:length, :]
            o = jnp.zeros((tq, bw), F32)
            lane = lax.broadcasted_iota(jnp.int32, (tq, LANES), 1)
            stats = jnp.zeros((tq, LANES), F32)
            for e, msk in enumerate(_head_masks(dh, bw)):
                c_row = _head_c_row(ct_ref[:, 0:length], hb * hpb + e)
                w, m, inv = _attn_weights(jnp.where(msk, qs, 0.0).astype(BF16), k, c_row, q0)
                o = o + _dot_nn(w.astype(BF16), jnp.where(msk, v, jnp.zeros_like(v))) * inv
                stats = jnp.where(lane == e, m, jnp.where(lane == hpb + e, inv, stats))
            o_ref[...] = o
            y_ref[...] = (o * _sigmoid(og_ref[...])).astype(BF16)
            st_ref[...] = stats

        _for_prefix(q0 + tq, buckets, run)

    qblk = pl.BlockSpec((tq, bw), lambda h, i: (i, h))
    return pl.pallas_call(
        body, name=name, grid=(nhb, t // tq),
        in_specs=[qblk, pl.BlockSpec((tq, bw), lambda h, i: (i, h + nhb)),
                  pl.BlockSpec((t, bw), lambda h, i: (0, h)), pl.BlockSpec((t, bw), lambda h, i: (0, h + nhb)),
                  pl.BlockSpec((N_HEADS, t), lambda h, i: (0, 0))],
        out_specs=[qblk, pl.BlockSpec((None, tq, bw), lambda h, i: (0, i, h)),
                   pl.BlockSpec((None, tq, LANES), lambda h, i: (h, i, 0))],
        out_shape=[jax.ShapeDtypeStruct((t, d), F32), jax.ShapeDtypeStruct((1, t, d), BF16),
                   jax.ShapeDtypeStruct((nhb, t, LANES), F32)],
        compiler_params=_params("parallel", "parallel"),
    )(qg, qg, kv, kv, ct)


def _attn_bwd(dy, qg, o, stats, kv, ct, *, tq, name):
    t, d2 = qg.shape
    d = d2 // 2
    dh = d // N_HEADS
    bw = _head_block_width(dh, BWD_HEAD_TILES)
    hpb = bw // dh
    nhb = d // bw
    scale = dh ** -0.5
    n_q = t // tq
    hpb_f = _head_block_width(dh, FWD_HEAD_TILES) // dh
    ratio = hpb_f // hpb
    chunk = 8 * LANES

    def body(dy_ref, q_ref, og_ref, o_ref, st_ref, k_ref, v_ref, ct_ref, dqg_ref, dk_ref, dv_ref, dc_ref, dcq_ref):
        hb = pl.program_id(0)
        step = pl.program_id(1)

        @pl.when(step == 0)
        def _():
            dk_ref[...] = jnp.zeros((t, bw), F32)
            dv_ref[...] = jnp.zeros((t, bw), F32)
            dc_ref[...] = jnp.zeros((8, t), F32)

        def run(i):
            q0 = i * tq
            length = min(-(-(q0 + tq) // LANES) * LANES, t)
            qs = q_ref[...] * scale
            sg = _sigmoid(og_ref[...])
            dyv = dy_ref[...]
            ov = o_ref[...]
            do = dyv * sg
            dqg_ref[1] = (dyv * ov * sg * (1.0 - sg)).astype(BF16)
            lane = lax.broadcasted_iota(jnp.int32, (tq, LANES), 1)
            stats = st_ref[...]
            masks = _head_masks(dh, bw)
            heads = []
            for e, msk in enumerate(masks):
                pos = (hb % ratio) * hpb + e
                m = jnp.sum(jnp.where(lane == pos, stats, 0.0), axis=1, keepdims=True)
                inv = jnp.sum(jnp.where(lane == hpb_f + pos, stats, 0.0), axis=1, keepdims=True)
                delta = jnp.sum(jnp.where(msk, do * ov, 0.0), axis=1, keepdims=True)
                heads.append((msk, m, inv, delta, jnp.where(msk, qs, 0.0).astype(BF16),
                              jnp.where(msk, do, 0.0).astype(BF16)))
            dq = jnp.zeros((tq, bw), F32)
            dcq = jnp.zeros((tq, LANES), F32)
            row_acc = [jnp.zeros((tq, chunk), F32) for _ in heads]
            for c0 in range(0, length, chunk):
                ch = min(chunk, length - c0)
                k = k_ref[c0:c0 + ch, :]
                v = v_ref[c0:c0 + ch, :]
                dk = jnp.zeros((ch, bw), F32)
                dv = jnp.zeros((ch, bw), F32)
                dc_rows = []
                for e, (msk, m, inv, delta, qm, dom) in enumerate(heads):
                    c_row = _head_c_row(ct_ref[:, c0:c0 + ch], hb * hpb + e)
                    s = _dot_nt(qm, k) - c_row
                    if c0 + ch - 1 > q0:
                        qi = q0 + lax.broadcasted_iota(jnp.int32, (tq, ch), 0)
                        ki = c0 + lax.broadcasted_iota(jnp.int32, (tq, ch), 1)
                        s = jnp.where(ki <= qi, s, -jnp.inf)
                    p = jnp.exp(s - m) * inv
                    dsc = p * (_dot_nt(dom, v) - delta)
                    dsb = dsc.astype(BF16)
                    dq = dq + _dot_nn(dsb, jnp.where(msk, k, jnp.zeros_like(k)))
                    dk = dk + _dot_tn(dsb, qm)
                    dv = dv + _dot_tn(p.astype(BF16), dom)
                    dc_rows.append(-jnp.sum(dsc, axis=0, keepdims=True))
                    if ch == chunk:
                        row_acc[e] = row_acc[e] + dsc
                    else:
                        dcq = dcq + jnp.where(lane == e, jnp.sum(dsc, axis=1, keepdims=True), 0.0)
                dk_ref[c0:c0 + ch, :] += dk
                dv_ref[c0:c0 + ch, :] += dv
                dc_ref[:, c0:c0 + ch] += _rows8(dc_rows, ch)
            dqg_ref[0] = (dq * scale).astype(BF16)
            for e in range(len(heads)):
                dcq = dcq + jnp.where(lane == e, jnp.sum(row_acc[e], axis=1, keepdims=True), 0.0)
            dcq_ref[...] = dcq

        for i in range(n_q):
            pl.when(step == i)(lambda i=i: run(i))

    qblk = pl.BlockSpec((tq, bw), lambda h, i: (i, h))
    kblk = pl.BlockSpec((t, bw), lambda h, i: (0, h))
    return pl.pallas_call(
        body, name=name, grid=(nhb, n_q),
        in_specs=[qblk, qblk, pl.BlockSpec((tq, bw), lambda h, i: (i, h + nhb)), qblk,
                  pl.BlockSpec((None, tq, LANES), lambda h, i: (h // ratio, i, 0)),
                  kblk, pl.BlockSpec((t, bw), lambda h, i: (0, h + nhb)),
                  pl.BlockSpec((N_HEADS, t), lambda h, i: (0, 0))],
        out_specs=[pl.BlockSpec((2, tq, bw), lambda h, i: (0, i, h)), kblk, kblk,
                   pl.BlockSpec((None, 8, t), lambda h, i: (h, 0, 0)),
                   pl.BlockSpec((None, tq, LANES), lambda h, i: (h, i, 0))],
        out_shape=[jax.ShapeDtypeStruct((2, t, d), BF16), jax.ShapeDtypeStruct((t, d), F32),
                   jax.ShapeDtypeStruct((t, d), F32), jax.ShapeDtypeStruct((nhb, 8, t), F32),
                   jax.ShapeDtypeStruct((nhb, t, LANES), F32)],
        compiler_params=_params("parallel", "arbitrary"),
    )(dy, qg, qg, o, stats, kv, kv, ct)


def _loss_bwd(h, tgt, *, lo, hi, tm, name):
    t, d = h.shape

    def body(h_ref, t_ref, l_ref, dy_ref):
        i = pl.program_id(0)
        rows = i * tm + lax.broadcasted_iota(jnp.int32, (tm, d), 0)
        err = jnp.where((rows >= lo) & (rows < hi), h_ref[...] - t_ref[...], 0.0)
        dy_ref[...] = err * (1.0 / d)
        part = jnp.sum(jnp.sum(err * err, axis=0, keepdims=True), axis=1, keepdims=True) * (0.5 / d)
        upd = jnp.broadcast_to(part, (8, LANES))

        @pl.when(i == 0)
        def _():
            l_ref[...] = upd

        @pl.when(i > 0)
        def _():
            l_ref[...] += upd

    row = pl.BlockSpec((tm, d), lambda i: (i, 0))
    return pl.pallas_call(
        body, name=name, grid=(t // tm,),
        in_specs=[row, row],
        out_specs=[pl.BlockSpec((8, LANES), lambda i: (0, 0)), row],
        out_shape=[jax.ShapeDtypeStruct((8, LANES), F32), jax.ShapeDtypeStruct((t, d), F32)],
        compiler_params=_params("arbitrary"),
    )(h, tgt)


def _adamw_math(w, gv, m, v):
    bc1 = 1.0 / (1.0 - ADAM_B1 ** ADAM_STEP)
    bc2 = 1.0 / (1.0 - ADAM_B2 ** ADAM_STEP)
    nm = ADAM_B1 * m + (1.0 - ADAM_B1) * gv
    nv = ADAM_B2 * v + (1.0 - ADAM_B2) * (gv * gv)
    delta = (-ADAM_LR) * ((nm * bc1) / (jnp.sqrt(nv * bc2) + ADAM_EPS) + ADAM_WD * w)
    return delta, nm, nv


def _adamw(w, g, m, v, *, name):
    r, c = w.shape
    tr = r
    for cand in (512, 256, 128, 64, 32, 16, 8):
        if r % cand == 0 and r > cand:
            tr = cand
            break

    def body(w_ref, g_ref, m_ref, v_ref, d_ref, nm_ref, nv_ref):
        d_ref[...], nm_ref[...], nv_ref[...] = _adamw_math(w_ref[...], g_ref[...], m_ref[...], v_ref[...])

    blk = pl.BlockSpec((tr, c), lambda i: (i, 0))
    out = jax.ShapeDtypeStruct((r, c), F32)
    return pl.pallas_call(
        body, name=name, grid=(r // tr,),
        in_specs=[blk] * 4, out_specs=[blk] * 3, out_shape=[out] * 3,
        compiler_params=_params("parallel"),
    )(w, g, m, v)


def _sum_adamw(recvs, sends, me, w, m, v, *, name):
    n_l = len(recvs)
    _, r, c = recvs[0].shape
    tr = _tile(r, (256, 192, 176, 128, 96, 64, 48, 32, 16))

    def body(me_ref, *refs):
        p_refs, own_refs = refs[:n_l], refs[n_l:2 * n_l]
        w_ref, m_ref, v_ref, g_ref, d_ref, nm_ref, nv_ref, acc_ref = refs[2 * n_l:]
        layer = pl.program_id(0)
        mine = me_ref[0]
        for k in range(n_l):
            @pl.when(layer == k)
            def _(k=k):
                acc_ref[...] = jnp.zeros((tr, c), F32)
                for dev in range(N_DEV):
                    @pl.when(mine == dev)
                    def _():
                        acc_ref[...] += own_refs[k][...].astype(F32)

                    @pl.when(mine != dev)
                    def _(dev=dev):
                        acc_ref[...] += p_refs[k][dev].astype(F32)
                acc = acc_ref[...]
                g_ref[...] = acc
                d_ref[...], nm_ref[...], nv_ref[...] = _adamw_math(w_ref[...], acc, m_ref[...], v_ref[...])

    p_specs = [pl.BlockSpec((N_DEV, tr, c), lambda l, i, me_ref, k=k: (0, jnp.where(l == k, i, 0), 0))
               for k in range(n_l)]
    own_specs = [pl.BlockSpec((None, tr, c), lambda l, i, me_ref, k=k: (me_ref[0], jnp.where(l == k, i, 0), 0))
                 for k in range(n_l)]
    blk = pl.BlockSpec((None, tr, c), lambda l, i, me_ref: (l, i, 0))
    out = jax.ShapeDtypeStruct((n_l, r, c), F32)
    return pl.pallas_call(
        body, name=name,
        grid_spec=pltpu.PrefetchScalarGridSpec(
            num_scalar_prefetch=1, grid=(n_l, r // tr),
            in_specs=p_specs + own_specs + [blk] * 3, out_specs=[blk] * 4,
            scratch_shapes=[pltpu.VMEM((tr, c), F32)]),
        out_shape=[out] * 4,
        compiler_params=_params("arbitrary", "arbitrary"),
    )(me, *recvs, *sends, w, m, v)


def _sum8(parts, *, name):
    _, r, c = parts.shape
    tr = r
    for cand in (512, 256, 128, 64, 32, 16):
        if r % cand == 0 and r > cand:
            tr = cand
            break

    def body(p_ref, o_ref):
        acc = p_ref[0].astype(F32)
        for k in range(1, N_DEV):
            acc = acc + p_ref[k].astype(F32)
        o_ref[...] = acc

    return pl.pallas_call(
        body, name=name, grid=(r // tr,),
        in_specs=[pl.BlockSpec((N_DEV, tr, c), lambda i: (0, i, 0))],
        out_specs=pl.BlockSpec((tr, c), lambda i: (i, 0)),
        out_shape=jax.ShapeDtypeStruct((r, c), F32),
        compiler_params=_params("parallel"),
    )(parts)


def _my_index():
    return 4 * lax.axis_index("x") + 2 * lax.axis_index("y") + lax.axis_index("c")


def _peer(k):
    x, y, c = lax.axis_index("x"), lax.axis_index("y"), lax.axis_index("c")
    px = x ^ ((k >> 2) & 1)
    py = y ^ ((k >> 1) & 1)
    pc = c ^ (k & 1)
    return (px, py, pc), 4 * px + 2 * py + pc


def _all_gather(shards, *, name):
    n_arr = len(shards)

    def body(*refs):
        ins, outs = refs[:n_arr], refs[n_arr:2 * n_arr]
        send_sems, recv_sems, local_sems = refs[2 * n_arr:]
        me = _my_index()
        local = [pltpu.make_async_copy(ins[n], outs[n].at[me], local_sems.at[n]) for n in range(n_arr)]
        for cp in local:
            cp.start()
        sends = []
        for k in range(1, N_DEV):
            peer, _ = _peer(k)
            for n in range(n_arr):
                cp = pltpu.make_async_remote_copy(
                    src_ref=ins[n], dst_ref=outs[n].at[me], send_sem=send_sems.at[n, k - 1],
                    recv_sem=recv_sems.at[n, k - 1], device_id=peer, device_id_type=pl.DeviceIdType.MESH)
                cp.start()
                sends.append(cp)
        for k in range(1, N_DEV):
            peer, pidx = _peer(k)
            for n in range(n_arr):
                pltpu.make_async_remote_copy(
                    src_ref=ins[n], dst_ref=outs[n].at[pidx], send_sem=send_sems.at[n, k - 1],
                    recv_sem=recv_sems.at[n, k - 1], device_id=peer, device_id_type=pl.DeviceIdType.MESH).wait_recv()
        for cp in sends:
            cp.wait_send()
        for cp in local:
            cp.wait()

    hbm = pl.BlockSpec(memory_space=pl.ANY)
    return pl.pallas_call(
        body, name=name,
        in_specs=[hbm] * n_arr, out_specs=[hbm] * n_arr,
        out_shape=[jax.ShapeDtypeStruct((N_DEV,) + s.shape, s.dtype) for s in shards],
        scratch_shapes=[pltpu.SemaphoreType.DMA((n_arr, N_DEV - 1)), pltpu.SemaphoreType.DMA((n_arr, N_DEV - 1)),
                        pltpu.SemaphoreType.DMA((n_arr,))],
        compiler_params=pltpu.CompilerParams(has_side_effects=True),
    )(*shards)


_HBM = pl.BlockSpec(memory_space=pltpu.HBM)
_SEM = pl.BlockSpec(memory_space=pltpu.SEMAPHORE)
_EFFECT = pltpu.SideEffectType.DATAFLOW_SIDE_EFFECTING


def _remote(src, dst, send_sem, recv_sem, peer):
    return pltpu.make_async_remote_copy(src_ref=src, dst_ref=dst, send_sem=send_sem, recv_sem=recv_sem,
                                        device_id=peer, device_id_type=pl.DeviceIdType.MESH)


def _place_own(src, layer, me, *, out_dtype, name):
    _, r, c = src.shape
    tr = _tile(r, (256, 192, 176, 128, 96, 64, 48, 32, 16))

    def body(me_ref, s_ref, o_ref):
        o_ref[...] = s_ref[...].astype(out_dtype)

    return pl.pallas_call(
        body, name=name,
        grid_spec=pltpu.PrefetchScalarGridSpec(
            num_scalar_prefetch=1, grid=(r // tr,),
            in_specs=[pl.BlockSpec((None, tr, c), lambda i, me_ref: (layer, i, 0))],
            out_specs=pl.BlockSpec((None, tr, c), lambda i, me_ref: (me_ref[0], i, 0))),
        out_shape=jax.ShapeDtypeStruct((N_DEV, r, c), out_dtype),
        compiler_params=_params("parallel"),
    )(me, src)


def _own_blocks(srcs, *, name):
    n = len(srcs)

    def body(*refs):
        ins, outs, sems = refs[:n], refs[n:2 * n], refs[2 * n]
        me = _my_index()
        cps = [pltpu.make_async_copy(ins[t].at[me], outs[t].at[me], sems.at[t]) for t in range(n)]
        for cp in cps:
            cp.start()
        for cp in cps:
            cp.wait()

    return pl.pallas_call(
        body, name=name, in_specs=[_HBM] * n, out_specs=[_HBM] * n,
        out_shape=[jax.ShapeDtypeStruct(s.shape, s.dtype) for s in srcs],
        scratch_shapes=[pltpu.SemaphoreType.DMA((n,))],
    )(*srcs)


def _split_start(groups, *, scatter, name):
    sizes = [len(srcs) for srcs, _ in groups]
    flat_src = [s for srcs, _ in groups for s in srcs]
    flat_land = [l for _, lands in groups for l in lands]
    n, n_g = len(flat_land), len(groups)
    if not scatter:
        flat_src = []
    n_in = len(flat_src) + n

    def body(*refs):
        lands = refs[n_in - n:n_in]
        ins = refs[:n] if scatter else lands
        sems = refs[n_in:n_in + 2 * n_g]
        token = refs[-1]
        me = _my_index()
        t = 0
        for g in range(n_g):
            for q in range(sizes[g]):
                for k in range(1, N_DEV):
                    peer, pidx = _peer(k)
                    src = ins[t].at[pidx] if scatter else ins[t].at[me]
                    slot = q * (N_DEV - 1) + k - 1
                    _remote(src, lands[t].at[me], sems[2 * g].at[slot], sems[2 * g + 1].at[slot], peer).start()
                t += 1
        token[...] = jnp.zeros_like(token)

    sem_shapes = []
    for sz in sizes:
        sem_shapes += [pltpu.SemaphoreType.DMA((sz * (N_DEV - 1),)), pltpu.SemaphoreType.DMA((sz * (N_DEV - 1),))]
    outs = pl.pallas_call(
        body, name=name,
        in_specs=[_HBM] * n_in,
        out_specs=[_SEM] * (2 * n_g) + [_HBM] * n_in + [pl.BlockSpec(memory_space=pltpu.VMEM)],
        out_shape=sem_shapes + [pltpu.HBM(a.shape, a.dtype) for a in flat_src + flat_land]
        + [jax.ShapeDtypeStruct((8, LANES), F32)],
        input_output_aliases={i: 2 * n_g + i for i in range(n_in)},
        compiler_params=pltpu.CompilerParams(has_side_effects=_EFFECT),
    )(*[pltpu.with_memory_space_constraint(a, pltpu.HBM) for a in flat_src + flat_land])
    sems, thru, token = outs[:2 * n_g], outs[2 * n_g:2 * n_g + n_in], outs[-1]
    handles, pos = [], 0
    for g, sz in enumerate(sizes):
        lands_g = thru[n_in - n + pos:n_in - n + pos + sz]
        handles.append((sems[2 * g], sems[2 * g + 1], thru[pos:pos + sz] if scatter else [], lands_g))
        pos += sz
    return handles, token


def _split_wait(handle, after, *, scatter, name):
    send_sems, recv_sems, srcs, lands = handle
    n, n_src = len(lands), len(srcs)

    def body(*refs):
        lnd = refs[n_src:n_src + n]
        ins = refs[:n_src] if scatter else lnd
        ssem, rsem = refs[n_src + n], refs[n_src + n + 1]
        me = _my_index()
        for t in range(n):
            for k in range(1, N_DEV):
                peer, pidx = _peer(k)
                block = ins[t].at[me]
                slot = t * (N_DEV - 1) + k - 1
                _remote(block, lnd[t].at[me], ssem.at[slot], rsem.at[slot], peer).wait_send()
                _remote(block, lnd[t].at[pidx], ssem.at[slot], rsem.at[slot], peer).wait_recv()

    return pl.pallas_call(
        body, name=name,
        in_specs=[_HBM] * (n_src + n) + [_SEM, _SEM, pl.BlockSpec(memory_space=pl.ANY)],
        out_specs=[_HBM] * n,
        out_shape=[pltpu.HBM(l.shape, l.dtype) for l in lands],
        input_output_aliases={n_src + t: t for t in range(n)},
        compiler_params=pltpu.CompilerParams(has_side_effects=_EFFECT),
    )(*srcs, *lands, send_sems, recv_sems, after)


def _pack(arrs, dtype, row_quantum=16):
    flat = jnp.concatenate([a.astype(dtype).reshape(-1) for a in arrs])
    pad = (-flat.shape[0]) % (row_quantum * PACK_COLS)
    if pad:
        flat = jnp.concatenate([flat, jnp.zeros((pad,), dtype)])
    return flat.reshape(-1, PACK_COLS)


def _pack8(arrs, dtype):
    flat = jnp.concatenate([a.astype(dtype).reshape(N_DEV, -1) for a in arrs], axis=1)
    pad = (-flat.shape[1]) % (16 * PACK_COLS)
    if pad:
        flat = jnp.concatenate([flat, jnp.zeros((N_DEV, pad), dtype)], axis=1)
    return flat.reshape(N_DEV, -1, PACK_COLS)


def _unpack(slab, shapes, lead):
    lead_shape = slab.shape[:lead]
    flat = slab.reshape(lead_shape + (-1,))
    outs, off = [], 0
    for shp in shapes:
        size = math.prod(shp)
        outs.append(flat[..., off:off + size].reshape(lead_shape + tuple(shp)))
        off += size
    return outs


def _cols_full(g):
    g = jnp.moveaxis(g, 0, -2)
    return g.reshape(g.shape[:-2] + (g.shape[-2] * g.shape[-1],))


def _cols_split(full):
    n = full.shape[-1] // N_DEV
    return jnp.moveaxis(full.reshape(full.shape[:-1] + (N_DEV, n)), -2, 0)


def _block_diag(w, per):
    n, b, _ = w.shape
    w4 = w.reshape(n // per, per, b, b)
    eye = jnp.eye(per, dtype=w.dtype)
    return jnp.einsum('gpab,pq->gpaqb', w4, eye).reshape(n // per, per * b, per * b)


def _block_diag_extract(g, per):
    gn, cb, _ = g.shape
    b = cb // per
    g5 = g.reshape(gn, per, b, per, b)
    return jnp.stack([g5[:, p, :, p, :] for p in range(per)], axis=1).reshape(gn * per, b, b)


def _slab2d(a):
    return a.reshape(-1, a.shape[-1])


def _lru_block_cols(r_dim):
    lru = r_dim // N_LRU_BLOCKS
    return lru * LANES // math.gcd(lru, LANES)


BIG = ("a_w_in", "a_w_out", "b_w_in", "b_w_out", "f_w_in", "f_w_out")
COL_F32 = ("meta", "a_conv_w", "a_conv_b", "a_b_r", "a_b_i", "a_lambda", "f_conv_w")
REPLICATED = ("a_w_r", "a_w_i", "kv_f_b", "f_conv_b", "ln1_g", "ln1_b", "ln2_g", "ln2_b")
WEIGHT_NAMES = ("meta", "a_w_in", "a_conv_w", "a_conv_b", "a_w_r", "a_b_r", "a_w_i", "a_b_i", "a_lambda", "a_w_out",
                "kv_w", "kv_f_b", "b_w_in", "b_w_out", "f_w_in", "f_conv_w", "f_conv_b", "f_w_out",
                "ln1_g", "ln1_b", "ln2_g", "ln2_b")


def _kv_layout(kv_gathered, d):
    kv_full = _cols_full(kv_gathered)
    kv_pad = 2 * d + LANES - kv_full.shape[1]
    return jnp.concatenate([kv_full, jnp.zeros((d, kv_pad), kv_full.dtype)], axis=1)


def _small_layouts(small):
    r_dim = small["a_lambda"].shape[1]
    n_f = small["f_conv_b"].shape[1] // N_DEV
    cb = _lru_block_cols(r_dim)
    per = cb // (r_dim // N_LRU_BLOCKS)
    n_a = small["a_lambda"].shape[0]
    f_conv_w3 = small["f_conv_w"].reshape(N_LAYERS, 3, N_DEV, n_f).transpose(0, 2, 1, 3)
    f_conv_b3 = small["f_conv_b"].reshape(N_LAYERS, N_DEV, 1, n_f)
    return {
        "kv_fb": jnp.concatenate([small["kv_f_b"], jnp.zeros((LANES - N_HEADS,), F32)])[None],
        "a_cwb": jnp.concatenate([small["a_conv_w"], small["a_conv_b"][:, None],
                                  jnp.zeros((n_a, 3, r_dim), F32)], axis=1),
        "a_vecs": jnp.concatenate([jnp.stack([small["a_b_r"], small["a_b_i"], small["a_lambda"]], axis=1),
                                   jnp.zeros((n_a, 5, r_dim), F32)], axis=1),
        "a_bd_r": jnp.stack([_block_diag(small["a_w_r"][l], per) for l in range(n_a)]).astype(BF16),
        "a_bd_i": jnp.stack([_block_diag(small["a_w_i"][l], per) for l in range(n_a)]).astype(BF16),
        "f_cwb3": jnp.concatenate([f_conv_w3, f_conv_b3, jnp.zeros((N_LAYERS, N_DEV, 4, n_f), F32)], axis=2),
        "ln1_g": small["ln1_g"][:, None], "ln1_b": small["ln1_b"][:, None],
        "ln2_g": small["ln2_g"][:, None], "ln2_b": small["ln2_b"][:, None],
    }


def _local_step(h0, tgt, n_meta, n_tok, wts, hooks):
    tp, d = h0.shape
    tm = tp // 8 if (tp // 8) % 16 == 0 else tp
    tmb = _tile(tp, (1088, 512, 320, 256, 128))
    tq = 128
    tqa_fwd = tp // 4 if tp % 64 == 0 else tq
    tqa_bwd = tp // 4 if tp % 64 == 0 else tq
    r_dim = wts["a_vecs"].shape[2]
    cb = wts["a_bd_r"].shape[-1]
    sb = LANES
    n_b = N_LAYERS - N_A_LAYERS

    h, h_bf = h0, h0.astype(BF16)
    saved = []
    kvs = None
    for layer in range(N_LAYERS):
        lw = {}
        sv = {"h_bf": h_bf, "w": lw}
        if layer < N_A_LAYERS:
            lw["in"] = hooks.weight(layer, "in", h)
            sv["gr"] = _proj_in(h_bf, lw["in"], shard_major=False, name="a_in_proj")
            sv["rec"] = _conv_a_fwd(sv["gr"], wts["a_cwb"][layer], cb=cb, name="a_conv_fwd")
            a, u, sv["r"], sv["i"] = _gates_fwd(sv["rec"], wts["a_bd_r"][layer], wts["a_bd_i"][layer],
                                                wts["a_vecs"][layer], tm=tmb // 2, name="a_gates_fwd")
            sv["a"] = a
            sv["hr"], y3 = _scan_fwd(a, u, sv["gr"], cb=sb, name="a_scan_fwd")
        else:
            j = layer - N_A_LAYERS
            if j == 0:
                kv_w = _kv_layout(hooks.weight(layer, "kv_w", h), d)
                kvs = {"h_bf": h_bf, "w": kv_w}
                kvs["kv"] = _mm_nn(h_bf, kv_w[:, :2 * d], tn=_tile(2 * d, (512, 256, 128)), out_dtype=BF16,
                                   name="kv_proj")
                kvs["fp"] = _mm_nn(h_bf, kv_w[:, 2 * d:], tn=LANES, out_dtype=F32, name="f_proj")
                kvs["c"], ct = _fgate_fwd(kvs["fp"], wts["kv_fb"], tq=tq, name="fgate_fwd")
                kvs["ct"] = ct[:N_HEADS]
            lw["in"] = hooks.weight(layer, "in", kvs["c"] if j == 0 else h)
            sv["qg"] = _proj_in(h_bf, lw["in"], shard_major=False, name="b_in_proj")
            sv["o"], y3, sv["st"] = _attn_fwd(sv["qg"], kvs["kv"], kvs["ct"], tq=tqa_fwd, name="attn_fwd")
        sv["y3"] = y3
        lw["out"] = hooks.weight(layer, "out", y3)
        sv["s1"], h, h_bf = _out_ln(y3, lw["out"], h, wts["ln1_g"][layer], wts["ln1_b"][layer], n_valid=n_tok,
                                    tm=tmb // 2, name="mix_out_ln")
        sv["h1_bf"] = h_bf
        lw["f_in"] = hooks.weight(layer, "f_in", h)
        sv["z3"] = _proj_in(h_bf, lw["f_in"], shard_major=True, transposed=True, name="f_in_proj")
        sv["yf3"] = _convglu_fwd(sv["z3"], wts["f_cwb3"][layer], name="f_convglu_fwd")
        lw["f_out"] = hooks.weight(layer, "f_out", sv["yf3"])
        sv["s2"], h, h_bf = _out_ln(sv["yf3"], lw["f_out"], h, wts["ln2_g"][layer], wts["ln2_b"][layer],
                                    n_valid=n_tok, tm=tmb // 2, name="ffn_out_ln")
        saved.append(sv)

    loss_tile, dh = _loss_bwd(h, tgt, lo=n_meta, hi=n_tok, tm=tm, name="loss")

    grads = {k: [None] * N_LAYERS for k in ("f_cwb3", "ln1_gb", "ln2_gb")}
    grads.update({k: [None] * N_A_LAYERS for k in ("a_cwb", "a_bd_r", "a_bd_i", "a_vecs")})
    dkv = []
    token = jnp.zeros((), F32)
    for layer in reversed(range(N_LAYERS)):
        sv = saved[layer]
        lw = sv["w"]
        big = {}
        ds, ds_bf, grads["ln2_gb"][layer] = _ln_bwd(dh, sv["s2"], wts["ln2_g"][layer] + token, tm=tm, name="ln_bwd")
        dz, dcw = _ffn_bwd_mid(ds_bf, lw["f_out"], sv["z3"], wts["f_cwb3"][layer], name="f_bwd_mid")
        grads["f_cwb3"][layer] = dcw.reshape((N_DEV,) + dcw.shape[2:])
        dz3 = dz
        big["f_out"] = _w_out_grad(sv["yf3"], ds_bf, lw["f_out"].shape[1], name="f_w_out_grad")
        dh = _in_bwd(dz3, lw["f_in"], ds, tm=tmb, transposed=True, name="f_in_bwd")
        big["f_in"] = _w_in_grad(sv["h1_bf"], dz3, transposed=True, name="f_w_in_grad")
        token = hooks.grads_ready(layer, "ffn", big)
        big = {}
        ds, ds_bf, grads["ln1_gb"][layer] = _ln_bwd(dh, sv["s1"], wts["ln1_g"][layer] + token, tm=tm, name="ln_bwd")
        if layer < N_A_LAYERS:
            dy = _out_bwd(ds_bf, lw["out"], tm=tmb // 2, name="a_out_bwd")
            big["out"] = _w_out_grad(sv["y3"], ds_bf, lw["out"].shape[1], name="a_w_out_grad")
            d_h, d_a, dgate = _scan_bwd(dy, sv["gr"], sv["hr"], sv["a"], cb=sb, name="a_scan_bwd")
            d_rec, dpr, dpi, grads["a_vecs"][layer] = _gates_bwd(
                sv["rec"], sv["r"], sv["i"], sv["a"], d_h, d_a, wts["a_bd_r"][layer], wts["a_bd_i"][layer],
                wts["a_vecs"][layer], tm=tmb // 2, name="a_gates_bwd")
            grads["a_bd_r"][layer], grads["a_bd_i"][layer] = _bd_grad(sv["rec"], dpr, dpi, cb=cb, name="a_bd_grad")
            dact, grads["a_cwb"][layer] = _conv_a_bwd(d_rec, sv["gr"], dgate, wts["a_cwb"][layer], cb=cb,
                                                      name="a_conv_bwd")
            dh = _in_bwd(dact, lw["in"], ds, tm=tmb, name="a_in_bwd")
            big["in"] = _w_in_grad(sv["h_bf"], dact, name="a_w_in_grad")
        else:
            j = layer - N_A_LAYERS
            dy = _out_bwd(ds_bf, lw["out"], tm=tmb // 2, name="b_out_bwd")
            big["out"] = _w_out_grad(sv["y3"], ds_bf, lw["out"].shape[1], name="b_w_out_grad")
            dqg, dk, dv, dc, dcq = _attn_bwd(dy, sv["qg"], sv["o"], sv["st"], kvs["kv"], kvs["ct"], tq=tqa_bwd,
                                             name="attn_bwd")
            dkv.append((dk, dv, dc, dcq))
            dh = _in_bwd(dqg, lw["in"], ds, tm=tmb, name="b_in_bwd")
            big["in"] = _w_in_grad(sv["h_bf"], dqg, name="b_w_in_grad")
            if j == 0:
                hpb = _head_block_width(d // N_HEADS, BWD_HEAD_TILES) // (d // N_HEADS)
                dct = (dkv[0][2] + dkv[1][2])[:, :hpb, :].reshape(N_HEADS, tp)
                dcq = (dkv[0][3] + dkv[1][3])[:, :, :hpb]
                dct = dct + jnp.transpose(dcq, (0, 2, 1)).reshape(N_HEADS, tp)
                dct = jnp.concatenate([dct, jnp.zeros((LANES - N_HEADS, tp), F32)])
                df_bf, grads["kv_fb"] = _fgate_bwd(dct, kvs["fp"], wts["kv_fb"], tq=tq, name="fgate_bwd")
                dkvz = jnp.concatenate([_pair_sum(dkv[0][0], dkv[1][0], tm=tm, name="kv_pair_sum"),
                                        _pair_sum(dkv[0][1], dkv[1][1], tm=tm, name="kv_pair_sum"), df_bf], axis=1)
                dh = _mm_nt_full(dkvz, kvs["w"], dh, tm=tmb // 2, name="kv_in_bwd")
                big["kv_w"] = _mm_tn_cols(kvs["h_bf"], dkvz, tn=LANES, name="kv_w_grad")
        token = hooks.grads_ready(layer, "mix", big)
    return loss_tile, dh, grads


def _finish_small_grads(grads, d_h0, n_meta):
    r_dim = grads["a_vecs"][0].shape[1]
    per = _lru_block_cols(r_dim) // (r_dim // N_LRU_BLOCKS)
    a_cwb = jnp.stack(grads["a_cwb"])
    a_vecs = jnp.stack(grads["a_vecs"])
    f_cwb3 = jnp.stack(grads["f_cwb3"])
    ln1 = jnp.stack(grads["ln1_gb"])
    ln2 = jnp.stack(grads["ln2_gb"])
    f_rows = f_cwb3.transpose(0, 2, 1, 3).reshape(N_LAYERS, 8, -1)
    return {
        "meta": d_h0[:n_meta],
        "a_conv_w": a_cwb[:, :4], "a_conv_b": a_cwb[:, 4],
        "a_w_r": jnp.stack([_block_diag_extract(g, per) for g in grads["a_bd_r"]]),
        "a_b_r": a_vecs[:, 0],
        "a_w_i": jnp.stack([_block_diag_extract(g, per) for g in grads["a_bd_i"]]),
        "a_b_i": a_vecs[:, 1], "a_lambda": a_vecs[:, 2],
        "kv_f_b": grads["kv_fb"][0, :N_HEADS],
        "f_conv_w": f_rows[:, :3], "f_conv_b": f_rows[:, 3],
        "ln1_g": ln1[:, 0], "ln1_b": ln1[:, 1], "ln2_g": ln2[:, 0], "ln2_b": ln2[:, 1],
    }


def kernel(x, meta, a_w_in, a_conv_w, a_conv_b, a_w_r, a_b_r, a_w_i, a_b_i, a_lambda, a_w_out, kv_w, kv_f_b, b_w_in, b_w_out, f_w_in, f_conv_w, f_conv_b, f_w_out, ln1_g, ln1_b, ln2_g, ln2_b, loss_target, m_meta, m_a_w_in, m_a_conv_w, m_a_conv_b, m_a_w_r, m_a_b_r, m_a_w_i, m_a_b_i, m_a_lambda, m_a_w_out, m_kv_w, m_kv_f_b, m_b_w_in, m_b_w_out, m_f_w_in, m_f_conv_w, m_f_conv_b, m_f_w_out, m_ln1_g, m_ln1_b, m_ln2_g, m_ln2_b, v_meta, v_a_w_in, v_a_conv_w, v_a_conv_b, v_a_w_r, v_a_b_r, v_a_w_i, v_a_b_i, v_a_lambda, v_a_w_out, v_kv_w, v_kv_f_b, v_b_w_in, v_b_w_out, v_f_w_in, v_f_conv_w, v_f_conv_b, v_f_w_out, v_ln1_g, v_ln1_b, v_ln2_g, v_ln2_b):
    w = dict(meta=meta, a_w_in=a_w_in, a_conv_w=a_conv_w, a_conv_b=a_conv_b, a_w_r=a_w_r, a_b_r=a_b_r, a_w_i=a_w_i,
             a_b_i=a_b_i, a_lambda=a_lambda, a_w_out=a_w_out, kv_w=kv_w, kv_f_b=kv_f_b, b_w_in=b_w_in,
             b_w_out=b_w_out, f_w_in=f_w_in, f_conv_w=f_conv_w, f_conv_b=f_conv_b, f_w_out=f_w_out, ln1_g=ln1_g,
             ln1_b=ln1_b, ln2_g=ln2_g, ln2_b=ln2_b)
    m = dict(meta=m_meta, a_w_in=m_a_w_in, a_conv_w=m_a_conv_w, a_conv_b=m_a_conv_b, a_w_r=m_a_w_r, a_b_r=m_a_b_r,
             a_w_i=m_a_w_i, a_b_i=m_a_b_i, a_lambda=m_a_lambda, a_w_out=m_a_w_out, kv_w=m_kv_w, kv_f_b=m_kv_f_b,
             b_w_in=m_b_w_in, b_w_out=m_b_w_out, f_w_in=m_f_w_in, f_conv_w=m_f_conv_w, f_conv_b=m_f_conv_b,
             f_w_out=m_f_w_out, ln1_g=m_ln1_g, ln1_b=m_ln1_b, ln2_g=m_ln2_g, ln2_b=m_ln2_b)
    v = dict(meta=v_meta, a_w_in=v_a_w_in, a_conv_w=v_a_conv_w, a_conv_b=v_a_conv_b, a_w_r=v_a_w_r, a_b_r=v_a_b_r,
             a_w_i=v_a_w_i, a_b_i=v_a_b_i, a_lambda=v_a_lambda, a_w_out=v_a_w_out, kv_w=v_kv_w, kv_f_b=v_kv_f_b,
             b_w_in=v_b_w_in, b_w_out=v_b_w_out, f_w_in=v_f_w_in, f_conv_w=v_f_conv_w, f_conv_b=v_f_conv_b,
             f_w_out=v_f_w_out, ln1_g=v_ln1_g, ln1_b=v_ln1_b, ln2_g=v_ln2_g, ln2_b=v_ln2_b)
    shapes = {n: w[n].shape for n in WEIGHT_NAMES}

    me = jnp.reshape(_my_index(), (1,)).astype(jnp.int32)

    def as_stored(name, a):
        return jnp.swapaxes(a, 1, 2) if name == "f_w_in" else a

    param_of = {"in": ("a_w_in", "b_w_in"), "out": ("a_w_out", "b_w_out"), "f_in": ("f_w_in",) * 2,
                "f_out": ("f_w_out",) * 2}
    order = [("small", None, None)]
    for layer in range(N_LAYERS):
        if layer == N_A_LAYERS:
            order.append(("kv_w", layer, 0))
        for key in ("in", "out", "f_in", "f_out"):
            order.append((key, layer, layer if key[0] == "f" or layer < N_A_LAYERS else layer - N_A_LAYERS))
    def place(key, layer, idx):
        if key == "small":
            return _place_own(_pack([w[n] for n in COL_F32], F32)[None], 0, me, out_dtype=F32, name="place_small")
        if key == "kv_w":
            return _place_own(w["kv_w"][None], 0, me, out_dtype=BF16, name="place_kv_w")
        name = param_of[key][0 if layer < N_A_LAYERS else 1]
        return _place_own(as_stored(name, w[name]), idx, me, out_dtype=BF16, name=f"place_{name}_{idx}")

    lands = [place(*o) for o in order]
    gather_handles, gather_token = _split_start([([l], [l]) for l in lands], scatter=False, name="gather_start")
    group_of = {(key, layer): g for g, (key, layer, _) in enumerate(order)}
    (got_s,) = _split_wait(gather_handles[0], gather_token, scatter=False, name="gather_wait_small")
    small = {n: w[n] for n in REPLICATED}
    for n, part in zip(COL_F32, _unpack(got_s, [w[n].shape for n in COL_F32], 1)):
        small[n] = _cols_full(part)
    n_meta, d = small["meta"].shape

    class Hooks:
        pending = None
        received = {}
        sent = {}

        @staticmethod
        def weight(layer, key, after):
            (got,) = _split_wait(gather_handles[group_of[(key, layer)]], after, scatter=False,
                                 name=f"gather_wait_{key}_{layer}")
            return got

        @staticmethod
        def collect(after):
            if Hooks.pending is not None:
                tag, names, handle = Hooks.pending
                got = _split_wait(handle, after, scatter=True, name=f"scatter_wait_{tag}")
                Hooks.received.update(zip(names, got))
                Hooks.pending = None

        @staticmethod
        def grads_ready(layer, part, big):
            if "kv_w" in big:
                big["kv_w"] = _cols_split(big["kv_w"][:, :shapes["kv_w"][1] * N_DEV]).astype(BF16)
            names = [(key, layer) for key in big]
            send = [big[key] for key in big]
            Hooks.collect(send[0])
            empty = [lax.empty(s.shape, s.dtype) for s in send]
            handles, token = _split_start([(send, empty)], scatter=True, name=f"scatter_start_{part}_{layer}")
            Hooks.pending = (f"{part}_{layer}", names, handles[0])
            Hooks.sent.update(zip(names, handles[0][2]))
            return token[0, 0]

    Hooks.pending, Hooks.received, Hooks.sent = None, {}, {}

    n_tok = n_meta + x.shape[1]
    tp = -(-n_tok // ROW_ALIGN) * ROW_ALIGN
    pad = jnp.zeros((tp - n_tok, d), F32)
    h0 = jnp.concatenate([small["meta"], x[0], pad])
    tgt = jnp.concatenate([jnp.zeros((n_meta, d), F32), loss_target[0], pad])
    loss_tile, d_h0, grads = _local_step(h0, tgt, n_meta, n_tok, _small_layouts(small), Hooks)
    g_small = _finish_small_grads(grads, d_h0, n_meta)
    loss = lax.psum(loss_tile[0, 0], MESH_AXES)
    grad_x = d_h0[n_meta:n_tok][None]

    rep = _pack([g_small[n] for n in REPLICATED], F32, row_quantum=16 * N_DEV)
    send = [_pack8([_cols_split(g_small[n]) for n in COL_F32], F32), rep.reshape(N_DEV, -1, PACK_COLS)]
    lands = _own_blocks(send, name="scatter_own_small")
    handles, token = _split_start([(send, lands)], scatter=True, name="scatter_start_small")

    g, delta, new_m, new_v = {}, {}, {}, {}
    layers_of = {
        "a_w_in": [("in", l) for l in range(N_A_LAYERS)], "a_w_out": [("out", l) for l in range(N_A_LAYERS)],
        "b_w_in": [("in", l) for l in range(N_A_LAYERS, N_LAYERS)],
        "b_w_out": [("out", l) for l in range(N_A_LAYERS, N_LAYERS)],
        "f_w_in": [("f_in", l) for l in range(N_LAYERS)], "f_w_out": [("f_out", l) for l in range(N_LAYERS)],
        "kv_w": [("kv_w", N_A_LAYERS)],
    }
    ready = [n for n in BIG + ("kv_w",) if all(t in Hooks.received for t in layers_of[n])]

    def done(names):
        return jnp.stack([g[n][(0,) * g[n].ndim] for n in names])

    for n in ready + [n for n in BIG + ("kv_w",) if n not in ready]:
        if n not in ready and Hooks.pending is not None:
            Hooks.collect(done(ready))
        lift = (lambda a: a[None]) if n == "kv_w" else (lambda a, n=n: as_stored(n, a))
        outs = _sum_adamw([Hooks.received[t] for t in layers_of[n]], [Hooks.sent[t] for t in layers_of[n]], me,
                          lift(w[n]), lift(m[n]), lift(v[n]), name="sum_adamw_" + n)
        g[n], delta[n], new_m[n], new_v[n] = [as_stored(n, o).reshape(shapes[n]) for o in outs]
    recv_s, recv_r = _split_wait(handles[0], done(BIG + ("kv_w",)), scatter=True, name="scatter_wait_small")
    sum_s = _sum8(recv_s, name="sum_grads_f32")
    g.update(zip(COL_F32, _unpack(sum_s, [shapes[n] for n in COL_F32], 0)))
    (got_r,) = _all_gather([_sum8(recv_r, name="sum_grads_replicated")], name="gather_replicated_sums")
    g.update(zip(REPLICATED, _unpack(got_r.reshape(-1, PACK_COLS), [shapes[n] for n in REPLICATED], 0)))

    for n in COL_F32 + REPLICATED:
        shp = shapes[n]
        dl, nm, nv = _adamw(_slab2d(w[n]), _slab2d(g[n]), _slab2d(m[n]), _slab2d(v[n]), name="adamw")
        delta[n], new_m[n], new_v[n] = dl.reshape(shp), nm.reshape(shp), nv.reshape(shp)
    return (loss, grad_x, *[g[n] for n in WEIGHT_NAMES], *[delta[n] for n in WEIGHT_NAMES],
            *[new_m[n] for n in WEIGHT_NAMES], *[new_v[n] for n in WEIGHT_NAMES])
```

```python
import math

import jax
import jax.numpy as jnp
from jax import lax
from jax.experimental import pallas as pl
from jax.experimental.pallas import tpu as pltpu

F32 = jnp.float32
BF16 = jnp.bfloat16

N_DEV = 8
MESH_AXES = ("x", "y", "c")
N_LAYERS = 4
N_A_LAYERS = 2
N_LRU_BLOCKS = 16
N_HEADS = 16
LRU_C = 8.0
DN_ALPHA = (2 * N_LAYERS) ** 0.25
LN_EPS = 1e-5
ADAM_LR, ADAM_B1, ADAM_B2, ADAM_EPS, ADAM_WD, ADAM_STEP = 0.001, 0.9, 0.999, 1e-08, 0.01, 10

LANES = 128
SUBLANES = 8
ROW_ALIGN = 128
VMEM_LIMIT_BYTES = 56 * 1024 * 1024
GELU_K = math.sqrt(2.0 / math.pi)
GELU_C = 0.044715
PACK_COLS = 1024


def _params(*sem):
    return pltpu.CompilerParams(dimension_semantics=sem, vmem_limit_bytes=VMEM_LIMIT_BYTES)


def _gelu(x):
    th = jnp.tanh(GELU_K * (x + GELU_C * x * x * x))
    return 0.5 * x * (1.0 + th)


def _gelu_and_grad(x):
    x2 = x * x
    th = jnp.tanh(GELU_K * (x + GELU_C * x2 * x))
    g = 0.5 * x * (1.0 + th)
    dg = 0.5 * (1.0 + th) + 0.5 * x * (1.0 - th * th) * (GELU_K * (1.0 + 3.0 * GELU_C * x2))
    return g, dg


def _sigmoid(x):
    return 0.5 * jnp.tanh(0.5 * x) + 0.5


def _expm1(x):
    small = x * (1.0 + 0.5 * x * (1.0 + (1.0 / 3.0) * x * (1.0 + 0.25 * x)))
    return jnp.where(jnp.abs(x) < 1e-2, small, jnp.exp(x) - 1.0)


def _softplus(x):
    e = jnp.exp(-jnp.abs(x))
    small = e * (1.0 - 0.5 * e * (1.0 - (2.0 / 3.0) * e))
    return jnp.maximum(x, 0.0) + jnp.where(e < 1e-2, small, jnp.log(1.0 + e))


def _shift_down(x, s):
    if s == 0:
        return x
    rows = lax.broadcasted_iota(jnp.int32, x.shape, 0)
    return jnp.where(rows >= s, pltpu.roll(x, s, 0), 0.0)


def _shift_up(x, s):
    if s == 0:
        return x
    n = x.shape[0]
    rows = lax.broadcasted_iota(jnp.int32, x.shape, 0)
    return jnp.where(rows < n - s, pltpu.roll(x, n - s, 0), 0.0)


def _dot_nn(a, b):
    return lax.dot_general(a, b, (((1,), (0,)), ((), ())), preferred_element_type=F32)


def _dot_nt(a, b):
    return lax.dot_general(a, b, (((1,), (1,)), ((), ())), preferred_element_type=F32)


def _dot_tn(a, b):
    return lax.dot_general(a, b, (((0,), (0,)), ((), ())), preferred_element_type=F32)


def _rows8(vals, width):
    rows = lax.broadcasted_iota(jnp.int32, (8, width), 0)
    out = jnp.zeros((8, width), F32)
    for k, v in enumerate(vals):
        out = jnp.where(rows == k, jnp.broadcast_to(v, (8, width)), out)
    return out


def _tile(n, prefer):
    for c in prefer:
        if n % c == 0:
            return c
    return n


def _mm_nn(a, b, *, tn, out_dtype, name):
    m, k = a.shape
    n = b.shape[1]

    def body(a_ref, b_ref, o_ref):
        o_ref[...] = _dot_nn(a_ref[...], b_ref[...]).astype(o_ref.dtype)

    return pl.pallas_call(
        body, name=name, grid=(n // tn,),
        in_specs=[pl.BlockSpec((m, k), lambda j: (0, 0)), pl.BlockSpec((k, tn), lambda j: (0, j))],
        out_specs=pl.BlockSpec((m, tn), lambda j: (0, j)),
        out_shape=jax.ShapeDtypeStruct((m, n), out_dtype),
        compiler_params=_params("parallel"),
    )(a, b)


def _proj_in(h_bf, g_in, *, shard_major, name, transposed=False):
    t, k = h_bf.shape
    n = g_in.shape[1] if transposed else g_in.shape[2]

    def body(a_ref, b_ref, o_ref):
        o_ref[...] = _dot_nt(a_ref[...], b_ref[...]) if transposed else _dot_nn(a_ref[...], b_ref[...])

    if shard_major:
        out_spec = pl.BlockSpec((None, t, n), lambda j: (j, 0, 0))
        out_shape = jax.ShapeDtypeStruct((N_DEV, t, n), F32)
    else:
        out_spec = pl.BlockSpec((t, n), lambda j: (0, j))
        out_shape = jax.ShapeDtypeStruct((t, N_DEV * n), F32)
    return pl.pallas_call(
        body, name=name, grid=(N_DEV,),
        in_specs=[pl.BlockSpec((t, k), lambda j: (0, 0)),
                  pl.BlockSpec((None,) + g_in.shape[1:], lambda j: (j, 0, 0))],
        out_specs=out_spec, out_shape=out_shape,
        compiler_params=_params("parallel"),
    )(h_bf, g_in)


def _out_ln(y3, g_out, hin, g, b, *, n_valid, tm, name):
    nj, t, kj = y3.shape
    _, r, d = g_out.shape

    def body(y_ref, w_ref, hin_ref, g_ref, b_ref, s_ref, h_ref, hb_ref):
        w = w_ref[...].reshape(N_DEV * r, d)
        s = DN_ALPHA * hin_ref[...]
        for jj in range(nj):
            s = s + _dot_nn(y_ref[jj], w[jj * kj:(jj + 1) * kj])
        mu = jnp.mean(s, axis=-1, keepdims=True)
        xc = s - mu
        var = jnp.mean(xc * xc, axis=-1, keepdims=True)
        h = xc * lax.rsqrt(var + LN_EPS) * g_ref[...] + b_ref[...]
        s_ref[...] = s
        h_ref[...] = h
        rows = pl.program_id(0) * tm + lax.broadcasted_iota(jnp.int32, (tm, d), 0)
        hb_ref[...] = jnp.where(rows < n_valid, h, 0.0).astype(BF16)

    row = pl.BlockSpec((tm, d), lambda i: (i, 0))
    vec = pl.BlockSpec((1, d), lambda i: (0, 0))
    return pl.pallas_call(
        body, name=name, grid=(t // tm,),
        in_specs=[pl.BlockSpec((nj, tm, kj), lambda i: (0, i, 0)),
                  pl.BlockSpec((N_DEV, r, d), lambda i: (0, 0, 0)), row, vec, vec],
        out_specs=[row, row, row],
        out_shape=[jax.ShapeDtypeStruct((t, d), F32), jax.ShapeDtypeStruct((t, d), F32),
                   jax.ShapeDtypeStruct((t, d), BF16)],
        compiler_params=_params("parallel"),
    )(y3, g_out, hin, g, b)


def _out_bwd(ds_bf, g_out, *, tm, name):
    t, d = ds_bf.shape
    r = g_out.shape[1]

    def body(a_ref, w_ref, o_ref):
        o_ref[...] = _dot_nt(a_ref[...], w_ref[...].reshape(N_DEV * r, d))

    return pl.pallas_call(
        body, name=name, grid=(t // tm,),
        in_specs=[pl.BlockSpec((tm, d), lambda i: (i, 0)),
                  pl.BlockSpec((N_DEV, r, d), lambda i: (0, 0, 0))],
        out_specs=pl.BlockSpec((tm, N_DEV * r), lambda i: (i, 0)),
        out_shape=jax.ShapeDtypeStruct((t, N_DEV * r), F32),
        compiler_params=_params("parallel"),
    )(ds_bf, g_out)


def _in_bwd(dact, g_in, add, *, tm, name, alpha=DN_ALPHA, transposed=False):
    t = dact.shape[-2]
    _, k, n = g_in.shape
    if transposed:
        k, n = n, k
    halves = dact.shape[0] == 2 and dact.ndim == 3
    per = N_DEV // 2

    def body(a_ref, b_ref, add_ref, o_ref, acc_ref):
        j = pl.program_id(1)

        @pl.when(j == 0)
        def _():
            acc_ref[...] = alpha * add_ref[...]

        acc_ref[...] += _dot_nn(a_ref[...], b_ref[...]) if transposed else _dot_nt(a_ref[...], b_ref[...])

        @pl.when(j == N_DEV - 1)
        def _():
            o_ref[...] = acc_ref[...]

    if halves:
        a_spec = pl.BlockSpec((None, tm, n), lambda i, j: (j // per, i, j % per))
    elif dact.ndim == 4:
        a_spec = pl.BlockSpec((None, None, tm, n), lambda i, j: (j // per, j % per, i, 0))
    else:
        a_spec = pl.BlockSpec((None, tm, n), lambda i, j: (j, i, 0))
    return pl.pallas_call(
        body, name=name, grid=(t // tm, N_DEV),
        in_specs=[a_spec, pl.BlockSpec((None,) + g_in.shape[1:], lambda i, j: (j, 0, 0)),
                  pl.BlockSpec((tm, k), lambda i, j: (i, 0))],
        out_specs=pl.BlockSpec((tm, k), lambda i, j: (i, 0)),
        out_shape=jax.ShapeDtypeStruct((t, k), F32),
        scratch_shapes=[pltpu.VMEM((tm, k), F32)],
        compiler_params=_params("parallel", "arbitrary"),
    )(dact, g_in, add)


def _mm_nt_full(a, b, add, *, tm, name):
    t, n = a.shape
    k = b.shape[0]

    def body(a_ref, b_ref, add_ref, o_ref):
        o_ref[...] = add_ref[...] + _dot_nt(a_ref[...], b_ref[...])

    return pl.pallas_call(
        body, name=name, grid=(t // tm,),
        in_specs=[pl.BlockSpec((tm, n), lambda i: (i, 0)), pl.BlockSpec((k, n), lambda i: (0, 0)),
                  pl.BlockSpec((tm, k), lambda i: (i, 0))],
        out_specs=pl.BlockSpec((tm, k), lambda i: (i, 0)),
        out_shape=jax.ShapeDtypeStruct((t, k), F32),
        compiler_params=_params("parallel"),
    )(a, b, add)


def _w_in_grad(h_bf, dact, *, name, transposed=False):
    t, k = h_bf.shape
    halves = dact.shape[0] == 2 and dact.ndim == 3
    per = N_DEV // 2
    n = dact.shape[-1] // per if halves else dact.shape[-1]

    def body(a_ref, b_ref, o_ref):
        if transposed:
            o_ref[...] = _dot_tn(b_ref[...], a_ref[...]).astype(BF16)
        else:
            o_ref[...] = _dot_tn(a_ref[...], b_ref[...]).astype(BF16)

    if halves:
        b_spec = pl.BlockSpec((None, t, n), lambda j: (j // per, 0, j % per))
    elif dact.ndim == 4:
        b_spec = pl.BlockSpec((None, None, t, n), lambda j: (j // per, j % per, 0, 0))
    else:
        b_spec = pl.BlockSpec((None, t, n), lambda j: (j, 0, 0))
    return pl.pallas_call(
        body, name=name, grid=(N_DEV,),
        in_specs=[pl.BlockSpec((t, k), lambda j: (0, 0)), b_spec],
        out_specs=pl.BlockSpec((None, n, k) if transposed else (None, k, n), lambda j: (j, 0, 0)),
        out_shape=jax.ShapeDtypeStruct((N_DEV, n, k) if transposed else (N_DEV, k, n), BF16),
        compiler_params=_params("parallel"),
    )(h_bf, dact)


def _w_out_grad(y3, ds_bf, r, *, name):
    nj, t, kj = y3.shape
    d = ds_bf.shape[1]
    unit = r * LANES // math.gcd(r, LANES)
    ks = max([c for c in range(unit, min(kj, 768) + 1, unit) if kj % c == 0], default=kj)
    gsz = ks // r
    per = kj // ks

    def body(a_ref, b_ref, o_ref):
        o_ref[...] = _dot_tn(a_ref[...], b_ref[...]).reshape(gsz, r, d).astype(BF16)

    return pl.pallas_call(
        body, name=name, grid=(nj * per,),
        in_specs=[pl.BlockSpec((None, t, ks), lambda j: (j // per, 0, j % per)),
                  pl.BlockSpec((t, d), lambda j: (0, 0))],
        out_specs=pl.BlockSpec((gsz, r, d), lambda j: (j, 0, 0)),
        out_shape=jax.ShapeDtypeStruct((N_DEV, r, d), BF16),
        compiler_params=_params("parallel"),
    )(y3, ds_bf)


def _mm_tn_cols(a, b, *, tn, name):
    t, m = a.shape
    n = b.shape[1]

    def body(a_ref, b_ref, o_ref):
        o_ref[...] = _dot_tn(a_ref[...], b_ref[...])

    return pl.pallas_call(
        body, name=name, grid=(n // tn,),
        in_specs=[pl.BlockSpec((t, m), lambda j: (0, 0)), pl.BlockSpec((t, tn), lambda j: (0, j))],
        out_specs=pl.BlockSpec((m, tn), lambda j: (0, j)),
        out_shape=jax.ShapeDtypeStruct((m, n), F32),
        compiler_params=_params("parallel"),
    )(a, b)


def _ln_bwd(dout, s, g, *, tm, name):
    t, d = s.shape

    def body(do_ref, s_ref, g_ref, ds_ref, dsb_ref, gb_ref):
        i = pl.program_id(0)
        sv = s_ref[...]
        do = do_ref[...]
        mu = jnp.mean(sv, axis=-1, keepdims=True)
        xc = sv - mu
        var = jnp.mean(xc * xc, axis=-1, keepdims=True)
        rstd = lax.rsqrt(var + LN_EPS)
        xhat = xc * rstd
        dxhat = do * g_ref[...]
        m1 = jnp.mean(dxhat, axis=-1, keepdims=True)
        m2 = jnp.mean(dxhat * xhat, axis=-1, keepdims=True)
        ds = rstd * (dxhat - m1 - xhat * m2)
        ds_ref[...] = ds
        dsb_ref[...] = ds.astype(BF16)
        upd = _rows8([jnp.sum(do * xhat, axis=0, keepdims=True), jnp.sum(do, axis=0, keepdims=True)], d)

        @pl.when(i == 0)
        def _():
            gb_ref[...] = upd

        @pl.when(i > 0)
        def _():
            gb_ref[...] += upd

    row = pl.BlockSpec((tm, d), lambda i: (i, 0))
    return pl.pallas_call(
        body, name=name, grid=(t // tm,),
        in_specs=[row, row, pl.BlockSpec((1, d), lambda i: (0, 0))],
        out_specs=[row, row, pl.BlockSpec((8, d), lambda i: (0, 0))],
        out_shape=[jax.ShapeDtypeStruct((t, d), F32), jax.ShapeDtypeStruct((t, d), BF16),
                   jax.ShapeDtypeStruct((8, d), F32)],
        compiler_params=_params("arbitrary"),
    )(dout, s, g)


def _roll_down(x, s):
    return x if s == 0 else pltpu.roll(x, s, 0)


def _conv_taps(x, wb, width):
    y = jnp.broadcast_to(wb[width:width + 1, :], x.shape)
    for k in range(width):
        y = y + _roll_down(x, width - 1 - k) * wb[k:k + 1, :]
    return y


def _conv_taps_bwd(dy, x, wb, width):
    n = dy.shape[0]
    dx = jnp.zeros_like(dy)
    rows = []
    for k in range(width):
        s = width - 1 - k
        dy_up = dy if s == 0 else pltpu.roll(dy, n - s, 0)
        dx = dx + dy_up * wb[k:k + 1, :]
        rows.append(jnp.sum(dy_up * x, axis=0, keepdims=True))
    rows.append(jnp.sum(dy, axis=0, keepdims=True))
    t_idx = lax.broadcasted_iota(jnp.int32, dy.shape, 0)
    return jnp.where(t_idx < n - (width - 1), dx, 0.0), _rows8(rows, dy.shape[1])


def _convglu_fwd(z3, fwb3, *, name):
    _, t, n = z3.shape
    half = N_DEV // 2
    nc = pl.cdiv(n, LANES)

    def body(zg_ref, zv_ref, wg_ref, wv_ref, y_ref):
        gate = _conv_taps(zg_ref[...], wg_ref[...], 3)
        val = _conv_taps(zv_ref[...], wv_ref[...], 3)
        y_ref[...] = (_gelu(gate) * val).astype(BF16)

    zblk = lambda off: pl.BlockSpec((None, t, LANES), lambda j, c: (j + off, 0, c))
    wblk = lambda off: pl.BlockSpec((None, 8, LANES), lambda j, c: (j + off, 0, c))
    return pl.pallas_call(
        body, name=name, grid=(half, nc),
        in_specs=[zblk(0), zblk(half), wblk(0), wblk(half)],
        out_specs=zblk(0),
        out_shape=jax.ShapeDtypeStruct((half, t, n), BF16),
        compiler_params=_params("parallel", "parallel"),
    )(z3, z3, fwb3, fwb3)


def _ffn_bwd_mid(ds_bf, g_out, z3, fwb3, *, name):
    t, d = ds_bf.shape
    r = g_out.shape[1]
    n = z3.shape[2]
    half = N_DEV // 2
    nc = pl.cdiv(n, LANES)
    assert n == 2 * r

    def body(ds_ref, w_ref, zg_ref, zv_ref, wg_ref, wv_ref, dz_ref, dwb_ref, wsc_ref):
        c = pl.program_id(1)

        @pl.when(c == 0)
        def _():
            wsc_ref[0:r, :] = w_ref[0]
            wsc_ref[r:2 * r, :] = w_ref[1]
            if nc * LANES > n:
                wsc_ref[n:nc * LANES, :] = jnp.zeros((nc * LANES - n, d), BF16)

        w = wsc_ref[pl.ds(pl.multiple_of(c * LANES, LANES), LANES), :]
        dyf = _dot_nt(ds_ref[...], w)
        zg, zv = zg_ref[...], zv_ref[...]
        wg, wv = wg_ref[...], wv_ref[...]
        gate = _conv_taps(zg, wg, 3)
        val = _conv_taps(zv, wv, 3)
        gl, dgl = _gelu_and_grad(gate)
        dzg, dwg = _conv_taps_bwd(dyf * val * dgl, zg, wg, 3)
        dzv, dwv = _conv_taps_bwd(dyf * gl, zv, wv, 3)
        dz_ref[0] = dzg.astype(BF16)
        dz_ref[1] = dzv.astype(BF16)
        dwb_ref[0] = dwg
        dwb_ref[1] = dwv

    zblk = lambda off: pl.BlockSpec((None, t, LANES), lambda j, c: (j + off, 0, c))
    wblk = lambda off: pl.BlockSpec((None, 8, LANES), lambda j, c: (j + off, 0, c))
    return pl.pallas_call(
        body, name=name, grid=(half, nc),
        in_specs=[pl.BlockSpec((t, d), lambda j, c: (0, 0)),
                  pl.BlockSpec((2, r, d), lambda j, c: (j, 0, 0)),
                  zblk(0), zblk(half), wblk(0), wblk(half)],
        out_specs=[pl.BlockSpec((2, None, t, LANES), lambda j, c: (0, j, 0, c)),
                   pl.BlockSpec((2, None, 8, LANES), lambda j, c: (0, j, 0, c))],
        out_shape=[jax.ShapeDtypeStruct((2, half, t, n), BF16), jax.ShapeDtypeStruct((2, half, 8, n), F32)],
        scratch_shapes=[pltpu.VMEM((nc * LANES, d), BF16)],
        compiler_params=_params("parallel", "arbitrary"),
    )(ds_bf, g_out, z3, z3, fwb3, fwb3)


def _conv_a_fwd(gr, cwb, *, cb, name):
    t, r2 = gr.shape
    r = r2 // 2
    nb = r // cb

    def body(x_ref, w_ref, o_ref):
        o_ref[...] = _conv_taps(x_ref[...], w_ref[...], 4)

    return pl.pallas_call(
        body, name=name, grid=(nb,),
        in_specs=[pl.BlockSpec((t, cb), lambda j: (0, j + nb)), pl.BlockSpec((8, cb), lambda j: (0, j))],
        out_specs=pl.BlockSpec((t, cb), lambda j: (0, j)),
        out_shape=jax.ShapeDtypeStruct((t, r), F32),
        compiler_params=_params("parallel"),
    )(gr, cwb)


def _gates_fwd(rec, bd_r, bd_i, vecs, *, tm, name):
    t, r_dim = rec.shape
    nb, cb, _ = bd_r.shape

    def body(x_ref, wr_ref, wi_ref, v_ref, a_ref, u_ref, r_ref, i_ref):
        x = x_ref[...]
        xb = x.astype(BF16)
        v = v_ref[...]
        r = _sigmoid(_dot_nn(xb, wr_ref[...]) + v[0:1, :])
        i = _sigmoid(_dot_nn(xb, wi_ref[...]) + v[1:2, :])
        log_a = (-LRU_C) * r * _softplus(-v[2:3, :])
        a_ref[...] = jnp.exp(log_a)
        u_ref[...] = jnp.sqrt(-_expm1(2.0 * log_a)) * (i * x)
        r_ref[...] = r
        i_ref[...] = i

    blk = pl.BlockSpec((tm, cb), lambda j, i: (i, j))
    wspec = pl.BlockSpec((None, cb, cb), lambda j, i: (j, 0, 0))
    out = jax.ShapeDtypeStruct((t, r_dim), F32)
    return pl.pallas_call(
        body, name=name, grid=(nb, t // tm),
        in_specs=[blk, wspec, wspec, pl.BlockSpec((8, cb), lambda j, i: (0, j))],
        out_specs=[blk, blk, blk, blk],
        out_shape=[out, out, out, out],
        compiler_params=_params("parallel", "parallel"),
    )(rec, bd_r, bd_i, vecs)


def _scan_fwd(a, u, gr, *, cb, name):
    t, r = a.shape
    nb = r // cb
    seg = t // SUBLANES

    def body(a_ref, u_ref, g_ref, h_ref, y_ref, p_ref):
        def step(k, carry):
            h, p = carry
            rows = pl.ds(k, SUBLANES, stride=seg)
            av = a_ref[rows, :]
            h = av * h + u_ref[rows, :]
            p = av * p
            h_ref[rows, :] = h
            p_ref[rows, :] = p
            return h, p

        h_fin, p_fin = lax.fori_loop(0, seg, step, (jnp.zeros((SUBLANES, cb), F32), jnp.ones((SUBLANES, cb), F32)),
                                     unroll=8)
        carry = h_fin[0:1, :]
        for s in range(1, SUBLANES):
            rows = slice(s * seg, (s + 1) * seg)
            h_ref[rows, :] = h_ref[rows, :] + p_ref[rows, :] * carry
            carry = h_fin[s:s + 1, :] + p_fin[s:s + 1, :] * carry
        y_ref[...] = (_gelu(g_ref[...]) * h_ref[...]).astype(BF16)

    blk = pl.BlockSpec((t, cb), lambda j: (0, j))
    return pl.pallas_call(
        body, name=name, grid=(nb,),
        in_specs=[blk, blk, blk],
        out_specs=[blk, pl.BlockSpec((None, t, cb), lambda j: (0, 0, j))],
        out_shape=[jax.ShapeDtypeStruct((t, r), F32), jax.ShapeDtypeStruct((1, t, r), BF16)],
        scratch_shapes=[pltpu.VMEM((t, cb), F32)],
        compiler_params=_params("parallel"),
    )(a, u, gr)


def _scan_bwd(dy, gr, hr, a, *, cb, name):
    t, r = a.shape
    nb = r // cb
    seg = t // SUBLANES

    def body(dy_ref, g_ref, h_ref, a_ref, dh_ref, da_ref, dg_ref, q_ref):
        gl, dgl = _gelu_and_grad(g_ref[...])
        dyv = dy_ref[...]
        dh_ref[...] = dyv * gl
        dg_ref[...] = (dyv * h_ref[...] * dgl).astype(BF16)

        def step(k, carry):
            cin, q = carry
            rows = pl.ds(seg - 1 - k, SUBLANES, stride=seg)
            dh = dh_ref[rows, :] + cin
            dh_ref[rows, :] = dh
            q_ref[rows, :] = q
            av = a_ref[rows, :]
            return av * dh, av * q

        c_fin, q_fin = lax.fori_loop(0, seg, step, (jnp.zeros((SUBLANES, cb), F32), jnp.ones((SUBLANES, cb), F32)),
                                     unroll=8)
        carry = c_fin[SUBLANES - 1:SUBLANES, :]
        for s in range(SUBLANES - 2, -1, -1):
            rows = slice(s * seg, (s + 1) * seg)
            dh_ref[rows, :] = dh_ref[rows, :] + q_ref[rows, :] * carry
            carry = c_fin[s:s + 1, :] + q_fin[s:s + 1, :] * carry
        da_ref[...] = dh_ref[...] * _shift_down(h_ref[...], 1)

    blk = pl.BlockSpec((t, cb), lambda j: (0, j))
    return pl.pallas_call(
        body, name=name, grid=(nb,),
        in_specs=[blk, blk, blk, blk],
        out_specs=[blk, blk, blk],
        out_shape=[jax.ShapeDtypeStruct((t, r), F32), jax.ShapeDtypeStruct((t, r), F32),
                   jax.ShapeDtypeStruct((t, r), BF16)],
        scratch_shapes=[pltpu.VMEM((t, cb), F32)],
        compiler_params=_params("parallel"),
    )(dy, gr, hr, a)


def _gates_bwd(rec, r, i, a, dh, da, bd_r, bd_i, vecs, *, tm, name):
    t, r_dim = rec.shape
    nb, cb, _ = bd_r.shape

    def body(x_ref, r_ref, i_ref, a_ref, dh_ref, da_ref, wr_ref, wi_ref, v_ref, dx_ref, dpr_ref, dpi_ref, dv_ref):
        step = pl.program_id(1)
        x, r, i, a, dh, da = x_ref[...], r_ref[...], i_ref[...], a_ref[...], dh_ref[...], da_ref[...]
        lam = v_ref[...][2:3, :]
        sp = _softplus(-lam)
        a2 = a * a
        mult = jnp.sqrt(-_expm1(2.0 * (-LRU_C) * r * sp))
        d_i = dh * mult * x
        d_log_a = da * a - (dh * i * x) * a2 / mult
        d_r = d_log_a * ((-LRU_C) * sp)
        d_sp = jnp.sum(d_log_a * ((-LRU_C) * r), axis=0, keepdims=True)
        d_pre_r = d_r * r * (1.0 - r)
        d_pre_i = d_i * i * (1.0 - i)
        dprb = d_pre_r.astype(BF16)
        dpib = d_pre_i.astype(BF16)
        dx_ref[...] = dh * mult * i + _dot_nt(dprb, wr_ref[...]) + _dot_nt(dpib, wi_ref[...])
        dpr_ref[...] = dprb
        dpi_ref[...] = dpib
        upd = _rows8([jnp.sum(d_pre_r, axis=0, keepdims=True), jnp.sum(d_pre_i, axis=0, keepdims=True),
                      -d_sp * _sigmoid(-lam)], cb)

        @pl.when(step == 0)
        def _():
            dv_ref[...] = upd

        @pl.when(step > 0)
        def _():
            dv_ref[...] += upd

    blk = pl.BlockSpec((tm, cb), lambda j, i: (i, j))
    wspec = pl.BlockSpec((None, cb, cb), lambda j, i: (j, 0, 0))
    vspec = pl.BlockSpec((8, cb), lambda j, i: (0, j))
    return pl.pallas_call(
        body, name=name, grid=(nb, t // tm),
        in_specs=[blk] * 6 + [wspec, wspec, vspec],
        out_specs=[blk, blk, blk, vspec],
        out_shape=[jax.ShapeDtypeStruct((t, r_dim), F32), jax.ShapeDtypeStruct((t, r_dim), BF16),
                   jax.ShapeDtypeStruct((t, r_dim), BF16), jax.ShapeDtypeStruct((8, r_dim), F32)],
        compiler_params=_params("parallel", "arbitrary"),
    )(rec, r, i, a, dh, da, bd_r, bd_i, vecs)


def _bd_grad(rec, dpr, dpi, *, cb, name):
    t, r = rec.shape
    nb = r // cb

    def body(x_ref, dr_ref, di_ref, gr_ref, gi_ref):
        xb = x_ref[...].astype(BF16)
        gr_ref[...] = _dot_tn(xb, dr_ref[...])
        gi_ref[...] = _dot_tn(xb, di_ref[...])

    blk = pl.BlockSpec((t, cb), lambda j: (0, j))
    wspec = pl.BlockSpec((None, cb, cb), lambda j: (j, 0, 0))
    out = jax.ShapeDtypeStruct((nb, cb, cb), F32)
    return pl.pallas_call(
        body, name=name, grid=(nb,),
        in_specs=[blk, blk, blk], out_specs=[wspec, wspec], out_shape=[out, out],
        compiler_params=_params("parallel"),
    )(rec, dpr, dpi)


def _conv_a_bwd(d_rec, gr, dgate, cwb, *, cb, name):
    t, r = d_rec.shape
    nb = r // cb

    def body(dy_ref, x_ref, dg_ref, w_ref, dact_ref, dw_ref):
        dx, dw = _conv_taps_bwd(dy_ref[...], x_ref[...], w_ref[...], 4)
        dact_ref[0] = dg_ref[...]
        dact_ref[1] = dx.astype(BF16)
        dw_ref[...] = dw

    blk = pl.BlockSpec((t, cb), lambda j: (0, j))
    vspec = pl.BlockSpec((8, cb), lambda j: (0, j))
    return pl.pallas_call(
        body, name=name, grid=(nb,),
        in_specs=[blk, pl.BlockSpec((t, cb), lambda j: (0, j + nb)), blk, vspec],
        out_specs=[pl.BlockSpec((2, t, cb), lambda j: (0, 0, j)), vspec],
        out_shape=[jax.ShapeDtypeStruct((2, t, r), BF16), jax.ShapeDtypeStruct((8, r), F32)],
        compiler_params=_params("parallel"),
    )(d_rec, gr, dgate, cwb)


def _split3(x):
    p0 = x.astype(BF16)
    r1 = x - p0.astype(F32)
    p1 = r1.astype(BF16)
    p2 = (r1 - p1.astype(F32)).astype(BF16)
    return p0, p1, p2


def _fgate_fwd(fp, fb, *, tq, name):
    t = fp.shape[0]

    def body(f_ref, b_ref, c_ref, ct_ref):
        logf = -_softplus(-(f_ref[...] + b_ref[...]))
        rows = pl.program_id(0) * tq + lax.broadcasted_iota(jnp.int32, (tq, t), 0)
        cols = lax.broadcasted_iota(jnp.int32, (tq, t), 1)
        tri = (cols <= rows).astype(BF16)
        p0, p1, p2 = _split3(logf)
        c = _dot_nn(tri, p0) + _dot_nn(tri, p1) + _dot_nn(tri, p2)
        c_ref[...] = c
        ct_ref[...] = c.T

    return pl.pallas_call(
        body, name=name, grid=(t // tq,),
        in_specs=[pl.BlockSpec((t, LANES), lambda i: (0, 0)), pl.BlockSpec((1, LANES), lambda i: (0, 0))],
        out_specs=[pl.BlockSpec((tq, LANES), lambda i: (i, 0)), pl.BlockSpec((LANES, tq), lambda i: (0, i))],
        out_shape=[jax.ShapeDtypeStruct((t, LANES), F32), jax.ShapeDtypeStruct((LANES, t), F32)],
        compiler_params=_params("parallel"),
    )(fp, fb)


def _fgate_bwd(dct, fp, fb, *, tq, name):
    t = fp.shape[0]

    def body(d_ref, f_ref, b_ref, o_ref, db_ref):
        i = pl.program_id(0)
        rows = lax.broadcasted_iota(jnp.int32, (t, tq), 0)
        cols = i * tq + lax.broadcasted_iota(jnp.int32, (t, tq), 1)
        tri = (rows >= cols).astype(BF16)
        p0, p1, p2 = _split3(d_ref[...])
        dlogf = (_dot_nn(p0, tri) + _dot_nn(p1, tri) + _dot_nn(p2, tri)).T
        df = dlogf * _sigmoid(-(f_ref[...] + b_ref[...]))
        o_ref[...] = df.astype(BF16)
        upd = _rows8([jnp.sum(df, axis=0, keepdims=True)], LANES)

        @pl.when(i == 0)
        def _():
            db_ref[...] = upd

        @pl.when(i > 0)
        def _():
            db_ref[...] += upd

    return pl.pallas_call(
        body, name=name, grid=(t // tq,),
        in_specs=[pl.BlockSpec((LANES, t), lambda i: (0, 0)), pl.BlockSpec((tq, LANES), lambda i: (i, 0)),
                  pl.BlockSpec((1, LANES), lambda i: (0, 0))],
        out_specs=[pl.BlockSpec((tq, LANES), lambda i: (i, 0)), pl.BlockSpec((8, LANES), lambda i: (0, 0))],
        out_shape=[jax.ShapeDtypeStruct((t, LANES), BF16), jax.ShapeDtypeStruct((8, LANES), F32)],
        compiler_params=_params("arbitrary"),
    )(dct, fp, fb)


def _pair_sum(a, b, *, tm, name):
    t, d = a.shape

    def body(a_ref, b_ref, o_ref):
        o_ref[...] = (a_ref[...] + b_ref[...]).astype(BF16)

    row = pl.BlockSpec((tm, d), lambda i: (i, 0))
    return pl.pallas_call(
        body, name=name, grid=(t // tm,), in_specs=[row, row], out_specs=row,
        out_shape=jax.ShapeDtypeStruct((t, d), BF16), compiler_params=_params("parallel"),
    )(a, b)


FWD_HEAD_TILES = 2
BWD_HEAD_TILES = 1


def _head_block_width(dh, tiles):
    return tiles * LANES if tiles * LANES // dh <= 8 else LANES


def _head_masks(dh, bw):
    lane = lax.broadcasted_iota(jnp.int32, (1, bw), 1)
    return [((lane >= e * dh) & (lane < (e + 1) * dh)) for e in range(bw // dh)]


def _head_c_row(ct_blk, head):
    sub = lax.broadcasted_iota(jnp.int32, ct_blk.shape, 0)
    return jnp.sum(jnp.where(sub == head, ct_blk, 0.0), axis=0, keepdims=True)


def _attn_weights(qm, k, c_row, q0):
    tq, t = qm.shape[0], k.shape[0]
    s = _dot_nt(qm, k) - c_row
    qi = q0 + lax.broadcasted_iota(jnp.int32, (tq, t), 0)
    ki = lax.broadcasted_iota(jnp.int32, (tq, t), 1)
    s = jnp.where(ki <= qi, s, -jnp.inf)
    m = jnp.max(s, axis=-1, keepdims=True)
    e = jnp.exp(s - m)
    return e, m, 1.0 / jnp.sum(e, axis=-1, keepdims=True)


def _key_buckets(t, tq):
    return tuple(sorted({min(-(-(i * tq) // LANES) * LANES, t) for i in range(1, t // tq + 1)}))


def _for_prefix(needed, buckets, fn):
    prev = 0
    for length in buckets:
        pl.when((needed > prev) & (needed <= length))(lambda length=length: fn(length))
        prev = length


def _attn_fwd(qg, kv, ct, *, tq, name):
    t, d2 = qg.shape
    d = d2 // 2
    dh = d // N_HEADS
    bw = _head_block_width(dh, FWD_HEAD_TILES)
    hpb = bw // dh
    nhb = d // bw
    scale = dh ** -0.5
    buckets = _key_buckets(t, tq)

    def body(q_ref, og_ref, k_ref, v_ref, ct_ref, o_ref, y_ref, st_ref):
        hb = pl.program_id(0)
        q0 = pl.program_id(1) * tq

        def run(length):
            qs = q_ref[...] * scale
            k = k_ref[0:length, :]
            v = v_ref[0:length, :]
            o = jnp.zeros((tq, bw), F32)
            lane = lax.broadcasted_iota(jnp.int32, (tq, LANES), 1)
            stats = jnp.zeros((tq, LANES), F32)
            for e, msk in enumerate(_head_masks(dh, bw)):
                c_row = _head_c_row(ct_ref[:, 0:length], hb * hpb + e)
                w, m, inv = _attn_weights(jnp.where(msk, qs, 0.0).astype(BF16), k, c_row, q0)
                o = o + _dot_nn(w.astype(BF16), jnp.where(msk, v, jnp.zeros_like(v))) * inv
                stats = jnp.where(lane == e, m, jnp.where(lane == hpb + e, inv, stats))
            o_ref[...] = o
            y_ref[...] = (o * _sigmoid(og_ref[...])).astype(BF16)
            st_ref[...] = stats

        _for_prefix(q0 + tq, buckets, run)

    qblk = pl.BlockSpec((tq, bw), lambda h, i: (i, h))
    return pl.pallas_call(
        body, name=name, grid=(nhb, t // tq),
        in_specs=[qblk, pl.BlockSpec((tq, bw), lambda h, i: (i, h + nhb)),
                  pl.BlockSpec((t, bw), lambda h, i: (0, h)), pl.BlockSpec((t, bw), lambda h, i: (0, h + nhb)),
                  pl.BlockSpec((N_HEADS, t), lambda h, i: (0, 0))],
        out_specs=[qblk, pl.BlockSpec((None, tq, bw), lambda h, i: (0, i, h)),
                   pl.BlockSpec((None, tq, LANES), lambda h, i: (h, i, 0))],
        out_shape=[jax.ShapeDtypeStruct((t, d), F32), jax.ShapeDtypeStruct((1, t, d), BF16),
                   jax.ShapeDtypeStruct((nhb, t, LANES), F32)],
        compiler_params=_params("parallel", "parallel"),
    )(qg, qg, kv, kv, ct)


def _attn_bwd(dy, qg, o, stats, kv, ct, *, tq, name):
    t, d2 = qg.shape
    d = d2 // 2
    dh = d // N_HEADS
    bw = _head_block_width(dh, BWD_HEAD_TILES)
    hpb = bw // dh
    nhb = d // bw
    scale = dh ** -0.5
    n_q = t // tq
    hpb_f = _head_block_width(dh, FWD_HEAD_TILES) // dh
    ratio = hpb_f // hpb
    chunk = 8 * LANES

    def body(dy_ref, q_ref, og_ref, o_ref, st_ref, k_ref, v_ref, ct_ref, dqg_ref, dk_ref, dv_ref, dc_ref, dcq_ref):
        hb = pl.program_id(0)
        step = pl.program_id(1)

        @pl.when(step == 0)
        def _():
            dk_ref[...] = jnp.zeros((t, bw), F32)
            dv_ref[...] = jnp.zeros((t, bw), F32)
            dc_ref[...] = jnp.zeros((8, t), F32)

        def run(i):
            q0 = i * tq
            length = min(-(-(q0 + tq) // LANES) * LANES, t)
            qs = q_ref[...] * scale
            sg = _sigmoid(og_ref[...])
            dyv = dy_ref[...]
            ov = o_ref[...]
            do = dyv * sg
            dqg_ref[1] = (dyv * ov * sg * (1.0 - sg)).astype(BF16)
            lane = lax.broadcasted_iota(jnp.int32, (tq, LANES), 1)
            stats = st_ref[...]
            masks = _head_masks(dh, bw)
            heads = []
            for e, msk in enumerate(masks):
                pos = (hb % ratio) * hpb + e
                m = jnp.sum(jnp.where(lane == pos, stats, 0.0), axis=1, keepdims=True)
                inv = jnp.sum(jnp.where(lane == hpb_f + pos, stats, 0.0), axis=1, keepdims=True)
                delta = jnp.sum(jnp.where(msk, do * ov, 0.0), axis=1, keepdims=True)
                heads.append((msk, m, inv, delta, jnp.where(msk, qs, 0.0).astype(BF16),
                              jnp.where(msk, do, 0.0).astype(BF16)))
            dq = jnp.zeros((tq, bw), F32)
            dcq = jnp.zeros((tq, LANES), F32)
            row_acc = [jnp.zeros((tq, chunk), F32) for _ in heads]
            for c0 in range(0, length, chunk):
                ch = min(chunk, length - c0)
                k = k_ref[c0:c0 + ch, :]
                v = v_ref[c0:c0 + ch, :]
                dk = jnp.zeros((ch, bw), F32)
                dv = jnp.zeros((ch, bw), F32)
                dc_rows = []
                for e, (msk, m, inv, delta, qm, dom) in enumerate(heads):
                    c_row = _head_c_row(ct_ref[:, c0:c0 + ch], hb * hpb + e)
                    s = _dot_nt(qm, k) - c_row
                    if c0 + ch - 1 > q0:
                        qi = q0 + lax.broadcasted_iota(jnp.int32, (tq, ch), 0)
                        ki = c0 + lax.broadcasted_iota(jnp.int32, (tq, ch), 1)
                        s = jnp.where(ki <= qi, s, -jnp.inf)
                    p = jnp.exp(s - m) * inv
                    dsc = p * (_dot_nt(dom, v) - delta)
                    dsb = dsc.astype(BF16)
                    dq = dq + _dot_nn(dsb, jnp.where(msk, k, jnp.zeros_like(k)))
                    dk = dk + _dot_tn(dsb, qm)
                    dv = dv + _dot_tn(p.astype(BF16), dom)
                    dc_rows.append(-jnp.sum(dsc, axis=0, keepdims=True))
                    if ch == chunk:
                        row_acc[e] = row_acc[e] + dsc
                    else:
                        dcq = dcq + jnp.where(lane == e, jnp.sum(dsc, axis=1, keepdims=True), 0.0)
                dk_ref[c0:c0 + ch, :] += dk
                dv_ref[c0:c0 + ch, :] += dv
                dc_ref[:, c0:c0 + ch] += _rows8(dc_rows, ch)
            dqg_ref[0] = (dq * scale).astype(BF16)
            for e in range(len(heads)):
                dcq = dcq + jnp.where(lane == e, jnp.sum(row_acc[e], axis=1, keepdims=True), 0.0)
            dcq_ref[...] = dcq

        for i in range(n_q):
            pl.when(step == i)(lambda i=i: run(i))

    qblk = pl.BlockSpec((tq, bw), lambda h, i: (i, h))
    kblk = pl.BlockSpec((t, bw), lambda h, i: (0, h))
    return pl.pallas_call(
        body, name=name, grid=(nhb, n_q),
        in_specs=[qblk, qblk, pl.BlockSpec((tq, bw), lambda h, i: (i, h + nhb)), qblk,
                  pl.BlockSpec((None, tq, LANES), lambda h, i: (h // ratio, i, 0)),
                  kblk, pl.BlockSpec((t, bw), lambda h, i: (0, h + nhb)),
                  pl.BlockSpec((N_HEADS, t), lambda h, i: (0, 0))],
        out_specs=[pl.BlockSpec((2, tq, bw), lambda h, i: (0, i, h)), kblk, kblk,
                   pl.BlockSpec((None, 8, t), lambda h, i: (h, 0, 0)),
                   pl.BlockSpec((None, tq, LANES), lambda h, i: (h, i, 0))],
        out_shape=[jax.ShapeDtypeStruct((2, t, d), BF16), jax.ShapeDtypeStruct((t, d), F32),
                   jax.ShapeDtypeStruct((t, d), F32), jax.ShapeDtypeStruct((nhb, 8, t), F32),
                   jax.ShapeDtypeStruct((nhb, t, LANES), F32)],
        compiler_params=_params("parallel", "arbitrary"),
    )(dy, qg, qg, o, stats, kv, kv, ct)


def _loss_bwd(h, tgt, *, lo, hi, tm, name):
    t, d = h.shape

    def body(h_ref, t_ref, l_ref, dy_ref):
        i = pl.program_id(0)
        rows = i * tm + lax.broadcasted_iota(jnp.int32, (tm, d), 0)
        err = jnp.where((rows >= lo) & (rows < hi), h_ref[...] - t_ref[...], 0.0)
        dy_ref[...] = err * (1.0 / d)
        part = jnp.sum(jnp.sum(err * err, axis=0, keepdims=True), axis=1, keepdims=True) * (0.5 / d)
        upd = jnp.broadcast_to(part, (8, LANES))

        @pl.when(i == 0)
        def _():
            l_ref[...] = upd

        @pl.when(i > 0)
        def _():
            l_ref[...] += upd

    row = pl.BlockSpec((tm, d), lambda i: (i, 0))
    return pl.pallas_call(
        body, name=name, grid=(t // tm,),
        in_specs=[row, row],
        out_specs=[pl.BlockSpec((8, LANES), lambda i: (0, 0)), row],
        out_shape=[jax.ShapeDtypeStruct((8, LANES), F32), jax.ShapeDtypeStruct((t, d), F32)],
        compiler_params=_params("arbitrary"),
    )(h, tgt)


def _adamw_math(w, gv, m, v):
    bc1 = 1.0 / (1.0 - ADAM_B1 ** ADAM_STEP)
    bc2 = 1.0 / (1.0 - ADAM_B2 ** ADAM_STEP)
    nm = ADAM_B1 * m + (1.0 - ADAM_B1) * gv
    nv = ADAM_B2 * v + (1.0 - ADAM_B2) * (gv * gv)
    delta = (-ADAM_LR) * ((nm * bc1) / (jnp.sqrt(nv * bc2) + ADAM_EPS) + ADAM_WD * w)
    return delta, nm, nv


def _adamw(w, g, m, v, *, name):
    r, c = w.shape
    tr = r
    for cand in (512, 256, 128, 64, 32, 16, 8):
        if r % cand == 0 and r > cand:
            tr = cand
            break

    def body(w_ref, g_ref, m_ref, v_ref, d_ref, nm_ref, nv_ref):
        d_ref[...], nm_ref[...], nv_ref[...] = _adamw_math(w_ref[...], g_ref[...], m_ref[...], v_ref[...])

    blk = pl.BlockSpec((tr, c), lambda i: (i, 0))
    out = jax.ShapeDtypeStruct((r, c), F32)
    return pl.pallas_call(
        body, name=name, grid=(r // tr,),
        in_specs=[blk] * 4, out_specs=[blk] * 3, out_shape=[out] * 3,
        compiler_params=_params("parallel"),
    )(w, g, m, v)


def _sum_adamw(recvs, sends, me, w, m, v, *, name):
    n_l = len(recvs)
    _, r, c = recvs[0].shape
    tr = _tile(r, (256, 192, 176, 128, 96, 64, 48, 32, 16))

    def body(me_ref, *refs):
        p_refs, own_refs = refs[:n_l], refs[n_l:2 * n_l]
        w_ref, m_ref, v_ref, g_ref, d_ref, nm_ref, nv_ref, acc_ref = refs[2 * n_l:]
        layer = pl.program_id(0)
        mine = me_ref[0]
        for k in range(n_l):
            @pl.when(layer == k)
            def _(k=k):
                acc_ref[...] = jnp.zeros((tr, c), F32)
                for dev in range(N_DEV):
                    @pl.when(mine == dev)
                    def _():
                        acc_ref[...] += own_refs[k][...].astype(F32)

                    @pl.when(mine != dev)
                    def _(dev=dev):
                        acc_ref[...] += p_refs[k][dev].astype(F32)
                acc = acc_ref[...]
                g_ref[...] = acc
                d_ref[...], nm_ref[...], nv_ref[...] = _adamw_math(w_ref[...], acc, m_ref[...], v_ref[...])

    p_specs = [pl.BlockSpec((N_DEV, tr, c), lambda l, i, me_ref, k=k: (0, jnp.where(l == k, i, 0), 0))
               for k in range(n_l)]
    own_specs = [pl.BlockSpec((None, tr, c), lambda l, i, me_ref, k=k: (me_ref[0], jnp.where(l == k, i, 0), 0))
                 for k in range(n_l)]
    blk = pl.BlockSpec((None, tr, c), lambda l, i, me_ref: (l, i, 0))
    out = jax.ShapeDtypeStruct((n_l, r, c), F32)
    return pl.pallas_call(
        body, name=name,
        grid_spec=pltpu.PrefetchScalarGridSpec(
            num_scalar_prefetch=1, grid=(n_l, r // tr),
            in_specs=p_specs + own_specs + [blk] * 3, out_specs=[blk] * 4,
            scratch_shapes=[pltpu.VMEM((tr, c), F32)]),
        out_shape=[out] * 4,
        compiler_params=_params("arbitrary", "arbitrary"),
    )(me, *recvs, *sends, w, m, v)


def _sum8(parts, *, name):
    _, r, c = parts.shape
    tr = r
    for cand in (512, 256, 128, 64, 32, 16):
        if r % cand == 0 and r > cand:
            tr = cand
            break

    def body(p_ref, o_ref):
        acc = p_ref[0].astype(F32)
        for k in range(1, N_DEV):
            acc = acc + p_ref[k].astype(F32)
        o_ref[...] = acc

    return pl.pallas_call(
        body, name=name, grid=(r // tr,),
        in_specs=[pl.BlockSpec((N_DEV, tr, c), lambda i: (0, i, 0))],
        out_specs=pl.BlockSpec((tr, c), lambda i: (i, 0)),
        out_shape=jax.ShapeDtypeStruct((r, c), F32),
        compiler_params=_params("parallel"),
    )(parts)


def _my_index():
    return 4 * lax.axis_index("x") + 2 * lax.axis_index("y") + lax.axis_index("c")


def _peer(k):
    x, y, c = lax.axis_index("x"), lax.axis_index("y"), lax.axis_index("c")
    px = x ^ ((k >> 2) & 1)
    py = y ^ ((k >> 1) & 1)
    pc = c ^ (k & 1)
    return (px, py, pc), 4 * px + 2 * py + pc


def _all_gather(shards, *, name):
    n_arr = len(shards)

    def body(*refs):
        ins, outs = refs[:n_arr], refs[n_arr:2 * n_arr]
        send_sems, recv_sems, local_sems = refs[2 * n_arr:]
        me = _my_index()
        local = [pltpu.make_async_copy(ins[n], outs[n].at[me], local_sems.at[n]) for n in range(n_arr)]
        for cp in local:
            cp.start()
        sends = []
        for k in range(1, N_DEV):
            peer, _ = _peer(k)
            for n in range(n_arr):
                cp = pltpu.make_async_remote_copy(
                    src_ref=ins[n], dst_ref=outs[n].at[me], send_sem=send_sems.at[n, k - 1],
                    recv_sem=recv_sems.at[n, k - 1], device_id=peer, device_id_type=pl.DeviceIdType.MESH)
                cp.start()
                sends.append(cp)
        for k in range(1, N_DEV):
            peer, pidx = _peer(k)
            for n in range(n_arr):
                pltpu.make_async_remote_copy(
                    src_ref=ins[n], dst_ref=outs[n].at[pidx], send_sem=send_sems.at[n, k - 1],
                    recv_sem=recv_sems.at[n, k - 1], device_id=peer, device_id_type=pl.DeviceIdType.MESH).wait_recv()
        for cp in sends:
            cp.wait_send()
        for cp in local:
            cp.wait()

    hbm = pl.BlockSpec(memory_space=pl.ANY)
    return pl.pallas_call(
        body, name=name,
        in_specs=[hbm] * n_arr, out_specs=[hbm] * n_arr,
        out_shape=[jax.ShapeDtypeStruct((N_DEV,) + s.shape, s.dtype) for s in shards],
        scratch_shapes=[pltpu.SemaphoreType.DMA((n_arr, N_DEV - 1)), pltpu.SemaphoreType.DMA((n_arr, N_DEV - 1)),
                        pltpu.SemaphoreType.DMA((n_arr,))],
        compiler_params=pltpu.CompilerParams(has_side_effects=True),
    )(*shards)


_HBM = pl.BlockSpec(memory_space=pltpu.HBM)
_SEM = pl.BlockSpec(memory_space=pltpu.SEMAPHORE)
_EFFECT = pltpu.SideEffectType.DATAFLOW_SIDE_EFFECTING


def _remote(src, dst, send_sem, recv_sem, peer):
    return pltpu.make_async_remote_copy(src_ref=src, dst_ref=dst, send_sem=send_sem, recv_sem=recv_sem,
                                        device_id=peer, device_id_type=pl.DeviceIdType.MESH)


def _place_own(src, layer, me, *, out_dtype, name):
    _, r, c = src.shape
    tr = _tile(r, (256, 192, 176, 128, 96, 64, 48, 32, 16))

    def body(me_ref, s_ref, o_ref):
        o_ref[...] = s_ref[...].astype(out_dtype)

    return pl.pallas_call(
        body, name=name,
        grid_spec=pltpu.PrefetchScalarGridSpec(
            num_scalar_prefetch=1, grid=(r // tr,),
            in_specs=[pl.BlockSpec((None, tr, c), lambda i, me_ref: (layer, i, 0))],
            out_specs=pl.BlockSpec((None, tr, c), lambda i, me_ref: (me_ref[0], i, 0))),
        out_shape=jax.ShapeDtypeStruct((N_DEV, r, c), out_dtype),
        compiler_params=_params("parallel"),
    )(me, src)


def _own_blocks(srcs, *, name):
    n = len(srcs)

    def body(*refs):
        ins, outs, sems = refs[:n], refs[n:2 * n], refs[2 * n]
        me = _my_index()
        cps = [pltpu.make_async_copy(ins[t].at[me], outs[t].at[me], sems.at[t]) for t in range(n)]
        for cp in cps:
            cp.start()
        for cp in cps:
            cp.wait()

    return pl.pallas_call(
        body, name=name, in_specs=[_HBM] * n, out_specs=[_HBM] * n,
        out_shape=[jax.ShapeDtypeStruct(s.shape, s.dtype) for s in srcs],
        scratch_shapes=[pltpu.SemaphoreType.DMA((n,))],
    )(*srcs)


def _split_start(groups, *, scatter, name):
    sizes = [len(srcs) for srcs, _ in groups]
    flat_src = [s for srcs, _ in groups for s in srcs]
    flat_land = [l for _, lands in groups for l in lands]
    n, n_g = len(flat_land), len(groups)
    if not scatter:
        flat_src = []
    n_in = len(flat_src) + n

    def body(*refs):
        lands = refs[n_in - n:n_in]
        ins = refs[:n] if scatter else lands
        sems = refs[n_in:n_in + 2 * n_g]
        token = refs[-1]
        me = _my_index()
        t = 0
        for g in range(n_g):
            for q in range(sizes[g]):
                for k in range(1, N_DEV):
                    peer, pidx = _peer(k)
                    src = ins[t].at[pidx] if scatter else ins[t].at[me]
                    slot = q * (N_DEV - 1) + k - 1
                    _remote(src, lands[t].at[me], sems[2 * g].at[slot], sems[2 * g + 1].at[slot], peer).start()
                t += 1
        token[...] = jnp.zeros_like(token)

    sem_shapes = []
    for sz in sizes:
        sem_shapes += [pltpu.SemaphoreType.DMA((sz * (N_DEV - 1),)), pltpu.SemaphoreType.DMA((sz * (N_DEV - 1),))]
    outs = pl.pallas_call(
        body, name=name,
        in_specs=[_HBM] * n_in,
        out_specs=[_SEM] * (2 * n_g) + [_HBM] * n_in + [pl.BlockSpec(memory_space=pltpu.VMEM)],
        out_shape=sem_shapes + [pltpu.HBM(a.shape, a.dtype) for a in flat_src + flat_land]
        + [jax.ShapeDtypeStruct((8, LANES), F32)],
        input_output_aliases={i: 2 * n_g + i for i in range(n_in)},
        compiler_params=pltpu.CompilerParams(has_side_effects=_EFFECT),
    )(*[pltpu.with_memory_space_constraint(a, pltpu.HBM) for a in flat_src + flat_land])
    sems, thru, token = outs[:2 * n_g], outs[2 * n_g:2 * n_g + n_in], outs[-1]
    handles, pos = [], 0
    for g, sz in enumerate(sizes):
        lands_g = thru[n_in - n + pos:n_in - n + pos + sz]
        handles.append((sems[2 * g], sems[2 * g + 1], thru[pos:pos + sz] if scatter else [], lands_g))
        pos += sz
    return handles, token


def _split_wait(handle, after, *, scatter, name):
    send_sems, recv_sems, srcs, lands = handle
    n, n_src = len(lands), len(srcs)

    def body(*refs):
        lnd = refs[n_src:n_src + n]
        ins = refs[:n_src] if scatter else lnd
        ssem, rsem = refs[n_src + n], refs[n_src + n + 1]
        me = _my_index()
        for t in range(n):
            for k in range(1, N_DEV):
                peer, pidx = _peer(k)
                block = ins[t].at[me]
                slot = t * (N_DEV - 1) + k - 1
                _remote(block, lnd[t].at[me], ssem.at[slot], rsem.at[slot], peer).wait_send()
                _remote(block, lnd[t].at[pidx], ssem.at[slot], rsem.at[slot], peer).wait_recv()

    return pl.pallas_call(
        body, name=name,
        in_specs=[_HBM] * (n_src + n) + [_SEM, _SEM, pl.BlockSpec(memory_space=pl.ANY)],
        out_specs=[_HBM] * n,
        out_shape=[pltpu.HBM(l.shape, l.dtype) for l in lands],
        input_output_aliases={n_src + t: t for t in range(n)},
        compiler_params=pltpu.CompilerParams(has_side_effects=_EFFECT),
    )(*srcs, *lands, send_sems, recv_sems, after)


def _pack(arrs, dtype, row_quantum=16):
    flat = jnp.concatenate([a.astype(dtype).reshape(-1) for a in arrs])
    pad = (-flat.shape[0]) % (row_quantum * PACK_COLS)
    if pad:
        flat = jnp.concatenate([flat, jnp.zeros((pad,), dtype)])
    return flat.reshape(-1, PACK_COLS)


def _pack8(arrs, dtype):
    flat = jnp.concatenate([a.astype(dtype).reshape(N_DEV, -1) for a in arrs], axis=1)
    pad = (-flat.shape[1]) % (16 * PACK_COLS)
    if pad:
        flat = jnp.concatenate([flat, jnp.zeros((N_DEV, pad), dtype)], axis=1)
    return flat.reshape(N_DEV, -1, PACK_COLS)


def _unpack(slab, shapes, lead):
    lead_shape = slab.shape[:lead]
    flat = slab.reshape(lead_shape + (-1,))
    outs, off = [], 0
    for shp in shapes:
        size = math.prod(shp)
        outs.append(flat[..., off:off + size].reshape(lead_shape + tuple(shp)))
        off += size
    return outs


def _cols_full(g):
    g = jnp.moveaxis(g, 0, -2)
    return g.reshape(g.shape[:-2] + (g.shape[-2] * g.shape[-1],))


def _cols_split(full):
    n = full.shape[-1] // N_DEV
    return jnp.moveaxis(full.reshape(full.shape[:-1] + (N_DEV, n)), -2, 0)


def _block_diag(w, per):
    n, b, _ = w.shape
    w4 = w.reshape(n // per, per, b, b)
    eye = jnp.eye(per, dtype=w.dtype)
    return jnp.einsum('gpab,pq->gpaqb', w4, eye).reshape(n // per, per * b, per * b)


def _block_diag_extract(g, per):
    gn, cb, _ = g.shape
    b = cb // per
    g5 = g.reshape(gn, per, b, per, b)
    return jnp.stack([g5[:, p, :, p, :] for p in range(per)], axis=1).reshape(gn * per, b, b)


def _slab2d(a):
    return a.reshape(-1, a.shape[-1])


def _lru_block_cols(r_dim):
    lru = r_dim // N_LRU_BLOCKS
    return lru * LANES // math.gcd(lru, LANES)


BIG = ("a_w_in", "a_w_out", "b_w_in", "b_w_out", "f_w_in", "f_w_out")
COL_F32 = ("meta", "a_conv_w", "a_conv_b", "a_b_r", "a_b_i", "a_lambda", "f_conv_w")
REPLICATED = ("a_w_r", "a_w_i", "kv_f_b", "f_conv_b", "ln1_g", "ln1_b", "ln2_g", "ln2_b")
WEIGHT_NAMES = ("meta", "a_w_in", "a_conv_w", "a_conv_b", "a_w_r", "a_b_r", "a_w_i", "a_b_i", "a_lambda", "a_w_out",
                "kv_w", "kv_f_b", "b_w_in", "b_w_out", "f_w_in", "f_conv_w", "f_conv_b", "f_w_out",
                "ln1_g", "ln1_b", "ln2_g", "ln2_b")


def _kv_layout(kv_gathered, d):
    kv_full = _cols_full(kv_gathered)
    kv_pad = 2 * d + LANES - kv_full.shape[1]
    return jnp.concatenate([kv_full, jnp.zeros((d, kv_pad), kv_full.dtype)], axis=1)


def _small_layouts(small):
    r_dim = small["a_lambda"].shape[1]
    n_f = small["f_conv_b"].shape[1] // N_DEV
    cb = _lru_block_cols(r_dim)
    per = cb // (r_dim // N_LRU_BLOCKS)
    n_a = small["a_lambda"].shape[0]
    f_conv_w3 = small["f_conv_w"].reshape(N_LAYERS, 3, N_DEV, n_f).transpose(0, 2, 1, 3)
    f_conv_b3 = small["f_conv_b"].reshape(N_LAYERS, N_DEV, 1, n_f)
    return {
        "kv_fb": jnp.concatenate([small["kv_f_b"], jnp.zeros((LANES - N_HEADS,), F32)])[None],
        "a_cwb": jnp.concatenate([small["a_conv_w"], small["a_conv_b"][:, None],
                                  jnp.zeros((n_a, 3, r_dim), F32)], axis=1),
        "a_vecs": jnp.concatenate([jnp.stack([small["a_b_r"], small["a_b_i"], small["a_lambda"]], axis=1),
                                   jnp.zeros((n_a, 5, r_dim), F32)], axis=1),
        "a_bd_r": jnp.stack([_block_diag(small["a_w_r"][l], per) for l in range(n_a)]).astype(BF16),
        "a_bd_i": jnp.stack([_block_diag(small["a_w_i"][l], per) for l in range(n_a)]).astype(BF16),
        "f_cwb3": jnp.concatenate([f_conv_w3, f_conv_b3, jnp.zeros((N_LAYERS, N_DEV, 4, n_f), F32)], axis=2),
        "ln1_g": small["ln1_g"][:, None], "ln1_b": small["ln1_b"][:, None],
        "ln2_g": small["ln2_g"][:, None], "ln2_b": small["ln2_b"][:, None],
    }


def _local_step(h0, tgt, n_meta, n_tok, wts, hooks):
    tp, d = h0.shape
    tm = tp // 8 if (tp // 8) % 16 == 0 else tp
    tmb = _tile(tp, (1088, 512, 320, 256, 128))
    tq = 128
    tqa_fwd = tp // 4 if tp % 64 == 0 else tq
    tqa_bwd = tp // 4 if tp % 64 == 0 else tq
    r_dim = wts["a_vecs"].shape[2]
    cb = wts["a_bd_r"].shape[-1]
    sb = LANES
    n_b = N_LAYERS - N_A_LAYERS

    h, h_bf = h0, h0.astype(BF16)
    saved = []
    kvs = None
    for layer in range(N_LAYERS):
        lw = {}
        sv = {"h_bf": h_bf, "w": lw}
        if layer < N_A_LAYERS:
            lw["in"] = hooks.weight(layer, "in", h)
            sv["gr"] = _proj_in(h_bf, lw["in"], shard_major=False, name="a_in_proj")
            sv["rec"] = _conv_a_fwd(sv["gr"], wts["a_cwb"][layer], cb=cb, name="a_conv_fwd")
            a, u, sv["r"], sv["i"] = _gates_fwd(sv["rec"], wts["a_bd_r"][layer], wts["a_bd_i"][layer],
                                                wts["a_vecs"][layer], tm=tmb // 2, name="a_gates_fwd")
            sv["a"] = a
            sv["hr"], y3 = _scan_fwd(a, u, sv["gr"], cb=sb, name="a_scan_fwd")
        else:
            j = layer - N_A_LAYERS
            if j == 0:
                kv_w = _kv_layout(hooks.weight(layer, "kv_w", h), d)
                kvs = {"h_bf": h_bf, "w": kv_w}
                kvs["kv"] = _mm_nn(h_bf, kv_w[:, :2 * d], tn=_tile(2 * d, (512, 256, 128)), out_dtype=BF16,
                                   name="kv_proj")
                kvs["fp"] = _mm_nn(h_bf, kv_w[:, 2 * d:], tn=LANES, out_dtype=F32, name="f_proj")
                kvs["c"], ct = _fgate_fwd(kvs["fp"], wts["kv_fb"], tq=tq, name="fgate_fwd")
                kvs["ct"] = ct[:N_HEADS]
            lw["in"] = hooks.weight(layer, "in", kvs["c"] if j == 0 else h)
            sv["qg"] = _proj_in(h_bf, lw["in"], shard_major=False, name="b_in_proj")
            sv["o"], y3, sv["st"] = _attn_fwd(sv["qg"], kvs["kv"], kvs["ct"], tq=tqa_fwd, name="attn_fwd")
        sv["y3"] = y3
        lw["out"] = hooks.weight(layer, "out", y3)
        sv["s1"], h, h_bf = _out_ln(y3, lw["out"], h, wts["ln1_g"][layer], wts["ln1_b"][layer], n_valid=n_tok,
                                    tm=tmb // 2, name="mix_out_ln")
        sv["h1_bf"] = h_bf
        lw["f_in"] = hooks.weight(layer, "f_in", h)
        sv["z3"] = _proj_in(h_bf, lw["f_in"], shard_major=True, transposed=True, name="f_in_proj")
        sv["yf3"] = _convglu_fwd(sv["z3"], wts["f_cwb3"][layer], name="f_convglu_fwd")
        lw["f_out"] = hooks.weight(layer, "f_out", sv["yf3"])
        sv["s2"], h, h_bf = _out_ln(sv["yf3"], lw["f_out"], h, wts["ln2_g"][layer], wts["ln2_b"][layer],
                                    n_valid=n_tok, tm=tmb // 2, name="ffn_out_ln")
        saved.append(sv)

    loss_tile, dh = _loss_bwd(h, tgt, lo=n_meta, hi=n_tok, tm=tm, name="loss")

    grads = {k: [None] * N_LAYERS for k in ("f_cwb3", "ln1_gb", "ln2_gb")}
    grads.update({k: [None] * N_A_LAYERS for k in ("a_cwb", "a_bd_r", "a_bd_i", "a_vecs")})
    dkv = []
    token = jnp.zeros((), F32)
    for layer in reversed(range(N_LAYERS)):
        sv = saved[layer]
        lw = sv["w"]
        big = {}
        ds, ds_bf, grads["ln2_gb"][layer] = _ln_bwd(dh, sv["s2"], wts["ln2_g"][layer] + token, tm=tmb // 2,
                                                    name="ln_bwd")
        dz, dcw = _ffn_bwd_mid(ds_bf, lw["f_out"], sv["z3"], wts["f_cwb3"][layer], name="f_bwd_mid")
        grads["f_cwb3"][layer] = dcw.reshape((N_DEV,) + dcw.shape[2:])
        dz3 = dz
        big["f_out"] = _w_out_grad(sv["yf3"], ds_bf, lw["f_out"].shape[1], name="f_w_out_grad")
        dh = _in_bwd(dz3, lw["f_in"], ds, tm=tmb, transposed=True, name="f_in_bwd")
        big["f_in"] = _w_in_grad(sv["h1_bf"], dz3, transposed=True, name="f_w_in_grad")
        token = hooks.grads_ready(layer, "ffn", big)
        big = {}
        ds, ds_bf, grads["ln1_gb"][layer] = _ln_bwd(dh, sv["s1"], wts["ln1_g"][layer] + token, tm=tmb // 2,
                                                    name="ln_bwd")
        if layer < N_A_LAYERS:
            dy = _out_bwd(ds_bf, lw["out"], tm=tmb // 2, name="a_out_bwd")
            big["out"] = _w_out_grad(sv["y3"], ds_bf, lw["out"].shape[1], name="a_w_out_grad")
            d_h, d_a, dgate = _scan_bwd(dy, sv["gr"], sv["hr"], sv["a"], cb=sb, name="a_scan_bwd")
            d_rec, dpr, dpi, grads["a_vecs"][layer] = _gates_bwd(
                sv["rec"], sv["r"], sv["i"], sv["a"], d_h, d_a, wts["a_bd_r"][layer], wts["a_bd_i"][layer],
                wts["a_vecs"][layer], tm=tmb // 2, name="a_gates_bwd")
            grads["a_bd_r"][layer], grads["a_bd_i"][layer] = _bd_grad(sv["rec"], dpr, dpi, cb=cb, name="a_bd_grad")
            dact, grads["a_cwb"][layer] = _conv_a_bwd(d_rec, sv["gr"], dgate, wts["a_cwb"][layer], cb=cb,
                                                      name="a_conv_bwd")
            dh = _in_bwd(dact, lw["in"], ds, tm=tmb, name="a_in_bwd")
            big["in"] = _w_in_grad(sv["h_bf"], dact, name="a_w_in_grad")
        else:
            j = layer - N_A_LAYERS
            dy = _out_bwd(ds_bf, lw["out"], tm=tmb // 2, name="b_out_bwd")
            big["out"] = _w_out_grad(sv["y3"], ds_bf, lw["out"].shape[1], name="b_w_out_grad")
            dqg, dk, dv, dc, dcq = _attn_bwd(dy, sv["qg"], sv["o"], sv["st"], kvs["kv"], kvs["ct"], tq=tqa_bwd,
                                             name="attn_bwd")
            dkv.append((dk, dv, dc, dcq))
            dh = _in_bwd(dqg, lw["in"], ds, tm=tmb, name="b_in_bwd")
            big["in"] = _w_in_grad(sv["h_bf"], dqg, name="b_w_in_grad")
            if j == 0:
                hpb = _head_block_width(d // N_HEADS, BWD_HEAD_TILES) // (d // N_HEADS)
                dct = (dkv[0][2] + dkv[1][2])[:, :hpb, :].reshape(N_HEADS, tp)
                dcq = (dkv[0][3] + dkv[1][3])[:, :, :hpb]
                dct = dct + jnp.transpose(dcq, (0, 2, 1)).reshape(N_HEADS, tp)
                dct = jnp.concatenate([dct, jnp.zeros((LANES - N_HEADS, tp), F32)])
                df_bf, grads["kv_fb"] = _fgate_bwd(dct, kvs["fp"], wts["kv_fb"], tq=tq, name="fgate_bwd")
                dkvz = jnp.concatenate([_pair_sum(dkv[0][0], dkv[1][0], tm=tm, name="kv_pair_sum"),
                                        _pair_sum(dkv[0][1], dkv[1][1], tm=tm, name="kv_pair_sum"), df_bf], axis=1)
                dh = _mm_nt_full(dkvz, kvs["w"], dh, tm=tmb // 2, name="kv_in_bwd")
                big["kv_w"] = _mm_tn_cols(kvs["h_bf"], dkvz, tn=LANES, name="kv_w_grad")
        token = hooks.grads_ready(layer, "mix", big)
    return loss_tile, dh, grads


def _finish_small_grads(grads, d_h0, n_meta):
    r_dim = grads["a_vecs"][0].shape[1]
    per = _lru_block_cols(r_dim) // (r_dim // N_LRU_BLOCKS)
    a_cwb = jnp.stack(grads["a_cwb"])
    a_vecs = jnp.stack(grads["a_vecs"])
    f_cwb3 = jnp.stack(grads["f_cwb3"])
    ln1 = jnp.stack(grads["ln1_gb"])
    ln2 = jnp.stack(grads["ln2_gb"])
    f_rows = f_cwb3.transpose(0, 2, 1, 3).reshape(N_LAYERS, 8, -1)
    return {
        "meta": d_h0[:n_meta],
        "a_conv_w": a_cwb[:, :4], "a_conv_b": a_cwb[:, 4],
        "a_w_r": jnp.stack([_block_diag_extract(g, per) for g in grads["a_bd_r"]]),
        "a_b_r": a_vecs[:, 0],
        "a_w_i": jnp.stack([_block_diag_extract(g, per) for g in grads["a_bd_i"]]),
        "a_b_i": a_vecs[:, 1], "a_lambda": a_vecs[:, 2],
        "kv_f_b": grads["kv_fb"][0, :N_HEADS],
        "f_conv_w": f_rows[:, :3], "f_conv_b": f_rows[:, 3],
        "ln1_g": ln1[:, 0], "ln1_b": ln1[:, 1], "ln2_g": ln2[:, 0], "ln2_b": ln2[:, 1],
    }


def kernel(x, meta, a_w_in, a_conv_w, a_conv_b, a_w_r, a_b_r, a_w_i, a_b_i, a_lambda, a_w_out, kv_w, kv_f_b, b_w_in, b_w_out, f_w_in, f_conv_w, f_conv_b, f_w_out, ln1_g, ln1_b, ln2_g, ln2_b, loss_target, m_meta, m_a_w_in, m_a_conv_w, m_a_conv_b, m_a_w_r, m_a_b_r, m_a_w_i, m_a_b_i, m_a_lambda, m_a_w_out, m_kv_w, m_kv_f_b, m_b_w_in, m_b_w_out, m_f_w_in, m_f_conv_w, m_f_conv_b, m_f_w_out, m_ln1_g, m_ln1_b, m_ln2_g, m_ln2_b, v_meta, v_a_w_in, v_a_conv_w, v_a_conv_b, v_a_w_r, v_a_b_r, v_a_w_i, v_a_b_i, v_a_lambda, v_a_w_out, v_kv_w, v_kv_f_b, v_b_w_in, v_b_w_out, v_f_w_in, v_f_conv_w, v_f_conv_b, v_f_w_out, v_ln1_g, v_ln1_b, v_ln2_g, v_ln2_b):
    w = dict(meta=meta, a_w_in=a_w_in, a_conv_w=a_conv_w, a_conv_b=a_conv_b, a_w_r=a_w_r, a_b_r=a_b_r, a_w_i=a_w_i,
             a_b_i=a_b_i, a_lambda=a_lambda, a_w_out=a_w_out, kv_w=kv_w, kv_f_b=kv_f_b, b_w_in=b_w_in,
             b_w_out=b_w_out, f_w_in=f_w_in, f_conv_w=f_conv_w, f_conv_b=f_conv_b, f_w_out=f_w_out, ln1_g=ln1_g,
             ln1_b=ln1_b, ln2_g=ln2_g, ln2_b=ln2_b)
    m = dict(meta=m_meta, a_w_in=m_a_w_in, a_conv_w=m_a_conv_w, a_conv_b=m_a_conv_b, a_w_r=m_a_w_r, a_b_r=m_a_b_r,
             a_w_i=m_a_w_i, a_b_i=m_a_b_i, a_lambda=m_a_lambda, a_w_out=m_a_w_out, kv_w=m_kv_w, kv_f_b=m_kv_f_b,
             b_w_in=m_b_w_in, b_w_out=m_b_w_out, f_w_in=m_f_w_in, f_conv_w=m_f_conv_w, f_conv_b=m_f_conv_b,
             f_w_out=m_f_w_out, ln1_g=m_ln1_g, ln1_b=m_ln1_b, ln2_g=m_ln2_g, ln2_b=m_ln2_b)
    v = dict(meta=v_meta, a_w_in=v_a_w_in, a_conv_w=v_a_conv_w, a_conv_b=v_a_conv_b, a_w_r=v_a_w_r, a_b_r=v_a_b_r,
             a_w_i=v_a_w_i, a_b_i=v_a_b_i, a_lambda=v_a_lambda, a_w_out=v_a_w_out, kv_w=v_kv_w, kv_f_b=v_kv_f_b,
             b_w_in=v_b_w_in, b_w_out=v_b_w_out, f_w_in=v_f_w_in, f_conv_w=v_f_conv_w, f_conv_b=v_f_conv_b,
             f_w_out=v_f_w_out, ln1_g=v_ln1_g, ln1_b=v_ln1_b, ln2_g=v_ln2_g, ln2_b=v_ln2_b)
    shapes = {n: w[n].shape for n in WEIGHT_NAMES}

    me = jnp.reshape(_my_index(), (1,)).astype(jnp.int32)

    def as_stored(name, a):
        return jnp.swapaxes(a, 1, 2) if name == "f_w_in" else a

    param_of = {"in": ("a_w_in", "b_w_in"), "out": ("a_w_out", "b_w_out"), "f_in": ("f_w_in",) * 2,
                "f_out": ("f_w_out",) * 2}
    order = [("small", None, None)]
    for layer in range(N_LAYERS):
        if layer == N_A_LAYERS:
            order.append(("kv_w", layer, 0))
        for key in ("in", "out", "f_in", "f_out"):
            order.append((key, layer, layer if key[0] == "f" or layer < N_A_LAYERS else layer - N_A_LAYERS))
    def place(key, layer, idx):
        if key == "small":
            return _place_own(_pack([w[n] for n in COL_F32], F32)[None], 0, me, out_dtype=F32, name="place_small")
        if key == "kv_w":
            return _place_own(w["kv_w"][None], 0, me, out_dtype=BF16, name="place_kv_w")
        name = param_of[key][0 if layer < N_A_LAYERS else 1]
        return _place_own(as_stored(name, w[name]), idx, me, out_dtype=BF16, name=f"place_{name}_{idx}")

    lands = [place(*o) for o in order]
    gather_handles, gather_token = _split_start([([l], [l]) for l in lands], scatter=False, name="gather_start")
    group_of = {(key, layer): g for g, (key, layer, _) in enumerate(order)}
    (got_s,) = _split_wait(gather_handles[0], gather_token, scatter=False, name="gather_wait_small")
    small = {n: w[n] for n in REPLICATED}
    for n, part in zip(COL_F32, _unpack(got_s, [w[n].shape for n in COL_F32], 1)):
        small[n] = _cols_full(part)
    n_meta, d = small["meta"].shape

    class Hooks:
        pending = None
        received = {}
        sent = {}

        @staticmethod
        def weight(layer, key, after):
            (got,) = _split_wait(gather_handles[group_of[(key, layer)]], after, scatter=False,
                                 name=f"gather_wait_{key}_{layer}")
            return got

        @staticmethod
        def collect(after):
            if Hooks.pending is not None:
                tag, names, handle = Hooks.pending
                got = _split_wait(handle, after, scatter=True, name=f"scatter_wait_{tag}")
                Hooks.received.update(zip(names, got))
                Hooks.pending = None

        @staticmethod
        def grads_ready(layer, part, big):
            if "kv_w" in big:
                big["kv_w"] = _cols_split(big["kv_w"][:, :shapes["kv_w"][1] * N_DEV]).astype(BF16)
            names = [(key, layer) for key in big]
            send = [big[key] for key in big]
            Hooks.collect(send[0])
            empty = [lax.empty(s.shape, s.dtype) for s in send]
            handles, token = _split_start([(send, empty)], scatter=True, name=f"scatter_start_{part}_{layer}")
            Hooks.pending = (f"{part}_{layer}", names, handles[0])
            Hooks.sent.update(zip(names, handles[0][2]))
            return token[0, 0]

    Hooks.pending, Hooks.received, Hooks.sent = None, {}, {}

    n_tok = n_meta + x.shape[1]
    tp = -(-n_tok // ROW_ALIGN) * ROW_ALIGN
    pad = jnp.zeros((tp - n_tok, d), F32)
    h0 = jnp.concatenate([small["meta"], x[0], pad])
    tgt = jnp.concatenate([jnp.zeros((n_meta, d), F32), loss_target[0], pad])
    loss_tile, d_h0, grads = _local_step(h0, tgt, n_meta, n_tok, _small_layouts(small), Hooks)
    g_small = _finish_small_grads(grads, d_h0, n_meta)
    loss = lax.psum(loss_tile[0, 0], MESH_AXES)
    grad_x = d_h0[n_meta:n_tok][None]

    rep = _pack([g_small[n] for n in REPLICATED], F32, row_quantum=16 * N_DEV)
    send = [_pack8([_cols_split(g_small[n]) for n in COL_F32], F32), rep.reshape(N_DEV, -1, PACK_COLS)]
    lands = _own_blocks(send, name="scatter_own_small")
    handles, token = _split_start([(send, lands)], scatter=True, name="scatter_start_small")

    g, delta, new_m, new_v = {}, {}, {}, {}
    layers_of = {
        "a_w_in": [("in", l) for l in range(N_A_LAYERS)], "a_w_out": [("out", l) for l in range(N_A_LAYERS)],
        "b_w_in": [("in", l) for l in range(N_A_LAYERS, N_LAYERS)],
        "b_w_out": [("out", l) for l in range(N_A_LAYERS, N_LAYERS)],
        "f_w_in": [("f_in", l) for l in range(N_LAYERS)], "f_w_out": [("f_out", l) for l in range(N_LAYERS)],
        "kv_w": [("kv_w", N_A_LAYERS)],
    }
    ready = [n for n in BIG + ("kv_w",) if all(t in Hooks.received for t in layers_of[n])]

    def done(names):
        return jnp.stack([g[n][(0,) * g[n].ndim] for n in names])

    for n in ready + [n for n in BIG + ("kv_w",) if n not in ready]:
        if n not in ready and Hooks.pending is not None:
            Hooks.collect(done(ready))
        lift = (lambda a: a[None]) if n == "kv_w" else (lambda a, n=n: as_stored(n, a))
        outs = _sum_adamw([Hooks.received[t] for t in layers_of[n]], [Hooks.sent[t] for t in layers_of[n]], me,
                          lift(w[n]), lift(m[n]), lift(v[n]), name="sum_adamw_" + n)
        g[n], delta[n], new_m[n], new_v[n] = [as_stored(n, o).reshape(shapes[n]) for o in outs]
    recv_s, recv_r = _split_wait(handles[0], done(BIG + ("kv_w",)), scatter=True, name="scatter_wait_small")
    sum_s = _sum8(recv_s, name="sum_grads_f32")
    g.update(zip(COL_F32, _unpack(sum_s, [shapes[n] for n in COL_F32], 0)))
    (got_r,) = _all_gather([_sum8(recv_r, name="sum_grads_replicated")], name="gather_replicated_sums")
    g.update(zip(REPLICATED, _unpack(got_r.reshape(-1, PACK_COLS), [shapes[n] for n in REPLICATED], 0)))

    for n in COL_F32 + REPLICATED:
        shp = shapes[n]
        dl, nm, nv = _adamw(_slab2d(w[n]), _slab2d(g[n]), _slab2d(m[n]), _slab2d(v[n]), name="adamw")
        delta[n], new_m[n], new_v[n] = dl.reshape(shp), nm.reshape(shp), nv.reshape(shp)
    return (loss, grad_x, *[g[n] for n in WEIGHT_NAMES], *[delta[n] for n in WEIGHT_NAMES],
            *[new_m[n] for n in WEIGHT_NAMES], *[new_v[n] for n in WEIGHT_NAMES])
```

```python
import math

import jax
import jax.numpy as jnp
from jax import lax
from jax.experimental import pallas as pl
from jax.experimental.pallas import tpu as pltpu

F32 = jnp.float32
BF16 = jnp.bfloat16

N_DEV = 8
MESH_AXES = ("x", "y", "c")
N_LAYERS = 4
N_A_LAYERS = 2
N_LRU_BLOCKS = 16
N_HEADS = 16
LRU_C = 8.0
DN_ALPHA = (2 * N_LAYERS) ** 0.25
LN_EPS = 1e-5
ADAM_LR, ADAM_B1, ADAM_B2, ADAM_EPS, ADAM_WD, ADAM_STEP = 0.001, 0.9, 0.999, 1e-08, 0.01, 10

LANES = 128
SUBLANES = 8
SCAN_GROUPS = 2
ROW_ALIGN = 128
VMEM_LIMIT_BYTES = 56 * 1024 * 1024
GELU_K = math.sqrt(2.0 / math.pi)
GELU_C = 0.044715
PACK_COLS = 1024


def _params(*sem):
    return pltpu.CompilerParams(dimension_semantics=sem, vmem_limit_bytes=VMEM_LIMIT_BYTES)


def _gelu(x):
    th = jnp.tanh(GELU_K * (x + GELU_C * x * x * x))
    return 0.5 * x * (1.0 + th)


def _gelu_and_grad(x):
    x2 = x * x
    th = jnp.tanh(GELU_K * (x + GELU_C * x2 * x))
    g = 0.5 * x * (1.0 + th)
    dg = 0.5 * (1.0 + th) + 0.5 * x * (1.0 - th * th) * (GELU_K * (1.0 + 3.0 * GELU_C * x2))
    return g, dg


def _sigmoid(x):
    return 0.5 * jnp.tanh(0.5 * x) + 0.5


def _expm1(x):
    small = x * (1.0 + 0.5 * x * (1.0 + (1.0 / 3.0) * x * (1.0 + 0.25 * x)))
    return jnp.where(jnp.abs(x) < 1e-2, small, jnp.exp(x) - 1.0)


def _softplus(x):
    e = jnp.exp(-jnp.abs(x))
    small = e * (1.0 - 0.5 * e * (1.0 - (2.0 / 3.0) * e))
    return jnp.maximum(x, 0.0) + jnp.where(e < 1e-2, small, jnp.log(1.0 + e))


def _shift_down(x, s):
    if s == 0:
        return x
    rows = lax.broadcasted_iota(jnp.int32, x.shape, 0)
    return jnp.where(rows >= s, pltpu.roll(x, s, 0), 0.0)


def _shift_up(x, s):
    if s == 0:
        return x
    n = x.shape[0]
    rows = lax.broadcasted_iota(jnp.int32, x.shape, 0)
    return jnp.where(rows < n - s, pltpu.roll(x, n - s, 0), 0.0)


def _dot_nn(a, b):
    return lax.dot_general(a, b, (((1,), (0,)), ((), ())), preferred_element_type=F32)


def _dot_nt(a, b):
    return lax.dot_general(a, b, (((1,), (1,)), ((), ())), preferred_element_type=F32)


def _dot_tn(a, b):
    return lax.dot_general(a, b, (((0,), (0,)), ((), ())), preferred_element_type=F32)


def _rows8(vals, width):
    rows = lax.broadcasted_iota(jnp.int32, (8, width), 0)
    out = jnp.zeros((8, width), F32)
    for k, v in enumerate(vals):
        out = jnp.where(rows == k, jnp.broadcast_to(v, (8, width)), out)
    return out


def _tile(n, prefer):
    for c in prefer:
        if n % c == 0:
            return c
    return n


def _mm_nn(a, b, *, tn, out_dtype, name):
    m, k = a.shape
    n = b.shape[1]

    def body(a_ref, b_ref, o_ref):
        o_ref[...] = _dot_nn(a_ref[...], b_ref[...]).astype(o_ref.dtype)

    return pl.pallas_call(
        body, name=name, grid=(n // tn,),
        in_specs=[pl.BlockSpec((m, k), lambda j: (0, 0)), pl.BlockSpec((k, tn), lambda j: (0, j))],
        out_specs=pl.BlockSpec((m, tn), lambda j: (0, j)),
        out_shape=jax.ShapeDtypeStruct((m, n), out_dtype),
        compiler_params=_params("parallel"),
    )(a, b)


def _proj_in(h_bf, g_in, *, shard_major, name, transposed=False):
    t, k = h_bf.shape
    n = g_in.shape[1] if transposed else g_in.shape[2]

    def body(a_ref, b_ref, o_ref):
        o_ref[...] = _dot_nt(a_ref[...], b_ref[...]) if transposed else _dot_nn(a_ref[...], b_ref[...])

    if shard_major:
        out_spec = pl.BlockSpec((None, t, n), lambda j: (j, 0, 0))
        out_shape = jax.ShapeDtypeStruct((N_DEV, t, n), F32)
    else:
        out_spec = pl.BlockSpec((t, n), lambda j: (0, j))
        out_shape = jax.ShapeDtypeStruct((t, N_DEV * n), F32)
    return pl.pallas_call(
        body, name=name, grid=(N_DEV,),
        in_specs=[pl.BlockSpec((t, k), lambda j: (0, 0)),
                  pl.BlockSpec((None,) + g_in.shape[1:], lambda j: (j, 0, 0))],
        out_specs=out_spec, out_shape=out_shape,
        compiler_params=_params("parallel"),
    )(h_bf, g_in)


def _out_ln(y3, g_out, hin, g, b, *, n_valid, tm, name):
    nj, t, kj = y3.shape
    _, r, d = g_out.shape

    def body(y_ref, w_ref, hin_ref, g_ref, b_ref, s_ref, h_ref, hb_ref):
        w = w_ref[...].reshape(N_DEV * r, d)
        s = DN_ALPHA * hin_ref[...]
        for jj in range(nj):
            s = s + _dot_nn(y_ref[jj], w[jj * kj:(jj + 1) * kj])
        mu = jnp.mean(s, axis=-1, keepdims=True)
        xc = s - mu
        var = jnp.mean(xc * xc, axis=-1, keepdims=True)
        h = xc * lax.rsqrt(var + LN_EPS) * g_ref[...] + b_ref[...]
        s_ref[...] = s
        h_ref[...] = h
        rows = pl.program_id(0) * tm + lax.broadcasted_iota(jnp.int32, (tm, d), 0)
        hb_ref[...] = jnp.where(rows < n_valid, h, 0.0).astype(BF16)

    row = pl.BlockSpec((tm, d), lambda i: (i, 0))
    vec = pl.BlockSpec((1, d), lambda i: (0, 0))
    return pl.pallas_call(
        body, name=name, grid=(t // tm,),
        in_specs=[pl.BlockSpec((nj, tm, kj), lambda i: (0, i, 0)),
                  pl.BlockSpec((N_DEV, r, d), lambda i: (0, 0, 0)), row, vec, vec],
        out_specs=[row, row, row],
        out_shape=[jax.ShapeDtypeStruct((t, d), F32), jax.ShapeDtypeStruct((t, d), F32),
                   jax.ShapeDtypeStruct((t, d), BF16)],
        compiler_params=_params("parallel"),
    )(y3, g_out, hin, g, b)


def _out_bwd(ds_bf, g_out, *, tm, name):
    t, d = ds_bf.shape
    r = g_out.shape[1]

    def body(a_ref, w_ref, o_ref):
        o_ref[...] = _dot_nt(a_ref[...], w_ref[...].reshape(N_DEV * r, d))

    return pl.pallas_call(
        body, name=name, grid=(t // tm,),
        in_specs=[pl.BlockSpec((tm, d), lambda i: (i, 0)),
                  pl.BlockSpec((N_DEV, r, d), lambda i: (0, 0, 0))],
        out_specs=pl.BlockSpec((tm, N_DEV * r), lambda i: (i, 0)),
        out_shape=jax.ShapeDtypeStruct((t, N_DEV * r), F32),
        compiler_params=_params("parallel"),
    )(ds_bf, g_out)


def _in_bwd(dact, g_in, add, *, tm, name, alpha=DN_ALPHA, transposed=False):
    t = dact.shape[-2]
    _, k, n = g_in.shape
    if transposed:
        k, n = n, k
    halves = dact.shape[0] == 2 and dact.ndim == 3
    per = N_DEV // 2

    def body(a_ref, b_ref, add_ref, o_ref, acc_ref):
        j = pl.program_id(1)

        @pl.when(j == 0)
        def _():
            acc_ref[...] = alpha * add_ref[...]

        acc_ref[...] += _dot_nn(a_ref[...], b_ref[...]) if transposed else _dot_nt(a_ref[...], b_ref[...])

        @pl.when(j == N_DEV - 1)
        def _():
            o_ref[...] = acc_ref[...]

    if halves:
        a_spec = pl.BlockSpec((None, tm, n), lambda i, j: (j // per, i, j % per))
    elif dact.ndim == 4:
        a_spec = pl.BlockSpec((None, None, tm, n), lambda i, j: (j // per, j % per, i, 0))
    else:
        a_spec = pl.BlockSpec((None, tm, n), lambda i, j: (j, i, 0))
    return pl.pallas_call(
        body, name=name, grid=(t // tm, N_DEV),
        in_specs=[a_spec, pl.BlockSpec((None,) + g_in.shape[1:], lambda i, j: (j, 0, 0)),
                  pl.BlockSpec((tm, k), lambda i, j: (i, 0))],
        out_specs=pl.BlockSpec((tm, k), lambda i, j: (i, 0)),
        out_shape=jax.ShapeDtypeStruct((t, k), F32),
        scratch_shapes=[pltpu.VMEM((tm, k), F32)],
        compiler_params=_params("parallel", "arbitrary"),
    )(dact, g_in, add)


def _mm_nt_full(a, b, add, *, tm, name):
    t, n = a.shape
    k = b.shape[0]

    def body(a_ref, b_ref, add_ref, o_ref):
        o_ref[...] = add_ref[...] + _dot_nt(a_ref[...], b_ref[...])

    return pl.pallas_call(
        body, name=name, grid=(t // tm,),
        in_specs=[pl.BlockSpec((tm, n), lambda i: (i, 0)), pl.BlockSpec((k, n), lambda i: (0, 0)),
                  pl.BlockSpec((tm, k), lambda i: (i, 0))],
        out_specs=pl.BlockSpec((tm, k), lambda i: (i, 0)),
        out_shape=jax.ShapeDtypeStruct((t, k), F32),
        compiler_params=_params("parallel"),
    )(a, b, add)


def _w_in_grad(h_bf, dact, *, name, transposed=False):
    t, k = h_bf.shape
    halves = dact.shape[0] == 2 and dact.ndim == 3
    per = N_DEV // 2
    n = dact.shape[-1] // per if halves else dact.shape[-1]

    def body(a_ref, b_ref, o_ref):
        if transposed:
            o_ref[...] = _dot_tn(b_ref[...], a_ref[...]).astype(BF16)
        else:
            o_ref[...] = _dot_tn(a_ref[...], b_ref[...]).astype(BF16)

    if halves:
        b_spec = pl.BlockSpec((None, t, n), lambda j: (j // per, 0, j % per))
    elif dact.ndim == 4:
        b_spec = pl.BlockSpec((None, None, t, n), lambda j: (j // per, j % per, 0, 0))
    else:
        b_spec = pl.BlockSpec((None, t, n), lambda j: (j, 0, 0))
    return pl.pallas_call(
        body, name=name, grid=(N_DEV,),
        in_specs=[pl.BlockSpec((t, k), lambda j: (0, 0)), b_spec],
        out_specs=pl.BlockSpec((None, n, k) if transposed else (None, k, n), lambda j: (j, 0, 0)),
        out_shape=jax.ShapeDtypeStruct((N_DEV, n, k) if transposed else (N_DEV, k, n), BF16),
        compiler_params=_params("parallel"),
    )(h_bf, dact)


def _w_out_grad(y3, ds_bf, r, *, name):
    nj, t, kj = y3.shape
    d = ds_bf.shape[1]
    unit = r * LANES // math.gcd(r, LANES)
    ks = max([c for c in range(unit, min(kj, 768) + 1, unit) if kj % c == 0], default=kj)
    gsz = ks // r
    per = kj // ks

    def body(a_ref, b_ref, o_ref):
        o_ref[...] = _dot_tn(a_ref[...], b_ref[...]).reshape(gsz, r, d).astype(BF16)

    return pl.pallas_call(
        body, name=name, grid=(nj * per,),
        in_specs=[pl.BlockSpec((None, t, ks), lambda j: (j // per, 0, j % per)),
                  pl.BlockSpec((t, d), lambda j: (0, 0))],
        out_specs=pl.BlockSpec((gsz, r, d), lambda j: (j, 0, 0)),
        out_shape=jax.ShapeDtypeStruct((N_DEV, r, d), BF16),
        compiler_params=_params("parallel"),
    )(y3, ds_bf)


def _mm_tn_cols(a, b, *, tn, name):
    t, m = a.shape
    n = b.shape[1]

    def body(a_ref, b_ref, o_ref):
        o_ref[...] = _dot_tn(a_ref[...], b_ref[...])

    return pl.pallas_call(
        body, name=name, grid=(n // tn,),
        in_specs=[pl.BlockSpec((t, m), lambda j: (0, 0)), pl.BlockSpec((t, tn), lambda j: (0, j))],
        out_specs=pl.BlockSpec((m, tn), lambda j: (0, j)),
        out_shape=jax.ShapeDtypeStruct((m, n), F32),
        compiler_params=_params("parallel"),
    )(a, b)


def _ln_bwd(dout, s, g, *, tm, name):
    t, d = s.shape

    def body(do_ref, s_ref, g_ref, ds_ref, dsb_ref, gb_ref):
        i = pl.program_id(0)
        sv = s_ref[...]
        do = do_ref[...]
        mu = jnp.mean(sv, axis=-1, keepdims=True)
        xc = sv - mu
        var = jnp.mean(xc * xc, axis=-1, keepdims=True)
        rstd = lax.rsqrt(var + LN_EPS)
        xhat = xc * rstd
        dxhat = do * g_ref[...]
        m1 = jnp.mean(dxhat, axis=-1, keepdims=True)
        m2 = jnp.mean(dxhat * xhat, axis=-1, keepdims=True)
        ds = rstd * (dxhat - m1 - xhat * m2)
        ds_ref[...] = ds
        dsb_ref[...] = ds.astype(BF16)
        upd = _rows8([jnp.sum(do * xhat, axis=0, keepdims=True), jnp.sum(do, axis=0, keepdims=True)], d)

        @pl.when(i == 0)
        def _():
            gb_ref[...] = upd

        @pl.when(i > 0)
        def _():
            gb_ref[...] += upd

    row = pl.BlockSpec((tm, d), lambda i: (i, 0))
    return pl.pallas_call(
        body, name=name, grid=(t // tm,),
        in_specs=[row, row, pl.BlockSpec((1, d), lambda i: (0, 0))],
        out_specs=[row, row, pl.BlockSpec((8, d), lambda i: (0, 0))],
        out_shape=[jax.ShapeDtypeStruct((t, d), F32), jax.ShapeDtypeStruct((t, d), BF16),
                   jax.ShapeDtypeStruct((8, d), F32)],
        compiler_params=_params("arbitrary"),
    )(dout, s, g)


def _roll_down(x, s):
    return x if s == 0 else pltpu.roll(x, s, 0)


def _conv_taps(x, wb, width):
    y = jnp.broadcast_to(wb[width:width + 1, :], x.shape)
    for k in range(width):
        y = y + _roll_down(x, width - 1 - k) * wb[k:k + 1, :]
    return y


def _conv_taps_bwd(dy, x, wb, width):
    n = dy.shape[0]
    dx = jnp.zeros_like(dy)
    rows = []
    for k in range(width):
        s = width - 1 - k
        dy_up = dy if s == 0 else pltpu.roll(dy, n - s, 0)
        dx = dx + dy_up * wb[k:k + 1, :]
        rows.append(jnp.sum(dy_up * x, axis=0, keepdims=True))
    rows.append(jnp.sum(dy, axis=0, keepdims=True))
    t_idx = lax.broadcasted_iota(jnp.int32, dy.shape, 0)
    return jnp.where(t_idx < n - (width - 1), dx, 0.0), _rows8(rows, dy.shape[1])


def _convglu_fwd(z3, fwb3, *, name):
    _, t, n = z3.shape
    half = N_DEV // 2
    nc = pl.cdiv(n, LANES)

    def body(zg_ref, zv_ref, wg_ref, wv_ref, y_ref):
        gate = _conv_taps(zg_ref[...], wg_ref[...], 3)
        val = _conv_taps(zv_ref[...], wv_ref[...], 3)
        y_ref[...] = (_gelu(gate) * val).astype(BF16)

    zblk = lambda off: pl.BlockSpec((None, t, LANES), lambda j, c: (j + off, 0, c))
    wblk = lambda off: pl.BlockSpec((None, 8, LANES), lambda j, c: (j + off, 0, c))
    return pl.pallas_call(
        body, name=name, grid=(half, nc),
        in_specs=[zblk(0), zblk(half), wblk(0), wblk(half)],
        out_specs=zblk(0),
        out_shape=jax.ShapeDtypeStruct((half, t, n), BF16),
        compiler_params=_params("parallel", "parallel"),
    )(z3, z3, fwb3, fwb3)


def _ffn_bwd_mid(ds_bf, g_out, z3, fwb3, *, name):
    t, d = ds_bf.shape
    r = g_out.shape[1]
    n = z3.shape[2]
    half = N_DEV // 2
    nc = pl.cdiv(n, LANES)
    assert n == 2 * r

    def body(ds_ref, w_ref, zg_ref, zv_ref, wg_ref, wv_ref, dz_ref, dwb_ref, wsc_ref):
        c = pl.program_id(1)

        @pl.when(c == 0)
        def _():
            wsc_ref[0:r, :] = w_ref[0]
            wsc_ref[r:2 * r, :] = w_ref[1]
            if nc * LANES > n:
                wsc_ref[n:nc * LANES, :] = jnp.zeros((nc * LANES - n, d), BF16)

        w = wsc_ref[pl.ds(pl.multiple_of(c * LANES, LANES), LANES), :]
        dyf = _dot_nt(ds_ref[...], w)
        zg, zv = zg_ref[...], zv_ref[...]
        wg, wv = wg_ref[...], wv_ref[...]
        gate = _conv_taps(zg, wg, 3)
        val = _conv_taps(zv, wv, 3)
        gl, dgl = _gelu_and_grad(gate)
        dzg, dwg = _conv_taps_bwd(dyf * val * dgl, zg, wg, 3)
        dzv, dwv = _conv_taps_bwd(dyf * gl, zv, wv, 3)
        dz_ref[0] = dzg.astype(BF16)
        dz_ref[1] = dzv.astype(BF16)
        dwb_ref[0] = dwg
        dwb_ref[1] = dwv

    zblk = lambda off: pl.BlockSpec((None, t, LANES), lambda j, c: (j + off, 0, c))
    wblk = lambda off: pl.BlockSpec((None, 8, LANES), lambda j, c: (j + off, 0, c))
    return pl.pallas_call(
        body, name=name, grid=(half, nc),
        in_specs=[pl.BlockSpec((t, d), lambda j, c: (0, 0)),
                  pl.BlockSpec((2, r, d), lambda j, c: (j, 0, 0)),
                  zblk(0), zblk(half), wblk(0), wblk(half)],
        out_specs=[pl.BlockSpec((2, None, t, LANES), lambda j, c: (0, j, 0, c)),
                   pl.BlockSpec((2, None, 8, LANES), lambda j, c: (0, j, 0, c))],
        out_shape=[jax.ShapeDtypeStruct((2, half, t, n), BF16), jax.ShapeDtypeStruct((2, half, 8, n), F32)],
        scratch_shapes=[pltpu.VMEM((nc * LANES, d), BF16)],
        compiler_params=_params("parallel", "arbitrary"),
    )(ds_bf, g_out, z3, z3, fwb3, fwb3)


def _conv_a_fwd(gr, cwb, *, cb, name):
    t, r2 = gr.shape
    r = r2 // 2
    nb = r // cb

    def body(x_ref, w_ref, o_ref):
        o_ref[...] = _conv_taps(x_ref[...], w_ref[...], 4)

    return pl.pallas_call(
        body, name=name, grid=(nb,),
        in_specs=[pl.BlockSpec((t, cb), lambda j: (0, j + nb)), pl.BlockSpec((8, cb), lambda j: (0, j))],
        out_specs=pl.BlockSpec((t, cb), lambda j: (0, j)),
        out_shape=jax.ShapeDtypeStruct((t, r), F32),
        compiler_params=_params("parallel"),
    )(gr, cwb)


def _gates_fwd(rec, bd_r, bd_i, vecs, *, tm, name):
    t, r_dim = rec.shape
    nb, cb, _ = bd_r.shape

    def body(x_ref, wr_ref, wi_ref, v_ref, a_ref, u_ref, r_ref, i_ref):
        x = x_ref[...]
        xb = x.astype(BF16)
        v = v_ref[...]
        r = _sigmoid(_dot_nn(xb, wr_ref[...]) + v[0:1, :])
        i = _sigmoid(_dot_nn(xb, wi_ref[...]) + v[1:2, :])
        log_a = (-LRU_C) * r * _softplus(-v[2:3, :])
        a_ref[...] = jnp.exp(log_a)
        u_ref[...] = jnp.sqrt(-_expm1(2.0 * log_a)) * (i * x)
        r_ref[...] = r
        i_ref[...] = i

    blk = pl.BlockSpec((tm, cb), lambda j, i: (i, j))
    wspec = pl.BlockSpec((None, cb, cb), lambda j, i: (j, 0, 0))
    out = jax.ShapeDtypeStruct((t, r_dim), F32)
    return pl.pallas_call(
        body, name=name, grid=(nb, t // tm),
        in_specs=[blk, wspec, wspec, pl.BlockSpec((8, cb), lambda j, i: (0, j))],
        out_specs=[blk, blk, blk, blk],
        out_shape=[out, out, out, out],
        compiler_params=_params("parallel", "parallel"),
    )(rec, bd_r, bd_i, vecs)


def _scan_fwd(a, u, gr, *, cb, name):
    t, r = a.shape
    nb = r // cb
    seg = t // (SCAN_GROUPS * SUBLANES)

    def body(a_ref, u_ref, g_ref, h_ref, y_ref, p_ref):
        def step(k, carry):
            out = []
            for g in range(SCAN_GROUPS):
                h, p = carry[g]
                rows = pl.ds(k + g * SUBLANES * seg, SUBLANES, stride=seg)
                av = a_ref[rows, :]
                h = av * h + u_ref[rows, :]
                p = av * p
                h_ref[rows, :] = h
                p_ref[rows, :] = p
                out.append((h, p))
            return tuple(out)

        init = tuple((jnp.zeros((SUBLANES, cb), F32), jnp.ones((SUBLANES, cb), F32)) for _ in range(SCAN_GROUPS))
        fin = lax.fori_loop(0, seg, step, init, unroll=8)
        carry = fin[0][0][0:1, :]
        for s in range(1, SCAN_GROUPS * SUBLANES):
            h_fin, p_fin = fin[s // SUBLANES]
            e = s % SUBLANES
            rows = slice(s * seg, (s + 1) * seg)
            h_ref[rows, :] = h_ref[rows, :] + p_ref[rows, :] * carry
            carry = h_fin[e:e + 1, :] + p_fin[e:e + 1, :] * carry
        y_ref[...] = (_gelu(g_ref[...]) * h_ref[...]).astype(BF16)

    blk = pl.BlockSpec((t, cb), lambda j: (0, j))
    return pl.pallas_call(
        body, name=name, grid=(nb,),
        in_specs=[blk, blk, blk],
        out_specs=[blk, pl.BlockSpec((None, t, cb), lambda j: (0, 0, j))],
        out_shape=[jax.ShapeDtypeStruct((t, r), F32), jax.ShapeDtypeStruct((1, t, r), BF16)],
        scratch_shapes=[pltpu.VMEM((t, cb), F32)],
        compiler_params=_params("parallel"),
    )(a, u, gr)


def _scan_bwd(dy, gr, hr, a, *, cb, name):
    t, r = a.shape
    nb = r // cb
    n_seg = SCAN_GROUPS * SUBLANES
    seg = t // n_seg

    def body(dy_ref, g_ref, h_ref, a_ref, dh_ref, da_ref, dg_ref, q_ref):
        gl, dgl = _gelu_and_grad(g_ref[...])
        dyv = dy_ref[...]
        dh_ref[...] = dyv * gl
        dg_ref[...] = (dyv * h_ref[...] * dgl).astype(BF16)

        def step(k, carry):
            out = []
            for g in range(SCAN_GROUPS):
                cin, q = carry[g]
                rows = pl.ds(seg - 1 - k + g * SUBLANES * seg, SUBLANES, stride=seg)
                dh = dh_ref[rows, :] + cin
                dh_ref[rows, :] = dh
                q_ref[rows, :] = q
                av = a_ref[rows, :]
                out.append((av * dh, av * q))
            return tuple(out)

        init = tuple((jnp.zeros((SUBLANES, cb), F32), jnp.ones((SUBLANES, cb), F32)) for _ in range(SCAN_GROUPS))
        fin = lax.fori_loop(0, seg, step, init, unroll=8)
        last = n_seg - 1
        carry = fin[last // SUBLANES][0][last % SUBLANES:last % SUBLANES + 1, :]
        for s in range(n_seg - 2, -1, -1):
            c_fin, q_fin = fin[s // SUBLANES]
            e = s % SUBLANES
            rows = slice(s * seg, (s + 1) * seg)
            dh_ref[rows, :] = dh_ref[rows, :] + q_ref[rows, :] * carry
            carry = c_fin[e:e + 1, :] + q_fin[e:e + 1, :] * carry
        da_ref[...] = dh_ref[...] * _shift_down(h_ref[...], 1)

    blk = pl.BlockSpec((t, cb), lambda j: (0, j))
    return pl.pallas_call(
        body, name=name, grid=(nb,),
        in_specs=[blk, blk, blk, blk],
        out_specs=[blk, blk, blk],
        out_shape=[jax.ShapeDtypeStruct((t, r), F32), jax.ShapeDtypeStruct((t, r), F32),
                   jax.ShapeDtypeStruct((t, r), BF16)],
        scratch_shapes=[pltpu.VMEM((t, cb), F32)],
        compiler_params=_params("parallel"),
    )(dy, gr, hr, a)


def _gates_bwd(rec, r, i, a, dh, da, bd_r, bd_i, vecs, *, tm, name):
    t, r_dim = rec.shape
    nb, cb, _ = bd_r.shape

    def body(x_ref, r_ref, i_ref, a_ref, dh_ref, da_ref, wr_ref, wi_ref, v_ref, dx_ref, dpr_ref, dpi_ref, dv_ref):
        step = pl.program_id(1)
        x, r, i, a, dh, da = x_ref[...], r_ref[...], i_ref[...], a_ref[...], dh_ref[...], da_ref[...]
        lam = v_ref[...][2:3, :]
        sp = _softplus(-lam)
        a2 = a * a
        mult = jnp.sqrt(-_expm1(2.0 * (-LRU_C) * r * sp))
        d_i = dh * mult * x
        d_log_a = da * a - (dh * i * x) * a2 / mult
        d_r = d_log_a * ((-LRU_C) * sp)
        d_sp = jnp.sum(d_log_a * ((-LRU_C) * r), axis=0, keepdims=True)
        d_pre_r = d_r * r * (1.0 - r)
        d_pre_i = d_i * i * (1.0 - i)
        dprb = d_pre_r.astype(BF16)
        dpib = d_pre_i.astype(BF16)
        dx_ref[...] = dh * mult * i + _dot_nt(dprb, wr_ref[...]) + _dot_nt(dpib, wi_ref[...])
        dpr_ref[...] = dprb
        dpi_ref[...] = dpib
        upd = _rows8([jnp.sum(d_pre_r, axis=0, keepdims=True), jnp.sum(d_pre_i, axis=0, keepdims=True),
                      -d_sp * _sigmoid(-lam)], cb)

        @pl.when(step == 0)
        def _():
            dv_ref[...] = upd

        @pl.when(step > 0)
        def _():
            dv_ref[...] += upd

    blk = pl.BlockSpec((tm, cb), lambda j, i: (i, j))
    wspec = pl.BlockSpec((None, cb, cb), lambda j, i: (j, 0, 0))
    vspec = pl.BlockSpec((8, cb), lambda j, i: (0, j))
    return pl.pallas_call(
        body, name=name, grid=(nb, t // tm),
        in_specs=[blk] * 6 + [wspec, wspec, vspec],
        out_specs=[blk, blk, blk, vspec],
        out_shape=[jax.ShapeDtypeStruct((t, r_dim), F32), jax.ShapeDtypeStruct((t, r_dim), BF16),
                   jax.ShapeDtypeStruct((t, r_dim), BF16), jax.ShapeDtypeStruct((8, r_dim), F32)],
        compiler_params=_params("parallel", "arbitrary"),
    )(rec, r, i, a, dh, da, bd_r, bd_i, vecs)


def _bd_grad(rec, dpr, dpi, *, cb, name):
    t, r = rec.shape
    nb = r // cb

    def body(x_ref, dr_ref, di_ref, gr_ref, gi_ref):
        xb = x_ref[...].astype(BF16)
        gr_ref[...] = _dot_tn(xb, dr_ref[...])
        gi_ref[...] = _dot_tn(xb, di_ref[...])

    blk = pl.BlockSpec((t, cb), lambda j: (0, j))
    wspec = pl.BlockSpec((None, cb, cb), lambda j: (j, 0, 0))
    out = jax.ShapeDtypeStruct((nb, cb, cb), F32)
    return pl.pallas_call(
        body, name=name, grid=(nb,),
        in_specs=[blk, blk, blk], out_specs=[wspec, wspec], out_shape=[out, out],
        compiler_params=_params("parallel"),
    )(rec, dpr, dpi)


def _conv_a_bwd(d_rec, gr, dgate, cwb, *, cb, name):
    t, r = d_rec.shape
    nb = r // cb

    def body(dy_ref, x_ref, dg_ref, w_ref, dact_ref, dw_ref):
        dx, dw = _conv_taps_bwd(dy_ref[...], x_ref[...], w_ref[...], 4)
        dact_ref[0] = dg_ref[...]
        dact_ref[1] = dx.astype(BF16)
        dw_ref[...] = dw

    blk = pl.BlockSpec((t, cb), lambda j: (0, j))
    vspec = pl.BlockSpec((8, cb), lambda j: (0, j))
    return pl.pallas_call(
        body, name=name, grid=(nb,),
        in_specs=[blk, pl.BlockSpec((t, cb), lambda j: (0, j + nb)), blk, vspec],
        out_specs=[pl.BlockSpec((2, t, cb), lambda j: (0, 0, j)), vspec],
        out_shape=[jax.ShapeDtypeStruct((2, t, r), BF16), jax.ShapeDtypeStruct((8, r), F32)],
        compiler_params=_params("parallel"),
    )(d_rec, gr, dgate, cwb)


def _split3(x):
    p0 = x.astype(BF16)
    r1 = x - p0.astype(F32)
    p1 = r1.astype(BF16)
    p2 = (r1 - p1.astype(F32)).astype(BF16)
    return p0, p1, p2


def _fgate_fwd(fp, fb, *, tq, name):
    t = fp.shape[0]

    def body(f_ref, b_ref, c_ref, ct_ref):
        logf = -_softplus(-(f_ref[...] + b_ref[...]))
        rows = pl.program_id(0) * tq + lax.broadcasted_iota(jnp.int32, (tq, t), 0)
        cols = lax.broadcasted_iota(jnp.int32, (tq, t), 1)
        tri = (cols <= rows).astype(BF16)
        p0, p1, p2 = _split3(logf)
        c = _dot_nn(tri, p0) + _dot_nn(tri, p1) + _dot_nn(tri, p2)
        c_ref[...] = c
        ct_ref[...] = c.T

    return pl.pallas_call(
        body, name=name, grid=(t // tq,),
        in_specs=[pl.BlockSpec((t, LANES), lambda i: (0, 0)), pl.BlockSpec((1, LANES), lambda i: (0, 0))],
        out_specs=[pl.BlockSpec((tq, LANES), lambda i: (i, 0)), pl.BlockSpec((LANES, tq), lambda i: (0, i))],
        out_shape=[jax.ShapeDtypeStruct((t, LANES), F32), jax.ShapeDtypeStruct((LANES, t), F32)],
        compiler_params=_params("parallel"),
    )(fp, fb)


def _fgate_bwd(dct, fp, fb, *, tq, name):
    t = fp.shape[0]

    def body(d_ref, f_ref, b_ref, o_ref, db_ref):
        i = pl.program_id(0)
        rows = lax.broadcasted_iota(jnp.int32, (t, tq), 0)
        cols = i * tq + lax.broadcasted_iota(jnp.int32, (t, tq), 1)
        tri = (rows >= cols).astype(BF16)
        p0, p1, p2 = _split3(d_ref[...])
        dlogf = (_dot_nn(p0, tri) + _dot_nn(p1, tri) + _dot_nn(p2, tri)).T
        df = dlogf * _sigmoid(-(f_ref[...] + b_ref[...]))
        o_ref[...] = df.astype(BF16)
        upd = _rows8([jnp.sum(df, axis=0, keepdims=True)], LANES)

        @pl.when(i == 0)
        def _():
            db_ref[...] = upd

        @pl.when(i > 0)
        def _():
            db_ref[...] += upd

    return pl.pallas_call(
        body, name=name, grid=(t // tq,),
        in_specs=[pl.BlockSpec((LANES, t), lambda i: (0, 0)), pl.BlockSpec((tq, LANES), lambda i: (i, 0)),
                  pl.BlockSpec((1, LANES), lambda i: (0, 0))],
        out_specs=[pl.BlockSpec((tq, LANES), lambda i: (i, 0)), pl.BlockSpec((8, LANES), lambda i: (0, 0))],
        out_shape=[jax.ShapeDtypeStruct((t, LANES), BF16), jax.ShapeDtypeStruct((8, LANES), F32)],
        compiler_params=_params("arbitrary"),
    )(dct, fp, fb)


def _pair_sum(a, b, *, tm, name):
    t, d = a.shape

    def body(a_ref, b_ref, o_ref):
        o_ref[...] = (a_ref[...] + b_ref[...]).astype(BF16)

    row = pl.BlockSpec((tm, d), lambda i: (i, 0))
    return pl.pallas_call(
        body, name=name, grid=(t // tm,), in_specs=[row, row], out_specs=row,
        out_shape=jax.ShapeDtypeStruct((t, d), BF16), compiler_params=_params("parallel"),
    )(a, b)


FWD_HEAD_TILES = 2
BWD_HEAD_TILES = 1


def _head_block_width(dh, tiles):
    return tiles * LANES if tiles * LANES // dh <= 8 else LANES


def _head_masks(dh, bw):
    lane = lax.broadcasted_iota(jnp.int32, (1, bw), 1)
    return [((lane >= e * dh) & (lane < (e + 1) * dh)) for e in range(bw // dh)]


def _head_c_row(ct_blk, head):
    sub = lax.broadcasted_iota(jnp.int32, ct_blk.shape, 0)
    return jnp.sum(jnp.where(sub == head, ct_blk, 0.0), axis=0, keepdims=True)


def _attn_weights(qm, k, c_row, q0):
    tq, t = qm.shape[0], k.shape[0]
    s = _dot_nt(qm, k) - c_row
    qi = q0 + lax.broadcasted_iota(jnp.int32, (tq, t), 0)
    ki = lax.broadcasted_iota(jnp.int32, (tq, t), 1)
    s = jnp.where(ki <= qi, s, -jnp.inf)
    m = jnp.max(s, axis=-1, keepdims=True)
    e = jnp.exp(s - m)
    return e, m, 1.0 / jnp.sum(e, axis=-1, keepdims=True)


def _key_buckets(t, tq):
    return tuple(sorted({min(-(-(i * tq) // LANES) * LANES, t) for i in range(1, t // tq + 1)}))


def _for_prefix(needed, buckets, fn):
    prev = 0
    for length in buckets:
        pl.when((needed > prev) & (needed <= length))(lambda length=length: fn(length))
        prev = length


def _attn_fwd(qg, kv, ct, *, tq, name):
    t, d2 = qg.shape
    d = d2 // 2
    dh = d // N_HEADS
    bw = _head_block_width(dh, FWD_HEAD_TILES)
    hpb = bw // dh
    nhb = d // bw
    scale = dh ** -0.5
    buckets = _key_buckets(t, tq)

    def body(q_ref, og_ref, k_ref, v_ref, ct_ref, o_ref, y_ref, st_ref):
        hb = pl.program_id(0)
        q0 = pl.program_id(1) * tq

        def run(length):
            qs = q_ref[...] * scale
            k = k_ref[0:length, :]
            v = v_ref[0:length, :]
            o = jnp.zeros((tq, bw), F32)
            lane = lax.broadcasted_iota(jnp.int32, (tq, LANES), 1)
            stats = jnp.zeros((tq, LANES), F32)
            for e, msk in enumerate(_head_masks(dh, bw)):
                c_row = _head_c_row(ct_ref[:, 0:length], hb * hpb + e)
                w, m, inv = _attn_weights(jnp.where(msk, qs, 0.0).astype(BF16), k, c_row, q0)
                o = o + _dot_nn(w.astype(BF16), jnp.where(msk, v, jnp.zeros_like(v))) * inv
                stats = jnp.where(lane == e, m, jnp.where(lane == hpb + e, inv, stats))
            o_ref[...] = o
            y_ref[...] = (o * _sigmoid(og_ref[...])).astype(BF16)
            st_ref[...] = stats

        _for_prefix(q0 + tq, buckets, run)

    qblk = pl.BlockSpec((tq, bw), lambda h, i: (i, h))
    return pl.pallas_call(
        body, name=name, grid=(nhb, t // tq),
        in_specs=[qblk, pl.BlockSpec((tq, bw), lambda h, i: (i, h + nhb)),
                  pl.BlockSpec((t, bw), lambda h, i: (0, h)), pl.BlockSpec((t, bw), lambda h, i: (0, h + nhb)),
                  pl.BlockSpec((N_HEADS, t), lambda h, i: (0, 0))],
        out_specs=[qblk, pl.BlockSpec((None, tq, bw), lambda h, i: (0, i, h)),
                   pl.BlockSpec((None, tq, LANES), lambda h, i: (h, i, 0))],
        out_shape=[jax.ShapeDtypeStruct((t, d), F32), jax.ShapeDtypeStruct((1, t, d), BF16),
                   jax.ShapeDtypeStruct((nhb, t, LANES), F32)],
        compiler_params=_params("parallel", "parallel"),
    )(qg, qg, kv, kv, ct)


def _attn_bwd(dy, qg, o, stats, kv, ct, *, tq, name):
    t, d2 = qg.shape
    d = d2 // 2
    dh = d // N_HEADS
    bw = _head_block_width(dh, BWD_HEAD_TILES)
    hpb = bw // dh
    nhb = d // bw
    scale = dh ** -0.5
    n_q = t // tq
    hpb_f = _head_block_width(dh, FWD_HEAD_TILES) // dh
    ratio = hpb_f // hpb
    chunk = 8 * LANES

    def body(dy_ref, q_ref, og_ref, o_ref, st_ref, k_ref, v_ref, ct_ref, dqg_ref, dk_ref, dv_ref, dc_ref, dcq_ref):
        hb = pl.program_id(0)
        step = pl.program_id(1)

        @pl.when(step == 0)
        def _():
            dk_ref[...] = jnp.zeros((t, bw), F32)
            dv_ref[...] = jnp.zeros((t, bw), F32)
            dc_ref[...] = jnp.zeros((8, t), F32)

        def run(i):
            q0 = i * tq
            length = min(-(-(q0 + tq) // LANES) * LANES, t)
            qs = q_ref[...] * scale
            sg = _sigmoid(og_ref[...])
            dyv = dy_ref[...]
            ov = o_ref[...]
            do = dyv * sg
            dqg_ref[1] = (dyv * ov * sg * (1.0 - sg)).astype(BF16)
            lane = lax.broadcasted_iota(jnp.int32, (tq, LANES), 1)
            stats = st_ref[...]
            masks = _head_masks(dh, bw)
            heads = []
            for e, msk in enumerate(masks):
                pos = (hb % ratio) * hpb + e
                m = jnp.sum(jnp.where(lane == pos, stats, 0.0), axis=1, keepdims=True)
                inv = jnp.sum(jnp.where(lane == hpb_f + pos, stats, 0.0), axis=1, keepdims=True)
                delta = jnp.sum(jnp.where(msk, do * ov, 0.0), axis=1, keepdims=True)
                heads.append((msk, m, inv, delta, jnp.where(msk, qs, 0.0).astype(BF16),
                              jnp.where(msk, do, 0.0).astype(BF16)))
            dq = jnp.zeros((tq, bw), F32)
            dcq = jnp.zeros((tq, LANES), F32)
            row_acc = [jnp.zeros((tq, chunk), F32) for _ in heads]
            for c0 in range(0, length, chunk):
                ch = min(chunk, length - c0)
                k = k_ref[c0:c0 + ch, :]
                v = v_ref[c0:c0 + ch, :]
                dk = jnp.zeros((ch, bw), F32)
                dv = jnp.zeros((ch, bw), F32)
                dc_rows = []
                for e, (msk, m, inv, delta, qm, dom) in enumerate(heads):
                    c_row = _head_c_row(ct_ref[:, c0:c0 + ch], hb * hpb + e)
                    s = _dot_nt(qm, k) - c_row
                    if c0 + ch - 1 > q0:
                        qi = q0 + lax.broadcasted_iota(jnp.int32, (tq, ch), 0)
                        ki = c0 + lax.broadcasted_iota(jnp.int32, (tq, ch), 1)
                        s = jnp.where(ki <= qi, s, -jnp.inf)
                    p = jnp.exp(s - m) * inv
                    dsc = p * (_dot_nt(dom, v) - delta)
                    dsb = dsc.astype(BF16)
                    dq = dq + _dot_nn(dsb, jnp.where(msk, k, jnp.zeros_like(k)))
                    dk = dk + _dot_tn(dsb, qm)
                    dv = dv + _dot_tn(p.astype(BF16), dom)
                    dc_rows.append(-jnp.sum(dsc, axis=0, keepdims=True))
                    if ch == chunk:
                        row_acc[e] = row_acc[e] + dsc
                    else:
                        dcq = dcq + jnp.where(lane == e, jnp.sum(dsc, axis=1, keepdims=True), 0.0)
                dk_ref[c0:c0 + ch, :] += dk
                dv_ref[c0:c0 + ch, :] += dv
                dc_ref[:, c0:c0 + ch] += _rows8(dc_rows, ch)
            dqg_ref[0] = (dq * scale).astype(BF16)
            for e in range(len(heads)):
                dcq = dcq + jnp.where(lane == e, jnp.sum(row_acc[e], axis=1, keepdims=True), 0.0)
            dcq_ref[...] = dcq

        for i in range(n_q):
            pl.when(step == i)(lambda i=i: run(i))

    qblk = pl.BlockSpec((tq, bw), lambda h, i: (i, h))
    kblk = pl.BlockSpec((t, bw), lambda h, i: (0, h))
    return pl.pallas_call(
        body, name=name, grid=(nhb, n_q),
        in_specs=[qblk, qblk, pl.BlockSpec((tq, bw), lambda h, i: (i, h + nhb)), qblk,
                  pl.BlockSpec((None, tq, LANES), lambda h, i: (h // ratio, i, 0)),
                  kblk, pl.BlockSpec((t, bw), lambda h, i: (0, h + nhb)),
                  pl.BlockSpec((N_HEADS, t), lambda h, i: (0, 0))],
        out_specs=[pl.BlockSpec((2, tq, bw), lambda h, i: (0, i, h)), kblk, kblk,
                   pl.BlockSpec((None, 8, t), lambda h, i: (h, 0, 0)),
                   pl.BlockSpec((None, tq, LANES), lambda h, i: (h, i, 0))],
        out_shape=[jax.ShapeDtypeStruct((2, t, d), BF16), jax.ShapeDtypeStruct((t, d), F32),
                   jax.ShapeDtypeStruct((t, d), F32), jax.ShapeDtypeStruct((nhb, 8, t), F32),
                   jax.ShapeDtypeStruct((nhb, t, LANES), F32)],
        compiler_params=_params("parallel", "arbitrary"),
    )(dy, qg, qg, o, stats, kv, kv, ct)


def _loss_bwd(h, tgt, *, lo, hi, tm, name):
    t, d = h.shape

    def body(h_ref, t_ref, l_ref, dy_ref):
        i = pl.program_id(0)
        rows = i * tm + lax.broadcasted_iota(jnp.int32, (tm, d), 0)
        err = jnp.where((rows >= lo) & (rows < hi), h_ref[...] - t_ref[...], 0.0)
        dy_ref[...] = err * (1.0 / d)
        part = jnp.sum(jnp.sum(err * err, axis=0, keepdims=True), axis=1, keepdims=True) * (0.5 / d)
        upd = jnp.broadcast_to(part, (8, LANES))

        @pl.when(i == 0)
        def _():
            l_ref[...] = upd

        @pl.when(i > 0)
        def _():
            l_ref[...] += upd

    row = pl.BlockSpec((tm, d), lambda i: (i, 0))
    return pl.pallas_call(
        body, name=name, grid=(t // tm,),
        in_specs=[row, row],
        out_specs=[pl.BlockSpec((8, LANES), lambda i: (0, 0)), row],
        out_shape=[jax.ShapeDtypeStruct((8, LANES), F32), jax.ShapeDtypeStruct((t, d), F32)],
        compiler_params=_params("arbitrary"),
    )(h, tgt)


def _adamw_math(w, gv, m, v):
    bc1 = 1.0 / (1.0 - ADAM_B1 ** ADAM_STEP)
    bc2 = 1.0 / (1.0 - ADAM_B2 ** ADAM_STEP)
    nm = ADAM_B1 * m + (1.0 - ADAM_B1) * gv
    nv = ADAM_B2 * v + (1.0 - ADAM_B2) * (gv * gv)
    delta = (-ADAM_LR) * ((nm * bc1) / (jnp.sqrt(nv * bc2) + ADAM_EPS) + ADAM_WD * w)
    return delta, nm, nv


def _adamw(w, g, m, v, *, name):
    r, c = w.shape
    tr = r
    for cand in (512, 256, 128, 64, 32, 16, 8):
        if r % cand == 0 and r > cand:
            tr = cand
            break

    def body(w_ref, g_ref, m_ref, v_ref, d_ref, nm_ref, nv_ref):
        d_ref[...], nm_ref[...], nv_ref[...] = _adamw_math(w_ref[...], g_ref[...], m_ref[...], v_ref[...])

    blk = pl.BlockSpec((tr, c), lambda i: (i, 0))
    out = jax.ShapeDtypeStruct((r, c), F32)
    return pl.pallas_call(
        body, name=name, grid=(r // tr,),
        in_specs=[blk] * 4, out_specs=[blk] * 3, out_shape=[out] * 3,
        compiler_params=_params("parallel"),
    )(w, g, m, v)


def _sum_adamw(recvs, sends, me, w, m, v, *, name):
    n_l = len(recvs)
    _, r, c = recvs[0].shape
    tr = _tile(r, (256, 192, 176, 128, 96, 64, 48, 32, 16))

    def body(me_ref, *refs):
        p_refs, own_refs = refs[:n_l], refs[n_l:2 * n_l]
        w_ref, m_ref, v_ref, g_ref, d_ref, nm_ref, nv_ref, acc_ref = refs[2 * n_l:]
        layer = pl.program_id(0)
        mine = me_ref[0]
        for k in range(n_l):
            @pl.when(layer == k)
            def _(k=k):
                acc_ref[...] = jnp.zeros((tr, c), F32)
                for dev in range(N_DEV):
                    @pl.when(mine == dev)
                    def _():
                        acc_ref[...] += own_refs[k][...].astype(F32)

                    @pl.when(mine != dev)
                    def _(dev=dev):
                        acc_ref[...] += p_refs[k][dev].astype(F32)
                acc = acc_ref[...]
                g_ref[...] = acc
                d_ref[...], nm_ref[...], nv_ref[...] = _adamw_math(w_ref[...], acc, m_ref[...], v_ref[...])

    p_specs = [pl.BlockSpec((N_DEV, tr, c), lambda l, i, me_ref, k=k: (0, jnp.where(l == k, i, 0), 0))
               for k in range(n_l)]
    own_specs = [pl.BlockSpec((None, tr, c), lambda l, i, me_ref, k=k: (me_ref[0], jnp.where(l == k, i, 0), 0))
                 for k in range(n_l)]
    blk = pl.BlockSpec((None, tr, c), lambda l, i, me_ref: (l, i, 0))
    out = jax.ShapeDtypeStruct((n_l, r, c), F32)
    return pl.pallas_call(
        body, name=name,
        grid_spec=pltpu.PrefetchScalarGridSpec(
            num_scalar_prefetch=1, grid=(n_l, r // tr),
            in_specs=p_specs + own_specs + [blk] * 3, out_specs=[blk] * 4,
            scratch_shapes=[pltpu.VMEM((tr, c), F32)]),
        out_shape=[out] * 4,
        compiler_params=_params("arbitrary", "arbitrary"),
    )(me, *recvs, *sends, w, m, v)


def _sum8(parts, *, name):
    _, r, c = parts.shape
    tr = r
    for cand in (512, 256, 128, 64, 32, 16):
        if r % cand == 0 and r > cand:
            tr = cand
            break

    def body(p_ref, o_ref):
        acc = p_ref[0].astype(F32)
        for k in range(1, N_DEV):
            acc = acc + p_ref[k].astype(F32)
        o_ref[...] = acc

    return pl.pallas_call(
        body, name=name, grid=(r // tr,),
        in_specs=[pl.BlockSpec((N_DEV, tr, c), lambda i: (0, i, 0))],
        out_specs=pl.BlockSpec((tr, c), lambda i: (i, 0)),
        out_shape=jax.ShapeDtypeStruct((r, c), F32),
        compiler_params=_params("parallel"),
    )(parts)


def _my_index():
    return 4 * lax.axis_index("x") + 2 * lax.axis_index("y") + lax.axis_index("c")


def _peer(k):
    x, y, c = lax.axis_index("x"), lax.axis_index("y"), lax.axis_index("c")
    px = x ^ ((k >> 2) & 1)
    py = y ^ ((k >> 1) & 1)
    pc = c ^ (k & 1)
    return (px, py, pc), 4 * px + 2 * py + pc


def _all_gather(shards, *, name):
    n_arr = len(shards)

    def body(*refs):
        ins, outs = refs[:n_arr], refs[n_arr:2 * n_arr]
        send_sems, recv_sems, local_sems = refs[2 * n_arr:]
        me = _my_index()
        local = [pltpu.make_async_copy(ins[n], outs[n].at[me], local_sems.at[n]) for n in range(n_arr)]
        for cp in local:
            cp.start()
        sends = []
        for k in range(1, N_DEV):
            peer, _ = _peer(k)
            for n in range(n_arr):
                cp = pltpu.make_async_remote_copy(
                    src_ref=ins[n], dst_ref=outs[n].at[me], send_sem=send_sems.at[n, k - 1],
                    recv_sem=recv_sems.at[n, k - 1], device_id=peer, device_id_type=pl.DeviceIdType.MESH)
                cp.start()
                sends.append(cp)
        for k in range(1, N_DEV):
            peer, pidx = _peer(k)
            for n in range(n_arr):
                pltpu.make_async_remote_copy(
                    src_ref=ins[n], dst_ref=outs[n].at[pidx], send_sem=send_sems.at[n, k - 1],
                    recv_sem=recv_sems.at[n, k - 1], device_id=peer, device_id_type=pl.DeviceIdType.MESH).wait_recv()
        for cp in sends:
            cp.wait_send()
        for cp in local:
            cp.wait()

    hbm = pl.BlockSpec(memory_space=pl.ANY)
    return pl.pallas_call(
        body, name=name,
        in_specs=[hbm] * n_arr, out_specs=[hbm] * n_arr,
        out_shape=[jax.ShapeDtypeStruct((N_DEV,) + s.shape, s.dtype) for s in shards],
        scratch_shapes=[pltpu.SemaphoreType.DMA((n_arr, N_DEV - 1)), pltpu.SemaphoreType.DMA((n_arr, N_DEV - 1)),
                        pltpu.SemaphoreType.DMA((n_arr,))],
        compiler_params=pltpu.CompilerParams(has_side_effects=True),
    )(*shards)


_HBM = pl.BlockSpec(memory_space=pltpu.HBM)
_SEM = pl.BlockSpec(memory_space=pltpu.SEMAPHORE)
_EFFECT = pltpu.SideEffectType.DATAFLOW_SIDE_EFFECTING


def _remote(src, dst, send_sem, recv_sem, peer):
    return pltpu.make_async_remote_copy(src_ref=src, dst_ref=dst, send_sem=send_sem, recv_sem=recv_sem,
                                        device_id=peer, device_id_type=pl.DeviceIdType.MESH)


def _place_own(src, layer, me, *, out_dtype, name):
    _, r, c = src.shape
    tr = _tile(r, (256, 192, 176, 128, 96, 64, 48, 32, 16))

    def body(me_ref, s_ref, o_ref):
        o_ref[...] = s_ref[...].astype(out_dtype)

    return pl.pallas_call(
        body, name=name,
        grid_spec=pltpu.PrefetchScalarGridSpec(
            num_scalar_prefetch=1, grid=(r // tr,),
            in_specs=[pl.BlockSpec((None, tr, c), lambda i, me_ref: (layer, i, 0))],
            out_specs=pl.BlockSpec((None, tr, c), lambda i, me_ref: (me_ref[0], i, 0))),
        out_shape=jax.ShapeDtypeStruct((N_DEV, r, c), out_dtype),
        compiler_params=_params("parallel"),
    )(me, src)


def _own_blocks(srcs, *, name):
    n = len(srcs)

    def body(*refs):
        ins, outs, sems = refs[:n], refs[n:2 * n], refs[2 * n]
        me = _my_index()
        cps = [pltpu.make_async_copy(ins[t].at[me], outs[t].at[me], sems.at[t]) for t in range(n)]
        for cp in cps:
            cp.start()
        for cp in cps:
            cp.wait()

    return pl.pallas_call(
        body, name=name, in_specs=[_HBM] * n, out_specs=[_HBM] * n,
        out_shape=[jax.ShapeDtypeStruct(s.shape, s.dtype) for s in srcs],
        scratch_shapes=[pltpu.SemaphoreType.DMA((n,))],
    )(*srcs)


def _split_start(groups, *, scatter, name):
    sizes = [len(srcs) for srcs, _ in groups]
    flat_src = [s for srcs, _ in groups for s in srcs]
    flat_land = [l for _, lands in groups for l in lands]
    n, n_g = len(flat_land), len(groups)
    if not scatter:
        flat_src = []
    n_in = len(flat_src) + n

    def body(*refs):
        lands = refs[n_in - n:n_in]
        ins = refs[:n] if scatter else lands
        sems = refs[n_in:n_in + 2 * n_g]
        token = refs[-1]
        me = _my_index()
        t = 0
        for g in range(n_g):
            for q in range(sizes[g]):
                for k in range(1, N_DEV):
                    peer, pidx = _peer(k)
                    src = ins[t].at[pidx] if scatter else ins[t].at[me]
                    slot = q * (N_DEV - 1) + k - 1
                    _remote(src, lands[t].at[me], sems[2 * g].at[slot], sems[2 * g + 1].at[slot], peer).start()
                t += 1
        token[...] = jnp.zeros_like(token)

    sem_shapes = []
    for sz in sizes:
        sem_shapes += [pltpu.SemaphoreType.DMA((sz * (N_DEV - 1),)), pltpu.SemaphoreType.DMA((sz * (N_DEV - 1),))]
    outs = pl.pallas_call(
        body, name=name,
        in_specs=[_HBM] * n_in,
        out_specs=[_SEM] * (2 * n_g) + [_HBM] * n_in + [pl.BlockSpec(memory_space=pltpu.VMEM)],
        out_shape=sem_shapes + [pltpu.HBM(a.shape, a.dtype) for a in flat_src + flat_land]
        + [jax.ShapeDtypeStruct((8, LANES), F32)],
        input_output_aliases={i: 2 * n_g + i for i in range(n_in)},
        compiler_params=pltpu.CompilerParams(has_side_effects=_EFFECT),
    )(*[pltpu.with_memory_space_constraint(a, pltpu.HBM) for a in flat_src + flat_land])
    sems, thru, token = outs[:2 * n_g], outs[2 * n_g:2 * n_g + n_in], outs[-1]
    handles, pos = [], 0
    for g, sz in enumerate(sizes):
        lands_g = thru[n_in - n + pos:n_in - n + pos + sz]
        handles.append((sems[2 * g], sems[2 * g + 1], thru[pos:pos + sz] if scatter else [], lands_g))
        pos += sz
    return handles, token


def _split_wait(handle, after, *, scatter, name):
    send_sems, recv_sems, srcs, lands = handle
    n, n_src = len(lands), len(srcs)

    def body(*refs):
        lnd = refs[n_src:n_src + n]
        ins = refs[:n_src] if scatter else lnd
        ssem, rsem = refs[n_src + n], refs[n_src + n + 1]
        me = _my_index()
        for t in range(n):
            for k in range(1, N_DEV):
                peer, pidx = _peer(k)
                block = ins[t].at[me]
                slot = t * (N_DEV - 1) + k - 1
                _remote(block, lnd[t].at[me], ssem.at[slot], rsem.at[slot], peer).wait_send()
                _remote(block, lnd[t].at[pidx], ssem.at[slot], rsem.at[slot], peer).wait_recv()

    return pl.pallas_call(
        body, name=name,
        in_specs=[_HBM] * (n_src + n) + [_SEM, _SEM, pl.BlockSpec(memory_space=pl.ANY)],
        out_specs=[_HBM] * n,
        out_shape=[pltpu.HBM(l.shape, l.dtype) for l in lands],
        input_output_aliases={n_src + t: t for t in range(n)},
        compiler_params=pltpu.CompilerParams(has_side_effects=_EFFECT),
    )(*srcs, *lands, send_sems, recv_sems, after)


def _pack(arrs, dtype, row_quantum=16):
    flat = jnp.concatenate([a.astype(dtype).reshape(-1) for a in arrs])
    pad = (-flat.shape[0]) % (row_quantum * PACK_COLS)
    if pad:
        flat = jnp.concatenate([flat, jnp.zeros((pad,), dtype)])
    return flat.reshape(-1, PACK_COLS)


def _pack8(arrs, dtype):
    flat = jnp.concatenate([a.astype(dtype).reshape(N_DEV, -1) for a in arrs], axis=1)
    pad = (-flat.shape[1]) % (16 * PACK_COLS)
    if pad:
        flat = jnp.concatenate([flat, jnp.zeros((N_DEV, pad), dtype)], axis=1)
    return flat.reshape(N_DEV, -1, PACK_COLS)


def _unpack(slab, shapes, lead):
    lead_shape = slab.shape[:lead]
    flat = slab.reshape(lead_shape + (-1,))
    outs, off = [], 0
    for shp in shapes:
        size = math.prod(shp)
        outs.append(flat[..., off:off + size].reshape(lead_shape + tuple(shp)))
        off += size
    return outs


def _cols_full(g):
    g = jnp.moveaxis(g, 0, -2)
    return g.reshape(g.shape[:-2] + (g.shape[-2] * g.shape[-1],))


def _cols_split(full):
    n = full.shape[-1] // N_DEV
    return jnp.moveaxis(full.reshape(full.shape[:-1] + (N_DEV, n)), -2, 0)


def _block_diag(w, per):
    n, b, _ = w.shape
    w4 = w.reshape(n // per, per, b, b)
    eye = jnp.eye(per, dtype=w.dtype)
    return jnp.einsum('gpab,pq->gpaqb', w4, eye).reshape(n // per, per * b, per * b)


def _block_diag_extract(g, per):
    gn, cb, _ = g.shape
    b = cb // per
    g5 = g.reshape(gn, per, b, per, b)
    return jnp.stack([g5[:, p, :, p, :] for p in range(per)], axis=1).reshape(gn * per, b, b)


def _slab2d(a):
    return a.reshape(-1, a.shape[-1])


def _lru_block_cols(r_dim):
    lru = r_dim // N_LRU_BLOCKS
    return lru * LANES // math.gcd(lru, LANES)


BIG = ("a_w_in", "a_w_out", "b_w_in", "b_w_out", "f_w_in", "f_w_out")
COL_F32 = ("meta", "a_conv_w", "a_conv_b", "a_b_r", "a_b_i", "a_lambda", "f_conv_w")
REPLICATED = ("a_w_r", "a_w_i", "kv_f_b", "f_conv_b", "ln1_g", "ln1_b", "ln2_g", "ln2_b")
WEIGHT_NAMES = ("meta", "a_w_in", "a_conv_w", "a_conv_b", "a_w_r", "a_b_r", "a_w_i", "a_b_i", "a_lambda", "a_w_out",
                "kv_w", "kv_f_b", "b_w_in", "b_w_out", "f_w_in", "f_conv_w", "f_conv_b", "f_w_out",
                "ln1_g", "ln1_b", "ln2_g", "ln2_b")


def _kv_layout(kv_gathered, d):
    kv_full = _cols_full(kv_gathered)
    kv_pad = 2 * d + LANES - kv_full.shape[1]
    return jnp.concatenate([kv_full, jnp.zeros((d, kv_pad), kv_full.dtype)], axis=1)


def _small_layouts(small):
    r_dim = small["a_lambda"].shape[1]
    n_f = small["f_conv_b"].shape[1] // N_DEV
    cb = _lru_block_cols(r_dim)
    per = cb // (r_dim // N_LRU_BLOCKS)
    n_a = small["a_lambda"].shape[0]
    f_conv_w3 = small["f_conv_w"].reshape(N_LAYERS, 3, N_DEV, n_f).transpose(0, 2, 1, 3)
    f_conv_b3 = small["f_conv_b"].reshape(N_LAYERS, N_DEV, 1, n_f)
    return {
        "kv_fb": jnp.concatenate([small["kv_f_b"], jnp.zeros((LANES - N_HEADS,), F32)])[None],
        "a_cwb": jnp.concatenate([small["a_conv_w"], small["a_conv_b"][:, None],
                                  jnp.zeros((n_a, 3, r_dim), F32)], axis=1),
        "a_vecs": jnp.concatenate([jnp.stack([small["a_b_r"], small["a_b_i"], small["a_lambda"]], axis=1),
                                   jnp.zeros((n_a, 5, r_dim), F32)], axis=1),
        "a_bd_r": jnp.stack([_block_diag(small["a_w_r"][l], per) for l in range(n_a)]).astype(BF16),
        "a_bd_i": jnp.stack([_block_diag(small["a_w_i"][l], per) for l in range(n_a)]).astype(BF16),
        "f_cwb3": jnp.concatenate([f_conv_w3, f_conv_b3, jnp.zeros((N_LAYERS, N_DEV, 4, n_f), F32)], axis=2),
        "ln1_g": small["ln1_g"][:, None], "ln1_b": small["ln1_b"][:, None],
        "ln2_g": small["ln2_g"][:, None], "ln2_b": small["ln2_b"][:, None],
    }


def _local_step(h0, tgt, n_meta, n_tok, wts, hooks):
    tp, d = h0.shape
    tm = tp // 8 if (tp // 8) % 16 == 0 else tp
    tmb = _tile(tp, (1088, 512, 320, 256, 128))
    tq = 128
    tqa_fwd = tp // 4 if tp % 64 == 0 else tq
    tqa_bwd = tp // 4 if tp % 64 == 0 else tq
    r_dim = wts["a_vecs"].shape[2]
    cb = wts["a_bd_r"].shape[-1]
    sb = LANES
    n_b = N_LAYERS - N_A_LAYERS

    h, h_bf = h0, h0.astype(BF16)
    saved = []
    kvs = None
    for layer in range(N_LAYERS):
        lw = {}
        sv = {"h_bf": h_bf, "w": lw}
        if layer < N_A_LAYERS:
            lw["in"] = hooks.weight(layer, "in", h)
            sv["gr"] = _proj_in(h_bf, lw["in"], shard_major=False, name="a_in_proj")
            sv["rec"] = _conv_a_fwd(sv["gr"], wts["a_cwb"][layer], cb=cb, name="a_conv_fwd")
            a, u, sv["r"], sv["i"] = _gates_fwd(sv["rec"], wts["a_bd_r"][layer], wts["a_bd_i"][layer],
                                                wts["a_vecs"][layer], tm=tmb // 2, name="a_gates_fwd")
            sv["a"] = a
            sv["hr"], y3 = _scan_fwd(a, u, sv["gr"], cb=sb, name="a_scan_fwd")
        else:
            j = layer - N_A_LAYERS
            if j == 0:
                kv_w = _kv_layout(hooks.weight(layer, "kv_w", h), d)
                kvs = {"h_bf": h_bf, "w": kv_w}
                kvs["kv"] = _mm_nn(h_bf, kv_w[:, :2 * d], tn=_tile(2 * d, (512, 256, 128)), out_dtype=BF16,
                                   name="kv_proj")
                kvs["fp"] = _mm_nn(h_bf, kv_w[:, 2 * d:], tn=LANES, out_dtype=F32, name="f_proj")
                kvs["c"], ct = _fgate_fwd(kvs["fp"], wts["kv_fb"], tq=tq, name="fgate_fwd")
                kvs["ct"] = ct[:N_HEADS]
            lw["in"] = hooks.weight(layer, "in", kvs["c"] if j == 0 else h)
            sv["qg"] = _proj_in(h_bf, lw["in"], shard_major=False, name="b_in_proj")
            sv["o"], y3, sv["st"] = _attn_fwd(sv["qg"], kvs["kv"], kvs["ct"], tq=tqa_fwd, name="attn_fwd")
        sv["y3"] = y3
        lw["out"] = hooks.weight(layer, "out", y3)
        sv["s1"], h, h_bf = _out_ln(y3, lw["out"], h, wts["ln1_g"][layer], wts["ln1_b"][layer], n_valid=n_tok,
                                    tm=tmb // 2, name="mix_out_ln")
        sv["h1_bf"] = h_bf
        lw["f_in"] = hooks.weight(layer, "f_in", h)
        sv["z3"] = _proj_in(h_bf, lw["f_in"], shard_major=True, transposed=True, name="f_in_proj")
        sv["yf3"] = _convglu_fwd(sv["z3"], wts["f_cwb3"][layer], name="f_convglu_fwd")
        lw["f_out"] = hooks.weight(layer, "f_out", sv["yf3"])
        sv["s2"], h, h_bf = _out_ln(sv["yf3"], lw["f_out"], h, wts["ln2_g"][layer], wts["ln2_b"][layer],
                                    n_valid=n_tok, tm=tmb // 2, name="ffn_out_ln")
        saved.append(sv)

    loss_tile, dh = _loss_bwd(h, tgt, lo=n_meta, hi=n_tok, tm=tm, name="loss")

    grads = {k: [None] * N_LAYERS for k in ("f_cwb3", "ln1_gb", "ln2_gb")}
    grads.update({k: [None] * N_A_LAYERS for k in ("a_cwb", "a_bd_r", "a_bd_i", "a_vecs")})
    dkv = []
    token = jnp.zeros((), F32)
    for layer in reversed(range(N_LAYERS)):
        sv = saved[layer]
        lw = sv["w"]
        big = {}
        ds, ds_bf, grads["ln2_gb"][layer] = _ln_bwd(dh, sv["s2"], wts["ln2_g"][layer] + token, tm=tmb // 2,
                                                    name="ln_bwd")
        dz, dcw = _ffn_bwd_mid(ds_bf, lw["f_out"], sv["z3"], wts["f_cwb3"][layer], name="f_bwd_mid")
        grads["f_cwb3"][layer] = dcw.reshape((N_DEV,) + dcw.shape[2:])
        dz3 = dz
        big["f_out"] = _w_out_grad(sv["yf3"], ds_bf, lw["f_out"].shape[1], name="f_w_out_grad")
        dh = _in_bwd(dz3, lw["f_in"], ds, tm=tmb, transposed=True, name="f_in_bwd")
        big["f_in"] = _w_in_grad(sv["h1_bf"], dz3, transposed=True, name="f_w_in_grad")
        token = hooks.grads_ready(layer, "ffn", big)
        big = {}
        ds, ds_bf, grads["ln1_gb"][layer] = _ln_bwd(dh, sv["s1"], wts["ln1_g"][layer] + token, tm=tmb // 2,
                                                    name="ln_bwd")
        if layer < N_A_LAYERS:
            dy = _out_bwd(ds_bf, lw["out"], tm=tmb // 2, name="a_out_bwd")
            big["out"] = _w_out_grad(sv["y3"], ds_bf, lw["out"].shape[1], name="a_w_out_grad")
            d_h, d_a, dgate = _scan_bwd(dy, sv["gr"], sv["hr"], sv["a"], cb=sb, name="a_scan_bwd")
            d_rec, dpr, dpi, grads["a_vecs"][layer] = _gates_bwd(
                sv["rec"], sv["r"], sv["i"], sv["a"], d_h, d_a, wts["a_bd_r"][layer], wts["a_bd_i"][layer],
                wts["a_vecs"][layer], tm=tmb // 2, name="a_gates_bwd")
            grads["a_bd_r"][layer], grads["a_bd_i"][layer] = _bd_grad(sv["rec"], dpr, dpi, cb=cb, name="a_bd_grad")
            dact, grads["a_cwb"][layer] = _conv_a_bwd(d_rec, sv["gr"], dgate, wts["a_cwb"][layer], cb=cb,
                                                      name="a_conv_bwd")
            dh = _in_bwd(dact, lw["in"], ds, tm=tmb, name="a_in_bwd")
            big["in"] = _w_in_grad(sv["h_bf"], dact, name="a_w_in_grad")
        else:
            j = layer - N_A_LAYERS
            dy = _out_bwd(ds_bf, lw["out"], tm=tmb // 2, name="b_out_bwd")
            big["out"] = _w_out_grad(sv["y3"], ds_bf, lw["out"].shape[1], name="b_w_out_grad")
            dqg, dk, dv, dc, dcq = _attn_bwd(dy, sv["qg"], sv["o"], sv["st"], kvs["kv"], kvs["ct"], tq=tqa_bwd,
                                             name="attn_bwd")
            dkv.append((dk, dv, dc, dcq))
            dh = _in_bwd(dqg, lw["in"], ds, tm=tmb, name="b_in_bwd")
            big["in"] = _w_in_grad(sv["h_bf"], dqg, name="b_w_in_grad")
            if j == 0:
                hpb = _head_block_width(d // N_HEADS, BWD_HEAD_TILES) // (d // N_HEADS)
                dct = (dkv[0][2] + dkv[1][2])[:, :hpb, :].reshape(N_HEADS, tp)
                dcq = (dkv[0][3] + dkv[1][3])[:, :, :hpb]
                dct = dct + jnp.transpose(dcq, (0, 2, 1)).reshape(N_HEADS, tp)
                dct = jnp.concatenate([dct, jnp.zeros((LANES - N_HEADS, tp), F32)])
                df_bf, grads["kv_fb"] = _fgate_bwd(dct, kvs["fp"], wts["kv_fb"], tq=tq, name="fgate_bwd")
                dkvz = jnp.concatenate([_pair_sum(dkv[0][0], dkv[1][0], tm=tm, name="kv_pair_sum"),
                                        _pair_sum(dkv[0][1], dkv[1][1], tm=tm, name="kv_pair_sum"), df_bf], axis=1)
                dh = _mm_nt_full(dkvz, kvs["w"], dh, tm=tmb // 2, name="kv_in_bwd")
                big["kv_w"] = _mm_tn_cols(kvs["h_bf"], dkvz, tn=LANES, name="kv_w_grad")
        token = hooks.grads_ready(layer, "mix", big)
    return loss_tile, dh, grads


def _finish_small_grads(grads, d_h0, n_meta):
    r_dim = grads["a_vecs"][0].shape[1]
    per = _lru_block_cols(r_dim) // (r_dim // N_LRU_BLOCKS)
    a_cwb = jnp.stack(grads["a_cwb"])
    a_vecs = jnp.stack(grads["a_vecs"])
    f_cwb3 = jnp.stack(grads["f_cwb3"])
    ln1 = jnp.stack(grads["ln1_gb"])
    ln2 = jnp.stack(grads["ln2_gb"])
    f_rows = f_cwb3.transpose(0, 2, 1, 3).reshape(N_LAYERS, 8, -1)
    return {
        "meta": d_h0[:n_meta],
        "a_conv_w": a_cwb[:, :4], "a_conv_b": a_cwb[:, 4],
        "a_w_r": jnp.stack([_block_diag_extract(g, per) for g in grads["a_bd_r"]]),
        "a_b_r": a_vecs[:, 0],
        "a_w_i": jnp.stack([_block_diag_extract(g, per) for g in grads["a_bd_i"]]),
        "a_b_i": a_vecs[:, 1], "a_lambda": a_vecs[:, 2],
        "kv_f_b": grads["kv_fb"][0, :N_HEADS],
        "f_conv_w": f_rows[:, :3], "f_conv_b": f_rows[:, 3],
        "ln1_g": ln1[:, 0], "ln1_b": ln1[:, 1], "ln2_g": ln2[:, 0], "ln2_b": ln2[:, 1],
    }


def kernel(x, meta, a_w_in, a_conv_w, a_conv_b, a_w_r, a_b_r, a_w_i, a_b_i, a_lambda, a_w_out, kv_w, kv_f_b, b_w_in, b_w_out, f_w_in, f_conv_w, f_conv_b, f_w_out, ln1_g, ln1_b, ln2_g, ln2_b, loss_target, m_meta, m_a_w_in, m_a_conv_w, m_a_conv_b, m_a_w_r, m_a_b_r, m_a_w_i, m_a_b_i, m_a_lambda, m_a_w_out, m_kv_w, m_kv_f_b, m_b_w_in, m_b_w_out, m_f_w_in, m_f_conv_w, m_f_conv_b, m_f_w_out, m_ln1_g, m_ln1_b, m_ln2_g, m_ln2_b, v_meta, v_a_w_in, v_a_conv_w, v_a_conv_b, v_a_w_r, v_a_b_r, v_a_w_i, v_a_b_i, v_a_lambda, v_a_w_out, v_kv_w, v_kv_f_b, v_b_w_in, v_b_w_out, v_f_w_in, v_f_conv_w, v_f_conv_b, v_f_w_out, v_ln1_g, v_ln1_b, v_ln2_g, v_ln2_b):
    w = dict(meta=meta, a_w_in=a_w_in, a_conv_w=a_conv_w, a_conv_b=a_conv_b, a_w_r=a_w_r, a_b_r=a_b_r, a_w_i=a_w_i,
             a_b_i=a_b_i, a_lambda=a_lambda, a_w_out=a_w_out, kv_w=kv_w, kv_f_b=kv_f_b, b_w_in=b_w_in,
             b_w_out=b_w_out, f_w_in=f_w_in, f_conv_w=f_conv_w, f_conv_b=f_conv_b, f_w_out=f_w_out, ln1_g=ln1_g,
             ln1_b=ln1_b, ln2_g=ln2_g, ln2_b=ln2_b)
    m = dict(meta=m_meta, a_w_in=m_a_w_in, a_conv_w=m_a_conv_w, a_conv_b=m_a_conv_b, a_w_r=m_a_w_r, a_b_r=m_a_b_r,
             a_w_i=m_a_w_i, a_b_i=m_a_b_i, a_lambda=m_a_lambda, a_w_out=m_a_w_out, kv_w=m_kv_w, kv_f_b=m_kv_f_b,
             b_w_in=m_b_w_in, b_w_out=m_b_w_out, f_w_in=m_f_w_in, f_conv_w=m_f_conv_w, f_conv_b=m_f_conv_b,
             f_w_out=m_f_w_out, ln1_g=m_ln1_g, ln1_b=m_ln1_b, ln2_g=m_ln2_g, ln2_b=m_ln2_b)
    v = dict(meta=v_meta, a_w_in=v_a_w_in, a_conv_w=v_a_conv_w, a_conv_b=v_a_conv_b, a_w_r=v_a_w_r, a_b_r=v_a_b_r,
             a_w_i=v_a_w_i, a_b_i=v_a_b_i, a_lambda=v_a_lambda, a_w_out=v_a_w_out, kv_w=v_kv_w, kv_f_b=v_kv_f_b,
             b_w_in=v_b_w_in, b_w_out=v_b_w_out, f_w_in=v_f_w_in, f_conv_w=v_f_conv_w, f_conv_b=v_f_conv_b,
             f_w_out=v_f_w_out, ln1_g=v_ln1_g, ln1_b=v_ln1_b, ln2_g=v_ln2_g, ln2_b=v_ln2_b)
    shapes = {n: w[n].shape for n in WEIGHT_NAMES}

    me = jnp.reshape(_my_index(), (1,)).astype(jnp.int32)

    def as_stored(name, a):
        return jnp.swapaxes(a, 1, 2) if name == "f_w_in" else a

    param_of = {"in": ("a_w_in", "b_w_in"), "out": ("a_w_out", "b_w_out"), "f_in": ("f_w_in",) * 2,
                "f_out": ("f_w_out",) * 2}
    order = [("small", None, None)]
    for layer in range(N_LAYERS):
        if layer == N_A_LAYERS:
            order.append(("kv_w", layer, 0))
        for key in ("in", "out", "f_in", "f_out"):
            order.append((key, layer, layer if key[0] == "f" or layer < N_A_LAYERS else layer - N_A_LAYERS))
    def place(key, layer, idx):
        if key == "small":
            return _place_own(_pack([w[n] for n in COL_F32], F32)[None], 0, me, out_dtype=F32, name="place_small")
        if key == "kv_w":
            return _place_own(w["kv_w"][None], 0, me, out_dtype=BF16, name="place_kv_w")
        name = param_of[key][0 if layer < N_A_LAYERS else 1]
        return _place_own(as_stored(name, w[name]), idx, me, out_dtype=BF16, name=f"place_{name}_{idx}")

    lands = [place(*o) for o in order]
    gather_handles, gather_token = _split_start([([l], [l]) for l in lands], scatter=False, name="gather_start")
    group_of = {(key, layer): g for g, (key, layer, _) in enumerate(order)}
    (got_s,) = _split_wait(gather_handles[0], gather_token, scatter=False, name="gather_wait_small")
    small = {n: w[n] for n in REPLICATED}
    for n, part in zip(COL_F32, _unpack(got_s, [w[n].shape for n in COL_F32], 1)):
        small[n] = _cols_full(part)
    n_meta, d = small["meta"].shape

    class Hooks:
        pending = None
        received = {}
        sent = {}

        @staticmethod
        def weight(layer, key, after):
            (got,) = _split_wait(gather_handles[group_of[(key, layer)]], after, scatter=False,
                                 name=f"gather_wait_{key}_{layer}")
            return got

        @staticmethod
        def collect(after):
            if Hooks.pending is not None:
                tag, names, handle = Hooks.pending
                got = _split_wait(handle, after, scatter=True, name=f"scatter_wait_{tag}")
                Hooks.received.update(zip(names, got))
                Hooks.pending = None

        @staticmethod
        def grads_ready(layer, part, big):
            if "kv_w" in big:
                big["kv_w"] = _cols_split(big["kv_w"][:, :shapes["kv_w"][1] * N_DEV]).astype(BF16)
            names = [(key, layer) for key in big]
            send = [big[key] for key in big]
            Hooks.collect(send[0])
            empty = [lax.empty(s.shape, s.dtype) for s in send]
            handles, token = _split_start([(send, empty)], scatter=True, name=f"scatter_start_{part}_{layer}")
            Hooks.pending = (f"{part}_{layer}", names, handles[0])
            Hooks.sent.update(zip(names, handles[0][2]))
            return token[0, 0]

    Hooks.pending, Hooks.received, Hooks.sent = None, {}, {}

    n_tok = n_meta + x.shape[1]
    tp = -(-n_tok // ROW_ALIGN) * ROW_ALIGN
    pad = jnp.zeros((tp - n_tok, d), F32)
    h0 = jnp.concatenate([small["meta"], x[0], pad])
    tgt = jnp.concatenate([jnp.zeros((n_meta, d), F32), loss_target[0], pad])
    loss_tile, d_h0, grads = _local_step(h0, tgt, n_meta, n_tok, _small_layouts(small), Hooks)
    g_small = _finish_small_grads(grads, d_h0, n_meta)
    loss = lax.psum(loss_tile[0, 0], MESH_AXES)
    grad_x = d_h0[n_meta:n_tok][None]

    rep = _pack([g_small[n] for n in REPLICATED], F32, row_quantum=16 * N_DEV)
    send = [_pack8([_cols_split(g_small[n]) for n in COL_F32], F32), rep.reshape(N_DEV, -1, PACK_COLS)]
    lands = _own_blocks(send, name="scatter_own_small")
    handles, token = _split_start([(send, lands)], scatter=True, name="scatter_start_small")

    g, delta, new_m, new_v = {}, {}, {}, {}
    layers_of = {
        "a_w_in": [("in", l) for l in range(N_A_LAYERS)], "a_w_out": [("out", l) for l in range(N_A_LAYERS)],
        "b_w_in": [("in", l) for l in range(N_A_LAYERS, N_LAYERS)],
        "b_w_out": [("out", l) for l in range(N_A_LAYERS, N_LAYERS)],
        "f_w_in": [("f_in", l) for l in range(N_LAYERS)], "f_w_out": [("f_out", l) for l in range(N_LAYERS)],
        "kv_w": [("kv_w", N_A_LAYERS)],
    }
    ready = [n for n in BIG + ("kv_w",) if all(t in Hooks.received for t in layers_of[n])]

    def done(names):
        return jnp.stack([g[n][(0,) * g[n].ndim] for n in names])

    for n in ready + [n for n in BIG + ("kv_w",) if n not in ready]:
        if n not in ready and Hooks.pending is not None:
            Hooks.collect(done(ready))
        lift = (lambda a: a[None]) if n == "kv_w" else (lambda a, n=n: as_stored(n, a))
        outs = _sum_adamw([Hooks.received[t] for t in layers_of[n]], [Hooks.sent[t] for t in layers_of[n]], me,
                          lift(w[n]), lift(m[n]), lift(v[n]), name="sum_adamw_" + n)
        g[n], delta[n], new_m[n], new_v[n] = [as_stored(n, o).reshape(shapes[n]) for o in outs]
    recv_s, recv_r = _split_wait(handles[0], done(BIG + ("kv_w",)), scatter=True, name="scatter_wait_small")
    sum_s = _sum8(recv_s, name="sum_grads_f32")
    g.update(zip(COL_F32, _unpack(sum_s, [shapes[n] for n in COL_F32], 0)))
    (got_r,) = _all_gather([_sum8(recv_r, name="sum_grads_replicated")], name="gather_replicated_sums")
    g.update(zip(REPLICATED, _unpack(got_r.reshape(-1, PACK_COLS), [shapes[n] for n in REPLICATED], 0)))

    for n in COL_F32 + REPLICATED:
        shp = shapes[n]
        dl, nm, nv = _adamw(_slab2d(w[n]), _slab2d(g[n]), _slab2d(m[n]), _slab2d(v[n]), name="adamw")
        delta[n], new_m[n], new_v[n] = dl.reshape(shp), nm.reshape(shp), nv.reshape(shp)
    return (loss, grad_x, *[g[n] for n in WEIGHT_NAMES], *[delta[n] for n in WEIGHT_NAMES],
            *[new_m[n] for n in WEIGHT_NAMES], *[new_v[n] for n in WEIGHT_NAMES])
```
